```python
import math
import jax
import jax.numpy as jnp
from jax import lax
import numpy as np

D_MODEL = 1024
BATCH = 8
SEQ = 16384
DEPTH = 2

MEM_LEN = 256
EPS = 1e-6
MASK_VALUE = -1e30
TINY = 1e-30

A_HEADS = 8
A_NOPE = 64
A_ROPE = 32
A_V = 64
A_Q_RANK = 384
A_KV_RANK = 256
A_QBLOCK = 128
ROPE_THETA = 10000.0

B_HEADS = 8
B_DK = 128
B_DV = 64
B_CHUNK = 16

C_HEADS = 8
C_KV_HEADS = 2
C_DH = 64
C_WINDOW = 128
C_BLOCK = 128

REL_BUCKETS = 32
REL_MAX_DIST = 128

X_HEADS = 4
X_DH = 256

D_FF = -(-(8 * D_MODEL) // (3 * 256)) * 256

IN_SPLITS = (
    A_Q_RANK, A_KV_RANK, A_ROPE,
    B_HEADS * B_DK, B_HEADS * B_DK, B_HEADS * B_DK,
    B_HEADS * B_DV, B_HEADS * B_DV,
    C_HEADS * C_DH, C_KV_HEADS * C_DH, C_KV_HEADS * C_DH,
    D_MODEL, D_MODEL, D_MODEL,
)
IN_WIDTH = sum(IN_SPLITS)

kernel_name = 'hybrid_mla_hgrn2_swa_encoder'


def _rmsnorm(x, g):
    x32 = x.astype(jnp.float32)
    y = x32 * lax.rsqrt(jnp.mean(x32 * x32, axis=-1, keepdims=True) + EPS)
    return (y * g.astype(jnp.float32)).astype(x.dtype)


def _split_cols(z, sizes):
    out, start = [], 0
    for s in sizes:
        out.append(z[..., start:start + s])
        start += s
    return out


def _rope(x, pos):
    half = x.shape[-1] // 2
    inv = ROPE_THETA ** (-jnp.arange(half, dtype=jnp.float32) / half)
    ang = pos.astype(jnp.float32)[:, None] * inv[None, :]
    cos = jnp.cos(ang)[None, :, None, :]
    sin = jnp.sin(ang)[None, :, None, :]
    x32 = x.astype(jnp.float32)
    x1, x2 = x32[..., :half], x32[..., half:]
    return jnp.concatenate([x1 * cos - x2 * sin, x1 * sin + x2 * cos], axis=-1).astype(x.dtype)


def _t5_bucket(rel):
    nb = REL_BUCKETS // 2
    max_exact = nb // 2
    ret = (rel > 0).astype(jnp.int32) * nb
    n = jnp.abs(rel)
    large = max_exact + (jnp.log(jnp.maximum(n, 1).astype(jnp.float32) / max_exact)
                         / math.log(REL_MAX_DIST / max_exact) * (nb - max_exact)).astype(jnp.int32)
    large = jnp.minimum(large, nb - 1)
    return ret + jnp.where(n < max_exact, n, large)


def _mla(cq, ckv, kr, gq, gkv, wuq, wukv, pos):
    Bsz, S, _ = cq.shape
    q = (_rmsnorm(cq, gq) @ wuq).reshape(Bsz, S, A_HEADS, A_NOPE + A_ROPE)
    q = jnp.concatenate([q[..., :A_NOPE], _rope(q[..., A_NOPE:], pos)], axis=-1)
    kv = (_rmsnorm(ckv, gkv) @ wukv).reshape(Bsz, S, A_HEADS, A_NOPE + A_V)
    k_rope = jnp.broadcast_to(_rope(kr[:, :, None, :], pos), (Bsz, S, A_HEADS, A_ROPE))
    k = jnp.concatenate([kv[..., :A_NOPE], k_rope], axis=-1)
    v = kv[..., A_NOPE:]
    scale = (A_NOPE + A_ROPE) ** -0.5
    nb = S // A_QBLOCK
    qb = jnp.moveaxis(q.reshape(Bsz, nb, A_QBLOCK, A_HEADS, A_NOPE + A_ROPE), 1, 0)

    def attend(q_blk):
        s = jnp.einsum('bqhd,bkhd->bhqk', q_blk, k).astype(jnp.float32) * scale
        p = jax.nn.softmax(s, axis=-1).astype(v.dtype)
        return jnp.einsum('bhqk,bkhd->bqhd', p, v)

    o = lax.map(attend, qb)
    return jnp.moveaxis(o, 0, 1).reshape(Bsz, S, A_HEADS * A_V)


def _gated_scan(q, k, v, log_f):
    Bsz, S, H, DK = q.shape
    DV = v.shape[-1]
    nc = S // B_CHUNK
    q, k, log_f = [t.reshape(Bsz, nc, B_CHUNK, H, DK) for t in (q, k, log_f)]
    v = v.reshape(Bsz, nc, B_CHUNK, H, DV)
    b = jnp.cumsum(log_f, axis=2)
    b_last = b[:, :, -1:]
    q_dec = q * jnp.exp(b)
    k_inv = k * jnp.exp(-b)
    k_end = k * jnp.exp(b_last - b)
    scores = jnp.einsum('bnthk,bnshk->bnhts', q_dec, k_inv)
    tri = jnp.tril(jnp.ones((B_CHUNK, B_CHUNK), dtype=bool))
    scores = jnp.where(tri, scores, 0.0)
    o_intra = jnp.einsum('bnhts,bnshv->bnthv', scores, v)

    def step(state, inp):
        q_c, k_c, v_c, dec_c = inp
        o_c = jnp.einsum('bthk,bhkv->bthv', q_c, state)
        state = state * dec_c[:, 0, :, :, None] + jnp.einsum('bshk,bshv->bhkv', k_c, v_c)
        return state, o_c

    xs = tuple(jnp.moveaxis(t, 1, 0) for t in (q_dec, k_end, v, jnp.exp(b_last)))
    s0 = jnp.zeros((Bsz, H, DK, DV), jnp.float32)
    _, o_inter = lax.scan(step, s0, xs)
    o = o_intra + jnp.moveaxis(o_inter, 0, 1)
    return o.reshape(Bsz, S, H, DV)


def _hgrn2(q, f_fwd, f_bwd, i, g, lb_fwd, lb_bwd, g_out):
    Bsz, S, _ = q.shape
    dt = q.dtype

    def heads(t, d):
        return t.astype(jnp.float32).reshape(Bsz, S, B_HEADS, d)

    def gates(z, lb):
        lb = lb.astype(jnp.float32).reshape(B_HEADS, B_DK)
        zh = heads(z, B_DK)
        f = lb + (1.0 - lb) * jax.nn.sigmoid(zh)
        log_f = jnp.log(jnp.maximum(f, TINY))
        key = (1.0 - lb) * jax.nn.sigmoid(-zh)
        return log_f, key

    qh = heads(q, B_DK)
    vh = heads(i, B_DV)
    lf_f, k_f = gates(f_fwd, lb_fwd)
    lf_b, k_b = gates(f_bwd, lb_bwd)
    o_f = _gated_scan(qh, k_f, vh, lf_f)
    flip = lambda t: jnp.flip(t, axis=1)
    o_b = flip(_gated_scan(flip(qh), flip(k_b), flip(vh), flip(lf_b)))
    o = _rmsnorm(o_f + o_b, g_out) * jax.nn.silu(heads(g, B_DV))
    return o.reshape(Bsz, S, B_HEADS * B_DV).astype(dt)


def _window_gqa(q, k, v, rel_bias, sink):
    Bsz, S, _ = q.shape
    nb = S // C_BLOCK
    G = C_HEADS // C_KV_HEADS
    span = 3 * C_BLOCK
    q = q.reshape(Bsz, nb, C_BLOCK, C_KV_HEADS, G, C_DH)

    def band(t):
        t = t.reshape(Bsz, S, C_KV_HEADS, C_DH)
        t = jnp.pad(t, ((0, 0), (C_BLOCK, C_BLOCK), (0, 0), (0, 0)))
        t = t.reshape(Bsz, nb + 2, C_BLOCK, C_KV_HEADS, C_DH)
        return jnp.concatenate([t[:, :-2], t[:, 1:-1], t[:, 2:]], axis=2)

    kb, vb = band(k), band(v)
    rel = jnp.arange(span)[None, :] - C_BLOCK - jnp.arange(C_BLOCK)[:, None]
    bias = rel_bias.astype(jnp.float32)[_t5_bucket(rel)]
    bias = jnp.transpose(bias, (2, 0, 1)).reshape(C_KV_HEADS, G, C_BLOCK, span)
    key_pos = (jnp.arange(nb)[:, None] - 1) * C_BLOCK + jnp.arange(span)[None, :]
    valid = (jnp.abs(rel) <= C_WINDOW)[None] & ((key_pos >= 0) & (key_pos < S))[:, None, :]
    s = jnp.einsum('bnqkgd,bnskd->bnkgqs', q, kb).astype(jnp.float32) * (C_DH ** -0.5) + bias
    s = jnp.where(valid[None, :, None, None], s, MASK_VALUE)
    sink_l = sink.astype(jnp.float32).reshape(C_KV_HEADS, G)[:, :, None, None]
    m = jnp.maximum(jnp.max(s, axis=-1, keepdims=True), sink_l)
    p = jnp.exp(s - m)
    p = p / (jnp.sum(p, axis=-1, keepdims=True) + jnp.exp(sink_l - m))
    o = jnp.einsum('bnkgqs,bnskd->bnqkgd', p.astype(v.dtype), vb)
    return o.reshape(Bsz, S, C_HEADS * C_DH)


def _cross(h, mem_n, wq, wkv, wo):
    Bsz, S, _ = h.shape
    q = (h @ wq).reshape(Bsz, S, X_HEADS, X_DH)
    kv = (mem_n @ wkv).reshape(Bsz, mem_n.shape[1], 2, X_HEADS, X_DH)
    s = jnp.einsum('bqhd,bkhd->bhqk', q, kv[:, :, 0]).astype(jnp.float32) * (X_DH ** -0.5)
    p = jax.nn.softmax(s, axis=-1).astype(h.dtype)
    o = jnp.einsum('bhqk,bkhd->bqhd', p, kv[:, :, 1]).reshape(Bsz, S, X_HEADS * X_DH)
    return o @ wo


def _swiglu(h, w1, w3, w2):
    return (jax.nn.silu(h @ w1) * (h @ w3)) @ w2


def _fwd_setup_inputs(seed: int = 0) -> dict:
    key = jax.random.key(seed)
    ks = iter(jax.random.split(key, 32))
    f32 = jnp.float32

    def nrm(shape, fan_in):
        return jax.random.normal(next(ks), shape, f32) * (fan_in ** -0.5)

    def gain(shape):
        return 1.0 + 0.02 * jax.random.normal(next(ks), shape, f32)

    L, D = DEPTH, D_MODEL
    return {
        'x': jax.random.normal(next(ks), (BATCH, SEQ, D), f32),
        'mem': jax.random.normal(next(ks), (BATCH, MEM_LEN, D), f32),
        'w_in': nrm((L, D, IN_WIDTH), D),
        'g_mix': gain((L, D)),
        'a_gq': gain((L, A_Q_RANK)),
        'a_gkv': gain((L, A_KV_RANK)),
        'a_wuq': nrm((L, A_Q_RANK, A_HEADS * (A_NOPE + A_ROPE)), A_Q_RANK),
        'a_wukv': nrm((L, A_KV_RANK, A_HEADS * (A_NOPE + A_V)), A_KV_RANK),
        'b_lb': jax.random.normal(next(ks), (2, L, B_HEADS * B_DK), f32),
        'b_gout': gain((L, B_DV)),
        'c_sink': 0.5 * jax.random.normal(next(ks), (L, C_HEADS), f32),
        'rel_bias': 0.5 * jax.random.normal(next(ks), (REL_BUCKETS, C_HEADS), f32),
        'w_br_a': nrm((L, A_HEADS * A_V, D), A_HEADS * A_V),
        'w_br_b': nrm((L, B_HEADS * B_DV, D), B_HEADS * B_DV),
        'w_br_c': nrm((L, C_HEADS * C_DH, D), C_HEADS * C_DH),
        'w_out': nrm((L, D, D), D),
        'g_x': gain((L, D)),
        'g_mem': gain((L, D)),
        'x_wq': nrm((L, D, X_HEADS * X_DH), D),
        'x_wkv': nrm((L, D, 2 * X_HEADS * X_DH), D),
        'x_wo': nrm((L, X_HEADS * X_DH, D), X_HEADS * X_DH),
        'g_ffn': gain((L, D)),
        'f_w1': nrm((L, D, D_FF), D),
        'f_w3': nrm((L, D, D_FF), D),
        'f_w2': nrm((L, D_FF, D), D_FF),
        'g_final': gain((D,)),
    }


def _fwd_reference(x, mem, w_in, g_mix, a_gq, a_gkv, a_wuq, a_wukv, b_lb, b_gout, c_sink, rel_bias,
              w_br_a, w_br_b, w_br_c, w_out, g_x, g_mem, x_wq, x_wkv, x_wo, g_ffn,
              f_w1, f_w3, f_w2, g_final):
    S = x.shape[1]
    pos = jnp.arange(S, dtype=jnp.int32)
    sm = jax.nn.softmax(b_lb.astype(jnp.float32), axis=1)
    lower_bounds = jnp.cumsum(sm, axis=1) - sm[:, :1]
    for l in range(DEPTH):
        h = _rmsnorm(x, g_mix[l])
        (a_cq, a_ckv, a_kr, b_q, b_ff, b_fb, b_i, b_g,
         c_q, c_k, c_v, gate_a, gate_b, gate_c) = _split_cols(h @ w_in[l], IN_SPLITS)
        y_a = _mla(a_cq, a_ckv, a_kr, a_gq[l], a_gkv[l], a_wuq[l], a_wukv[l], pos)
        y_b = _hgrn2(b_q, b_ff, b_fb, b_i, b_g, lower_bounds[0, l], lower_bounds[1, l], b_gout[l])
        y_c = _window_gqa(c_q, c_k, c_v, rel_bias, c_sink[l])
        merged = (jax.nn.sigmoid(gate_a) * (y_a @ w_br_a[l])
                  + jax.nn.sigmoid(gate_b) * (y_b @ w_br_b[l])
                  + jax.nn.sigmoid(gate_c) * (y_c @ w_br_c[l]))
        x = x + merged @ w_out[l]
        h = _rmsnorm(x, g_x[l])
        x = x + _cross(h, _rmsnorm(mem, g_mem[l]), x_wq[l], x_wkv[l], x_wo[l])
        h = _rmsnorm(x, g_ffn[l])
        x = x + _swiglu(h, f_w1[l], f_w3[l], f_w2[l])
    return _rmsnorm(x, g_final)


import jax as _jax
import jax.numpy as _jnp

TWIN_FORMAT = 'train_step'
FWD_PARAMS = ['x', 'mem', 'w_in', 'g_mix', 'a_gq', 'a_gkv', 'a_wuq', 'a_wukv', 'b_lb', 'b_gout', 'c_sink', 'rel_bias', 'w_br_a', 'w_br_b', 'w_br_c', 'w_out', 'g_x', 'g_mem', 'x_wq', 'x_wkv', 'x_wo', 'g_ffn', 'f_w1', 'f_w3', 'f_w2', 'g_final']
TWIN_WEIGHTS = ['w_in', 'g_mix', 'a_gq', 'a_gkv', 'a_wuq', 'a_wukv', 'b_lb', 'b_gout', 'c_sink', 'rel_bias', 'w_br_a', 'w_br_b', 'w_br_c', 'w_out', 'g_x', 'g_mem', 'x_wq', 'x_wkv', 'x_wo', 'g_ffn', 'f_w1', 'f_w3', 'f_w2', 'g_final']
TWIN_DIFF_INPUT = 'x'
TWIN_INPUTS = ['x', 'mem', 'w_in', 'g_mix', 'a_gq', 'a_gkv', 'a_wuq', 'a_wukv', 'b_lb', 'b_gout', 'c_sink', 'rel_bias', 'w_br_a', 'w_br_b', 'w_br_c', 'w_out', 'g_x', 'g_mem', 'x_wq', 'x_wkv', 'x_wo', 'g_ffn', 'f_w1', 'f_w3', 'f_w2', 'g_final', 'loss_target', 'm_w_in', 'm_g_mix', 'm_a_gq', 'm_a_gkv', 'm_a_wuq', 'm_a_wukv', 'm_b_lb', 'm_b_gout', 'm_c_sink', 'm_rel_bias', 'm_w_br_a', 'm_w_br_b', 'm_w_br_c', 'm_w_out', 'm_g_x', 'm_g_mem', 'm_x_wq', 'm_x_wkv', 'm_x_wo', 'm_g_ffn', 'm_f_w1', 'm_f_w3', 'm_f_w2', 'm_g_final', 'v_w_in', 'v_g_mix', 'v_a_gq', 'v_a_gkv', 'v_a_wuq', 'v_a_wukv', 'v_b_lb', 'v_b_gout', 'v_c_sink', 'v_rel_bias', 'v_w_br_a', 'v_w_br_b', 'v_w_br_c', 'v_w_out', 'v_g_x', 'v_g_mem', 'v_x_wq', 'v_x_wkv', 'v_x_wo', 'v_g_ffn', 'v_f_w1', 'v_f_w3', 'v_f_w2', 'v_g_final']
TWIN_OUTPUTS = ['loss', 'grad_x', 'grad_w_in', 'grad_g_mix', 'grad_a_gq', 'grad_a_gkv', 'grad_a_wuq', 'grad_a_wukv', 'grad_b_lb', 'grad_b_gout', 'grad_c_sink', 'grad_rel_bias', 'grad_w_br_a', 'grad_w_br_b', 'grad_w_br_c', 'grad_w_out', 'grad_g_x', 'grad_g_mem', 'grad_x_wq', 'grad_x_wkv', 'grad_x_wo', 'grad_g_ffn', 'grad_f_w1', 'grad_f_w3', 'grad_f_w2', 'grad_g_final', 'delta_w_in', 'delta_g_mix', 'delta_a_gq', 'delta_a_gkv', 'delta_a_wuq', 'delta_a_wukv', 'delta_b_lb', 'delta_b_gout', 'delta_c_sink', 'delta_rel_bias', 'delta_w_br_a', 'delta_w_br_b', 'delta_w_br_c', 'delta_w_out', 'delta_g_x', 'delta_g_mem', 'delta_x_wq', 'delta_x_wkv', 'delta_x_wo', 'delta_g_ffn', 'delta_f_w1', 'delta_f_w3', 'delta_f_w2', 'delta_g_final', 'new_m_w_in', 'new_m_g_mix', 'new_m_a_gq', 'new_m_a_gkv', 'new_m_a_wuq', 'new_m_a_wukv', 'new_m_b_lb', 'new_m_b_gout', 'new_m_c_sink', 'new_m_rel_bias', 'new_m_w_br_a', 'new_m_w_br_b', 'new_m_w_br_c', 'new_m_w_out', 'new_m_g_x', 'new_m_g_mem', 'new_m_x_wq', 'new_m_x_wkv', 'new_m_x_wo', 'new_m_g_ffn', 'new_m_f_w1', 'new_m_f_w3', 'new_m_f_w2', 'new_m_g_final', 'new_v_w_in', 'new_v_g_mix', 'new_v_a_gq', 'new_v_a_gkv', 'new_v_a_wuq', 'new_v_a_wukv', 'new_v_b_lb', 'new_v_b_gout', 'new_v_c_sink', 'new_v_rel_bias', 'new_v_w_br_a', 'new_v_w_br_b', 'new_v_w_br_c', 'new_v_w_out', 'new_v_g_x', 'new_v_g_mem', 'new_v_x_wq', 'new_v_x_wkv', 'new_v_x_wo', 'new_v_g_ffn', 'new_v_f_w1', 'new_v_f_w3', 'new_v_f_w2', 'new_v_g_final']
TWIN_LEAF_KINDS = {'loss': 'loss', 'grad_x': 'grad_x', 'grad_w_in': 'grad_w', 'grad_g_mix': 'grad_w', 'grad_a_gq': 'grad_w', 'grad_a_gkv': 'grad_w', 'grad_a_wuq': 'grad_w', 'grad_a_wukv': 'grad_w', 'grad_b_lb': 'grad_w', 'grad_b_gout': 'grad_w', 'grad_c_sink': 'grad_w', 'grad_rel_bias': 'grad_w', 'grad_w_br_a': 'grad_w', 'grad_w_br_b': 'grad_w', 'grad_w_br_c': 'grad_w', 'grad_w_out': 'grad_w', 'grad_g_x': 'grad_w', 'grad_g_mem': 'grad_w', 'grad_x_wq': 'grad_w', 'grad_x_wkv': 'grad_w', 'grad_x_wo': 'grad_w', 'grad_g_ffn': 'grad_w', 'grad_f_w1': 'grad_w', 'grad_f_w3': 'grad_w', 'grad_f_w2': 'grad_w', 'grad_g_final': 'grad_w', 'delta_w_in': 'delta_w', 'delta_g_mix': 'delta_w', 'delta_a_gq': 'delta_w', 'delta_a_gkv': 'delta_w', 'delta_a_wuq': 'delta_w', 'delta_a_wukv': 'delta_w', 'delta_b_lb': 'delta_w', 'delta_b_gout': 'delta_w', 'delta_c_sink': 'delta_w', 'delta_rel_bias': 'delta_w', 'delta_w_br_a': 'delta_w', 'delta_w_br_b': 'delta_w', 'delta_w_br_c': 'delta_w', 'delta_w_out': 'delta_w', 'delta_g_x': 'delta_w', 'delta_g_mem': 'delta_w', 'delta_x_wq': 'delta_w', 'delta_x_wkv': 'delta_w', 'delta_x_wo': 'delta_w', 'delta_g_ffn': 'delta_w', 'delta_f_w1': 'delta_w', 'delta_f_w3': 'delta_w', 'delta_f_w2': 'delta_w', 'delta_g_final': 'delta_w', 'new_m_w_in': 'new_m', 'new_m_g_mix': 'new_m', 'new_m_a_gq': 'new_m', 'new_m_a_gkv': 'new_m', 'new_m_a_wuq': 'new_m', 'new_m_a_wukv': 'new_m', 'new_m_b_lb': 'new_m', 'new_m_b_gout': 'new_m', 'new_m_c_sink': 'new_m', 'new_m_rel_bias': 'new_m', 'new_m_w_br_a': 'new_m', 'new_m_w_br_b': 'new_m', 'new_m_w_br_c': 'new_m', 'new_m_w_out': 'new_m', 'new_m_g_x': 'new_m', 'new_m_g_mem': 'new_m', 'new_m_x_wq': 'new_m', 'new_m_x_wkv': 'new_m', 'new_m_x_wo': 'new_m', 'new_m_g_ffn': 'new_m', 'new_m_f_w1': 'new_m', 'new_m_f_w3': 'new_m', 'new_m_f_w2': 'new_m', 'new_m_g_final': 'new_m', 'new_v_w_in': 'new_v', 'new_v_g_mix': 'new_v', 'new_v_a_gq': 'new_v', 'new_v_a_gkv': 'new_v', 'new_v_a_wuq': 'new_v', 'new_v_a_wukv': 'new_v', 'new_v_b_lb': 'new_v', 'new_v_b_gout': 'new_v', 'new_v_c_sink': 'new_v', 'new_v_rel_bias': 'new_v', 'new_v_w_br_a': 'new_v', 'new_v_w_br_b': 'new_v', 'new_v_w_br_c': 'new_v', 'new_v_w_out': 'new_v', 'new_v_g_x': 'new_v', 'new_v_g_mem': 'new_v', 'new_v_x_wq': 'new_v', 'new_v_x_wkv': 'new_v', 'new_v_x_wo': 'new_v', 'new_v_g_ffn': 'new_v', 'new_v_f_w1': 'new_v', 'new_v_f_w3': 'new_v', 'new_v_f_w2': 'new_v', 'new_v_g_final': 'new_v'}


def _forward(args):
    return _fwd_reference(*[args[k] for k in FWD_PARAMS])


def _output_shape():
    def fwd():
        inp = _fwd_setup_inputs(0)
        return _fwd_reference(*[inp[k] for k in FWD_PARAMS])
    out = _jax.eval_shape(fwd)
    return out.shape, out.dtype

N_MICROBATCH = 1
ADAM_LR = 0.001
ADAM_B1 = 0.9
ADAM_B2 = 0.999
ADAM_EPS = 1e-08
ADAM_WD = 0.01
ADAM_STEP = 10
PER_EXAMPLE_BATCH_AXIS = {'x': 0, 'mem': 0, 'loss_target': 0}
SHARED_INPUTS = []
_WEIGHT_DTYPES = {'w_in': _jnp.float32, 'g_mix': _jnp.float32, 'a_gq': _jnp.float32, 'a_gkv': _jnp.float32, 'a_wuq': _jnp.float32, 'a_wukv': _jnp.float32, 'b_lb': _jnp.float32, 'b_gout': _jnp.float32, 'c_sink': _jnp.float32, 'rel_bias': _jnp.float32, 'w_br_a': _jnp.float32, 'w_br_b': _jnp.float32, 'w_br_c': _jnp.float32, 'w_out': _jnp.float32, 'g_x': _jnp.float32, 'g_mem': _jnp.float32, 'x_wq': _jnp.float32, 'x_wkv': _jnp.float32, 'x_wo': _jnp.float32, 'g_ffn': _jnp.float32, 'f_w1': _jnp.float32, 'f_w3': _jnp.float32, 'f_w2': _jnp.float32, 'g_final': _jnp.float32}
MOMENT_SCALE = {'w_in': 1.035920e-01, 'g_mix': 3.020535e-01, 'a_gq': 3.804105e-02, 'a_gkv': 7.561291e-02, 'a_wuq': 2.789296e-02, 'a_wukv': 3.369209e-02, 'b_lb': 3.176688e-02, 'b_gout': 5.440175e-01, 'c_sink': 1.579585e-03, 'rel_bias': 7.690042e-02, 'w_br_a': 2.669552e-02, 'w_br_b': 1.268692e-01, 'w_br_c': 3.011614e-02, 'w_out': 1.326697e-01, 'g_x': 4.073645e-02, 'g_mem': 6.214064e-02, 'x_wq': 4.088692e-02, 'x_wkv': 4.205648e-02, 'x_wo': 4.357835e-02, 'g_ffn': 2.699511e-01, 'f_w1': 1.150805e-01, 'f_w3': 1.117576e-01, 'f_w2': 1.848037e-01, 'g_final': 1.278816e+02}


def _to_microbatches(a, axis):
    t = _jnp.moveaxis(a, axis, 0)
    t = t.reshape((N_MICROBATCH, t.shape[0] // N_MICROBATCH) + t.shape[1:])
    return _jnp.moveaxis(t, 1, axis + 1)


def setup_inputs(seed: int = 0) -> dict:
    inp = _fwd_setup_inputs(seed)
    key = _jax.random.fold_in(_jax.random.key(seed), 7919)
    shape, _ = _output_shape()
    out = dict(inp)
    out["loss_target"] = _jax.random.normal(_jax.random.fold_in(key, 0), shape, _jnp.float32)
    for i, name in enumerate(TWIN_WEIGHTS):
        w = inp[name].astype(_jnp.float32)
        if MOMENT_SCALE is None:
            s = _jnp.sqrt(_jnp.mean(_jnp.square(w)) + 1e-30)
        else:
            s = MOMENT_SCALE[name]
        km, kv = _jax.random.split(_jax.random.fold_in(key, i + 1))
        out[name] = w
        out["m_" + name] = s * _jax.random.normal(km, w.shape, _jnp.float32)
        out["v_" + name] = (s * s) * _jax.random.uniform(kv, w.shape, _jnp.float32, 0.5, 1.5)
    if N_MICROBATCH > 1:
        for name, axis in PER_EXAMPLE_BATCH_AXIS.items():
            out[name] = _to_microbatches(out[name], axis)
    return {'x': out['x'], 'mem': out['mem'], 'w_in': out['w_in'], 'g_mix': out['g_mix'], 'a_gq': out['a_gq'], 'a_gkv': out['a_gkv'], 'a_wuq': out['a_wuq'], 'a_wukv': out['a_wukv'], 'b_lb': out['b_lb'], 'b_gout': out['b_gout'], 'c_sink': out['c_sink'], 'rel_bias': out['rel_bias'], 'w_br_a': out['w_br_a'], 'w_br_b': out['w_br_b'], 'w_br_c': out['w_br_c'], 'w_out': out['w_out'], 'g_x': out['g_x'], 'g_mem': out['g_mem'], 'x_wq': out['x_wq'], 'x_wkv': out['x_wkv'], 'x_wo': out['x_wo'], 'g_ffn': out['g_ffn'], 'f_w1': out['f_w1'], 'f_w3': out['f_w3'], 'f_w2': out['f_w2'], 'g_final': out['g_final'], 'loss_target': out['loss_target'], 'm_w_in': out['m_w_in'], 'm_g_mix': out['m_g_mix'], 'm_a_gq': out['m_a_gq'], 'm_a_gkv': out['m_a_gkv'], 'm_a_wuq': out['m_a_wuq'], 'm_a_wukv': out['m_a_wukv'], 'm_b_lb': out['m_b_lb'], 'm_b_gout': out['m_b_gout'], 'm_c_sink': out['m_c_sink'], 'm_rel_bias': out['m_rel_bias'], 'm_w_br_a': out['m_w_br_a'], 'm_w_br_b': out['m_w_br_b'], 'm_w_br_c': out['m_w_br_c'], 'm_w_out': out['m_w_out'], 'm_g_x': out['m_g_x'], 'm_g_mem': out['m_g_mem'], 'm_x_wq': out['m_x_wq'], 'm_x_wkv': out['m_x_wkv'], 'm_x_wo': out['m_x_wo'], 'm_g_ffn': out['m_g_ffn'], 'm_f_w1': out['m_f_w1'], 'm_f_w3': out['m_f_w3'], 'm_f_w2': out['m_f_w2'], 'm_g_final': out['m_g_final'], 'v_w_in': out['v_w_in'], 'v_g_mix': out['v_g_mix'], 'v_a_gq': out['v_a_gq'], 'v_a_gkv': out['v_a_gkv'], 'v_a_wuq': out['v_a_wuq'], 'v_a_wukv': out['v_a_wukv'], 'v_b_lb': out['v_b_lb'], 'v_b_gout': out['v_b_gout'], 'v_c_sink': out['v_c_sink'], 'v_rel_bias': out['v_rel_bias'], 'v_w_br_a': out['v_w_br_a'], 'v_w_br_b': out['v_w_br_b'], 'v_w_br_c': out['v_w_br_c'], 'v_w_out': out['v_w_out'], 'v_g_x': out['v_g_x'], 'v_g_mem': out['v_g_mem'], 'v_x_wq': out['v_x_wq'], 'v_x_wkv': out['v_x_wkv'], 'v_x_wo': out['v_x_wo'], 'v_g_ffn': out['v_g_ffn'], 'v_f_w1': out['v_f_w1'], 'v_f_w3': out['v_f_w3'], 'v_f_w2': out['v_f_w2'], 'v_g_final': out['v_g_final']}


def _loss(weights, diff, rest, loss_target):
    with _jax.named_scope("forward"):
        args = {**rest, TWIN_DIFF_INPUT: diff, **{k: w.astype(_WEIGHT_DTYPES[k]) for k, w in weights.items()}}
        y = _forward(args)
    with _jax.named_scope("loss_head"):
        err = _jnp.square(y.astype(_jnp.float32) - loss_target)
        return 0.5 * _jnp.sum(_jnp.mean(err, axis=-1)) if err.ndim else 0.5 * err


def _adamw(w, g, m, v):
    m = ADAM_B1 * m + (1.0 - ADAM_B1) * g
    v = ADAM_B2 * v + (1.0 - ADAM_B2) * _jnp.square(g)
    m_hat = m / (1.0 - ADAM_B1 ** ADAM_STEP)
    v_hat = v / (1.0 - ADAM_B2 ** ADAM_STEP)
    delta = -ADAM_LR * (m_hat / (_jnp.sqrt(v_hat) + ADAM_EPS) + ADAM_WD * w)
    return delta, m, v


def reference(x, mem, w_in, g_mix, a_gq, a_gkv, a_wuq, a_wukv, b_lb, b_gout, c_sink, rel_bias, w_br_a, w_br_b, w_br_c, w_out, g_x, g_mem, x_wq, x_wkv, x_wo, g_ffn, f_w1, f_w3, f_w2, g_final, loss_target, m_w_in, m_g_mix, m_a_gq, m_a_gkv, m_a_wuq, m_a_wukv, m_b_lb, m_b_gout, m_c_sink, m_rel_bias, m_w_br_a, m_w_br_b, m_w_br_c, m_w_out, m_g_x, m_g_mem, m_x_wq, m_x_wkv, m_x_wo, m_g_ffn, m_f_w1, m_f_w3, m_f_w2, m_g_final, v_w_in, v_g_mix, v_a_gq, v_a_gkv, v_a_wuq, v_a_wukv, v_b_lb, v_b_gout, v_c_sink, v_rel_bias, v_w_br_a, v_w_br_b, v_w_br_c, v_w_out, v_g_x, v_g_mem, v_x_wq, v_x_wkv, v_x_wo, v_g_ffn, v_f_w1, v_f_w3, v_f_w2, v_g_final):
    given = dict(x=x, mem=mem, w_in=w_in, g_mix=g_mix, a_gq=a_gq, a_gkv=a_gkv, a_wuq=a_wuq, a_wukv=a_wukv, b_lb=b_lb, b_gout=b_gout, c_sink=c_sink, rel_bias=rel_bias, w_br_a=w_br_a, w_br_b=w_br_b, w_br_c=w_br_c, w_out=w_out, g_x=g_x, g_mem=g_mem, x_wq=x_wq, x_wkv=x_wkv, x_wo=x_wo, g_ffn=g_ffn, f_w1=f_w1, f_w3=f_w3, f_w2=f_w2, g_final=g_final, loss_target=loss_target, m_w_in=m_w_in, m_g_mix=m_g_mix, m_a_gq=m_a_gq, m_a_gkv=m_a_gkv, m_a_wuq=m_a_wuq, m_a_wukv=m_a_wukv, m_b_lb=m_b_lb, m_b_gout=m_b_gout, m_c_sink=m_c_sink, m_rel_bias=m_rel_bias, m_w_br_a=m_w_br_a, m_w_br_b=m_w_br_b, m_w_br_c=m_w_br_c, m_w_out=m_w_out, m_g_x=m_g_x, m_g_mem=m_g_mem, m_x_wq=m_x_wq, m_x_wkv=m_x_wkv, m_x_wo=m_x_wo, m_g_ffn=m_g_ffn, m_f_w1=m_f_w1, m_f_w3=m_f_w3, m_f_w2=m_f_w2, m_g_final=m_g_final, v_w_in=v_w_in, v_g_mix=v_g_mix, v_a_gq=v_a_gq, v_a_gkv=v_a_gkv, v_a_wuq=v_a_wuq, v_a_wukv=v_a_wukv, v_b_lb=v_b_lb, v_b_gout=v_b_gout, v_c_sink=v_c_sink, v_rel_bias=v_rel_bias, v_w_br_a=v_w_br_a, v_w_br_b=v_w_br_b, v_w_br_c=v_w_br_c, v_w_out=v_w_out, v_g_x=v_g_x, v_g_mem=v_g_mem, v_x_wq=v_x_wq, v_x_wkv=v_x_wkv, v_x_wo=v_x_wo, v_g_ffn=v_g_ffn, v_f_w1=v_f_w1, v_f_w3=v_f_w3, v_f_w2=v_f_w2, v_g_final=v_g_final)
    weights = {n: given[n] for n in TWIN_WEIGHTS}
    shared = {n: given[n] for n in SHARED_INPUTS}
    per_example = {n: given[n] for n in ['x', 'mem']}
    grad_fn = _jax.value_and_grad(_loss, argnums=(0, 1))

    def one_microbatch(ex, loss_target):
        ex = dict(ex)
        diff = ex.pop(TWIN_DIFF_INPUT)
        return grad_fn(weights, diff, {**shared, **ex}, loss_target)

    if N_MICROBATCH == 1:
        loss, (grad_w, grad_x) = one_microbatch(per_example, given["loss_target"])
    else:
        def body(carry, xs):
            loss_sum, grad_sum = carry
            l_k, (gw_k, gx_k) = one_microbatch(xs[0], xs[1])
            with _jax.named_scope("update"):
                return (loss_sum + l_k, _jax.tree.map(_jnp.add, grad_sum, gw_k)), gx_k

        init = (_jnp.zeros((), _jnp.float32), _jax.tree.map(_jnp.zeros_like, weights))
        (loss, grad_w), grad_x = _jax.lax.scan(body, init, (per_example, given["loss_target"]))
    with _jax.named_scope("update"):
        delta_w, new_m, new_v = {}, {}, {}
        for n in TWIN_WEIGHTS:
            delta_w[n], new_m[n], new_v[n] = _adamw(weights[n], grad_w[n], given["m_" + n], given["v_" + n])
    return (loss, grad_x, *[grad_w[n] for n in TWIN_WEIGHTS], *[delta_w[n] for n in TWIN_WEIGHTS],
            *[new_m[n] for n in TWIN_WEIGHTS], *[new_v[n] for n in TWIN_WEIGHTS])
```

```python
import functools
import math

import jax
import jax.numpy as jnp
import numpy as np
from jax import lax
from jax.experimental import pallas as pl
from jax.experimental.pallas import tpu as pltpu

F32 = jnp.float32
MXU_DTYPE = jnp.bfloat16
ACT_DTYPE = jnp.bfloat16

V7X_VMEM_LIMIT_BYTES = 56 * 1024 * 1024
LANES = 128
SUBLANES = 8

N_DEV = 8
D_MODEL = 1024
DEPTH = 2
EPS = 1e-6
TINY = 1e-30
NEG = -1e30

A_HEADS, A_NOPE, A_ROPE, A_V, A_Q_RANK, A_KV_RANK = 8, 64, 32, 64, 384, 256
ROPE_THETA = 10000.0
B_HEADS, B_DK, B_DV, B_CHUNK = 8, 128, 64, 16
C_HEADS, C_KV_HEADS, C_DH, C_WINDOW, C_BLOCK = 8, 2, 64, 128, 128
REL_BUCKETS, REL_MAX_DIST = 32, 128
X_HEADS, X_DH = 4, 256
D_FF = 2816
IN_SPLITS = (A_Q_RANK, A_KV_RANK, A_ROPE, 1024, 1024, 1024, 512, 512, 512, 128, 128, 1024, 1024, 1024)
IN_WIDTH = sum(IN_SPLITS)
KR_PAD = LANES - A_ROPE
IN_SPLITS_PADDED = (A_Q_RANK, A_KV_RANK, LANES, 1024, 1024, 1024, 512, 512, 512, 128, 128, 1024, 1024, 1024)

ADAM_LR, ADAM_B1, ADAM_B2, ADAM_EPS, ADAM_WD, ADAM_STEP = 0.001, 0.9, 0.999, 1e-08, 0.01, 10

SHARDED = (("w_in", 2), ("a_wuq", 2), ("a_wukv", 2), ("b_lb", 2), ("w_br_a", 2), ("w_br_b", 2), ("w_br_c", 2),
           ("w_out", 1), ("x_wq", 1), ("x_wkv", 2), ("x_wo", 1), ("f_w1", 2), ("f_w3", 2), ("f_w2", 1))
REPLICATED = ("g_mix", "a_gq", "a_gkv", "b_gout", "c_sink", "rel_bias", "g_x", "g_mem", "g_ffn", "g_final")
WEIGHT_ORDER = ("w_in", "g_mix", "a_gq", "a_gkv", "a_wuq", "a_wukv", "b_lb", "b_gout", "c_sink", "rel_bias", "w_br_a",
                "w_br_b", "w_br_c", "w_out", "g_x", "g_mem", "x_wq", "x_wkv", "x_wo", "g_ffn", "f_w1", "f_w3", "f_w2",
                "g_final")
PACK_QUANTUM = SUBLANES * LANES
PACK_ROW_TILE = 512


def _pallas(body, **kw):
    return pl.pallas_call(body, **kw)


def _cparams(*sem):
    return pltpu.CompilerParams(dimension_semantics=sem, vmem_limit_bytes=V7X_VMEM_LIMIT_BYTES)


def _tile(n, target, mult=LANES):
    t = (min(target, n) // mult) * mult
    while t >= mult:
        if n % t == 0:
            return t
        t -= mult
    return n


def _dot(a, b, dims):
    return lax.dot_general(a.astype(MXU_DTYPE), b.astype(MXU_DTYPE), (dims, ((), ())), preferred_element_type=F32)


NN = ((1,), (0,))
NT = ((1,), (1,))
TN = ((0,), (0,))


def _mm(a, b, ta=False, tb=False, out_dtype=F32, name="mm"):
    m, k = (a.shape[1], a.shape[0]) if ta else a.shape
    kb, n = (b.shape[1], b.shape[0]) if tb else b.shape
    assert k == kb, (a.shape, b.shape, ta, tb)
    tm, tn, tk = _tile(m, 512), _tile(n, 512), _tile(k, 1024)
    nk = k // tk
    dims = ((0 if ta else 1,), (1 if tb else 0,))

    def body(a_ref, b_ref, o_ref, acc_ref):
        kk = pl.program_id(2)

        @pl.when(kk == 0)
        def _():
            acc_ref[...] = jnp.zeros_like(acc_ref)

        acc_ref[...] += _dot(a_ref[...], b_ref[...], dims)

        @pl.when(kk == nk - 1)
        def _():
            o_ref[...] = acc_ref[...].astype(o_ref.dtype)

    a_spec = pl.BlockSpec((tk, tm), lambda i, j, kk: (kk, i)) if ta else pl.BlockSpec((tm, tk), lambda i, j, kk: (i, kk))
    b_spec = pl.BlockSpec((tn, tk), lambda i, j, kk: (j, kk)) if tb else pl.BlockSpec((tk, tn), lambda i, j, kk: (kk, j))
    return _pallas(
        body, name=name, grid=(m // tm, n // tn, nk), in_specs=[a_spec, b_spec],
        out_specs=pl.BlockSpec((tm, tn), lambda i, j, kk: (i, j)), out_shape=jax.ShapeDtypeStruct((m, n), out_dtype),
        scratch_shapes=[pltpu.VMEM((tm, tn), F32)], compiler_params=_cparams("parallel", "parallel", "arbitrary"),
    )(a, b)


def linear(a, w, out_dtype=F32, name="lin"):
    @jax.custom_vjp
    def f(a, w):
        return _mm(a, w.astype(MXU_DTYPE), out_dtype=out_dtype, name=name + "_fwd")

    def fwd(a, w):
        wb = w.astype(MXU_DTYPE)
        return _mm(a, wb, out_dtype=out_dtype, name=name + "_fwd"), (a, wb)

    def bwd(res, g):
        a, wb = res
        da = _mm(g, wb, tb=True, out_dtype=a.dtype, name=name + "_dx")
        dw = _mm(a, g, ta=True, out_dtype=F32, name=name + "_dw")
        return da, dw

    f.defvjp(fwd, bwd)
    return f(a, w)


def _row_tile(rows, width):
    return _tile(rows, max(SUBLANES, (512 * 1024) // width), 16)


def rmsnorm(x, g, out_dtype=F32, name="rms"):
    rows, d = x.shape
    tr = _row_tile(rows, d)
    n_steps = rows // tr

    def fwd_body(x_ref, g_ref, o_ref):
        xv = x_ref[...].astype(F32)
        r = lax.rsqrt(jnp.mean(xv * xv, axis=-1, keepdims=True) + EPS)
        o_ref[...] = (xv * r * g_ref[...]).astype(o_ref.dtype)

    def bwd_body(x_ref, g_ref, dy_ref, dx_ref, dg_ref):
        xv = x_ref[...].astype(F32)
        dy = dy_ref[...].astype(F32)
        r = lax.rsqrt(jnp.mean(xv * xv, axis=-1, keepdims=True) + EPS)
        xh = xv * r
        dxh = dy * g_ref[...]
        dx_ref[...] = (r * (dxh - xh * jnp.mean(dxh * xh, axis=-1, keepdims=True))).astype(dx_ref.dtype)

        @pl.when(pl.program_id(0) == 0)
        def _():
            dg_ref[...] = jnp.zeros_like(dg_ref)

        dg_ref[...] += jnp.sum(dy * xh, axis=0, keepdims=True)

    row_spec = pl.BlockSpec((tr, d), lambda i: (i, 0))
    vec_spec = pl.BlockSpec((1, d), lambda i: (0, 0))

    def run_fwd(x, g):
        return _pallas(fwd_body, name=name + "_fwd", grid=(n_steps,), in_specs=[row_spec, vec_spec], out_specs=row_spec,
                       out_shape=jax.ShapeDtypeStruct((rows, d), out_dtype), compiler_params=_cparams("parallel"))(
            x, g.reshape(1, d).astype(F32))

    @jax.custom_vjp
    def f(x, g):
        return run_fwd(x, g)

    def fwd(x, g):
        return run_fwd(x, g), (x, g)

    def bwd(res, dy):
        x, g = res
        dx, dg = _pallas(
            bwd_body, name=name + "_bwd", grid=(n_steps,), in_specs=[row_spec, vec_spec, row_spec],
            out_specs=[row_spec, vec_spec],
            out_shape=[jax.ShapeDtypeStruct((rows, d), x.dtype), jax.ShapeDtypeStruct((1, d), F32)],
            compiler_params=_cparams("arbitrary"))(x, g.reshape(1, d).astype(F32), dy)
        return dx, dg.reshape(g.shape).astype(g.dtype)

    f.defvjp(fwd, bwd)
    return f(x, g)


def _rowdot(a, b, name):
    h, s, d = a.shape
    ts = _tile(s, 2048)

    def body(a_ref, b_ref, o_ref):
        o_ref[...] = jnp.sum(a_ref[...].astype(F32) * b_ref[...].astype(F32), axis=-1, keepdims=True)

    spec = pl.BlockSpec((None, ts, d), lambda hh, i: (hh, i, 0))
    return _pallas(body, name=name, grid=(h, s // ts), in_specs=[spec, spec],
                   out_specs=pl.BlockSpec((None, ts, 1), lambda hh, i: (hh, i, 0)),
                   out_shape=jax.ShapeDtypeStruct((h, s, 1), F32), compiler_params=_cparams("parallel", "parallel"))(a, b)


def attention(q, k, v, scale, tq, tk, name):
    h, sq, d = q.shape
    sk, dv = k.shape[1], v.shape[2]
    tq, tk = _tile(sq, tq), _tile(sk, tk)
    nq, nk = sq // tq, sk // tk

    def fwd_body(q_ref, k_ref, v_ref, o_ref, lse_ref, m_ref, l_ref, acc_ref):
        j = pl.program_id(2)

        @pl.when(j == 0)
        def _():
            m_ref[...] = jnp.full_like(m_ref, NEG)
            l_ref[...] = jnp.zeros_like(l_ref)
            acc_ref[...] = jnp.zeros_like(acc_ref)

        s = _dot(q_ref[...], k_ref[...], NT) * scale
        m_prev = m_ref[...]
        m_new = jnp.maximum(m_prev, jnp.max(s, axis=-1, keepdims=True))
        alpha = jnp.exp(m_prev - m_new)
        p = jnp.exp(s - m_new)
        l_ref[...] = alpha * l_ref[...] + jnp.sum(p, axis=-1, keepdims=True)
        acc_ref[...] = alpha * acc_ref[...] + _dot(p, v_ref[...], NN)
        m_ref[...] = m_new

        @pl.when(j == nk - 1)
        def _():
            o_ref[...] = (acc_ref[...] / l_ref[...]).astype(o_ref.dtype)
            lse_ref[...] = m_ref[...] + jnp.log(l_ref[...])

    def dq_body(q_ref, k_ref, v_ref, do_ref, lse_ref, dl_ref, dq_ref, acc_ref):
        j = pl.program_id(2)

        @pl.when(j == 0)
        def _():
            acc_ref[...] = jnp.zeros_like(acc_ref)

        s = _dot(q_ref[...], k_ref[...], NT) * scale
        p = jnp.exp(s - lse_ref[...])
        dp = _dot(do_ref[...], v_ref[...], NT)
        ds = p * (dp - dl_ref[...]) * scale
        acc_ref[...] += _dot(ds, k_ref[...], NN)

        @pl.when(j == nk - 1)
        def _():
            dq_ref[...] = acc_ref[...].astype(dq_ref.dtype)

    def dkv_body(q_ref, k_ref, v_ref, do_ref, lse_ref, dl_ref, dk_ref, dv_ref, dk_acc, dv_acc):
        i = pl.program_id(2)

        @pl.when(i == 0)
        def _():
            dk_acc[...] = jnp.zeros_like(dk_acc)
            dv_acc[...] = jnp.zeros_like(dv_acc)

        s = _dot(q_ref[...], k_ref[...], NT) * scale
        p = jnp.exp(s - lse_ref[...])
        dv_acc[...] += _dot(p, do_ref[...], TN)
        dp = _dot(do_ref[...], v_ref[...], NT)
        ds = p * (dp - dl_ref[...]) * scale
        dk_acc[...] += _dot(ds, q_ref[...], TN)

        @pl.when(i == nq - 1)
        def _():
            dk_ref[...] = dk_acc[...].astype(dk_ref.dtype)
            dv_ref[...] = dv_acc[...].astype(dv_ref.dtype)

    def q_spec(width):
        return pl.BlockSpec((None, tq, width), lambda hh, i, j: (hh, i, 0))

    def k_spec(width):
        return pl.BlockSpec((None, tk, width), lambda hh, i, j: (hh, j, 0))

    def run_fwd(q, k, v):
        return _pallas(
            fwd_body, name=name + "_fwd", grid=(h, nq, nk), in_specs=[q_spec(d), k_spec(d), k_spec(dv)],
            out_specs=[q_spec(dv), q_spec(1)],
            out_shape=[jax.ShapeDtypeStruct((h, sq, dv), q.dtype), jax.ShapeDtypeStruct((h, sq, 1), F32)],
            scratch_shapes=[pltpu.VMEM((tq, 1), F32), pltpu.VMEM((tq, 1), F32), pltpu.VMEM((tq, dv), F32)],
            compiler_params=_cparams("parallel", "parallel", "arbitrary"))(q, k, v)

    @jax.custom_vjp
    def f(q, k, v):
        return run_fwd(q, k, v)[0]

    def fwd(q, k, v):
        o, lse = run_fwd(q, k, v)
        return o, (q, k, v, o, lse)

    def bwd(res, do):
        q, k, v, o, lse = res
        delta = _rowdot(o, do, name + "_delta")
        dq = _pallas(
            dq_body, name=name + "_dq", grid=(h, nq, nk),
            in_specs=[q_spec(d), k_spec(d), k_spec(dv), q_spec(dv), q_spec(1), q_spec(1)], out_specs=q_spec(d),
            out_shape=jax.ShapeDtypeStruct((h, sq, d), q.dtype), scratch_shapes=[pltpu.VMEM((tq, d), F32)],
            compiler_params=_cparams("parallel", "parallel", "arbitrary"))(q, k, v, do, lse, delta)

        def qs(width):
            return pl.BlockSpec((None, tq, width), lambda hh, j, i: (hh, i, 0))

        def ks(width):
            return pl.BlockSpec((None, tk, width), lambda hh, j, i: (hh, j, 0))

        dk, dv_ = _pallas(
            dkv_body, name=name + "_dkv", grid=(h, nk, nq),
            in_specs=[qs(d), ks(d), ks(dv), qs(dv), qs(1), qs(1)], out_specs=[ks(d), ks(dv)],
            out_shape=[jax.ShapeDtypeStruct((h, sk, d), k.dtype), jax.ShapeDtypeStruct((h, sk, dv), v.dtype)],
            scratch_shapes=[pltpu.VMEM((tk, d), F32), pltpu.VMEM((tk, dv), F32)],
            compiler_params=_cparams("parallel", "parallel", "arbitrary"))(q, k, v, do, lse, delta)
        return dq, dk, dv_

    f.defvjp(fwd, bwd)
    return f(q, k, v)


def window_attention(q, k, v, bias, sink, name="wattn"):
    hq, s, dh = q.shape
    g = hq // C_KV_HEADS
    blk = C_BLOCK
    nb = s // blk
    scale = dh ** -0.5
    sink_b = jnp.broadcast_to(sink.astype(F32).reshape(hq, 1, 1), (hq, 1, LANES))

    def key_block(i, j):
        return jnp.clip(i - 1 + j, 0, nb - 1)

    def in_range(i, j):
        kb = i - 1 + j
        return jnp.logical_and(kb >= 0, kb < nb)

    def scores(q_ref, k_ref, b_ref, hh):
        return _dot(q_ref[hh], k_ref[...], NT) * scale + b_ref[hh]

    def fwd_body(q_ref, k_ref, v_ref, b_ref, sk_ref, o_ref, lse_ref, m_ref, l_ref, acc_ref):
        i, j = pl.program_id(1), pl.program_id(2)

        @pl.when(j == 0)
        def _():
            for hh in range(g):
                m_ref[hh] = jnp.broadcast_to(sk_ref[hh][:, :1], (blk, 1))
            l_ref[...] = jnp.ones_like(l_ref)
            acc_ref[...] = jnp.zeros_like(acc_ref)

        @pl.when(in_range(i, j))
        def _():
            for hh in range(g):
                sc = scores(q_ref, k_ref, b_ref, hh)
                m_prev = m_ref[hh]
                m_new = jnp.maximum(m_prev, jnp.max(sc, axis=-1, keepdims=True))
                alpha = jnp.exp(m_prev - m_new)
                p = jnp.exp(sc - m_new)
                l_ref[hh] = alpha * l_ref[hh] + jnp.sum(p, axis=-1, keepdims=True)
                acc_ref[hh] = alpha * acc_ref[hh] + _dot(p, v_ref[...], NN)
                m_ref[hh] = m_new

        @pl.when(j == 2)
        def _():
            o_ref[...] = (acc_ref[...] / l_ref[...]).astype(o_ref.dtype)
            lse_ref[...] = m_ref[...] + jnp.log(l_ref[...])

    def dq_body(q_ref, k_ref, v_ref, b_ref, sk_ref, do_ref, lse_ref, dl_ref, dq_ref, db_ref, dsink_ref, acc_ref):
        i, j = pl.program_id(1), pl.program_id(2)

        @pl.when(jnp.logical_and(i == 0, j == 0))
        def _():
            db_ref[...] = jnp.zeros_like(db_ref)
            dsink_ref[...] = jnp.zeros_like(dsink_ref)

        @pl.when(j == 0)
        def _():
            acc_ref[...] = jnp.zeros_like(acc_ref)
            for hh in range(g):
                p_sink = jnp.exp(sk_ref[hh][:, :1] - lse_ref[hh])
                total = jnp.broadcast_to(-jnp.sum(p_sink * dl_ref[hh], axis=0, keepdims=True), (1, LANES))
                dsink_ref[hh] += jnp.where(lax.broadcasted_iota(jnp.int32, (1, LANES), 1) == 0, total, 0.0)

        @pl.when(in_range(i, j))
        def _():
            for hh in range(g):
                p = jnp.exp(scores(q_ref, k_ref, b_ref, hh) - lse_ref[hh])
                dp = _dot(do_ref[hh], v_ref[...], NT)
                ds = p * (dp - dl_ref[hh])
                db_ref[hh, j] += ds
                acc_ref[hh] += _dot(ds * scale, k_ref[...], NN)

        @pl.when(j == 2)
        def _():
            dq_ref[...] = acc_ref[...].astype(dq_ref.dtype)

    def dkv_body(q_ref, k_ref, v_ref, b_ref, do_ref, lse_ref, dl_ref, dk_ref, dv_ref, dk_acc, dv_acc):
        kb, jj = pl.program_id(1), pl.program_id(2)
        qi = kb + 1 - jj

        @pl.when(jj == 0)
        def _():
            dk_acc[...] = jnp.zeros_like(dk_acc)
            dv_acc[...] = jnp.zeros_like(dv_acc)

        @pl.when(jnp.logical_and(qi >= 0, qi < nb))
        def _():
            for hh in range(g):
                p = jnp.exp(scores(q_ref, k_ref, b_ref, hh) - lse_ref[hh])
                dv_acc[...] += _dot(p, do_ref[hh], TN)
                dp = _dot(do_ref[hh], v_ref[...], NT)
                ds = p * (dp - dl_ref[hh]) * scale
                dk_acc[...] += _dot(ds, q_ref[hh], TN)

        @pl.when(jj == 2)
        def _():
            dk_ref[...] = dk_acc[...].astype(dk_ref.dtype)
            dv_ref[...] = dv_acc[...].astype(dv_ref.dtype)

    def q_spec(width):
        return pl.BlockSpec((g, blk, width), lambda kv, i, j: (kv, i, 0))

    kv_spec = pl.BlockSpec((None, blk, dh), lambda kv, i, j: (kv, key_block(i, j), 0))
    b_spec = pl.BlockSpec((g, None, blk, blk), lambda kv, i, j: (kv, j, 0, 0))
    sk_spec = pl.BlockSpec((g, 1, LANES), lambda kv, i, j: (kv, 0, 0))

    def run_fwd(q, k, v, bias, sink_b):
        return _pallas(
            fwd_body, name=name + "_fwd", grid=(C_KV_HEADS, nb, 3), in_specs=[q_spec(dh), kv_spec, kv_spec, b_spec, sk_spec],
            out_specs=[q_spec(dh), q_spec(1)],
            out_shape=[jax.ShapeDtypeStruct((hq, s, dh), q.dtype), jax.ShapeDtypeStruct((hq, s, 1), F32)],
            scratch_shapes=[pltpu.VMEM((g, blk, 1), F32), pltpu.VMEM((g, blk, 1), F32), pltpu.VMEM((g, blk, dh), F32)],
            compiler_params=_cparams("parallel", "parallel", "arbitrary"))(q, k, v, bias, sink_b)

    @jax.custom_vjp
    def f(q, k, v, bias, sink_b):
        return run_fwd(q, k, v, bias, sink_b)[0]

    def fwd(q, k, v, bias, sink_b):
        o, lse = run_fwd(q, k, v, bias, sink_b)
        return o, (q, k, v, bias, sink_b, o, lse)

    def bwd(res, do):
        q, k, v, bias, sink_b, o, lse = res
        delta = _rowdot(o, do, name + "_delta")
        dq, dbias, dsink = _pallas(
            dq_body, name=name + "_dq", grid=(C_KV_HEADS, nb, 3),
            in_specs=[q_spec(dh), kv_spec, kv_spec, b_spec, sk_spec, q_spec(dh), q_spec(1), q_spec(1)],
            out_specs=[q_spec(dh), pl.BlockSpec((g, 3, blk, blk), lambda kv, i, j: (kv, 0, 0, 0)), sk_spec],
            out_shape=[jax.ShapeDtypeStruct((hq, s, dh), q.dtype), jax.ShapeDtypeStruct((hq, 3, blk, blk), F32),
                       jax.ShapeDtypeStruct((hq, 1, LANES), F32)],
            scratch_shapes=[pltpu.VMEM((g, blk, dh), F32)],
            compiler_params=_cparams("arbitrary", "arbitrary", "arbitrary"))(q, k, v, bias, sink_b, do, lse, delta)

        def qs(width):
            return pl.BlockSpec((g, blk, width), lambda kv, kb, jj: (kv, jnp.clip(kb + 1 - jj, 0, nb - 1), 0))

        ks = pl.BlockSpec((None, blk, dh), lambda kv, kb, jj: (kv, kb, 0))
        bs = pl.BlockSpec((g, None, blk, blk), lambda kv, kb, jj: (kv, jj, 0, 0))
        dk, dv_ = _pallas(
            dkv_body, name=name + "_dkv", grid=(C_KV_HEADS, nb, 3),
            in_specs=[qs(dh), ks, ks, bs, qs(dh), qs(1), qs(1)], out_specs=[ks, ks],
            out_shape=[jax.ShapeDtypeStruct(k.shape, k.dtype), jax.ShapeDtypeStruct(v.shape, v.dtype)],
            scratch_shapes=[pltpu.VMEM((blk, dh), F32), pltpu.VMEM((blk, dh), F32)],
            compiler_params=_cparams("parallel", "parallel", "arbitrary"))(q, k, v, bias, do, lse, delta)
        return dq, dk, dv_, dbias, dsink

    f.defvjp(fwd, bwd)
    return f(q, k, v, bias, sink_b)


HG_INTRA_BLOCK = 256


def hgrn_intra(qd, ki, v, name="hg_intra"):
    h, s, dk = qd.shape
    dv = v.shape[2]
    tb = _tile(s, HG_INTRA_BLOCK)

    def mask():
        r = lax.broadcasted_iota(jnp.int32, (tb, tb), 0)
        c = lax.broadcasted_iota(jnp.int32, (tb, tb), 1)
        return jnp.logical_and(r // B_CHUNK == c // B_CHUNK, c <= r)

    def fwd_body(q_ref, k_ref, v_ref, o_ref):
        sc = jnp.where(mask(), _dot(q_ref[...], k_ref[...], NT), 0.0)
        o_ref[...] = _dot(sc, v_ref[...], NN)

    def bwd_body(q_ref, k_ref, v_ref, do_ref, dq_ref, dk_ref, dv_ref):
        msk = mask()
        sc = jnp.where(msk, _dot(q_ref[...], k_ref[...], NT), 0.0)
        ds = jnp.where(msk, _dot(do_ref[...], v_ref[...], NT), 0.0)
        dq_ref[...] = _dot(ds, k_ref[...], NN).astype(dq_ref.dtype)
        dk_ref[...] = _dot(ds, q_ref[...], TN).astype(dk_ref.dtype)
        dv_ref[...] = _dot(sc, do_ref[...], TN).astype(dv_ref.dtype)

    def spec(width):
        return pl.BlockSpec((None, tb, width), lambda hh, i: (hh, i, 0))

    def run_fwd(qd, ki, v):
        return _pallas(fwd_body, name=name + "_fwd", grid=(h, s // tb), in_specs=[spec(dk), spec(dk), spec(dv)],
                       out_specs=spec(dv), out_shape=jax.ShapeDtypeStruct((h, s, dv), F32),
                       compiler_params=_cparams("parallel", "parallel"))(qd, ki, v)

    @jax.custom_vjp
    def f(qd, ki, v):
        return run_fwd(qd, ki, v)

    def fwd(qd, ki, v):
        return run_fwd(qd, ki, v), (qd, ki, v)

    def bwd(res, do):
        qd, ki, v = res
        return tuple(_pallas(
            bwd_body, name=name + "_bwd", grid=(h, s // tb), in_specs=[spec(dk), spec(dk), spec(dv), spec(dv)],
            out_specs=[spec(dk), spec(dk), spec(dv)],
            out_shape=[jax.ShapeDtypeStruct(qd.shape, qd.dtype), jax.ShapeDtypeStruct(ki.shape, ki.dtype),
                       jax.ShapeDtypeStruct(v.shape, v.dtype)],
            compiler_params=_cparams("parallel", "parallel"))(qd, ki, v, do))

    f.defvjp(fwd, bwd)
    return f(qd, ki, v)


HG_INTER_CHUNKS = 16
MLA_TQ, MLA_TK = 512, 1024
CROSS_TQ = 1024


def hgrn_inter(qd, ke, v, dec, name="hg_inter"):
    h, s, dk = qd.shape
    dv = v.shape[2]
    nc = s // B_CHUNK
    cpb = HG_INTER_CHUNKS if nc % HG_INTER_CHUNKS == 0 else nc
    tb = cpb * B_CHUNK
    nblk = nc // cpb

    def rows(c):
        return pl.ds(c * B_CHUNK, B_CHUNK)

    def fwd_body(q_ref, k_ref, v_ref, dec_ref, o_ref, st_ref, state):
        @pl.when(pl.program_id(1) == 0)
        def _():
            state[...] = jnp.zeros_like(state)

        for c in range(cpb):
            st = state[...]
            st_ref[c] = st
            o_ref[rows(c), :] = _dot(q_ref[rows(c), :], st, NT)
            state[...] = st * dec_ref[pl.ds(c, 1), :] + _dot(v_ref[rows(c), :], k_ref[rows(c), :], TN)

    def bwd_body(q_ref, k_ref, v_ref, dec_ref, st_ref, do_ref, dq_ref, dk_ref, dv_ref, ddec_ref, dstate):
        @pl.when(pl.program_id(1) == 0)
        def _():
            dstate[...] = jnp.zeros_like(dstate)

        for c in reversed(range(cpb)):
            dst = dstate[...]
            st = st_ref[c]
            do_c = do_ref[rows(c), :]
            dk_ref[rows(c), :] = _dot(v_ref[rows(c), :], dst, NN).astype(dk_ref.dtype)
            dv_ref[rows(c), :] = _dot(k_ref[rows(c), :], dst, NT).astype(dv_ref.dtype)
            ddec_ref[pl.ds(c, 1), :] = jnp.sum(dst * st, axis=0, keepdims=True)
            dq_ref[rows(c), :] = _dot(do_c, st, NN).astype(dq_ref.dtype)
            dstate[...] = dst * dec_ref[pl.ds(c, 1), :] + _dot(do_c, q_ref[rows(c), :], TN)

    def tok(width, rev=False):
        if rev:
            return pl.BlockSpec((None, tb, width), lambda hh, i: (hh, nblk - 1 - i, 0))
        return pl.BlockSpec((None, tb, width), lambda hh, i: (hh, i, 0))

    def chk(rev=False):
        if rev:
            return pl.BlockSpec((None, cpb, dk), lambda hh, i: (hh, nblk - 1 - i, 0))
        return pl.BlockSpec((None, cpb, dk), lambda hh, i: (hh, i, 0))

    def sts(rev=False):
        if rev:
            return pl.BlockSpec((None, cpb, dv, dk), lambda hh, i: (hh, nblk - 1 - i, 0, 0))
        return pl.BlockSpec((None, cpb, dv, dk), lambda hh, i: (hh, i, 0, 0))

    def run_fwd(qd, ke, v, dec):
        return _pallas(
            fwd_body, name=name + "_fwd", grid=(h, nblk), in_specs=[tok(dk), tok(dk), tok(dv), chk()],
            out_specs=[tok(dv), sts()],
            out_shape=[jax.ShapeDtypeStruct((h, s, dv), F32), jax.ShapeDtypeStruct((h, nc, dv, dk), F32)],
            scratch_shapes=[pltpu.VMEM((dv, dk), F32)], compiler_params=_cparams("parallel", "arbitrary"))(qd, ke, v, dec)

    @jax.custom_vjp
    def f(qd, ke, v, dec):
        return run_fwd(qd, ke, v, dec)[0]

    def fwd(qd, ke, v, dec):
        o, st = run_fwd(qd, ke, v, dec)
        return o, (qd, ke, v, dec, st)

    def bwd(res, do):
        qd, ke, v, dec, st = res
        return tuple(_pallas(
            bwd_body, name=name + "_bwd", grid=(h, nblk),
            in_specs=[tok(dk, True), tok(dk, True), tok(dv, True), chk(True), sts(True), tok(dv, True)],
            out_specs=[tok(dk, True), tok(dk, True), tok(dv, True), chk(True)],
            out_shape=[jax.ShapeDtypeStruct(qd.shape, qd.dtype), jax.ShapeDtypeStruct(ke.shape, ke.dtype),
                       jax.ShapeDtypeStruct(v.shape, v.dtype), jax.ShapeDtypeStruct(dec.shape, F32)],
            scratch_shapes=[pltpu.VMEM((dv, dk), F32)], compiler_params=_cparams("parallel", "arbitrary"))(
            qd, ke, v, dec, st, do))

    f.defvjp(fwd, bwd)
    return f(qd, ke, v, dec)


def loss_head(y, target, name="loss"):
    s, d = y.shape
    tr = _row_tile(s, d)

    def body(y_ref, t_ref, o_ref):
        @pl.when(pl.program_id(0) == 0)
        def _():
            o_ref[...] = jnp.zeros_like(o_ref)

        e = y_ref[...] - t_ref[...]
        part = jnp.sum(jnp.sum(e * e, axis=-1, keepdims=True), axis=0, keepdims=True) * (0.5 / d)
        o_ref[...] += jnp.broadcast_to(part, o_ref.shape)

    spec = pl.BlockSpec((tr, d), lambda i: (i, 0))

    def run(y, t):
        out = _pallas(body, name=name, grid=(s // tr,), in_specs=[spec, spec],
                      out_specs=pl.BlockSpec((SUBLANES, LANES), lambda i: (0, 0)),
                      out_shape=jax.ShapeDtypeStruct((SUBLANES, LANES), F32), compiler_params=_cparams("arbitrary"))(y, t)
        return out[0, 0]

    @jax.custom_vjp
    def f(y, t):
        return run(y, t)

    def fwd(y, t):
        return run(y, t), (y, t)

    def bwd(res, g):
        y, t = res
        dy = g * (y - t) * (1.0 / d)
        return dy, -dy

    f.defvjp(fwd, bwd)
    return f(y, target)


def _mesh_pos():
    return lax.axis_index("x"), lax.axis_index("y"), lax.axis_index("c")


def all_gather_rows(shard):
    r, w = shard.shape

    def body(x_ref, out_ref, send_sems, recv_sems, local_sem):
        x, y, c = _mesh_pos()
        me, sibling = (x, y, c), (x, y, 1 - c)
        chips = [(1 - x, y), (x, 1 - y), (1 - x, 1 - y)]

        def slot(px, py, pc):
            return out_ref.at[4 * px + 2 * py + pc]

        def copy(k, block, to, src=None):
            return pltpu.make_async_remote_copy(
                src_ref=slot(*block) if src is None else src, dst_ref=slot(*block), send_sem=send_sems.at[k],
                recv_sem=recv_sems.at[k], device_id=to, device_id_type=pl.DeviceIdType.MESH)

        mine = pltpu.make_async_copy(x_ref, slot(*me), local_sem)
        mine.start()
        first = [copy(0, me, sibling, src=x_ref)]
        first += [copy(1 + j, me, (*chip, c), src=x_ref) for j, chip in enumerate(chips)]
        for cp in first:
            cp.start()
        passed = [copy(4 + j, (*chip, c), sibling) for j, chip in enumerate(chips)]
        for j, chip in enumerate(chips):
            copy(1 + j, (*chip, c), me).wait_recv()
            passed[j].start()
        copy(0, sibling, me).wait_recv()
        for j, chip in enumerate(chips):
            copy(4 + j, (*chip, 1 - c), me).wait_recv()
        for cp in first + passed:
            cp.wait_send()
        mine.wait()

    any_spec = pl.BlockSpec(memory_space=pl.ANY)
    return _pallas(
        body, name="all_gather_weights", out_shape=jax.ShapeDtypeStruct((N_DEV, r, w), shard.dtype), in_specs=[any_spec],
        out_specs=any_spec,
        scratch_shapes=[pltpu.SemaphoreType.DMA((7,)), pltpu.SemaphoreType.DMA((7,)), pltpu.SemaphoreType.DMA],
    )(shard)


def all_to_all_rows(blocks):
    n, r, w = blocks.shape

    def body(g_ref, out_ref, send_sems, recv_sems, local_sem):
        x, y, c = _mesh_pos()
        me = 4 * x + 2 * y + c
        mine = pltpu.make_async_copy(g_ref.at[me], out_ref.at[me], local_sem)
        mine.start()
        copies = []
        for k in range(1, N_DEV):
            px = 1 - x if k & 4 else x
            py = 1 - y if k & 2 else y
            pc = 1 - c if k & 1 else c
            cp = pltpu.make_async_remote_copy(
                src_ref=g_ref.at[4 * px + 2 * py + pc], dst_ref=out_ref.at[me], send_sem=send_sems.at[k - 1],
                recv_sem=recv_sems.at[k - 1], device_id=(px, py, pc), device_id_type=pl.DeviceIdType.MESH)
            cp.start()
            copies.append(cp)
        for cp in copies:
            cp.wait_recv()
        for cp in copies:
            cp.wait_send()
        mine.wait()

    any_spec = pl.BlockSpec(memory_space=pl.ANY)
    return _pallas(
        body, name="all_to_all_grads", out_shape=jax.ShapeDtypeStruct((n, r, w), blocks.dtype), in_specs=[any_spec],
        out_specs=any_spec,
        scratch_shapes=[pltpu.SemaphoreType.DMA((7,)), pltpu.SemaphoreType.DMA((7,)), pltpu.SemaphoreType.DMA],
    )(blocks)


def all_gather_small(v):
    r, w = v.shape

    def body(x_ref, out_ref, send_sems, recv_sems):
        x, y, c = _mesh_pos()
        me = 4 * x + 2 * y + c
        copies = []
        for k in range(1, N_DEV):
            px = 1 - x if k & 4 else x
            py = 1 - y if k & 2 else y
            pc = 1 - c if k & 1 else c
            cp = pltpu.make_async_remote_copy(
                src_ref=x_ref, dst_ref=out_ref.at[me], send_sem=send_sems.at[k - 1], recv_sem=recv_sems.at[k - 1],
                device_id=(px, py, pc), device_id_type=pl.DeviceIdType.MESH)
            cp.start()
            copies.append(cp)
        out_ref[me] = x_ref[...]
        for cp in copies:
            cp.wait_recv()
        for cp in copies:
            cp.wait_send()

    vmem = pl.BlockSpec(memory_space=pltpu.VMEM)
    return _pallas(
        body, name="all_gather_small", out_shape=jax.ShapeDtypeStruct((N_DEV, r, w), v.dtype), in_specs=[vmem],
        out_specs=vmem, scratch_shapes=[pltpu.SemaphoreType.DMA((7,)), pltpu.SemaphoreType.DMA((7,))],
    )(v)


def adamw_rows(parts, w, m, v, name):
    n, r, lanes = parts.shape
    tr = _tile(r, PACK_ROW_TILE, SUBLANES)
    c1 = 1.0 / (1.0 - ADAM_B1 ** ADAM_STEP)
    c2 = 1.0 / (1.0 - ADAM_B2 ** ADAM_STEP)

    def body(p_ref, w_ref, m_ref, v_ref, g_ref, d_ref, nm_ref, nv_ref):
        g = p_ref[0]
        for j in range(1, n):
            g = g + p_ref[j]
        nm = ADAM_B1 * m_ref[...] + (1.0 - ADAM_B1) * g
        nv = ADAM_B2 * v_ref[...] + (1.0 - ADAM_B2) * (g * g)
        g_ref[...] = g
        nm_ref[...] = nm
        nv_ref[...] = nv
        d_ref[...] = -ADAM_LR * ((nm * c1) / (jnp.sqrt(nv * c2) + ADAM_EPS) + ADAM_WD * w_ref[...])

    row = pl.BlockSpec((tr, lanes), lambda i: (i, 0))
    out = jax.ShapeDtypeStruct((r, lanes), F32)
    return _pallas(body, name=name, grid=(r // tr,), in_specs=[pl.BlockSpec((n, tr, lanes), lambda i: (0, i, 0)), row, row, row],
                   out_specs=[row, row, row, row], out_shape=[out, out, out, out], compiler_params=_cparams("parallel"))(
        parts, w, m, v)


def _padded(n):
    return -(-n // PACK_QUANTUM) * PACK_QUANTUM


def _pack(pieces, total_rows=None):
    flat = []
    for p in pieces:
        p = p.reshape(-1).astype(F32)
        flat.append(jnp.pad(p, (0, _padded(p.size) - p.size)))
    out = jnp.concatenate(flat).reshape(-1, LANES)
    if total_rows is not None and out.shape[0] != total_rows:
        out = jnp.pad(out, ((0, total_rows - out.shape[0]), (0, 0)))
    return out


def _pack_rows(sizes):
    rows = sum(_padded(n) for n in sizes) // LANES
    return -(-rows // PACK_ROW_TILE) * PACK_ROW_TILE


def _unpack(rows, shapes):
    lead = rows.shape[:-2]
    flat = rows.reshape(*lead, -1)
    out, off = [], 0
    for shp in shapes:
        n = int(np.prod(shp))
        out.append(flat[..., off:off + n].reshape(*lead, *shp))
        off += _padded(n)
    return out


def _shards_to_full(stacked, axis):
    moved = jnp.moveaxis(stacked, 0, axis)
    shp = list(stacked.shape[1:])
    shp[axis] *= N_DEV
    return moved.reshape(shp)


def _full_to_shards(full, axis):
    shp = list(full.shape)
    shp[axis:axis + 1] = [N_DEV, shp[axis] // N_DEV]
    return jnp.moveaxis(full.reshape(shp), axis, 0)


def _heads(t, n, d):
    return jnp.transpose(t.reshape(t.shape[0], n, d), (1, 0, 2)).astype(ACT_DTYPE)


def _unheads(t):
    return jnp.transpose(t, (1, 0, 2)).reshape(t.shape[1], -1)


def _rope_tables(s):
    half = A_ROPE // 2
    inv = ROPE_THETA ** (-jnp.arange(half, dtype=F32) / half)
    ang = jnp.arange(s, dtype=jnp.int32).astype(F32)[:, None] * inv[None, :]
    return jnp.cos(ang), jnp.sin(ang)


def _rope(t, cos, sin):
    half = A_ROPE // 2
    t1, t2 = t[..., :half], t[..., half:]
    c, sn = cos[:, None, :], sin[:, None, :]
    return jnp.concatenate([t1 * c - t2 * sn, t1 * sn + t2 * c], axis=-1)


def _t5_bucket(rel):
    nb = REL_BUCKETS // 2
    max_exact = nb // 2
    ret = (rel > 0).astype(jnp.int32) * nb
    n = jnp.abs(rel)
    large = max_exact + (jnp.log(jnp.maximum(n, 1).astype(F32) / max_exact)
                         / math.log(REL_MAX_DIST / max_exact) * (nb - max_exact)).astype(jnp.int32)
    large = jnp.minimum(large, nb - 1)
    return ret + jnp.where(n < max_exact, n, large)


def _window_bias(rel_bias):
    span = 3 * C_BLOCK
    rel = jnp.arange(span)[None, :] - C_BLOCK - jnp.arange(C_BLOCK)[:, None]
    onehot = (_t5_bucket(rel)[..., None] == jnp.arange(REL_BUCKETS)).astype(F32)
    bias = jnp.einsum("qkb,bh->hqk", onehot, rel_bias.astype(F32), precision=lax.Precision.HIGHEST)
    bias = jnp.where((jnp.abs(rel) <= C_WINDOW)[None], bias, NEG)
    return jnp.transpose(bias.reshape(C_HEADS, C_BLOCK, 3, C_BLOCK), (0, 2, 1, 3))


def _mla(cq, ckv, kr, gq, gkv, wuq, wukv, cos, sin):
    s = cq.shape[0]
    q = linear(rmsnorm(cq, gq, ACT_DTYPE, "rms_cq"), wuq, name="a_wuq").reshape(s, A_HEADS, A_NOPE + A_ROPE)
    q = jnp.concatenate([q[..., :A_NOPE], _rope(q[..., A_NOPE:], cos, sin)], axis=-1)
    kv = linear(rmsnorm(ckv, gkv, ACT_DTYPE, "rms_ckv"), wukv, name="a_wukv").reshape(s, A_HEADS, A_NOPE + A_V)
    k_rope = jnp.broadcast_to(_rope(kr[:, None, :], cos, sin), (s, A_HEADS, A_ROPE))
    k = jnp.concatenate([kv[..., :A_NOPE], k_rope], axis=-1)
    v = kv[..., A_NOPE:]
    tr = lambda t: jnp.transpose(t, (1, 0, 2)).astype(ACT_DTYPE)
    o = attention(tr(q), tr(k), tr(v), (A_NOPE + A_ROPE) ** -0.5, MLA_TQ, MLA_TK, "mla")
    return _unheads(o)


def _gated_scan(qh, kh, vh, log_f):
    s = qh.shape[0]
    nc = s // B_CHUNK
    ck = lambda t: t.reshape(nc, B_CHUNK, B_HEADS, -1)
    q, k, lf = ck(qh), ck(kh), ck(log_f)
    b = jnp.cumsum(lf, axis=1)
    b_last = b[:, -1:]
    hm = lambda t: jnp.transpose(t.reshape(s, B_HEADS, -1), (1, 0, 2)).astype(ACT_DTYPE)
    q_dec, k_inv, k_end = hm(q * jnp.exp(b)), hm(k * jnp.exp(-b)), hm(k * jnp.exp(b_last - b))
    dec = jnp.transpose(jnp.exp(b_last)[:, 0], (1, 0, 2))
    v = jnp.transpose(vh, (1, 0, 2)).astype(ACT_DTYPE)
    o = hgrn_intra(q_dec, k_inv, v) + hgrn_inter(q_dec, k_end, v, dec)
    return jnp.transpose(o, (1, 0, 2))


def _hgrn2(q, f_fwd, f_bwd, i, g, lb_fwd, lb_bwd, g_out):
    s = q.shape[0]
    heads = lambda t, d: t.astype(F32).reshape(s, B_HEADS, d)

    def gates(z, lb):
        lb = lb.astype(F32).reshape(B_HEADS, B_DK)
        zh = heads(z, B_DK)
        f = lb + (1.0 - lb) * jax.nn.sigmoid(zh)
        return jnp.log(jnp.maximum(f, TINY)), (1.0 - lb) * jax.nn.sigmoid(-zh)

    qh, vh = heads(q, B_DK), heads(i, B_DV)
    lf_f, k_f = gates(f_fwd, lb_fwd)
    lf_b, k_b = gates(f_bwd, lb_bwd)
    o_f = _gated_scan(qh, k_f, vh, lf_f)
    flip = lambda t: jnp.flip(t, axis=0)
    o_b = flip(_gated_scan(flip(qh), flip(k_b), flip(vh), flip(lf_b)))
    o = rmsnorm((o_f + o_b).reshape(s * B_HEADS, B_DV), g_out, F32, "rms_hg").reshape(s, B_HEADS, B_DV)
    return (o * jax.nn.silu(heads(g, B_DV))).reshape(s, B_HEADS * B_DV)


def _cross(h, mem_n, wq, wkv, wo):
    q = _heads(linear(h, wq, name="x_wq"), X_HEADS, X_DH)
    kv = linear(mem_n, wkv, name="x_wkv").reshape(mem_n.shape[0], 2, X_HEADS, X_DH)
    k = jnp.transpose(kv[:, 0], (1, 0, 2)).astype(ACT_DTYPE)
    v = jnp.transpose(kv[:, 1], (1, 0, 2)).astype(ACT_DTYPE)
    o = attention(q, k, v, X_DH ** -0.5, CROSS_TQ, 256, "cross")
    return linear(_unheads(o), wo, name="x_wo")


def _pad_w_in(w):
    cut = A_Q_RANK + A_KV_RANK + A_ROPE
    return jnp.concatenate([w[:, :cut], jnp.zeros((w.shape[0], KR_PAD), w.dtype), w[:, cut:]], axis=1)


def _model_loss(p, x, mem, target):
    s = x.shape[0]
    cos, sin = _rope_tables(s)
    sm = jax.nn.softmax(p["b_lb"].astype(F32), axis=1)
    lower_bounds = jnp.cumsum(sm, axis=1) - sm[:, :1]
    bias = _window_bias(p["rel_bias"])
    for l in range(DEPTH):
        h = rmsnorm(x, p["g_mix"][l], ACT_DTYPE, "rms_mix")
        z = linear(h, _pad_w_in(p["w_in"][l]), name="w_in")
        parts, start = [], 0
        for width in IN_SPLITS_PADDED:
            parts.append(z[:, start:start + width])
            start += width
        a_cq, a_ckv, a_kr, b_q, b_ff, b_fb, b_i, b_g, c_q, c_k, c_v, gate_a, gate_b, gate_c = parts
        y_a = _mla(a_cq, a_ckv, a_kr[:, :A_ROPE], p["a_gq"][l], p["a_gkv"][l], p["a_wuq"][l], p["a_wukv"][l], cos, sin)
        y_b = _hgrn2(b_q, b_ff, b_fb, b_i, b_g, lower_bounds[0, l], lower_bounds[1, l], p["b_gout"][l])
        y_c = _unheads(window_attention(_heads(c_q, C_HEADS, C_DH), _heads(c_k, C_KV_HEADS, C_DH),
                                        _heads(c_v, C_KV_HEADS, C_DH), bias, p["c_sink"][l]))
        merged = (jax.nn.sigmoid(gate_a) * linear(y_a, p["w_br_a"][l], name="w_br_a")
                  + jax.nn.sigmoid(gate_b) * linear(y_b, p["w_br_b"][l], name="w_br_b")
                  + jax.nn.sigmoid(gate_c) * linear(y_c, p["w_br_c"][l], name="w_br_c"))
        x = x + linear(merged, p["w_out"][l], name="w_out")
        h = rmsnorm(x, p["g_x"][l], ACT_DTYPE, "rms_x")
        x = x + _cross(h, rmsnorm(mem, p["g_mem"][l], ACT_DTYPE, "rms_mem"), p["x_wq"][l], p["x_wkv"][l], p["x_wo"][l])
        h = rmsnorm(x, p["g_ffn"][l], ACT_DTYPE, "rms_ffn")
        t = jax.nn.silu(linear(h, p["f_w1"][l], name="f_w1")) * linear(h, p["f_w3"][l], name="f_w3")
        x = x + linear(t, p["f_w2"][l], name="f_w2")
    y = rmsnorm(x, p["g_final"], F32, "rms_final")
    return loss_head(y, target)


def kernel(x, mem, w_in, g_mix, a_gq, a_gkv, a_wuq, a_wukv, b_lb, b_gout, c_sink, rel_bias, w_br_a, w_br_b, w_br_c, w_out, g_x, g_mem, x_wq, x_wkv, x_wo, g_ffn, f_w1, f_w3, f_w2, g_final, loss_target, m_w_in, m_g_mix, m_a_gq, m_a_gkv, m_a_wuq, m_a_wukv, m_b_lb, m_b_gout, m_c_sink, m_rel_bias, m_w_br_a, m_w_br_b, m_w_br_c, m_w_out, m_g_x, m_g_mem, m_x_wq, m_x_wkv, m_x_wo, m_g_ffn, m_f_w1, m_f_w3, m_f_w2, m_g_final, v_w_in, v_g_mix, v_a_gq, v_a_gkv, v_a_wuq, v_a_wukv, v_b_lb, v_b_gout, v_c_sink, v_rel_bias, v_w_br_a, v_w_br_b, v_w_br_c, v_w_out, v_g_x, v_g_mem, v_x_wq, v_x_wkv, v_x_wo, v_g_ffn, v_f_w1, v_f_w3, v_f_w2, v_g_final):
    given = dict(locals())
    w = {n: given[n] for n in WEIGHT_ORDER}
    m = {n: given["m_" + n] for n in WEIGHT_ORDER}
    v = {n: given["v_" + n] for n in WEIGHT_ORDER}
    sh_names = [n for n, _ in SHARDED]
    sh_shapes = [w[n].shape for n in sh_names]
    sh_rows = _pack_rows([int(np.prod(s)) for s in sh_shapes])
    rep_shapes = [w[n].shape for n in REPLICATED] + [(1,)]
    rep_rows = _pack_rows([int(np.prod(s)) for s in rep_shapes])

    gathered = _unpack(all_gather_rows(_pack([w[n] for n in sh_names], sh_rows)), sh_shapes)
    full = {n: _shards_to_full(t, ax) for (n, ax), t in zip(SHARDED, gathered)}
    full.update({n: w[n] for n in REPLICATED})

    loss, (grad_full, grad_x) = jax.value_and_grad(_model_loss, argnums=(0, 1))(full, x[0], mem[0], loss_target[0])

    per_dev = [_full_to_shards(grad_full[n], ax) for n, ax in SHARDED]
    flat = []
    for t in per_dev:
        t = t.reshape(N_DEV, -1)
        flat.append(jnp.pad(t, ((0, 0), (0, _padded(t.shape[1]) - t.shape[1]))))
    packed = jnp.concatenate(flat, axis=1)
    packed = jnp.pad(packed, ((0, 0), (0, sh_rows * LANES - packed.shape[1]))).reshape(N_DEV, sh_rows, LANES)
    received = all_to_all_rows(packed)
    outs = adamw_rows(received, _pack([w[n] for n in sh_names], sh_rows), _pack([m[n] for n in sh_names], sh_rows),
                      _pack([v[n] for n in sh_names], sh_rows), "adamw_sharded")
    g_sh, d_sh, nm_sh, nv_sh = [dict(zip(sh_names, _unpack(o, sh_shapes))) for o in outs]

    mine = _pack([grad_full[n] for n in REPLICATED] + [loss.reshape(1)], rep_rows)
    everyone = all_gather_small(mine)
    rep_w = [w[n] for n in REPLICATED] + [jnp.zeros((1,), F32)]
    outs = adamw_rows(everyone, _pack(rep_w, rep_rows), _pack([m[n] for n in REPLICATED] + [jnp.zeros((1,), F32)], rep_rows),
                      _pack([v[n] for n in REPLICATED] + [jnp.ones((1,), F32)], rep_rows), "adamw_replicated")
    rep_names = list(REPLICATED) + ["loss"]
    g_rp, d_rp, nm_rp, nv_rp = [dict(zip(rep_names, _unpack(o, rep_shapes))) for o in outs]

    def pick(sharded, replicated, n):
        return sharded[n] if n in sharded else replicated[n]

    return (g_rp["loss"].reshape(()), grad_x[None],
            *[pick(g_sh, g_rp, n) for n in WEIGHT_ORDER], *[pick(d_sh, d_rp, n) for n in WEIGHT_ORDER],
            *[pick(nm_sh, nm_rp, n) for n in WEIGHT_ORDER], *[pick(nv_sh, nv_rp, n) for n in WEIGHT_ORDER])
```

```python
import functools
import math

import jax
import jax.numpy as jnp
import numpy as np
from jax import lax
from jax.experimental import pallas as pl
from jax.experimental.pallas import tpu as pltpu

F32 = jnp.float32
MXU_DTYPE = jnp.bfloat16
ACT_DTYPE = jnp.bfloat16

V7X_VMEM_LIMIT_BYTES = 56 * 1024 * 1024
LANES = 128
SUBLANES = 8

N_DEV = 8
D_MODEL = 1024
DEPTH = 2
EPS = 1e-6
TINY = 1e-30
NEG = -1e30

A_HEADS, A_NOPE, A_ROPE, A_V, A_Q_RANK, A_KV_RANK = 8, 64, 32, 64, 384, 256
ROPE_THETA = 10000.0
B_HEADS, B_DK, B_DV, B_CHUNK = 8, 128, 64, 16
C_HEADS, C_KV_HEADS, C_DH, C_WINDOW, C_BLOCK = 8, 2, 64, 128, 128
REL_BUCKETS, REL_MAX_DIST = 32, 128
X_HEADS, X_DH = 4, 256
D_FF = 2816
IN_SPLITS = (A_Q_RANK, A_KV_RANK, A_ROPE, 1024, 1024, 1024, 512, 512, 512, 128, 128, 1024, 1024, 1024)
IN_WIDTH = sum(IN_SPLITS)
KR_PAD = LANES - A_ROPE
IN_SPLITS_PADDED = (A_Q_RANK, A_KV_RANK, LANES, 1024, 1024, 1024, 512, 512, 512, 128, 128, 1024, 1024, 1024)

ADAM_LR, ADAM_B1, ADAM_B2, ADAM_EPS, ADAM_WD, ADAM_STEP = 0.001, 0.9, 0.999, 1e-08, 0.01, 10

SHARDED = (("w_in", 2), ("a_wuq", 2), ("a_wukv", 2), ("b_lb", 2), ("w_br_a", 2), ("w_br_b", 2), ("w_br_c", 2),
           ("w_out", 1), ("x_wq", 1), ("x_wkv", 2), ("x_wo", 1), ("f_w1", 2), ("f_w3", 2), ("f_w2", 1))
REPLICATED = ("g_mix", "a_gq", "a_gkv", "b_gout", "c_sink", "rel_bias", "g_x", "g_mem", "g_ffn", "g_final")
WEIGHT_ORDER = ("w_in", "g_mix", "a_gq", "a_gkv", "a_wuq", "a_wukv", "b_lb", "b_gout", "c_sink", "rel_bias", "w_br_a",
                "w_br_b", "w_br_c", "w_out", "g_x", "g_mem", "x_wq", "x_wkv", "x_wo", "g_ffn", "f_w1", "f_w3", "f_w2",
                "g_final")
PACK_QUANTUM = SUBLANES * LANES
PACK_ROW_TILE = 512


def _pallas(body, **kw):
    return pl.pallas_call(body, **kw)


def _cparams(*sem):
    return pltpu.CompilerParams(dimension_semantics=sem, vmem_limit_bytes=V7X_VMEM_LIMIT_BYTES)


def _tile(n, target, mult=LANES):
    t = (min(target, n) // mult) * mult
    while t >= mult:
        if n % t == 0:
            return t
        t -= mult
    return n


def _dot(a, b, dims):
    return lax.dot_general(a.astype(MXU_DTYPE), b.astype(MXU_DTYPE), (dims, ((), ())), preferred_element_type=F32)


NN = ((1,), (0,))
NT = ((1,), (1,))
TN = ((0,), (0,))


def _mm(a, b, ta=False, tb=False, out_dtype=F32, name="mm"):
    m, k = (a.shape[1], a.shape[0]) if ta else a.shape
    kb, n = (b.shape[1], b.shape[0]) if tb else b.shape
    assert k == kb, (a.shape, b.shape, ta, tb)
    tm, tn, tk = _tile(m, 512), _tile(n, 512), _tile(k, 1024)
    nk = k // tk
    dims = ((0 if ta else 1,), (1 if tb else 0,))

    def body(a_ref, b_ref, o_ref, acc_ref):
        kk = pl.program_id(2)

        @pl.when(kk == 0)
        def _():
            acc_ref[...] = jnp.zeros_like(acc_ref)

        acc_ref[...] += _dot(a_ref[...], b_ref[...], dims)

        @pl.when(kk == nk - 1)
        def _():
            o_ref[...] = acc_ref[...].astype(o_ref.dtype)

    a_spec = pl.BlockSpec((tk, tm), lambda i, j, kk: (kk, i)) if ta else pl.BlockSpec((tm, tk), lambda i, j, kk: (i, kk))
    b_spec = pl.BlockSpec((tn, tk), lambda i, j, kk: (j, kk)) if tb else pl.BlockSpec((tk, tn), lambda i, j, kk: (kk, j))
    return _pallas(
        body, name=name, grid=(m // tm, n // tn, nk), in_specs=[a_spec, b_spec],
        out_specs=pl.BlockSpec((tm, tn), lambda i, j, kk: (i, j)), out_shape=jax.ShapeDtypeStruct((m, n), out_dtype),
        scratch_shapes=[pltpu.VMEM((tm, tn), F32)], compiler_params=_cparams("parallel", "parallel", "arbitrary"),
    )(a, b)


def linear(a, w, out_dtype=F32, name="lin"):
    @jax.custom_vjp
    def f(a, w):
        return _mm(a, w.astype(MXU_DTYPE), out_dtype=out_dtype, name=name + "_fwd")

    def fwd(a, w):
        wb = w.astype(MXU_DTYPE)
        return _mm(a, wb, out_dtype=out_dtype, name=name + "_fwd"), (a, wb)

    def bwd(res, g):
        a, wb = res
        da = _mm(g, wb, tb=True, out_dtype=a.dtype, name=name + "_dx")
        dw = _mm(a, g, ta=True, out_dtype=F32, name=name + "_dw")
        return da, dw

    f.defvjp(fwd, bwd)
    return f(a, w)


def _row_tile(rows, width):
    return _tile(rows, max(SUBLANES, (512 * 1024) // width), 16)


def rmsnorm(x, g, out_dtype=F32, name="rms"):
    rows, d = x.shape
    tr = _row_tile(rows, d)
    n_steps = rows // tr

    def fwd_body(x_ref, g_ref, o_ref):
        xv = x_ref[...].astype(F32)
        r = lax.rsqrt(jnp.mean(xv * xv, axis=-1, keepdims=True) + EPS)
        o_ref[...] = (xv * r * g_ref[...]).astype(o_ref.dtype)

    def bwd_body(x_ref, g_ref, dy_ref, dx_ref, dg_ref):
        xv = x_ref[...].astype(F32)
        dy = dy_ref[...].astype(F32)
        r = lax.rsqrt(jnp.mean(xv * xv, axis=-1, keepdims=True) + EPS)
        xh = xv * r
        dxh = dy * g_ref[...]
        dx_ref[...] = (r * (dxh - xh * jnp.mean(dxh * xh, axis=-1, keepdims=True))).astype(dx_ref.dtype)

        @pl.when(pl.program_id(0) == 0)
        def _():
            dg_ref[...] = jnp.zeros_like(dg_ref)

        dg_ref[...] += jnp.sum(dy * xh, axis=0, keepdims=True)

    row_spec = pl.BlockSpec((tr, d), lambda i: (i, 0))
    vec_spec = pl.BlockSpec((1, d), lambda i: (0, 0))

    def run_fwd(x, g):
        return _pallas(fwd_body, name=name + "_fwd", grid=(n_steps,), in_specs=[row_spec, vec_spec], out_specs=row_spec,
                       out_shape=jax.ShapeDtypeStruct((rows, d), out_dtype), compiler_params=_cparams("parallel"))(
            x, g.reshape(1, d).astype(F32))

    @jax.custom_vjp
    def f(x, g):
        return run_fwd(x, g)

    def fwd(x, g):
        return run_fwd(x, g), (x, g)

    def bwd(res, dy):
        x, g = res
        dx, dg = _pallas(
            bwd_body, name=name + "_bwd", grid=(n_steps,), in_specs=[row_spec, vec_spec, row_spec],
            out_specs=[row_spec, vec_spec],
            out_shape=[jax.ShapeDtypeStruct((rows, d), x.dtype), jax.ShapeDtypeStruct((1, d), F32)],
            compiler_params=_cparams("arbitrary"))(x, g.reshape(1, d).astype(F32), dy)
        return dx, dg.reshape(g.shape).astype(g.dtype)

    f.defvjp(fwd, bwd)
    return f(x, g)


def _rowdot(a, b, name):
    h, s, d = a.shape
    ts = _tile(s, 2048)

    def body(a_ref, b_ref, o_ref):
        o_ref[...] = jnp.sum(a_ref[...].astype(F32) * b_ref[...].astype(F32), axis=-1, keepdims=True)

    spec = pl.BlockSpec((None, ts, d), lambda hh, i: (hh, i, 0))
    return _pallas(body, name=name, grid=(h, s // ts), in_specs=[spec, spec],
                   out_specs=pl.BlockSpec((None, ts, 1), lambda hh, i: (hh, i, 0)),
                   out_shape=jax.ShapeDtypeStruct((h, s, 1), F32), compiler_params=_cparams("parallel", "parallel"))(a, b)


def attention(q, k, v, scale, tq, tk, name):
    h, sq, d = q.shape
    sk, dv = k.shape[1], v.shape[2]
    tq, tk = _tile(sq, tq), _tile(sk, tk)
    nq, nk = sq // tq, sk // tk

    def fwd_body(q_ref, k_ref, v_ref, o_ref, lse_ref, m_ref, l_ref, acc_ref):
        j = pl.program_id(2)

        @pl.when(j == 0)
        def _():
            m_ref[...] = jnp.full_like(m_ref, NEG)
            l_ref[...] = jnp.zeros_like(l_ref)
            acc_ref[...] = jnp.zeros_like(acc_ref)

        s = _dot(q_ref[...], k_ref[...], NT) * scale
        m_prev = m_ref[...]
        m_new = jnp.maximum(m_prev, jnp.max(s, axis=-1, keepdims=True))
        alpha = jnp.exp(m_prev - m_new)
        p = jnp.exp(s - m_new)
        l_ref[...] = alpha * l_ref[...] + jnp.sum(p, axis=-1, keepdims=True)
        acc_ref[...] = alpha * acc_ref[...] + _dot(p, v_ref[...], NN)
        m_ref[...] = m_new

        @pl.when(j == nk - 1)
        def _():
            o_ref[...] = (acc_ref[...] / l_ref[...]).astype(o_ref.dtype)
            lse_ref[...] = m_ref[...] + jnp.log(l_ref[...])

    def dq_body(q_ref, k_ref, v_ref, do_ref, lse_ref, dl_ref, dq_ref, acc_ref):
        j = pl.program_id(2)

        @pl.when(j == 0)
        def _():
            acc_ref[...] = jnp.zeros_like(acc_ref)

        s = _dot(q_ref[...], k_ref[...], NT) * scale
        p = jnp.exp(s - lse_ref[...])
        dp = _dot(do_ref[...], v_ref[...], NT)
        ds = p * (dp - dl_ref[...]) * scale
        acc_ref[...] += _dot(ds, k_ref[...], NN)

        @pl.when(j == nk - 1)
        def _():
            dq_ref[...] = acc_ref[...].astype(dq_ref.dtype)

    def dkv_body(q_ref, k_ref, v_ref, do_ref, lse_ref, dl_ref, dk_ref, dv_ref, dk_acc, dv_acc):
        i = pl.program_id(2)

        @pl.when(i == 0)
        def _():
            dk_acc[...] = jnp.zeros_like(dk_acc)
            dv_acc[...] = jnp.zeros_like(dv_acc)

        s = _dot(q_ref[...], k_ref[...], NT) * scale
        p = jnp.exp(s - lse_ref[...])
        dv_acc[...] += _dot(p, do_ref[...], TN)
        dp = _dot(do_ref[...], v_ref[...], NT)
        ds = p * (dp - dl_ref[...]) * scale
        dk_acc[...] += _dot(ds, q_ref[...], TN)

        @pl.when(i == nq - 1)
        def _():
            dk_ref[...] = dk_acc[...].astype(dk_ref.dtype)
            dv_ref[...] = dv_acc[...].astype(dv_ref.dtype)

    def q_spec(width):
        return pl.BlockSpec((None, tq, width), lambda hh, i, j: (hh, i, 0))

    def k_spec(width):
        return pl.BlockSpec((None, tk, width), lambda hh, i, j: (hh, j, 0))

    def run_fwd(q, k, v):
        return _pallas(
            fwd_body, name=name + "_fwd", grid=(h, nq, nk), in_specs=[q_spec(d), k_spec(d), k_spec(dv)],
            out_specs=[q_spec(dv), q_spec(1)],
            out_shape=[jax.ShapeDtypeStruct((h, sq, dv), q.dtype), jax.ShapeDtypeStruct((h, sq, 1), F32)],
            scratch_shapes=[pltpu.VMEM((tq, 1), F32), pltpu.VMEM((tq, 1), F32), pltpu.VMEM((tq, dv), F32)],
            compiler_params=_cparams("parallel", "parallel", "arbitrary"))(q, k, v)

    @jax.custom_vjp
    def f(q, k, v):
        return run_fwd(q, k, v)[0]

    def fwd(q, k, v):
        o, lse = run_fwd(q, k, v)
        return o, (q, k, v, o, lse)

    def bwd(res, do):
        q, k, v, o, lse = res
        delta = _rowdot(o, do, name + "_delta")
        dq = _pallas(
            dq_body, name=name + "_dq", grid=(h, nq, nk),
            in_specs=[q_spec(d), k_spec(d), k_spec(dv), q_spec(dv), q_spec(1), q_spec(1)], out_specs=q_spec(d),
            out_shape=jax.ShapeDtypeStruct((h, sq, d), q.dtype), scratch_shapes=[pltpu.VMEM((tq, d), F32)],
            compiler_params=_cparams("parallel", "parallel", "arbitrary"))(q, k, v, do, lse, delta)

        def qs(width):
            return pl.BlockSpec((None, tq, width), lambda hh, j, i: (hh, i, 0))

        def ks(width):
            return pl.BlockSpec((None, tk, width), lambda hh, j, i: (hh, j, 0))

        dk, dv_ = _pallas(
            dkv_body, name=name + "_dkv", grid=(h, nk, nq),
            in_specs=[qs(d), ks(d), ks(dv), qs(dv), qs(1), qs(1)], out_specs=[ks(d), ks(dv)],
            out_shape=[jax.ShapeDtypeStruct((h, sk, d), k.dtype), jax.ShapeDtypeStruct((h, sk, dv), v.dtype)],
            scratch_shapes=[pltpu.VMEM((tk, d), F32), pltpu.VMEM((tk, dv), F32)],
            compiler_params=_cparams("parallel", "parallel", "arbitrary"))(q, k, v, do, lse, delta)
        return dq, dk, dv_

    f.defvjp(fwd, bwd)
    return f(q, k, v)


def window_attention(q, k, v, bias, sink, name="wattn"):
    hq, s, dh = q.shape
    g = hq // C_KV_HEADS
    blk = C_BLOCK
    nb = s // blk
    scale = dh ** -0.5
    sink_b = jnp.broadcast_to(sink.astype(F32).reshape(hq, 1, 1), (hq, 1, LANES))

    def key_block(i, j):
        return jnp.clip(i - 1 + j, 0, nb - 1)

    def in_range(i, j):
        kb = i - 1 + j
        return jnp.logical_and(kb >= 0, kb < nb)

    def scores(q_ref, k_ref, b_ref, hh):
        return _dot(q_ref[hh], k_ref[...], NT) * scale + b_ref[hh]

    def fwd_body(q_ref, k_ref, v_ref, b_ref, sk_ref, o_ref, lse_ref, m_ref, l_ref, acc_ref):
        i, j = pl.program_id(1), pl.program_id(2)

        @pl.when(j == 0)
        def _():
            for hh in range(g):
                m_ref[hh] = jnp.broadcast_to(sk_ref[hh][:, :1], (blk, 1))
            l_ref[...] = jnp.ones_like(l_ref)
            acc_ref[...] = jnp.zeros_like(acc_ref)

        @pl.when(in_range(i, j))
        def _():
            for hh in range(g):
                sc = scores(q_ref, k_ref, b_ref, hh)
                m_prev = m_ref[hh]
                m_new = jnp.maximum(m_prev, jnp.max(sc, axis=-1, keepdims=True))
                alpha = jnp.exp(m_prev - m_new)
                p = jnp.exp(sc - m_new)
                l_ref[hh] = alpha * l_ref[hh] + jnp.sum(p, axis=-1, keepdims=True)
                acc_ref[hh] = alpha * acc_ref[hh] + _dot(p, v_ref[...], NN)
                m_ref[hh] = m_new

        @pl.when(j == 2)
        def _():
            o_ref[...] = (acc_ref[...] / l_ref[...]).astype(o_ref.dtype)
            lse_ref[...] = m_ref[...] + jnp.log(l_ref[...])

    def dq_body(q_ref, k_ref, v_ref, b_ref, sk_ref, do_ref, lse_ref, dl_ref, dq_ref, db_ref, dsink_ref, acc_ref):
        i, j = pl.program_id(1), pl.program_id(2)

        @pl.when(jnp.logical_and(i == 0, j == 0))
        def _():
            db_ref[...] = jnp.zeros_like(db_ref)
            dsink_ref[...] = jnp.zeros_like(dsink_ref)

        @pl.when(j == 0)
        def _():
            acc_ref[...] = jnp.zeros_like(acc_ref)
            for hh in range(g):
                p_sink = jnp.exp(sk_ref[hh][:, :1] - lse_ref[hh])
                total = jnp.broadcast_to(-jnp.sum(p_sink * dl_ref[hh], axis=0, keepdims=True), (1, LANES))
                dsink_ref[hh] += jnp.where(lax.broadcasted_iota(jnp.int32, (1, LANES), 1) == 0, total, 0.0)

        @pl.when(in_range(i, j))
        def _():
            for hh in range(g):
                p = jnp.exp(scores(q_ref, k_ref, b_ref, hh) - lse_ref[hh])
                dp = _dot(do_ref[hh], v_ref[...], NT)
                ds = p * (dp - dl_ref[hh])
                db_ref[hh, j] += ds
                acc_ref[hh] += _dot(ds * scale, k_ref[...], NN)

        @pl.when(j == 2)
        def _():
            dq_ref[...] = acc_ref[...].astype(dq_ref.dtype)

    def dkv_body(q_ref, k_ref, v_ref, b_ref, do_ref, lse_ref, dl_ref, dk_ref, dv_ref, dk_acc, dv_acc):
        kb, jj = pl.program_id(1), pl.program_id(2)
        qi = kb + 1 - jj

        @pl.when(jj == 0)
        def _():
            dk_acc[...] = jnp.zeros_like(dk_acc)
            dv_acc[...] = jnp.zeros_like(dv_acc)

        @pl.when(jnp.logical_and(qi >= 0, qi < nb))
        def _():
            for hh in range(g):
                p = jnp.exp(scores(q_ref, k_ref, b_ref, hh) - lse_ref[hh])
                dv_acc[...] += _dot(p, do_ref[hh], TN)
                dp = _dot(do_ref[hh], v_ref[...], NT)
                ds = p * (dp - dl_ref[hh]) * scale
                dk_acc[...] += _dot(ds, q_ref[hh], TN)

        @pl.when(jj == 2)
        def _():
            dk_ref[...] = dk_acc[...].astype(dk_ref.dtype)
            dv_ref[...] = dv_acc[...].astype(dv_ref.dtype)

    def q_spec(width):
        return pl.BlockSpec((g, blk, width), lambda kv, i, j: (kv, i, 0))

    kv_spec = pl.BlockSpec((None, blk, dh), lambda kv, i, j: (kv, key_block(i, j), 0))
    b_spec = pl.BlockSpec((g, None, blk, blk), lambda kv, i, j: (kv, j, 0, 0))
    sk_spec = pl.BlockSpec((g, 1, LANES), lambda kv, i, j: (kv, 0, 0))

    def run_fwd(q, k, v, bias, sink_b):
        return _pallas(
            fwd_body, name=name + "_fwd", grid=(C_KV_HEADS, nb, 3), in_specs=[q_spec(dh), kv_spec, kv_spec, b_spec, sk_spec],
            out_specs=[q_spec(dh), q_spec(1)],
            out_shape=[jax.ShapeDtypeStruct((hq, s, dh), q.dtype), jax.ShapeDtypeStruct((hq, s, 1), F32)],
            scratch_shapes=[pltpu.VMEM((g, blk, 1), F32), pltpu.VMEM((g, blk, 1), F32), pltpu.VMEM((g, blk, dh), F32)],
            compiler_params=_cparams("parallel", "parallel", "arbitrary"))(q, k, v, bias, sink_b)

    @jax.custom_vjp
    def f(q, k, v, bias, sink_b):
        return run_fwd(q, k, v, bias, sink_b)[0]

    def fwd(q, k, v, bias, sink_b):
        o, lse = run_fwd(q, k, v, bias, sink_b)
        return o, (q, k, v, bias, sink_b, o, lse)

    def bwd(res, do):
        q, k, v, bias, sink_b, o, lse = res
        delta = _rowdot(o, do, name + "_delta")
        dq, dbias, dsink = _pallas(
            dq_body, name=name + "_dq", grid=(C_KV_HEADS, nb, 3),
            in_specs=[q_spec(dh), kv_spec, kv_spec, b_spec, sk_spec, q_spec(dh), q_spec(1), q_spec(1)],
            out_specs=[q_spec(dh), pl.BlockSpec((g, 3, blk, blk), lambda kv, i, j: (kv, 0, 0, 0)), sk_spec],
            out_shape=[jax.ShapeDtypeStruct((hq, s, dh), q.dtype), jax.ShapeDtypeStruct((hq, 3, blk, blk), F32),
                       jax.ShapeDtypeStruct((hq, 1, LANES), F32)],
            scratch_shapes=[pltpu.VMEM((g, blk, dh), F32)],
            compiler_params=_cparams("arbitrary", "arbitrary", "arbitrary"))(q, k, v, bias, sink_b, do, lse, delta)

        def qs(width):
            return pl.BlockSpec((g, blk, width), lambda kv, kb, jj: (kv, jnp.clip(kb + 1 - jj, 0, nb - 1), 0))

        ks = pl.BlockSpec((None, blk, dh), lambda kv, kb, jj: (kv, kb, 0))
        bs = pl.BlockSpec((g, None, blk, blk), lambda kv, kb, jj: (kv, jj, 0, 0))
        dk, dv_ = _pallas(
            dkv_body, name=name + "_dkv", grid=(C_KV_HEADS, nb, 3),
            in_specs=[qs(dh), ks, ks, bs, qs(dh), qs(1), qs(1)], out_specs=[ks, ks],
            out_shape=[jax.ShapeDtypeStruct(k.shape, k.dtype), jax.ShapeDtypeStruct(v.shape, v.dtype)],
            scratch_shapes=[pltpu.VMEM((blk, dh), F32), pltpu.VMEM((blk, dh), F32)],
            compiler_params=_cparams("parallel", "parallel", "arbitrary"))(q, k, v, bias, do, lse, delta)
        return dq, dk, dv_, dbias, dsink

    f.defvjp(fwd, bwd)
    return f(q, k, v, bias, sink_b)


HG_PREP_ROWS = 256
HG_PAIR = 2
HG_INTRA_BLOCK = 256
HG_INTER_CHUNKS = 16
MLA_TQ, MLA_TK = 512, 1024
CROSS_TQ = 1024


def _hdot(a, b, dims):
    return lax.dot_general(a, b, (dims, ((), ())), precision=lax.Precision.HIGHEST, preferred_element_type=F32)


def hgrn_prep(q, z, lb, reverse, name):
    s, c = q.shape
    tb = _tile(s, HG_PREP_ROWS)
    tc = HG_PAIR * B_DK
    ncb = tb // B_CHUNK

    def chunk_matrices():
        r = lax.broadcasted_iota(jnp.int32, (tb, tb), 0)
        cc = lax.broadcasted_iota(jnp.int32, (tb, tb), 1)
        same = r // B_CHUNK == cc // B_CHUNK
        tri = (cc >= r) if reverse else (cc <= r)
        cum = jnp.where(jnp.logical_and(same, tri), 1.0, 0.0).astype(F32)
        every = jnp.where(same, 1.0, 0.0).astype(F32)
        pr = lax.broadcasted_iota(jnp.int32, (ncb, tb), 0)
        pc = lax.broadcasted_iota(jnp.int32, (ncb, tb), 1)
        per_chunk = jnp.where(pc // B_CHUNK == pr, 1.0, 0.0).astype(F32)
        return cum, every, per_chunk

    def gates(zv, lbv):
        e = jnp.exp(-jnp.abs(zv))
        big, small = 1.0 / (1.0 + e), e / (1.0 + e)
        sig = jnp.where(zv >= 0, big, small)
        nsig = jnp.where(zv >= 0, small, big)
        f = lbv + (1.0 - lbv) * sig
        return sig, nsig, f, jnp.log(jnp.maximum(f, TINY)), (1.0 - lbv) * nsig

    def fwd_body(q_ref, z_ref, lb_ref, qd_ref, ki_ref, ke_ref, dec_ref):
        cum, every, per_chunk = chunk_matrices()
        _, _, _, lf, key = gates(z_ref[...], lb_ref[...])
        b = _hdot(cum, lf, NN)
        tot = _hdot(every, lf, NN)
        qd_ref[...] = q_ref[...] * jnp.exp(b)
        ki_ref[...] = key * jnp.exp(-b)
        ke_ref[...] = key * jnp.exp(tot - b)
        dec_ref[...] = jnp.exp(_hdot(per_chunk, lf, NN))

    def bwd_body(q_ref, z_ref, lb_ref, dqd_ref, dki_ref, dke_ref, ddec_ref, dq_ref, dz_ref, dlb_ref):
        cum, every, per_chunk = chunk_matrices()
        lbv = lb_ref[...]
        sig, nsig, f, lf, key = gates(z_ref[...], lbv)
        b = _hdot(cum, lf, NN)
        tot = _hdot(every, lf, NN)
        e_b, e_nb, e_tb = jnp.exp(b), jnp.exp(-b), jnp.exp(tot - b)
        dqd, dki, dke = dqd_ref[...], dki_ref[...], dke_ref[...]
        dq_ref[...] = dqd * e_b
        dkey = dki * e_nb + dke * e_tb
        t_end = dke * key * e_tb
        db = dqd * q_ref[...] * e_b - dki * key * e_nb - t_end
        dtot = ddec_ref[...] * jnp.exp(_hdot(per_chunk, lf, NN)) + _hdot(per_chunk, t_end, NN)
        dlf = _hdot(cum, db, TN) + _hdot(per_chunk, dtot, TN)
        df = jnp.where(f > TINY, dlf / f, 0.0)
        one_m_lb = 1.0 - lbv
        dz_ref[...] = (df - dkey) * one_m_lb * sig * nsig
        dlb_part = jnp.sum(df * nsig - dkey * nsig, axis=0, keepdims=True)

        @pl.when(pl.program_id(1) == 0)
        def _():
            dlb_ref[...] = jnp.zeros_like(dlb_ref)

        dlb_ref[...] += dlb_part

    tok = pl.BlockSpec((tb, tc), lambda j, i: (i, j))
    vec = pl.BlockSpec((1, tc), lambda j, i: (0, j))
    chk = pl.BlockSpec((ncb, tc), lambda j, i: (i, j))
    grid = (c // tc, s // tb)
    tok_shape = jax.ShapeDtypeStruct((s, c), F32)
    chk_shape = jax.ShapeDtypeStruct((s // B_CHUNK, c), F32)

    def run_fwd(q, z, lb):
        return _pallas(fwd_body, name=name + "_fwd", grid=grid, in_specs=[tok, tok, vec], out_specs=[tok, tok, tok, chk],
                       out_shape=[tok_shape, tok_shape, tok_shape, chk_shape],
                       compiler_params=_cparams("parallel", "parallel"))(q, z, lb)

    @jax.custom_vjp
    def f(q, z, lb):
        return tuple(run_fwd(q, z, lb))

    def fwd(q, z, lb):
        return tuple(run_fwd(q, z, lb)), (q, z, lb)

    def bwd(res, cts):
        q, z, lb = res
        dq, dz, dlb = _pallas(
            bwd_body, name=name + "_bwd", grid=grid, in_specs=[tok, tok, vec, tok, tok, tok, chk], out_specs=[tok, tok, vec],
            out_shape=[tok_shape, tok_shape, jax.ShapeDtypeStruct((1, c), F32)],
            compiler_params=_cparams("parallel", "arbitrary"))(q, z, lb, *cts)
        return dq, dz, dlb

    f.defvjp(fwd, bwd)
    return f(q, z, lb)


def _pair_cols(ref, hh, width):
    return ref[:, hh * width:(hh + 1) * width]


def hgrn_intra(qd, ki, v, reverse, name):
    s = qd.shape[0]
    tb = _tile(s, HG_INTRA_BLOCK)
    wk, wv = HG_PAIR * B_DK, HG_PAIR * B_DV

    def mask():
        r = lax.broadcasted_iota(jnp.int32, (tb, tb), 0)
        c = lax.broadcasted_iota(jnp.int32, (tb, tb), 1)
        return jnp.logical_and(r // B_CHUNK == c // B_CHUNK, (c >= r) if reverse else (c <= r))

    def fwd_body(q_ref, k_ref, v_ref, o_ref):
        msk = mask()
        for hh in range(HG_PAIR):
            sc = jnp.where(msk, _dot(_pair_cols(q_ref, hh, B_DK), _pair_cols(k_ref, hh, B_DK), NT), 0.0)
            o_ref[:, hh * B_DV:(hh + 1) * B_DV] = _dot(sc, _pair_cols(v_ref, hh, B_DV), NN)

    def bwd_body(q_ref, k_ref, v_ref, do_ref, dq_ref, dk_ref, dv_ref):
        msk = mask()
        for hh in range(HG_PAIR):
            q, k = _pair_cols(q_ref, hh, B_DK), _pair_cols(k_ref, hh, B_DK)
            vv, do = _pair_cols(v_ref, hh, B_DV), _pair_cols(do_ref, hh, B_DV)
            sc = jnp.where(msk, _dot(q, k, NT), 0.0)
            ds = jnp.where(msk, _dot(do, vv, NT), 0.0)
            dq_ref[:, hh * B_DK:(hh + 1) * B_DK] = _dot(ds, k, NN)
            dk_ref[:, hh * B_DK:(hh + 1) * B_DK] = _dot(ds, q, TN)
            dv_ref[:, hh * B_DV:(hh + 1) * B_DV] = _dot(sc, do, TN)

    ks = pl.BlockSpec((tb, wk), lambda hp, i: (i, hp))
    vs = pl.BlockSpec((tb, wv), lambda hp, i: (i, hp))
    grid = (B_HEADS // HG_PAIR, s // tb)

    def run_fwd(qd, ki, v):
        return _pallas(fwd_body, name=name + "_fwd", grid=grid, in_specs=[ks, ks, vs], out_specs=vs,
                       out_shape=jax.ShapeDtypeStruct(v.shape, F32), compiler_params=_cparams("parallel", "parallel"))(qd, ki, v)

    @jax.custom_vjp
    def f(qd, ki, v):
        return run_fwd(qd, ki, v)

    def fwd(qd, ki, v):
        return run_fwd(qd, ki, v), (qd, ki, v)

    def bwd(res, do):
        qd, ki, v = res
        return tuple(_pallas(
            bwd_body, name=name + "_bwd", grid=grid, in_specs=[ks, ks, vs, vs], out_specs=[ks, ks, vs],
            out_shape=[jax.ShapeDtypeStruct(qd.shape, F32), jax.ShapeDtypeStruct(ki.shape, F32),
                       jax.ShapeDtypeStruct(v.shape, F32)],
            compiler_params=_cparams("parallel", "parallel"))(qd, ki, v, do))

    f.defvjp(fwd, bwd)
    return f(qd, ki, v)


def hgrn_inter(qd, ke, v, dec, reverse, name):
    s = qd.shape[0]
    nc = s // B_CHUNK
    cpb = HG_INTER_CHUNKS if nc % HG_INTER_CHUNKS == 0 else nc
    tb = cpb * B_CHUNK
    nblk = nc // cpb
    wk, wv = HG_PAIR * B_DK, HG_PAIR * B_DV
    n_hp = B_HEADS // HG_PAIR

    def rows(c):
        return pl.ds(c * B_CHUNK, B_CHUNK)

    def kcols(hh):
        return slice(hh * B_DK, (hh + 1) * B_DK)

    def vcols(hh):
        return slice(hh * B_DV, (hh + 1) * B_DV)

    def order(flip):
        return reversed(range(cpb)) if flip else range(cpb)

    def fwd_body(q_ref, k_ref, v_ref, dec_ref, o_ref, st_ref, state):
        @pl.when(pl.program_id(1) == 0)
        def _():
            state[...] = jnp.zeros_like(state)

        for c in order(reverse):
            for hh in range(HG_PAIR):
                st = state[hh]
                st_ref[c, hh] = st
                o_ref[rows(c), vcols(hh)] = _dot(q_ref[rows(c), kcols(hh)], st, NT)
                state[hh] = st * dec_ref[pl.ds(c, 1), kcols(hh)] + _dot(v_ref[rows(c), vcols(hh)], k_ref[rows(c), kcols(hh)], TN)

    def bwd_body(q_ref, k_ref, v_ref, dec_ref, st_ref, do_ref, dq_ref, dk_ref, dv_ref, ddec_ref, dstate):
        @pl.when(pl.program_id(1) == 0)
        def _():
            dstate[...] = jnp.zeros_like(dstate)

        for c in order(not reverse):
            for hh in range(HG_PAIR):
                dst = dstate[hh]
                st = st_ref[c, hh]
                do_c = do_ref[rows(c), vcols(hh)]
                dk_ref[rows(c), kcols(hh)] = _dot(v_ref[rows(c), vcols(hh)], dst, NN)
                dv_ref[rows(c), vcols(hh)] = _dot(k_ref[rows(c), kcols(hh)], dst, NT)
                ddec_ref[pl.ds(c, 1), kcols(hh)] = jnp.sum(dst * st, axis=0, keepdims=True)
                dq_ref[rows(c), kcols(hh)] = _dot(do_c, st, NN)
                dstate[hh] = dst * dec_ref[pl.ds(c, 1), kcols(hh)] + _dot(do_c, q_ref[rows(c), kcols(hh)], TN)

    def specs(flip):
        blk = (lambda i: nblk - 1 - i) if flip else (lambda i: i)
        tok_k = pl.BlockSpec((tb, wk), lambda hp, i: (blk(i), hp))
        tok_v = pl.BlockSpec((tb, wv), lambda hp, i: (blk(i), hp))
        chk = pl.BlockSpec((cpb, wk), lambda hp, i: (blk(i), hp))
        sts = pl.BlockSpec((cpb, HG_PAIR, B_DV, B_DK), lambda hp, i: (blk(i), hp, 0, 0))
        return tok_k, tok_v, chk, sts

    scratch = [pltpu.VMEM((HG_PAIR, B_DV, B_DK), F32)]

    def run_fwd(qd, ke, v, dec):
        tok_k, tok_v, chk, sts = specs(reverse)
        return _pallas(
            fwd_body, name=name + "_fwd", grid=(n_hp, nblk), in_specs=[tok_k, tok_k, tok_v, chk], out_specs=[tok_v, sts],
            out_shape=[jax.ShapeDtypeStruct(v.shape, F32), jax.ShapeDtypeStruct((nc, B_HEADS, B_DV, B_DK), F32)],
            scratch_shapes=scratch, compiler_params=_cparams("parallel", "arbitrary"))(qd, ke, v, dec)

    @jax.custom_vjp
    def f(qd, ke, v, dec):
        return run_fwd(qd, ke, v, dec)[0]

    def fwd(qd, ke, v, dec):
        o, st = run_fwd(qd, ke, v, dec)
        return o, (qd, ke, v, dec, st)

    def bwd(res, do):
        qd, ke, v, dec, st = res
        tok_k, tok_v, chk, sts = specs(not reverse)
        return tuple(_pallas(
            bwd_body, name=name + "_bwd", grid=(n_hp, nblk), in_specs=[tok_k, tok_k, tok_v, chk, sts, tok_v],
            out_specs=[tok_k, tok_k, tok_v, chk],
            out_shape=[jax.ShapeDtypeStruct(qd.shape, F32), jax.ShapeDtypeStruct(ke.shape, F32),
                       jax.ShapeDtypeStruct(v.shape, F32), jax.ShapeDtypeStruct(dec.shape, F32)],
            scratch_shapes=scratch, compiler_params=_cparams("parallel", "arbitrary"))(qd, ke, v, dec, st, do))

    f.defvjp(fwd, bwd)
    return f(qd, ke, v, dec)


def loss_head(y, target, name="loss"):
    s, d = y.shape
    tr = _row_tile(s, d)

    def body(y_ref, t_ref, o_ref):
        @pl.when(pl.program_id(0) == 0)
        def _():
            o_ref[...] = jnp.zeros_like(o_ref)

        e = y_ref[...] - t_ref[...]
        part = jnp.sum(jnp.sum(e * e, axis=-1, keepdims=True), axis=0, keepdims=True) * (0.5 / d)
        o_ref[...] += jnp.broadcast_to(part, o_ref.shape)

    spec = pl.BlockSpec((tr, d), lambda i: (i, 0))

    def run(y, t):
        out = _pallas(body, name=name, grid=(s // tr,), in_specs=[spec, spec],
                      out_specs=pl.BlockSpec((SUBLANES, LANES), lambda i: (0, 0)),
                      out_shape=jax.ShapeDtypeStruct((SUBLANES, LANES), F32), compiler_params=_cparams("arbitrary"))(y, t)
        return out[0, 0]

    @jax.custom_vjp
    def f(y, t):
        return run(y, t)

    def fwd(y, t):
        return run(y, t), (y, t)

    def bwd(res, g):
        y, t = res
        dy = g * (y - t) * (1.0 / d)
        return dy, -dy

    f.defvjp(fwd, bwd)
    return f(y, target)


def _mesh_pos():
    return lax.axis_index("x"), lax.axis_index("y"), lax.axis_index("c")


def all_gather_rows(shard):
    r, w = shard.shape

    def body(x_ref, out_ref, send_sems, recv_sems, local_sem):
        x, y, c = _mesh_pos()
        me, sibling = (x, y, c), (x, y, 1 - c)
        chips = [(1 - x, y), (x, 1 - y), (1 - x, 1 - y)]

        def slot(px, py, pc):
            return out_ref.at[4 * px + 2 * py + pc]

        def copy(k, block, to, src=None):
            return pltpu.make_async_remote_copy(
                src_ref=slot(*block) if src is None else src, dst_ref=slot(*block), send_sem=send_sems.at[k],
                recv_sem=recv_sems.at[k], device_id=to, device_id_type=pl.DeviceIdType.MESH)

        mine = pltpu.make_async_copy(x_ref, slot(*me), local_sem)
        mine.start()
        first = [copy(0, me, sibling, src=x_ref)]
        first += [copy(1 + j, me, (*chip, c), src=x_ref) for j, chip in enumerate(chips)]
        for cp in first:
            cp.start()
        passed = [copy(4 + j, (*chip, c), sibling) for j, chip in enumerate(chips)]
        for j, chip in enumerate(chips):
            copy(1 + j, (*chip, c), me).wait_recv()
            passed[j].start()
        copy(0, sibling, me).wait_recv()
        for j, chip in enumerate(chips):
            copy(4 + j, (*chip, 1 - c), me).wait_recv()
        for cp in first + passed:
            cp.wait_send()
        mine.wait()

    any_spec = pl.BlockSpec(memory_space=pl.ANY)
    return _pallas(
        body, name="all_gather_weights", out_shape=jax.ShapeDtypeStruct((N_DEV, r, w), shard.dtype), in_specs=[any_spec],
        out_specs=any_spec,
        scratch_shapes=[pltpu.SemaphoreType.DMA((7,)), pltpu.SemaphoreType.DMA((7,)), pltpu.SemaphoreType.DMA],
    )(shard)


def all_to_all_rows(blocks):
    n, r, w = blocks.shape

    def body(g_ref, out_ref, send_sems, recv_sems, local_sem):
        x, y, c = _mesh_pos()
        me = 4 * x + 2 * y + c
        mine = pltpu.make_async_copy(g_ref.at[me], out_ref.at[me], local_sem)
        mine.start()
        copies = []
        for k in range(1, N_DEV):
            px = 1 - x if k & 4 else x
            py = 1 - y if k & 2 else y
            pc = 1 - c if k & 1 else c
            cp = pltpu.make_async_remote_copy(
                src_ref=g_ref.at[4 * px + 2 * py + pc], dst_ref=out_ref.at[me], send_sem=send_sems.at[k - 1],
                recv_sem=recv_sems.at[k - 1], device_id=(px, py, pc), device_id_type=pl.DeviceIdType.MESH)
            cp.start()
            copies.append(cp)
        for cp in copies:
            cp.wait_recv()
        for cp in copies:
            cp.wait_send()
        mine.wait()

    any_spec = pl.BlockSpec(memory_space=pl.ANY)
    return _pallas(
        body, name="all_to_all_grads", out_shape=jax.ShapeDtypeStruct((n, r, w), blocks.dtype), in_specs=[any_spec],
        out_specs=any_spec,
        scratch_shapes=[pltpu.SemaphoreType.DMA((7,)), pltpu.SemaphoreType.DMA((7,)), pltpu.SemaphoreType.DMA],
    )(blocks)


def all_gather_small(v):
    r, w = v.shape

    def body(x_ref, out_ref, send_sems, recv_sems):
        x, y, c = _mesh_pos()
        me = 4 * x + 2 * y + c
        copies = []
        for k in range(1, N_DEV):
            px = 1 - x if k & 4 else x
            py = 1 - y if k & 2 else y
            pc = 1 - c if k & 1 else c
            cp = pltpu.make_async_remote_copy(
                src_ref=x_ref, dst_ref=out_ref.at[me], send_sem=send_sems.at[k - 1], recv_sem=recv_sems.at[k - 1],
                device_id=(px, py, pc), device_id_type=pl.DeviceIdType.MESH)
            cp.start()
            copies.append(cp)
        out_ref[me] = x_ref[...]
        for cp in copies:
            cp.wait_recv()
        for cp in copies:
            cp.wait_send()

    vmem = pl.BlockSpec(memory_space=pltpu.VMEM)
    return _pallas(
        body, name="all_gather_small", out_shape=jax.ShapeDtypeStruct((N_DEV, r, w), v.dtype), in_specs=[vmem],
        out_specs=vmem, scratch_shapes=[pltpu.SemaphoreType.DMA((7,)), pltpu.SemaphoreType.DMA((7,))],
    )(v)


def adamw_rows(parts, w, m, v, name):
    n, r, lanes = parts.shape
    tr = _tile(r, PACK_ROW_TILE, SUBLANES)
    c1 = 1.0 / (1.0 - ADAM_B1 ** ADAM_STEP)
    c2 = 1.0 / (1.0 - ADAM_B2 ** ADAM_STEP)

    def body(p_ref, w_ref, m_ref, v_ref, g_ref, d_ref, nm_ref, nv_ref):
        g = p_ref[0]
        for j in range(1, n):
            g = g + p_ref[j]
        nm = ADAM_B1 * m_ref[...] + (1.0 - ADAM_B1) * g
        nv = ADAM_B2 * v_ref[...] + (1.0 - ADAM_B2) * (g * g)
        g_ref[...] = g
        nm_ref[...] = nm
        nv_ref[...] = nv
        d_ref[...] = -ADAM_LR * ((nm * c1) / (jnp.sqrt(nv * c2) + ADAM_EPS) + ADAM_WD * w_ref[...])

    row = pl.BlockSpec((tr, lanes), lambda i: (i, 0))
    out = jax.ShapeDtypeStruct((r, lanes), F32)
    return _pallas(body, name=name, grid=(r // tr,), in_specs=[pl.BlockSpec((n, tr, lanes), lambda i: (0, i, 0)), row, row, row],
                   out_specs=[row, row, row, row], out_shape=[out, out, out, out], compiler_params=_cparams("parallel"))(
        parts, w, m, v)


def _padded(n):
    return -(-n // PACK_QUANTUM) * PACK_QUANTUM


def _pack(pieces, total_rows=None):
    flat = []
    for p in pieces:
        p = p.reshape(-1).astype(F32)
        flat.append(jnp.pad(p, (0, _padded(p.size) - p.size)))
    out = jnp.concatenate(flat).reshape(-1, LANES)
    if total_rows is not None and out.shape[0] != total_rows:
        out = jnp.pad(out, ((0, total_rows - out.shape[0]), (0, 0)))
    return out


def _pack_rows(sizes):
    rows = sum(_padded(n) for n in sizes) // LANES
    return -(-rows // PACK_ROW_TILE) * PACK_ROW_TILE


def _unpack(rows, shapes):
    lead = rows.shape[:-2]
    flat = rows.reshape(*lead, -1)
    out, off = [], 0
    for shp in shapes:
        n = int(np.prod(shp))
        out.append(flat[..., off:off + n].reshape(*lead, *shp))
        off += _padded(n)
    return out


def _shards_to_full(stacked, axis):
    moved = jnp.moveaxis(stacked, 0, axis)
    shp = list(stacked.shape[1:])
    shp[axis] *= N_DEV
    return moved.reshape(shp)


def _full_to_shards(full, axis):
    shp = list(full.shape)
    shp[axis:axis + 1] = [N_DEV, shp[axis] // N_DEV]
    return jnp.moveaxis(full.reshape(shp), axis, 0)


def _heads(t, n, d):
    return jnp.transpose(t.reshape(t.shape[0], n, d), (1, 0, 2)).astype(ACT_DTYPE)


def _unheads(t):
    return jnp.transpose(t, (1, 0, 2)).reshape(t.shape[1], -1)


def _rope_tables(s):
    half = A_ROPE // 2
    inv = ROPE_THETA ** (-jnp.arange(half, dtype=F32) / half)
    ang = jnp.arange(s, dtype=jnp.int32).astype(F32)[:, None] * inv[None, :]
    return jnp.cos(ang), jnp.sin(ang)


def _rope(t, cos, sin):
    half = A_ROPE // 2
    t1, t2 = t[..., :half], t[..., half:]
    c, sn = cos[:, None, :], sin[:, None, :]
    return jnp.concatenate([t1 * c - t2 * sn, t1 * sn + t2 * c], axis=-1)


def _t5_bucket(rel):
    nb = REL_BUCKETS // 2
    max_exact = nb // 2
    ret = (rel > 0).astype(jnp.int32) * nb
    n = jnp.abs(rel)
    large = max_exact + (jnp.log(jnp.maximum(n, 1).astype(F32) / max_exact)
                         / math.log(REL_MAX_DIST / max_exact) * (nb - max_exact)).astype(jnp.int32)
    large = jnp.minimum(large, nb - 1)
    return ret + jnp.where(n < max_exact, n, large)


def _window_bias(rel_bias):
    span = 3 * C_BLOCK
    rel = jnp.arange(span)[None, :] - C_BLOCK - jnp.arange(C_BLOCK)[:, None]
    onehot = (_t5_bucket(rel)[..., None] == jnp.arange(REL_BUCKETS)).astype(F32)
    bias = jnp.einsum("qkb,bh->hqk", onehot, rel_bias.astype(F32), precision=lax.Precision.HIGHEST)
    bias = jnp.where((jnp.abs(rel) <= C_WINDOW)[None], bias, NEG)
    return jnp.transpose(bias.reshape(C_HEADS, C_BLOCK, 3, C_BLOCK), (0, 2, 1, 3))


def _mla(cq, ckv, kr, gq, gkv, wuq, wukv, cos, sin):
    s = cq.shape[0]
    q = linear(rmsnorm(cq, gq, ACT_DTYPE, "rms_cq"), wuq, name="a_wuq").reshape(s, A_HEADS, A_NOPE + A_ROPE)
    q = jnp.concatenate([q[..., :A_NOPE], _rope(q[..., A_NOPE:], cos, sin)], axis=-1)
    kv = linear(rmsnorm(ckv, gkv, ACT_DTYPE, "rms_ckv"), wukv, name="a_wukv").reshape(s, A_HEADS, A_NOPE + A_V)
    k_rope = jnp.broadcast_to(_rope(kr[:, None, :], cos, sin), (s, A_HEADS, A_ROPE))
    k = jnp.concatenate([kv[..., :A_NOPE], k_rope], axis=-1)
    v = kv[..., A_NOPE:]
    tr = lambda t: jnp.transpose(t, (1, 0, 2)).astype(ACT_DTYPE)
    o = attention(tr(q), tr(k), tr(v), (A_NOPE + A_ROPE) ** -0.5, MLA_TQ, MLA_TK, "mla")
    return _unheads(o)


def _hgrn2(q, f_fwd, f_bwd, i, g, lb_fwd, lb_bwd, g_out):
    s = q.shape[0]
    o = None
    for z, lb, rev, tag in ((f_fwd, lb_fwd, False, "hgf"), (f_bwd, lb_bwd, True, "hgb")):
        qd, ki, ke, dec = hgrn_prep(q, z, lb.astype(F32).reshape(1, -1), rev, tag + "_prep")
        part = hgrn_intra(qd, ki, i, rev, tag + "_intra") + hgrn_inter(qd, ke, i, dec, rev, tag + "_inter")
        o = part if o is None else o + part
    o = rmsnorm(o.reshape(s * B_HEADS, B_DV), g_out, F32, "rms_hg").reshape(s, B_HEADS * B_DV)
    return o * jax.nn.silu(g)


def _cross(h, mem_n, wq, wkv, wo):
    q = _heads(linear(h, wq, name="x_wq"), X_HEADS, X_DH)
    kv = linear(mem_n, wkv, name="x_wkv").reshape(mem_n.shape[0], 2, X_HEADS, X_DH)
    k = jnp.transpose(kv[:, 0], (1, 0, 2)).astype(ACT_DTYPE)
    v = jnp.transpose(kv[:, 1], (1, 0, 2)).astype(ACT_DTYPE)
    o = attention(q, k, v, X_DH ** -0.5, CROSS_TQ, 256, "cross")
    return linear(_unheads(o), wo, name="x_wo")


def _pad_w_in(w):
    cut = A_Q_RANK + A_KV_RANK + A_ROPE
    return jnp.concatenate([w[:, :cut], jnp.zeros((w.shape[0], KR_PAD), w.dtype), w[:, cut:]], axis=1)


def _model_loss(p, x, mem, target):
    s = x.shape[0]
    cos, sin = _rope_tables(s)
    sm = jax.nn.softmax(p["b_lb"].astype(F32), axis=1)
    lower_bounds = jnp.cumsum(sm, axis=1) - sm[:, :1]
    bias = _window_bias(p["rel_bias"])
    for l in range(DEPTH):
        h = rmsnorm(x, p["g_mix"][l], ACT_DTYPE, "rms_mix")
        z = linear(h, _pad_w_in(p["w_in"][l]), name="w_in")
        parts, start = [], 0
        for width in IN_SPLITS_PADDED:
            parts.append(z[:, start:start + width])
            start += width
        a_cq, a_ckv, a_kr, b_q, b_ff, b_fb, b_i, b_g, c_q, c_k, c_v, gate_a, gate_b, gate_c = parts
        y_a = _mla(a_cq, a_ckv, a_kr[:, :A_ROPE], p["a_gq"][l], p["a_gkv"][l], p["a_wuq"][l], p["a_wukv"][l], cos, sin)
        y_b = _hgrn2(b_q, b_ff, b_fb, b_i, b_g, lower_bounds[0, l], lower_bounds[1, l], p["b_gout"][l])
        y_c = _unheads(window_attention(_heads(c_q, C_HEADS, C_DH), _heads(c_k, C_KV_HEADS, C_DH),
                                        _heads(c_v, C_KV_HEADS, C_DH), bias, p["c_sink"][l]))
        merged = (jax.nn.sigmoid(gate_a) * linear(y_a, p["w_br_a"][l], name="w_br_a")
                  + jax.nn.sigmoid(gate_b) * linear(y_b, p["w_br_b"][l], name="w_br_b")
                  + jax.nn.sigmoid(gate_c) * linear(y_c, p["w_br_c"][l], name="w_br_c"))
        x = x + linear(merged, p["w_out"][l], name="w_out")
        h = rmsnorm(x, p["g_x"][l], ACT_DTYPE, "rms_x")
        x = x + _cross(h, rmsnorm(mem, p["g_mem"][l], ACT_DTYPE, "rms_mem"), p["x_wq"][l], p["x_wkv"][l], p["x_wo"][l])
        h = rmsnorm(x, p["g_ffn"][l], ACT_DTYPE, "rms_ffn")
        t = jax.nn.silu(linear(h, p["f_w1"][l], name="f_w1")) * linear(h, p["f_w3"][l], name="f_w3")
        x = x + linear(t, p["f_w2"][l], name="f_w2")
    y = rmsnorm(x, p["g_final"], F32, "rms_final")
    return loss_head(y, target)


def kernel(x, mem, w_in, g_mix, a_gq, a_gkv, a_wuq, a_wukv, b_lb, b_gout, c_sink, rel_bias, w_br_a, w_br_b, w_br_c, w_out, g_x, g_mem, x_wq, x_wkv, x_wo, g_ffn, f_w1, f_w3, f_w2, g_final, loss_target, m_w_in, m_g_mix, m_a_gq, m_a_gkv, m_a_wuq, m_a_wukv, m_b_lb, m_b_gout, m_c_sink, m_rel_bias, m_w_br_a, m_w_br_b, m_w_br_c, m_w_out, m_g_x, m_g_mem, m_x_wq, m_x_wkv, m_x_wo, m_g_ffn, m_f_w1, m_f_w3, m_f_w2, m_g_final, v_w_in, v_g_mix, v_a_gq, v_a_gkv, v_a_wuq, v_a_wukv, v_b_lb, v_b_gout, v_c_sink, v_rel_bias, v_w_br_a, v_w_br_b, v_w_br_c, v_w_out, v_g_x, v_g_mem, v_x_wq, v_x_wkv, v_x_wo, v_g_ffn, v_f_w1, v_f_w3, v_f_w2, v_g_final):
    given = dict(locals())
    w = {n: given[n] for n in WEIGHT_ORDER}
    m = {n: given["m_" + n] for n in WEIGHT_ORDER}
    v = {n: given["v_" + n] for n in WEIGHT_ORDER}
    sh_names = [n for n, _ in SHARDED]
    sh_shapes = [w[n].shape for n in sh_names]
    sh_rows = _pack_rows([int(np.prod(s)) for s in sh_shapes])
    rep_shapes = [w[n].shape for n in REPLICATED] + [(1,)]
    rep_rows = _pack_rows([int(np.prod(s)) for s in rep_shapes])

    gathered = _unpack(all_gather_rows(_pack([w[n] for n in sh_names], sh_rows)), sh_shapes)
    full = {n: _shards_to_full(t, ax) for (n, ax), t in zip(SHARDED, gathered)}
    full.update({n: w[n] for n in REPLICATED})

    loss, (grad_full, grad_x) = jax.value_and_grad(_model_loss, argnums=(0, 1))(full, x[0], mem[0], loss_target[0])

    per_dev = [_full_to_shards(grad_full[n], ax) for n, ax in SHARDED]
    flat = []
    for t in per_dev:
        t = t.reshape(N_DEV, -1)
        flat.append(jnp.pad(t, ((0, 0), (0, _padded(t.shape[1]) - t.shape[1]))))
    packed = jnp.concatenate(flat, axis=1)
    packed = jnp.pad(packed, ((0, 0), (0, sh_rows * LANES - packed.shape[1]))).reshape(N_DEV, sh_rows, LANES)
    received = all_to_all_rows(packed)
    outs = adamw_rows(received, _pack([w[n] for n in sh_names], sh_rows), _pack([m[n] for n in sh_names], sh_rows),
                      _pack([v[n] for n in sh_names], sh_rows), "adamw_sharded")
    g_sh, d_sh, nm_sh, nv_sh = [dict(zip(sh_names, _unpack(o, sh_shapes))) for o in outs]

    mine = _pack([grad_full[n] for n in REPLICATED] + [loss.reshape(1)], rep_rows)
    everyone = all_gather_small(mine)
    rep_w = [w[n] for n in REPLICATED] + [jnp.zeros((1,), F32)]
    outs = adamw_rows(everyone, _pack(rep_w, rep_rows), _pack([m[n] for n in REPLICATED] + [jnp.zeros((1,), F32)], rep_rows),
                      _pack([v[n] for n in REPLICATED] + [jnp.ones((1,), F32)], rep_rows), "adamw_replicated")
    rep_names = list(REPLICATED) + ["loss"]
    g_rp, d_rp, nm_rp, nv_rp = [dict(zip(rep_names, _unpack(o, rep_shapes))) for o in outs]

    def pick(sharded, replicated, n):
        return sharded[n] if n in sharded else replicated[n]

    return (g_rp["loss"].reshape(()), grad_x[None],
            *[pick(g_sh, g_rp, n) for n in WEIGHT_ORDER], *[pick(d_sh, d_rp, n) for n in WEIGHT_ORDER],
            *[pick(nm_sh, nm_rp, n) for n in WEIGHT_ORDER], *[pick(nv_sh, nv_rp, n) for n in WEIGHT_ORDER])
```

```python
import functools
import math

import jax
import jax.numpy as jnp
import numpy as np
from jax import lax
from jax.experimental import pallas as pl
from jax.experimental.pallas import tpu as pltpu

F32 = jnp.float32
MXU_DTYPE = jnp.bfloat16
ACT_DTYPE = jnp.bfloat16

V7X_VMEM_LIMIT_BYTES = 56 * 1024 * 1024
LANES = 128
SUBLANES = 8

N_DEV = 8
D_MODEL = 1024
DEPTH = 2
EPS = 1e-6
TINY = 1e-30
NEG = -1e30

A_HEADS, A_NOPE, A_ROPE, A_V, A_Q_RANK, A_KV_RANK = 8, 64, 32, 64, 384, 256
ROPE_THETA = 10000.0
B_HEADS, B_DK, B_DV, B_CHUNK = 8, 128, 64, 16
C_HEADS, C_KV_HEADS, C_DH, C_WINDOW, C_BLOCK = 8, 2, 64, 128, 128
REL_BUCKETS, REL_MAX_DIST = 32, 128
X_HEADS, X_DH = 4, 256
D_FF = 2816
IN_SPLITS = (A_Q_RANK, A_KV_RANK, A_ROPE, 1024, 1024, 1024, 512, 512, 512, 128, 128, 1024, 1024, 1024)
IN_WIDTH = sum(IN_SPLITS)
KR_PAD = LANES - A_ROPE
IN_SPLITS_PADDED = (A_Q_RANK, A_KV_RANK, LANES, 1024, 1024, 1024, 512, 512, 512, 128, 128, 1024, 1024, 1024)

ADAM_LR, ADAM_B1, ADAM_B2, ADAM_EPS, ADAM_WD, ADAM_STEP = 0.001, 0.9, 0.999, 1e-08, 0.01, 10

SHARDED = (("w_in", 2), ("a_wuq", 2), ("a_wukv", 2), ("b_lb", 2), ("w_br_a", 2), ("w_br_b", 2), ("w_br_c", 2),
           ("w_out", 1), ("x_wq", 1), ("x_wkv", 2), ("x_wo", 1), ("f_w1", 2), ("f_w3", 2), ("f_w2", 1))
ELEMENTWISE_SHARDED = ("b_lb",)
REPLICATED = ("g_mix", "a_gq", "a_gkv", "b_gout", "c_sink", "rel_bias", "g_x", "g_mem", "g_ffn", "g_final")
WEIGHT_ORDER = ("w_in", "g_mix", "a_gq", "a_gkv", "a_wuq", "a_wukv", "b_lb", "b_gout", "c_sink", "rel_bias", "w_br_a",
                "w_br_b", "w_br_c", "w_out", "g_x", "g_mem", "x_wq", "x_wkv", "x_wo", "g_ffn", "f_w1", "f_w3", "f_w2",
                "g_final")
PACK_QUANTUM = SUBLANES * LANES
PACK_ROW_TILE = 512


def _pallas(body, **kw):
    return pl.pallas_call(body, **kw)


def _cparams(*sem):
    return pltpu.CompilerParams(dimension_semantics=sem, vmem_limit_bytes=V7X_VMEM_LIMIT_BYTES)


def _tile(n, target, mult=LANES):
    t = (min(target, n) // mult) * mult
    while t >= mult:
        if n % t == 0:
            return t
        t -= mult
    return n


def _dot(a, b, dims):
    return lax.dot_general(a.astype(MXU_DTYPE), b.astype(MXU_DTYPE), (dims, ((), ())), preferred_element_type=F32)


NN = ((1,), (0,))
NT = ((1,), (1,))
TN = ((0,), (0,))


MM_VMEM_BUDGET_BYTES = 40 * 1024 * 1024
MM_MAX_TILE = 4352
MM_MAX_ROW_TILE = 2048
MM_HBM_BYTES_PER_S = 2.5e12
MM_STEP_S = 0.4e-6
MM_DMA_ROW_OVERHEAD_BYTES = 512.0


def _tile_options(n, cap):
    out = [t for t in range(LANES, min(n, cap) + 1, LANES) if n % t == 0]
    if n <= cap and n not in out:
        out.append(n)
    return out or [n]


@functools.lru_cache(maxsize=None)
def _mm_plan(m, n, k, ta, tb, a_bytes, b_bytes, o_bytes):
    best = None
    for tk in _tile_options(k, MM_MAX_TILE):
        nk = k // tk
        for tn in _tile_options(n, MM_MAX_TILE):
            for tm in _tile_options(m, MM_MAX_ROW_TILE):
                vmem = 2 * (tm * tk * a_bytes + tk * tn * b_bytes + tm * tn * o_bytes) + tm * tn * 4
                vmem += (tm * tk * 2 if a_bytes == 4 else 0) + (tk * tn * 2 if b_bytes == 4 else 0)
                if vmem > MM_VMEM_BUDGET_BYTES:
                    continue

                def eff(elems, nbytes):
                    return (elems * nbytes) / (elems * nbytes + MM_DMA_ROW_OVERHEAD_BYTES)

                ea, eb, eo = eff(tm if ta else tk, a_bytes), eff(tk if tb else tn, b_bytes), eff(tn, o_bytes)
                for order in ("mn", "nm"):
                    if nk == 1 and order == "nm":
                        a_tr, b_tr = m * k * a_bytes * (n // tn), k * n * b_bytes
                    elif nk == 1:
                        a_tr, b_tr = m * k * a_bytes, k * n * b_bytes * (m // tm)
                    else:
                        a_tr, b_tr = m * k * a_bytes * (n // tn), k * n * b_bytes * (m // tm)
                    steps = (m // tm) * (n // tn) * nk
                    cost = (a_tr / ea + b_tr / eb + m * n * o_bytes / eo) / MM_HBM_BYTES_PER_S + steps * MM_STEP_S
                    if best is None or cost < best[0]:
                        best = (cost, tm, tn, tk, order)
    assert best is not None, (m, n, k)
    return best[1:]


def _mm(a, b, ta=False, tb=False, out_dtype=F32, name="mm"):
    m, k = (a.shape[1], a.shape[0]) if ta else a.shape
    kb, n = (b.shape[1], b.shape[0]) if tb else b.shape
    assert k == kb, (a.shape, b.shape, ta, tb)
    tm, tn, tk, order = _mm_plan(m, n, k, ta, tb, a.dtype.itemsize, b.dtype.itemsize, jnp.dtype(out_dtype).itemsize)
    nk = k // tk
    dims = ((0 if ta else 1,), (1 if tb else 0,))
    out_shape = jax.ShapeDtypeStruct((m, n), out_dtype)

    if nk == 1:
        def body(a_ref, b_ref, o_ref):
            o_ref[...] = _dot(a_ref[...], b_ref[...], dims).astype(o_ref.dtype)

        if order == "nm":
            mi, ni = (lambda j, i: i), (lambda j, i: j)
            grid = (n // tn, m // tm)
        else:
            mi, ni = (lambda i, j: i), (lambda i, j: j)
            grid = (m // tm, n // tn)
        a_spec = pl.BlockSpec((tk, tm), lambda p, q: (0, mi(p, q))) if ta else pl.BlockSpec((tm, tk), lambda p, q: (mi(p, q), 0))
        b_spec = pl.BlockSpec((tn, tk), lambda p, q: (ni(p, q), 0)) if tb else pl.BlockSpec((tk, tn), lambda p, q: (0, ni(p, q)))
        return _pallas(body, name=name, grid=grid, in_specs=[a_spec, b_spec],
                       out_specs=pl.BlockSpec((tm, tn), lambda p, q: (mi(p, q), ni(p, q))), out_shape=out_shape,
                       compiler_params=_cparams("parallel", "parallel"))(a, b)

    direct = jnp.dtype(out_dtype) == jnp.dtype(F32)

    def body(a_ref, b_ref, o_ref, *scratch):
        acc_ref = o_ref if direct else scratch[0]
        kk = pl.program_id(2)

        @pl.when(kk == 0)
        def _():
            acc_ref[...] = jnp.zeros_like(acc_ref)

        acc_ref[...] += _dot(a_ref[...], b_ref[...], dims)

        if not direct:
            @pl.when(kk == nk - 1)
            def _():
                o_ref[...] = acc_ref[...].astype(o_ref.dtype)

    a_spec = pl.BlockSpec((tk, tm), lambda i, j, kk: (kk, i)) if ta else pl.BlockSpec((tm, tk), lambda i, j, kk: (i, kk))
    b_spec = pl.BlockSpec((tn, tk), lambda i, j, kk: (j, kk)) if tb else pl.BlockSpec((tk, tn), lambda i, j, kk: (kk, j))
    return _pallas(
        body, name=name, grid=(m // tm, n // tn, nk), in_specs=[a_spec, b_spec],
        out_specs=pl.BlockSpec((tm, tn), lambda i, j, kk: (i, j)), out_shape=out_shape,
        scratch_shapes=[] if direct else [pltpu.VMEM((tm, tn), F32)],
        compiler_params=_cparams("parallel", "parallel", "arbitrary"),
    )(a, b)


def linear(a, w, out_dtype=F32, name="lin"):
    @jax.custom_vjp
    def f(a, w):
        return _mm(a.astype(ACT_DTYPE), w.astype(MXU_DTYPE), out_dtype=out_dtype, name=name + "_fwd")

    def fwd(a, w):
        ab, wb = a.astype(ACT_DTYPE), w.astype(MXU_DTYPE)
        return _mm(ab, wb, out_dtype=out_dtype, name=name + "_fwd"), (ab, wb, jnp.zeros((0,), a.dtype))

    def bwd(res, g):
        ab, wb, like_a = res
        gb = g.astype(ACT_DTYPE)
        da = _mm(gb, wb, tb=True, out_dtype=like_a.dtype, name=name + "_dx")
        dw = _mm(ab, gb, ta=True, out_dtype=F32, name=name + "_dw")
        return da, dw

    f.defvjp(fwd, bwd)
    return f(a, w)


def _row_tile(rows, width):
    return _tile(rows, max(SUBLANES, (512 * 1024) // width), 16)


def rmsnorm(x, g, out_dtype=F32, name="rms"):
    rows, d = x.shape
    tr = _row_tile(rows, d)
    n_steps = rows // tr

    def fwd_body(x_ref, g_ref, o_ref):
        xv = x_ref[...].astype(F32)
        r = lax.rsqrt(jnp.mean(xv * xv, axis=-1, keepdims=True) + EPS)
        o_ref[...] = (xv * r * g_ref[...]).astype(o_ref.dtype)

    def bwd_body(x_ref, g_ref, dy_ref, dx_ref, dg_ref):
        xv = x_ref[...].astype(F32)
        dy = dy_ref[...].astype(F32)
        r = lax.rsqrt(jnp.mean(xv * xv, axis=-1, keepdims=True) + EPS)
        xh = xv * r
        dxh = dy * g_ref[...]
        dx_ref[...] = (r * (dxh - xh * jnp.mean(dxh * xh, axis=-1, keepdims=True))).astype(dx_ref.dtype)

        @pl.when(pl.program_id(0) == 0)
        def _():
            dg_ref[...] = jnp.zeros_like(dg_ref)

        dg_ref[...] += jnp.sum(dy * xh, axis=0, keepdims=True)

    row_spec = pl.BlockSpec((tr, d), lambda i: (i, 0))
    vec_spec = pl.BlockSpec((1, d), lambda i: (0, 0))

    def run_fwd(x, g):
        return _pallas(fwd_body, name=name + "_fwd", grid=(n_steps,), in_specs=[row_spec, vec_spec], out_specs=row_spec,
                       out_shape=jax.ShapeDtypeStruct((rows, d), out_dtype), compiler_params=_cparams("parallel"))(
            x, g.reshape(1, d).astype(F32))

    @jax.custom_vjp
    def f(x, g):
        return run_fwd(x, g)

    def fwd(x, g):
        return run_fwd(x, g), (x, g)

    def bwd(res, dy):
        x, g = res
        dx, dg = _pallas(
            bwd_body, name=name + "_bwd", grid=(n_steps,), in_specs=[row_spec, vec_spec, row_spec],
            out_specs=[row_spec, vec_spec],
            out_shape=[jax.ShapeDtypeStruct((rows, d), x.dtype), jax.ShapeDtypeStruct((1, d), F32)],
            compiler_params=_cparams("arbitrary"))(x, g.reshape(1, d).astype(F32), dy)
        return dx, dg.reshape(g.shape).astype(g.dtype)

    f.defvjp(fwd, bwd)
    return f(x, g)


def _rowdot(a, b, name):
    h, s, d = a.shape
    ts = _tile(s, 2048)

    def body(a_ref, b_ref, o_ref):
        o_ref[...] = jnp.sum(a_ref[...].astype(F32) * b_ref[...].astype(F32), axis=-1, keepdims=True)

    spec = pl.BlockSpec((None, ts, d), lambda hh, i: (hh, i, 0))
    return _pallas(body, name=name, grid=(h, s // ts), in_specs=[spec, spec],
                   out_specs=pl.BlockSpec((None, ts, 1), lambda hh, i: (hh, i, 0)),
                   out_shape=jax.ShapeDtypeStruct((h, s, 1), F32), compiler_params=_cparams("parallel", "parallel"))(a, b)


def attention(q, k, v, scale, tq, tk, name):
    h, sq, d = q.shape
    sk, dv = k.shape[1], v.shape[2]
    tq, tk = _tile(sq, tq), _tile(sk, tk)
    nq, nk = sq // tq, sk // tk

    def fwd_body(q_ref, k_ref, v_ref, o_ref, lse_ref, m_ref, l_ref, acc_ref):
        j = pl.program_id(2)

        @pl.when(j == 0)
        def _():
            m_ref[...] = jnp.full_like(m_ref, NEG)
            l_ref[...] = jnp.zeros_like(l_ref)
            acc_ref[...] = jnp.zeros_like(acc_ref)

        s = _dot(q_ref[...], k_ref[...], NT) * scale
        m_prev = m_ref[...]
        m_new = jnp.maximum(m_prev, jnp.max(s, axis=-1, keepdims=True))
        alpha = jnp.exp(m_prev - m_new)
        p = jnp.exp(s - m_new)
        l_ref[...] = alpha * l_ref[...] + jnp.sum(p, axis=-1, keepdims=True)
        acc_ref[...] = alpha * acc_ref[...] + _dot(p, v_ref[...], NN)
        m_ref[...] = m_new

        @pl.when(j == nk - 1)
        def _():
            o_ref[...] = (acc_ref[...] / l_ref[...]).astype(o_ref.dtype)
            lse_ref[...] = m_ref[...] + jnp.log(l_ref[...])

    def dq_body(q_ref, k_ref, v_ref, do_ref, lse_ref, dl_ref, dq_ref, acc_ref):
        j = pl.program_id(2)

        @pl.when(j == 0)
        def _():
            acc_ref[...] = jnp.zeros_like(acc_ref)

        s = _dot(q_ref[...], k_ref[...], NT) * scale
        p = jnp.exp(s - lse_ref[...])
        dp = _dot(do_ref[...], v_ref[...], NT)
        ds = p * (dp - dl_ref[...]) * scale
        acc_ref[...] += _dot(ds, k_ref[...], NN)

        @pl.when(j == nk - 1)
        def _():
            dq_ref[...] = acc_ref[...].astype(dq_ref.dtype)

    def dkv_body(q_ref, k_ref, v_ref, do_ref, lse_ref, dl_ref, dk_ref, dv_ref, dk_acc, dv_acc):
        i = pl.program_id(2)

        @pl.when(i == 0)
        def _():
            dk_acc[...] = jnp.zeros_like(dk_acc)
            dv_acc[...] = jnp.zeros_like(dv_acc)

        s = _dot(q_ref[...], k_ref[...], NT) * scale
        p = jnp.exp(s - lse_ref[...])
        dv_acc[...] += _dot(p, do_ref[...], TN)
        dp = _dot(do_ref[...], v_ref[...], NT)
        ds = p * (dp - dl_ref[...]) * scale
        dk_acc[...] += _dot(ds, q_ref[...], TN)

        @pl.when(i == nq - 1)
        def _():
            dk_ref[...] = dk_acc[...].astype(dk_ref.dtype)
            dv_ref[...] = dv_acc[...].astype(dv_ref.dtype)

    def q_spec(width):
        return pl.BlockSpec((None, tq, width), lambda hh, i, j: (hh, i, 0))

    def k_spec(width):
        return pl.BlockSpec((None, tk, width), lambda hh, i, j: (hh, j, 0))

    def run_fwd(q, k, v):
        return _pallas(
            fwd_body, name=name + "_fwd", grid=(h, nq, nk), in_specs=[q_spec(d), k_spec(d), k_spec(dv)],
            out_specs=[q_spec(dv), q_spec(1)],
            out_shape=[jax.ShapeDtypeStruct((h, sq, dv), q.dtype), jax.ShapeDtypeStruct((h, sq, 1), F32)],
            scratch_shapes=[pltpu.VMEM((tq, 1), F32), pltpu.VMEM((tq, 1), F32), pltpu.VMEM((tq, dv), F32)],
            compiler_params=_cparams("parallel", "parallel", "arbitrary"))(q, k, v)

    @jax.custom_vjp
    def f(q, k, v):
        return run_fwd(q, k, v)[0]

    def fwd(q, k, v):
        o, lse = run_fwd(q, k, v)
        return o, (q, k, v, o, lse)

    def bwd(res, do):
        q, k, v, o, lse = res
        delta = _rowdot(o, do, name + "_delta")
        dq = _pallas(
            dq_body, name=name + "_dq", grid=(h, nq, nk),
            in_specs=[q_spec(d), k_spec(d), k_spec(dv), q_spec(dv), q_spec(1), q_spec(1)], out_specs=q_spec(d),
            out_shape=jax.ShapeDtypeStruct((h, sq, d), q.dtype), scratch_shapes=[pltpu.VMEM((tq, d), F32)],
            compiler_params=_cparams("parallel", "parallel", "arbitrary"))(q, k, v, do, lse, delta)

        def qs(width):
            return pl.BlockSpec((None, tq, width), lambda hh, j, i: (hh, i, 0))

        def ks(width):
            return pl.BlockSpec((None, tk, width), lambda hh, j, i: (hh, j, 0))

        dk, dv_ = _pallas(
            dkv_body, name=name + "_dkv", grid=(h, nk, nq),
            in_specs=[qs(d), ks(d), ks(dv), qs(dv), qs(1), qs(1)], out_specs=[ks(d), ks(dv)],
            out_shape=[jax.ShapeDtypeStruct((h, sk, d), k.dtype), jax.ShapeDtypeStruct((h, sk, dv), v.dtype)],
            scratch_shapes=[pltpu.VMEM((tk, d), F32), pltpu.VMEM((tk, dv), F32)],
            compiler_params=_cparams("parallel", "parallel", "arbitrary"))(q, k, v, do, lse, delta)
        return dq, dk, dv_

    f.defvjp(fwd, bwd)
    return f(q, k, v)


LOG2E = 1.4426950408889634


def linear_t(a_t, w, out_dtype=F32, name="lin_t"):
    @jax.custom_vjp
    def f(a_t, w):
        return _mm(a_t, w.astype(MXU_DTYPE), ta=True, out_dtype=out_dtype, name=name + "_fwd")

    def fwd(a_t, w):
        wb = w.astype(MXU_DTYPE)
        return _mm(a_t, wb, ta=True, out_dtype=out_dtype, name=name + "_fwd"), (a_t, wb)

    def bwd(res, g):
        a_t, wb = res
        gb = g.astype(ACT_DTYPE)
        da_t = _mm(wb, gb, tb=True, out_dtype=a_t.dtype, name=name + "_dx")
        dw = _mm(a_t, gb, out_dtype=F32, name=name + "_dw")
        return da_t, dw

    f.defvjp(fwd, bwd)
    return f(a_t, w)


def mla_attention(q, k, v, scale, name="mla"):
    s, h, d = q.shape
    dv = v.shape[2]
    tq, tk = _tile(s, MLA_TQ), _tile(s, MLA_TK)
    nq, nk = s // tq, s // tk
    c = scale * LOG2E

    def fwd_body(qt_ref, k_ref, vt_ref, ot_ref, lse_ref, m_ref, l_ref, acc_ref):
        j = pl.program_id(2)

        @pl.when(j == 0)
        def _():
            m_ref[...] = jnp.full_like(m_ref, NEG)
            l_ref[...] = jnp.zeros_like(l_ref)
            acc_ref[...] = jnp.zeros_like(acc_ref)

        st = _dot(k_ref[...], qt_ref[...], NN)
        m_prev = m_ref[...]
        m_new = jnp.maximum(m_prev, jnp.max(st, axis=0, keepdims=True) * c)
        alpha = jnp.exp2(m_prev - m_new)
        pt = jnp.exp2(st * c - m_new)
        l_ref[...] = alpha * l_ref[...] + jnp.sum(pt, axis=0, keepdims=True)
        acc_ref[...] = alpha * acc_ref[...] + _dot(vt_ref[...], pt, NN)
        m_ref[...] = m_new

        @pl.when(j == nk - 1)
        def _():
            ot_ref[...] = (acc_ref[...] / l_ref[...]).astype(ot_ref.dtype)
            lse_ref[...] = m_ref[...] + jnp.log2(l_ref[...])

    def delta_body(ot_ref, dot_ref, o_ref):
        o_ref[...] = jnp.sum(ot_ref[...].astype(F32) * dot_ref[...].astype(F32), axis=0, keepdims=True)

    def bwd_body(qt_ref, k_ref, kt_ref, v_ref, dot_ref, lse_ref, dl_ref, dqt_ref, dk_hbm, dv_hbm, dq_acc, dk_acc, dv_acc):
        hh, i, j = pl.program_id(0), pl.program_id(1), pl.program_id(2)

        @pl.when(j == 0)
        def _():
            dq_acc[...] = jnp.zeros_like(dq_acc)

        @pl.when(i == 0)
        def _():
            dk_acc[j] = jnp.zeros((d, tk), F32)
            dv_acc[j] = jnp.zeros((dv, tk), F32)

        qt, dot_ = qt_ref[...], dot_ref[...]
        pt = jnp.exp2(_dot(k_ref[...], qt, NN) * c - lse_ref[...])
        dst = (pt * (_dot(v_ref[...], dot_, NN) - dl_ref[...])).astype(MXU_DTYPE)
        dv_acc[j] += _dot(dot_, pt, NT)
        dk_acc[j] += _dot(qt, dst, NT)
        dq_acc[...] += _dot(kt_ref[...], dst, NN)

        @pl.when(j == nk - 1)
        def _():
            dqt_ref[...] = dq_acc[...] * scale

        @pl.when(i == nq - 1)
        def _():
            dk_acc[j] = dk_acc[j] * scale
            pltpu.sync_copy(dk_acc.at[j], dk_hbm.at[hh, j])
            pltpu.sync_copy(dv_acc.at[j], dv_hbm.at[hh, j])

    def qt_spec(width):
        return pl.BlockSpec((None, width, tq), lambda hh, i, j: (hh, 0, i))

    def kt_spec(width):
        return pl.BlockSpec((None, width, tk), lambda hh, i, j: (hh, 0, j))

    def k_spec(width):
        return pl.BlockSpec((None, tk, width), lambda hh, i, j: (hh, j, 0))

    def layouts(q, k, v):
        cast = lambda t: t.astype(ACT_DTYPE)
        return (cast(jnp.transpose(q, (1, 2, 0))), cast(jnp.transpose(k, (1, 0, 2))), cast(jnp.transpose(k, (1, 2, 0))),
                cast(jnp.transpose(v, (1, 0, 2))), cast(jnp.transpose(v, (1, 2, 0))))

    def run_fwd(qt, kh, vt):
        return _pallas(
            fwd_body, name=name + "_fwd", grid=(h, nq, nk), in_specs=[qt_spec(d), k_spec(d), kt_spec(dv)],
            out_specs=[qt_spec(dv), qt_spec(1)],
            out_shape=[jax.ShapeDtypeStruct((h, dv, s), ACT_DTYPE), jax.ShapeDtypeStruct((h, 1, s), F32)],
            scratch_shapes=[pltpu.VMEM((1, tq), F32), pltpu.VMEM((1, tq), F32), pltpu.VMEM((dv, tq), F32)],
            compiler_params=_cparams("parallel", "parallel", "arbitrary"))(qt, kh, vt)

    @jax.custom_vjp
    def f(q, k, v):
        qt, kh, _, _, vt = layouts(q, k, v)
        return run_fwd(qt, kh, vt)[0].reshape(h * dv, s)

    def fwd(q, k, v):
        qt, kh, kt, vh, vt = layouts(q, k, v)
        ot, lse = run_fwd(qt, kh, vt)
        return ot.reshape(h * dv, s), (qt, kh, kt, vh, ot, lse)

    def bwd(res, dy):
        qt, kh, kt, vh, ot, lse = res
        dot_ = dy.reshape(h, dv, s)
        ts = _tile(s, 2048)
        col = pl.BlockSpec((None, dv, ts), lambda hh, i: (hh, 0, i))
        delta = _pallas(delta_body, name=name + "_delta", grid=(h, s // ts), in_specs=[col, col],
                        out_specs=pl.BlockSpec((None, 1, ts), lambda hh, i: (hh, 0, i)),
                        out_shape=jax.ShapeDtypeStruct((h, 1, s), F32), compiler_params=_cparams("parallel", "parallel"))(ot, dot_)
        any_spec = pl.BlockSpec(memory_space=pl.ANY)
        dqt, dkt, dvt = _pallas(
            bwd_body, name=name + "_bwd", grid=(h, nq, nk),
            in_specs=[qt_spec(d), k_spec(d), kt_spec(d), k_spec(dv), qt_spec(dv), qt_spec(1), qt_spec(1)],
            out_specs=[qt_spec(d), any_spec, any_spec],
            out_shape=[jax.ShapeDtypeStruct((h, d, s), F32), jax.ShapeDtypeStruct((h, nk, d, tk), F32),
                       jax.ShapeDtypeStruct((h, nk, dv, tk), F32)],
            scratch_shapes=[pltpu.VMEM((d, tq), F32), pltpu.VMEM((nk, d, tk), F32), pltpu.VMEM((nk, dv, tk), F32)],
            compiler_params=_cparams("parallel", "arbitrary", "arbitrary"))(qt, kh, kt, vh, dot_, lse, delta)
        to_tokens = lambda t: jnp.transpose(t, (1, 3, 0, 2)).reshape(s, h, t.shape[2])
        return jnp.transpose(dqt, (2, 0, 1)), to_tokens(dkt), to_tokens(dvt)

    f.defvjp(fwd, bwd)
    return f(q, k, v)


def window_attention(q, k, v, bias, sink, name="wattn"):
    hq, s, dh = q.shape
    g = hq // C_KV_HEADS
    blk = C_BLOCK
    nb = s // blk
    scale = dh ** -0.5
    sink_b = jnp.broadcast_to(sink.astype(F32).reshape(hq, 1, 1), (hq, 1, LANES))

    def key_block(i, j):
        return jnp.clip(i - 1 + j, 0, nb - 1)

    def in_range(i, j):
        kb = i - 1 + j
        return jnp.logical_and(kb >= 0, kb < nb)

    def scores(q_ref, k_ref, b_ref, hh):
        return _dot(q_ref[hh], k_ref[...], NT) * scale + b_ref[hh]

    def fwd_body(q_ref, k_ref, v_ref, b_ref, sk_ref, o_ref, lse_ref, m_ref, l_ref, acc_ref):
        i, j = pl.program_id(1), pl.program_id(2)

        @pl.when(j == 0)
        def _():
            for hh in range(g):
                m_ref[hh] = jnp.broadcast_to(sk_ref[hh][:, :1], (blk, 1))
            l_ref[...] = jnp.ones_like(l_ref)
            acc_ref[...] = jnp.zeros_like(acc_ref)

        @pl.when(in_range(i, j))
        def _():
            for hh in range(g):
                sc = scores(q_ref, k_ref, b_ref, hh)
                m_prev = m_ref[hh]
                m_new = jnp.maximum(m_prev, jnp.max(sc, axis=-1, keepdims=True))
                alpha = jnp.exp(m_prev - m_new)
                p = jnp.exp(sc - m_new)
                l_ref[hh] = alpha * l_ref[hh] + jnp.sum(p, axis=-1, keepdims=True)
                acc_ref[hh] = alpha * acc_ref[hh] + _dot(p, v_ref[...], NN)
                m_ref[hh] = m_new

        @pl.when(j == 2)
        def _():
            o_ref[...] = (acc_ref[...] / l_ref[...]).astype(o_ref.dtype)
            lse_ref[...] = m_ref[...] + jnp.log(l_ref[...])

    def dq_body(q_ref, k_ref, v_ref, b_ref, sk_ref, do_ref, lse_ref, dl_ref, dq_ref, db_ref, dsink_ref, acc_ref):
        i, j = pl.program_id(1), pl.program_id(2)

        @pl.when(jnp.logical_and(i == 0, j == 0))
        def _():
            db_ref[...] = jnp.zeros_like(db_ref)
            dsink_ref[...] = jnp.zeros_like(dsink_ref)

        @pl.when(j == 0)
        def _():
            acc_ref[...] = jnp.zeros_like(acc_ref)
            for hh in range(g):
                p_sink = jnp.exp(sk_ref[hh][:, :1] - lse_ref[hh])
                total = jnp.broadcast_to(-jnp.sum(p_sink * dl_ref[hh], axis=0, keepdims=True), (1, LANES))
                dsink_ref[hh] += jnp.where(lax.broadcasted_iota(jnp.int32, (1, LANES), 1) == 0, total, 0.0)

        @pl.when(in_range(i, j))
        def _():
            for hh in range(g):
                p = jnp.exp(scores(q_ref, k_ref, b_ref, hh) - lse_ref[hh])
                dp = _dot(do_ref[hh], v_ref[...], NT)
                ds = p * (dp - dl_ref[hh])
                db_ref[hh, j] += ds
                acc_ref[hh] += _dot(ds * scale, k_ref[...], NN)

        @pl.when(j == 2)
        def _():
            dq_ref[...] = acc_ref[...].astype(dq_ref.dtype)

    def dkv_body(q_ref, k_ref, v_ref, b_ref, do_ref, lse_ref, dl_ref, dk_ref, dv_ref, dk_acc, dv_acc):
        kb, jj = pl.program_id(1), pl.program_id(2)
        qi = kb + 1 - jj

        @pl.when(jj == 0)
        def _():
            dk_acc[...] = jnp.zeros_like(dk_acc)
            dv_acc[...] = jnp.zeros_like(dv_acc)

        @pl.when(jnp.logical_and(qi >= 0, qi < nb))
        def _():
            for hh in range(g):
                p = jnp.exp(scores(q_ref, k_ref, b_ref, hh) - lse_ref[hh])
                dv_acc[...] += _dot(p, do_ref[hh], TN)
                dp = _dot(do_ref[hh], v_ref[...], NT)
                ds = p * (dp - dl_ref[hh]) * scale
                dk_acc[...] += _dot(ds, q_ref[hh], TN)

        @pl.when(jj == 2)
        def _():
            dk_ref[...] = dk_acc[...].astype(dk_ref.dtype)
            dv_ref[...] = dv_acc[...].astype(dv_ref.dtype)

    def q_spec(width):
        return pl.BlockSpec((g, blk, width), lambda kv, i, j: (kv, i, 0))

    kv_spec = pl.BlockSpec((None, blk, dh), lambda kv, i, j: (kv, key_block(i, j), 0))
    b_spec = pl.BlockSpec((g, None, blk, blk), lambda kv, i, j: (kv, j, 0, 0))
    sk_spec = pl.BlockSpec((g, 1, LANES), lambda kv, i, j: (kv, 0, 0))

    def run_fwd(q, k, v, bias, sink_b):
        return _pallas(
            fwd_body, name=name + "_fwd", grid=(C_KV_HEADS, nb, 3), in_specs=[q_spec(dh), kv_spec, kv_spec, b_spec, sk_spec],
            out_specs=[q_spec(dh), q_spec(1)],
            out_shape=[jax.ShapeDtypeStruct((hq, s, dh), q.dtype), jax.ShapeDtypeStruct((hq, s, 1), F32)],
            scratch_shapes=[pltpu.VMEM((g, blk, 1), F32), pltpu.VMEM((g, blk, 1), F32), pltpu.VMEM((g, blk, dh), F32)],
            compiler_params=_cparams("parallel", "parallel", "arbitrary"))(q, k, v, bias, sink_b)

    @jax.custom_vjp
    def f(q, k, v, bias, sink_b):
        return run_fwd(q, k, v, bias, sink_b)[0]

    def fwd(q, k, v, bias, sink_b):
        o, lse = run_fwd(q, k, v, bias, sink_b)
        return o, (q, k, v, bias, sink_b, o, lse)

    def bwd(res, do):
        q, k, v, bias, sink_b, o, lse = res
        delta = _rowdot(o, do, name + "_delta")
        dq, dbias, dsink = _pallas(
            dq_body, name=name + "_dq", grid=(C_KV_HEADS, nb, 3),
            in_specs=[q_spec(dh), kv_spec, kv_spec, b_spec, sk_spec, q_spec(dh), q_spec(1), q_spec(1)],
            out_specs=[q_spec(dh), pl.BlockSpec((g, 3, blk, blk), lambda kv, i, j: (kv, 0, 0, 0)), sk_spec],
            out_shape=[jax.ShapeDtypeStruct((hq, s, dh), q.dtype), jax.ShapeDtypeStruct((hq, 3, blk, blk), F32),
                       jax.ShapeDtypeStruct((hq, 1, LANES), F32)],
            scratch_shapes=[pltpu.VMEM((g, blk, dh), F32)],
            compiler_params=_cparams("arbitrary", "arbitrary", "arbitrary"))(q, k, v, bias, sink_b, do, lse, delta)

        def qs(width):
            return pl.BlockSpec((g, blk, width), lambda kv, kb, jj: (kv, jnp.clip(kb + 1 - jj, 0, nb - 1), 0))

        ks = pl.BlockSpec((None, blk, dh), lambda kv, kb, jj: (kv, kb, 0))
        bs = pl.BlockSpec((g, None, blk, blk), lambda kv, kb, jj: (kv, jj, 0, 0))
        dk, dv_ = _pallas(
            dkv_body, name=name + "_dkv", grid=(C_KV_HEADS, nb, 3),
            in_specs=[qs(dh), ks, ks, bs, qs(dh), qs(1), qs(1)], out_specs=[ks, ks],
            out_shape=[jax.ShapeDtypeStruct(k.shape, k.dtype), jax.ShapeDtypeStruct(v.shape, v.dtype)],
            scratch_shapes=[pltpu.VMEM((blk, dh), F32), pltpu.VMEM((blk, dh), F32)],
            compiler_params=_cparams("parallel", "parallel", "arbitrary"))(q, k, v, bias, do, lse, delta)
        return dq, dk, dv_, dbias, dsink

    f.defvjp(fwd, bwd)
    return f(q, k, v, bias, sink_b)


HG_PREP_ROWS = 256
HG_PAIR = 2
HG_INTRA_BLOCK = 256
HG_INTER_CHUNKS = 16
MLA_TQ, MLA_TK = 1024, 1024
CROSS_TQ = 1024


def _hdot(a, b, dims):
    return lax.dot_general(a, b, (dims, ((), ())), precision=lax.Precision.HIGHEST, preferred_element_type=F32)


def hgrn_prep(q, z, lb, reverse, name):
    s, c = q.shape
    tb = _tile(s, HG_PREP_ROWS)
    tc = HG_PAIR * B_DK
    ncb = tb // B_CHUNK

    def chunk_matrices():
        r = lax.broadcasted_iota(jnp.int32, (tb, tb), 0)
        cc = lax.broadcasted_iota(jnp.int32, (tb, tb), 1)
        same = r // B_CHUNK == cc // B_CHUNK
        tri = (cc >= r) if reverse else (cc <= r)
        cum = jnp.where(jnp.logical_and(same, tri), 1.0, 0.0).astype(F32)
        every = jnp.where(same, 1.0, 0.0).astype(F32)
        pr = lax.broadcasted_iota(jnp.int32, (ncb, tb), 0)
        pc = lax.broadcasted_iota(jnp.int32, (ncb, tb), 1)
        per_chunk = jnp.where(pc // B_CHUNK == pr, 1.0, 0.0).astype(F32)
        return cum, every, per_chunk

    def gates(zv, lbv):
        e = jnp.exp(-jnp.abs(zv))
        big, small = 1.0 / (1.0 + e), e / (1.0 + e)
        sig = jnp.where(zv >= 0, big, small)
        nsig = jnp.where(zv >= 0, small, big)
        f = lbv + (1.0 - lbv) * sig
        return sig, nsig, f, jnp.log(jnp.maximum(f, TINY)), (1.0 - lbv) * nsig

    def fwd_body(q_ref, z_ref, lb_ref, qd_ref, ki_ref, ke_ref, dec_ref):
        cum, every, per_chunk = chunk_matrices()
        _, _, _, lf, key = gates(z_ref[...], lb_ref[...])
        b = _hdot(cum, lf, NN)
        tot = _hdot(every, lf, NN)
        qd_ref[...] = q_ref[...] * jnp.exp(b)
        ki_ref[...] = key * jnp.exp(-b)
        ke_ref[...] = key * jnp.exp(tot - b)
        dec_ref[...] = jnp.exp(_hdot(per_chunk, lf, NN))

    def bwd_body(q_ref, z_ref, lb_ref, dqd_ref, dki_ref, dke_ref, ddec_ref, dq_ref, dz_ref, dlb_ref):
        cum, every, per_chunk = chunk_matrices()
        lbv = lb_ref[...]
        sig, nsig, f, lf, key = gates(z_ref[...], lbv)
        b = _hdot(cum, lf, NN)
        tot = _hdot(every, lf, NN)
        e_b, e_nb, e_tb = jnp.exp(b), jnp.exp(-b), jnp.exp(tot - b)
        dqd, dki, dke = dqd_ref[...], dki_ref[...], dke_ref[...]
        dq_ref[...] = dqd * e_b
        dkey = dki * e_nb + dke * e_tb
        t_end = dke * key * e_tb
        db = dqd * q_ref[...] * e_b - dki * key * e_nb - t_end
        dtot = ddec_ref[...] * jnp.exp(_hdot(per_chunk, lf, NN)) + _hdot(per_chunk, t_end, NN)
        dlf = _hdot(cum, db, TN) + _hdot(per_chunk, dtot, TN)
        df = jnp.where(f > TINY, dlf / f, 0.0)
        one_m_lb = 1.0 - lbv
        dz_ref[...] = (df - dkey) * one_m_lb * sig * nsig
        dlb_part = jnp.sum(df * nsig - dkey * nsig, axis=0, keepdims=True)

        @pl.when(pl.program_id(1) == 0)
        def _():
            dlb_ref[...] = jnp.zeros_like(dlb_ref)

        dlb_ref[...] += dlb_part

    tok = pl.BlockSpec((tb, tc), lambda j, i: (i, j))
    vec = pl.BlockSpec((1, tc), lambda j, i: (0, j))
    chk = pl.BlockSpec((ncb, tc), lambda j, i: (i, j))
    grid = (c // tc, s // tb)
    tok_shape = jax.ShapeDtypeStruct((s, c), F32)
    chk_shape = jax.ShapeDtypeStruct((s // B_CHUNK, c), F32)

    def run_fwd(q, z, lb):
        return _pallas(fwd_body, name=name + "_fwd", grid=grid, in_specs=[tok, tok, vec], out_specs=[tok, tok, tok, chk],
                       out_shape=[tok_shape, tok_shape, tok_shape, chk_shape],
                       compiler_params=_cparams("parallel", "parallel"))(q, z, lb)

    @jax.custom_vjp
    def f(q, z, lb):
        return tuple(run_fwd(q, z, lb))

    def fwd(q, z, lb):
        return tuple(run_fwd(q, z, lb)), (q, z, lb)

    def bwd(res, cts):
        q, z, lb = res
        dq, dz, dlb = _pallas(
            bwd_body, name=name + "_bwd", grid=grid, in_specs=[tok, tok, vec, tok, tok, tok, chk], out_specs=[tok, tok, vec],
            out_shape=[tok_shape, tok_shape, jax.ShapeDtypeStruct((1, c), F32)],
            compiler_params=_cparams("parallel", "arbitrary"))(q, z, lb, *cts)
        return dq, dz, dlb

    f.defvjp(fwd, bwd)
    return f(q, z, lb)


def _pair_cols(ref, hh, width):
    return ref[:, hh * width:(hh + 1) * width]


def hgrn_intra(qd, ki, v, reverse, name):
    s = qd.shape[0]
    tb = _tile(s, HG_INTRA_BLOCK)
    wk, wv = HG_PAIR * B_DK, HG_PAIR * B_DV

    def mask():
        r = lax.broadcasted_iota(jnp.int32, (tb, tb), 0)
        c = lax.broadcasted_iota(jnp.int32, (tb, tb), 1)
        return jnp.logical_and(r // B_CHUNK == c // B_CHUNK, (c >= r) if reverse else (c <= r))

    def fwd_body(q_ref, k_ref, v_ref, o_ref):
        msk = mask()
        for hh in range(HG_PAIR):
            sc = jnp.where(msk, _dot(_pair_cols(q_ref, hh, B_DK), _pair_cols(k_ref, hh, B_DK), NT), 0.0)
            o_ref[:, hh * B_DV:(hh + 1) * B_DV] = _dot(sc, _pair_cols(v_ref, hh, B_DV), NN)

    def bwd_body(q_ref, k_ref, v_ref, do_ref, dq_ref, dk_ref, dv_ref):
        msk = mask()
        for hh in range(HG_PAIR):
            q, k = _pair_cols(q_ref, hh, B_DK), _pair_cols(k_ref, hh, B_DK)
            vv, do = _pair_cols(v_ref, hh, B_DV), _pair_cols(do_ref, hh, B_DV)
            sc = jnp.where(msk, _dot(q, k, NT), 0.0)
            ds = jnp.where(msk, _dot(do, vv, NT), 0.0)
            dq_ref[:, hh * B_DK:(hh + 1) * B_DK] = _dot(ds, k, NN)
            dk_ref[:, hh * B_DK:(hh + 1) * B_DK] = _dot(ds, q, TN)
            dv_ref[:, hh * B_DV:(hh + 1) * B_DV] = _dot(sc, do, TN)

    ks = pl.BlockSpec((tb, wk), lambda hp, i: (i, hp))
    vs = pl.BlockSpec((tb, wv), lambda hp, i: (i, hp))
    grid = (B_HEADS // HG_PAIR, s // tb)

    def run_fwd(qd, ki, v):
        return _pallas(fwd_body, name=name + "_fwd", grid=grid, in_specs=[ks, ks, vs], out_specs=vs,
                       out_shape=jax.ShapeDtypeStruct(v.shape, F32), compiler_params=_cparams("parallel", "parallel"))(qd, ki, v)

    @jax.custom_vjp
    def f(qd, ki, v):
        return run_fwd(qd, ki, v)

    def fwd(qd, ki, v):
        return run_fwd(qd, ki, v), (qd, ki, v)

    def bwd(res, do):
        qd, ki, v = res
        return tuple(_pallas(
            bwd_body, name=name + "_bwd", grid=grid, in_specs=[ks, ks, vs, vs], out_specs=[ks, ks, vs],
            out_shape=[jax.ShapeDtypeStruct(qd.shape, F32), jax.ShapeDtypeStruct(ki.shape, F32),
                       jax.ShapeDtypeStruct(v.shape, F32)],
            compiler_params=_cparams("parallel", "parallel"))(qd, ki, v, do))

    f.defvjp(fwd, bwd)
    return f(qd, ki, v)


def hgrn_inter(qd, ke, v, dec, reverse, name):
    s = qd.shape[0]
    nc = s // B_CHUNK
    cpb = HG_INTER_CHUNKS if nc % HG_INTER_CHUNKS == 0 else nc
    tb = cpb * B_CHUNK
    nblk = nc // cpb
    wk, wv = HG_PAIR * B_DK, HG_PAIR * B_DV
    n_hp = B_HEADS // HG_PAIR

    def rows(c):
        return pl.ds(c * B_CHUNK, B_CHUNK)

    def kcols(hh):
        return slice(hh * B_DK, (hh + 1) * B_DK)

    def vcols(hh):
        return slice(hh * B_DV, (hh + 1) * B_DV)

    def order(flip):
        return reversed(range(cpb)) if flip else range(cpb)

    def fwd_body(q_ref, k_ref, v_ref, dec_ref, o_ref, st_ref, state):
        @pl.when(pl.program_id(1) == 0)
        def _():
            state[...] = jnp.zeros_like(state)

        for c in order(reverse):
            for hh in range(HG_PAIR):
                st = state[hh]
                st_ref[c, hh] = st
                o_ref[rows(c), vcols(hh)] = _dot(q_ref[rows(c), kcols(hh)], st, NT)
                state[hh] = st * dec_ref[pl.ds(c, 1), kcols(hh)] + _dot(v_ref[rows(c), vcols(hh)], k_ref[rows(c), kcols(hh)], TN)

    def bwd_body(q_ref, k_ref, v_ref, dec_ref, st_ref, do_ref, dq_ref, dk_ref, dv_ref, ddec_ref, dstate):
        @pl.when(pl.program_id(1) == 0)
        def _():
            dstate[...] = jnp.zeros_like(dstate)

        for c in order(not reverse):
            for hh in range(HG_PAIR):
                dst = dstate[hh]
                st = st_ref[c, hh]
                do_c = do_ref[rows(c), vcols(hh)]
                dk_ref[rows(c), kcols(hh)] = _dot(v_ref[rows(c), vcols(hh)], dst, NN)
                dv_ref[rows(c), vcols(hh)] = _dot(k_ref[rows(c), kcols(hh)], dst, NT)
                ddec_ref[pl.ds(c, 1), kcols(hh)] = jnp.sum(dst * st, axis=0, keepdims=True)
                dq_ref[rows(c), kcols(hh)] = _dot(do_c, st, NN)
                dstate[hh] = dst * dec_ref[pl.ds(c, 1), kcols(hh)] + _dot(do_c, q_ref[rows(c), kcols(hh)], TN)

    def specs(flip):
        blk = (lambda i: nblk - 1 - i) if flip else (lambda i: i)
        tok_k = pl.BlockSpec((tb, wk), lambda hp, i: (blk(i), hp))
        tok_v = pl.BlockSpec((tb, wv), lambda hp, i: (blk(i), hp))
        chk = pl.BlockSpec((cpb, wk), lambda hp, i: (blk(i), hp))
        sts = pl.BlockSpec((cpb, HG_PAIR, B_DV, B_DK), lambda hp, i: (blk(i), hp, 0, 0))
        return tok_k, tok_v, chk, sts

    scratch = [pltpu.VMEM((HG_PAIR, B_DV, B_DK), F32)]

    def run_fwd(qd, ke, v, dec):
        tok_k, tok_v, chk, sts = specs(reverse)
        return _pallas(
            fwd_body, name=name + "_fwd", grid=(n_hp, nblk), in_specs=[tok_k, tok_k, tok_v, chk], out_specs=[tok_v, sts],
            out_shape=[jax.ShapeDtypeStruct(v.shape, F32), jax.ShapeDtypeStruct((nc, B_HEADS, B_DV, B_DK), F32)],
            scratch_shapes=scratch, compiler_params=_cparams("parallel", "arbitrary"))(qd, ke, v, dec)

    @jax.custom_vjp
    def f(qd, ke, v, dec):
        return run_fwd(qd, ke, v, dec)[0]

    def fwd(qd, ke, v, dec):
        o, st = run_fwd(qd, ke, v, dec)
        return o, (qd, ke, v, dec, st)

    def bwd(res, do):
        qd, ke, v, dec, st = res
        tok_k, tok_v, chk, sts = specs(not reverse)
        return tuple(_pallas(
            bwd_body, name=name + "_bwd", grid=(n_hp, nblk), in_specs=[tok_k, tok_k, tok_v, chk, sts, tok_v],
            out_specs=[tok_k, tok_k, tok_v, chk],
            out_shape=[jax.ShapeDtypeStruct(qd.shape, F32), jax.ShapeDtypeStruct(ke.shape, F32),
                       jax.ShapeDtypeStruct(v.shape, F32), jax.ShapeDtypeStruct(dec.shape, F32)],
            scratch_shapes=scratch, compiler_params=_cparams("parallel", "arbitrary"))(qd, ke, v, dec, st, do))

    f.defvjp(fwd, bwd)
    return f(qd, ke, v, dec)


def loss_head(y, target, name="loss"):
    s, d = y.shape
    tr = _row_tile(s, d)

    def body(y_ref, t_ref, o_ref):
        @pl.when(pl.program_id(0) == 0)
        def _():
            o_ref[...] = jnp.zeros_like(o_ref)

        e = y_ref[...] - t_ref[...]
        part = jnp.sum(jnp.sum(e * e, axis=-1, keepdims=True), axis=0, keepdims=True) * (0.5 / d)
        o_ref[...] += jnp.broadcast_to(part, o_ref.shape)

    spec = pl.BlockSpec((tr, d), lambda i: (i, 0))

    def run(y, t):
        out = _pallas(body, name=name, grid=(s // tr,), in_specs=[spec, spec],
                      out_specs=pl.BlockSpec((SUBLANES, LANES), lambda i: (0, 0)),
                      out_shape=jax.ShapeDtypeStruct((SUBLANES, LANES), F32), compiler_params=_cparams("arbitrary"))(y, t)
        return out[0, 0]

    @jax.custom_vjp
    def f(y, t):
        return run(y, t)

    def fwd(y, t):
        return run(y, t), (y, t)

    def bwd(res, g):
        y, t = res
        dy = g * (y - t) * (1.0 / d)
        return dy, -dy

    f.defvjp(fwd, bwd)
    return f(y, target)


def _mesh_pos():
    return lax.axis_index("x"), lax.axis_index("y"), lax.axis_index("c")


def all_gather_shards(shards):
    n = len(shards)

    def body(*refs):
        ins, outs = refs[:n], refs[n:2 * n]
        send_sems, recv_sems, local_sems = refs[2 * n:]
        x, y, c = _mesh_pos()
        me, sibling = (x, y, c), (x, y, 1 - c)
        chips = [(1 - x, y), (x, 1 - y), (1 - x, 1 - y)]

        def slot(t, px, py, pc):
            return outs[t].at[4 * px + 2 * py + pc]

        def copy(t, k, block, to, src=None):
            return pltpu.make_async_remote_copy(
                src_ref=slot(t, *block) if src is None else src, dst_ref=slot(t, *block), send_sem=send_sems.at[t, k],
                recv_sem=recv_sems.at[t, k], device_id=to, device_id_type=pl.DeviceIdType.MESH)

        mine = [pltpu.make_async_copy(ins[t], slot(t, *me), local_sems.at[t]) for t in range(n)]
        for cp in mine:
            cp.start()
        first = []
        for t in range(n):
            first.append(copy(t, 0, me, sibling, src=ins[t]))
            first += [copy(t, 1 + j, me, (*chip, c), src=ins[t]) for j, chip in enumerate(chips)]
        for cp in first:
            cp.start()
        passed = []
        for j, chip in enumerate(chips):
            for t in range(n):
                copy(t, 1 + j, (*chip, c), me).wait_recv()
                cp = copy(t, 4 + j, (*chip, c), sibling)
                cp.start()
                passed.append(cp)
        for t in range(n):
            copy(t, 0, sibling, me).wait_recv()
            for j, chip in enumerate(chips):
                copy(t, 4 + j, (*chip, 1 - c), me).wait_recv()
        for cp in first + passed:
            cp.wait_send()
        for cp in mine:
            cp.wait()

    any_spec = pl.BlockSpec(memory_space=pl.ANY)
    return _pallas(
        body, name="all_gather_weights", out_shape=[jax.ShapeDtypeStruct((N_DEV, *s.shape), s.dtype) for s in shards],
        in_specs=[any_spec] * n, out_specs=[any_spec] * n,
        scratch_shapes=[pltpu.SemaphoreType.DMA((n, 7)), pltpu.SemaphoreType.DMA((n, 7)), pltpu.SemaphoreType.DMA((n,))],
    )(*shards)


def all_to_all_blocks(stacks):
    n = len(stacks)

    def body(*refs):
        ins, outs = refs[:n], refs[n:2 * n]
        send_sems, recv_sems, local_sems = refs[2 * n:]
        x, y, c = _mesh_pos()
        me = 4 * x + 2 * y + c
        mine = [pltpu.make_async_copy(ins[t].at[me], outs[t].at[me], local_sems.at[t]) for t in range(n)]
        for cp in mine:
            cp.start()
        copies = []
        for k in range(1, N_DEV):
            px = 1 - x if k & 4 else x
            py = 1 - y if k & 2 else y
            pc = 1 - c if k & 1 else c
            for t in range(n):
                cp = pltpu.make_async_remote_copy(
                    src_ref=ins[t].at[4 * px + 2 * py + pc], dst_ref=outs[t].at[me], send_sem=send_sems.at[t, k - 1],
                    recv_sem=recv_sems.at[t, k - 1], device_id=(px, py, pc), device_id_type=pl.DeviceIdType.MESH)
                cp.start()
                copies.append(cp)
        for cp in copies:
            cp.wait_recv()
        for cp in copies:
            cp.wait_send()
        for cp in mine:
            cp.wait()

    any_spec = pl.BlockSpec(memory_space=pl.ANY)
    return _pallas(
        body, name="all_to_all_grads", out_shape=[jax.ShapeDtypeStruct(s.shape, s.dtype) for s in stacks],
        in_specs=[any_spec] * n, out_specs=[any_spec] * n,
        scratch_shapes=[pltpu.SemaphoreType.DMA((n, 7)), pltpu.SemaphoreType.DMA((n, 7)), pltpu.SemaphoreType.DMA((n,))],
    )(*stacks)


def all_gather_small(v):
    r, w = v.shape

    def body(x_ref, out_ref, send_sems, recv_sems):
        x, y, c = _mesh_pos()
        me = 4 * x + 2 * y + c
        copies = []
        for k in range(1, N_DEV):
            px = 1 - x if k & 4 else x
            py = 1 - y if k & 2 else y
            pc = 1 - c if k & 1 else c
            cp = pltpu.make_async_remote_copy(
                src_ref=x_ref, dst_ref=out_ref.at[me], send_sem=send_sems.at[k - 1], recv_sem=recv_sems.at[k - 1],
                device_id=(px, py, pc), device_id_type=pl.DeviceIdType.MESH)
            cp.start()
            copies.append(cp)
        out_ref[me] = x_ref[...]
        for cp in copies:
            cp.wait_recv()
        for cp in copies:
            cp.wait_send()

    vmem = pl.BlockSpec(memory_space=pltpu.VMEM)
    return _pallas(
        body, name="all_gather_small", out_shape=jax.ShapeDtypeStruct((N_DEV, r, w), v.dtype), in_specs=[vmem],
        out_specs=vmem, scratch_shapes=[pltpu.SemaphoreType.DMA((7,)), pltpu.SemaphoreType.DMA((7,))],
    )(v)


def adamw_rows(parts, w, m, v, name):
    n, r, lanes = parts.shape
    tr = _tile(r, max(SUBLANES, (256 * 1024) // lanes), SUBLANES)
    c1 = 1.0 / (1.0 - ADAM_B1 ** ADAM_STEP)
    c2 = 1.0 / (1.0 - ADAM_B2 ** ADAM_STEP)

    def body(p_ref, w_ref, m_ref, v_ref, g_ref, d_ref, nm_ref, nv_ref):
        g = p_ref[0]
        for j in range(1, n):
            g = g + p_ref[j]
        nm = ADAM_B1 * m_ref[...] + (1.0 - ADAM_B1) * g
        nv = ADAM_B2 * v_ref[...] + (1.0 - ADAM_B2) * (g * g)
        g_ref[...] = g
        nm_ref[...] = nm
        nv_ref[...] = nv
        d_ref[...] = -ADAM_LR * ((nm * c1) / (jnp.sqrt(nv * c2) + ADAM_EPS) + ADAM_WD * w_ref[...])

    row = pl.BlockSpec((tr, lanes), lambda i: (i, 0))
    out = jax.ShapeDtypeStruct((r, lanes), F32)
    return _pallas(body, name=name, grid=(r // tr,), in_specs=[pl.BlockSpec((n, tr, lanes), lambda i: (0, i, 0)), row, row, row],
                   out_specs=[row, row, row, row], out_shape=[out, out, out, out], compiler_params=_cparams("parallel"))(
        parts, w, m, v)


def _padded(n):
    return -(-n // PACK_QUANTUM) * PACK_QUANTUM


def _pack(pieces, total_rows=None):
    flat = []
    for p in pieces:
        p = p.reshape(-1).astype(F32)
        flat.append(jnp.pad(p, (0, _padded(p.size) - p.size)))
    out = jnp.concatenate(flat).reshape(-1, LANES)
    if total_rows is not None and out.shape[0] != total_rows:
        out = jnp.pad(out, ((0, total_rows - out.shape[0]), (0, 0)))
    return out


def _pack_rows(sizes):
    rows = sum(_padded(n) for n in sizes) // LANES
    return -(-rows // PACK_ROW_TILE) * PACK_ROW_TILE


def _unpack(rows, shapes):
    lead = rows.shape[:-2]
    flat = rows.reshape(*lead, -1)
    out, off = [], 0
    for shp in shapes:
        n = int(np.prod(shp))
        out.append(flat[..., off:off + n].reshape(*lead, *shp))
        off += _padded(n)
    return out


def _shards_to_full(stacked, axis):
    moved = jnp.moveaxis(stacked, 0, axis)
    shp = list(stacked.shape[1:])
    shp[axis] *= N_DEV
    return moved.reshape(shp)


def _full_to_shards(full, axis):
    shp = list(full.shape)
    shp[axis:axis + 1] = [N_DEV, shp[axis] // N_DEV]
    return jnp.moveaxis(full.reshape(shp), axis, 0)


def _heads(t, n, d):
    return jnp.transpose(t.reshape(t.shape[0], n, d), (1, 0, 2)).astype(ACT_DTYPE)


def _unheads(t):
    return jnp.transpose(t, (1, 0, 2)).reshape(t.shape[1], -1)


def _rope_tables(s):
    half = A_ROPE // 2
    inv = ROPE_THETA ** (-jnp.arange(half, dtype=F32) / half)
    ang = jnp.arange(s, dtype=jnp.int32).astype(F32)[:, None] * inv[None, :]
    return jnp.cos(ang), jnp.sin(ang)


def _rope(t, cos, sin):
    half = A_ROPE // 2
    t1, t2 = t[..., :half], t[..., half:]
    c, sn = cos[:, None, :], sin[:, None, :]
    return jnp.concatenate([t1 * c - t2 * sn, t1 * sn + t2 * c], axis=-1)


def _t5_bucket(rel):
    nb = REL_BUCKETS // 2
    max_exact = nb // 2
    ret = (rel > 0).astype(jnp.int32) * nb
    n = jnp.abs(rel)
    large = max_exact + (jnp.log(jnp.maximum(n, 1).astype(F32) / max_exact)
                         / math.log(REL_MAX_DIST / max_exact) * (nb - max_exact)).astype(jnp.int32)
    large = jnp.minimum(large, nb - 1)
    return ret + jnp.where(n < max_exact, n, large)


def _window_bias(rel_bias):
    span = 3 * C_BLOCK
    rel = jnp.arange(span)[None, :] - C_BLOCK - jnp.arange(C_BLOCK)[:, None]
    onehot = (_t5_bucket(rel)[..., None] == jnp.arange(REL_BUCKETS)).astype(F32)
    bias = jnp.einsum("qkb,bh->hqk", onehot, rel_bias.astype(F32), precision=lax.Precision.HIGHEST)
    bias = jnp.where((jnp.abs(rel) <= C_WINDOW)[None], bias, NEG)
    return jnp.transpose(bias.reshape(C_HEADS, C_BLOCK, 3, C_BLOCK), (0, 2, 1, 3))


def _mla(cq, ckv, kr, gq, gkv, wuq, wukv, cos, sin):
    s = cq.shape[0]
    q = linear(rmsnorm(cq, gq, ACT_DTYPE, "rms_cq"), wuq, name="a_wuq").reshape(s, A_HEADS, A_NOPE + A_ROPE)
    q = jnp.concatenate([q[..., :A_NOPE], _rope(q[..., A_NOPE:], cos, sin)], axis=-1)
    kv = linear(rmsnorm(ckv, gkv, ACT_DTYPE, "rms_ckv"), wukv, name="a_wukv").reshape(s, A_HEADS, A_NOPE + A_V)
    k_rope = jnp.broadcast_to(_rope(kr[:, None, :], cos, sin), (s, A_HEADS, A_ROPE))
    k = jnp.concatenate([kv[..., :A_NOPE], k_rope], axis=-1)
    v = kv[..., A_NOPE:]
    return mla_attention(q, k, v, (A_NOPE + A_ROPE) ** -0.5)


def _hgrn2(q, f_fwd, f_bwd, i, g, lb_fwd, lb_bwd, g_out):
    s = q.shape[0]
    o = None
    for z, lb, rev, tag in ((f_fwd, lb_fwd, False, "hgf"), (f_bwd, lb_bwd, True, "hgb")):
        qd, ki, ke, dec = hgrn_prep(q, z, lb.astype(F32).reshape(1, -1), rev, tag + "_prep")
        part = hgrn_intra(qd, ki, i, rev, tag + "_intra") + hgrn_inter(qd, ke, i, dec, rev, tag + "_inter")
        o = part if o is None else o + part
    o = rmsnorm(o.reshape(s * B_HEADS, B_DV), g_out, F32, "rms_hg").reshape(s, B_HEADS * B_DV)
    return o * jax.nn.silu(g)


def _cross(h, mem_n, wq, wkv, wo):
    q = _heads(linear(h, wq, name="x_wq"), X_HEADS, X_DH)
    kv = linear(mem_n, wkv, name="x_wkv").reshape(mem_n.shape[0], 2, X_HEADS, X_DH)
    k = jnp.transpose(kv[:, 0], (1, 0, 2)).astype(ACT_DTYPE)
    v = jnp.transpose(kv[:, 1], (1, 0, 2)).astype(ACT_DTYPE)
    o = attention(q, k, v, X_DH ** -0.5, CROSS_TQ, 256, "cross")
    return linear(_unheads(o), wo, name="x_wo")


def _pad_w_in(w):
    cut = A_Q_RANK + A_KV_RANK + A_ROPE
    return jnp.concatenate([w[:, :cut], jnp.zeros((w.shape[0], KR_PAD), w.dtype), w[:, cut:]], axis=1)


def _model_loss(p, x, mem, target):
    s = x.shape[0]
    cos, sin = _rope_tables(s)
    sm = jax.nn.softmax(p["b_lb"].astype(F32), axis=1)
    lower_bounds = jnp.cumsum(sm, axis=1) - sm[:, :1]
    bias = _window_bias(p["rel_bias"])
    for l in range(DEPTH):
        h = rmsnorm(x, p["g_mix"][l], ACT_DTYPE, "rms_mix")
        z = linear(h, _pad_w_in(p["w_in"][l]), name="w_in")
        parts, start = [], 0
        for width in IN_SPLITS_PADDED:
            parts.append(z[:, start:start + width])
            start += width
        a_cq, a_ckv, a_kr, b_q, b_ff, b_fb, b_i, b_g, c_q, c_k, c_v, gate_a, gate_b, gate_c = parts
        y_a = _mla(a_cq, a_ckv, a_kr[:, :A_ROPE], p["a_gq"][l], p["a_gkv"][l], p["a_wuq"][l], p["a_wukv"][l], cos, sin)
        y_b = _hgrn2(b_q, b_ff, b_fb, b_i, b_g, lower_bounds[0, l], lower_bounds[1, l], p["b_gout"][l])
        y_c = _unheads(window_attention(_heads(c_q, C_HEADS, C_DH), _heads(c_k, C_KV_HEADS, C_DH),
                                        _heads(c_v, C_KV_HEADS, C_DH), bias, p["c_sink"][l]))
        merged = (jax.nn.sigmoid(gate_a) * linear_t(y_a, p["w_br_a"][l], name="w_br_a")
                  + jax.nn.sigmoid(gate_b) * linear(y_b, p["w_br_b"][l], name="w_br_b")
                  + jax.nn.sigmoid(gate_c) * linear(y_c, p["w_br_c"][l], name="w_br_c"))
        x = x + linear(merged, p["w_out"][l], name="w_out")
        h = rmsnorm(x, p["g_x"][l], ACT_DTYPE, "rms_x")
        x = x + _cross(h, rmsnorm(mem, p["g_mem"][l], ACT_DTYPE, "rms_mem"), p["x_wq"][l], p["x_wkv"][l], p["x_wo"][l])
        h = rmsnorm(x, p["g_ffn"][l], ACT_DTYPE, "rms_ffn")
        t = jax.nn.silu(linear(h, p["f_w1"][l], name="f_w1")) * linear(h, p["f_w3"][l], name="f_w3")
        x = x + linear(t, p["f_w2"][l], name="f_w2")
    y = rmsnorm(x, p["g_final"], F32, "rms_final")
    return loss_head(y, target)


def kernel(x, mem, w_in, g_mix, a_gq, a_gkv, a_wuq, a_wukv, b_lb, b_gout, c_sink, rel_bias, w_br_a, w_br_b, w_br_c, w_out, g_x, g_mem, x_wq, x_wkv, x_wo, g_ffn, f_w1, f_w3, f_w2, g_final, loss_target, m_w_in, m_g_mix, m_a_gq, m_a_gkv, m_a_wuq, m_a_wukv, m_b_lb, m_b_gout, m_c_sink, m_rel_bias, m_w_br_a, m_w_br_b, m_w_br_c, m_w_out, m_g_x, m_g_mem, m_x_wq, m_x_wkv, m_x_wo, m_g_ffn, m_f_w1, m_f_w3, m_f_w2, m_g_final, v_w_in, v_g_mix, v_a_gq, v_a_gkv, v_a_wuq, v_a_wukv, v_b_lb, v_b_gout, v_c_sink, v_rel_bias, v_w_br_a, v_w_br_b, v_w_br_c, v_w_out, v_g_x, v_g_mem, v_x_wq, v_x_wkv, v_x_wo, v_g_ffn, v_f_w1, v_f_w3, v_f_w2, v_g_final):
    given = dict(locals())
    w = {n: given[n] for n in WEIGHT_ORDER}
    m = {n: given["m_" + n] for n in WEIGHT_ORDER}
    v = {n: given["v_" + n] for n in WEIGHT_ORDER}
    sh_names = [n for n, _ in SHARDED]
    rep_shapes = [w[n].shape for n in REPLICATED] + [(1,)]
    rep_rows = _pack_rows([int(np.prod(s)) for s in rep_shapes])

    wire = [w[n] if n in ELEMENTWISE_SHARDED else w[n].astype(MXU_DTYPE) for n in sh_names]
    gathered = all_gather_shards(wire)
    full = {n: _shards_to_full(t, ax).astype(F32) for (n, ax), t in zip(SHARDED, gathered)}
    full.update({n: w[n] for n in REPLICATED})

    loss, (grad_full, grad_x) = jax.value_and_grad(_model_loss, argnums=(0, 1))(full, x[0], mem[0], loss_target[0])

    received = all_to_all_blocks([_full_to_shards(grad_full[n], ax) for n, ax in SHARDED])
    g_sh, d_sh, nm_sh, nv_sh = {}, {}, {}, {}
    for n, got in zip(sh_names, received):
        shp = w[n].shape
        rows = lambda t: t.reshape(-1, shp[-1])
        outs = adamw_rows(got.reshape(N_DEV, -1, shp[-1]), rows(w[n]), rows(m[n]), rows(v[n]), "adamw_" + n)
        g_sh[n], d_sh[n], nm_sh[n], nv_sh[n] = [o.reshape(shp) for o in outs]

    mine = _pack([grad_full[n] for n in REPLICATED] + [loss.reshape(1)], rep_rows)
    everyone = all_gather_small(mine)
    rep_w = [w[n] for n in REPLICATED] + [jnp.zeros((1,), F32)]
    outs = adamw_rows(everyone, _pack(rep_w, rep_rows), _pack([m[n] for n in REPLICATED] + [jnp.zeros((1,), F32)], rep_rows),
                      _pack([v[n] for n in REPLICATED] + [jnp.ones((1,), F32)], rep_rows), "adamw_replicated")
    rep_names = list(REPLICATED) + ["loss"]
    g_rp, d_rp, nm_rp, nv_rp = [dict(zip(rep_names, _unpack(o, rep_shapes))) for o in outs]

    def pick(sharded, replicated, n):
        return sharded[n] if n in sharded else replicated[n]

    return (g_rp["loss"].reshape(()), grad_x[None],
            *[pick(g_sh, g_rp, n) for n in WEIGHT_ORDER], *[pick(d_sh, d_rp, n) for n in WEIGHT_ORDER],
            *[pick(nm_sh, nm_rp, n) for n in WEIGHT_ORDER], *[pick(nv_sh, nv_rp, n) for n in WEIGHT_ORDER])
```

```python
import functools
import math

import jax
import jax.numpy as jnp
import numpy as np
from jax import lax
from jax.experimental import pallas as pl
from jax.experimental.pallas import tpu as pltpu

F32 = jnp.float32
MXU_DTYPE = jnp.bfloat16
ACT_DTYPE = jnp.bfloat16

V7X_VMEM_LIMIT_BYTES = 56 * 1024 * 1024
LANES = 128
SUBLANES = 8

N_DEV = 8
D_MODEL = 1024
DEPTH = 2
EPS = 1e-6
TINY = 1e-30
NEG = -1e30

A_HEADS, A_NOPE, A_ROPE, A_V, A_Q_RANK, A_KV_RANK = 8, 64, 32, 64, 384, 256
ROPE_THETA = 10000.0
B_HEADS, B_DK, B_DV, B_CHUNK = 8, 128, 64, 16
C_HEADS, C_KV_HEADS, C_DH, C_WINDOW, C_BLOCK = 8, 2, 64, 128, 128
REL_BUCKETS, REL_MAX_DIST = 32, 128
X_HEADS, X_DH = 4, 256
D_FF = 2816
IN_SPLITS = (A_Q_RANK, A_KV_RANK, A_ROPE, 1024, 1024, 1024, 512, 512, 512, 128, 128, 1024, 1024, 1024)
IN_WIDTH = sum(IN_SPLITS)
KR_PAD = LANES - A_ROPE
IN_SPLITS_PADDED = (A_Q_RANK, A_KV_RANK, LANES, 1024, 1024, 1024, 512, 512, 512, 128, 128, 1024, 1024, 1024)

ADAM_LR, ADAM_B1, ADAM_B2, ADAM_EPS, ADAM_WD, ADAM_STEP = 0.001, 0.9, 0.999, 1e-08, 0.01, 10

SHARDED = (("w_in", 2), ("a_wuq", 2), ("a_wukv", 2), ("b_lb", 2), ("w_br_a", 2), ("w_br_b", 2), ("w_br_c", 2),
           ("w_out", 1), ("x_wq", 1), ("x_wkv", 2), ("x_wo", 1), ("f_w1", 2), ("f_w3", 2), ("f_w2", 1))
ELEMENTWISE_SHARDED = ("b_lb",)
REPLICATED = ("g_mix", "a_gq", "a_gkv", "b_gout", "c_sink", "rel_bias", "g_x", "g_mem", "g_ffn", "g_final")
WEIGHT_ORDER = ("w_in", "g_mix", "a_gq", "a_gkv", "a_wuq", "a_wukv", "b_lb", "b_gout", "c_sink", "rel_bias", "w_br_a",
                "w_br_b", "w_br_c", "w_out", "g_x", "g_mem", "x_wq", "x_wkv", "x_wo", "g_ffn", "f_w1", "f_w3", "f_w2",
                "g_final")
PACK_QUANTUM = SUBLANES * LANES
PACK_ROW_TILE = 512


def _pallas(body, **kw):
    return pl.pallas_call(body, **kw)


def _cparams(*sem):
    return pltpu.CompilerParams(dimension_semantics=sem, vmem_limit_bytes=V7X_VMEM_LIMIT_BYTES)


def _tile(n, target, mult=LANES):
    t = (min(target, n) // mult) * mult
    while t >= mult:
        if n % t == 0:
            return t
        t -= mult
    return n


def _dot(a, b, dims):
    return lax.dot_general(a.astype(MXU_DTYPE), b.astype(MXU_DTYPE), (dims, ((), ())), preferred_element_type=F32)


NN = ((1,), (0,))
NT = ((1,), (1,))
TN = ((0,), (0,))


MM_VMEM_BUDGET_BYTES = 40 * 1024 * 1024
MM_MAX_TILE = 4352
MM_MAX_ROW_TILE = 2048
MM_HBM_BYTES_PER_S = 2.5e12
MM_STEP_S = 0.4e-6
MM_DMA_ROW_OVERHEAD_BYTES = 512.0


def _tile_options(n, cap):
    out = [t for t in range(LANES, min(n, cap) + 1, LANES) if n % t == 0]
    if n <= cap and n not in out:
        out.append(n)
    return out or [n]


@functools.lru_cache(maxsize=None)
def _mm_plan(m, n, k, ta, tb, a_bytes, b_bytes, o_bytes):
    best = None
    for tk in _tile_options(k, MM_MAX_TILE):
        nk = k // tk
        for tn in _tile_options(n, MM_MAX_TILE):
            for tm in _tile_options(m, MM_MAX_ROW_TILE):
                vmem = 2 * (tm * tk * a_bytes + tk * tn * b_bytes + tm * tn * o_bytes) + tm * tn * 4
                vmem += (tm * tk * 2 if a_bytes == 4 else 0) + (tk * tn * 2 if b_bytes == 4 else 0)
                if vmem > MM_VMEM_BUDGET_BYTES:
                    continue

                def eff(elems, nbytes):
                    return (elems * nbytes) / (elems * nbytes + MM_DMA_ROW_OVERHEAD_BYTES)

                ea, eb, eo = eff(tm if ta else tk, a_bytes), eff(tk if tb else tn, b_bytes), eff(tn, o_bytes)
                for order in ("mn", "nm"):
                    if nk == 1 and order == "nm":
                        a_tr, b_tr = m * k * a_bytes * (n // tn), k * n * b_bytes
                    elif nk == 1:
                        a_tr, b_tr = m * k * a_bytes, k * n * b_bytes * (m // tm)
                    else:
                        a_tr, b_tr = m * k * a_bytes * (n // tn), k * n * b_bytes * (m // tm)
                    steps = (m // tm) * (n // tn) * nk
                    cost = (a_tr / ea + b_tr / eb + m * n * o_bytes / eo) / MM_HBM_BYTES_PER_S + steps * MM_STEP_S
                    if best is None or cost < best[0]:
                        best = (cost, tm, tn, tk, order)
    assert best is not None, (m, n, k)
    return best[1:]


def _mm(a, b, ta=False, tb=False, out_dtype=F32, name="mm"):
    m, k = (a.shape[1], a.shape[0]) if ta else a.shape
    kb, n = (b.shape[1], b.shape[0]) if tb else b.shape
    assert k == kb, (a.shape, b.shape, ta, tb)
    tm, tn, tk, order = _mm_plan(m, n, k, ta, tb, a.dtype.itemsize, b.dtype.itemsize, jnp.dtype(out_dtype).itemsize)
    nk = k // tk
    dims = ((0 if ta else 1,), (1 if tb else 0,))
    out_shape = jax.ShapeDtypeStruct((m, n), out_dtype)

    if nk == 1:
        def body(a_ref, b_ref, o_ref):
            o_ref[...] = _dot(a_ref[...], b_ref[...], dims).astype(o_ref.dtype)

        if order == "nm":
            mi, ni = (lambda j, i: i), (lambda j, i: j)
            grid = (n // tn, m // tm)
        else:
            mi, ni = (lambda i, j: i), (lambda i, j: j)
            grid = (m // tm, n // tn)
        a_spec = pl.BlockSpec((tk, tm), lambda p, q: (0, mi(p, q))) if ta else pl.BlockSpec((tm, tk), lambda p, q: (mi(p, q), 0))
        b_spec = pl.BlockSpec((tn, tk), lambda p, q: (ni(p, q), 0)) if tb else pl.BlockSpec((tk, tn), lambda p, q: (0, ni(p, q)))
        return _pallas(body, name=name, grid=grid, in_specs=[a_spec, b_spec],
                       out_specs=pl.BlockSpec((tm, tn), lambda p, q: (mi(p, q), ni(p, q))), out_shape=out_shape,
                       compiler_params=_cparams("parallel", "parallel"))(a, b)

    direct = jnp.dtype(out_dtype) == jnp.dtype(F32)

    def body(a_ref, b_ref, o_ref, *scratch):
        acc_ref = o_ref if direct else scratch[0]
        kk = pl.program_id(2)

        @pl.when(kk == 0)
        def _():
            acc_ref[...] = jnp.zeros_like(acc_ref)

        acc_ref[...] += _dot(a_ref[...], b_ref[...], dims)

        if not direct:
            @pl.when(kk == nk - 1)
            def _():
                o_ref[...] = acc_ref[...].astype(o_ref.dtype)

    a_spec = pl.BlockSpec((tk, tm), lambda i, j, kk: (kk, i)) if ta else pl.BlockSpec((tm, tk), lambda i, j, kk: (i, kk))
    b_spec = pl.BlockSpec((tn, tk), lambda i, j, kk: (j, kk)) if tb else pl.BlockSpec((tk, tn), lambda i, j, kk: (kk, j))
    return _pallas(
        body, name=name, grid=(m // tm, n // tn, nk), in_specs=[a_spec, b_spec],
        out_specs=pl.BlockSpec((tm, tn), lambda i, j, kk: (i, j)), out_shape=out_shape,
        scratch_shapes=[] if direct else [pltpu.VMEM((tm, tn), F32)],
        compiler_params=_cparams("parallel", "parallel", "arbitrary"),
    )(a, b)


def linear(a, w, out_dtype=F32, name="lin"):
    @jax.custom_vjp
    def f(a, w):
        return _mm(a.astype(ACT_DTYPE), w.astype(MXU_DTYPE), out_dtype=out_dtype, name=name + "_fwd")

    def fwd(a, w):
        ab, wb = a.astype(ACT_DTYPE), w.astype(MXU_DTYPE)
        return _mm(ab, wb, out_dtype=out_dtype, name=name + "_fwd"), (ab, wb, jnp.zeros((0,), a.dtype))

    def bwd(res, g):
        ab, wb, like_a = res
        gb = g.astype(ACT_DTYPE)
        da = _mm(gb, wb, tb=True, out_dtype=like_a.dtype, name=name + "_dx")
        dw = _mm(ab, gb, ta=True, out_dtype=F32, name=name + "_dw")
        return da, dw

    f.defvjp(fwd, bwd)
    return f(a, w)


def _row_tile(rows, width):
    return _tile(rows, max(SUBLANES, (512 * 1024) // width), 16)


def rmsnorm(x, g, out_dtype=F32, name="rms"):
    rows, d = x.shape
    tr = _row_tile(rows, d)
    n_steps = rows // tr

    def fwd_body(x_ref, g_ref, o_ref):
        xv = x_ref[...].astype(F32)
        r = lax.rsqrt(jnp.mean(xv * xv, axis=-1, keepdims=True) + EPS)
        o_ref[...] = (xv * r * g_ref[...]).astype(o_ref.dtype)

    def bwd_body(x_ref, g_ref, dy_ref, dx_ref, dg_ref):
        xv = x_ref[...].astype(F32)
        dy = dy_ref[...].astype(F32)
        r = lax.rsqrt(jnp.mean(xv * xv, axis=-1, keepdims=True) + EPS)
        xh = xv * r
        dxh = dy * g_ref[...]
        dx_ref[...] = (r * (dxh - xh * jnp.mean(dxh * xh, axis=-1, keepdims=True))).astype(dx_ref.dtype)

        @pl.when(pl.program_id(0) == 0)
        def _():
            dg_ref[...] = jnp.zeros_like(dg_ref)

        dg_ref[...] += jnp.sum(dy * xh, axis=0, keepdims=True)

    row_spec = pl.BlockSpec((tr, d), lambda i: (i, 0))
    vec_spec = pl.BlockSpec((1, d), lambda i: (0, 0))

    def run_fwd(x, g):
        return _pallas(fwd_body, name=name + "_fwd", grid=(n_steps,), in_specs=[row_spec, vec_spec], out_specs=row_spec,
                       out_shape=jax.ShapeDtypeStruct((rows, d), out_dtype), compiler_params=_cparams("parallel"))(
            x, g.reshape(1, d).astype(F32))

    @jax.custom_vjp
    def f(x, g):
        return run_fwd(x, g)

    def fwd(x, g):
        return run_fwd(x, g), (x, g)

    def bwd(res, dy):
        x, g = res
        dx, dg = _pallas(
            bwd_body, name=name + "_bwd", grid=(n_steps,), in_specs=[row_spec, vec_spec, row_spec],
            out_specs=[row_spec, vec_spec],
            out_shape=[jax.ShapeDtypeStruct((rows, d), x.dtype), jax.ShapeDtypeStruct((1, d), F32)],
            compiler_params=_cparams("arbitrary"))(x, g.reshape(1, d).astype(F32), dy)
        return dx, dg.reshape(g.shape).astype(g.dtype)

    f.defvjp(fwd, bwd)
    return f(x, g)


def _rowdot(a, b, name):
    h, s, d = a.shape
    ts = _tile(s, 2048)

    def body(a_ref, b_ref, o_ref):
        o_ref[...] = jnp.sum(a_ref[...].astype(F32) * b_ref[...].astype(F32), axis=-1, keepdims=True)

    spec = pl.BlockSpec((None, ts, d), lambda hh, i: (hh, i, 0))
    return _pallas(body, name=name, grid=(h, s // ts), in_specs=[spec, spec],
                   out_specs=pl.BlockSpec((None, ts, 1), lambda hh, i: (hh, i, 0)),
                   out_shape=jax.ShapeDtypeStruct((h, s, 1), F32), compiler_params=_cparams("parallel", "parallel"))(a, b)


def attention(q, k, v, scale, tq, tk, name):
    h, sq, d = q.shape
    sk, dv = k.shape[1], v.shape[2]
    tq, tk = _tile(sq, tq), _tile(sk, tk)
    nq, nk = sq // tq, sk // tk

    def fwd_body(q_ref, k_ref, v_ref, o_ref, lse_ref, m_ref, l_ref, acc_ref):
        j = pl.program_id(2)

        @pl.when(j == 0)
        def _():
            m_ref[...] = jnp.full_like(m_ref, NEG)
            l_ref[...] = jnp.zeros_like(l_ref)
            acc_ref[...] = jnp.zeros_like(acc_ref)

        s = _dot(q_ref[...], k_ref[...], NT) * scale
        m_prev = m_ref[...]
        m_new = jnp.maximum(m_prev, jnp.max(s, axis=-1, keepdims=True))
        alpha = jnp.exp(m_prev - m_new)
        p = jnp.exp(s - m_new)
        l_ref[...] = alpha * l_ref[...] + jnp.sum(p, axis=-1, keepdims=True)
        acc_ref[...] = alpha * acc_ref[...] + _dot(p, v_ref[...], NN)
        m_ref[...] = m_new

        @pl.when(j == nk - 1)
        def _():
            o_ref[...] = (acc_ref[...] / l_ref[...]).astype(o_ref.dtype)
            lse_ref[...] = m_ref[...] + jnp.log(l_ref[...])

    def dq_body(q_ref, k_ref, v_ref, do_ref, lse_ref, dl_ref, dq_ref, acc_ref):
        j = pl.program_id(2)

        @pl.when(j == 0)
        def _():
            acc_ref[...] = jnp.zeros_like(acc_ref)

        s = _dot(q_ref[...], k_ref[...], NT) * scale
        p = jnp.exp(s - lse_ref[...])
        dp = _dot(do_ref[...], v_ref[...], NT)
        ds = p * (dp - dl_ref[...]) * scale
        acc_ref[...] += _dot(ds, k_ref[...], NN)

        @pl.when(j == nk - 1)
        def _():
            dq_ref[...] = acc_ref[...].astype(dq_ref.dtype)

    def dkv_body(q_ref, k_ref, v_ref, do_ref, lse_ref, dl_ref, dk_ref, dv_ref, dk_acc, dv_acc):
        i = pl.program_id(2)

        @pl.when(i == 0)
        def _():
            dk_acc[...] = jnp.zeros_like(dk_acc)
            dv_acc[...] = jnp.zeros_like(dv_acc)

        s = _dot(q_ref[...], k_ref[...], NT) * scale
        p = jnp.exp(s - lse_ref[...])
        dv_acc[...] += _dot(p, do_ref[...], TN)
        dp = _dot(do_ref[...], v_ref[...], NT)
        ds = p * (dp - dl_ref[...]) * scale
        dk_acc[...] += _dot(ds, q_ref[...], TN)

        @pl.when(i == nq - 1)
        def _():
            dk_ref[...] = dk_acc[...].astype(dk_ref.dtype)
            dv_ref[...] = dv_acc[...].astype(dv_ref.dtype)

    def q_spec(width):
        return pl.BlockSpec((None, tq, width), lambda hh, i, j: (hh, i, 0))

    def k_spec(width):
        return pl.BlockSpec((None, tk, width), lambda hh, i, j: (hh, j, 0))

    def run_fwd(q, k, v):
        return _pallas(
            fwd_body, name=name + "_fwd", grid=(h, nq, nk), in_specs=[q_spec(d), k_spec(d), k_spec(dv)],
            out_specs=[q_spec(dv), q_spec(1)],
            out_shape=[jax.ShapeDtypeStruct((h, sq, dv), q.dtype), jax.ShapeDtypeStruct((h, sq, 1), F32)],
            scratch_shapes=[pltpu.VMEM((tq, 1), F32), pltpu.VMEM((tq, 1), F32), pltpu.VMEM((tq, dv), F32)],
            compiler_params=_cparams("parallel", "parallel", "arbitrary"))(q, k, v)

    @jax.custom_vjp
    def f(q, k, v):
        return run_fwd(q, k, v)[0]

    def fwd(q, k, v):
        o, lse = run_fwd(q, k, v)
        return o, (q, k, v, o, lse)

    def bwd(res, do):
        q, k, v, o, lse = res
        delta = _rowdot(o, do, name + "_delta")
        dq = _pallas(
            dq_body, name=name + "_dq", grid=(h, nq, nk),
            in_specs=[q_spec(d), k_spec(d), k_spec(dv), q_spec(dv), q_spec(1), q_spec(1)], out_specs=q_spec(d),
            out_shape=jax.ShapeDtypeStruct((h, sq, d), q.dtype), scratch_shapes=[pltpu.VMEM((tq, d), F32)],
            compiler_params=_cparams("parallel", "parallel", "arbitrary"))(q, k, v, do, lse, delta)

        def qs(width):
            return pl.BlockSpec((None, tq, width), lambda hh, j, i: (hh, i, 0))

        def ks(width):
            return pl.BlockSpec((None, tk, width), lambda hh, j, i: (hh, j, 0))

        dk, dv_ = _pallas(
            dkv_body, name=name + "_dkv", grid=(h, nk, nq),
            in_specs=[qs(d), ks(d), ks(dv), qs(dv), qs(1), qs(1)], out_specs=[ks(d), ks(dv)],
            out_shape=[jax.ShapeDtypeStruct((h, sk, d), k.dtype), jax.ShapeDtypeStruct((h, sk, dv), v.dtype)],
            scratch_shapes=[pltpu.VMEM((tk, d), F32), pltpu.VMEM((tk, dv), F32)],
            compiler_params=_cparams("parallel", "parallel", "arbitrary"))(q, k, v, do, lse, delta)
        return dq, dk, dv_

    f.defvjp(fwd, bwd)
    return f(q, k, v)


LOG2E = 1.4426950408889634


def linear_t(a_t, w, out_dtype=F32, name="lin_t"):
    @jax.custom_vjp
    def f(a_t, w):
        return _mm(a_t, w.astype(MXU_DTYPE), ta=True, out_dtype=out_dtype, name=name + "_fwd")

    def fwd(a_t, w):
        wb = w.astype(MXU_DTYPE)
        return _mm(a_t, wb, ta=True, out_dtype=out_dtype, name=name + "_fwd"), (a_t, wb)

    def bwd(res, g):
        a_t, wb = res
        gb = g.astype(ACT_DTYPE)
        da_t = _mm(wb, gb, tb=True, out_dtype=a_t.dtype, name=name + "_dx")
        dw = _mm(a_t, gb, out_dtype=F32, name=name + "_dw")
        return da_t, dw

    f.defvjp(fwd, bwd)
    return f(a_t, w)


def mla_attention(q, k, v, scale, name="mla"):
    s, h, d = q.shape
    dv = v.shape[2]
    tq, tk = _tile(s, MLA_BWD_TQ), _tile(s, MLA_TK)
    nq, nk = s // tq, s // tk
    tqf = _tile(s, MLA_FWD_TQ)
    ones_rows = 16
    c = scale * LOG2E

    def fwd_body(qt_ref, k_ref, vt_ref, ot_ref, lse_ref, m_ref, acc_ref):
        j = pl.program_id(2)

        @pl.when(j == 0)
        def _():
            m_ref[...] = jnp.full_like(m_ref, NEG)
            acc_ref[...] = jnp.zeros_like(acc_ref)

        st = _dot(k_ref[...], qt_ref[...], NN)
        m_prev = m_ref[...]
        m_new = jnp.maximum(m_prev, jnp.max(st, axis=0, keepdims=True) * c)
        pt = jnp.exp2(st * c - m_new)
        acc_ref[...] = jnp.exp2(m_prev - m_new) * acc_ref[...] + _dot(vt_ref[...], pt, NN)
        m_ref[...] = m_new

        @pl.when(j == nk - 1)
        def _():
            l = acc_ref[dv:dv + 1, :]
            ot_ref[...] = (acc_ref[:dv, :] / l).astype(ot_ref.dtype)
            lse_ref[...] = m_ref[...] + jnp.log2(l)

    def delta_body(ot_ref, dot_ref, o_ref):
        o_ref[...] = jnp.sum(ot_ref[...].astype(F32) * dot_ref[...].astype(F32), axis=0, keepdims=True)

    def bwd_body(qt_ref, k_ref, kt_ref, v_ref, dot_ref, lse_ref, dl_ref, dqt_ref, dk_hbm, dv_hbm, dq_acc, dk_acc, dv_acc):
        hh, i, j = pl.program_id(0), pl.program_id(1), pl.program_id(2)

        @pl.when(j == 0)
        def _():
            dq_acc[...] = jnp.zeros_like(dq_acc)

        @pl.when(i == 0)
        def _():
            dk_acc[j] = jnp.zeros((d, tk), F32)
            dv_acc[j] = jnp.zeros((dv, tk), F32)

        qt, dot_ = qt_ref[...], dot_ref[...]
        pt = jnp.exp2(_dot(k_ref[...], qt, NN) * c - lse_ref[...])
        dst = (pt * (_dot(v_ref[...], dot_, NN) - dl_ref[...])).astype(MXU_DTYPE)
        dv_acc[j] += _dot(dot_, pt, NT)
        dk_acc[j] += _dot(qt, dst, NT)
        dq_acc[...] += _dot(kt_ref[...], dst, NN)

        @pl.when(j == nk - 1)
        def _():
            dqt_ref[...] = dq_acc[...] * scale

        @pl.when(i == nq - 1)
        def _():
            dk_acc[j] = dk_acc[j] * scale
            pltpu.sync_copy(dk_acc.at[j], dk_hbm.at[hh, j])
            pltpu.sync_copy(dv_acc.at[j], dv_hbm.at[hh, j])

    def qt_spec(width):
        return pl.BlockSpec((None, width, tq), lambda hh, i, j: (hh, 0, i))

    def kt_spec(width):
        return pl.BlockSpec((None, width, tk), lambda hh, i, j: (hh, 0, j))

    def k_spec(width):
        return pl.BlockSpec((None, tk, width), lambda hh, i, j: (hh, j, 0))

    def layouts(q, k, v):
        cast = lambda t: t.astype(ACT_DTYPE)
        return (cast(jnp.transpose(q, (1, 2, 0))), cast(jnp.transpose(k, (1, 0, 2))), cast(jnp.transpose(k, (1, 2, 0))),
                cast(jnp.transpose(v, (1, 0, 2))), cast(jnp.transpose(v, (1, 2, 0))))

    def run_fwd(qt, kh, vt):
        vt_ones = jnp.concatenate([vt, jnp.ones((h, ones_rows, s), vt.dtype)], axis=1)

        def qf_spec(width):
            return pl.BlockSpec((None, width, tqf), lambda hh, i, j: (hh, 0, i))

        return _pallas(
            fwd_body, name=name + "_fwd", grid=(h, s // tqf, nk), in_specs=[qf_spec(d), k_spec(d), kt_spec(dv + ones_rows)],
            out_specs=[qf_spec(dv), qf_spec(1)],
            out_shape=[jax.ShapeDtypeStruct((h, dv, s), ACT_DTYPE), jax.ShapeDtypeStruct((h, 1, s), F32)],
            scratch_shapes=[pltpu.VMEM((1, tqf), F32), pltpu.VMEM((dv + ones_rows, tqf), F32)],
            compiler_params=_cparams("parallel", "parallel", "arbitrary"))(qt, kh, vt_ones)

    @jax.custom_vjp
    def f(q, k, v):
        qt, kh, _, _, vt = layouts(q, k, v)
        return run_fwd(qt, kh, vt)[0].reshape(h * dv, s)

    def fwd(q, k, v):
        qt, kh, kt, vh, vt = layouts(q, k, v)
        ot, lse = run_fwd(qt, kh, vt)
        return ot.reshape(h * dv, s), (qt, kh, kt, vh, ot, lse)

    def bwd(res, dy):
        qt, kh, kt, vh, ot, lse = res
        dot_ = dy.reshape(h, dv, s)
        ts = _tile(s, 2048)
        col = pl.BlockSpec((None, dv, ts), lambda hh, i: (hh, 0, i))
        delta = _pallas(delta_body, name=name + "_delta", grid=(h, s // ts), in_specs=[col, col],
                        out_specs=pl.BlockSpec((None, 1, ts), lambda hh, i: (hh, 0, i)),
                        out_shape=jax.ShapeDtypeStruct((h, 1, s), F32), compiler_params=_cparams("parallel", "parallel"))(ot, dot_)
        any_spec = pl.BlockSpec(memory_space=pl.ANY)
        dqt, dkt, dvt = _pallas(
            bwd_body, name=name + "_bwd", grid=(h, nq, nk),
            in_specs=[qt_spec(d), k_spec(d), kt_spec(d), k_spec(dv), qt_spec(dv), qt_spec(1), qt_spec(1)],
            out_specs=[qt_spec(d), any_spec, any_spec],
            out_shape=[jax.ShapeDtypeStruct((h, d, s), F32), jax.ShapeDtypeStruct((h, nk, d, tk), F32),
                       jax.ShapeDtypeStruct((h, nk, dv, tk), F32)],
            scratch_shapes=[pltpu.VMEM((d, tq), F32), pltpu.VMEM((nk, d, tk), F32), pltpu.VMEM((nk, dv, tk), F32)],
            compiler_params=_cparams("parallel", "arbitrary", "arbitrary"))(qt, kh, kt, vh, dot_, lse, delta)
        to_tokens = lambda t: jnp.transpose(t, (1, 3, 0, 2)).reshape(s, h, t.shape[2])
        return jnp.transpose(dqt, (2, 0, 1)), to_tokens(dkt), to_tokens(dvt)

    f.defvjp(fwd, bwd)
    return f(q, k, v)


WATTN_TQ = 2 * C_BLOCK
WATTN_KW = WATTN_TQ + 2 * C_BLOCK


def window_attention(q, k, v, bias, sink, name="wattn"):
    hq, s, dh = q.shape
    g = hq // C_KV_HEADS
    tq, kw, half = WATTN_TQ, WATTN_KW, WATTN_KW // 2
    nt = s // tq
    scale = dh ** -0.5
    sink_b = jnp.broadcast_to(sink.astype(F32).reshape(hq, 1, 1), (hq, 1, LANES))
    neg = jnp.full((hq, C_BLOCK, C_BLOCK), NEG, F32)
    tile = jnp.concatenate(
        [jnp.concatenate([bias[:, cb - rb] if 0 <= cb - rb <= 2 else neg for cb in range(kw // C_BLOCK)], axis=2)
         for rb in range(tq // C_BLOCK)], axis=1)

    def key_bias(i):
        pos = lax.broadcasted_iota(jnp.int32, (1, kw), 1) + i * tq - C_BLOCK
        return jnp.where(jnp.logical_and(pos >= 0, pos < s), 0.0, NEG)

    def both(a_ref, b_ref):
        return jnp.concatenate([a_ref[...], b_ref[...]], axis=0)

    def fwd_body(q_ref, ka_ref, kb_ref, va_ref, vb_ref, b_ref, sk_ref, o_ref, lse_ref):
        kb_ = key_bias(pl.program_id(1))
        k_all, v_all = both(ka_ref, kb_ref), both(va_ref, vb_ref)
        for hh in range(g):
            sc = _dot(q_ref[hh], k_all, NT) * scale + b_ref[hh] + kb_
            snk = sk_ref[hh][:, :1]
            m = jnp.maximum(jnp.max(sc, axis=-1, keepdims=True), snk)
            p = jnp.exp(sc - m)
            l = jnp.sum(p, axis=-1, keepdims=True) + jnp.exp(snk - m)
            o_ref[hh] = (_dot(p, v_all, NN) / l).astype(o_ref.dtype)
            lse_ref[hh] = m + jnp.log(l)

    def bwd_body(q_ref, ka_ref, kb_ref, va_ref, vb_ref, b_ref, sk_ref, do_ref, lse_ref, dl_ref,
                 dq_ref, db_ref, dsink_ref, dk_hbm, dv_hbm, dk_acc, dv_acc):
        kv, i = pl.program_id(0), pl.program_id(1)

        @pl.when(i == 0)
        def _():
            dk_acc[...] = jnp.zeros_like(dk_acc)
            dv_acc[...] = jnp.zeros_like(dv_acc)
            db_ref[...] = jnp.zeros_like(db_ref)
            dsink_ref[...] = jnp.zeros_like(dsink_ref)

        kb_ = key_bias(i)
        k_all, v_all = both(ka_ref, kb_ref), both(va_ref, vb_ref)
        dk_t = jnp.zeros((kw, dh), F32)
        dv_t = jnp.zeros((kw, dh), F32)
        for hh in range(g):
            lse, dl, do = lse_ref[hh], dl_ref[hh], do_ref[hh]
            p = jnp.exp(_dot(q_ref[hh], k_all, NT) * scale + b_ref[hh] + kb_ - lse)
            ds = p * (_dot(do, v_all, NT) - dl)
            db_ref[hh] += ds
            total = jnp.broadcast_to(-jnp.sum(jnp.exp(sk_ref[hh][:, :1] - lse) * dl, axis=0, keepdims=True), (1, LANES))
            dsink_ref[hh] += jnp.where(lax.broadcasted_iota(jnp.int32, (1, LANES), 1) == 0, total, 0.0)
            dsb = (ds * scale).astype(MXU_DTYPE)
            dq_ref[hh] = _dot(dsb, k_all, NN).astype(dq_ref.dtype)
            dk_t += _dot(dsb, q_ref[hh], TN)
            dv_t += _dot(p, do, TN)
        rows = pl.ds(pl.multiple_of(i * tq, tq), kw)
        dk_acc[rows, :] += dk_t
        dv_acc[rows, :] += dv_t

        @pl.when(i == nt - 1)
        def _():
            pltpu.sync_copy(dk_acc, dk_hbm.at[kv])
            pltpu.sync_copy(dv_acc, dv_hbm.at[kv])

    def q_spec(width):
        return pl.BlockSpec((g, tq, width), lambda kv, i: (kv, i, 0))

    ka_spec = pl.BlockSpec((None, half, dh), lambda kv, i: (kv, i, 0))
    kb_spec = pl.BlockSpec((None, half, dh), lambda kv, i: (kv, i + 1, 0))
    b_spec = pl.BlockSpec((g, tq, kw), lambda kv, i: (kv, 0, 0))
    sk_spec = pl.BlockSpec((g, 1, LANES), lambda kv, i: (kv, 0, 0))

    def padded(t):
        return jnp.pad(t, ((0, 0), (C_BLOCK, C_BLOCK), (0, 0)))

    def run_fwd(q, kp, vp, tile, sink_b):
        return _pallas(
            fwd_body, name=name + "_fwd", grid=(C_KV_HEADS, nt),
            in_specs=[q_spec(dh), ka_spec, kb_spec, ka_spec, kb_spec, b_spec, sk_spec], out_specs=[q_spec(dh), q_spec(1)],
            out_shape=[jax.ShapeDtypeStruct((hq, s, dh), q.dtype), jax.ShapeDtypeStruct((hq, s, 1), F32)],
            compiler_params=_cparams("parallel", "parallel"))(q, kp, kp, vp, vp, tile, sink_b)

    @jax.custom_vjp
    def f(q, k, v, tile, sink_b):
        return run_fwd(q, padded(k), padded(v), tile, sink_b)[0]

    def fwd(q, k, v, tile, sink_b):
        kp, vp = padded(k), padded(v)
        o, lse = run_fwd(q, kp, vp, tile, sink_b)
        return o, (q, kp, vp, tile, sink_b, o, lse)

    def bwd(res, do):
        q, kp, vp, tile, sink_b, o, lse = res
        delta = _rowdot(o, do, name + "_delta")
        any_spec = pl.BlockSpec(memory_space=pl.ANY)
        acc = jax.ShapeDtypeStruct((C_KV_HEADS, s + 2 * C_BLOCK, dh), F32)
        dq, dtile, dsink, dkp, dvp = _pallas(
            bwd_body, name=name + "_bwd", grid=(C_KV_HEADS, nt),
            in_specs=[q_spec(dh), ka_spec, kb_spec, ka_spec, kb_spec, b_spec, sk_spec, q_spec(dh), q_spec(1), q_spec(1)],
            out_specs=[q_spec(dh), b_spec, sk_spec, any_spec, any_spec],
            out_shape=[jax.ShapeDtypeStruct((hq, s, dh), q.dtype), jax.ShapeDtypeStruct((hq, tq, kw), F32),
                       jax.ShapeDtypeStruct((hq, 1, LANES), F32), acc, acc],
            scratch_shapes=[pltpu.VMEM((s + 2 * C_BLOCK, dh), F32), pltpu.VMEM((s + 2 * C_BLOCK, dh), F32)],
            compiler_params=_cparams("parallel", "arbitrary"))(q, kp, kp, vp, vp, tile, sink_b, do, lse, delta)
        unpad = lambda t: t[:, C_BLOCK:-C_BLOCK].astype(kp.dtype)
        return dq, unpad(dkp), unpad(dvp), dtile, dsink

    f.defvjp(fwd, bwd)
    return f(q, k, v, tile, sink_b)


HG_PREP_ROWS = 256
HG_PAIR = 2
HG_INTRA_BLOCK = 256
HG_INTER_CHUNKS = 16
MLA_FWD_TQ, MLA_BWD_TQ, MLA_TK = 2048, 1024, 1024
CROSS_TQ = 1024


def _hdot(a, b, dims):
    return lax.dot_general(a, b, (dims, ((), ())), precision=lax.Precision.HIGHEST, preferred_element_type=F32)


def hgrn_prep(q, z, lb, reverse, name):
    n_hp, s, tc = q.shape
    tb = _tile(s, HG_PREP_ROWS)
    ncb = tb // B_CHUNK

    def chunk_matrices():
        r = lax.broadcasted_iota(jnp.int32, (tb, tb), 0)
        cc = lax.broadcasted_iota(jnp.int32, (tb, tb), 1)
        same = r // B_CHUNK == cc // B_CHUNK
        tri = (cc >= r) if reverse else (cc <= r)
        cum = jnp.where(jnp.logical_and(same, tri), 1.0, 0.0).astype(F32)
        every = jnp.where(same, 1.0, 0.0).astype(F32)
        pr = lax.broadcasted_iota(jnp.int32, (ncb, tb), 0)
        pc = lax.broadcasted_iota(jnp.int32, (ncb, tb), 1)
        per_chunk = jnp.where(pc // B_CHUNK == pr, 1.0, 0.0).astype(F32)
        return cum, every, per_chunk

    def gates(zv, lbv):
        e = jnp.exp(-jnp.abs(zv))
        big, small = 1.0 / (1.0 + e), e / (1.0 + e)
        sig = jnp.where(zv >= 0, big, small)
        nsig = jnp.where(zv >= 0, small, big)
        f = lbv + (1.0 - lbv) * sig
        return sig, nsig, f, jnp.log(jnp.maximum(f, TINY)), (1.0 - lbv) * nsig

    def fwd_body(q_ref, z_ref, lb_ref, qd_ref, ki_ref, ke_ref, dec_ref):
        cum, every, per_chunk = chunk_matrices()
        _, _, _, lf, key = gates(z_ref[...], lb_ref[...])
        b = _hdot(cum, lf, NN)
        tot = _hdot(every, lf, NN)
        qd_ref[...] = q_ref[...] * jnp.exp(b)
        ki_ref[...] = key * jnp.exp(-b)
        ke_ref[...] = key * jnp.exp(tot - b)
        dec_ref[...] = jnp.exp(_hdot(per_chunk, lf, NN))

    def bwd_body(q_ref, z_ref, lb_ref, dqd_ref, dki_ref, dke_ref, ddec_ref, dq_ref, dz_ref, dlb_ref):
        cum, every, per_chunk = chunk_matrices()
        lbv = lb_ref[...]
        sig, nsig, f, lf, key = gates(z_ref[...], lbv)
        b = _hdot(cum, lf, NN)
        tot = _hdot(every, lf, NN)
        e_b, e_nb, e_tb = jnp.exp(b), jnp.exp(-b), jnp.exp(tot - b)
        dqd, dki, dke = dqd_ref[...], dki_ref[...], dke_ref[...]
        dq_ref[...] = dqd * e_b
        dkey = dki * e_nb + dke * e_tb
        t_end = dke * key * e_tb
        db = dqd * q_ref[...] * e_b - dki * key * e_nb - t_end
        dtot = ddec_ref[...] * jnp.exp(_hdot(per_chunk, lf, NN)) + _hdot(per_chunk, t_end, NN)
        dlf = _hdot(cum, db, TN) + _hdot(per_chunk, dtot, TN)
        df = jnp.where(f > TINY, dlf / f, 0.0)
        one_m_lb = 1.0 - lbv
        dz_ref[...] = (df - dkey) * one_m_lb * sig * nsig
        dlb_part = jnp.sum(df * nsig - dkey * nsig, axis=0, keepdims=True)

        @pl.when(pl.program_id(1) == 0)
        def _():
            dlb_ref[...] = jnp.zeros_like(dlb_ref)

        dlb_ref[...] += dlb_part

    tok = pl.BlockSpec((None, tb, tc), lambda j, i: (j, i, 0))
    vec = pl.BlockSpec((None, 1, tc), lambda j, i: (j, 0, 0))
    chk = pl.BlockSpec((None, ncb, tc), lambda j, i: (j, i, 0))
    grid = (n_hp, s // tb)
    tok_shape = jax.ShapeDtypeStruct((n_hp, s, tc), F32)
    chk_shape = jax.ShapeDtypeStruct((n_hp, s // B_CHUNK, tc), F32)

    def run_fwd(q, z, lb):
        return _pallas(fwd_body, name=name + "_fwd", grid=grid, in_specs=[tok, tok, vec], out_specs=[tok, tok, tok, chk],
                       out_shape=[tok_shape, tok_shape, tok_shape, chk_shape],
                       compiler_params=_cparams("parallel", "parallel"))(q, z, lb)

    @jax.custom_vjp
    def f(q, z, lb):
        return tuple(run_fwd(q, z, lb))

    def fwd(q, z, lb):
        return tuple(run_fwd(q, z, lb)), (q, z, lb)

    def bwd(res, cts):
        q, z, lb = res
        dq, dz, dlb = _pallas(
            bwd_body, name=name + "_bwd", grid=grid, in_specs=[tok, tok, vec, tok, tok, tok, chk], out_specs=[tok, tok, vec],
            out_shape=[tok_shape, tok_shape, jax.ShapeDtypeStruct((n_hp, 1, tc), F32)],
            compiler_params=_cparams("parallel", "arbitrary"))(q, z, lb, *cts)
        return dq, dz, dlb

    f.defvjp(fwd, bwd)
    return f(q, z, lb)


def _pair_cols(ref, hh, width):
    return ref[:, hh * width:(hh + 1) * width]


def hgrn_intra(qd, ki, v, reverse, name):
    s = qd.shape[1]
    tb = _tile(s, HG_INTRA_BLOCK)
    wk, wv = HG_PAIR * B_DK, HG_PAIR * B_DV

    def mask():
        r = lax.broadcasted_iota(jnp.int32, (tb, tb), 0)
        c = lax.broadcasted_iota(jnp.int32, (tb, tb), 1)
        return jnp.logical_and(r // B_CHUNK == c // B_CHUNK, (c >= r) if reverse else (c <= r))

    def fwd_body(q_ref, k_ref, v_ref, o_ref):
        msk = mask()
        for hh in range(HG_PAIR):
            sc = jnp.where(msk, _dot(_pair_cols(q_ref, hh, B_DK), _pair_cols(k_ref, hh, B_DK), NT), 0.0)
            o_ref[:, hh * B_DV:(hh + 1) * B_DV] = _dot(sc, _pair_cols(v_ref, hh, B_DV), NN)

    def bwd_body(q_ref, k_ref, v_ref, do_ref, dq_ref, dk_ref, dv_ref):
        msk = mask()
        for hh in range(HG_PAIR):
            q, k = _pair_cols(q_ref, hh, B_DK), _pair_cols(k_ref, hh, B_DK)
            vv, do = _pair_cols(v_ref, hh, B_DV), _pair_cols(do_ref, hh, B_DV)
            sc = jnp.where(msk, _dot(q, k, NT), 0.0)
            ds = jnp.where(msk, _dot(do, vv, NT), 0.0)
            dq_ref[:, hh * B_DK:(hh + 1) * B_DK] = _dot(ds, k, NN)
            dk_ref[:, hh * B_DK:(hh + 1) * B_DK] = _dot(ds, q, TN)
            dv_ref[:, hh * B_DV:(hh + 1) * B_DV] = _dot(sc, do, TN)

    ks = pl.BlockSpec((None, tb, wk), lambda hp, i: (hp, i, 0))
    vs = pl.BlockSpec((None, tb, wv), lambda hp, i: (hp, i, 0))
    grid = (B_HEADS // HG_PAIR, s // tb)

    def run_fwd(qd, ki, v):
        return _pallas(fwd_body, name=name + "_fwd", grid=grid, in_specs=[ks, ks, vs], out_specs=vs,
                       out_shape=jax.ShapeDtypeStruct(v.shape, F32), compiler_params=_cparams("parallel", "parallel"))(qd, ki, v)

    @jax.custom_vjp
    def f(qd, ki, v):
        return run_fwd(qd, ki, v)

    def fwd(qd, ki, v):
        return run_fwd(qd, ki, v), (qd, ki, v)

    def bwd(res, do):
        qd, ki, v = res
        return tuple(_pallas(
            bwd_body, name=name + "_bwd", grid=grid, in_specs=[ks, ks, vs, vs], out_specs=[ks, ks, vs],
            out_shape=[jax.ShapeDtypeStruct(qd.shape, F32), jax.ShapeDtypeStruct(ki.shape, F32),
                       jax.ShapeDtypeStruct(v.shape, F32)],
            compiler_params=_cparams("parallel", "parallel"))(qd, ki, v, do))

    f.defvjp(fwd, bwd)
    return f(qd, ki, v)


def hgrn_inter(qd, ke, v, dec, reverse, name):
    s = qd.shape[1]
    nc = s // B_CHUNK
    cpb = HG_INTER_CHUNKS if nc % HG_INTER_CHUNKS == 0 else nc
    tb = cpb * B_CHUNK
    nblk = nc // cpb
    wk, wv = HG_PAIR * B_DK, HG_PAIR * B_DV
    n_hp = B_HEADS // HG_PAIR

    def rows(c):
        return pl.ds(c * B_CHUNK, B_CHUNK)

    def kcols(hh):
        return slice(hh * B_DK, (hh + 1) * B_DK)

    def vcols(hh):
        return slice(hh * B_DV, (hh + 1) * B_DV)

    def order(flip):
        return reversed(range(cpb)) if flip else range(cpb)

    def fwd_body(q_ref, k_ref, v_ref, dec_ref, o_ref, st_ref, state):
        @pl.when(pl.program_id(1) == 0)
        def _():
            state[...] = jnp.zeros_like(state)

        for c in order(reverse):
            for hh in range(HG_PAIR):
                st = state[hh]
                st_ref[c, hh] = st
                o_ref[rows(c), vcols(hh)] = _dot(q_ref[rows(c), kcols(hh)], st, NT)
                state[hh] = st * dec_ref[pl.ds(c, 1), kcols(hh)] + _dot(v_ref[rows(c), vcols(hh)], k_ref[rows(c), kcols(hh)], TN)

    def bwd_body(q_ref, k_ref, v_ref, dec_ref, st_ref, do_ref, dq_ref, dk_ref, dv_ref, ddec_ref, dstate):
        @pl.when(pl.program_id(1) == 0)
        def _():
            dstate[...] = jnp.zeros_like(dstate)

        for c in order(not reverse):
            for hh in range(HG_PAIR):
                dst = dstate[hh]
                st = st_ref[c, hh]
                do_c = do_ref[rows(c), vcols(hh)]
                dk_ref[rows(c), kcols(hh)] = _dot(v_ref[rows(c), vcols(hh)], dst, NN)
                dv_ref[rows(c), vcols(hh)] = _dot(k_ref[rows(c), kcols(hh)], dst, NT)
                ddec_ref[pl.ds(c, 1), kcols(hh)] = jnp.sum(dst * st, axis=0, keepdims=True)
                dq_ref[rows(c), kcols(hh)] = _dot(do_c, st, NN)
                dstate[hh] = dst * dec_ref[pl.ds(c, 1), kcols(hh)] + _dot(do_c, q_ref[rows(c), kcols(hh)], TN)

    def specs(flip):
        blk = (lambda i: nblk - 1 - i) if flip else (lambda i: i)
        tok_k = pl.BlockSpec((None, tb, wk), lambda hp, i: (hp, blk(i), 0))
        tok_v = pl.BlockSpec((None, tb, wv), lambda hp, i: (hp, blk(i), 0))
        chk = pl.BlockSpec((None, cpb, wk), lambda hp, i: (hp, blk(i), 0))
        sts = pl.BlockSpec((None, cpb, HG_PAIR, B_DV, B_DK), lambda hp, i: (hp, blk(i), 0, 0, 0))
        return tok_k, tok_v, chk, sts

    scratch = [pltpu.VMEM((HG_PAIR, B_DV, B_DK), F32)]

    def run_fwd(qd, ke, v, dec):
        tok_k, tok_v, chk, sts = specs(reverse)
        return _pallas(
            fwd_body, name=name + "_fwd", grid=(n_hp, nblk), in_specs=[tok_k, tok_k, tok_v, chk], out_specs=[tok_v, sts],
            out_shape=[jax.ShapeDtypeStruct(v.shape, F32), jax.ShapeDtypeStruct((n_hp, nc, HG_PAIR, B_DV, B_DK), F32)],
            scratch_shapes=scratch, compiler_params=_cparams("parallel", "arbitrary"))(qd, ke, v, dec)

    @jax.custom_vjp
    def f(qd, ke, v, dec):
        return run_fwd(qd, ke, v, dec)[0]

    def fwd(qd, ke, v, dec):
        o, st = run_fwd(qd, ke, v, dec)
        return o, (qd, ke, v, dec, st)

    def bwd(res, do):
        qd, ke, v, dec, st = res
        tok_k, tok_v, chk, sts = specs(not reverse)
        return tuple(_pallas(
            bwd_body, name=name + "_bwd", grid=(n_hp, nblk), in_specs=[tok_k, tok_k, tok_v, chk, sts, tok_v],
            out_specs=[tok_k, tok_k, tok_v, chk],
            out_shape=[jax.ShapeDtypeStruct(qd.shape, F32), jax.ShapeDtypeStruct(ke.shape, F32),
                       jax.ShapeDtypeStruct(v.shape, F32), jax.ShapeDtypeStruct(dec.shape, F32)],
            scratch_shapes=scratch, compiler_params=_cparams("parallel", "arbitrary"))(qd, ke, v, dec, st, do))

    f.defvjp(fwd, bwd)
    return f(qd, ke, v, dec)


def loss_head(y, target, name="loss"):
    s, d = y.shape
    tr = _row_tile(s, d)

    def body(y_ref, t_ref, o_ref):
        @pl.when(pl.program_id(0) == 0)
        def _():
            o_ref[...] = jnp.zeros_like(o_ref)

        e = y_ref[...] - t_ref[...]
        part = jnp.sum(jnp.sum(e * e, axis=-1, keepdims=True), axis=0, keepdims=True) * (0.5 / d)
        o_ref[...] += jnp.broadcast_to(part, o_ref.shape)

    spec = pl.BlockSpec((tr, d), lambda i: (i, 0))

    def run(y, t):
        out = _pallas(body, name=name, grid=(s // tr,), in_specs=[spec, spec],
                      out_specs=pl.BlockSpec((SUBLANES, LANES), lambda i: (0, 0)),
                      out_shape=jax.ShapeDtypeStruct((SUBLANES, LANES), F32), compiler_params=_cparams("arbitrary"))(y, t)
        return out[0, 0]

    @jax.custom_vjp
    def f(y, t):
        return run(y, t)

    def fwd(y, t):
        return run(y, t), (y, t)

    def bwd(res, g):
        y, t = res
        dy = g * (y - t) * (1.0 / d)
        return dy, -dy

    f.defvjp(fwd, bwd)
    return f(y, target)


def _mesh_pos():
    return lax.axis_index("x"), lax.axis_index("y"), lax.axis_index("c")


def all_gather_shards(shards):
    n = len(shards)

    def body(*refs):
        ins, outs = refs[:n], refs[n:2 * n]
        send_sems, recv_sems, local_sems = refs[2 * n:]
        x, y, c = _mesh_pos()
        me, sibling = (x, y, c), (x, y, 1 - c)
        chips = [(1 - x, y), (x, 1 - y), (1 - x, 1 - y)]

        def slot(t, px, py, pc):
            return outs[t].at[4 * px + 2 * py + pc]

        def copy(t, k, block, to, src=None):
            return pltpu.make_async_remote_copy(
                src_ref=slot(t, *block) if src is None else src, dst_ref=slot(t, *block), send_sem=send_sems.at[t, k],
                recv_sem=recv_sems.at[t, k], device_id=to, device_id_type=pl.DeviceIdType.MESH)

        mine = [pltpu.make_async_copy(ins[t], slot(t, *me), local_sems.at[t]) for t in range(n)]
        for cp in mine:
            cp.start()
        first = []
        for t in range(n):
            first.append(copy(t, 0, me, sibling, src=ins[t]))
            first += [copy(t, 1 + j, me, (*chip, c), src=ins[t]) for j, chip in enumerate(chips)]
        for cp in first:
            cp.start()
        passed = []
        for j, chip in enumerate(chips):
            for t in range(n):
                copy(t, 1 + j, (*chip, c), me).wait_recv()
                cp = copy(t, 4 + j, (*chip, c), sibling)
                cp.start()
                passed.append(cp)
        for t in range(n):
            copy(t, 0, sibling, me).wait_recv()
            for j, chip in enumerate(chips):
                copy(t, 4 + j, (*chip, 1 - c), me).wait_recv()
        for cp in first + passed:
            cp.wait_send()
        for cp in mine:
            cp.wait()

    any_spec = pl.BlockSpec(memory_space=pl.ANY)
    return _pallas(
        body, name="all_gather_weights", out_shape=[jax.ShapeDtypeStruct((N_DEV, *s.shape), s.dtype) for s in shards],
        in_specs=[any_spec] * n, out_specs=[any_spec] * n,
        scratch_shapes=[pltpu.SemaphoreType.DMA((n, 7)), pltpu.SemaphoreType.DMA((n, 7)), pltpu.SemaphoreType.DMA((n,))],
    )(*shards)


def all_to_all_blocks(stacks):
    n = len(stacks)

    def body(*refs):
        ins, outs = refs[:n], refs[n:2 * n]
        send_sems, recv_sems, local_sems = refs[2 * n:]
        x, y, c = _mesh_pos()
        me = 4 * x + 2 * y + c
        mine = [pltpu.make_async_copy(ins[t].at[me], outs[t].at[me], local_sems.at[t]) for t in range(n)]
        for cp in mine:
            cp.start()
        copies = []
        for k in range(1, N_DEV):
            px = 1 - x if k & 4 else x
            py = 1 - y if k & 2 else y
            pc = 1 - c if k & 1 else c
            for t in range(n):
                cp = pltpu.make_async_remote_copy(
                    src_ref=ins[t].at[4 * px + 2 * py + pc], dst_ref=outs[t].at[me], send_sem=send_sems.at[t, k - 1],
                    recv_sem=recv_sems.at[t, k - 1], device_id=(px, py, pc), device_id_type=pl.DeviceIdType.MESH)
                cp.start()
                copies.append(cp)
        for cp in copies:
            cp.wait_recv()
        for cp in copies:
            cp.wait_send()
        for cp in mine:
            cp.wait()

    any_spec = pl.BlockSpec(memory_space=pl.ANY)
    return _pallas(
        body, name="all_to_all_grads", out_shape=[jax.ShapeDtypeStruct(s.shape, s.dtype) for s in stacks],
        in_specs=[any_spec] * n, out_specs=[any_spec] * n,
        scratch_shapes=[pltpu.SemaphoreType.DMA((n, 7)), pltpu.SemaphoreType.DMA((n, 7)), pltpu.SemaphoreType.DMA((n,))],
    )(*stacks)


def all_gather_small(v):
    r, w = v.shape

    def body(x_ref, out_ref, send_sems, recv_sems):
        x, y, c = _mesh_pos()
        me = 4 * x + 2 * y + c
        copies = []
        for k in range(1, N_DEV):
            px = 1 - x if k & 4 else x
            py = 1 - y if k & 2 else y
            pc = 1 - c if k & 1 else c
            cp = pltpu.make_async_remote_copy(
                src_ref=x_ref, dst_ref=out_ref.at[me], send_sem=send_sems.at[k - 1], recv_sem=recv_sems.at[k - 1],
                device_id=(px, py, pc), device_id_type=pl.DeviceIdType.MESH)
            cp.start()
            copies.append(cp)
        out_ref[me] = x_ref[...]
        for cp in copies:
            cp.wait_recv()
        for cp in copies:
            cp.wait_send()

    vmem = pl.BlockSpec(memory_space=pltpu.VMEM)
    return _pallas(
        body, name="all_gather_small", out_shape=jax.ShapeDtypeStruct((N_DEV, r, w), v.dtype), in_specs=[vmem],
        out_specs=vmem, scratch_shapes=[pltpu.SemaphoreType.DMA((7,)), pltpu.SemaphoreType.DMA((7,))],
    )(v)


def adamw_rows(parts, w, m, v, name):
    n, r, lanes = parts.shape
    tr = _tile(r, max(SUBLANES, (256 * 1024) // lanes), SUBLANES)
    c1 = 1.0 / (1.0 - ADAM_B1 ** ADAM_STEP)
    c2 = 1.0 / (1.0 - ADAM_B2 ** ADAM_STEP)

    def body(p_ref, w_ref, m_ref, v_ref, g_ref, d_ref, nm_ref, nv_ref):
        g = p_ref[0]
        for j in range(1, n):
            g = g + p_ref[j]
        nm = ADAM_B1 * m_ref[...] + (1.0 - ADAM_B1) * g
        nv = ADAM_B2 * v_ref[...] + (1.0 - ADAM_B2) * (g * g)
        g_ref[...] = g
        nm_ref[...] = nm
        nv_ref[...] = nv
        d_ref[...] = -ADAM_LR * ((nm * c1) / (jnp.sqrt(nv * c2) + ADAM_EPS) + ADAM_WD * w_ref[...])

    row = pl.BlockSpec((tr, lanes), lambda i: (i, 0))
    out = jax.ShapeDtypeStruct((r, lanes), F32)
    return _pallas(body, name=name, grid=(r // tr,), in_specs=[pl.BlockSpec((n, tr, lanes), lambda i: (0, i, 0)), row, row, row],
                   out_specs=[row, row, row, row], out_shape=[out, out, out, out], compiler_params=_cparams("parallel"))(
        parts, w, m, v)


def _padded(n):
    return -(-n // PACK_QUANTUM) * PACK_QUANTUM


def _pack(pieces, total_rows=None):
    flat = []
    for p in pieces:
        p = p.reshape(-1).astype(F32)
        flat.append(jnp.pad(p, (0, _padded(p.size) - p.size)))
    out = jnp.concatenate(flat).reshape(-1, LANES)
    if total_rows is not None and out.shape[0] != total_rows:
        out = jnp.pad(out, ((0, total_rows - out.shape[0]), (0, 0)))
    return out


def _pack_rows(sizes):
    rows = sum(_padded(n) for n in sizes) // LANES
    return -(-rows // PACK_ROW_TILE) * PACK_ROW_TILE


def _unpack(rows, shapes):
    lead = rows.shape[:-2]
    flat = rows.reshape(*lead, -1)
    out, off = [], 0
    for shp in shapes:
        n = int(np.prod(shp))
        out.append(flat[..., off:off + n].reshape(*lead, *shp))
        off += _padded(n)
    return out


def _shards_to_full(stacked, axis):
    moved = jnp.moveaxis(stacked, 0, axis)
    shp = list(stacked.shape[1:])
    shp[axis] *= N_DEV
    return moved.reshape(shp)


def _full_to_shards(full, axis):
    shp = list(full.shape)
    shp[axis:axis + 1] = [N_DEV, shp[axis] // N_DEV]
    return jnp.moveaxis(full.reshape(shp), axis, 0)


def _heads(t, n, d):
    return jnp.transpose(t.reshape(t.shape[0], n, d), (1, 0, 2)).astype(ACT_DTYPE)


def _unheads(t):
    return jnp.transpose(t, (1, 0, 2)).reshape(t.shape[1], -1)


def _rope_tables(s):
    half = A_ROPE // 2
    inv = ROPE_THETA ** (-jnp.arange(half, dtype=F32) / half)
    ang = jnp.arange(s, dtype=jnp.int32).astype(F32)[:, None] * inv[None, :]
    return jnp.cos(ang), jnp.sin(ang)


def _rope(t, cos, sin):
    half = A_ROPE // 2
    t1, t2 = t[..., :half], t[..., half:]
    c, sn = cos[:, None, :], sin[:, None, :]
    return jnp.concatenate([t1 * c - t2 * sn, t1 * sn + t2 * c], axis=-1)


def _t5_bucket(rel):
    nb = REL_BUCKETS // 2
    max_exact = nb // 2
    ret = (rel > 0).astype(jnp.int32) * nb
    n = jnp.abs(rel)
    large = max_exact + (jnp.log(jnp.maximum(n, 1).astype(F32) / max_exact)
                         / math.log(REL_MAX_DIST / max_exact) * (nb - max_exact)).astype(jnp.int32)
    large = jnp.minimum(large, nb - 1)
    return ret + jnp.where(n < max_exact, n, large)


def _window_bias(rel_bias):
    span = 3 * C_BLOCK
    rel = jnp.arange(span)[None, :] - C_BLOCK - jnp.arange(C_BLOCK)[:, None]
    onehot = (_t5_bucket(rel)[..., None] == jnp.arange(REL_BUCKETS)).astype(F32)
    bias = jnp.einsum("qkb,bh->hqk", onehot, rel_bias.astype(F32), precision=lax.Precision.HIGHEST)
    bias = jnp.where((jnp.abs(rel) <= C_WINDOW)[None], bias, NEG)
    return jnp.transpose(bias.reshape(C_HEADS, C_BLOCK, 3, C_BLOCK), (0, 2, 1, 3))


def _mla(cq, ckv, kr, gq, gkv, wuq, wukv, cos, sin):
    s = cq.shape[0]
    q = linear(rmsnorm(cq, gq, ACT_DTYPE, "rms_cq"), wuq, name="a_wuq").reshape(s, A_HEADS, A_NOPE + A_ROPE)
    q = jnp.concatenate([q[..., :A_NOPE], _rope(q[..., A_NOPE:], cos, sin)], axis=-1)
    kv = linear(rmsnorm(ckv, gkv, ACT_DTYPE, "rms_ckv"), wukv, name="a_wukv").reshape(s, A_HEADS, A_NOPE + A_V)
    k_rope = jnp.broadcast_to(_rope(kr[:, None, :], cos, sin), (s, A_HEADS, A_ROPE))
    k = jnp.concatenate([kv[..., :A_NOPE], k_rope], axis=-1)
    v = kv[..., A_NOPE:]
    return mla_attention(q, k, v, (A_NOPE + A_ROPE) ** -0.5)


def _hgrn2(q, f_fwd, f_bwd, i, g, lb_fwd, lb_bwd, g_out):
    s = q.shape[0]
    n_hp = B_HEADS // HG_PAIR
    pairs = lambda t: jnp.transpose(t.reshape(s, n_hp, -1), (1, 0, 2))
    qp, vp = pairs(q), pairs(i)
    o = None
    for z, lb, rev, tag in ((f_fwd, lb_fwd, False, "hgf"), (f_bwd, lb_bwd, True, "hgb")):
        qd, ki, ke, dec = hgrn_prep(qp, pairs(z), lb.astype(F32).reshape(n_hp, 1, -1), rev, tag + "_prep")
        part = hgrn_intra(qd, ki, vp, rev, tag + "_intra") + hgrn_inter(qd, ke, vp, dec, rev, tag + "_inter")
        o = part if o is None else o + part
    o = jnp.transpose(o, (1, 0, 2)).reshape(s * B_HEADS, B_DV)
    o = rmsnorm(o, g_out, F32, "rms_hg").reshape(s, B_HEADS * B_DV)
    return o * jax.nn.silu(g)


def _cross(h, mem_n, wq, wkv, wo):
    q = _heads(linear(h, wq, name="x_wq"), X_HEADS, X_DH)
    kv = linear(mem_n, wkv, name="x_wkv").reshape(mem_n.shape[0], 2, X_HEADS, X_DH)
    k = jnp.transpose(kv[:, 0], (1, 0, 2)).astype(ACT_DTYPE)
    v = jnp.transpose(kv[:, 1], (1, 0, 2)).astype(ACT_DTYPE)
    o = attention(q, k, v, X_DH ** -0.5, CROSS_TQ, 256, "cross")
    return linear(_unheads(o), wo, name="x_wo")


def _pad_w_in(w):
    cut = A_Q_RANK + A_KV_RANK + A_ROPE
    return jnp.concatenate([w[:, :cut], jnp.zeros((w.shape[0], KR_PAD), w.dtype), w[:, cut:]], axis=1)


def _model_loss(p, x, mem, target):
    s = x.shape[0]
    cos, sin = _rope_tables(s)
    sm = jax.nn.softmax(p["b_lb"].astype(F32), axis=1)
    lower_bounds = jnp.cumsum(sm, axis=1) - sm[:, :1]
    bias = _window_bias(p["rel_bias"])
    for l in range(DEPTH):
        h = rmsnorm(x, p["g_mix"][l], ACT_DTYPE, "rms_mix")
        z = linear(h, _pad_w_in(p["w_in"][l]), name="w_in")
        parts, start = [], 0
        for width in IN_SPLITS_PADDED:
            parts.append(z[:, start:start + width])
            start += width
        a_cq, a_ckv, a_kr, b_q, b_ff, b_fb, b_i, b_g, c_q, c_k, c_v, gate_a, gate_b, gate_c = parts
        y_a = _mla(a_cq, a_ckv, a_kr[:, :A_ROPE], p["a_gq"][l], p["a_gkv"][l], p["a_wuq"][l], p["a_wukv"][l], cos, sin)
        y_b = _hgrn2(b_q, b_ff, b_fb, b_i, b_g, lower_bounds[0, l], lower_bounds[1, l], p["b_gout"][l])
        y_c = _unheads(window_attention(_heads(c_q, C_HEADS, C_DH), _heads(c_k, C_KV_HEADS, C_DH),
                                        _heads(c_v, C_KV_HEADS, C_DH), bias, p["c_sink"][l]))
        merged = (jax.nn.sigmoid(gate_a) * linear_t(y_a, p["w_br_a"][l], name="w_br_a")
                  + jax.nn.sigmoid(gate_b) * linear(y_b, p["w_br_b"][l], name="w_br_b")
                  + jax.nn.sigmoid(gate_c) * linear(y_c, p["w_br_c"][l], name="w_br_c"))
        x = x + linear(merged, p["w_out"][l], name="w_out")
        h = rmsnorm(x, p["g_x"][l], ACT_DTYPE, "rms_x")
        x = x + _cross(h, rmsnorm(mem, p["g_mem"][l], ACT_DTYPE, "rms_mem"), p["x_wq"][l], p["x_wkv"][l], p["x_wo"][l])
        h = rmsnorm(x, p["g_ffn"][l], ACT_DTYPE, "rms_ffn")
        t = jax.nn.silu(linear(h, p["f_w1"][l], name="f_w1")) * linear(h, p["f_w3"][l], name="f_w3")
        x = x + linear(t, p["f_w2"][l], name="f_w2")
    y = rmsnorm(x, p["g_final"], F32, "rms_final")
    return loss_head(y, target)


def kernel(x, mem, w_in, g_mix, a_gq, a_gkv, a_wuq, a_wukv, b_lb, b_gout, c_sink, rel_bias, w_br_a, w_br_b, w_br_c, w_out, g_x, g_mem, x_wq, x_wkv, x_wo, g_ffn, f_w1, f_w3, f_w2, g_final, loss_target, m_w_in, m_g_mix, m_a_gq, m_a_gkv, m_a_wuq, m_a_wukv, m_b_lb, m_b_gout, m_c_sink, m_rel_bias, m_w_br_a, m_w_br_b, m_w_br_c, m_w_out, m_g_x, m_g_mem, m_x_wq, m_x_wkv, m_x_wo, m_g_ffn, m_f_w1, m_f_w3, m_f_w2, m_g_final, v_w_in, v_g_mix, v_a_gq, v_a_gkv, v_a_wuq, v_a_wukv, v_b_lb, v_b_gout, v_c_sink, v_rel_bias, v_w_br_a, v_w_br_b, v_w_br_c, v_w_out, v_g_x, v_g_mem, v_x_wq, v_x_wkv, v_x_wo, v_g_ffn, v_f_w1, v_f_w3, v_f_w2, v_g_final):
    given = dict(locals())
    w = {n: given[n] for n in WEIGHT_ORDER}
    m = {n: given["m_" + n] for n in WEIGHT_ORDER}
    v = {n: given["v_" + n] for n in WEIGHT_ORDER}
    sh_names = [n for n, _ in SHARDED]
    rep_shapes = [w[n].shape for n in REPLICATED] + [(1,)]
    rep_rows = _pack_rows([int(np.prod(s)) for s in rep_shapes])

    wire = [w[n] if n in ELEMENTWISE_SHARDED else w[n].astype(MXU_DTYPE) for n in sh_names]
    gathered = all_gather_shards(wire)
    full = {n: _shards_to_full(t, ax).astype(F32) for (n, ax), t in zip(SHARDED, gathered)}
    full.update({n: w[n] for n in REPLICATED})

    loss, (grad_full, grad_x) = jax.value_and_grad(_model_loss, argnums=(0, 1))(full, x[0], mem[0], loss_target[0])

    received = all_to_all_blocks([_full_to_shards(grad_full[n], ax) for n, ax in SHARDED])
    g_sh, d_sh, nm_sh, nv_sh = {}, {}, {}, {}
    for n, got in zip(sh_names, received):
        shp = w[n].shape
        rows = lambda t: t.reshape(-1, shp[-1])
        outs = adamw_rows(got.reshape(N_DEV, -1, shp[-1]), rows(w[n]), rows(m[n]), rows(v[n]), "adamw_" + n)
        g_sh[n], d_sh[n], nm_sh[n], nv_sh[n] = [o.reshape(shp) for o in outs]

    mine = _pack([grad_full[n] for n in REPLICATED] + [loss.reshape(1)], rep_rows)
    everyone = all_gather_small(mine)
    rep_w = [w[n] for n in REPLICATED] + [jnp.zeros((1,), F32)]
    outs = adamw_rows(everyone, _pack(rep_w, rep_rows), _pack([m[n] for n in REPLICATED] + [jnp.zeros((1,), F32)], rep_rows),
                      _pack([v[n] for n in REPLICATED] + [jnp.ones((1,), F32)], rep_rows), "adamw_replicated")
    rep_names = list(REPLICATED) + ["loss"]
    g_rp, d_rp, nm_rp, nv_rp = [dict(zip(rep_names, _unpack(o, rep_shapes))) for o in outs]

    def pick(sharded, replicated, n):
        return sharded[n] if n in sharded else replicated[n]

    return (g_rp["loss"].reshape(()), grad_x[None],
            *[pick(g_sh, g_rp, n) for n in WEIGHT_ORDER], *[pick(d_sh, d_rp, n) for n in WEIGHT_ORDER],
            *[pick(nm_sh, nm_rp, n) for n in WEIGHT_ORDER], *[pick(nv_sh, nv_rp, n) for n in WEIGHT_ORDER])
```

```python
import functools
import math

import jax
import jax.numpy as jnp
import numpy as np
from jax import lax
from jax.experimental import pallas as pl
from jax.experimental.pallas import tpu as pltpu

F32 = jnp.float32
MXU_DTYPE = jnp.bfloat16
ACT_DTYPE = jnp.bfloat16
GRAD_WIRE_DTYPE = jnp.bfloat16

V7X_VMEM_LIMIT_BYTES = 56 * 1024 * 1024
LANES = 128
SUBLANES = 8

N_DEV = 8
D_MODEL = 1024
DEPTH = 2
EPS = 1e-6
TINY = 1e-30
NEG = -1e30

A_HEADS, A_NOPE, A_ROPE, A_V, A_Q_RANK, A_KV_RANK = 8, 64, 32, 64, 384, 256
ROPE_THETA = 10000.0
B_HEADS, B_DK, B_DV, B_CHUNK = 8, 128, 64, 16
C_HEADS, C_KV_HEADS, C_DH, C_WINDOW, C_BLOCK = 8, 2, 64, 128, 128
REL_BUCKETS, REL_MAX_DIST = 32, 128
X_HEADS, X_DH = 4, 256
D_FF = 2816
IN_SPLITS = (A_Q_RANK, A_KV_RANK, A_ROPE, 1024, 1024, 1024, 512, 512, 512, 128, 128, 1024, 1024, 1024)
IN_WIDTH = sum(IN_SPLITS)
KR_PAD = LANES - A_ROPE
IN_SPLITS_PADDED = (A_Q_RANK, A_KV_RANK, LANES, 1024, 1024, 1024, 512, 512, 512, 128, 128, 1024, 1024, 1024)

ADAM_LR, ADAM_B1, ADAM_B2, ADAM_EPS, ADAM_WD, ADAM_STEP = 0.001, 0.9, 0.999, 1e-08, 0.01, 10

SHARDED = (("w_in", 2), ("a_wuq", 2), ("a_wukv", 2), ("b_lb", 2), ("w_br_a", 2), ("w_br_b", 2), ("w_br_c", 2),
           ("w_out", 1), ("x_wq", 1), ("x_wkv", 2), ("x_wo", 1), ("f_w1", 2), ("f_w3", 2), ("f_w2", 1))
ELEMENTWISE_SHARDED = ("b_lb",)
REPLICATED = ("g_mix", "a_gq", "a_gkv", "b_gout", "c_sink", "rel_bias", "g_x", "g_mem", "g_ffn", "g_final")
WEIGHT_ORDER = ("w_in", "g_mix", "a_gq", "a_gkv", "a_wuq", "a_wukv", "b_lb", "b_gout", "c_sink", "rel_bias", "w_br_a",
                "w_br_b", "w_br_c", "w_out", "g_x", "g_mem", "x_wq", "x_wkv", "x_wo", "g_ffn", "f_w1", "f_w3", "f_w2",
                "g_final")
PACK_QUANTUM = SUBLANES * LANES
PACK_ROW_TILE = 512


def _pallas(body, **kw):
    return pl.pallas_call(body, **kw)


def _cparams(*sem):
    return pltpu.CompilerParams(dimension_semantics=sem, vmem_limit_bytes=V7X_VMEM_LIMIT_BYTES)


def _tile(n, target, mult=LANES):
    t = (min(target, n) // mult) * mult
    while t >= mult:
        if n % t == 0:
            return t
        t -= mult
    return n


def _dot(a, b, dims):
    return lax.dot_general(a.astype(MXU_DTYPE), b.astype(MXU_DTYPE), (dims, ((), ())), preferred_element_type=F32)


NN = ((1,), (0,))
NT = ((1,), (1,))
TN = ((0,), (0,))


MM_VMEM_BUDGET_BYTES = 40 * 1024 * 1024
MM_MAX_TILE = 4352
MM_MAX_ROW_TILE = 2048
MM_HBM_BYTES_PER_S = 2.5e12
MM_STEP_S = 0.4e-6
MM_DMA_ROW_OVERHEAD_BYTES = 512.0


def _tile_options(n, cap):
    out = [t for t in range(LANES, min(n, cap) + 1, LANES) if n % t == 0]
    if n <= cap and n not in out:
        out.append(n)
    return out or [n]


@functools.lru_cache(maxsize=None)
def _mm_plan(m, n, k, ta, tb, a_bytes, b_bytes, o_bytes):
    best = None
    for tk in _tile_options(k, MM_MAX_TILE):
        nk = k // tk
        for tn in _tile_options(n, MM_MAX_TILE):
            for tm in _tile_options(m, MM_MAX_ROW_TILE):
                vmem = 2 * (tm * tk * a_bytes + tk * tn * b_bytes + tm * tn * o_bytes) + tm * tn * 4
                vmem += (tm * tk * 2 if a_bytes == 4 else 0) + (tk * tn * 2 if b_bytes == 4 else 0)
                if vmem > MM_VMEM_BUDGET_BYTES:
                    continue

                def eff(elems, nbytes):
                    return (elems * nbytes) / (elems * nbytes + MM_DMA_ROW_OVERHEAD_BYTES)

                ea, eb, eo = eff(tm if ta else tk, a_bytes), eff(tk if tb else tn, b_bytes), eff(tn, o_bytes)
                for order in ("mn", "nm"):
                    if nk == 1 and order == "nm":
                        a_tr, b_tr = m * k * a_bytes * (n // tn), k * n * b_bytes
                    elif nk == 1:
                        a_tr, b_tr = m * k * a_bytes, k * n * b_bytes * (m // tm)
                    else:
                        a_tr, b_tr = m * k * a_bytes * (n // tn), k * n * b_bytes * (m // tm)
                    steps = (m // tm) * (n // tn) * nk
                    cost = (a_tr / ea + b_tr / eb + m * n * o_bytes / eo) / MM_HBM_BYTES_PER_S + steps * MM_STEP_S
                    if best is None or cost < best[0]:
                        best = (cost, tm, tn, tk, order)
    assert best is not None, (m, n, k)
    return best[1:]


def _mm(a, b, ta=False, tb=False, out_dtype=F32, name="mm"):
    m, k = (a.shape[1], a.shape[0]) if ta else a.shape
    kb, n = (b.shape[1], b.shape[0]) if tb else b.shape
    assert k == kb, (a.shape, b.shape, ta, tb)
    tm, tn, tk, order = _mm_plan(m, n, k, ta, tb, a.dtype.itemsize, b.dtype.itemsize, jnp.dtype(out_dtype).itemsize)
    nk = k // tk
    dims = ((0 if ta else 1,), (1 if tb else 0,))
    out_shape = jax.ShapeDtypeStruct((m, n), out_dtype)

    if nk == 1:
        def body(a_ref, b_ref, o_ref):
            o_ref[...] = _dot(a_ref[...], b_ref[...], dims).astype(o_ref.dtype)

        if order == "nm":
            mi, ni = (lambda j, i: i), (lambda j, i: j)
            grid = (n // tn, m // tm)
        else:
            mi, ni = (lambda i, j: i), (lambda i, j: j)
            grid = (m // tm, n // tn)
        a_spec = pl.BlockSpec((tk, tm), lambda p, q: (0, mi(p, q))) if ta else pl.BlockSpec((tm, tk), lambda p, q: (mi(p, q), 0))
        b_spec = pl.BlockSpec((tn, tk), lambda p, q: (ni(p, q), 0)) if tb else pl.BlockSpec((tk, tn), lambda p, q: (0, ni(p, q)))
        return _pallas(body, name=name, grid=grid, in_specs=[a_spec, b_spec],
                       out_specs=pl.BlockSpec((tm, tn), lambda p, q: (mi(p, q), ni(p, q))), out_shape=out_shape,
                       compiler_params=_cparams("parallel", "parallel"))(a, b)

    direct = jnp.dtype(out_dtype) == jnp.dtype(F32)

    def body(a_ref, b_ref, o_ref, *scratch):
        acc_ref = o_ref if direct else scratch[0]
        kk = pl.program_id(2)

        @pl.when(kk == 0)
        def _():
            acc_ref[...] = jnp.zeros_like(acc_ref)

        acc_ref[...] += _dot(a_ref[...], b_ref[...], dims)

        if not direct:
            @pl.when(kk == nk - 1)
            def _():
                o_ref[...] = acc_ref[...].astype(o_ref.dtype)

    a_spec = pl.BlockSpec((tk, tm), lambda i, j, kk: (kk, i)) if ta else pl.BlockSpec((tm, tk), lambda i, j, kk: (i, kk))
    b_spec = pl.BlockSpec((tn, tk), lambda i, j, kk: (j, kk)) if tb else pl.BlockSpec((tk, tn), lambda i, j, kk: (kk, j))
    return _pallas(
        body, name=name, grid=(m // tm, n // tn, nk), in_specs=[a_spec, b_spec],
        out_specs=pl.BlockSpec((tm, tn), lambda i, j, kk: (i, j)), out_shape=out_shape,
        scratch_shapes=[] if direct else [pltpu.VMEM((tm, tn), F32)],
        compiler_params=_cparams("parallel", "parallel", "arbitrary"),
    )(a, b)


def linear(a, w, out_dtype=F32, name="lin"):
    @jax.custom_vjp
    def f(a, w):
        return _mm(a.astype(ACT_DTYPE), w.astype(MXU_DTYPE), out_dtype=out_dtype, name=name + "_fwd")

    def fwd(a, w):
        ab, wb = a.astype(ACT_DTYPE), w.astype(MXU_DTYPE)
        return _mm(ab, wb, out_dtype=out_dtype, name=name + "_fwd"), (ab, wb, jnp.zeros((0,), a.dtype))

    def bwd(res, g):
        ab, wb, like_a = res
        gb = g.astype(ACT_DTYPE)
        da = _mm(gb, wb, tb=True, out_dtype=like_a.dtype, name=name + "_dx")
        dw = _mm(ab, gb, ta=True, out_dtype=F32, name=name + "_dw")
        return da, dw

    f.defvjp(fwd, bwd)
    return f(a, w)


def _row_tile(rows, width):
    return _tile(rows, max(SUBLANES, (512 * 1024) // width), 16)


def rmsnorm(x, g, out_dtype=F32, name="rms"):
    rows, d = x.shape
    tr = _row_tile(rows, d)
    n_steps = rows // tr

    def fwd_body(x_ref, g_ref, o_ref):
        xv = x_ref[...].astype(F32)
        r = lax.rsqrt(jnp.mean(xv * xv, axis=-1, keepdims=True) + EPS)
        o_ref[...] = (xv * r * g_ref[...]).astype(o_ref.dtype)

    def bwd_body(x_ref, g_ref, dy_ref, dx_ref, dg_ref):
        xv = x_ref[...].astype(F32)
        dy = dy_ref[...].astype(F32)
        r = lax.rsqrt(jnp.mean(xv * xv, axis=-1, keepdims=True) + EPS)
        xh = xv * r
        dxh = dy * g_ref[...]
        dx_ref[...] = (r * (dxh - xh * jnp.mean(dxh * xh, axis=-1, keepdims=True))).astype(dx_ref.dtype)

        @pl.when(pl.program_id(0) == 0)
        def _():
            dg_ref[...] = jnp.zeros_like(dg_ref)

        dg_ref[...] += jnp.sum(dy * xh, axis=0, keepdims=True)

    row_spec = pl.BlockSpec((tr, d), lambda i: (i, 0))
    vec_spec = pl.BlockSpec((1, d), lambda i: (0, 0))

    def run_fwd(x, g):
        return _pallas(fwd_body, name=name + "_fwd", grid=(n_steps,), in_specs=[row_spec, vec_spec], out_specs=row_spec,
                       out_shape=jax.ShapeDtypeStruct((rows, d), out_dtype), compiler_params=_cparams("parallel"))(
            x, g.reshape(1, d).astype(F32))

    @jax.custom_vjp
    def f(x, g):
        return run_fwd(x, g)

    def fwd(x, g):
        return run_fwd(x, g), (x, g)

    def bwd(res, dy):
        x, g = res
        dx, dg = _pallas(
            bwd_body, name=name + "_bwd", grid=(n_steps,), in_specs=[row_spec, vec_spec, row_spec],
            out_specs=[row_spec, vec_spec],
            out_shape=[jax.ShapeDtypeStruct((rows, d), x.dtype), jax.ShapeDtypeStruct((1, d), F32)],
            compiler_params=_cparams("arbitrary"))(x, g.reshape(1, d).astype(F32), dy)
        return dx, dg.reshape(g.shape).astype(g.dtype)

    f.defvjp(fwd, bwd)
    return f(x, g)


def _rowdot(a, b, name):
    h, s, d = a.shape
    ts = _tile(s, 2048)

    def body(a_ref, b_ref, o_ref):
        o_ref[...] = jnp.sum(a_ref[...].astype(F32) * b_ref[...].astype(F32), axis=-1, keepdims=True)

    spec = pl.BlockSpec((None, ts, d), lambda hh, i: (hh, i, 0))
    return _pallas(body, name=name, grid=(h, s // ts), in_specs=[spec, spec],
                   out_specs=pl.BlockSpec((None, ts, 1), lambda hh, i: (hh, i, 0)),
                   out_shape=jax.ShapeDtypeStruct((h, s, 1), F32), compiler_params=_cparams("parallel", "parallel"))(a, b)


def attention(q, k, v, scale, tq, tk, name):
    h, sq, d = q.shape
    sk, dv = k.shape[1], v.shape[2]
    tq, tk = _tile(sq, tq), _tile(sk, tk)
    nq, nk = sq // tq, sk // tk

    def fwd_body(q_ref, k_ref, v_ref, o_ref, lse_ref, m_ref, l_ref, acc_ref):
        j = pl.program_id(2)

        @pl.when(j == 0)
        def _():
            m_ref[...] = jnp.full_like(m_ref, NEG)
            l_ref[...] = jnp.zeros_like(l_ref)
            acc_ref[...] = jnp.zeros_like(acc_ref)

        s = _dot(q_ref[...], k_ref[...], NT) * scale
        m_prev = m_ref[...]
        m_new = jnp.maximum(m_prev, jnp.max(s, axis=-1, keepdims=True))
        alpha = jnp.exp(m_prev - m_new)
        p = jnp.exp(s - m_new)
        l_ref[...] = alpha * l_ref[...] + jnp.sum(p, axis=-1, keepdims=True)
        acc_ref[...] = alpha * acc_ref[...] + _dot(p, v_ref[...], NN)
        m_ref[...] = m_new

        @pl.when(j == nk - 1)
        def _():
            o_ref[...] = (acc_ref[...] / l_ref[...]).astype(o_ref.dtype)
            lse_ref[...] = m_ref[...] + jnp.log(l_ref[...])

    def dq_body(q_ref, k_ref, v_ref, do_ref, lse_ref, dl_ref, dq_ref, acc_ref):
        j = pl.program_id(2)

        @pl.when(j == 0)
        def _():
            acc_ref[...] = jnp.zeros_like(acc_ref)

        s = _dot(q_ref[...], k_ref[...], NT) * scale
        p = jnp.exp(s - lse_ref[...])
        dp = _dot(do_ref[...], v_ref[...], NT)
        ds = p * (dp - dl_ref[...]) * scale
        acc_ref[...] += _dot(ds, k_ref[...], NN)

        @pl.when(j == nk - 1)
        def _():
            dq_ref[...] = acc_ref[...].astype(dq_ref.dtype)

    def dkv_body(q_ref, k_ref, v_ref, do_ref, lse_ref, dl_ref, dk_ref, dv_ref, dk_acc, dv_acc):
        i = pl.program_id(2)

        @pl.when(i == 0)
        def _():
            dk_acc[...] = jnp.zeros_like(dk_acc)
            dv_acc[...] = jnp.zeros_like(dv_acc)

        s = _dot(q_ref[...], k_ref[...], NT) * scale
        p = jnp.exp(s - lse_ref[...])
        dv_acc[...] += _dot(p, do_ref[...], TN)
        dp = _dot(do_ref[...], v_ref[...], NT)
        ds = p * (dp - dl_ref[...]) * scale
        dk_acc[...] += _dot(ds, q_ref[...], TN)

        @pl.when(i == nq - 1)
        def _():
            dk_ref[...] = dk_acc[...].astype(dk_ref.dtype)
            dv_ref[...] = dv_acc[...].astype(dv_ref.dtype)

    def q_spec(width):
        return pl.BlockSpec((None, tq, width), lambda hh, i, j: (hh, i, 0))

    def k_spec(width):
        return pl.BlockSpec((None, tk, width), lambda hh, i, j: (hh, j, 0))

    def run_fwd(q, k, v):
        return _pallas(
            fwd_body, name=name + "_fwd", grid=(h, nq, nk), in_specs=[q_spec(d), k_spec(d), k_spec(dv)],
            out_specs=[q_spec(dv), q_spec(1)],
            out_shape=[jax.ShapeDtypeStruct((h, sq, dv), q.dtype), jax.ShapeDtypeStruct((h, sq, 1), F32)],
            scratch_shapes=[pltpu.VMEM((tq, 1), F32), pltpu.VMEM((tq, 1), F32), pltpu.VMEM((tq, dv), F32)],
            compiler_params=_cparams("parallel", "parallel", "arbitrary"))(q, k, v)

    @jax.custom_vjp
    def f(q, k, v):
        return run_fwd(q, k, v)[0]

    def fwd(q, k, v):
        o, lse = run_fwd(q, k, v)
        return o, (q, k, v, o, lse)

    def bwd(res, do):
        q, k, v, o, lse = res
        delta = _rowdot(o, do, name + "_delta")
        dq = _pallas(
            dq_body, name=name + "_dq", grid=(h, nq, nk),
            in_specs=[q_spec(d), k_spec(d), k_spec(dv), q_spec(dv), q_spec(1), q_spec(1)], out_specs=q_spec(d),
            out_shape=jax.ShapeDtypeStruct((h, sq, d), q.dtype), scratch_shapes=[pltpu.VMEM((tq, d), F32)],
            compiler_params=_cparams("parallel", "parallel", "arbitrary"))(q, k, v, do, lse, delta)

        def qs(width):
            return pl.BlockSpec((None, tq, width), lambda hh, j, i: (hh, i, 0))

        def ks(width):
            return pl.BlockSpec((None, tk, width), lambda hh, j, i: (hh, j, 0))

        dk, dv_ = _pallas(
            dkv_body, name=name + "_dkv", grid=(h, nk, nq),
            in_specs=[qs(d), ks(d), ks(dv), qs(dv), qs(1), qs(1)], out_specs=[ks(d), ks(dv)],
            out_shape=[jax.ShapeDtypeStruct((h, sk, d), k.dtype), jax.ShapeDtypeStruct((h, sk, dv), v.dtype)],
            scratch_shapes=[pltpu.VMEM((tk, d), F32), pltpu.VMEM((tk, dv), F32)],
            compiler_params=_cparams("parallel", "parallel", "arbitrary"))(q, k, v, do, lse, delta)
        return dq, dk, dv_

    f.defvjp(fwd, bwd)
    return f(q, k, v)


LOG2E = 1.4426950408889634


def linear_t(a_t, w, out_dtype=F32, name="lin_t"):
    @jax.custom_vjp
    def f(a_t, w):
        return _mm(a_t, w.astype(MXU_DTYPE), ta=True, out_dtype=out_dtype, name=name + "_fwd")

    def fwd(a_t, w):
        wb = w.astype(MXU_DTYPE)
        return _mm(a_t, wb, ta=True, out_dtype=out_dtype, name=name + "_fwd"), (a_t, wb)

    def bwd(res, g):
        a_t, wb = res
        gb = g.astype(ACT_DTYPE)
        da_t = _mm(wb, gb, tb=True, out_dtype=a_t.dtype, name=name + "_dx")
        dw = _mm(a_t, gb, out_dtype=F32, name=name + "_dw")
        return da_t, dw

    f.defvjp(fwd, bwd)
    return f(a_t, w)


def mla_attention(q, k, v, scale, name="mla"):
    s, h, d = q.shape
    dv = v.shape[2]
    tq, tk = _tile(s, MLA_BWD_TQ), _tile(s, MLA_TK)
    nq, nk = s // tq, s // tk
    tqf = _tile(s, MLA_FWD_TQ)
    ones_rows = 16
    c = scale * LOG2E

    def fwd_body(qt_ref, k_ref, vt_ref, ot_ref, lse_ref, m_ref, acc_ref):
        j = pl.program_id(2)

        @pl.when(j == 0)
        def _():
            m_ref[...] = jnp.full_like(m_ref, NEG)
            acc_ref[...] = jnp.zeros_like(acc_ref)

        st = _dot(k_ref[...], qt_ref[...], NN)
        m_prev = m_ref[...]
        m_new = jnp.maximum(m_prev, jnp.max(st, axis=0, keepdims=True) * c)
        pt = jnp.exp2(st * c - m_new)
        acc_ref[...] = jnp.exp2(m_prev - m_new) * acc_ref[...] + _dot(vt_ref[...], pt, NN)
        m_ref[...] = m_new

        @pl.when(j == nk - 1)
        def _():
            l = acc_ref[dv:dv + 1, :]
            ot_ref[...] = (acc_ref[:dv, :] / l).astype(ot_ref.dtype)
            lse_ref[...] = m_ref[...] + jnp.log2(l)

    def delta_body(ot_ref, dot_ref, o_ref):
        o_ref[...] = jnp.sum(ot_ref[...].astype(F32) * dot_ref[...].astype(F32), axis=0, keepdims=True)

    def bwd_body(qt_ref, k_ref, kt_ref, v_ref, dot_ref, lse_ref, dl_ref, dqt_ref, dk_hbm, dv_hbm, dq_acc, dk_acc, dv_acc):
        hh, i, j = pl.program_id(0), pl.program_id(1), pl.program_id(2)

        @pl.when(j == 0)
        def _():
            dq_acc[...] = jnp.zeros_like(dq_acc)

        @pl.when(i == 0)
        def _():
            dk_acc[j] = jnp.zeros((d, tk), F32)
            dv_acc[j] = jnp.zeros((dv, tk), F32)

        qt, dot_ = qt_ref[...], dot_ref[...]
        pt = jnp.exp2(_dot(k_ref[...], qt, NN) * c - lse_ref[...])
        dst = (pt * (_dot(v_ref[...], dot_, NN) - dl_ref[...])).astype(MXU_DTYPE)
        dv_acc[j] += _dot(dot_, pt, NT)
        dk_acc[j] += _dot(qt, dst, NT)
        dq_acc[...] += _dot(kt_ref[...], dst, NN)

        @pl.when(j == nk - 1)
        def _():
            dqt_ref[...] = dq_acc[...] * scale

        @pl.when(i == nq - 1)
        def _():
            dk_acc[j] = dk_acc[j] * scale
            pltpu.sync_copy(dk_acc.at[j], dk_hbm.at[hh, j])
            pltpu.sync_copy(dv_acc.at[j], dv_hbm.at[hh, j])

    def qt_spec(width):
        return pl.BlockSpec((None, width, tq), lambda hh, i, j: (hh, 0, i))

    def kt_spec(width):
        return pl.BlockSpec((None, width, tk), lambda hh, i, j: (hh, 0, j))

    def k_spec(width):
        return pl.BlockSpec((None, tk, width), lambda hh, i, j: (hh, j, 0))

    def layouts(q, k, v):
        cast = lambda t: t.astype(ACT_DTYPE)
        return (cast(jnp.transpose(q, (1, 2, 0))), cast(jnp.transpose(k, (1, 0, 2))), cast(jnp.transpose(k, (1, 2, 0))),
                cast(jnp.transpose(v, (1, 0, 2))), cast(jnp.transpose(v, (1, 2, 0))))

    def run_fwd(qt, kh, vt):
        vt_ones = jnp.concatenate([vt, jnp.ones((h, ones_rows, s), vt.dtype)], axis=1)

        def qf_spec(width):
            return pl.BlockSpec((None, width, tqf), lambda hh, i, j: (hh, 0, i))

        return _pallas(
            fwd_body, name=name + "_fwd", grid=(h, s // tqf, nk), in_specs=[qf_spec(d), k_spec(d), kt_spec(dv + ones_rows)],
            out_specs=[qf_spec(dv), qf_spec(1)],
            out_shape=[jax.ShapeDtypeStruct((h, dv, s), ACT_DTYPE), jax.ShapeDtypeStruct((h, 1, s), F32)],
            scratch_shapes=[pltpu.VMEM((1, tqf), F32), pltpu.VMEM((dv + ones_rows, tqf), F32)],
            compiler_params=_cparams("parallel", "parallel", "arbitrary"))(qt, kh, vt_ones)

    @jax.custom_vjp
    def f(q, k, v):
        qt, kh, _, _, vt = layouts(q, k, v)
        return run_fwd(qt, kh, vt)[0].reshape(h * dv, s)

    def fwd(q, k, v):
        qt, kh, kt, vh, vt = layouts(q, k, v)
        ot, lse = run_fwd(qt, kh, vt)
        return ot.reshape(h * dv, s), (qt, kh, kt, vh, ot, lse)

    def bwd(res, dy):
        qt, kh, kt, vh, ot, lse = res
        dot_ = dy.reshape(h, dv, s)
        ts = _tile(s, 2048)
        col = pl.BlockSpec((None, dv, ts), lambda hh, i: (hh, 0, i))
        delta = _pallas(delta_body, name=name + "_delta", grid=(h, s // ts), in_specs=[col, col],
                        out_specs=pl.BlockSpec((None, 1, ts), lambda hh, i: (hh, 0, i)),
                        out_shape=jax.ShapeDtypeStruct((h, 1, s), F32), compiler_params=_cparams("parallel", "parallel"))(ot, dot_)
        any_spec = pl.BlockSpec(memory_space=pl.ANY)
        dqt, dkt, dvt = _pallas(
            bwd_body, name=name + "_bwd", grid=(h, nq, nk),
            in_specs=[qt_spec(d), k_spec(d), kt_spec(d), k_spec(dv), qt_spec(dv), qt_spec(1), qt_spec(1)],
            out_specs=[qt_spec(d), any_spec, any_spec],
            out_shape=[jax.ShapeDtypeStruct((h, d, s), F32), jax.ShapeDtypeStruct((h, nk, d, tk), F32),
                       jax.ShapeDtypeStruct((h, nk, dv, tk), F32)],
            scratch_shapes=[pltpu.VMEM((d, tq), F32), pltpu.VMEM((nk, d, tk), F32), pltpu.VMEM((nk, dv, tk), F32)],
            compiler_params=_cparams("parallel", "arbitrary", "arbitrary"))(qt, kh, kt, vh, dot_, lse, delta)
        to_tokens = lambda t: jnp.transpose(t, (1, 3, 0, 2)).reshape(s, h, t.shape[2])
        return jnp.transpose(dqt, (2, 0, 1)), to_tokens(dkt), to_tokens(dvt)

    f.defvjp(fwd, bwd)
    return f(q, k, v)


WATTN_TQ = 2 * C_BLOCK
WATTN_KW = WATTN_TQ + 2 * C_BLOCK


def window_attention(q, k, v, bias, sink, name="wattn"):
    hq, s, dh = q.shape
    g = hq // C_KV_HEADS
    tq, kw, half = WATTN_TQ, WATTN_KW, WATTN_KW // 2
    nt = s // tq
    scale = dh ** -0.5
    sink_b = jnp.broadcast_to(sink.astype(F32).reshape(hq, 1, 1), (hq, 1, LANES))
    neg = jnp.full((hq, C_BLOCK, C_BLOCK), NEG, F32)
    tile = jnp.concatenate(
        [jnp.concatenate([bias[:, cb - rb] if 0 <= cb - rb <= 2 else neg for cb in range(kw // C_BLOCK)], axis=2)
         for rb in range(tq // C_BLOCK)], axis=1)

    def key_bias(i):
        pos = lax.broadcasted_iota(jnp.int32, (1, kw), 1) + i * tq - C_BLOCK
        return jnp.where(jnp.logical_and(pos >= 0, pos < s), 0.0, NEG)

    def both(a_ref, b_ref):
        return jnp.concatenate([a_ref[...], b_ref[...]], axis=0)

    def fwd_body(q_ref, ka_ref, kb_ref, va_ref, vb_ref, b_ref, sk_ref, o_ref, lse_ref):
        kb_ = key_bias(pl.program_id(1))
        k_all, v_all = both(ka_ref, kb_ref), both(va_ref, vb_ref)
        for hh in range(g):
            sc = _dot(q_ref[hh], k_all, NT) * scale + b_ref[hh] + kb_
            snk = sk_ref[hh][:, :1]
            m = jnp.maximum(jnp.max(sc, axis=-1, keepdims=True), snk)
            p = jnp.exp(sc - m)
            l = jnp.sum(p, axis=-1, keepdims=True) + jnp.exp(snk - m)
            o_ref[hh] = (_dot(p, v_all, NN) / l).astype(o_ref.dtype)
            lse_ref[hh] = m + jnp.log(l)

    def bwd_body(q_ref, ka_ref, kb_ref, va_ref, vb_ref, b_ref, sk_ref, do_ref, lse_ref, dl_ref,
                 dq_ref, db_ref, dsink_ref, dk_hbm, dv_hbm, dk_acc, dv_acc):
        kv, i = pl.program_id(0), pl.program_id(1)

        @pl.when(i == 0)
        def _():
            dk_acc[...] = jnp.zeros_like(dk_acc)
            dv_acc[...] = jnp.zeros_like(dv_acc)
            db_ref[...] = jnp.zeros_like(db_ref)
            dsink_ref[...] = jnp.zeros_like(dsink_ref)

        kb_ = key_bias(i)
        k_all, v_all = both(ka_ref, kb_ref), both(va_ref, vb_ref)
        dk_t = jnp.zeros((kw, dh), F32)
        dv_t = jnp.zeros((kw, dh), F32)
        for hh in range(g):
            lse, dl, do = lse_ref[hh], dl_ref[hh], do_ref[hh]
            p = jnp.exp(_dot(q_ref[hh], k_all, NT) * scale + b_ref[hh] + kb_ - lse)
            ds = p * (_dot(do, v_all, NT) - dl)
            db_ref[hh] += ds
            total = jnp.broadcast_to(-jnp.sum(jnp.exp(sk_ref[hh][:, :1] - lse) * dl, axis=0, keepdims=True), (1, LANES))
            dsink_ref[hh] += jnp.where(lax.broadcasted_iota(jnp.int32, (1, LANES), 1) == 0, total, 0.0)
            dsb = (ds * scale).astype(MXU_DTYPE)
            dq_ref[hh] = _dot(dsb, k_all, NN).astype(dq_ref.dtype)
            dk_t += _dot(dsb, q_ref[hh], TN)
            dv_t += _dot(p, do, TN)
        rows = pl.ds(pl.multiple_of(i * tq, tq), kw)
        dk_acc[rows, :] += dk_t
        dv_acc[rows, :] += dv_t

        @pl.when(i == nt - 1)
        def _():
            pltpu.sync_copy(dk_acc, dk_hbm.at[kv])
            pltpu.sync_copy(dv_acc, dv_hbm.at[kv])

    def q_spec(width):
        return pl.BlockSpec((g, tq, width), lambda kv, i: (kv, i, 0))

    ka_spec = pl.BlockSpec((None, half, dh), lambda kv, i: (kv, i, 0))
    kb_spec = pl.BlockSpec((None, half, dh), lambda kv, i: (kv, i + 1, 0))
    b_spec = pl.BlockSpec((g, tq, kw), lambda kv, i: (kv, 0, 0))
    sk_spec = pl.BlockSpec((g, 1, LANES), lambda kv, i: (kv, 0, 0))

    def padded(t):
        return jnp.pad(t, ((0, 0), (C_BLOCK, C_BLOCK), (0, 0)))

    def run_fwd(q, kp, vp, tile, sink_b):
        return _pallas(
            fwd_body, name=name + "_fwd", grid=(C_KV_HEADS, nt),
            in_specs=[q_spec(dh), ka_spec, kb_spec, ka_spec, kb_spec, b_spec, sk_spec], out_specs=[q_spec(dh), q_spec(1)],
            out_shape=[jax.ShapeDtypeStruct((hq, s, dh), q.dtype), jax.ShapeDtypeStruct((hq, s, 1), F32)],
            compiler_params=_cparams("parallel", "parallel"))(q, kp, kp, vp, vp, tile, sink_b)

    @jax.custom_vjp
    def f(q, k, v, tile, sink_b):
        return run_fwd(q, padded(k), padded(v), tile, sink_b)[0]

    def fwd(q, k, v, tile, sink_b):
        kp, vp = padded(k), padded(v)
        o, lse = run_fwd(q, kp, vp, tile, sink_b)
        return o, (q, kp, vp, tile, sink_b, o, lse)

    def bwd(res, do):
        q, kp, vp, tile, sink_b, o, lse = res
        delta = _rowdot(o, do, name + "_delta")
        any_spec = pl.BlockSpec(memory_space=pl.ANY)
        acc = jax.ShapeDtypeStruct((C_KV_HEADS, s + 2 * C_BLOCK, dh), F32)
        dq, dtile, dsink, dkp, dvp = _pallas(
            bwd_body, name=name + "_bwd", grid=(C_KV_HEADS, nt),
            in_specs=[q_spec(dh), ka_spec, kb_spec, ka_spec, kb_spec, b_spec, sk_spec, q_spec(dh), q_spec(1), q_spec(1)],
            out_specs=[q_spec(dh), b_spec, sk_spec, any_spec, any_spec],
            out_shape=[jax.ShapeDtypeStruct((hq, s, dh), q.dtype), jax.ShapeDtypeStruct((hq, tq, kw), F32),
                       jax.ShapeDtypeStruct((hq, 1, LANES), F32), acc, acc],
            scratch_shapes=[pltpu.VMEM((s + 2 * C_BLOCK, dh), F32), pltpu.VMEM((s + 2 * C_BLOCK, dh), F32)],
            compiler_params=_cparams("parallel", "arbitrary"))(q, kp, kp, vp, vp, tile, sink_b, do, lse, delta)
        unpad = lambda t: t[:, C_BLOCK:-C_BLOCK].astype(kp.dtype)
        return dq, unpad(dkp), unpad(dvp), dtile, dsink

    f.defvjp(fwd, bwd)
    return f(q, k, v, tile, sink_b)


HG_PREP_ROWS = 256
HG_GROUP = 8
HG_INTRA_BLOCK = 256
HG_INTER_CHUNKS = 16
MLA_FWD_TQ, MLA_BWD_TQ, MLA_TK = 2048, 1024, 1024
CROSS_TQ = 1024


def _hdot(a, b, dims):
    return lax.dot_general(a, b, (dims, ((), ())), precision=lax.Precision.HIGHEST, preferred_element_type=F32)


def hgrn_prep(q, z, lb, reverse, name):
    n_hp, s, tc = q.shape
    tb = _tile(s, HG_PREP_ROWS)
    ncb = tb // B_CHUNK

    def chunk_matrices():
        r = lax.broadcasted_iota(jnp.int32, (tb, tb), 0)
        cc = lax.broadcasted_iota(jnp.int32, (tb, tb), 1)
        same = r // B_CHUNK == cc // B_CHUNK
        tri = (cc >= r) if reverse else (cc <= r)
        cum = jnp.where(jnp.logical_and(same, tri), 1.0, 0.0).astype(F32)
        every = jnp.where(same, 1.0, 0.0).astype(F32)
        pr = lax.broadcasted_iota(jnp.int32, (ncb, tb), 0)
        pc = lax.broadcasted_iota(jnp.int32, (ncb, tb), 1)
        per_chunk = jnp.where(pc // B_CHUNK == pr, 1.0, 0.0).astype(F32)
        return cum, every, per_chunk

    def gates(zv, lbv):
        e = jnp.exp(-jnp.abs(zv))
        big, small = 1.0 / (1.0 + e), e / (1.0 + e)
        sig = jnp.where(zv >= 0, big, small)
        nsig = jnp.where(zv >= 0, small, big)
        f = lbv + (1.0 - lbv) * sig
        return sig, nsig, f, jnp.log(jnp.maximum(f, TINY)), (1.0 - lbv) * nsig

    def fwd_body(q_ref, z_ref, lb_ref, qd_ref, ki_ref, ke_ref, dec_ref):
        cum, every, per_chunk = chunk_matrices()
        _, _, _, lf, key = gates(z_ref[...], lb_ref[...])
        b = _hdot(cum, lf, NN)
        tot = _hdot(every, lf, NN)
        qd_ref[...] = q_ref[...] * jnp.exp(b)
        ki_ref[...] = key * jnp.exp(-b)
        ke_ref[...] = key * jnp.exp(tot - b)
        dec_ref[...] = jnp.exp(_hdot(per_chunk, lf, NN))

    def bwd_body(q_ref, z_ref, lb_ref, dqd_ref, dki_ref, dke_ref, ddec_ref, dq_ref, dz_ref, dlb_ref):
        cum, every, per_chunk = chunk_matrices()
        lbv = lb_ref[...]
        sig, nsig, f, lf, key = gates(z_ref[...], lbv)
        b = _hdot(cum, lf, NN)
        tot = _hdot(every, lf, NN)
        e_b, e_nb, e_tb = jnp.exp(b), jnp.exp(-b), jnp.exp(tot - b)
        dqd, dki, dke = dqd_ref[...], dki_ref[...], dke_ref[...]
        dq_ref[...] = dqd * e_b
        dkey = dki * e_nb + dke * e_tb
        t_end = dke * key * e_tb
        db = dqd * q_ref[...] * e_b - dki * key * e_nb - t_end
        dtot = ddec_ref[...] * jnp.exp(_hdot(per_chunk, lf, NN)) + _hdot(per_chunk, t_end, NN)
        dlf = _hdot(cum, db, TN) + _hdot(per_chunk, dtot, TN)
        df = jnp.where(f > TINY, dlf / f, 0.0)
        one_m_lb = 1.0 - lbv
        dz_ref[...] = (df - dkey) * one_m_lb * sig * nsig
        dlb_part = jnp.sum(df * nsig - dkey * nsig, axis=0, keepdims=True)

        @pl.when(pl.program_id(1) == 0)
        def _():
            dlb_ref[...] = jnp.zeros_like(dlb_ref)

        dlb_ref[...] += dlb_part

    tok = pl.BlockSpec((None, tb, tc), lambda j, i: (j, i, 0))
    vec = pl.BlockSpec((None, 1, tc), lambda j, i: (j, 0, 0))
    chk = pl.BlockSpec((None, ncb, tc), lambda j, i: (j, i, 0))
    grid = (n_hp, s // tb)
    tok_shape = jax.ShapeDtypeStruct((n_hp, s, tc), F32)
    chk_shape = jax.ShapeDtypeStruct((n_hp, s // B_CHUNK, tc), F32)

    def run_fwd(q, z, lb):
        return _pallas(fwd_body, name=name + "_fwd", grid=grid, in_specs=[tok, tok, vec], out_specs=[tok, tok, tok, chk],
                       out_shape=[tok_shape, tok_shape, tok_shape, chk_shape],
                       compiler_params=_cparams("parallel", "parallel"))(q, z, lb)

    @jax.custom_vjp
    def f(q, z, lb):
        return tuple(run_fwd(q, z, lb))

    def fwd(q, z, lb):
        return tuple(run_fwd(q, z, lb)), (q, z, lb)

    def bwd(res, cts):
        q, z, lb = res
        dq, dz, dlb = _pallas(
            bwd_body, name=name + "_bwd", grid=grid, in_specs=[tok, tok, vec, tok, tok, tok, chk], out_specs=[tok, tok, vec],
            out_shape=[tok_shape, tok_shape, jax.ShapeDtypeStruct((n_hp, 1, tc), F32)],
            compiler_params=_cparams("parallel", "arbitrary"))(q, z, lb, *cts)
        return dq, dz, dlb

    f.defvjp(fwd, bwd)
    return f(q, z, lb)


def _pair_cols(ref, hh, width):
    return ref[:, hh * width:(hh + 1) * width]


def hgrn_intra(qd, ki, v, reverse, name):
    s = qd.shape[1]
    tb = _tile(s, HG_INTRA_BLOCK)
    wk, wv = HG_GROUP * B_DK, HG_GROUP * B_DV

    def mask():
        r = lax.broadcasted_iota(jnp.int32, (tb, tb), 0)
        c = lax.broadcasted_iota(jnp.int32, (tb, tb), 1)
        return jnp.logical_and(r // B_CHUNK == c // B_CHUNK, (c >= r) if reverse else (c <= r))

    def fwd_body(q_ref, k_ref, v_ref, o_ref):
        msk = mask()
        for hh in range(HG_GROUP):
            sc = jnp.where(msk, _dot(_pair_cols(q_ref, hh, B_DK), _pair_cols(k_ref, hh, B_DK), NT), 0.0)
            o_ref[:, hh * B_DV:(hh + 1) * B_DV] = _dot(sc, _pair_cols(v_ref, hh, B_DV), NN)

    def bwd_body(q_ref, k_ref, v_ref, do_ref, dq_ref, dk_ref, dv_ref):
        msk = mask()
        for hh in range(HG_GROUP):
            q, k = _pair_cols(q_ref, hh, B_DK), _pair_cols(k_ref, hh, B_DK)
            vv, do = _pair_cols(v_ref, hh, B_DV), _pair_cols(do_ref, hh, B_DV)
            sc = jnp.where(msk, _dot(q, k, NT), 0.0)
            ds = jnp.where(msk, _dot(do, vv, NT), 0.0)
            dq_ref[:, hh * B_DK:(hh + 1) * B_DK] = _dot(ds, k, NN)
            dk_ref[:, hh * B_DK:(hh + 1) * B_DK] = _dot(ds, q, TN)
            dv_ref[:, hh * B_DV:(hh + 1) * B_DV] = _dot(sc, do, TN)

    ks = pl.BlockSpec((None, tb, wk), lambda hp, i: (hp, i, 0))
    vs = pl.BlockSpec((None, tb, wv), lambda hp, i: (hp, i, 0))
    grid = (B_HEADS // HG_GROUP, s // tb)

    def run_fwd(qd, ki, v):
        return _pallas(fwd_body, name=name + "_fwd", grid=grid, in_specs=[ks, ks, vs], out_specs=vs,
                       out_shape=jax.ShapeDtypeStruct(v.shape, F32), compiler_params=_cparams("parallel", "parallel"))(qd, ki, v)

    @jax.custom_vjp
    def f(qd, ki, v):
        return run_fwd(qd, ki, v)

    def fwd(qd, ki, v):
        return run_fwd(qd, ki, v), (qd, ki, v)

    def bwd(res, do):
        qd, ki, v = res
        return tuple(_pallas(
            bwd_body, name=name + "_bwd", grid=grid, in_specs=[ks, ks, vs, vs], out_specs=[ks, ks, vs],
            out_shape=[jax.ShapeDtypeStruct(qd.shape, F32), jax.ShapeDtypeStruct(ki.shape, F32),
                       jax.ShapeDtypeStruct(v.shape, F32)],
            compiler_params=_cparams("parallel", "parallel"))(qd, ki, v, do))

    f.defvjp(fwd, bwd)
    return f(qd, ki, v)


def hgrn_inter(qd, ke, v, dec, reverse, name):
    s = qd.shape[1]
    nc = s // B_CHUNK
    cpb = HG_INTER_CHUNKS if nc % HG_INTER_CHUNKS == 0 else nc
    tb = cpb * B_CHUNK
    nblk = nc // cpb
    wk, wv = HG_GROUP * B_DK, HG_GROUP * B_DV
    n_hp = B_HEADS // HG_GROUP

    def rows(c):
        return pl.ds(c * B_CHUNK, B_CHUNK)

    def kcols(hh):
        return slice(hh * B_DK, (hh + 1) * B_DK)

    def vcols(hh):
        return slice(hh * B_DV, (hh + 1) * B_DV)

    def order(flip):
        return reversed(range(cpb)) if flip else range(cpb)

    def fwd_body(q_ref, k_ref, v_ref, dec_ref, o_ref, st_ref, state):
        @pl.when(pl.program_id(1) == 0)
        def _():
            state[...] = jnp.zeros_like(state)

        for c in order(reverse):
            for hh in range(HG_GROUP):
                st = state[hh]
                st_ref[c, hh] = st
                o_ref[rows(c), vcols(hh)] = _dot(q_ref[rows(c), kcols(hh)], st, NT)
                state[hh] = st * dec_ref[pl.ds(c, 1), kcols(hh)] + _dot(v_ref[rows(c), vcols(hh)], k_ref[rows(c), kcols(hh)], TN)

    def bwd_body(q_ref, k_ref, v_ref, dec_ref, st_ref, do_ref, dq_ref, dk_ref, dv_ref, ddec_ref, dstate):
        @pl.when(pl.program_id(1) == 0)
        def _():
            dstate[...] = jnp.zeros_like(dstate)

        for c in order(not reverse):
            for hh in range(HG_GROUP):
                dst = dstate[hh]
                st = st_ref[c, hh]
                do_c = do_ref[rows(c), vcols(hh)]
                dk_ref[rows(c), kcols(hh)] = _dot(v_ref[rows(c), vcols(hh)], dst, NN)
                dv_ref[rows(c), vcols(hh)] = _dot(k_ref[rows(c), kcols(hh)], dst, NT)
                ddec_ref[pl.ds(c, 1), kcols(hh)] = jnp.sum(dst * st, axis=0, keepdims=True)
                dq_ref[rows(c), kcols(hh)] = _dot(do_c, st, NN)
                dstate[hh] = dst * dec_ref[pl.ds(c, 1), kcols(hh)] + _dot(do_c, q_ref[rows(c), kcols(hh)], TN)

    def specs(flip):
        blk = (lambda i: nblk - 1 - i) if flip else (lambda i: i)
        tok_k = pl.BlockSpec((None, tb, wk), lambda hp, i: (hp, blk(i), 0))
        tok_v = pl.BlockSpec((None, tb, wv), lambda hp, i: (hp, blk(i), 0))
        chk = pl.BlockSpec((None, cpb, wk), lambda hp, i: (hp, blk(i), 0))
        sts = pl.BlockSpec((None, cpb, HG_GROUP, B_DV, B_DK), lambda hp, i: (hp, blk(i), 0, 0, 0))
        return tok_k, tok_v, chk, sts

    scratch = [pltpu.VMEM((HG_GROUP, B_DV, B_DK), F32)]

    def run_fwd(qd, ke, v, dec):
        tok_k, tok_v, chk, sts = specs(reverse)
        return _pallas(
            fwd_body, name=name + "_fwd", grid=(n_hp, nblk), in_specs=[tok_k, tok_k, tok_v, chk], out_specs=[tok_v, sts],
            out_shape=[jax.ShapeDtypeStruct(v.shape, F32), jax.ShapeDtypeStruct((n_hp, nc, HG_GROUP, B_DV, B_DK), F32)],
            scratch_shapes=scratch, compiler_params=_cparams("parallel", "arbitrary"))(qd, ke, v, dec)

    @jax.custom_vjp
    def f(qd, ke, v, dec):
        return run_fwd(qd, ke, v, dec)[0]

    def fwd(qd, ke, v, dec):
        o, st = run_fwd(qd, ke, v, dec)
        return o, (qd, ke, v, dec, st)

    def bwd(res, do):
        qd, ke, v, dec, st = res
        tok_k, tok_v, chk, sts = specs(not reverse)
        return tuple(_pallas(
            bwd_body, name=name + "_bwd", grid=(n_hp, nblk), in_specs=[tok_k, tok_k, tok_v, chk, sts, tok_v],
            out_specs=[tok_k, tok_k, tok_v, chk],
            out_shape=[jax.ShapeDtypeStruct(qd.shape, F32), jax.ShapeDtypeStruct(ke.shape, F32),
                       jax.ShapeDtypeStruct(v.shape, F32), jax.ShapeDtypeStruct(dec.shape, F32)],
            scratch_shapes=scratch, compiler_params=_cparams("parallel", "arbitrary"))(qd, ke, v, dec, st, do))

    f.defvjp(fwd, bwd)
    return f(qd, ke, v, dec)


def loss_head(y, target, name="loss"):
    s, d = y.shape
    tr = _row_tile(s, d)

    def body(y_ref, t_ref, o_ref):
        @pl.when(pl.program_id(0) == 0)
        def _():
            o_ref[...] = jnp.zeros_like(o_ref)

        e = y_ref[...] - t_ref[...]
        part = jnp.sum(jnp.sum(e * e, axis=-1, keepdims=True), axis=0, keepdims=True) * (0.5 / d)
        o_ref[...] += jnp.broadcast_to(part, o_ref.shape)

    spec = pl.BlockSpec((tr, d), lambda i: (i, 0))

    def run(y, t):
        out = _pallas(body, name=name, grid=(s // tr,), in_specs=[spec, spec],
                      out_specs=pl.BlockSpec((SUBLANES, LANES), lambda i: (0, 0)),
                      out_shape=jax.ShapeDtypeStruct((SUBLANES, LANES), F32), compiler_params=_cparams("arbitrary"))(y, t)
        return out[0, 0]

    @jax.custom_vjp
    def f(y, t):
        return run(y, t)

    def fwd(y, t):
        return run(y, t), (y, t)

    def bwd(res, g):
        y, t = res
        dy = g * (y - t) * (1.0 / d)
        return dy, -dy

    f.defvjp(fwd, bwd)
    return f(y, target)


def _mesh_pos():
    return lax.axis_index("x"), lax.axis_index("y"), lax.axis_index("c")


def all_gather_shards(shards):
    n = len(shards)

    def body(*refs):
        ins, outs = refs[:n], refs[n:2 * n]
        send_sems, recv_sems, local_sems = refs[2 * n:]
        x, y, c = _mesh_pos()
        me, sibling = (x, y, c), (x, y, 1 - c)
        chips = [(1 - x, y), (x, 1 - y), (1 - x, 1 - y)]

        def slot(t, px, py, pc):
            return outs[t].at[4 * px + 2 * py + pc]

        def copy(t, k, block, to, src=None):
            return pltpu.make_async_remote_copy(
                src_ref=slot(t, *block) if src is None else src, dst_ref=slot(t, *block), send_sem=send_sems.at[t, k],
                recv_sem=recv_sems.at[t, k], device_id=to, device_id_type=pl.DeviceIdType.MESH)

        mine = [pltpu.make_async_copy(ins[t], slot(t, *me), local_sems.at[t]) for t in range(n)]
        for cp in mine:
            cp.start()
        first = []
        for t in range(n):
            first.append(copy(t, 0, me, sibling, src=ins[t]))
            first += [copy(t, 1 + j, me, (*chip, c), src=ins[t]) for j, chip in enumerate(chips)]
        for cp in first:
            cp.start()
        passed = []
        for j, chip in enumerate(chips):
            for t in range(n):
                copy(t, 1 + j, (*chip, c), me).wait_recv()
                cp = copy(t, 4 + j, (*chip, c), sibling)
                cp.start()
                passed.append(cp)
        for t in range(n):
            copy(t, 0, sibling, me).wait_recv()
            for j, chip in enumerate(chips):
                copy(t, 4 + j, (*chip, 1 - c), me).wait_recv()
        for cp in first + passed:
            cp.wait_send()
        for cp in mine:
            cp.wait()

    any_spec = pl.BlockSpec(memory_space=pl.ANY)
    return _pallas(
        body, name="all_gather_weights", out_shape=[jax.ShapeDtypeStruct((N_DEV, *s.shape), s.dtype) for s in shards],
        in_specs=[any_spec] * n, out_specs=[any_spec] * n,
        scratch_shapes=[pltpu.SemaphoreType.DMA((n, 7)), pltpu.SemaphoreType.DMA((n, 7)), pltpu.SemaphoreType.DMA((n,))],
    )(*shards)


def all_to_all_blocks(stacks):
    n = len(stacks)

    def body(*refs):
        ins, outs = refs[:n], refs[n:2 * n]
        send_sems, recv_sems, local_sems = refs[2 * n:]
        x, y, c = _mesh_pos()
        me = 4 * x + 2 * y + c
        mine = [pltpu.make_async_copy(ins[t].at[me], outs[t].at[me], local_sems.at[t]) for t in range(n)]
        for cp in mine:
            cp.start()
        copies = []
        for k in range(1, N_DEV):
            px = 1 - x if k & 4 else x
            py = 1 - y if k & 2 else y
            pc = 1 - c if k & 1 else c
            for t in range(n):
                cp = pltpu.make_async_remote_copy(
                    src_ref=ins[t].at[4 * px + 2 * py + pc], dst_ref=outs[t].at[me], send_sem=send_sems.at[t, k - 1],
                    recv_sem=recv_sems.at[t, k - 1], device_id=(px, py, pc), device_id_type=pl.DeviceIdType.MESH)
                cp.start()
                copies.append(cp)
        for cp in copies:
            cp.wait_recv()
        for cp in copies:
            cp.wait_send()
        for cp in mine:
            cp.wait()

    any_spec = pl.BlockSpec(memory_space=pl.ANY)
    return _pallas(
        body, name="all_to_all_grads", out_shape=[jax.ShapeDtypeStruct(s.shape, s.dtype) for s in stacks],
        in_specs=[any_spec] * n, out_specs=[any_spec] * n,
        scratch_shapes=[pltpu.SemaphoreType.DMA((n, 7)), pltpu.SemaphoreType.DMA((n, 7)), pltpu.SemaphoreType.DMA((n,))],
    )(*stacks)


def all_gather_small(v):
    r, w = v.shape

    def body(x_ref, out_ref, send_sems, recv_sems):
        x, y, c = _mesh_pos()
        me = 4 * x + 2 * y + c
        copies = []
        for k in range(1, N_DEV):
            px = 1 - x if k & 4 else x
            py = 1 - y if k & 2 else y
            pc = 1 - c if k & 1 else c
            cp = pltpu.make_async_remote_copy(
                src_ref=x_ref, dst_ref=out_ref.at[me], send_sem=send_sems.at[k - 1], recv_sem=recv_sems.at[k - 1],
                device_id=(px, py, pc), device_id_type=pl.DeviceIdType.MESH)
            cp.start()
            copies.append(cp)
        out_ref[me] = x_ref[...]
        for cp in copies:
            cp.wait_recv()
        for cp in copies:
            cp.wait_send()

    vmem = pl.BlockSpec(memory_space=pltpu.VMEM)
    return _pallas(
        body, name="all_gather_small", out_shape=jax.ShapeDtypeStruct((N_DEV, r, w), v.dtype), in_specs=[vmem],
        out_specs=vmem, scratch_shapes=[pltpu.SemaphoreType.DMA((7,)), pltpu.SemaphoreType.DMA((7,))],
    )(v)


def adamw_rows(parts, w, m, v, name):
    n, r, lanes = parts.shape
    tr = _tile(r, max(SUBLANES, (256 * 1024) // lanes), SUBLANES)
    c1 = 1.0 / (1.0 - ADAM_B1 ** ADAM_STEP)
    c2 = 1.0 / (1.0 - ADAM_B2 ** ADAM_STEP)

    def body(p_ref, w_ref, m_ref, v_ref, g_ref, d_ref, nm_ref, nv_ref):
        g = p_ref[0].astype(F32)
        for j in range(1, n):
            g = g + p_ref[j].astype(F32)
        nm = ADAM_B1 * m_ref[...] + (1.0 - ADAM_B1) * g
        nv = ADAM_B2 * v_ref[...] + (1.0 - ADAM_B2) * (g * g)
        g_ref[...] = g
        nm_ref[...] = nm
        nv_ref[...] = nv
        d_ref[...] = -ADAM_LR * ((nm * c1) / (jnp.sqrt(nv * c2) + ADAM_EPS) + ADAM_WD * w_ref[...])

    row = pl.BlockSpec((tr, lanes), lambda i: (i, 0))
    out = jax.ShapeDtypeStruct((r, lanes), F32)
    return _pallas(body, name=name, grid=(r // tr,), in_specs=[pl.BlockSpec((n, tr, lanes), lambda i: (0, i, 0)), row, row, row],
                   out_specs=[row, row, row, row], out_shape=[out, out, out, out], compiler_params=_cparams("parallel"))(
        parts, w, m, v)


def _padded(n):
    return -(-n // PACK_QUANTUM) * PACK_QUANTUM


def _pack(pieces, total_rows=None):
    flat = []
    for p in pieces:
        p = p.reshape(-1).astype(F32)
        flat.append(jnp.pad(p, (0, _padded(p.size) - p.size)))
    out = jnp.concatenate(flat).reshape(-1, LANES)
    if total_rows is not None and out.shape[0] != total_rows:
        out = jnp.pad(out, ((0, total_rows - out.shape[0]), (0, 0)))
    return out


def _pack_rows(sizes):
    rows = sum(_padded(n) for n in sizes) // LANES
    return -(-rows // PACK_ROW_TILE) * PACK_ROW_TILE


def _unpack(rows, shapes):
    lead = rows.shape[:-2]
    flat = rows.reshape(*lead, -1)
    out, off = [], 0
    for shp in shapes:
        n = int(np.prod(shp))
        out.append(flat[..., off:off + n].reshape(*lead, *shp))
        off += _padded(n)
    return out


def _shards_to_full(stacked, axis):
    moved = jnp.moveaxis(stacked, 0, axis)
    shp = list(stacked.shape[1:])
    shp[axis] *= N_DEV
    return moved.reshape(shp)


def _full_to_shards(full, axis):
    shp = list(full.shape)
    shp[axis:axis + 1] = [N_DEV, shp[axis] // N_DEV]
    return jnp.moveaxis(full.reshape(shp), axis, 0)


def _heads(t, n, d):
    return jnp.transpose(t.reshape(t.shape[0], n, d), (1, 0, 2)).astype(ACT_DTYPE)


def _unheads(t):
    return jnp.transpose(t, (1, 0, 2)).reshape(t.shape[1], -1)


def _rope_tables(s):
    half = A_ROPE // 2
    inv = ROPE_THETA ** (-jnp.arange(half, dtype=F32) / half)
    ang = jnp.arange(s, dtype=jnp.int32).astype(F32)[:, None] * inv[None, :]
    return jnp.cos(ang), jnp.sin(ang)


def _rope(t, cos, sin):
    half = A_ROPE // 2
    t1, t2 = t[..., :half], t[..., half:]
    c, sn = cos[:, None, :], sin[:, None, :]
    return jnp.concatenate([t1 * c - t2 * sn, t1 * sn + t2 * c], axis=-1)


def _t5_bucket(rel):
    nb = REL_BUCKETS // 2
    max_exact = nb // 2
    ret = (rel > 0).astype(jnp.int32) * nb
    n = jnp.abs(rel)
    large = max_exact + (jnp.log(jnp.maximum(n, 1).astype(F32) / max_exact)
                         / math.log(REL_MAX_DIST / max_exact) * (nb - max_exact)).astype(jnp.int32)
    large = jnp.minimum(large, nb - 1)
    return ret + jnp.where(n < max_exact, n, large)


def _window_bias(rel_bias):
    span = 3 * C_BLOCK
    rel = jnp.arange(span)[None, :] - C_BLOCK - jnp.arange(C_BLOCK)[:, None]
    onehot = (_t5_bucket(rel)[..., None] == jnp.arange(REL_BUCKETS)).astype(F32)
    bias = jnp.einsum("qkb,bh->hqk", onehot, rel_bias.astype(F32), precision=lax.Precision.HIGHEST)
    bias = jnp.where((jnp.abs(rel) <= C_WINDOW)[None], bias, NEG)
    return jnp.transpose(bias.reshape(C_HEADS, C_BLOCK, 3, C_BLOCK), (0, 2, 1, 3))


def _mla(cq, ckv, kr, gq, gkv, wuq, wukv, cos, sin):
    s = cq.shape[0]
    q = linear(rmsnorm(cq, gq, ACT_DTYPE, "rms_cq"), wuq, name="a_wuq").reshape(s, A_HEADS, A_NOPE + A_ROPE)
    q = jnp.concatenate([q[..., :A_NOPE], _rope(q[..., A_NOPE:], cos, sin)], axis=-1)
    kv = linear(rmsnorm(ckv, gkv, ACT_DTYPE, "rms_ckv"), wukv, name="a_wukv").reshape(s, A_HEADS, A_NOPE + A_V)
    k_rope = jnp.broadcast_to(_rope(kr[:, None, :], cos, sin), (s, A_HEADS, A_ROPE))
    k = jnp.concatenate([kv[..., :A_NOPE], k_rope], axis=-1)
    v = kv[..., A_NOPE:]
    return mla_attention(q, k, v, (A_NOPE + A_ROPE) ** -0.5)


def _hgrn2(q, f_fwd, f_bwd, i, g, lb_fwd, lb_bwd, g_out):
    s = q.shape[0]
    n_hp = B_HEADS // HG_GROUP
    pairs = lambda t: jnp.transpose(t.reshape(s, n_hp, -1), (1, 0, 2))
    qp, vp = pairs(q), pairs(i)
    o = None
    for z, lb, rev, tag in ((f_fwd, lb_fwd, False, "hgf"), (f_bwd, lb_bwd, True, "hgb")):
        qd, ki, ke, dec = hgrn_prep(qp, pairs(z), lb.astype(F32).reshape(n_hp, 1, -1), rev, tag + "_prep")
        part = hgrn_intra(qd, ki, vp, rev, tag + "_intra") + hgrn_inter(qd, ke, vp, dec, rev, tag + "_inter")
        o = part if o is None else o + part
    o = jnp.transpose(o, (1, 0, 2)).reshape(s * B_HEADS, B_DV)
    o = rmsnorm(o, g_out, F32, "rms_hg").reshape(s, B_HEADS * B_DV)
    return o * jax.nn.silu(g)


def _cross(h, mem_n, wq, wkv, wo):
    q = _heads(linear(h, wq, name="x_wq"), X_HEADS, X_DH)
    kv = linear(mem_n, wkv, name="x_wkv").reshape(mem_n.shape[0], 2, X_HEADS, X_DH)
    k = jnp.transpose(kv[:, 0], (1, 0, 2)).astype(ACT_DTYPE)
    v = jnp.transpose(kv[:, 1], (1, 0, 2)).astype(ACT_DTYPE)
    o = attention(q, k, v, X_DH ** -0.5, CROSS_TQ, 256, "cross")
    return linear(_unheads(o), wo, name="x_wo")


def _pad_w_in(w):
    cut = A_Q_RANK + A_KV_RANK + A_ROPE
    return jnp.concatenate([w[:, :cut], jnp.zeros((w.shape[0], KR_PAD), w.dtype), w[:, cut:]], axis=1)


def _model_loss(p, x, mem, target):
    s = x.shape[0]
    cos, sin = _rope_tables(s)
    sm = jax.nn.softmax(p["b_lb"].astype(F32), axis=1)
    lower_bounds = jnp.cumsum(sm, axis=1) - sm[:, :1]
    bias = _window_bias(p["rel_bias"])
    for l in range(DEPTH):
        h = rmsnorm(x, p["g_mix"][l], ACT_DTYPE, "rms_mix")
        z = linear(h, _pad_w_in(p["w_in"][l]), name="w_in")
        parts, start = [], 0
        for width in IN_SPLITS_PADDED:
            parts.append(z[:, start:start + width])
            start += width
        a_cq, a_ckv, a_kr, b_q, b_ff, b_fb, b_i, b_g, c_q, c_k, c_v, gate_a, gate_b, gate_c = parts
        y_a = _mla(a_cq, a_ckv, a_kr[:, :A_ROPE], p["a_gq"][l], p["a_gkv"][l], p["a_wuq"][l], p["a_wukv"][l], cos, sin)
        y_b = _hgrn2(b_q, b_ff, b_fb, b_i, b_g, lower_bounds[0, l], lower_bounds[1, l], p["b_gout"][l])
        y_c = _unheads(window_attention(_heads(c_q, C_HEADS, C_DH), _heads(c_k, C_KV_HEADS, C_DH),
                                        _heads(c_v, C_KV_HEADS, C_DH), bias, p["c_sink"][l]))
        merged = (jax.nn.sigmoid(gate_a) * linear_t(y_a, p["w_br_a"][l], name="w_br_a")
                  + jax.nn.sigmoid(gate_b) * linear(y_b, p["w_br_b"][l], name="w_br_b")
                  + jax.nn.sigmoid(gate_c) * linear(y_c, p["w_br_c"][l], name="w_br_c"))
        x = x + linear(merged, p["w_out"][l], name="w_out")
        h = rmsnorm(x, p["g_x"][l], ACT_DTYPE, "rms_x")
        x = x + _cross(h, rmsnorm(mem, p["g_mem"][l], ACT_DTYPE, "rms_mem"), p["x_wq"][l], p["x_wkv"][l], p["x_wo"][l])
        h = rmsnorm(x, p["g_ffn"][l], ACT_DTYPE, "rms_ffn")
        t = jax.nn.silu(linear(h, p["f_w1"][l], name="f_w1")) * linear(h, p["f_w3"][l], name="f_w3")
        x = x + linear(t, p["f_w2"][l], name="f_w2")
    y = rmsnorm(x, p["g_final"], F32, "rms_final")
    return loss_head(y, target)


def kernel(x, mem, w_in, g_mix, a_gq, a_gkv, a_wuq, a_wukv, b_lb, b_gout, c_sink, rel_bias, w_br_a, w_br_b, w_br_c, w_out, g_x, g_mem, x_wq, x_wkv, x_wo, g_ffn, f_w1, f_w3, f_w2, g_final, loss_target, m_w_in, m_g_mix, m_a_gq, m_a_gkv, m_a_wuq, m_a_wukv, m_b_lb, m_b_gout, m_c_sink, m_rel_bias, m_w_br_a, m_w_br_b, m_w_br_c, m_w_out, m_g_x, m_g_mem, m_x_wq, m_x_wkv, m_x_wo, m_g_ffn, m_f_w1, m_f_w3, m_f_w2, m_g_final, v_w_in, v_g_mix, v_a_gq, v_a_gkv, v_a_wuq, v_a_wukv, v_b_lb, v_b_gout, v_c_sink, v_rel_bias, v_w_br_a, v_w_br_b, v_w_br_c, v_w_out, v_g_x, v_g_mem, v_x_wq, v_x_wkv, v_x_wo, v_g_ffn, v_f_w1, v_f_w3, v_f_w2, v_g_final):
    given = dict(locals())
    w = {n: given[n] for n in WEIGHT_ORDER}
    m = {n: given["m_" + n] for n in WEIGHT_ORDER}
    v = {n: given["v_" + n] for n in WEIGHT_ORDER}
    sh_names = [n for n, _ in SHARDED]
    rep_shapes = [w[n].shape for n in REPLICATED] + [(1,)]
    rep_rows = _pack_rows([int(np.prod(s)) for s in rep_shapes])

    wire = [w[n] if n in ELEMENTWISE_SHARDED else w[n].astype(MXU_DTYPE) for n in sh_names]
    gathered = all_gather_shards(wire)
    full = {n: _shards_to_full(t, ax).astype(F32) for (n, ax), t in zip(SHARDED, gathered)}
    full.update({n: w[n] for n in REPLICATED})

    loss, (grad_full, grad_x) = jax.value_and_grad(_model_loss, argnums=(0, 1))(full, x[0], mem[0], loss_target[0])

    received = all_to_all_blocks([_full_to_shards(grad_full[n], ax).astype(GRAD_WIRE_DTYPE) for n, ax in SHARDED])
    g_sh, d_sh, nm_sh, nv_sh = {}, {}, {}, {}
    for n, got in zip(sh_names, received):
        shp = w[n].shape
        rows = lambda t: t.reshape(-1, shp[-1])
        outs = adamw_rows(got.reshape(N_DEV, -1, shp[-1]), rows(w[n]), rows(m[n]), rows(v[n]), "adamw_" + n)
        g_sh[n], d_sh[n], nm_sh[n], nv_sh[n] = [o.reshape(shp) for o in outs]

    mine = _pack([grad_full[n] for n in REPLICATED] + [loss.reshape(1)], rep_rows)
    everyone = all_gather_small(mine)
    rep_w = [w[n] for n in REPLICATED] + [jnp.zeros((1,), F32)]
    outs = adamw_rows(everyone, _pack(rep_w, rep_rows), _pack([m[n] for n in REPLICATED] + [jnp.zeros((1,), F32)], rep_rows),
                      _pack([v[n] for n in REPLICATED] + [jnp.ones((1,), F32)], rep_rows), "adamw_replicated")
    rep_names = list(REPLICATED) + ["loss"]
    g_rp, d_rp, nm_rp, nv_rp = [dict(zip(rep_names, _unpack(o, rep_shapes))) for o in outs]

    def pick(sharded, replicated, n):
        return sharded[n] if n in sharded else replicated[n]

    return (g_rp["loss"].reshape(()), grad_x[None],
            *[pick(g_sh, g_rp, n) for n in WEIGHT_ORDER], *[pick(d_sh, d_rp, n) for n in WEIGHT_ORDER],
            *[pick(nm_sh, nm_rp, n) for n in WEIGHT_ORDER], *[pick(nv_sh, nv_rp, n) for n in WEIGHT_ORDER])
```

```python
import functools
import math

import jax
import jax.numpy as jnp
import numpy as np
from jax import lax
from jax.experimental import pallas as pl
from jax.experimental.pallas import tpu as pltpu

F32 = jnp.float32
MXU_DTYPE = jnp.bfloat16
ACT_DTYPE = jnp.bfloat16
GRAD_WIRE_DTYPE = jnp.bfloat16

V7X_VMEM_LIMIT_BYTES = 56 * 1024 * 1024
LANES = 128
SUBLANES = 8

N_DEV = 8
D_MODEL = 1024
DEPTH = 2
EPS = 1e-6
TINY = 1e-30
NEG = -1e30

A_HEADS, A_NOPE, A_ROPE, A_V, A_Q_RANK, A_KV_RANK = 8, 64, 32, 64, 384, 256
ROPE_THETA = 10000.0
B_HEADS, B_DK, B_DV, B_CHUNK = 8, 128, 64, 16
C_HEADS, C_KV_HEADS, C_DH, C_WINDOW, C_BLOCK = 8, 2, 64, 128, 128
REL_BUCKETS, REL_MAX_DIST = 32, 128
X_HEADS, X_DH = 4, 256
D_FF = 2816
IN_SPLITS = (A_Q_RANK, A_KV_RANK, A_ROPE, 1024, 1024, 1024, 512, 512, 512, 128, 128, 1024, 1024, 1024)
IN_WIDTH = sum(IN_SPLITS)
KR_PAD = LANES - A_ROPE
IN_SPLITS_PADDED = (A_Q_RANK, A_KV_RANK, LANES, 1024, 1024, 1024, 512, 512, 512, 128, 128, 1024, 1024, 1024)

ADAM_LR, ADAM_B1, ADAM_B2, ADAM_EPS, ADAM_WD, ADAM_STEP = 0.001, 0.9, 0.999, 1e-08, 0.01, 10

SHARDED = (("w_in", 2), ("a_wuq", 2), ("a_wukv", 2), ("b_lb", 2), ("w_br_a", 2), ("w_br_b", 2), ("w_br_c", 2),
           ("w_out", 1), ("x_wq", 1), ("x_wkv", 2), ("x_wo", 1), ("f_w1", 2), ("f_w3", 2), ("f_w2", 1))
ELEMENTWISE_SHARDED = ("b_lb",)
REPLICATED = ("g_mix", "a_gq", "a_gkv", "b_gout", "c_sink", "rel_bias", "g_x", "g_mem", "g_ffn", "g_final")
WEIGHT_ORDER = ("w_in", "g_mix", "a_gq", "a_gkv", "a_wuq", "a_wukv", "b_lb", "b_gout", "c_sink", "rel_bias", "w_br_a",
                "w_br_b", "w_br_c", "w_out", "g_x", "g_mem", "x_wq", "x_wkv", "x_wo", "g_ffn", "f_w1", "f_w3", "f_w2",
                "g_final")
PACK_QUANTUM = SUBLANES * LANES
PACK_ROW_TILE = 512


def _pallas(body, **kw):
    return pl.pallas_call(body, **kw)


def _cparams(*sem):
    return pltpu.CompilerParams(dimension_semantics=sem, vmem_limit_bytes=V7X_VMEM_LIMIT_BYTES)


def _tile(n, target, mult=LANES):
    t = (min(target, n) // mult) * mult
    while t >= mult:
        if n % t == 0:
            return t
        t -= mult
    return n


def _dot(a, b, dims):
    return lax.dot_general(a.astype(MXU_DTYPE), b.astype(MXU_DTYPE), (dims, ((), ())), preferred_element_type=F32)


NN = ((1,), (0,))
NT = ((1,), (1,))
TN = ((0,), (0,))


MM_VMEM_BUDGET_BYTES = 40 * 1024 * 1024
MM_MAX_TILE = 4352
MM_MAX_ROW_TILE = 2048
MM_HBM_BYTES_PER_S = 2.5e12
MM_STEP_S = 0.4e-6
MM_DMA_ROW_OVERHEAD_BYTES = 512.0


def _tile_options(n, cap):
    out = [t for t in range(LANES, min(n, cap) + 1, LANES) if n % t == 0]
    if n <= cap and n not in out:
        out.append(n)
    return out or [n]


@functools.lru_cache(maxsize=None)
def _mm_plan(m, n, k, ta, tb, a_bytes, b_bytes, o_bytes):
    best = None
    for tk in _tile_options(k, MM_MAX_TILE):
        nk = k // tk
        for tn in _tile_options(n, MM_MAX_TILE):
            for tm in _tile_options(m, MM_MAX_ROW_TILE):
                vmem = 2 * (tm * tk * a_bytes + tk * tn * b_bytes + tm * tn * o_bytes) + tm * tn * 4
                vmem += (tm * tk * 2 if a_bytes == 4 else 0) + (tk * tn * 2 if b_bytes == 4 else 0)
                if vmem > MM_VMEM_BUDGET_BYTES:
                    continue

                def eff(elems, nbytes):
                    return (elems * nbytes) / (elems * nbytes + MM_DMA_ROW_OVERHEAD_BYTES)

                ea, eb, eo = eff(tm if ta else tk, a_bytes), eff(tk if tb else tn, b_bytes), eff(tn, o_bytes)
                for order in ("mn", "nm"):
                    if nk == 1 and order == "nm":
                        a_tr, b_tr = m * k * a_bytes * (n // tn), k * n * b_bytes
                    elif nk == 1:
                        a_tr, b_tr = m * k * a_bytes, k * n * b_bytes * (m // tm)
                    else:
                        a_tr, b_tr = m * k * a_bytes * (n // tn), k * n * b_bytes * (m // tm)
                    steps = (m // tm) * (n // tn) * nk
                    cost = (a_tr / ea + b_tr / eb + m * n * o_bytes / eo) / MM_HBM_BYTES_PER_S + steps * MM_STEP_S
                    if best is None or cost < best[0]:
                        best = (cost, tm, tn, tk, order)
    assert best is not None, (m, n, k)
    return best[1:]


def _mm(a, b, ta=False, tb=False, out_dtype=F32, name="mm", res=None):
    m, k = (a.shape[1], a.shape[0]) if ta else a.shape
    kb, n = (b.shape[1], b.shape[0]) if tb else b.shape
    assert k == kb, (a.shape, b.shape, ta, tb)
    tm, tn, tk, order = _mm_plan(m, n, k, ta, tb, a.dtype.itemsize, b.dtype.itemsize, jnp.dtype(out_dtype).itemsize)
    nk = k // tk
    dims = ((0 if ta else 1,), (1 if tb else 0,))
    out_shape = jax.ShapeDtypeStruct((m, n), out_dtype)

    assert res is None or nk == 1, (name, k, tk)
    if nk == 1:
        def body(a_ref, b_ref, *rest):
            acc = _dot(a_ref[...], b_ref[...], dims)
            if res is not None:
                acc = rest[0][...] + acc
            rest[-1][...] = acc.astype(rest[-1].dtype)

        if order == "nm":
            mi, ni = (lambda j, i: i), (lambda j, i: j)
            grid = (n // tn, m // tm)
        else:
            mi, ni = (lambda i, j: i), (lambda i, j: j)
            grid = (m // tm, n // tn)
        a_spec = pl.BlockSpec((tk, tm), lambda p, q: (0, mi(p, q))) if ta else pl.BlockSpec((tm, tk), lambda p, q: (mi(p, q), 0))
        b_spec = pl.BlockSpec((tn, tk), lambda p, q: (ni(p, q), 0)) if tb else pl.BlockSpec((tk, tn), lambda p, q: (0, ni(p, q)))
        o_spec = pl.BlockSpec((tm, tn), lambda p, q: (mi(p, q), ni(p, q)))
        extra = [] if res is None else [res]
        return _pallas(body, name=name, grid=grid, in_specs=[a_spec, b_spec] + [o_spec] * len(extra), out_specs=o_spec,
                       out_shape=out_shape, compiler_params=_cparams("parallel", "parallel"))(a, b, *extra)

    direct = jnp.dtype(out_dtype) == jnp.dtype(F32)

    def body(a_ref, b_ref, o_ref, *scratch):
        acc_ref = o_ref if direct else scratch[0]
        kk = pl.program_id(2)

        @pl.when(kk == 0)
        def _():
            acc_ref[...] = jnp.zeros_like(acc_ref)

        acc_ref[...] += _dot(a_ref[...], b_ref[...], dims)

        if not direct:
            @pl.when(kk == nk - 1)
            def _():
                o_ref[...] = acc_ref[...].astype(o_ref.dtype)

    a_spec = pl.BlockSpec((tk, tm), lambda i, j, kk: (kk, i)) if ta else pl.BlockSpec((tm, tk), lambda i, j, kk: (i, kk))
    b_spec = pl.BlockSpec((tn, tk), lambda i, j, kk: (j, kk)) if tb else pl.BlockSpec((tk, tn), lambda i, j, kk: (kk, j))
    return _pallas(
        body, name=name, grid=(m // tm, n // tn, nk), in_specs=[a_spec, b_spec],
        out_specs=pl.BlockSpec((tm, tn), lambda i, j, kk: (i, j)), out_shape=out_shape,
        scratch_shapes=[] if direct else [pltpu.VMEM((tm, tn), F32)],
        compiler_params=_cparams("parallel", "parallel", "arbitrary"),
    )(a, b)


def linear(a, w, out_dtype=F32, name="lin"):
    @jax.custom_vjp
    def f(a, w):
        return _mm(a.astype(ACT_DTYPE), w.astype(MXU_DTYPE), out_dtype=out_dtype, name=name + "_fwd")

    def fwd(a, w):
        ab, wb = a.astype(ACT_DTYPE), w.astype(MXU_DTYPE)
        return _mm(ab, wb, out_dtype=out_dtype, name=name + "_fwd"), (ab, wb, jnp.zeros((0,), a.dtype))

    def bwd(res, g):
        ab, wb, like_a = res
        gb = g.astype(ACT_DTYPE)
        da = _mm(gb, wb, tb=True, out_dtype=like_a.dtype, name=name + "_dx")
        dw = _mm(ab, gb, ta=True, out_dtype=F32, name=name + "_dw")
        return da, dw

    f.defvjp(fwd, bwd)
    return f(a, w)


def linear_res(x, a, w, name="lin"):
    @jax.custom_vjp
    def f(x, a, w):
        return _mm(a.astype(ACT_DTYPE), w.astype(MXU_DTYPE), name=name + "_fwd", res=x)

    def fwd(x, a, w):
        ab, wb = a.astype(ACT_DTYPE), w.astype(MXU_DTYPE)
        return _mm(ab, wb, name=name + "_fwd", res=x), (ab, wb, jnp.zeros((0,), a.dtype))

    def bwd(res, g):
        ab, wb, like_a = res
        gb = g.astype(ACT_DTYPE)
        da = _mm(gb, wb, tb=True, out_dtype=like_a.dtype, name=name + "_dx")
        dw = _mm(ab, gb, ta=True, out_dtype=F32, name=name + "_dw")
        return g, da, dw

    f.defvjp(fwd, bwd)
    return f(x, a, w)


FFN_ROW_TILE = 512
FFN_COL_TILE = 1408


def _sigmoid(a):
    return 1.0 / (1.0 + jnp.exp(-a))


def swiglu_ffn(x, h, w1, w3, w2, name="ffn"):
    m, d = h.shape
    f_dim = w1.shape[1]
    tm, tn = _tile(m, FFN_ROW_TILE), _tile(f_dim, FFN_COL_TILE)

    def up_body(h_ref, w1_ref, w3_ref, t_ref, a_ref, b_ref):
        hv = h_ref[...]
        a = _dot(hv, w1_ref[...], NN)
        b = _dot(hv, w3_ref[...], NN)
        a_ref[...] = a
        b_ref[...] = b
        t_ref[...] = (a * _sigmoid(a) * b).astype(t_ref.dtype)

    def dt_body(g_ref, w2_ref, a_ref, b_ref, da_ref, db_ref):
        dt = _dot(g_ref[...], w2_ref[...], NT)
        a, b = a_ref[...], b_ref[...]
        sg = _sigmoid(a)
        da_ref[...] = (dt * b * (sg * (1.0 + a * (1.0 - sg)))).astype(da_ref.dtype)
        db_ref[...] = (dt * (a * sg)).astype(db_ref.dtype)

    row = pl.BlockSpec((tm, d), lambda j, i: (i, 0))
    w_up = pl.BlockSpec((d, tn), lambda j, i: (0, j))
    w_dn = pl.BlockSpec((tn, d), lambda j, i: (j, 0))
    tile = pl.BlockSpec((tm, tn), lambda j, i: (i, j))
    grid = (f_dim // tn, m // tm)

    def run_up(hb, w1b, w3b):
        return _pallas(up_body, name=name + "_up", grid=grid, in_specs=[row, w_up, w_up], out_specs=[tile, tile, tile],
                       out_shape=[jax.ShapeDtypeStruct((m, f_dim), ACT_DTYPE), jax.ShapeDtypeStruct((m, f_dim), F32),
                                  jax.ShapeDtypeStruct((m, f_dim), F32)],
                       compiler_params=_cparams("parallel", "parallel"))(hb, w1b, w3b)

    def forward(x, h, w1, w3, w2):
        hb = h.astype(ACT_DTYPE)
        w1b, w3b, w2b = w1.astype(MXU_DTYPE), w3.astype(MXU_DTYPE), w2.astype(MXU_DTYPE)
        t, a, b = run_up(hb, w1b, w3b)
        return _mm(t, w2b, name=name + "_down", res=x), (hb, w1b, w3b, w2b, t, a, b, jnp.zeros((0,), h.dtype))

    @jax.custom_vjp
    def f(x, h, w1, w3, w2):
        return forward(x, h, w1, w3, w2)[0]

    def bwd(res, g):
        hb, w1b, w3b, w2b, t, a, b, like_h = res
        gb = g.astype(ACT_DTYPE)
        da, db = _pallas(dt_body, name=name + "_dt", grid=grid, in_specs=[row, w_dn, tile, tile], out_specs=[tile, tile],
                         out_shape=[jax.ShapeDtypeStruct((m, f_dim), ACT_DTYPE)] * 2,
                         compiler_params=_cparams("parallel", "parallel"))(gb, w2b, a, b)
        dw2 = _mm(t, gb, ta=True, name=name + "_dw2")
        dh = _mm(da, w1b, tb=True, name=name + "_dx1") + _mm(db, w3b, tb=True, name=name + "_dx3")
        dw1 = _mm(hb, da, ta=True, name=name + "_dw1")
        dw3 = _mm(hb, db, ta=True, name=name + "_dw3")
        return g, dh.astype(like_h.dtype), dw1, dw3, dw2

    f.defvjp(lambda *args: forward(*args), bwd)
    return f(x, h, w1, w3, w2)


MERGE_ROW_TILE = 256


def gated_merge_out(x, ya_t, yb, yc, ga, gb, gc, wa, wb, wc, wo, name="merge"):
    s, d = x.shape
    e = yb.shape[1]
    tm = _tile(s, MERGE_ROW_TILE)

    def branches(ya_ref, yb_ref, yc_ref, wa_ref, wb_ref, wc_ref):
        return (_dot(ya_ref[...], wa_ref[...], TN), _dot(yb_ref[...], wb_ref[...], NN), _dot(yc_ref[...], wc_ref[...], NN))

    def fwd_body(x_ref, ya_ref, yb_ref, yc_ref, ga_ref, gb_ref, gc_ref, wa_ref, wb_ref, wc_ref, wo_ref, o_ref, m_ref):
        pa, pb, pc = branches(ya_ref, yb_ref, yc_ref, wa_ref, wb_ref, wc_ref)
        merged = _sigmoid(ga_ref[...]) * pa + _sigmoid(gb_ref[...]) * pb + _sigmoid(gc_ref[...]) * pc
        mb = merged.astype(m_ref.dtype)
        m_ref[...] = mb
        o_ref[...] = x_ref[...] + _dot(mb, wo_ref[...], NN)

    def bwd_body(g_ref, ya_ref, yb_ref, yc_ref, ga_ref, gb_ref, gc_ref, wa_ref, wb_ref, wc_ref, wo_ref,
                 dga_ref, dgb_ref, dgc_ref, dpa_ref, dpb_ref, dpc_ref):
        dm = _dot(g_ref[...], wo_ref[...], NT)
        ps = branches(ya_ref, yb_ref, yc_ref, wa_ref, wb_ref, wc_ref)
        for p_i, gate_ref, dg_ref, dp_ref in zip(ps, (ga_ref, gb_ref, gc_ref), (dga_ref, dgb_ref, dgc_ref),
                                                 (dpa_ref, dpb_ref, dpc_ref)):
            sg = _sigmoid(gate_ref[...])
            dg_ref[...] = dm * p_i * (sg * (1.0 - sg))
            dp_ref[...] = (dm * sg).astype(dp_ref.dtype)

    rows = lambda width: pl.BlockSpec((tm, width), lambda i: (i, 0))
    cols_t = pl.BlockSpec((e, tm), lambda i: (0, i))
    whole = lambda r, c: pl.BlockSpec((r, c), lambda i: (0, 0))
    in_common = [cols_t, rows(e), rows(e), rows(d), rows(d), rows(d), whole(e, d), whole(e, d), whole(e, d), whole(d, d)]

    def forward(x, ya_t, yb, yc, ga, gb, gc, wa, wb, wc, wo):
        cast = lambda t: t.astype(ACT_DTYPE)
        ops = (cast(ya_t), cast(yb), cast(yc), ga, gb, gc, cast(wa), cast(wb), cast(wc), cast(wo))
        out, merged = _pallas(
            fwd_body, name=name + "_fwd", grid=(s // tm,), in_specs=[rows(d)] + in_common, out_specs=[rows(d), rows(d)],
            out_shape=[jax.ShapeDtypeStruct((s, d), F32), jax.ShapeDtypeStruct((s, d), ACT_DTYPE)],
            compiler_params=_cparams("parallel"))(x, *ops)
        like = tuple(jnp.zeros((0,), t.dtype) for t in (ya_t, yb, yc))
        return out, (ops, merged, like)

    @jax.custom_vjp
    def f(*args):
        return forward(*args)[0]

    def bwd(res, g):
        ops, merged, like = res
        ya_b, yb_b, yc_b, ga, gb, gc, wa_b, wb_b, wc_b, wo_b = ops
        gbf = g.astype(ACT_DTYPE)
        gate_ct = jax.ShapeDtypeStruct((s, d), F32)
        branch_ct = jax.ShapeDtypeStruct((s, d), ACT_DTYPE)
        dga, dgb, dgc, dpa, dpb, dpc = _pallas(
            bwd_body, name=name + "_bwd", grid=(s // tm,), in_specs=[rows(d)] + in_common, out_specs=[rows(d)] * 6,
            out_shape=[gate_ct] * 3 + [branch_ct] * 3, compiler_params=_cparams("parallel"))(gbf, *ops)
        dya_t = _mm(wa_b, dpa, tb=True, out_dtype=like[0].dtype, name=name + "_dya")
        dyb = _mm(dpb, wb_b, tb=True, out_dtype=like[1].dtype, name=name + "_dyb")
        dyc = _mm(dpc, wc_b, tb=True, out_dtype=like[2].dtype, name=name + "_dyc")
        dwa = _mm(ya_b, dpa, name=name + "_dwa")
        dwb = _mm(yb_b, dpb, ta=True, name=name + "_dwb")
        dwc = _mm(yc_b, dpc, ta=True, name=name + "_dwc")
        dwo = _mm(merged, gbf, ta=True, name=name + "_dwo")
        return g, dya_t, dyb, dyc, dga, dgb, dgc, dwa, dwb, dwc, dwo

    f.defvjp(lambda *args: forward(*args), bwd)
    return f(x, ya_t, yb, yc, ga, gb, gc, wa, wb, wc, wo)


def _row_tile(rows, width):
    return _tile(rows, max(SUBLANES, (512 * 1024) // width), 16)


def rmsnorm(x, g, out_dtype=F32, name="rms"):
    rows, d = x.shape
    tr = _row_tile(rows, d)
    n_steps = rows // tr

    def fwd_body(x_ref, g_ref, o_ref):
        xv = x_ref[...].astype(F32)
        r = lax.rsqrt(jnp.mean(xv * xv, axis=-1, keepdims=True) + EPS)
        o_ref[...] = (xv * r * g_ref[...]).astype(o_ref.dtype)

    def bwd_body(x_ref, g_ref, dy_ref, dx_ref, dg_ref):
        xv = x_ref[...].astype(F32)
        dy = dy_ref[...].astype(F32)
        r = lax.rsqrt(jnp.mean(xv * xv, axis=-1, keepdims=True) + EPS)
        xh = xv * r
        dxh = dy * g_ref[...]
        dx_ref[...] = (r * (dxh - xh * jnp.mean(dxh * xh, axis=-1, keepdims=True))).astype(dx_ref.dtype)

        @pl.when(pl.program_id(0) == 0)
        def _():
            dg_ref[...] = jnp.zeros_like(dg_ref)

        dg_ref[...] += jnp.sum(dy * xh, axis=0, keepdims=True)

    row_spec = pl.BlockSpec((tr, d), lambda i: (i, 0))
    vec_spec = pl.BlockSpec((1, d), lambda i: (0, 0))

    def run_fwd(x, g):
        return _pallas(fwd_body, name=name + "_fwd", grid=(n_steps,), in_specs=[row_spec, vec_spec], out_specs=row_spec,
                       out_shape=jax.ShapeDtypeStruct((rows, d), out_dtype), compiler_params=_cparams("parallel"))(
            x, g.reshape(1, d).astype(F32))

    @jax.custom_vjp
    def f(x, g):
        return run_fwd(x, g)

    def fwd(x, g):
        return run_fwd(x, g), (x, g)

    def bwd(res, dy):
        x, g = res
        dx, dg = _pallas(
            bwd_body, name=name + "_bwd", grid=(n_steps,), in_specs=[row_spec, vec_spec, row_spec],
            out_specs=[row_spec, vec_spec],
            out_shape=[jax.ShapeDtypeStruct((rows, d), x.dtype), jax.ShapeDtypeStruct((1, d), F32)],
            compiler_params=_cparams("arbitrary"))(x, g.reshape(1, d).astype(F32), dy)
        return dx, dg.reshape(g.shape).astype(g.dtype)

    f.defvjp(fwd, bwd)
    return f(x, g)


def _rowdot(a, b, name):
    h, s, d = a.shape
    ts = _tile(s, 2048)

    def body(a_ref, b_ref, o_ref):
        o_ref[...] = jnp.sum(a_ref[...].astype(F32) * b_ref[...].astype(F32), axis=-1, keepdims=True)

    spec = pl.BlockSpec((None, ts, d), lambda hh, i: (hh, i, 0))
    return _pallas(body, name=name, grid=(h, s // ts), in_specs=[spec, spec],
                   out_specs=pl.BlockSpec((None, ts, 1), lambda hh, i: (hh, i, 0)),
                   out_shape=jax.ShapeDtypeStruct((h, s, 1), F32), compiler_params=_cparams("parallel", "parallel"))(a, b)


def attention(q, k, v, scale, tq, tk, name):
    h, sq, d = q.shape
    sk, dv = k.shape[1], v.shape[2]
    tq, tk = _tile(sq, tq), _tile(sk, tk)
    nq, nk = sq // tq, sk // tk

    def fwd_body(q_ref, k_ref, v_ref, o_ref, lse_ref, m_ref, l_ref, acc_ref):
        j = pl.program_id(2)

        @pl.when(j == 0)
        def _():
            m_ref[...] = jnp.full_like(m_ref, NEG)
            l_ref[...] = jnp.zeros_like(l_ref)
            acc_ref[...] = jnp.zeros_like(acc_ref)

        s = _dot(q_ref[...], k_ref[...], NT) * scale
        m_prev = m_ref[...]
        m_new = jnp.maximum(m_prev, jnp.max(s, axis=-1, keepdims=True))
        alpha = jnp.exp(m_prev - m_new)
        p = jnp.exp(s - m_new)
        l_ref[...] = alpha * l_ref[...] + jnp.sum(p, axis=-1, keepdims=True)
        acc_ref[...] = alpha * acc_ref[...] + _dot(p, v_ref[...], NN)
        m_ref[...] = m_new

        @pl.when(j == nk - 1)
        def _():
            o_ref[...] = (acc_ref[...] / l_ref[...]).astype(o_ref.dtype)
            lse_ref[...] = m_ref[...] + jnp.log(l_ref[...])

    def dq_body(q_ref, k_ref, v_ref, do_ref, lse_ref, dl_ref, dq_ref, acc_ref):
        j = pl.program_id(2)

        @pl.when(j == 0)
        def _():
            acc_ref[...] = jnp.zeros_like(acc_ref)

        s = _dot(q_ref[...], k_ref[...], NT) * scale
        p = jnp.exp(s - lse_ref[...])
        dp = _dot(do_ref[...], v_ref[...], NT)
        ds = p * (dp - dl_ref[...]) * scale
        acc_ref[...] += _dot(ds, k_ref[...], NN)

        @pl.when(j == nk - 1)
        def _():
            dq_ref[...] = acc_ref[...].astype(dq_ref.dtype)

    def dkv_body(q_ref, k_ref, v_ref, do_ref, lse_ref, dl_ref, dk_ref, dv_ref, dk_acc, dv_acc):
        i = pl.program_id(2)

        @pl.when(i == 0)
        def _():
            dk_acc[...] = jnp.zeros_like(dk_acc)
            dv_acc[...] = jnp.zeros_like(dv_acc)

        s = _dot(q_ref[...], k_ref[...], NT) * scale
        p = jnp.exp(s - lse_ref[...])
        dv_acc[...] += _dot(p, do_ref[...], TN)
        dp = _dot(do_ref[...], v_ref[...], NT)
        ds = p * (dp - dl_ref[...]) * scale
        dk_acc[...] += _dot(ds, q_ref[...], TN)

        @pl.when(i == nq - 1)
        def _():
            dk_ref[...] = dk_acc[...].astype(dk_ref.dtype)
            dv_ref[...] = dv_acc[...].astype(dv_ref.dtype)

    def q_spec(width):
        return pl.BlockSpec((None, tq, width), lambda hh, i, j: (hh, i, 0))

    def k_spec(width):
        return pl.BlockSpec((None, tk, width), lambda hh, i, j: (hh, j, 0))

    def run_fwd(q, k, v):
        return _pallas(
            fwd_body, name=name + "_fwd", grid=(h, nq, nk), in_specs=[q_spec(d), k_spec(d), k_spec(dv)],
            out_specs=[q_spec(dv), q_spec(1)],
            out_shape=[jax.ShapeDtypeStruct((h, sq, dv), q.dtype), jax.ShapeDtypeStruct((h, sq, 1), F32)],
            scratch_shapes=[pltpu.VMEM((tq, 1), F32), pltpu.VMEM((tq, 1), F32), pltpu.VMEM((tq, dv), F32)],
            compiler_params=_cparams("parallel", "parallel", "arbitrary"))(q, k, v)

    @jax.custom_vjp
    def f(q, k, v):
        return run_fwd(q, k, v)[0]

    def fwd(q, k, v):
        o, lse = run_fwd(q, k, v)
        return o, (q, k, v, o, lse)

    def bwd(res, do):
        q, k, v, o, lse = res
        delta = _rowdot(o, do, name + "_delta")
        dq = _pallas(
            dq_body, name=name + "_dq", grid=(h, nq, nk),
            in_specs=[q_spec(d), k_spec(d), k_spec(dv), q_spec(dv), q_spec(1), q_spec(1)], out_specs=q_spec(d),
            out_shape=jax.ShapeDtypeStruct((h, sq, d), q.dtype), scratch_shapes=[pltpu.VMEM((tq, d), F32)],
            compiler_params=_cparams("parallel", "parallel", "arbitrary"))(q, k, v, do, lse, delta)

        def qs(width):
            return pl.BlockSpec((None, tq, width), lambda hh, j, i: (hh, i, 0))

        def ks(width):
            return pl.BlockSpec((None, tk, width), lambda hh, j, i: (hh, j, 0))

        dk, dv_ = _pallas(
            dkv_body, name=name + "_dkv", grid=(h, nk, nq),
            in_specs=[qs(d), ks(d), ks(dv), qs(dv), qs(1), qs(1)], out_specs=[ks(d), ks(dv)],
            out_shape=[jax.ShapeDtypeStruct((h, sk, d), k.dtype), jax.ShapeDtypeStruct((h, sk, dv), v.dtype)],
            scratch_shapes=[pltpu.VMEM((tk, d), F32), pltpu.VMEM((tk, dv), F32)],
            compiler_params=_cparams("parallel", "parallel", "arbitrary"))(q, k, v, do, lse, delta)
        return dq, dk, dv_

    f.defvjp(fwd, bwd)
    return f(q, k, v)


LOG2E = 1.4426950408889634


def linear_t(a_t, w, out_dtype=F32, name="lin_t"):
    @jax.custom_vjp
    def f(a_t, w):
        return _mm(a_t, w.astype(MXU_DTYPE), ta=True, out_dtype=out_dtype, name=name + "_fwd")

    def fwd(a_t, w):
        wb = w.astype(MXU_DTYPE)
        return _mm(a_t, wb, ta=True, out_dtype=out_dtype, name=name + "_fwd"), (a_t, wb)

    def bwd(res, g):
        a_t, wb = res
        gb = g.astype(ACT_DTYPE)
        da_t = _mm(wb, gb, tb=True, out_dtype=a_t.dtype, name=name + "_dx")
        dw = _mm(a_t, gb, out_dtype=F32, name=name + "_dw")
        return da_t, dw

    f.defvjp(fwd, bwd)
    return f(a_t, w)


def mla_attention(q, k, v, scale, name="mla"):
    s, h, d = q.shape
    dv = v.shape[2]
    tq, tk = _tile(s, MLA_BWD_TQ), _tile(s, MLA_TK)
    nq, nk = s // tq, s // tk
    tqf = _tile(s, MLA_FWD_TQ)
    ones_rows = 16
    c = scale * LOG2E

    def fwd_body(qt_ref, k_ref, vt_ref, ot_ref, lse_ref, m_ref, acc_ref):
        j = pl.program_id(2)

        @pl.when(j == 0)
        def _():
            m_ref[...] = jnp.full_like(m_ref, NEG)
            acc_ref[...] = jnp.zeros_like(acc_ref)

        st = _dot(k_ref[...], qt_ref[...], NN)
        m_prev = m_ref[...]
        m_new = jnp.maximum(m_prev, jnp.max(st, axis=0, keepdims=True) * c)
        pt = jnp.exp2(st * c - m_new)
        acc_ref[...] = jnp.exp2(m_prev - m_new) * acc_ref[...] + _dot(vt_ref[...], pt, NN)
        m_ref[...] = m_new

        @pl.when(j == nk - 1)
        def _():
            l = acc_ref[dv:dv + 1, :]
            ot_ref[...] = (acc_ref[:dv, :] / l).astype(ot_ref.dtype)
            lse_ref[...] = m_ref[...] + jnp.log2(l)

    def delta_body(ot_ref, dot_ref, o_ref):
        o_ref[...] = jnp.sum(ot_ref[...].astype(F32) * dot_ref[...].astype(F32), axis=0, keepdims=True)

    def bwd_body(qt_ref, k_ref, kt_ref, v_ref, dot_ref, lse_ref, dl_ref, dqt_ref, dk_hbm, dv_hbm, dq_acc, dk_acc, dv_acc):
        hh, i, j = pl.program_id(0), pl.program_id(1), pl.program_id(2)

        @pl.when(j == 0)
        def _():
            dq_acc[...] = jnp.zeros_like(dq_acc)

        @pl.when(i == 0)
        def _():
            dk_acc[j] = jnp.zeros((d, tk), F32)
            dv_acc[j] = jnp.zeros((dv, tk), F32)

        qt, dot_ = qt_ref[...], dot_ref[...]
        pt = jnp.exp2(_dot(k_ref[...], qt, NN) * c - lse_ref[...])
        dst = (pt * (_dot(v_ref[...], dot_, NN) - dl_ref[...])).astype(MXU_DTYPE)
        dv_acc[j] += _dot(dot_, pt, NT)
        dk_acc[j] += _dot(qt, dst, NT)
        dq_acc[...] += _dot(kt_ref[...], dst, NN)

        @pl.when(j == nk - 1)
        def _():
            dqt_ref[...] = dq_acc[...] * scale

        @pl.when(i == nq - 1)
        def _():
            dk_acc[j] = dk_acc[j] * scale
            pltpu.sync_copy(dk_acc.at[j], dk_hbm.at[hh, j])
            pltpu.sync_copy(dv_acc.at[j], dv_hbm.at[hh, j])

    def qt_spec(width):
        return pl.BlockSpec((None, width, tq), lambda hh, i, j: (hh, 0, i))

    def kt_spec(width):
        return pl.BlockSpec((None, width, tk), lambda hh, i, j: (hh, 0, j))

    def k_spec(width):
        return pl.BlockSpec((None, tk, width), lambda hh, i, j: (hh, j, 0))

    def layouts(q, k, v):
        cast = lambda t: t.astype(ACT_DTYPE)
        return (cast(jnp.transpose(q, (1, 2, 0))), cast(jnp.transpose(k, (1, 0, 2))), cast(jnp.transpose(k, (1, 2, 0))),
                cast(jnp.transpose(v, (1, 0, 2))), cast(jnp.transpose(v, (1, 2, 0))))

    def run_fwd(qt, kh, vt):
        vt_ones = jnp.concatenate([vt, jnp.ones((h, ones_rows, s), vt.dtype)], axis=1)

        def qf_spec(width):
            return pl.BlockSpec((None, width, tqf), lambda hh, i, j: (hh, 0, i))

        return _pallas(
            fwd_body, name=name + "_fwd", grid=(h, s // tqf, nk), in_specs=[qf_spec(d), k_spec(d), kt_spec(dv + ones_rows)],
            out_specs=[qf_spec(dv), qf_spec(1)],
            out_shape=[jax.ShapeDtypeStruct((h, dv, s), ACT_DTYPE), jax.ShapeDtypeStruct((h, 1, s), F32)],
            scratch_shapes=[pltpu.VMEM((1, tqf), F32), pltpu.VMEM((dv + ones_rows, tqf), F32)],
            compiler_params=_cparams("parallel", "parallel", "arbitrary"))(qt, kh, vt_ones)

    @jax.custom_vjp
    def f(q, k, v):
        qt, kh, _, _, vt = layouts(q, k, v)
        return run_fwd(qt, kh, vt)[0].reshape(h * dv, s)

    def fwd(q, k, v):
        qt, kh, kt, vh, vt = layouts(q, k, v)
        ot, lse = run_fwd(qt, kh, vt)
        return ot.reshape(h * dv, s), (qt, kh, kt, vh, ot, lse)

    def bwd(res, dy):
        qt, kh, kt, vh, ot, lse = res
        dot_ = dy.reshape(h, dv, s)
        ts = _tile(s, 2048)
        col = pl.BlockSpec((None, dv, ts), lambda hh, i: (hh, 0, i))
        delta = _pallas(delta_body, name=name + "_delta", grid=(h, s // ts), in_specs=[col, col],
                        out_specs=pl.BlockSpec((None, 1, ts), lambda hh, i: (hh, 0, i)),
                        out_shape=jax.ShapeDtypeStruct((h, 1, s), F32), compiler_params=_cparams("parallel", "parallel"))(ot, dot_)
        any_spec = pl.BlockSpec(memory_space=pl.ANY)
        dqt, dkt, dvt = _pallas(
            bwd_body, name=name + "_bwd", grid=(h, nq, nk),
            in_specs=[qt_spec(d), k_spec(d), kt_spec(d), k_spec(dv), qt_spec(dv), qt_spec(1), qt_spec(1)],
            out_specs=[qt_spec(d), any_spec, any_spec],
            out_shape=[jax.ShapeDtypeStruct((h, d, s), F32), jax.ShapeDtypeStruct((h, nk, d, tk), F32),
                       jax.ShapeDtypeStruct((h, nk, dv, tk), F32)],
            scratch_shapes=[pltpu.VMEM((d, tq), F32), pltpu.VMEM((nk, d, tk), F32), pltpu.VMEM((nk, dv, tk), F32)],
            compiler_params=_cparams("parallel", "arbitrary", "arbitrary"))(qt, kh, kt, vh, dot_, lse, delta)
        to_tokens = lambda t: jnp.transpose(t, (1, 3, 0, 2)).reshape(s, h, t.shape[2])
        return jnp.transpose(dqt, (2, 0, 1)), to_tokens(dkt), to_tokens(dvt)

    f.defvjp(fwd, bwd)
    return f(q, k, v)


WATTN_TQ = 2 * C_BLOCK
WATTN_KW = WATTN_TQ + 2 * C_BLOCK


def window_attention(q, k, v, bias, sink, name="wattn"):
    hq, s, dh = q.shape
    g = hq // C_KV_HEADS
    tq, kw, half = WATTN_TQ, WATTN_KW, WATTN_KW // 2
    nt = s // tq
    scale = dh ** -0.5
    sink_b = jnp.broadcast_to(sink.astype(F32).reshape(hq, 1, 1), (hq, 1, LANES))
    neg = jnp.full((hq, C_BLOCK, C_BLOCK), NEG, F32)
    tile = jnp.concatenate(
        [jnp.concatenate([bias[:, cb - rb] if 0 <= cb - rb <= 2 else neg for cb in range(kw // C_BLOCK)], axis=2)
         for rb in range(tq // C_BLOCK)], axis=1)

    def key_bias(i):
        pos = lax.broadcasted_iota(jnp.int32, (1, kw), 1) + i * tq - C_BLOCK
        return jnp.where(jnp.logical_and(pos >= 0, pos < s), 0.0, NEG)

    def both(a_ref, b_ref):
        return jnp.concatenate([a_ref[...], b_ref[...]], axis=0)

    def fwd_body(q_ref, ka_ref, kb_ref, va_ref, vb_ref, b_ref, sk_ref, o_ref, lse_ref):
        kb_ = key_bias(pl.program_id(1))
        k_all, v_all = both(ka_ref, kb_ref), both(va_ref, vb_ref)
        for hh in range(g):
            sc = _dot(q_ref[hh], k_all, NT) * scale + b_ref[hh] + kb_
            snk = sk_ref[hh][:, :1]
            m = jnp.maximum(jnp.max(sc, axis=-1, keepdims=True), snk)
            p = jnp.exp(sc - m)
            l = jnp.sum(p, axis=-1, keepdims=True) + jnp.exp(snk - m)
            o_ref[hh] = (_dot(p, v_all, NN) / l).astype(o_ref.dtype)
            lse_ref[hh] = m + jnp.log(l)

    def bwd_body(q_ref, ka_ref, kb_ref, va_ref, vb_ref, b_ref, sk_ref, do_ref, lse_ref, dl_ref,
                 dq_ref, db_ref, dsink_ref, dk_hbm, dv_hbm, dk_acc, dv_acc):
        kv, i = pl.program_id(0), pl.program_id(1)

        @pl.when(i == 0)
        def _():
            dk_acc[...] = jnp.zeros_like(dk_acc)
            dv_acc[...] = jnp.zeros_like(dv_acc)
            db_ref[...] = jnp.zeros_like(db_ref)
            dsink_ref[...] = jnp.zeros_like(dsink_ref)

        kb_ = key_bias(i)
        k_all, v_all = both(ka_ref, kb_ref), both(va_ref, vb_ref)
        dk_t = jnp.zeros((kw, dh), F32)
        dv_t = jnp.zeros((kw, dh), F32)
        for hh in range(g):
            lse, dl, do = lse_ref[hh], dl_ref[hh], do_ref[hh]
            p = jnp.exp(_dot(q_ref[hh], k_all, NT) * scale + b_ref[hh] + kb_ - lse)
            ds = p * (_dot(do, v_all, NT) - dl)
            db_ref[hh] += ds
            total = jnp.broadcast_to(-jnp.sum(jnp.exp(sk_ref[hh][:, :1] - lse) * dl, axis=0, keepdims=True), (1, LANES))
            dsink_ref[hh] += jnp.where(lax.broadcasted_iota(jnp.int32, (1, LANES), 1) == 0, total, 0.0)
            dsb = (ds * scale).astype(MXU_DTYPE)
            dq_ref[hh] = _dot(dsb, k_all, NN).astype(dq_ref.dtype)
            dk_t += _dot(dsb, q_ref[hh], TN)
            dv_t += _dot(p, do, TN)
        rows = pl.ds(pl.multiple_of(i * tq, tq), kw)
        dk_acc[rows, :] += dk_t
        dv_acc[rows, :] += dv_t

        @pl.when(i == nt - 1)
        def _():
            pltpu.sync_copy(dk_acc, dk_hbm.at[kv])
            pltpu.sync_copy(dv_acc, dv_hbm.at[kv])

    def q_spec(width):
        return pl.BlockSpec((g, tq, width), lambda kv, i: (kv, i, 0))

    ka_spec = pl.BlockSpec((None, half, dh), lambda kv, i: (kv, i, 0))
    kb_spec = pl.BlockSpec((None, half, dh), lambda kv, i: (kv, i + 1, 0))
    b_spec = pl.BlockSpec((g, tq, kw), lambda kv, i: (kv, 0, 0))
    sk_spec = pl.BlockSpec((g, 1, LANES), lambda kv, i: (kv, 0, 0))

    def padded(t):
        return jnp.pad(t, ((0, 0), (C_BLOCK, C_BLOCK), (0, 0)))

    def run_fwd(q, kp, vp, tile, sink_b):
        return _pallas(
            fwd_body, name=name + "_fwd", grid=(C_KV_HEADS, nt),
            in_specs=[q_spec(dh), ka_spec, kb_spec, ka_spec, kb_spec, b_spec, sk_spec], out_specs=[q_spec(dh), q_spec(1)],
            out_shape=[jax.ShapeDtypeStruct((hq, s, dh), q.dtype), jax.ShapeDtypeStruct((hq, s, 1), F32)],
            compiler_params=_cparams("parallel", "parallel"))(q, kp, kp, vp, vp, tile, sink_b)

    @jax.custom_vjp
    def f(q, k, v, tile, sink_b):
        return run_fwd(q, padded(k), padded(v), tile, sink_b)[0]

    def fwd(q, k, v, tile, sink_b):
        kp, vp = padded(k), padded(v)
        o, lse = run_fwd(q, kp, vp, tile, sink_b)
        return o, (q, kp, vp, tile, sink_b, o, lse)

    def bwd(res, do):
        q, kp, vp, tile, sink_b, o, lse = res
        delta = _rowdot(o, do, name + "_delta")
        any_spec = pl.BlockSpec(memory_space=pl.ANY)
        acc = jax.ShapeDtypeStruct((C_KV_HEADS, s + 2 * C_BLOCK, dh), F32)
        dq, dtile, dsink, dkp, dvp = _pallas(
            bwd_body, name=name + "_bwd", grid=(C_KV_HEADS, nt),
            in_specs=[q_spec(dh), ka_spec, kb_spec, ka_spec, kb_spec, b_spec, sk_spec, q_spec(dh), q_spec(1), q_spec(1)],
            out_specs=[q_spec(dh), b_spec, sk_spec, any_spec, any_spec],
            out_shape=[jax.ShapeDtypeStruct((hq, s, dh), q.dtype), jax.ShapeDtypeStruct((hq, tq, kw), F32),
                       jax.ShapeDtypeStruct((hq, 1, LANES), F32), acc, acc],
            scratch_shapes=[pltpu.VMEM((s + 2 * C_BLOCK, dh), F32), pltpu.VMEM((s + 2 * C_BLOCK, dh), F32)],
            compiler_params=_cparams("parallel", "arbitrary"))(q, kp, kp, vp, vp, tile, sink_b, do, lse, delta)
        unpad = lambda t: t[:, C_BLOCK:-C_BLOCK].astype(kp.dtype)
        return dq, unpad(dkp), unpad(dvp), dtile, dsink

    f.defvjp(fwd, bwd)
    return f(q, k, v, tile, sink_b)


HG_PREP_ROWS = 256
HG_GROUP = 8
HG_INTRA_BLOCK = 256
HG_INTER_CHUNKS = 16
MLA_FWD_TQ, MLA_BWD_TQ, MLA_TK = 2048, 1024, 1024
CROSS_TQ = 1024


def _hdot(a, b, dims):
    return lax.dot_general(a, b, (dims, ((), ())), precision=lax.Precision.HIGHEST, preferred_element_type=F32)


def hgrn_prep(q, z, lb, reverse, name):
    n_hp, s, tc = q.shape
    tb = _tile(s, HG_PREP_ROWS)
    ncb = tb // B_CHUNK

    def chunk_matrices():
        r = lax.broadcasted_iota(jnp.int32, (tb, tb), 0)
        cc = lax.broadcasted_iota(jnp.int32, (tb, tb), 1)
        same = r // B_CHUNK == cc // B_CHUNK
        tri = (cc >= r) if reverse else (cc <= r)
        cum = jnp.where(jnp.logical_and(same, tri), 1.0, 0.0).astype(F32)
        every = jnp.where(same, 1.0, 0.0).astype(F32)
        pr = lax.broadcasted_iota(jnp.int32, (ncb, tb), 0)
        pc = lax.broadcasted_iota(jnp.int32, (ncb, tb), 1)
        per_chunk = jnp.where(pc // B_CHUNK == pr, 1.0, 0.0).astype(F32)
        return cum, every, per_chunk

    def gates(zv, lbv):
        e = jnp.exp(-jnp.abs(zv))
        big, small = 1.0 / (1.0 + e), e / (1.0 + e)
        sig = jnp.where(zv >= 0, big, small)
        nsig = jnp.where(zv >= 0, small, big)
        f = lbv + (1.0 - lbv) * sig
        return sig, nsig, f, jnp.log(jnp.maximum(f, TINY)), (1.0 - lbv) * nsig

    def fwd_body(q_ref, z_ref, lb_ref, qd_ref, ki_ref, ke_ref, dec_ref):
        cum, every, per_chunk = chunk_matrices()
        _, _, _, lf, key = gates(z_ref[...], lb_ref[...])
        b = _hdot(cum, lf, NN)
        tot = _hdot(every, lf, NN)
        qd_ref[...] = q_ref[...] * jnp.exp(b)
        ki_ref[...] = key * jnp.exp(-b)
        ke_ref[...] = key * jnp.exp(tot - b)
        dec_ref[...] = jnp.exp(_hdot(per_chunk, lf, NN))

    def bwd_body(q_ref, z_ref, lb_ref, dqd_ref, dki_ref, dke_ref, ddec_ref, dq_ref, dz_ref, dlb_ref):
        cum, every, per_chunk = chunk_matrices()
        lbv = lb_ref[...]
        sig, nsig, f, lf, key = gates(z_ref[...], lbv)
        b = _hdot(cum, lf, NN)
        tot = _hdot(every, lf, NN)
        e_b, e_nb, e_tb = jnp.exp(b), jnp.exp(-b), jnp.exp(tot - b)
        dqd, dki, dke = dqd_ref[...], dki_ref[...], dke_ref[...]
        dq_ref[...] = dqd * e_b
        dkey = dki * e_nb + dke * e_tb
        t_end = dke * key * e_tb
        db = dqd * q_ref[...] * e_b - dki * key * e_nb - t_end
        dtot = ddec_ref[...] * jnp.exp(_hdot(per_chunk, lf, NN)) + _hdot(per_chunk, t_end, NN)
        dlf = _hdot(cum, db, TN) + _hdot(per_chunk, dtot, TN)
        df = jnp.where(f > TINY, dlf / f, 0.0)
        one_m_lb = 1.0 - lbv
        dz_ref[...] = (df - dkey) * one_m_lb * sig * nsig
        dlb_part = jnp.sum(df * nsig - dkey * nsig, axis=0, keepdims=True)

        @pl.when(pl.program_id(1) == 0)
        def _():
            dlb_ref[...] = jnp.zeros_like(dlb_ref)

        dlb_ref[...] += dlb_part

    tok = pl.BlockSpec((None, tb, tc), lambda j, i: (j, i, 0))
    vec = pl.BlockSpec((None, 1, tc), lambda j, i: (j, 0, 0))
    chk = pl.BlockSpec((None, ncb, tc), lambda j, i: (j, i, 0))
    grid = (n_hp, s // tb)
    tok_shape = jax.ShapeDtypeStruct((n_hp, s, tc), F32)
    chk_shape = jax.ShapeDtypeStruct((n_hp, s // B_CHUNK, tc), F32)

    def run_fwd(q, z, lb):
        return _pallas(fwd_body, name=name + "_fwd", grid=grid, in_specs=[tok, tok, vec], out_specs=[tok, tok, tok, chk],
                       out_shape=[tok_shape, tok_shape, tok_shape, chk_shape],
                       compiler_params=_cparams("parallel", "parallel"))(q, z, lb)

    @jax.custom_vjp
    def f(q, z, lb):
        return tuple(run_fwd(q, z, lb))

    def fwd(q, z, lb):
        return tuple(run_fwd(q, z, lb)), (q, z, lb)

    def bwd(res, cts):
        q, z, lb = res
        dq, dz, dlb = _pallas(
            bwd_body, name=name + "_bwd", grid=grid, in_specs=[tok, tok, vec, tok, tok, tok, chk], out_specs=[tok, tok, vec],
            out_shape=[tok_shape, tok_shape, jax.ShapeDtypeStruct((n_hp, 1, tc), F32)],
            compiler_params=_cparams("parallel", "arbitrary"))(q, z, lb, *cts)
        return dq, dz, dlb

    f.defvjp(fwd, bwd)
    return f(q, z, lb)


def _pair_cols(ref, hh, width):
    return ref[:, hh * width:(hh + 1) * width]


def hgrn_intra(qd, ki, v, reverse, name):
    s = qd.shape[1]
    tb = _tile(s, HG_INTRA_BLOCK)
    wk, wv = HG_GROUP * B_DK, HG_GROUP * B_DV

    def mask():
        r = lax.broadcasted_iota(jnp.int32, (tb, tb), 0)
        c = lax.broadcasted_iota(jnp.int32, (tb, tb), 1)
        return jnp.logical_and(r // B_CHUNK == c // B_CHUNK, (c >= r) if reverse else (c <= r))

    def fwd_body(q_ref, k_ref, v_ref, o_ref):
        msk = mask()
        for hh in range(HG_GROUP):
            sc = jnp.where(msk, _dot(_pair_cols(q_ref, hh, B_DK), _pair_cols(k_ref, hh, B_DK), NT), 0.0)
            o_ref[:, hh * B_DV:(hh + 1) * B_DV] = _dot(sc, _pair_cols(v_ref, hh, B_DV), NN)

    def bwd_body(q_ref, k_ref, v_ref, do_ref, dq_ref, dk_ref, dv_ref):
        msk = mask()
        for hh in range(HG_GROUP):
            q, k = _pair_cols(q_ref, hh, B_DK), _pair_cols(k_ref, hh, B_DK)
            vv, do = _pair_cols(v_ref, hh, B_DV), _pair_cols(do_ref, hh, B_DV)
            sc = jnp.where(msk, _dot(q, k, NT), 0.0)
            ds = jnp.where(msk, _dot(do, vv, NT), 0.0)
            dq_ref[:, hh * B_DK:(hh + 1) * B_DK] = _dot(ds, k, NN)
            dk_ref[:, hh * B_DK:(hh + 1) * B_DK] = _dot(ds, q, TN)
            dv_ref[:, hh * B_DV:(hh + 1) * B_DV] = _dot(sc, do, TN)

    ks = pl.BlockSpec((None, tb, wk), lambda hp, i: (hp, i, 0))
    vs = pl.BlockSpec((None, tb, wv), lambda hp, i: (hp, i, 0))
    grid = (B_HEADS // HG_GROUP, s // tb)

    def run_fwd(qd, ki, v):
        return _pallas(fwd_body, name=name + "_fwd", grid=grid, in_specs=[ks, ks, vs], out_specs=vs,
                       out_shape=jax.ShapeDtypeStruct(v.shape, F32), compiler_params=_cparams("parallel", "parallel"))(qd, ki, v)

    @jax.custom_vjp
    def f(qd, ki, v):
        return run_fwd(qd, ki, v)

    def fwd(qd, ki, v):
        return run_fwd(qd, ki, v), (qd, ki, v)

    def bwd(res, do):
        qd, ki, v = res
        return tuple(_pallas(
            bwd_body, name=name + "_bwd", grid=grid, in_specs=[ks, ks, vs, vs], out_specs=[ks, ks, vs],
            out_shape=[jax.ShapeDtypeStruct(qd.shape, F32), jax.ShapeDtypeStruct(ki.shape, F32),
                       jax.ShapeDtypeStruct(v.shape, F32)],
            compiler_params=_cparams("parallel", "parallel"))(qd, ki, v, do))

    f.defvjp(fwd, bwd)
    return f(qd, ki, v)


def hgrn_inter(qd, ke, v, dec, reverse, name):
    s = qd.shape[1]
    nc = s // B_CHUNK
    cpb = HG_INTER_CHUNKS if nc % HG_INTER_CHUNKS == 0 else nc
    tb = cpb * B_CHUNK
    nblk = nc // cpb
    wk, wv = HG_GROUP * B_DK, HG_GROUP * B_DV
    n_hp = B_HEADS // HG_GROUP

    def rows(c):
        return pl.ds(c * B_CHUNK, B_CHUNK)

    def kcols(hh):
        return slice(hh * B_DK, (hh + 1) * B_DK)

    def vcols(hh):
        return slice(hh * B_DV, (hh + 1) * B_DV)

    def order(flip):
        return reversed(range(cpb)) if flip else range(cpb)

    def fwd_body(q_ref, k_ref, v_ref, dec_ref, o_ref, st_ref, state):
        @pl.when(pl.program_id(1) == 0)
        def _():
            state[...] = jnp.zeros_like(state)

        for c in order(reverse):
            for hh in range(HG_GROUP):
                st = state[hh]
                st_ref[c, hh] = st
                o_ref[rows(c), vcols(hh)] = _dot(q_ref[rows(c), kcols(hh)], st, NT)
                state[hh] = st * dec_ref[pl.ds(c, 1), kcols(hh)] + _dot(v_ref[rows(c), vcols(hh)], k_ref[rows(c), kcols(hh)], TN)

    def bwd_body(q_ref, k_ref, v_ref, dec_ref, st_ref, do_ref, dq_ref, dk_ref, dv_ref, ddec_ref, dstate):
        @pl.when(pl.program_id(1) == 0)
        def _():
            dstate[...] = jnp.zeros_like(dstate)

        for c in order(not reverse):
            for hh in range(HG_GROUP):
                dst = dstate[hh]
                st = st_ref[c, hh]
                do_c = do_ref[rows(c), vcols(hh)]
                dk_ref[rows(c), kcols(hh)] = _dot(v_ref[rows(c), vcols(hh)], dst, NN)
                dv_ref[rows(c), vcols(hh)] = _dot(k_ref[rows(c), kcols(hh)], dst, NT)
                ddec_ref[pl.ds(c, 1), kcols(hh)] = jnp.sum(dst * st, axis=0, keepdims=True)
                dq_ref[rows(c), kcols(hh)] = _dot(do_c, st, NN)
                dstate[hh] = dst * dec_ref[pl.ds(c, 1), kcols(hh)] + _dot(do_c, q_ref[rows(c), kcols(hh)], TN)

    def specs(flip):
        blk = (lambda i: nblk - 1 - i) if flip else (lambda i: i)
        tok_k = pl.BlockSpec((None, tb, wk), lambda hp, i: (hp, blk(i), 0))
        tok_v = pl.BlockSpec((None, tb, wv), lambda hp, i: (hp, blk(i), 0))
        chk = pl.BlockSpec((None, cpb, wk), lambda hp, i: (hp, blk(i), 0))
        sts = pl.BlockSpec((None, cpb, HG_GROUP, B_DV, B_DK), lambda hp, i: (hp, blk(i), 0, 0, 0))
        return tok_k, tok_v, chk, sts

    scratch = [pltpu.VMEM((HG_GROUP, B_DV, B_DK), F32)]

    def run_fwd(qd, ke, v, dec):
        tok_k, tok_v, chk, sts = specs(reverse)
        return _pallas(
            fwd_body, name=name + "_fwd", grid=(n_hp, nblk), in_specs=[tok_k, tok_k, tok_v, chk], out_specs=[tok_v, sts],
            out_shape=[jax.ShapeDtypeStruct(v.shape, F32), jax.ShapeDtypeStruct((n_hp, nc, HG_GROUP, B_DV, B_DK), F32)],
            scratch_shapes=scratch, compiler_params=_cparams("parallel", "arbitrary"))(qd, ke, v, dec)

    @jax.custom_vjp
    def f(qd, ke, v, dec):
        return run_fwd(qd, ke, v, dec)[0]

    def fwd(qd, ke, v, dec):
        o, st = run_fwd(qd, ke, v, dec)
        return o, (qd, ke, v, dec, st)

    def bwd(res, do):
        qd, ke, v, dec, st = res
        tok_k, tok_v, chk, sts = specs(not reverse)
        return tuple(_pallas(
            bwd_body, name=name + "_bwd", grid=(n_hp, nblk), in_specs=[tok_k, tok_k, tok_v, chk, sts, tok_v],
            out_specs=[tok_k, tok_k, tok_v, chk],
            out_shape=[jax.ShapeDtypeStruct(qd.shape, F32), jax.ShapeDtypeStruct(ke.shape, F32),
                       jax.ShapeDtypeStruct(v.shape, F32), jax.ShapeDtypeStruct(dec.shape, F32)],
            scratch_shapes=scratch, compiler_params=_cparams("parallel", "arbitrary"))(qd, ke, v, dec, st, do))

    f.defvjp(fwd, bwd)
    return f(qd, ke, v, dec)


def loss_head(y, target, name="loss"):
    s, d = y.shape
    tr = _row_tile(s, d)

    def body(y_ref, t_ref, o_ref):
        @pl.when(pl.program_id(0) == 0)
        def _():
            o_ref[...] = jnp.zeros_like(o_ref)

        e = y_ref[...] - t_ref[...]
        part = jnp.sum(jnp.sum(e * e, axis=-1, keepdims=True), axis=0, keepdims=True) * (0.5 / d)
        o_ref[...] += jnp.broadcast_to(part, o_ref.shape)

    spec = pl.BlockSpec((tr, d), lambda i: (i, 0))

    def run(y, t):
        out = _pallas(body, name=name, grid=(s // tr,), in_specs=[spec, spec],
                      out_specs=pl.BlockSpec((SUBLANES, LANES), lambda i: (0, 0)),
                      out_shape=jax.ShapeDtypeStruct((SUBLANES, LANES), F32), compiler_params=_cparams("arbitrary"))(y, t)
        return out[0, 0]

    @jax.custom_vjp
    def f(y, t):
        return run(y, t)

    def fwd(y, t):
        return run(y, t), (y, t)

    def bwd(res, g):
        y, t = res
        dy = g * (y - t) * (1.0 / d)
        return dy, -dy

    f.defvjp(fwd, bwd)
    return f(y, target)


def _mesh_pos():
    return lax.axis_index("x"), lax.axis_index("y"), lax.axis_index("c")


def all_gather_shards(shards):
    n = len(shards)

    def body(*refs):
        ins, outs = refs[:n], refs[n:2 * n]
        send_sems, recv_sems, local_sems = refs[2 * n:]
        x, y, c = _mesh_pos()
        me, sibling = (x, y, c), (x, y, 1 - c)
        chips = [(1 - x, y), (x, 1 - y), (1 - x, 1 - y)]

        def slot(t, px, py, pc):
            return outs[t].at[4 * px + 2 * py + pc]

        def copy(t, k, block, to, src=None):
            return pltpu.make_async_remote_copy(
                src_ref=slot(t, *block) if src is None else src, dst_ref=slot(t, *block), send_sem=send_sems.at[t, k],
                recv_sem=recv_sems.at[t, k], device_id=to, device_id_type=pl.DeviceIdType.MESH)

        mine = [pltpu.make_async_copy(ins[t], slot(t, *me), local_sems.at[t]) for t in range(n)]
        for cp in mine:
            cp.start()
        first = []
        for t in range(n):
            first.append(copy(t, 0, me, sibling, src=ins[t]))
            first += [copy(t, 1 + j, me, (*chip, c), src=ins[t]) for j, chip in enumerate(chips)]
        for cp in first:
            cp.start()
        passed = []
        for j, chip in enumerate(chips):
            for t in range(n):
                copy(t, 1 + j, (*chip, c), me).wait_recv()
                cp = copy(t, 4 + j, (*chip, c), sibling)
                cp.start()
                passed.append(cp)
        for t in range(n):
            copy(t, 0, sibling, me).wait_recv()
            for j, chip in enumerate(chips):
                copy(t, 4 + j, (*chip, 1 - c), me).wait_recv()
        for cp in first + passed:
            cp.wait_send()
        for cp in mine:
            cp.wait()

    any_spec = pl.BlockSpec(memory_space=pl.ANY)
    return _pallas(
        body, name="all_gather_weights", out_shape=[jax.ShapeDtypeStruct((N_DEV, *s.shape), s.dtype) for s in shards],
        in_specs=[any_spec] * n, out_specs=[any_spec] * n,
        scratch_shapes=[pltpu.SemaphoreType.DMA((n, 7)), pltpu.SemaphoreType.DMA((n, 7)), pltpu.SemaphoreType.DMA((n,))],
    )(*shards)


def all_to_all_blocks(stacks):
    n = len(stacks)

    def body(*refs):
        ins, outs = refs[:n], refs[n:2 * n]
        send_sems, recv_sems, local_sems = refs[2 * n:]
        x, y, c = _mesh_pos()
        me = 4 * x + 2 * y + c
        mine = [pltpu.make_async_copy(ins[t].at[me], outs[t].at[me], local_sems.at[t]) for t in range(n)]
        for cp in mine:
            cp.start()
        copies = []
        for k in range(1, N_DEV):
            px = 1 - x if k & 4 else x
            py = 1 - y if k & 2 else y
            pc = 1 - c if k & 1 else c
            for t in range(n):
                cp = pltpu.make_async_remote_copy(
                    src_ref=ins[t].at[4 * px + 2 * py + pc], dst_ref=outs[t].at[me], send_sem=send_sems.at[t, k - 1],
                    recv_sem=recv_sems.at[t, k - 1], device_id=(px, py, pc), device_id_type=pl.DeviceIdType.MESH)
                cp.start()
                copies.append(cp)
        for cp in copies:
            cp.wait_recv()
        for cp in copies:
            cp.wait_send()
        for cp in mine:
            cp.wait()

    any_spec = pl.BlockSpec(memory_space=pl.ANY)
    return _pallas(
        body, name="all_to_all_grads", out_shape=[jax.ShapeDtypeStruct(s.shape, s.dtype) for s in stacks],
        in_specs=[any_spec] * n, out_specs=[any_spec] * n,
        scratch_shapes=[pltpu.SemaphoreType.DMA((n, 7)), pltpu.SemaphoreType.DMA((n, 7)), pltpu.SemaphoreType.DMA((n,))],
    )(*stacks)


def all_gather_small(v):
    r, w = v.shape

    def body(x_ref, out_ref, send_sems, recv_sems):
        x, y, c = _mesh_pos()
        me = 4 * x + 2 * y + c
        copies = []
        for k in range(1, N_DEV):
            px = 1 - x if k & 4 else x
            py = 1 - y if k & 2 else y
            pc = 1 - c if k & 1 else c
            cp = pltpu.make_async_remote_copy(
                src_ref=x_ref, dst_ref=out_ref.at[me], send_sem=send_sems.at[k - 1], recv_sem=recv_sems.at[k - 1],
                device_id=(px, py, pc), device_id_type=pl.DeviceIdType.MESH)
            cp.start()
            copies.append(cp)
        out_ref[me] = x_ref[...]
        for cp in copies:
            cp.wait_recv()
        for cp in copies:
            cp.wait_send()

    vmem = pl.BlockSpec(memory_space=pltpu.VMEM)
    return _pallas(
        body, name="all_gather_small", out_shape=jax.ShapeDtypeStruct((N_DEV, r, w), v.dtype), in_specs=[vmem],
        out_specs=vmem, scratch_shapes=[pltpu.SemaphoreType.DMA((7,)), pltpu.SemaphoreType.DMA((7,))],
    )(v)


def adamw_rows(parts, w, m, v, name):
    n, r, lanes = parts.shape
    tr = _tile(r, max(SUBLANES, (256 * 1024) // lanes), SUBLANES)
    c1 = 1.0 / (1.0 - ADAM_B1 ** ADAM_STEP)
    c2 = 1.0 / (1.0 - ADAM_B2 ** ADAM_STEP)

    def body(p_ref, w_ref, m_ref, v_ref, g_ref, d_ref, nm_ref, nv_ref):
        g = p_ref[0].astype(F32)
        for j in range(1, n):
            g = g + p_ref[j].astype(F32)
        nm = ADAM_B1 * m_ref[...] + (1.0 - ADAM_B1) * g
        nv = ADAM_B2 * v_ref[...] + (1.0 - ADAM_B2) * (g * g)
        g_ref[...] = g
        nm_ref[...] = nm
        nv_ref[...] = nv
        d_ref[...] = -ADAM_LR * ((nm * c1) / (jnp.sqrt(nv * c2) + ADAM_EPS) + ADAM_WD * w_ref[...])

    row = pl.BlockSpec((tr, lanes), lambda i: (i, 0))
    out = jax.ShapeDtypeStruct((r, lanes), F32)
    return _pallas(body, name=name, grid=(r // tr,), in_specs=[pl.BlockSpec((n, tr, lanes), lambda i: (0, i, 0)), row, row, row],
                   out_specs=[row, row, row, row], out_shape=[out, out, out, out], compiler_params=_cparams("parallel"))(
        parts, w, m, v)


def _padded(n):
    return -(-n // PACK_QUANTUM) * PACK_QUANTUM


def _pack(pieces, total_rows=None):
    flat = []
    for p in pieces:
        p = p.reshape(-1).astype(F32)
        flat.append(jnp.pad(p, (0, _padded(p.size) - p.size)))
    out = jnp.concatenate(flat).reshape(-1, LANES)
    if total_rows is not None and out.shape[0] != total_rows:
        out = jnp.pad(out, ((0, total_rows - out.shape[0]), (0, 0)))
    return out


def _pack_rows(sizes):
    rows = sum(_padded(n) for n in sizes) // LANES
    return -(-rows // PACK_ROW_TILE) * PACK_ROW_TILE


def _unpack(rows, shapes):
    lead = rows.shape[:-2]
    flat = rows.reshape(*lead, -1)
    out, off = [], 0
    for shp in shapes:
        n = int(np.prod(shp))
        out.append(flat[..., off:off + n].reshape(*lead, *shp))
        off += _padded(n)
    return out


def _shards_to_full(stacked, axis):
    moved = jnp.moveaxis(stacked, 0, axis)
    shp = list(stacked.shape[1:])
    shp[axis] *= N_DEV
    return moved.reshape(shp)


def _full_to_shards(full, axis):
    shp = list(full.shape)
    shp[axis:axis + 1] = [N_DEV, shp[axis] // N_DEV]
    return jnp.moveaxis(full.reshape(shp), axis, 0)


def _heads(t, n, d):
    return jnp.transpose(t.reshape(t.shape[0], n, d), (1, 0, 2)).astype(ACT_DTYPE)


def _unheads(t):
    return jnp.transpose(t, (1, 0, 2)).reshape(t.shape[1], -1)


def _rope_tables(s):
    half = A_ROPE // 2
    inv = ROPE_THETA ** (-jnp.arange(half, dtype=F32) / half)
    ang = jnp.arange(s, dtype=jnp.int32).astype(F32)[:, None] * inv[None, :]
    return jnp.cos(ang), jnp.sin(ang)


def _rope(t, cos, sin):
    half = A_ROPE // 2
    t1, t2 = t[..., :half], t[..., half:]
    c, sn = cos[:, None, :], sin[:, None, :]
    return jnp.concatenate([t1 * c - t2 * sn, t1 * sn + t2 * c], axis=-1)


def _t5_bucket(rel):
    nb = REL_BUCKETS // 2
    max_exact = nb // 2
    ret = (rel > 0).astype(jnp.int32) * nb
    n = jnp.abs(rel)
    large = max_exact + (jnp.log(jnp.maximum(n, 1).astype(F32) / max_exact)
                         / math.log(REL_MAX_DIST / max_exact) * (nb - max_exact)).astype(jnp.int32)
    large = jnp.minimum(large, nb - 1)
    return ret + jnp.where(n < max_exact, n, large)


def _window_bias(rel_bias):
    span = 3 * C_BLOCK
    rel = jnp.arange(span)[None, :] - C_BLOCK - jnp.arange(C_BLOCK)[:, None]
    onehot = (_t5_bucket(rel)[..., None] == jnp.arange(REL_BUCKETS)).astype(F32)
    bias = jnp.einsum("qkb,bh->hqk", onehot, rel_bias.astype(F32), precision=lax.Precision.HIGHEST)
    bias = jnp.where((jnp.abs(rel) <= C_WINDOW)[None], bias, NEG)
    return jnp.transpose(bias.reshape(C_HEADS, C_BLOCK, 3, C_BLOCK), (0, 2, 1, 3))


def _mla(cq, ckv, kr, gq, gkv, wuq, wukv, cos, sin):
    s = cq.shape[0]
    q = linear(rmsnorm(cq, gq, ACT_DTYPE, "rms_cq"), wuq, name="a_wuq").reshape(s, A_HEADS, A_NOPE + A_ROPE)
    q = jnp.concatenate([q[..., :A_NOPE], _rope(q[..., A_NOPE:], cos, sin)], axis=-1)
    kv = linear(rmsnorm(ckv, gkv, ACT_DTYPE, "rms_ckv"), wukv, name="a_wukv").reshape(s, A_HEADS, A_NOPE + A_V)
    k_rope = jnp.broadcast_to(_rope(kr[:, None, :], cos, sin), (s, A_HEADS, A_ROPE))
    k = jnp.concatenate([kv[..., :A_NOPE], k_rope], axis=-1)
    v = kv[..., A_NOPE:]
    return mla_attention(q, k, v, (A_NOPE + A_ROPE) ** -0.5)


def _hgrn2(q, f_fwd, f_bwd, i, g, lb_fwd, lb_bwd, g_out):
    s = q.shape[0]
    n_hp = B_HEADS // HG_GROUP
    pairs = lambda t: jnp.transpose(t.reshape(s, n_hp, -1), (1, 0, 2))
    qp, vp = pairs(q), pairs(i)
    o = None
    for z, lb, rev, tag in ((f_fwd, lb_fwd, False, "hgf"), (f_bwd, lb_bwd, True, "hgb")):
        qd, ki, ke, dec = hgrn_prep(qp, pairs(z), lb.astype(F32).reshape(n_hp, 1, -1), rev, tag + "_prep")
        part = hgrn_intra(qd, ki, vp, rev, tag + "_intra") + hgrn_inter(qd, ke, vp, dec, rev, tag + "_inter")
        o = part if o is None else o + part
    o = jnp.transpose(o, (1, 0, 2)).reshape(s * B_HEADS, B_DV)
    o = rmsnorm(o, g_out, F32, "rms_hg").reshape(s, B_HEADS * B_DV)
    return o * jax.nn.silu(g)


def _cross(x, h, mem_n, wq, wkv, wo):
    q = _heads(linear(h, wq, name="x_wq"), X_HEADS, X_DH)
    kv = linear(mem_n, wkv, name="x_wkv").reshape(mem_n.shape[0], 2, X_HEADS, X_DH)
    k = jnp.transpose(kv[:, 0], (1, 0, 2)).astype(ACT_DTYPE)
    v = jnp.transpose(kv[:, 1], (1, 0, 2)).astype(ACT_DTYPE)
    o = attention(q, k, v, X_DH ** -0.5, CROSS_TQ, 256, "cross")
    return linear_res(x, _unheads(o), wo, name="x_wo")


def _pad_w_in(w):
    cut = A_Q_RANK + A_KV_RANK + A_ROPE
    return jnp.concatenate([w[:, :cut], jnp.zeros((w.shape[0], KR_PAD), w.dtype), w[:, cut:]], axis=1)


def _model_loss(p, x, mem, target):
    s = x.shape[0]
    cos, sin = _rope_tables(s)
    sm = jax.nn.softmax(p["b_lb"].astype(F32), axis=1)
    lower_bounds = jnp.cumsum(sm, axis=1) - sm[:, :1]
    bias = _window_bias(p["rel_bias"])
    for l in range(DEPTH):
        h = rmsnorm(x, p["g_mix"][l], ACT_DTYPE, "rms_mix")
        z = linear(h, _pad_w_in(p["w_in"][l]), name="w_in")
        parts, start = [], 0
        for width in IN_SPLITS_PADDED:
            parts.append(z[:, start:start + width])
            start += width
        a_cq, a_ckv, a_kr, b_q, b_ff, b_fb, b_i, b_g, c_q, c_k, c_v, gate_a, gate_b, gate_c = parts
        y_a = _mla(a_cq, a_ckv, a_kr[:, :A_ROPE], p["a_gq"][l], p["a_gkv"][l], p["a_wuq"][l], p["a_wukv"][l], cos, sin)
        y_b = _hgrn2(b_q, b_ff, b_fb, b_i, b_g, lower_bounds[0, l], lower_bounds[1, l], p["b_gout"][l])
        y_c = _unheads(window_attention(_heads(c_q, C_HEADS, C_DH), _heads(c_k, C_KV_HEADS, C_DH),
                                        _heads(c_v, C_KV_HEADS, C_DH), bias, p["c_sink"][l]))
        x = gated_merge_out(x, y_a, y_b, y_c, gate_a, gate_b, gate_c, p["w_br_a"][l], p["w_br_b"][l], p["w_br_c"][l],
                            p["w_out"][l])
        h = rmsnorm(x, p["g_x"][l], ACT_DTYPE, "rms_x")
        x = _cross(x, h, rmsnorm(mem, p["g_mem"][l], ACT_DTYPE, "rms_mem"), p["x_wq"][l], p["x_wkv"][l], p["x_wo"][l])
        h = rmsnorm(x, p["g_ffn"][l], ACT_DTYPE, "rms_ffn")
        x = swiglu_ffn(x, h, p["f_w1"][l], p["f_w3"][l], p["f_w2"][l])
    y = rmsnorm(x, p["g_final"], F32, "rms_final")
    return loss_head(y, target)


def kernel(x, mem, w_in, g_mix, a_gq, a_gkv, a_wuq, a_wukv, b_lb, b_gout, c_sink, rel_bias, w_br_a, w_br_b, w_br_c, w_out, g_x, g_mem, x_wq, x_wkv, x_wo, g_ffn, f_w1, f_w3, f_w2, g_final, loss_target, m_w_in, m_g_mix, m_a_gq, m_a_gkv, m_a_wuq, m_a_wukv, m_b_lb, m_b_gout, m_c_sink, m_rel_bias, m_w_br_a, m_w_br_b, m_w_br_c, m_w_out, m_g_x, m_g_mem, m_x_wq, m_x_wkv, m_x_wo, m_g_ffn, m_f_w1, m_f_w3, m_f_w2, m_g_final, v_w_in, v_g_mix, v_a_gq, v_a_gkv, v_a_wuq, v_a_wukv, v_b_lb, v_b_gout, v_c_sink, v_rel_bias, v_w_br_a, v_w_br_b, v_w_br_c, v_w_out, v_g_x, v_g_mem, v_x_wq, v_x_wkv, v_x_wo, v_g_ffn, v_f_w1, v_f_w3, v_f_w2, v_g_final):
    given = dict(locals())
    w = {n: given[n] for n in WEIGHT_ORDER}
    m = {n: given["m_" + n] for n in WEIGHT_ORDER}
    v = {n: given["v_" + n] for n in WEIGHT_ORDER}
    sh_names = [n for n, _ in SHARDED]
    rep_shapes = [w[n].shape for n in REPLICATED] + [(1,)]
    rep_rows = _pack_rows([int(np.prod(s)) for s in rep_shapes])

    wire = [w[n] if n in ELEMENTWISE_SHARDED else w[n].astype(MXU_DTYPE) for n in sh_names]
    gathered = all_gather_shards(wire)
    full = {n: _shards_to_full(t, ax).astype(F32) for (n, ax), t in zip(SHARDED, gathered)}
    full.update({n: w[n] for n in REPLICATED})

    loss, (grad_full, grad_x) = jax.value_and_grad(_model_loss, argnums=(0, 1))(full, x[0], mem[0], loss_target[0])

    received = all_to_all_blocks([_full_to_shards(grad_full[n], ax).astype(GRAD_WIRE_DTYPE) for n, ax in SHARDED])
    g_sh, d_sh, nm_sh, nv_sh = {}, {}, {}, {}
    for n, got in zip(sh_names, received):
        shp = w[n].shape
        rows = lambda t: t.reshape(-1, shp[-1])
        outs = adamw_rows(got.reshape(N_DEV, -1, shp[-1]), rows(w[n]), rows(m[n]), rows(v[n]), "adamw_" + n)
        g_sh[n], d_sh[n], nm_sh[n], nv_sh[n] = [o.reshape(shp) for o in outs]

    mine = _pack([grad_full[n] for n in REPLICATED] + [loss.reshape(1)], rep_rows)
    everyone = all_gather_small(mine)
    rep_w = [w[n] for n in REPLICATED] + [jnp.zeros((1,), F32)]
    outs = adamw_rows(everyone, _pack(rep_w, rep_rows), _pack([m[n] for n in REPLICATED] + [jnp.zeros((1,), F32)], rep_rows),
                      _pack([v[n] for n in REPLICATED] + [jnp.ones((1,), F32)], rep_rows), "adamw_replicated")
    rep_names = list(REPLICATED) + ["loss"]
    g_rp, d_rp, nm_rp, nv_rp = [dict(zip(rep_names, _unpack(o, rep_shapes))) for o in outs]

    def pick(sharded, replicated, n):
        return sharded[n] if n in sharded else replicated[n]

    return (g_rp["loss"].reshape(()), grad_x[None],
            *[pick(g_sh, g_rp, n) for n in WEIGHT_ORDER], *[pick(d_sh, d_rp, n) for n in WEIGHT_ORDER],
            *[pick(nm_sh, nm_rp, n) for n in WEIGHT_ORDER], *[pick(nv_sh, nv_rp, n) for n in WEIGHT_ORDER])
```

```python
import functools
import math

import jax
import jax.numpy as jnp
import numpy as np
from jax import lax
from jax.experimental import pallas as pl
from jax.experimental.pallas import tpu as pltpu

F32 = jnp.float32
MXU_DTYPE = jnp.bfloat16
ACT_DTYPE = jnp.bfloat16
GRAD_WIRE_DTYPE = jnp.bfloat16

V7X_VMEM_LIMIT_BYTES = 56 * 1024 * 1024
LANES = 128
SUBLANES = 8

N_DEV = 8
D_MODEL = 1024
DEPTH = 2
EPS = 1e-6
TINY = 1e-30
NEG = -1e30

A_HEADS, A_NOPE, A_ROPE, A_V, A_Q_RANK, A_KV_RANK = 8, 64, 32, 64, 384, 256
ROPE_THETA = 10000.0
B_HEADS, B_DK, B_DV, B_CHUNK = 8, 128, 64, 16
C_HEADS, C_KV_HEADS, C_DH, C_WINDOW, C_BLOCK = 8, 2, 64, 128, 128
REL_BUCKETS, REL_MAX_DIST = 32, 128
X_HEADS, X_DH = 4, 256
D_FF = 2816
IN_SPLITS = (A_Q_RANK, A_KV_RANK, A_ROPE, 1024, 1024, 1024, 512, 512, 512, 128, 128, 1024, 1024, 1024)
IN_WIDTH = sum(IN_SPLITS)
KR_PAD = LANES - A_ROPE
IN_SPLITS_PADDED = (A_Q_RANK, A_KV_RANK, LANES, 1024, 1024, 1024, 512, 512, 512, 128, 128, 1024, 1024, 1024)

ADAM_LR, ADAM_B1, ADAM_B2, ADAM_EPS, ADAM_WD, ADAM_STEP = 0.001, 0.9, 0.999, 1e-08, 0.01, 10

SHARDED = (("w_in", 2), ("a_wuq", 2), ("a_wukv", 2), ("b_lb", 2), ("w_br_a", 2), ("w_br_b", 2), ("w_br_c", 2),
           ("w_out", 1), ("x_wq", 1), ("x_wkv", 2), ("x_wo", 1), ("f_w1", 2), ("f_w3", 2), ("f_w2", 1))
ELEMENTWISE_SHARDED = ("b_lb",)
REPLICATED = ("g_mix", "a_gq", "a_gkv", "b_gout", "c_sink", "rel_bias", "g_x", "g_mem", "g_ffn", "g_final")
WEIGHT_ORDER = ("w_in", "g_mix", "a_gq", "a_gkv", "a_wuq", "a_wukv", "b_lb", "b_gout", "c_sink", "rel_bias", "w_br_a",
                "w_br_b", "w_br_c", "w_out", "g_x", "g_mem", "x_wq", "x_wkv", "x_wo", "g_ffn", "f_w1", "f_w3", "f_w2",
                "g_final")
PACK_QUANTUM = SUBLANES * LANES
PACK_ROW_TILE = 512


def _pallas(body, **kw):
    return pl.pallas_call(body, **kw)


def _cparams(*sem):
    return pltpu.CompilerParams(dimension_semantics=sem, vmem_limit_bytes=V7X_VMEM_LIMIT_BYTES)


def _tile(n, target, mult=LANES):
    t = (min(target, n) // mult) * mult
    while t >= mult:
        if n % t == 0:
            return t
        t -= mult
    return n


def _dot(a, b, dims):
    return lax.dot_general(a.astype(MXU_DTYPE), b.astype(MXU_DTYPE), (dims, ((), ())), preferred_element_type=F32)


NN = ((1,), (0,))
NT = ((1,), (1,))
TN = ((0,), (0,))


MM_VMEM_BUDGET_BYTES = 40 * 1024 * 1024
MM_MAX_TILE = 4352
MM_MAX_ROW_TILE = 2048
MM_HBM_BYTES_PER_S = 2.5e12
MM_STEP_S = 0.4e-6
MM_DMA_ROW_OVERHEAD_BYTES = 512.0


def _tile_options(n, cap):
    out = [t for t in range(LANES, min(n, cap) + 1, LANES) if n % t == 0]
    if n <= cap and n not in out:
        out.append(n)
    return out or [n]


@functools.lru_cache(maxsize=None)
def _mm_plan(m, n, k, ta, tb, a_bytes, b_bytes, o_bytes):
    best = None
    for tk in _tile_options(k, MM_MAX_TILE):
        nk = k // tk
        for tn in _tile_options(n, MM_MAX_TILE):
            for tm in _tile_options(m, MM_MAX_ROW_TILE):
                vmem = 2 * (tm * tk * a_bytes + tk * tn * b_bytes + tm * tn * o_bytes) + tm * tn * 4
                vmem += (tm * tk * 2 if a_bytes == 4 else 0) + (tk * tn * 2 if b_bytes == 4 else 0)
                if vmem > MM_VMEM_BUDGET_BYTES:
                    continue

                def eff(elems, nbytes):
                    return (elems * nbytes) / (elems * nbytes + MM_DMA_ROW_OVERHEAD_BYTES)

                ea, eb, eo = eff(tm if ta else tk, a_bytes), eff(tk if tb else tn, b_bytes), eff(tn, o_bytes)
                for order in ("mn", "nm"):
                    if nk == 1 and order == "nm":
                        a_tr, b_tr = m * k * a_bytes * (n // tn), k * n * b_bytes
                    elif nk == 1:
                        a_tr, b_tr = m * k * a_bytes, k * n * b_bytes * (m // tm)
                    else:
                        a_tr, b_tr = m * k * a_bytes * (n // tn), k * n * b_bytes * (m // tm)
                    steps = (m // tm) * (n // tn) * nk
                    cost = (a_tr / ea + b_tr / eb + m * n * o_bytes / eo) / MM_HBM_BYTES_PER_S + steps * MM_STEP_S
                    if best is None or cost < best[0]:
                        best = (cost, tm, tn, tk, order)
    assert best is not None, (m, n, k)
    return best[1:]


def _mm(a, b, ta=False, tb=False, out_dtype=F32, name="mm", res=None):
    m, k = (a.shape[1], a.shape[0]) if ta else a.shape
    kb, n = (b.shape[1], b.shape[0]) if tb else b.shape
    assert k == kb, (a.shape, b.shape, ta, tb)
    tm, tn, tk, order = _mm_plan(m, n, k, ta, tb, a.dtype.itemsize, b.dtype.itemsize, jnp.dtype(out_dtype).itemsize)
    nk = k // tk
    dims = ((0 if ta else 1,), (1 if tb else 0,))
    out_shape = jax.ShapeDtypeStruct((m, n), out_dtype)

    assert res is None or nk == 1, (name, k, tk)
    if nk == 1:
        def body(a_ref, b_ref, *rest):
            acc = _dot(a_ref[...], b_ref[...], dims)
            if res is not None:
                acc = rest[0][...] + acc
            rest[-1][...] = acc.astype(rest[-1].dtype)

        if order == "nm":
            mi, ni = (lambda j, i: i), (lambda j, i: j)
            grid = (n // tn, m // tm)
        else:
            mi, ni = (lambda i, j: i), (lambda i, j: j)
            grid = (m // tm, n // tn)
        a_spec = pl.BlockSpec((tk, tm), lambda p, q: (0, mi(p, q))) if ta else pl.BlockSpec((tm, tk), lambda p, q: (mi(p, q), 0))
        b_spec = pl.BlockSpec((tn, tk), lambda p, q: (ni(p, q), 0)) if tb else pl.BlockSpec((tk, tn), lambda p, q: (0, ni(p, q)))
        o_spec = pl.BlockSpec((tm, tn), lambda p, q: (mi(p, q), ni(p, q)))
        extra = [] if res is None else [res]
        return _pallas(body, name=name, grid=grid, in_specs=[a_spec, b_spec] + [o_spec] * len(extra), out_specs=o_spec,
                       out_shape=out_shape, compiler_params=_cparams("parallel", "parallel"))(a, b, *extra)

    direct = jnp.dtype(out_dtype) == jnp.dtype(F32)

    def body(a_ref, b_ref, o_ref, *scratch):
        acc_ref = o_ref if direct else scratch[0]
        kk = pl.program_id(2)

        @pl.when(kk == 0)
        def _():
            acc_ref[...] = jnp.zeros_like(acc_ref)

        acc_ref[...] += _dot(a_ref[...], b_ref[...], dims)

        if not direct:
            @pl.when(kk == nk - 1)
            def _():
                o_ref[...] = acc_ref[...].astype(o_ref.dtype)

    a_spec = pl.BlockSpec((tk, tm), lambda i, j, kk: (kk, i)) if ta else pl.BlockSpec((tm, tk), lambda i, j, kk: (i, kk))
    b_spec = pl.BlockSpec((tn, tk), lambda i, j, kk: (j, kk)) if tb else pl.BlockSpec((tk, tn), lambda i, j, kk: (kk, j))
    return _pallas(
        body, name=name, grid=(m // tm, n // tn, nk), in_specs=[a_spec, b_spec],
        out_specs=pl.BlockSpec((tm, tn), lambda i, j, kk: (i, j)), out_shape=out_shape,
        scratch_shapes=[] if direct else [pltpu.VMEM((tm, tn), F32)],
        compiler_params=_cparams("parallel", "parallel", "arbitrary"),
    )(a, b)


def linear(a, w, out_dtype=F32, name="lin"):
    @jax.custom_vjp
    def f(a, w):
        return _mm(a.astype(ACT_DTYPE), w.astype(MXU_DTYPE), out_dtype=out_dtype, name=name + "_fwd")

    def fwd(a, w):
        ab, wb = a.astype(ACT_DTYPE), w.astype(MXU_DTYPE)
        return _mm(ab, wb, out_dtype=out_dtype, name=name + "_fwd"), (ab, wb, jnp.zeros((0,), a.dtype))

    def bwd(res, g):
        ab, wb, like_a = res
        gb = g.astype(ACT_DTYPE)
        da = _mm(gb, wb, tb=True, out_dtype=like_a.dtype, name=name + "_dx")
        dw = _mm(ab, gb, ta=True, out_dtype=F32, name=name + "_dw")
        return da, dw

    f.defvjp(fwd, bwd)
    return f(a, w)


def linear_res(x, a, w, name="lin", a_transposed=False):
    @jax.custom_vjp
    def f(x, a, w):
        return _mm(a.astype(ACT_DTYPE), w.astype(MXU_DTYPE), ta=a_transposed, name=name + "_fwd", res=x)

    def fwd(x, a, w):
        ab, wb = a.astype(ACT_DTYPE), w.astype(MXU_DTYPE)
        return _mm(ab, wb, ta=a_transposed, name=name + "_fwd", res=x), (ab, wb, jnp.zeros((0,), a.dtype))

    def bwd(res, g):
        ab, wb, like_a = res
        gb = g.astype(ACT_DTYPE)
        if a_transposed:
            da = _mm(wb, gb, tb=True, out_dtype=like_a.dtype, name=name + "_dx")
        else:
            da = _mm(gb, wb, tb=True, out_dtype=like_a.dtype, name=name + "_dx")
        dw = _mm(ab, gb, ta=not a_transposed, out_dtype=F32, name=name + "_dw")
        return g, da, dw

    f.defvjp(fwd, bwd)
    return f(x, a, w)


FFN_ROW_TILE = 512
FFN_COL_TILE = 1408


def _sigmoid(a):
    return 1.0 / (1.0 + jnp.exp(-a))


def swiglu_ffn(x, h, w1, w3, w2, name="ffn"):
    m, d = h.shape
    f_dim = w1.shape[1]
    tm, tn = _tile(m, FFN_ROW_TILE), _tile(f_dim, FFN_COL_TILE)

    def up_body(h_ref, w1_ref, w3_ref, t_ref, a_ref, b_ref):
        hv = h_ref[...]
        a = _dot(hv, w1_ref[...], NN)
        b = _dot(hv, w3_ref[...], NN)
        a_ref[...] = a
        b_ref[...] = b
        t_ref[...] = (a * _sigmoid(a) * b).astype(t_ref.dtype)

    def dt_body(g_ref, w2_ref, a_ref, b_ref, da_ref, db_ref):
        dt = _dot(g_ref[...], w2_ref[...], NT)
        a, b = a_ref[...], b_ref[...]
        sg = _sigmoid(a)
        da_ref[...] = (dt * b * (sg * (1.0 + a * (1.0 - sg)))).astype(da_ref.dtype)
        db_ref[...] = (dt * (a * sg)).astype(db_ref.dtype)

    row = pl.BlockSpec((tm, d), lambda j, i: (i, 0))
    w_up = pl.BlockSpec((d, tn), lambda j, i: (0, j))
    w_dn = pl.BlockSpec((tn, d), lambda j, i: (j, 0))
    tile = pl.BlockSpec((tm, tn), lambda j, i: (i, j))
    grid = (f_dim // tn, m // tm)

    def run_up(hb, w1b, w3b):
        return _pallas(up_body, name=name + "_up", grid=grid, in_specs=[row, w_up, w_up], out_specs=[tile, tile, tile],
                       out_shape=[jax.ShapeDtypeStruct((m, f_dim), ACT_DTYPE), jax.ShapeDtypeStruct((m, f_dim), F32),
                                  jax.ShapeDtypeStruct((m, f_dim), F32)],
                       compiler_params=_cparams("parallel", "parallel"))(hb, w1b, w3b)

    def forward(x, h, w1, w3, w2):
        hb = h.astype(ACT_DTYPE)
        w1b, w3b, w2b = w1.astype(MXU_DTYPE), w3.astype(MXU_DTYPE), w2.astype(MXU_DTYPE)
        t, a, b = run_up(hb, w1b, w3b)
        return _mm(t, w2b, name=name + "_down", res=x), (hb, w1b, w3b, w2b, t, a, b, jnp.zeros((0,), h.dtype))

    @jax.custom_vjp
    def f(x, h, w1, w3, w2):
        return forward(x, h, w1, w3, w2)[0]

    def bwd(res, g):
        hb, w1b, w3b, w2b, t, a, b, like_h = res
        gb = g.astype(ACT_DTYPE)
        da, db = _pallas(dt_body, name=name + "_dt", grid=grid, in_specs=[row, w_dn, tile, tile], out_specs=[tile, tile],
                         out_shape=[jax.ShapeDtypeStruct((m, f_dim), ACT_DTYPE)] * 2,
                         compiler_params=_cparams("parallel", "parallel"))(gb, w2b, a, b)
        dw2 = _mm(t, gb, ta=True, name=name + "_dw2")
        dh = _mm(da, w1b, tb=True, name=name + "_dx1") + _mm(db, w3b, tb=True, name=name + "_dx3")
        dw1 = _mm(hb, da, ta=True, name=name + "_dw1")
        dw3 = _mm(hb, db, ta=True, name=name + "_dw3")
        return g, dh.astype(like_h.dtype), dw1, dw3, dw2

    f.defvjp(lambda *args: forward(*args), bwd)
    return f(x, h, w1, w3, w2)


MERGE_ROW_TILE = 256


def gated_merge_out(x, ya_t, yb, yc, ga, gb, gc, wa, wb, wc, wo, name="merge"):
    s, d = x.shape
    e = yb.shape[1]
    tm = _tile(s, MERGE_ROW_TILE)

    def branches(ya_ref, yb_ref, yc_ref, wa_ref, wb_ref, wc_ref):
        return (_dot(ya_ref[...], wa_ref[...], TN), _dot(yb_ref[...], wb_ref[...], NN), _dot(yc_ref[...], wc_ref[...], NN))

    def fwd_body(x_ref, ya_ref, yb_ref, yc_ref, ga_ref, gb_ref, gc_ref, wa_ref, wb_ref, wc_ref, wo_ref, o_ref, m_ref):
        pa, pb, pc = branches(ya_ref, yb_ref, yc_ref, wa_ref, wb_ref, wc_ref)
        merged = _sigmoid(ga_ref[...]) * pa + _sigmoid(gb_ref[...]) * pb + _sigmoid(gc_ref[...]) * pc
        mb = merged.astype(m_ref.dtype)
        m_ref[...] = mb
        o_ref[...] = x_ref[...] + _dot(mb, wo_ref[...], NN)

    def bwd_body(g_ref, ya_ref, yb_ref, yc_ref, ga_ref, gb_ref, gc_ref, wa_ref, wb_ref, wc_ref, wo_ref,
                 dga_ref, dgb_ref, dgc_ref, dpa_ref, dpb_ref, dpc_ref):
        dm = _dot(g_ref[...], wo_ref[...], NT)
        ps = branches(ya_ref, yb_ref, yc_ref, wa_ref, wb_ref, wc_ref)
        for p_i, gate_ref, dg_ref, dp_ref in zip(ps, (ga_ref, gb_ref, gc_ref), (dga_ref, dgb_ref, dgc_ref),
                                                 (dpa_ref, dpb_ref, dpc_ref)):
            sg = _sigmoid(gate_ref[...])
            dg_ref[...] = dm * p_i * (sg * (1.0 - sg))
            dp_ref[...] = (dm * sg).astype(dp_ref.dtype)

    rows = lambda width: pl.BlockSpec((tm, width), lambda i: (i, 0))
    cols_t = pl.BlockSpec((e, tm), lambda i: (0, i))
    whole = lambda r, c: pl.BlockSpec((r, c), lambda i: (0, 0))
    in_common = [cols_t, rows(e), rows(e), rows(d), rows(d), rows(d), whole(e, d), whole(e, d), whole(e, d), whole(d, d)]

    def forward(x, ya_t, yb, yc, ga, gb, gc, wa, wb, wc, wo):
        cast = lambda t: t.astype(ACT_DTYPE)
        ops = (cast(ya_t), cast(yb), cast(yc), ga, gb, gc, cast(wa), cast(wb), cast(wc), cast(wo))
        out, merged = _pallas(
            fwd_body, name=name + "_fwd", grid=(s // tm,), in_specs=[rows(d)] + in_common, out_specs=[rows(d), rows(d)],
            out_shape=[jax.ShapeDtypeStruct((s, d), F32), jax.ShapeDtypeStruct((s, d), ACT_DTYPE)],
            compiler_params=_cparams("parallel"))(x, *ops)
        like = tuple(jnp.zeros((0,), t.dtype) for t in (ya_t, yb, yc))
        return out, (ops, merged, like)

    @jax.custom_vjp
    def f(*args):
        return forward(*args)[0]

    def bwd(res, g):
        ops, merged, like = res
        ya_b, yb_b, yc_b, ga, gb, gc, wa_b, wb_b, wc_b, wo_b = ops
        gbf = g.astype(ACT_DTYPE)
        gate_ct = jax.ShapeDtypeStruct((s, d), F32)
        branch_ct = jax.ShapeDtypeStruct((s, d), ACT_DTYPE)
        dga, dgb, dgc, dpa, dpb, dpc = _pallas(
            bwd_body, name=name + "_bwd", grid=(s // tm,), in_specs=[rows(d)] + in_common, out_specs=[rows(d)] * 6,
            out_shape=[gate_ct] * 3 + [branch_ct] * 3, compiler_params=_cparams("parallel"))(gbf, *ops)
        dya_t = _mm(wa_b, dpa, tb=True, out_dtype=like[0].dtype, name=name + "_dya")
        dyb = _mm(dpb, wb_b, tb=True, out_dtype=like[1].dtype, name=name + "_dyb")
        dyc = _mm(dpc, wc_b, tb=True, out_dtype=like[2].dtype, name=name + "_dyc")
        dwa = _mm(ya_b, dpa, name=name + "_dwa")
        dwb = _mm(yb_b, dpb, ta=True, name=name + "_dwb")
        dwc = _mm(yc_b, dpc, ta=True, name=name + "_dwc")
        dwo = _mm(merged, gbf, ta=True, name=name + "_dwo")
        return g, dya_t, dyb, dyc, dga, dgb, dgc, dwa, dwb, dwc, dwo

    f.defvjp(lambda *args: forward(*args), bwd)
    return f(x, ya_t, yb, yc, ga, gb, gc, wa, wb, wc, wo)


def _row_tile(rows, width):
    return _tile(rows, max(SUBLANES, (512 * 1024) // width), 16)


def rmsnorm(x, g, out_dtype=F32, name="rms"):
    rows, d = x.shape
    tr = _row_tile(rows, d)
    n_steps = rows // tr

    def fwd_body(x_ref, g_ref, o_ref):
        xv = x_ref[...].astype(F32)
        r = lax.rsqrt(jnp.mean(xv * xv, axis=-1, keepdims=True) + EPS)
        o_ref[...] = (xv * r * g_ref[...]).astype(o_ref.dtype)

    def bwd_body(x_ref, g_ref, dy_ref, dx_ref, dg_ref):
        xv = x_ref[...].astype(F32)
        dy = dy_ref[...].astype(F32)
        r = lax.rsqrt(jnp.mean(xv * xv, axis=-1, keepdims=True) + EPS)
        xh = xv * r
        dxh = dy * g_ref[...]
        dx_ref[...] = (r * (dxh - xh * jnp.mean(dxh * xh, axis=-1, keepdims=True))).astype(dx_ref.dtype)

        @pl.when(pl.program_id(0) == 0)
        def _():
            dg_ref[...] = jnp.zeros_like(dg_ref)

        dg_ref[...] += jnp.sum(dy * xh, axis=0, keepdims=True)

    row_spec = pl.BlockSpec((tr, d), lambda i: (i, 0))
    vec_spec = pl.BlockSpec((1, d), lambda i: (0, 0))

    def run_fwd(x, g):
        return _pallas(fwd_body, name=name + "_fwd", grid=(n_steps,), in_specs=[row_spec, vec_spec], out_specs=row_spec,
                       out_shape=jax.ShapeDtypeStruct((rows, d), out_dtype), compiler_params=_cparams("parallel"))(
            x, g.reshape(1, d).astype(F32))

    @jax.custom_vjp
    def f(x, g):
        return run_fwd(x, g)

    def fwd(x, g):
        return run_fwd(x, g), (x, g)

    def bwd(res, dy):
        x, g = res
        dx, dg = _pallas(
            bwd_body, name=name + "_bwd", grid=(n_steps,), in_specs=[row_spec, vec_spec, row_spec],
            out_specs=[row_spec, vec_spec],
            out_shape=[jax.ShapeDtypeStruct((rows, d), x.dtype), jax.ShapeDtypeStruct((1, d), F32)],
            compiler_params=_cparams("arbitrary"))(x, g.reshape(1, d).astype(F32), dy)
        return dx, dg.reshape(g.shape).astype(g.dtype)

    f.defvjp(fwd, bwd)
    return f(x, g)


def _rowdot(a, b, name):
    h, s, d = a.shape
    ts = _tile(s, 2048)

    def body(a_ref, b_ref, o_ref):
        o_ref[...] = jnp.sum(a_ref[...].astype(F32) * b_ref[...].astype(F32), axis=-1, keepdims=True)

    spec = pl.BlockSpec((None, ts, d), lambda hh, i: (hh, i, 0))
    return _pallas(body, name=name, grid=(h, s // ts), in_specs=[spec, spec],
                   out_specs=pl.BlockSpec((None, ts, 1), lambda hh, i: (hh, i, 0)),
                   out_shape=jax.ShapeDtypeStruct((h, s, 1), F32), compiler_params=_cparams("parallel", "parallel"))(a, b)


LOG2E = 1.4426950408889634


def mla_attention(q, k, v, scale, name="mla"):
    s, h, d = q.shape
    sk, dv = k.shape[0], v.shape[2]
    tq, tk = _tile(s, MLA_BWD_TQ), _tile(sk, MLA_TK)
    nq, nk = s // tq, sk // tk
    tqf = _tile(s, MLA_FWD_TQ)
    ones_rows = 16
    c = scale * LOG2E

    def fwd_body(qt_ref, k_ref, vt_ref, ot_ref, lse_ref, m_ref, acc_ref):
        j = pl.program_id(2)

        @pl.when(j == 0)
        def _():
            m_ref[...] = jnp.full_like(m_ref, NEG)
            acc_ref[...] = jnp.zeros_like(acc_ref)

        st = _dot(k_ref[...], qt_ref[...], NN)
        m_prev = m_ref[...]
        m_new = jnp.maximum(m_prev, jnp.max(st, axis=0, keepdims=True) * c)
        pt = jnp.exp2(st * c - m_new)
        acc_ref[...] = jnp.exp2(m_prev - m_new) * acc_ref[...] + _dot(vt_ref[...], pt, NN)
        m_ref[...] = m_new

        @pl.when(j == nk - 1)
        def _():
            l = acc_ref[dv:dv + 1, :]
            ot_ref[...] = (acc_ref[:dv, :] / l).astype(ot_ref.dtype)
            lse_ref[...] = m_ref[...] + jnp.log2(l)

    def delta_body(ot_ref, dot_ref, o_ref):
        o_ref[...] = jnp.sum(ot_ref[...].astype(F32) * dot_ref[...].astype(F32), axis=0, keepdims=True)

    def bwd_body(qt_ref, k_ref, kt_ref, v_ref, dot_ref, lse_ref, dl_ref, dqt_ref, dk_hbm, dv_hbm, dq_acc, dk_acc, dv_acc):
        hh, i, j = pl.program_id(0), pl.program_id(1), pl.program_id(2)

        @pl.when(j == 0)
        def _():
            dq_acc[...] = jnp.zeros_like(dq_acc)

        @pl.when(i == 0)
        def _():
            dk_acc[j] = jnp.zeros((d, tk), F32)
            dv_acc[j] = jnp.zeros((dv, tk), F32)

        qt, dot_ = qt_ref[...], dot_ref[...]
        pt = jnp.exp2(_dot(k_ref[...], qt, NN) * c - lse_ref[...])
        dst = (pt * (_dot(v_ref[...], dot_, NN) - dl_ref[...])).astype(MXU_DTYPE)
        dv_acc[j] += _dot(dot_, pt, NT)
        dk_acc[j] += _dot(qt, dst, NT)
        dq_acc[...] += _dot(kt_ref[...], dst, NN)

        @pl.when(j == nk - 1)
        def _():
            dqt_ref[...] = dq_acc[...] * scale

        @pl.when(i == nq - 1)
        def _():
            dk_acc[j] = dk_acc[j] * scale
            pltpu.sync_copy(dk_acc.at[j], dk_hbm.at[hh, j])
            pltpu.sync_copy(dv_acc.at[j], dv_hbm.at[hh, j])

    def qt_spec(width):
        return pl.BlockSpec((None, width, tq), lambda hh, i, j: (hh, 0, i))

    def kt_spec(width):
        return pl.BlockSpec((None, width, tk), lambda hh, i, j: (hh, 0, j))

    def k_spec(width):
        return pl.BlockSpec((None, tk, width), lambda hh, i, j: (hh, j, 0))

    def layouts(q, k, v):
        cast = lambda t: t.astype(ACT_DTYPE)
        return (cast(jnp.transpose(q, (1, 2, 0))), cast(jnp.transpose(k, (1, 0, 2))), cast(jnp.transpose(k, (1, 2, 0))),
                cast(jnp.transpose(v, (1, 0, 2))), cast(jnp.transpose(v, (1, 2, 0))))

    def run_fwd(qt, kh, vt):
        vt_ones = jnp.concatenate([vt, jnp.ones((h, ones_rows, sk), vt.dtype)], axis=1)

        def qf_spec(width):
            return pl.BlockSpec((None, width, tqf), lambda hh, i, j: (hh, 0, i))

        return _pallas(
            fwd_body, name=name + "_fwd", grid=(h, s // tqf, nk), in_specs=[qf_spec(d), k_spec(d), kt_spec(dv + ones_rows)],
            out_specs=[qf_spec(dv), qf_spec(1)],
            out_shape=[jax.ShapeDtypeStruct((h, dv, s), ACT_DTYPE), jax.ShapeDtypeStruct((h, 1, s), F32)],
            scratch_shapes=[pltpu.VMEM((1, tqf), F32), pltpu.VMEM((dv + ones_rows, tqf), F32)],
            compiler_params=_cparams("parallel", "parallel", "arbitrary"))(qt, kh, vt_ones)

    @jax.custom_vjp
    def f(q, k, v):
        qt, kh, _, _, vt = layouts(q, k, v)
        return run_fwd(qt, kh, vt)[0].reshape(h * dv, s)

    def fwd(q, k, v):
        qt, kh, kt, vh, vt = layouts(q, k, v)
        ot, lse = run_fwd(qt, kh, vt)
        return ot.reshape(h * dv, s), (qt, kh, kt, vh, ot, lse)

    def bwd(res, dy):
        qt, kh, kt, vh, ot, lse = res
        dot_ = dy.reshape(h, dv, s)
        ts = _tile(s, 2048)
        col = pl.BlockSpec((None, dv, ts), lambda hh, i: (hh, 0, i))
        delta = _pallas(delta_body, name=name + "_delta", grid=(h, s // ts), in_specs=[col, col],
                        out_specs=pl.BlockSpec((None, 1, ts), lambda hh, i: (hh, 0, i)),
                        out_shape=jax.ShapeDtypeStruct((h, 1, s), F32), compiler_params=_cparams("parallel", "parallel"))(ot, dot_)
        any_spec = pl.BlockSpec(memory_space=pl.ANY)
        dqt, dkt, dvt = _pallas(
            bwd_body, name=name + "_bwd", grid=(h, nq, nk),
            in_specs=[qt_spec(d), k_spec(d), kt_spec(d), k_spec(dv), qt_spec(dv), qt_spec(1), qt_spec(1)],
            out_specs=[qt_spec(d), any_spec, any_spec],
            out_shape=[jax.ShapeDtypeStruct((h, d, s), F32), jax.ShapeDtypeStruct((h, nk, d, tk), F32),
                       jax.ShapeDtypeStruct((h, nk, dv, tk), F32)],
            scratch_shapes=[pltpu.VMEM((d, tq), F32), pltpu.VMEM((nk, d, tk), F32), pltpu.VMEM((nk, dv, tk), F32)],
            compiler_params=_cparams("parallel", "arbitrary", "arbitrary"))(qt, kh, kt, vh, dot_, lse, delta)
        to_tokens = lambda t: jnp.transpose(t, (1, 3, 0, 2)).reshape(sk, h, t.shape[2])
        return jnp.transpose(dqt, (2, 0, 1)), to_tokens(dkt), to_tokens(dvt)

    f.defvjp(fwd, bwd)
    return f(q, k, v)


WATTN_TQ = 2 * C_BLOCK
WATTN_KW = WATTN_TQ + 2 * C_BLOCK


def window_attention(q, k, v, bias, sink, name="wattn"):
    hq, s, dh = q.shape
    g = hq // C_KV_HEADS
    tq, kw, half = WATTN_TQ, WATTN_KW, WATTN_KW // 2
    nt = s // tq
    scale = dh ** -0.5
    sink_b = jnp.broadcast_to(sink.astype(F32).reshape(hq, 1, 1), (hq, 1, LANES))
    neg = jnp.full((hq, C_BLOCK, C_BLOCK), NEG, F32)
    tile = jnp.concatenate(
        [jnp.concatenate([bias[:, cb - rb] if 0 <= cb - rb <= 2 else neg for cb in range(kw // C_BLOCK)], axis=2)
         for rb in range(tq // C_BLOCK)], axis=1)

    def key_bias(i):
        pos = lax.broadcasted_iota(jnp.int32, (1, kw), 1) + i * tq - C_BLOCK
        return jnp.where(jnp.logical_and(pos >= 0, pos < s), 0.0, NEG)

    def both(a_ref, b_ref):
        return jnp.concatenate([a_ref[...], b_ref[...]], axis=0)

    def fwd_body(q_ref, ka_ref, kb_ref, va_ref, vb_ref, b_ref, sk_ref, o_ref, lse_ref):
        kb_ = key_bias(pl.program_id(1))
        k_all, v_all = both(ka_ref, kb_ref), both(va_ref, vb_ref)
        for hh in range(g):
            sc = _dot(q_ref[hh], k_all, NT) * scale + b_ref[hh] + kb_
            snk = sk_ref[hh][:, :1]
            m = jnp.maximum(jnp.max(sc, axis=-1, keepdims=True), snk)
            p = jnp.exp(sc - m)
            l = jnp.sum(p, axis=-1, keepdims=True) + jnp.exp(snk - m)
            o_ref[hh] = (_dot(p, v_all, NN) / l).astype(o_ref.dtype)
            lse_ref[hh] = m + jnp.log(l)

    def bwd_body(q_ref, ka_ref, kb_ref, va_ref, vb_ref, b_ref, sk_ref, do_ref, lse_ref, dl_ref,
                 dq_ref, db_ref, dsink_ref, dk_hbm, dv_hbm, dk_acc, dv_acc):
        kv, i = pl.program_id(0), pl.program_id(1)

        @pl.when(i == 0)
        def _():
            dk_acc[...] = jnp.zeros_like(dk_acc)
            dv_acc[...] = jnp.zeros_like(dv_acc)
            db_ref[...] = jnp.zeros_like(db_ref)
            dsink_ref[...] = jnp.zeros_like(dsink_ref)

        kb_ = key_bias(i)
        k_all, v_all = both(ka_ref, kb_ref), both(va_ref, vb_ref)
        dk_t = jnp.zeros((kw, dh), F32)
        dv_t = jnp.zeros((kw, dh), F32)
        for hh in range(g):
            lse, dl, do = lse_ref[hh], dl_ref[hh], do_ref[hh]
            p = jnp.exp(_dot(q_ref[hh], k_all, NT) * scale + b_ref[hh] + kb_ - lse)
            ds = p * (_dot(do, v_all, NT) - dl)
            db_ref[hh] += ds
            total = jnp.broadcast_to(-jnp.sum(jnp.exp(sk_ref[hh][:, :1] - lse) * dl, axis=0, keepdims=True), (1, LANES))
            dsink_ref[hh] += jnp.where(lax.broadcasted_iota(jnp.int32, (1, LANES), 1) == 0, total, 0.0)
            dsb = (ds * scale).astype(MXU_DTYPE)
            dq_ref[hh] = _dot(dsb, k_all, NN).astype(dq_ref.dtype)
            dk_t += _dot(dsb, q_ref[hh], TN)
            dv_t += _dot(p, do, TN)
        rows = pl.ds(pl.multiple_of(i * tq, tq), kw)
        dk_acc[rows, :] += dk_t
        dv_acc[rows, :] += dv_t

        @pl.when(i == nt - 1)
        def _():
            pltpu.sync_copy(dk_acc, dk_hbm.at[kv])
            pltpu.sync_copy(dv_acc, dv_hbm.at[kv])

    def q_spec(width):
        return pl.BlockSpec((g, tq, width), lambda kv, i: (kv, i, 0))

    ka_spec = pl.BlockSpec((None, half, dh), lambda kv, i: (kv, i, 0))
    kb_spec = pl.BlockSpec((None, half, dh), lambda kv, i: (kv, i + 1, 0))
    b_spec = pl.BlockSpec((g, tq, kw), lambda kv, i: (kv, 0, 0))
    sk_spec = pl.BlockSpec((g, 1, LANES), lambda kv, i: (kv, 0, 0))

    def padded(t):
        return jnp.pad(t, ((0, 0), (C_BLOCK, C_BLOCK), (0, 0)))

    def run_fwd(q, kp, vp, tile, sink_b):
        return _pallas(
            fwd_body, name=name + "_fwd", grid=(C_KV_HEADS, nt),
            in_specs=[q_spec(dh), ka_spec, kb_spec, ka_spec, kb_spec, b_spec, sk_spec], out_specs=[q_spec(dh), q_spec(1)],
            out_shape=[jax.ShapeDtypeStruct((hq, s, dh), ACT_DTYPE), jax.ShapeDtypeStruct((hq, s, 1), F32)],
            compiler_params=_cparams("parallel", "parallel"))(q, kp, kp, vp, vp, tile, sink_b)

    @jax.custom_vjp
    def f(q, k, v, tile, sink_b):
        return run_fwd(q, padded(k), padded(v), tile, sink_b)[0]

    def fwd(q, k, v, tile, sink_b):
        kp, vp = padded(k), padded(v)
        o, lse = run_fwd(q, kp, vp, tile, sink_b)
        return o, (q, kp, vp, tile, sink_b, o, lse)

    def bwd(res, do):
        q, kp, vp, tile, sink_b, o, lse = res
        delta = _rowdot(o, do, name + "_delta")
        any_spec = pl.BlockSpec(memory_space=pl.ANY)
        acc = jax.ShapeDtypeStruct((C_KV_HEADS, s + 2 * C_BLOCK, dh), F32)
        dq, dtile, dsink, dkp, dvp = _pallas(
            bwd_body, name=name + "_bwd", grid=(C_KV_HEADS, nt),
            in_specs=[q_spec(dh), ka_spec, kb_spec, ka_spec, kb_spec, b_spec, sk_spec, q_spec(dh), q_spec(1), q_spec(1)],
            out_specs=[q_spec(dh), b_spec, sk_spec, any_spec, any_spec],
            out_shape=[jax.ShapeDtypeStruct((hq, s, dh), q.dtype), jax.ShapeDtypeStruct((hq, tq, kw), F32),
                       jax.ShapeDtypeStruct((hq, 1, LANES), F32), acc, acc],
            scratch_shapes=[pltpu.VMEM((s + 2 * C_BLOCK, dh), F32), pltpu.VMEM((s + 2 * C_BLOCK, dh), F32)],
            compiler_params=_cparams("parallel", "arbitrary"))(q, kp, kp, vp, vp, tile, sink_b, do, lse, delta)
        unpad = lambda t: t[:, C_BLOCK:-C_BLOCK].astype(kp.dtype)
        return dq, unpad(dkp), unpad(dvp), dtile, dsink

    f.defvjp(fwd, bwd)
    return f(q, k, v, tile, sink_b)


HG_PREP_ROWS = 256
HG_GROUP = 8
HG_INTRA_BLOCK = 256
HG_INTER_CHUNKS = 16
MLA_FWD_TQ, MLA_BWD_TQ, MLA_TK = 2048, 1024, 1024


def _hdot(a, b, dims):
    b16 = jnp.bfloat16
    hi = b.astype(b16)
    rest = b - hi.astype(F32)
    mid = rest.astype(b16)
    lo = (rest - mid.astype(F32)).astype(b16)
    a16 = a.astype(b16)
    dn = (dims, ((), ()))
    return (lax.dot_general(a16, hi, dn, preferred_element_type=F32) + lax.dot_general(a16, mid, dn, preferred_element_type=F32)
            + lax.dot_general(a16, lo, dn, preferred_element_type=F32))


def hgrn_prep(q, z, lb, reverse, name):
    n_hp, s, tc = q.shape
    tb = _tile(s, HG_PREP_ROWS)
    ncb = tb // B_CHUNK

    def chunk_matrices():
        r = lax.broadcasted_iota(jnp.int32, (tb, tb), 0)
        cc = lax.broadcasted_iota(jnp.int32, (tb, tb), 1)
        same = r // B_CHUNK == cc // B_CHUNK
        tri = (cc >= r) if reverse else (cc <= r)
        cum = jnp.where(jnp.logical_and(same, tri), 1.0, 0.0).astype(F32)
        every = jnp.where(same, 1.0, 0.0).astype(F32)
        pr = lax.broadcasted_iota(jnp.int32, (ncb, tb), 0)
        pc = lax.broadcasted_iota(jnp.int32, (ncb, tb), 1)
        per_chunk = jnp.where(pc // B_CHUNK == pr, 1.0, 0.0).astype(F32)
        return cum, every, per_chunk

    def gates(zv, lbv):
        e = jnp.exp(-jnp.abs(zv))
        big, small = 1.0 / (1.0 + e), e / (1.0 + e)
        sig = jnp.where(zv >= 0, big, small)
        nsig = jnp.where(zv >= 0, small, big)
        f = lbv + (1.0 - lbv) * sig
        return sig, nsig, f, jnp.log(jnp.maximum(f, TINY)), (1.0 - lbv) * nsig

    def fwd_body(q_ref, z_ref, lb_ref, qd_ref, ki_ref, ke_ref, dec_ref):
        cum, every, per_chunk = chunk_matrices()
        _, _, _, lf, key = gates(z_ref[...], lb_ref[...])
        b = _hdot(cum, lf, NN)
        tot = _hdot(every, lf, NN)
        qd_ref[...] = q_ref[...] * jnp.exp(b)
        ki_ref[...] = key * jnp.exp(-b)
        ke_ref[...] = key * jnp.exp(tot - b)
        dec_ref[...] = jnp.exp(_hdot(per_chunk, lf, NN))

    def bwd_body(q_ref, z_ref, lb_ref, dqd_ref, dki_ref, dke_ref, ddec_ref, dq_ref, dz_ref, dlb_ref):
        cum, every, per_chunk = chunk_matrices()
        lbv = lb_ref[...]
        sig, nsig, f, lf, key = gates(z_ref[...], lbv)
        b = _hdot(cum, lf, NN)
        tot = _hdot(every, lf, NN)
        e_b, e_nb, e_tb = jnp.exp(b), jnp.exp(-b), jnp.exp(tot - b)
        dqd, dki, dke = dqd_ref[...], dki_ref[...], dke_ref[...]
        dq_ref[...] = dqd * e_b
        dkey = dki * e_nb + dke * e_tb
        t_end = dke * key * e_tb
        db = dqd * q_ref[...] * e_b - dki * key * e_nb - t_end
        dtot = ddec_ref[...] * jnp.exp(_hdot(per_chunk, lf, NN)) + _hdot(per_chunk, t_end, NN)
        dlf = _hdot(cum, db, TN) + _hdot(per_chunk, dtot, TN)
        df = jnp.where(f > TINY, dlf / f, 0.0)
        one_m_lb = 1.0 - lbv
        dz_ref[...] = (df - dkey) * one_m_lb * sig * nsig
        dlb_part = jnp.sum(df * nsig - dkey * nsig, axis=0, keepdims=True)

        @pl.when(pl.program_id(1) == 0)
        def _():
            dlb_ref[...] = jnp.zeros_like(dlb_ref)

        dlb_ref[...] += dlb_part

    tok = pl.BlockSpec((None, tb, tc), lambda j, i: (j, i, 0))
    vec = pl.BlockSpec((None, 1, tc), lambda j, i: (j, 0, 0))
    chk = pl.BlockSpec((None, ncb, tc), lambda j, i: (j, i, 0))
    grid = (n_hp, s // tb)
    tok_shape = jax.ShapeDtypeStruct((n_hp, s, tc), F32)
    chk_shape = jax.ShapeDtypeStruct((n_hp, s // B_CHUNK, tc), F32)

    def run_fwd(q, z, lb):
        return _pallas(fwd_body, name=name + "_fwd", grid=grid, in_specs=[tok, tok, vec], out_specs=[tok, tok, tok, chk],
                       out_shape=[tok_shape, tok_shape, tok_shape, chk_shape],
                       compiler_params=_cparams("parallel", "parallel"))(q, z, lb)

    @jax.custom_vjp
    def f(q, z, lb):
        return tuple(run_fwd(q, z, lb))

    def fwd(q, z, lb):
        return tuple(run_fwd(q, z, lb)), (q, z, lb)

    def bwd(res, cts):
        q, z, lb = res
        dq, dz, dlb = _pallas(
            bwd_body, name=name + "_bwd", grid=grid, in_specs=[tok, tok, vec, tok, tok, tok, chk], out_specs=[tok, tok, vec],
            out_shape=[tok_shape, tok_shape, jax.ShapeDtypeStruct((n_hp, 1, tc), F32)],
            compiler_params=_cparams("parallel", "arbitrary"))(q, z, lb, *cts)
        return dq, dz, dlb

    f.defvjp(fwd, bwd)
    return f(q, z, lb)


def _pair_cols(ref, hh, width):
    return ref[:, hh * width:(hh + 1) * width]


def hgrn_intra(qd, ki, v, reverse, name):
    s = qd.shape[1]
    tb = _tile(s, HG_INTRA_BLOCK)
    wk, wv = HG_GROUP * B_DK, HG_GROUP * B_DV

    def mask():
        r = lax.broadcasted_iota(jnp.int32, (tb, tb), 0)
        c = lax.broadcasted_iota(jnp.int32, (tb, tb), 1)
        return jnp.logical_and(r // B_CHUNK == c // B_CHUNK, (c >= r) if reverse else (c <= r))

    def fwd_body(q_ref, k_ref, v_ref, o_ref):
        msk = mask()
        for hh in range(HG_GROUP):
            sc = jnp.where(msk, _dot(_pair_cols(q_ref, hh, B_DK), _pair_cols(k_ref, hh, B_DK), NT), 0.0)
            o_ref[:, hh * B_DV:(hh + 1) * B_DV] = _dot(sc, _pair_cols(v_ref, hh, B_DV), NN)

    def bwd_body(q_ref, k_ref, v_ref, do_ref, dq_ref, dk_ref, dv_ref):
        msk = mask()
        for hh in range(HG_GROUP):
            q, k = _pair_cols(q_ref, hh, B_DK), _pair_cols(k_ref, hh, B_DK)
            vv, do = _pair_cols(v_ref, hh, B_DV), _pair_cols(do_ref, hh, B_DV)
            sc = jnp.where(msk, _dot(q, k, NT), 0.0)
            ds = jnp.where(msk, _dot(do, vv, NT), 0.0)
            dq_ref[:, hh * B_DK:(hh + 1) * B_DK] = _dot(ds, k, NN)
            dk_ref[:, hh * B_DK:(hh + 1) * B_DK] = _dot(ds, q, TN)
            dv_ref[:, hh * B_DV:(hh + 1) * B_DV] = _dot(sc, do, TN)

    ks = pl.BlockSpec((None, tb, wk), lambda hp, i: (hp, i, 0))
    vs = pl.BlockSpec((None, tb, wv), lambda hp, i: (hp, i, 0))
    grid = (B_HEADS // HG_GROUP, s // tb)

    def run_fwd(qd, ki, v):
        return _pallas(fwd_body, name=name + "_fwd", grid=grid, in_specs=[ks, ks, vs], out_specs=vs,
                       out_shape=jax.ShapeDtypeStruct(v.shape, F32), compiler_params=_cparams("parallel", "parallel"))(qd, ki, v)

    @jax.custom_vjp
    def f(qd, ki, v):
        return run_fwd(qd, ki, v)

    def fwd(qd, ki, v):
        return run_fwd(qd, ki, v), (qd, ki, v)

    def bwd(res, do):
        qd, ki, v = res
        return tuple(_pallas(
            bwd_body, name=name + "_bwd", grid=grid, in_specs=[ks, ks, vs, vs], out_specs=[ks, ks, vs],
            out_shape=[jax.ShapeDtypeStruct(qd.shape, F32), jax.ShapeDtypeStruct(ki.shape, F32),
                       jax.ShapeDtypeStruct(v.shape, F32)],
            compiler_params=_cparams("parallel", "parallel"))(qd, ki, v, do))

    f.defvjp(fwd, bwd)
    return f(qd, ki, v)


def hgrn_inter(qd, ke, v, dec, reverse, name):
    s = qd.shape[1]
    nc = s // B_CHUNK
    cpb = HG_INTER_CHUNKS if nc % HG_INTER_CHUNKS == 0 else nc
    tb = cpb * B_CHUNK
    nblk = nc // cpb
    wk, wv = HG_GROUP * B_DK, HG_GROUP * B_DV
    n_hp = B_HEADS // HG_GROUP

    def rows(c):
        return pl.ds(c * B_CHUNK, B_CHUNK)

    def kcols(hh):
        return slice(hh * B_DK, (hh + 1) * B_DK)

    def vcols(hh):
        return slice(hh * B_DV, (hh + 1) * B_DV)

    def order(flip):
        return reversed(range(cpb)) if flip else range(cpb)

    def fwd_body(q_ref, k_ref, v_ref, dec_ref, o_ref, st_ref, state):
        @pl.when(pl.program_id(1) == 0)
        def _():
            state[...] = jnp.zeros_like(state)

        for c in order(reverse):
            for hh in range(HG_GROUP):
                st = state[hh]
                st_ref[c, hh] = st
                o_ref[rows(c), vcols(hh)] = _dot(q_ref[rows(c), kcols(hh)], st, NT)
                state[hh] = st * dec_ref[pl.ds(c, 1), kcols(hh)] + _dot(v_ref[rows(c), vcols(hh)], k_ref[rows(c), kcols(hh)], TN)

    def bwd_body(q_ref, k_ref, v_ref, dec_ref, st_ref, do_ref, dq_ref, dk_ref, dv_ref, ddec_ref, dstate):
        @pl.when(pl.program_id(1) == 0)
        def _():
            dstate[...] = jnp.zeros_like(dstate)

        for c in order(not reverse):
            for hh in range(HG_GROUP):
                dst = dstate[hh]
                st = st_ref[c, hh]
                do_c = do_ref[rows(c), vcols(hh)]
                dk_ref[rows(c), kcols(hh)] = _dot(v_ref[rows(c), vcols(hh)], dst, NN)
                dv_ref[rows(c), vcols(hh)] = _dot(k_ref[rows(c), kcols(hh)], dst, NT)
                ddec_ref[pl.ds(c, 1), kcols(hh)] = jnp.sum(dst * st, axis=0, keepdims=True)
                dq_ref[rows(c), kcols(hh)] = _dot(do_c, st, NN)
                dstate[hh] = dst * dec_ref[pl.ds(c, 1), kcols(hh)] + _dot(do_c, q_ref[rows(c), kcols(hh)], TN)

    def specs(flip):
        blk = (lambda i: nblk - 1 - i) if flip else (lambda i: i)
        tok_k = pl.BlockSpec((None, tb, wk), lambda hp, i: (hp, blk(i), 0))
        tok_v = pl.BlockSpec((None, tb, wv), lambda hp, i: (hp, blk(i), 0))
        chk = pl.BlockSpec((None, cpb, wk), lambda hp, i: (hp, blk(i), 0))
        sts = pl.BlockSpec((None, cpb, HG_GROUP, B_DV, B_DK), lambda hp, i: (hp, blk(i), 0, 0, 0))
        return tok_k, tok_v, chk, sts

    scratch = [pltpu.VMEM((HG_GROUP, B_DV, B_DK), F32)]

    def run_fwd(qd, ke, v, dec):
        tok_k, tok_v, chk, sts = specs(reverse)
        return _pallas(
            fwd_body, name=name + "_fwd", grid=(n_hp, nblk), in_specs=[tok_k, tok_k, tok_v, chk], out_specs=[tok_v, sts],
            out_shape=[jax.ShapeDtypeStruct(v.shape, F32), jax.ShapeDtypeStruct((n_hp, nc, HG_GROUP, B_DV, B_DK), F32)],
            scratch_shapes=scratch, compiler_params=_cparams("parallel", "arbitrary"))(qd, ke, v, dec)

    @jax.custom_vjp
    def f(qd, ke, v, dec):
        return run_fwd(qd, ke, v, dec)[0]

    def fwd(qd, ke, v, dec):
        o, st = run_fwd(qd, ke, v, dec)
        return o, (qd, ke, v, dec, st)

    def bwd(res, do):
        qd, ke, v, dec, st = res
        tok_k, tok_v, chk, sts = specs(not reverse)
        return tuple(_pallas(
            bwd_body, name=name + "_bwd", grid=(n_hp, nblk), in_specs=[tok_k, tok_k, tok_v, chk, sts, tok_v],
            out_specs=[tok_k, tok_k, tok_v, chk],
            out_shape=[jax.ShapeDtypeStruct(qd.shape, F32), jax.ShapeDtypeStruct(ke.shape, F32),
                       jax.ShapeDtypeStruct(v.shape, F32), jax.ShapeDtypeStruct(dec.shape, F32)],
            scratch_shapes=scratch, compiler_params=_cparams("parallel", "arbitrary"))(qd, ke, v, dec, st, do))

    f.defvjp(fwd, bwd)
    return f(qd, ke, v, dec)


def loss_head(y, target, name="loss"):
    s, d = y.shape
    tr = _row_tile(s, d)

    def body(y_ref, t_ref, o_ref):
        @pl.when(pl.program_id(0) == 0)
        def _():
            o_ref[...] = jnp.zeros_like(o_ref)

        e = y_ref[...] - t_ref[...]
        part = jnp.sum(jnp.sum(e * e, axis=-1, keepdims=True), axis=0, keepdims=True) * (0.5 / d)
        o_ref[...] += jnp.broadcast_to(part, o_ref.shape)

    spec = pl.BlockSpec((tr, d), lambda i: (i, 0))

    def run(y, t):
        out = _pallas(body, name=name, grid=(s // tr,), in_specs=[spec, spec],
                      out_specs=pl.BlockSpec((SUBLANES, LANES), lambda i: (0, 0)),
                      out_shape=jax.ShapeDtypeStruct((SUBLANES, LANES), F32), compiler_params=_cparams("arbitrary"))(y, t)
        return out[0, 0]

    @jax.custom_vjp
    def f(y, t):
        return run(y, t)

    def fwd(y, t):
        return run(y, t), (y, t)

    def bwd(res, g):
        y, t = res
        dy = g * (y - t) * (1.0 / d)
        return dy, -dy

    f.defvjp(fwd, bwd)
    return f(y, target)


def _mesh_pos():
    return lax.axis_index("x"), lax.axis_index("y"), lax.axis_index("c")


def all_gather_shards(shards):
    n = len(shards)

    def body(*refs):
        ins, outs = refs[:n], refs[n:2 * n]
        send_sems, recv_sems, local_sems = refs[2 * n:]
        x, y, c = _mesh_pos()
        me, sibling = (x, y, c), (x, y, 1 - c)
        chips = [(1 - x, y), (x, 1 - y), (1 - x, 1 - y)]

        def slot(t, px, py, pc):
            return outs[t].at[4 * px + 2 * py + pc]

        def copy(t, k, block, to, src=None):
            return pltpu.make_async_remote_copy(
                src_ref=slot(t, *block) if src is None else src, dst_ref=slot(t, *block), send_sem=send_sems.at[t, k],
                recv_sem=recv_sems.at[t, k], device_id=to, device_id_type=pl.DeviceIdType.MESH)

        mine = [pltpu.make_async_copy(ins[t], slot(t, *me), local_sems.at[t]) for t in range(n)]
        for cp in mine:
            cp.start()
        first = []
        for t in range(n):
            first.append(copy(t, 0, me, sibling, src=ins[t]))
            first += [copy(t, 1 + j, me, (*chip, c), src=ins[t]) for j, chip in enumerate(chips)]
        for cp in first:
            cp.start()
        passed = []
        for j, chip in enumerate(chips):
            for t in range(n):
                copy(t, 1 + j, (*chip, c), me).wait_recv()
                cp = copy(t, 4 + j, (*chip, c), sibling)
                cp.start()
                passed.append(cp)
        for t in range(n):
            copy(t, 0, sibling, me).wait_recv()
            for j, chip in enumerate(chips):
                copy(t, 4 + j, (*chip, 1 - c), me).wait_recv()
        for cp in first + passed:
            cp.wait_send()
        for cp in mine:
            cp.wait()

    any_spec = pl.BlockSpec(memory_space=pl.ANY)
    return _pallas(
        body, name="all_gather_weights", out_shape=[jax.ShapeDtypeStruct((N_DEV, *s.shape), s.dtype) for s in shards],
        in_specs=[any_spec] * n, out_specs=[any_spec] * n,
        scratch_shapes=[pltpu.SemaphoreType.DMA((n, 7)), pltpu.SemaphoreType.DMA((n, 7)), pltpu.SemaphoreType.DMA((n,))],
    )(*shards)


def all_to_all_blocks(stacks):
    n = len(stacks)

    def body(*refs):
        ins, outs = refs[:n], refs[n:2 * n]
        send_sems, recv_sems, local_sems = refs[2 * n:]
        x, y, c = _mesh_pos()
        me = 4 * x + 2 * y + c
        mine = [pltpu.make_async_copy(ins[t].at[me], outs[t].at[me], local_sems.at[t]) for t in range(n)]
        for cp in mine:
            cp.start()
        copies = []
        for k in range(1, N_DEV):
            px = 1 - x if k & 4 else x
            py = 1 - y if k & 2 else y
            pc = 1 - c if k & 1 else c
            for t in range(n):
                cp = pltpu.make_async_remote_copy(
                    src_ref=ins[t].at[4 * px + 2 * py + pc], dst_ref=outs[t].at[me], send_sem=send_sems.at[t, k - 1],
                    recv_sem=recv_sems.at[t, k - 1], device_id=(px, py, pc), device_id_type=pl.DeviceIdType.MESH)
                cp.start()
                copies.append(cp)
        for cp in copies:
            cp.wait_recv()
        for cp in copies:
            cp.wait_send()
        for cp in mine:
            cp.wait()

    any_spec = pl.BlockSpec(memory_space=pl.ANY)
    return _pallas(
        body, name="all_to_all_grads", out_shape=[jax.ShapeDtypeStruct(s.shape, s.dtype) for s in stacks],
        in_specs=[any_spec] * n, out_specs=[any_spec] * n,
        scratch_shapes=[pltpu.SemaphoreType.DMA((n, 7)), pltpu.SemaphoreType.DMA((n, 7)), pltpu.SemaphoreType.DMA((n,))],
    )(*stacks)


def all_gather_small(v):
    r, w = v.shape

    def body(x_ref, out_ref, send_sems, recv_sems):
        x, y, c = _mesh_pos()
        me = 4 * x + 2 * y + c
        copies = []
        for k in range(1, N_DEV):
            px = 1 - x if k & 4 else x
            py = 1 - y if k & 2 else y
            pc = 1 - c if k & 1 else c
            cp = pltpu.make_async_remote_copy(
                src_ref=x_ref, dst_ref=out_ref.at[me], send_sem=send_sems.at[k - 1], recv_sem=recv_sems.at[k - 1],
                device_id=(px, py, pc), device_id_type=pl.DeviceIdType.MESH)
            cp.start()
            copies.append(cp)
        out_ref[me] = x_ref[...]
        for cp in copies:
            cp.wait_recv()
        for cp in copies:
            cp.wait_send()

    vmem = pl.BlockSpec(memory_space=pltpu.VMEM)
    return _pallas(
        body, name="all_gather_small", out_shape=jax.ShapeDtypeStruct((N_DEV, r, w), v.dtype), in_specs=[vmem],
        out_specs=vmem, scratch_shapes=[pltpu.SemaphoreType.DMA((7,)), pltpu.SemaphoreType.DMA((7,))],
    )(v)


def adamw_rows(parts, w, m, v, name):
    n, r, lanes = parts.shape
    tr = _tile(r, max(SUBLANES, (256 * 1024) // lanes), SUBLANES)
    c1 = 1.0 / (1.0 - ADAM_B1 ** ADAM_STEP)
    c2 = 1.0 / (1.0 - ADAM_B2 ** ADAM_STEP)

    def body(p_ref, w_ref, m_ref, v_ref, g_ref, d_ref, nm_ref, nv_ref):
        g = p_ref[0].astype(F32)
        for j in range(1, n):
            g = g + p_ref[j].astype(F32)
        nm = ADAM_B1 * m_ref[...] + (1.0 - ADAM_B1) * g
        nv = ADAM_B2 * v_ref[...] + (1.0 - ADAM_B2) * (g * g)
        g_ref[...] = g
        nm_ref[...] = nm
        nv_ref[...] = nv
        d_ref[...] = -ADAM_LR * ((nm * c1) / (jnp.sqrt(nv * c2) + ADAM_EPS) + ADAM_WD * w_ref[...])

    row = pl.BlockSpec((tr, lanes), lambda i: (i, 0))
    out = jax.ShapeDtypeStruct((r, lanes), F32)
    return _pallas(body, name=name, grid=(r // tr,), in_specs=[pl.BlockSpec((n, tr, lanes), lambda i: (0, i, 0)), row, row, row],
                   out_specs=[row, row, row, row], out_shape=[out, out, out, out], compiler_params=_cparams("parallel"))(
        parts, w, m, v)


def _padded(n):
    return -(-n // PACK_QUANTUM) * PACK_QUANTUM


def _pack(pieces, total_rows=None):
    flat = []
    for p in pieces:
        p = p.reshape(-1).astype(F32)
        flat.append(jnp.pad(p, (0, _padded(p.size) - p.size)))
    out = jnp.concatenate(flat).reshape(-1, LANES)
    if total_rows is not None and out.shape[0] != total_rows:
        out = jnp.pad(out, ((0, total_rows - out.shape[0]), (0, 0)))
    return out


def _pack_rows(sizes):
    rows = sum(_padded(n) for n in sizes) // LANES
    return -(-rows // PACK_ROW_TILE) * PACK_ROW_TILE


def _unpack(rows, shapes):
    lead = rows.shape[:-2]
    flat = rows.reshape(*lead, -1)
    out, off = [], 0
    for shp in shapes:
        n = int(np.prod(shp))
        out.append(flat[..., off:off + n].reshape(*lead, *shp))
        off += _padded(n)
    return out


def _shards_to_full(stacked, axis):
    moved = jnp.moveaxis(stacked, 0, axis)
    shp = list(stacked.shape[1:])
    shp[axis] *= N_DEV
    return moved.reshape(shp)


def _full_to_shards(full, axis):
    shp = list(full.shape)
    shp[axis:axis + 1] = [N_DEV, shp[axis] // N_DEV]
    return jnp.moveaxis(full.reshape(shp), axis, 0)


def _heads(t, n, d, dtype=ACT_DTYPE):
    return jnp.transpose(t.reshape(t.shape[0], n, d), (1, 0, 2)).astype(dtype)


def _unheads(t):
    return jnp.transpose(t, (1, 0, 2)).reshape(t.shape[1], -1)


def _rope_tables(s):
    half = A_ROPE // 2
    inv = ROPE_THETA ** (-jnp.arange(half, dtype=F32) / half)
    ang = jnp.arange(s, dtype=jnp.int32).astype(F32)[:, None] * inv[None, :]
    return jnp.cos(ang), jnp.sin(ang)


def _rope(t, cos, sin):
    half = A_ROPE // 2
    t1, t2 = t[..., :half], t[..., half:]
    c, sn = cos[:, None, :], sin[:, None, :]
    return jnp.concatenate([t1 * c - t2 * sn, t1 * sn + t2 * c], axis=-1)


def _t5_bucket(rel):
    nb = REL_BUCKETS // 2
    max_exact = nb // 2
    ret = (rel > 0).astype(jnp.int32) * nb
    n = jnp.abs(rel)
    large = max_exact + (jnp.log(jnp.maximum(n, 1).astype(F32) / max_exact)
                         / math.log(REL_MAX_DIST / max_exact) * (nb - max_exact)).astype(jnp.int32)
    large = jnp.minimum(large, nb - 1)
    return ret + jnp.where(n < max_exact, n, large)


def _window_bias(rel_bias):
    span = 3 * C_BLOCK
    rel = jnp.arange(span)[None, :] - C_BLOCK - jnp.arange(C_BLOCK)[:, None]
    onehot = (_t5_bucket(rel)[..., None] == jnp.arange(REL_BUCKETS)).astype(F32)
    bias = jnp.einsum("qkb,bh->hqk", onehot, rel_bias.astype(F32), precision=lax.Precision.HIGHEST)
    bias = jnp.where((jnp.abs(rel) <= C_WINDOW)[None], bias, NEG)
    return jnp.transpose(bias.reshape(C_HEADS, C_BLOCK, 3, C_BLOCK), (0, 2, 1, 3))


def _mla(cq, ckv, kr, gq, gkv, wuq, wukv, cos, sin):
    s = cq.shape[0]
    q = linear(rmsnorm(cq, gq, ACT_DTYPE, "rms_cq"), wuq, name="a_wuq").reshape(s, A_HEADS, A_NOPE + A_ROPE)
    q = jnp.concatenate([q[..., :A_NOPE], _rope(q[..., A_NOPE:], cos, sin)], axis=-1)
    kv = linear(rmsnorm(ckv, gkv, ACT_DTYPE, "rms_ckv"), wukv, name="a_wukv").reshape(s, A_HEADS, A_NOPE + A_V)
    k_rope = jnp.broadcast_to(_rope(kr[:, None, :], cos, sin), (s, A_HEADS, A_ROPE))
    k = jnp.concatenate([kv[..., :A_NOPE], k_rope], axis=-1)
    v = kv[..., A_NOPE:]
    return mla_attention(q, k, v, (A_NOPE + A_ROPE) ** -0.5)


def _hgrn2(q, f_fwd, f_bwd, i, g, lb_fwd, lb_bwd, g_out):
    s = q.shape[0]
    n_hp = B_HEADS // HG_GROUP
    pairs = lambda t: jnp.transpose(t.reshape(s, n_hp, -1), (1, 0, 2))
    qp, vp = pairs(q), pairs(i)
    o = None
    for z, lb, rev, tag in ((f_fwd, lb_fwd, False, "hgf"), (f_bwd, lb_bwd, True, "hgb")):
        qd, ki, ke, dec = hgrn_prep(qp, pairs(z), lb.astype(F32).reshape(n_hp, 1, -1), rev, tag + "_prep")
        part = hgrn_intra(qd, ki, vp, rev, tag + "_intra") + hgrn_inter(qd, ke, vp, dec, rev, tag + "_inter")
        o = part if o is None else o + part
    o = jnp.transpose(o, (1, 0, 2)).reshape(s * B_HEADS, B_DV)
    o = rmsnorm(o, g_out, F32, "rms_hg").reshape(s, B_HEADS * B_DV)
    return o * jax.nn.silu(g)


def _cross(x, h, mem_n, wq, wkv, wo):
    q = linear(h, wq, name="x_wq").reshape(h.shape[0], X_HEADS, X_DH)
    kv = linear(mem_n, wkv, name="x_wkv").reshape(mem_n.shape[0], 2, X_HEADS, X_DH)
    o_t = mla_attention(q, kv[:, 0], kv[:, 1], X_DH ** -0.5, name="cross")
    return linear_res(x, o_t, wo, name="x_wo", a_transposed=True)


def _pad_w_in(w):
    cut = A_Q_RANK + A_KV_RANK + A_ROPE
    return jnp.concatenate([w[:, :cut], jnp.zeros((w.shape[0], KR_PAD), w.dtype), w[:, cut:]], axis=1)


def _model_loss(p, x, mem, target):
    s = x.shape[0]
    cos, sin = _rope_tables(s)
    sm = jax.nn.softmax(p["b_lb"].astype(F32), axis=1)
    lower_bounds = jnp.cumsum(sm, axis=1) - sm[:, :1]
    bias = _window_bias(p["rel_bias"])
    for l in range(DEPTH):
        h = rmsnorm(x, p["g_mix"][l], ACT_DTYPE, "rms_mix")
        z = linear(h, _pad_w_in(p["w_in"][l]), name="w_in")
        parts, start = [], 0
        for width in IN_SPLITS_PADDED:
            parts.append(z[:, start:start + width])
            start += width
        a_cq, a_ckv, a_kr, b_q, b_ff, b_fb, b_i, b_g, c_q, c_k, c_v, gate_a, gate_b, gate_c = parts
        y_a = _mla(a_cq, a_ckv, a_kr[:, :A_ROPE], p["a_gq"][l], p["a_gkv"][l], p["a_wuq"][l], p["a_wukv"][l], cos, sin)
        y_b = _hgrn2(b_q, b_ff, b_fb, b_i, b_g, lower_bounds[0, l], lower_bounds[1, l], p["b_gout"][l])
        y_c = _unheads(window_attention(_heads(c_q, C_HEADS, C_DH, F32), _heads(c_k, C_KV_HEADS, C_DH, F32),
                                        _heads(c_v, C_KV_HEADS, C_DH, F32), bias, p["c_sink"][l]))
        x = gated_merge_out(x, y_a, y_b, y_c, gate_a, gate_b, gate_c, p["w_br_a"][l], p["w_br_b"][l], p["w_br_c"][l],
                            p["w_out"][l])
        h = rmsnorm(x, p["g_x"][l], ACT_DTYPE, "rms_x")
        x = _cross(x, h, rmsnorm(mem, p["g_mem"][l], ACT_DTYPE, "rms_mem"), p["x_wq"][l], p["x_wkv"][l], p["x_wo"][l])
        h = rmsnorm(x, p["g_ffn"][l], ACT_DTYPE, "rms_ffn")
        x = swiglu_ffn(x, h, p["f_w1"][l], p["f_w3"][l], p["f_w2"][l])
    y = rmsnorm(x, p["g_final"], F32, "rms_final")
    return loss_head(y, target)


def kernel(x, mem, w_in, g_mix, a_gq, a_gkv, a_wuq, a_wukv, b_lb, b_gout, c_sink, rel_bias, w_br_a, w_br_b, w_br_c, w_out, g_x, g_mem, x_wq, x_wkv, x_wo, g_ffn, f_w1, f_w3, f_w2, g_final, loss_target, m_w_in, m_g_mix, m_a_gq, m_a_gkv, m_a_wuq, m_a_wukv, m_b_lb, m_b_gout, m_c_sink, m_rel_bias, m_w_br_a, m_w_br_b, m_w_br_c, m_w_out, m_g_x, m_g_mem, m_x_wq, m_x_wkv, m_x_wo, m_g_ffn, m_f_w1, m_f_w3, m_f_w2, m_g_final, v_w_in, v_g_mix, v_a_gq, v_a_gkv, v_a_wuq, v_a_wukv, v_b_lb, v_b_gout, v_c_sink, v_rel_bias, v_w_br_a, v_w_br_b, v_w_br_c, v_w_out, v_g_x, v_g_mem, v_x_wq, v_x_wkv, v_x_wo, v_g_ffn, v_f_w1, v_f_w3, v_f_w2, v_g_final):
    given = dict(locals())
    w = {n: given[n] for n in WEIGHT_ORDER}
    m = {n: given["m_" + n] for n in WEIGHT_ORDER}
    v = {n: given["v_" + n] for n in WEIGHT_ORDER}
    sh_names = [n for n, _ in SHARDED]
    rep_shapes = [w[n].shape for n in REPLICATED] + [(1,)]
    rep_rows = _pack_rows([int(np.prod(s)) for s in rep_shapes])

    wire = [w[n] if n in ELEMENTWISE_SHARDED else w[n].astype(MXU_DTYPE) for n in sh_names]
    gathered = all_gather_shards(wire)
    full = {n: _shards_to_full(t, ax).astype(F32) for (n, ax), t in zip(SHARDED, gathered)}
    full.update({n: w[n] for n in REPLICATED})

    loss, (grad_full, grad_x) = jax.value_and_grad(_model_loss, argnums=(0, 1))(full, x[0], mem[0], loss_target[0])

    received = all_to_all_blocks([_full_to_shards(grad_full[n], ax).astype(GRAD_WIRE_DTYPE) for n, ax in SHARDED])
    g_sh, d_sh, nm_sh, nv_sh = {}, {}, {}, {}
    for n, got in zip(sh_names, received):
        shp = w[n].shape
        rows = lambda t: t.reshape(-1, shp[-1])
        outs = adamw_rows(got.reshape(N_DEV, -1, shp[-1]), rows(w[n]), rows(m[n]), rows(v[n]), "adamw_" + n)
        g_sh[n], d_sh[n], nm_sh[n], nv_sh[n] = [o.reshape(shp) for o in outs]

    mine = _pack([grad_full[n] for n in REPLICATED] + [loss.reshape(1)], rep_rows)
    everyone = all_gather_small(mine)
    rep_w = [w[n] for n in REPLICATED] + [jnp.zeros((1,), F32)]
    outs = adamw_rows(everyone, _pack(rep_w, rep_rows), _pack([m[n] for n in REPLICATED] + [jnp.zeros((1,), F32)], rep_rows),
                      _pack([v[n] for n in REPLICATED] + [jnp.ones((1,), F32)], rep_rows), "adamw_replicated")
    rep_names = list(REPLICATED) + ["loss"]
    g_rp, d_rp, nm_rp, nv_rp = [dict(zip(rep_names, _unpack(o, rep_shapes))) for o in outs]

    def pick(sharded, replicated, n):
        return sharded[n] if n in sharded else replicated[n]

    return (g_rp["loss"].reshape(()), grad_x[None],
            *[pick(g_sh, g_rp, n) for n in WEIGHT_ORDER], *[pick(d_sh, d_rp, n) for n in WEIGHT_ORDER],
            *[pick(nm_sh, nm_rp, n) for n in WEIGHT_ORDER], *[pick(nv_sh, nv_rp, n) for n in WEIGHT_ORDER])
```

```python
import functools
import math

import jax
import jax.numpy as jnp
import numpy as np
from jax import lax
from jax.experimental import pallas as pl
from jax.experimental.pallas import tpu as pltpu

F32 = jnp.float32
MXU_DTYPE = jnp.bfloat16
ACT_DTYPE = jnp.bfloat16
GRAD_WIRE_DTYPE = jnp.bfloat16

V7X_VMEM_LIMIT_BYTES = 56 * 1024 * 1024
LANES = 128
SUBLANES = 8

N_DEV = 8
D_MODEL = 1024
DEPTH = 2
EPS = 1e-6
TINY = 1e-30
NEG = -1e30

A_HEADS, A_NOPE, A_ROPE, A_V, A_Q_RANK, A_KV_RANK = 8, 64, 32, 64, 384, 256
ROPE_THETA = 10000.0
B_HEADS, B_DK, B_DV, B_CHUNK = 8, 128, 64, 16
C_HEADS, C_KV_HEADS, C_DH, C_WINDOW, C_BLOCK = 8, 2, 64, 128, 128
REL_BUCKETS, REL_MAX_DIST = 32, 128
X_HEADS, X_DH = 4, 256
D_FF = 2816
IN_SPLITS = (A_Q_RANK, A_KV_RANK, A_ROPE, 1024, 1024, 1024, 512, 512, 512, 128, 128, 1024, 1024, 1024)
IN_WIDTH = sum(IN_SPLITS)
KR_PAD = LANES - A_ROPE
IN_SPLITS_PADDED = (A_Q_RANK, A_KV_RANK, LANES, 1024, 1024, 1024, 512, 512, 512, 128, 128, 1024, 1024, 1024)

ADAM_LR, ADAM_B1, ADAM_B2, ADAM_EPS, ADAM_WD, ADAM_STEP = 0.001, 0.9, 0.999, 1e-08, 0.01, 10

SHARDED = (("w_in", 2), ("a_wuq", 2), ("a_wukv", 2), ("b_lb", 2), ("w_br_a", 2), ("w_br_b", 2), ("w_br_c", 2),
           ("w_out", 1), ("x_wq", 1), ("x_wkv", 2), ("x_wo", 1), ("f_w1", 2), ("f_w3", 2), ("f_w2", 1))
ELEMENTWISE_SHARDED = ("b_lb",)
REPLICATED = ("g_mix", "a_gq", "a_gkv", "b_gout", "c_sink", "rel_bias", "g_x", "g_mem", "g_ffn", "g_final")
WEIGHT_ORDER = ("w_in", "g_mix", "a_gq", "a_gkv", "a_wuq", "a_wukv", "b_lb", "b_gout", "c_sink", "rel_bias", "w_br_a",
                "w_br_b", "w_br_c", "w_out", "g_x", "g_mem", "x_wq", "x_wkv", "x_wo", "g_ffn", "f_w1", "f_w3", "f_w2",
                "g_final")
PACK_QUANTUM = SUBLANES * LANES
PACK_ROW_TILE = 512


def _pallas(body, **kw):
    return pl.pallas_call(body, **kw)


def _cparams(*sem):
    return pltpu.CompilerParams(dimension_semantics=sem, vmem_limit_bytes=V7X_VMEM_LIMIT_BYTES)


def _tile(n, target, mult=LANES):
    t = (min(target, n) // mult) * mult
    while t >= mult:
        if n % t == 0:
            return t
        t -= mult
    return n


def _dot(a, b, dims):
    return lax.dot_general(a.astype(MXU_DTYPE), b.astype(MXU_DTYPE), (dims, ((), ())), preferred_element_type=F32)


NN = ((1,), (0,))
NT = ((1,), (1,))
TN = ((0,), (0,))


MM_VMEM_BUDGET_BYTES = 40 * 1024 * 1024
MM_MAX_TILE = 4352
MM_MAX_ROW_TILE = 2048
MM_HBM_BYTES_PER_S = 2.5e12
MM_STEP_S = 0.4e-6
MM_DMA_ROW_OVERHEAD_BYTES = 512.0


def _tile_options(n, cap):
    out = [t for t in range(LANES, min(n, cap) + 1, LANES) if n % t == 0]
    if n <= cap and n not in out:
        out.append(n)
    return out or [n]


@functools.lru_cache(maxsize=None)
def _mm_plan(m, n, k, ta, tb, a_bytes, b_bytes, o_bytes):
    best = None
    for tk in _tile_options(k, MM_MAX_TILE):
        nk = k // tk
        for tn in _tile_options(n, MM_MAX_TILE):
            for tm in _tile_options(m, MM_MAX_ROW_TILE):
                vmem = 2 * (tm * tk * a_bytes + tk * tn * b_bytes + tm * tn * o_bytes) + tm * tn * 4
                vmem += (tm * tk * 2 if a_bytes == 4 else 0) + (tk * tn * 2 if b_bytes == 4 else 0)
                if vmem > MM_VMEM_BUDGET_BYTES:
                    continue

                def eff(elems, nbytes):
                    return (elems * nbytes) / (elems * nbytes + MM_DMA_ROW_OVERHEAD_BYTES)

                ea, eb, eo = eff(tm if ta else tk, a_bytes), eff(tk if tb else tn, b_bytes), eff(tn, o_bytes)
                for order in ("mn", "nm"):
                    if nk == 1 and order == "nm":
                        a_tr, b_tr = m * k * a_bytes * (n // tn), k * n * b_bytes
                    elif nk == 1:
                        a_tr, b_tr = m * k * a_bytes, k * n * b_bytes * (m // tm)
                    else:
                        a_tr, b_tr = m * k * a_bytes * (n // tn), k * n * b_bytes * (m // tm)
                    steps = (m // tm) * (n // tn) * nk
                    cost = (a_tr / ea + b_tr / eb + m * n * o_bytes / eo) / MM_HBM_BYTES_PER_S + steps * MM_STEP_S
                    if best is None or cost < best[0]:
                        best = (cost, tm, tn, tk, order)
    assert best is not None, (m, n, k)
    return best[1:]


def _mm(a, b, ta=False, tb=False, out_dtype=F32, name="mm", res=None):
    m, k = (a.shape[1], a.shape[0]) if ta else a.shape
    kb, n = (b.shape[1], b.shape[0]) if tb else b.shape
    assert k == kb, (a.shape, b.shape, ta, tb)
    tm, tn, tk, order = _mm_plan(m, n, k, ta, tb, a.dtype.itemsize, b.dtype.itemsize, jnp.dtype(out_dtype).itemsize)
    nk = k // tk
    dims = ((0 if ta else 1,), (1 if tb else 0,))
    out_shape = jax.ShapeDtypeStruct((m, n), out_dtype)

    assert res is None or nk == 1, (name, k, tk)
    if nk == 1:
        def body(a_ref, b_ref, *rest):
            acc = _dot(a_ref[...], b_ref[...], dims)
            if res is not None:
                acc = rest[0][...] + acc
            rest[-1][...] = acc.astype(rest[-1].dtype)

        if order == "nm":
            mi, ni = (lambda j, i: i), (lambda j, i: j)
            grid = (n // tn, m // tm)
        else:
            mi, ni = (lambda i, j: i), (lambda i, j: j)
            grid = (m // tm, n // tn)
        a_spec = pl.BlockSpec((tk, tm), lambda p, q: (0, mi(p, q))) if ta else pl.BlockSpec((tm, tk), lambda p, q: (mi(p, q), 0))
        b_spec = pl.BlockSpec((tn, tk), lambda p, q: (ni(p, q), 0)) if tb else pl.BlockSpec((tk, tn), lambda p, q: (0, ni(p, q)))
        o_spec = pl.BlockSpec((tm, tn), lambda p, q: (mi(p, q), ni(p, q)))
        extra = [] if res is None else [res]
        return _pallas(body, name=name, grid=grid, in_specs=[a_spec, b_spec] + [o_spec] * len(extra), out_specs=o_spec,
                       out_shape=out_shape, compiler_params=_cparams("parallel", "parallel"))(a, b, *extra)

    direct = jnp.dtype(out_dtype) == jnp.dtype(F32)

    def body(a_ref, b_ref, o_ref, *scratch):
        acc_ref = o_ref if direct else scratch[0]
        kk = pl.program_id(2)

        @pl.when(kk == 0)
        def _():
            acc_ref[...] = jnp.zeros_like(acc_ref)

        acc_ref[...] += _dot(a_ref[...], b_ref[...], dims)

        if not direct:
            @pl.when(kk == nk - 1)
            def _():
                o_ref[...] = acc_ref[...].astype(o_ref.dtype)

    a_spec = pl.BlockSpec((tk, tm), lambda i, j, kk: (kk, i)) if ta else pl.BlockSpec((tm, tk), lambda i, j, kk: (i, kk))
    b_spec = pl.BlockSpec((tn, tk), lambda i, j, kk: (j, kk)) if tb else pl.BlockSpec((tk, tn), lambda i, j, kk: (kk, j))
    return _pallas(
        body, name=name, grid=(m // tm, n // tn, nk), in_specs=[a_spec, b_spec],
        out_specs=pl.BlockSpec((tm, tn), lambda i, j, kk: (i, j)), out_shape=out_shape,
        scratch_shapes=[] if direct else [pltpu.VMEM((tm, tn), F32)],
        compiler_params=_cparams("parallel", "parallel", "arbitrary"),
    )(a, b)


def linear(a, w, out_dtype=F32, name="lin"):
    @jax.custom_vjp
    def f(a, w):
        return _mm(a.astype(ACT_DTYPE), w.astype(MXU_DTYPE), out_dtype=out_dtype, name=name + "_fwd")

    def fwd(a, w):
        ab, wb = a.astype(ACT_DTYPE), w.astype(MXU_DTYPE)
        return _mm(ab, wb, out_dtype=out_dtype, name=name + "_fwd"), (ab, wb, jnp.zeros((0,), a.dtype))

    def bwd(res, g):
        ab, wb, like_a = res
        gb = g.astype(ACT_DTYPE)
        da = _mm(gb, wb, tb=True, out_dtype=like_a.dtype, name=name + "_dx")
        dw = _mm(ab, gb, ta=True, out_dtype=F32, name=name + "_dw")
        return da, dw

    f.defvjp(fwd, bwd)
    return f(a, w)


def linear_res(x, a, w, name="lin", a_transposed=False):
    @jax.custom_vjp
    def f(x, a, w):
        return _mm(a.astype(ACT_DTYPE), w.astype(MXU_DTYPE), ta=a_transposed, name=name + "_fwd", res=x)

    def fwd(x, a, w):
        ab, wb = a.astype(ACT_DTYPE), w.astype(MXU_DTYPE)
        return _mm(ab, wb, ta=a_transposed, name=name + "_fwd", res=x), (ab, wb, jnp.zeros((0,), a.dtype))

    def bwd(res, g):
        ab, wb, like_a = res
        gb = g.astype(ACT_DTYPE)
        if a_transposed:
            da = _mm(wb, gb, tb=True, out_dtype=like_a.dtype, name=name + "_dx")
        else:
            da = _mm(gb, wb, tb=True, out_dtype=like_a.dtype, name=name + "_dx")
        dw = _mm(ab, gb, ta=not a_transposed, out_dtype=F32, name=name + "_dw")
        return g, da, dw

    f.defvjp(fwd, bwd)
    return f(x, a, w)


FFN_ROW_TILE = 512
FFN_COL_TILE = 1408


def _sigmoid(a):
    return 1.0 / (1.0 + jnp.exp(-a))


def swiglu_ffn(x, h, w1, w3, w2, name="ffn"):
    m, d = h.shape
    f_dim = w1.shape[1]
    tm, tn = _tile(m, FFN_ROW_TILE), _tile(f_dim, FFN_COL_TILE)

    def up_body(h_ref, w1_ref, w3_ref, t_ref, a_ref, b_ref):
        hv = h_ref[...]
        a = _dot(hv, w1_ref[...], NN)
        b = _dot(hv, w3_ref[...], NN)
        a_ref[...] = a
        b_ref[...] = b
        t_ref[...] = (a * _sigmoid(a) * b).astype(t_ref.dtype)

    def dt_body(g_ref, w2_ref, a_ref, b_ref, da_ref, db_ref):
        dt = _dot(g_ref[...], w2_ref[...], NT)
        a, b = a_ref[...], b_ref[...]
        sg = _sigmoid(a)
        da_ref[...] = (dt * b * (sg * (1.0 + a * (1.0 - sg)))).astype(da_ref.dtype)
        db_ref[...] = (dt * (a * sg)).astype(db_ref.dtype)

    row = pl.BlockSpec((tm, d), lambda j, i: (i, 0))
    w_up = pl.BlockSpec((d, tn), lambda j, i: (0, j))
    w_dn = pl.BlockSpec((tn, d), lambda j, i: (j, 0))
    tile = pl.BlockSpec((tm, tn), lambda j, i: (i, j))
    grid = (f_dim // tn, m // tm)

    def run_up(hb, w1b, w3b):
        return _pallas(up_body, name=name + "_up", grid=grid, in_specs=[row, w_up, w_up], out_specs=[tile, tile, tile],
                       out_shape=[jax.ShapeDtypeStruct((m, f_dim), ACT_DTYPE), jax.ShapeDtypeStruct((m, f_dim), F32),
                                  jax.ShapeDtypeStruct((m, f_dim), F32)],
                       compiler_params=_cparams("parallel", "parallel"))(hb, w1b, w3b)

    def forward(x, h, w1, w3, w2):
        hb = h.astype(ACT_DTYPE)
        w1b, w3b, w2b = w1.astype(MXU_DTYPE), w3.astype(MXU_DTYPE), w2.astype(MXU_DTYPE)
        t, a, b = run_up(hb, w1b, w3b)
        return _mm(t, w2b, name=name + "_down", res=x), (hb, w1b, w3b, w2b, t, a, b, jnp.zeros((0,), h.dtype))

    @jax.custom_vjp
    def f(x, h, w1, w3, w2):
        return forward(x, h, w1, w3, w2)[0]

    def bwd(res, g):
        hb, w1b, w3b, w2b, t, a, b, like_h = res
        gb = g.astype(ACT_DTYPE)
        da, db = _pallas(dt_body, name=name + "_dt", grid=grid, in_specs=[row, w_dn, tile, tile], out_specs=[tile, tile],
                         out_shape=[jax.ShapeDtypeStruct((m, f_dim), ACT_DTYPE)] * 2,
                         compiler_params=_cparams("parallel", "parallel"))(gb, w2b, a, b)
        dw2 = _mm(t, gb, ta=True, name=name + "_dw2")
        dh = _mm(da, w1b, tb=True, name=name + "_dx1") + _mm(db, w3b, tb=True, name=name + "_dx3")
        dw1 = _mm(hb, da, ta=True, name=name + "_dw1")
        dw3 = _mm(hb, db, ta=True, name=name + "_dw3")
        return g, dh.astype(like_h.dtype), dw1, dw3, dw2

    f.defvjp(lambda *args: forward(*args), bwd)
    return f(x, h, w1, w3, w2)


MERGE_ROW_TILE = 256


def gated_merge_out(x, ya_t, yb, yc, ga, gb, gc, wa, wb, wc, wo, name="merge"):
    s, d = x.shape
    e = yb.shape[1]
    tm = _tile(s, MERGE_ROW_TILE)

    def branches(ya_ref, yb_ref, yc_ref, wa_ref, wb_ref, wc_ref):
        return (_dot(ya_ref[...], wa_ref[...], TN), _dot(yb_ref[...], wb_ref[...], NN), _dot(yc_ref[...], wc_ref[...], NN))

    def fwd_body(x_ref, ya_ref, yb_ref, yc_ref, ga_ref, gb_ref, gc_ref, wa_ref, wb_ref, wc_ref, wo_ref, o_ref, m_ref):
        pa, pb, pc = branches(ya_ref, yb_ref, yc_ref, wa_ref, wb_ref, wc_ref)
        merged = _sigmoid(ga_ref[...]) * pa + _sigmoid(gb_ref[...]) * pb + _sigmoid(gc_ref[...]) * pc
        mb = merged.astype(m_ref.dtype)
        m_ref[...] = mb
        o_ref[...] = x_ref[...] + _dot(mb, wo_ref[...], NN)

    def bwd_body(g_ref, ya_ref, yb_ref, yc_ref, ga_ref, gb_ref, gc_ref, wa_ref, wb_ref, wc_ref, wo_ref,
                 dga_ref, dgb_ref, dgc_ref, dpa_ref, dpb_ref, dpc_ref):
        dm = _dot(g_ref[...], wo_ref[...], NT)
        ps = branches(ya_ref, yb_ref, yc_ref, wa_ref, wb_ref, wc_ref)
        for p_i, gate_ref, dg_ref, dp_ref in zip(ps, (ga_ref, gb_ref, gc_ref), (dga_ref, dgb_ref, dgc_ref),
                                                 (dpa_ref, dpb_ref, dpc_ref)):
            sg = _sigmoid(gate_ref[...])
            dg_ref[...] = dm * p_i * (sg * (1.0 - sg))
            dp_ref[...] = (dm * sg).astype(dp_ref.dtype)

    rows = lambda width: pl.BlockSpec((tm, width), lambda i: (i, 0))
    cols_t = pl.BlockSpec((e, tm), lambda i: (0, i))
    whole = lambda r, c: pl.BlockSpec((r, c), lambda i: (0, 0))
    in_common = [cols_t, rows(e), rows(e), rows(d), rows(d), rows(d), whole(e, d), whole(e, d), whole(e, d), whole(d, d)]

    def forward(x, ya_t, yb, yc, ga, gb, gc, wa, wb, wc, wo):
        cast = lambda t: t.astype(ACT_DTYPE)
        ops = (cast(ya_t), cast(yb), cast(yc), ga, gb, gc, cast(wa), cast(wb), cast(wc), cast(wo))
        out, merged = _pallas(
            fwd_body, name=name + "_fwd", grid=(s // tm,), in_specs=[rows(d)] + in_common, out_specs=[rows(d), rows(d)],
            out_shape=[jax.ShapeDtypeStruct((s, d), F32), jax.ShapeDtypeStruct((s, d), ACT_DTYPE)],
            compiler_params=_cparams("parallel"))(x, *ops)
        like = tuple(jnp.zeros((0,), t.dtype) for t in (ya_t, yb, yc))
        return out, (ops, merged, like)

    @jax.custom_vjp
    def f(*args):
        return forward(*args)[0]

    def bwd(res, g):
        ops, merged, like = res
        ya_b, yb_b, yc_b, ga, gb, gc, wa_b, wb_b, wc_b, wo_b = ops
        gbf = g.astype(ACT_DTYPE)
        gate_ct = jax.ShapeDtypeStruct((s, d), F32)
        branch_ct = jax.ShapeDtypeStruct((s, d), ACT_DTYPE)
        dga, dgb, dgc, dpa, dpb, dpc = _pallas(
            bwd_body, name=name + "_bwd", grid=(s // tm,), in_specs=[rows(d)] + in_common, out_specs=[rows(d)] * 6,
            out_shape=[gate_ct] * 3 + [branch_ct] * 3, compiler_params=_cparams("parallel"))(gbf, *ops)
        dya_t = _mm(wa_b, dpa, tb=True, out_dtype=like[0].dtype, name=name + "_dya")
        dyb = _mm(dpb, wb_b, tb=True, out_dtype=like[1].dtype, name=name + "_dyb")
        dyc = _mm(dpc, wc_b, tb=True, out_dtype=like[2].dtype, name=name + "_dyc")
        dwa = _mm(ya_b, dpa, name=name + "_dwa")
        dwb = _mm(yb_b, dpb, ta=True, name=name + "_dwb")
        dwc = _mm(yc_b, dpc, ta=True, name=name + "_dwc")
        dwo = _mm(merged, gbf, ta=True, name=name + "_dwo")
        return g, dya_t, dyb, dyc, dga, dgb, dgc, dwa, dwb, dwc, dwo

    f.defvjp(lambda *args: forward(*args), bwd)
    return f(x, ya_t, yb, yc, ga, gb, gc, wa, wb, wc, wo)


def _row_tile(rows, width):
    return _tile(rows, max(SUBLANES, (512 * 1024) // width), 16)


def rmsnorm(x, g, out_dtype=F32, name="rms"):
    rows, d = x.shape
    tr = _row_tile(rows, d)
    n_steps = rows // tr

    def fwd_body(x_ref, g_ref, o_ref):
        xv = x_ref[...].astype(F32)
        r = lax.rsqrt(jnp.mean(xv * xv, axis=-1, keepdims=True) + EPS)
        o_ref[...] = (xv * r * g_ref[...]).astype(o_ref.dtype)

    def bwd_body(x_ref, g_ref, dy_ref, dx_ref, dg_ref):
        xv = x_ref[...].astype(F32)
        dy = dy_ref[...].astype(F32)
        r = lax.rsqrt(jnp.mean(xv * xv, axis=-1, keepdims=True) + EPS)
        xh = xv * r
        dxh = dy * g_ref[...]
        dx_ref[...] = (r * (dxh - xh * jnp.mean(dxh * xh, axis=-1, keepdims=True))).astype(dx_ref.dtype)

        @pl.when(pl.program_id(0) == 0)
        def _():
            dg_ref[...] = jnp.zeros_like(dg_ref)

        dg_ref[...] += jnp.sum(dy * xh, axis=0, keepdims=True)

    row_spec = pl.BlockSpec((tr, d), lambda i: (i, 0))
    vec_spec = pl.BlockSpec((1, d), lambda i: (0, 0))

    def run_fwd(x, g):
        return _pallas(fwd_body, name=name + "_fwd", grid=(n_steps,), in_specs=[row_spec, vec_spec], out_specs=row_spec,
                       out_shape=jax.ShapeDtypeStruct((rows, d), out_dtype), compiler_params=_cparams("parallel"))(
            x, g.reshape(1, d).astype(F32))

    @jax.custom_vjp
    def f(x, g):
        return run_fwd(x, g)

    def fwd(x, g):
        return run_fwd(x, g), (x, g)

    def bwd(res, dy):
        x, g = res
        dx, dg = _pallas(
            bwd_body, name=name + "_bwd", grid=(n_steps,), in_specs=[row_spec, vec_spec, row_spec],
            out_specs=[row_spec, vec_spec],
            out_shape=[jax.ShapeDtypeStruct((rows, d), x.dtype), jax.ShapeDtypeStruct((1, d), F32)],
            compiler_params=_cparams("arbitrary"))(x, g.reshape(1, d).astype(F32), dy)
        return dx, dg.reshape(g.shape).astype(g.dtype)

    f.defvjp(fwd, bwd)
    return f(x, g)


def _rowdot(a, b, name):
    h, s, d = a.shape
    ts = _tile(s, 2048)

    def body(a_ref, b_ref, o_ref):
        o_ref[...] = jnp.sum(a_ref[...].astype(F32) * b_ref[...].astype(F32), axis=-1, keepdims=True)

    spec = pl.BlockSpec((None, ts, d), lambda hh, i: (hh, i, 0))
    return _pallas(body, name=name, grid=(h, s // ts), in_specs=[spec, spec],
                   out_specs=pl.BlockSpec((None, ts, 1), lambda hh, i: (hh, i, 0)),
                   out_shape=jax.ShapeDtypeStruct((h, s, 1), F32), compiler_params=_cparams("parallel", "parallel"))(a, b)


LOG2E = 1.4426950408889634


def mla_attention(q, k, v, scale, name="mla"):
    s, h, d = q.shape
    sk, dv = k.shape[0], v.shape[2]
    tq, tk = _tile(s, MLA_BWD_TQ), _tile(sk, MLA_TK)
    nq, nk = s // tq, sk // tk
    tqf, tkf = _tile(s, MLA_FWD_TQ), _tile(sk, MLA_FWD_TK)
    nkf = sk // tkf
    ones_rows = 16
    c = scale * LOG2E

    def fwd_body(qt_ref, k_ref, vt_ref, ot_ref, lse_ref, m_ref, acc_ref):
        j = pl.program_id(2)

        @pl.when(j == 0)
        def _():
            m_ref[...] = jnp.full_like(m_ref, NEG)
            acc_ref[...] = jnp.zeros_like(acc_ref)

        st = _dot(k_ref[...], qt_ref[...], NN)
        m_prev = m_ref[...]
        m_new = jnp.maximum(m_prev, jnp.max(st, axis=0, keepdims=True) * c)
        pt = jnp.exp2(st * c - m_new)
        acc_ref[...] = jnp.exp2(m_prev - m_new) * acc_ref[...] + _dot(vt_ref[...], pt, NN)
        m_ref[...] = m_new

        @pl.when(j == nkf - 1)
        def _():
            l = acc_ref[dv:dv + 1, :]
            ot_ref[...] = (acc_ref[:dv, :] / l).astype(ot_ref.dtype)
            lse_ref[...] = m_ref[...] + jnp.log2(l)

    def delta_body(ot_ref, dot_ref, o_ref):
        o_ref[...] = jnp.sum(ot_ref[...].astype(F32) * dot_ref[...].astype(F32), axis=0, keepdims=True)

    def bwd_body(qt_ref, k_ref, kt_ref, v_ref, dot_ref, lse_ref, dl_ref, dqt_ref, dk_hbm, dv_hbm, dq_acc, dk_acc, dv_acc):
        hh, i, j = pl.program_id(0), pl.program_id(1), pl.program_id(2)

        @pl.when(j == 0)
        def _():
            dq_acc[...] = jnp.zeros_like(dq_acc)

        @pl.when(i == 0)
        def _():
            dk_acc[j] = jnp.zeros((d, tk), F32)
            dv_acc[j] = jnp.zeros((dv, tk), F32)

        qt, dot_ = qt_ref[...], dot_ref[...]
        pt = jnp.exp2(_dot(k_ref[...], qt, NN) * c - lse_ref[...])
        dst = (pt * (_dot(v_ref[...], dot_, NN) - dl_ref[...])).astype(MXU_DTYPE)
        dv_acc[j] += _dot(dot_, pt, NT)
        dk_acc[j] += _dot(qt, dst, NT)
        dq_acc[...] += _dot(kt_ref[...], dst, NN)

        @pl.when(j == nk - 1)
        def _():
            dqt_ref[...] = dq_acc[...] * scale

        @pl.when(i == nq - 1)
        def _():
            dk_acc[j] = dk_acc[j] * scale
            pltpu.sync_copy(dk_acc.at[j], dk_hbm.at[hh, j])
            pltpu.sync_copy(dv_acc.at[j], dv_hbm.at[hh, j])

    def qt_spec(width):
        return pl.BlockSpec((None, width, tq), lambda hh, i, j: (hh, 0, i))

    def kt_spec(width):
        return pl.BlockSpec((None, width, tk), lambda hh, i, j: (hh, 0, j))

    def k_spec(width):
        return pl.BlockSpec((None, tk, width), lambda hh, i, j: (hh, j, 0))

    def layouts(q, k, v):
        cast = lambda t: t.astype(ACT_DTYPE)
        return (cast(jnp.transpose(q, (1, 2, 0))), cast(jnp.transpose(k, (1, 0, 2))), cast(jnp.transpose(k, (1, 2, 0))),
                cast(jnp.transpose(v, (1, 0, 2))), cast(jnp.transpose(v, (1, 2, 0))))

    def run_fwd(qt, kh, vt):
        vt_ones = jnp.concatenate([vt, jnp.ones((h, ones_rows, sk), vt.dtype)], axis=1)

        def qf_spec(width):
            return pl.BlockSpec((None, width, tqf), lambda hh, i, j: (hh, 0, i))

        kf_spec = pl.BlockSpec((None, tkf, d), lambda hh, i, j: (hh, j, 0))
        vf_spec = pl.BlockSpec((None, dv + ones_rows, tkf), lambda hh, i, j: (hh, 0, j))
        return _pallas(
            fwd_body, name=name + "_fwd", grid=(h, s // tqf, nkf), in_specs=[qf_spec(d), kf_spec, vf_spec],
            out_specs=[qf_spec(dv), qf_spec(1)],
            out_shape=[jax.ShapeDtypeStruct((h, dv, s), ACT_DTYPE), jax.ShapeDtypeStruct((h, 1, s), F32)],
            scratch_shapes=[pltpu.VMEM((1, tqf), F32), pltpu.VMEM((dv + ones_rows, tqf), F32)],
            compiler_params=_cparams("parallel", "parallel", "arbitrary"))(qt, kh, vt_ones)

    @jax.custom_vjp
    def f(q, k, v):
        qt, kh, _, _, vt = layouts(q, k, v)
        return run_fwd(qt, kh, vt)[0].reshape(h * dv, s)

    def fwd(q, k, v):
        qt, kh, kt, vh, vt = layouts(q, k, v)
        ot, lse = run_fwd(qt, kh, vt)
        return ot.reshape(h * dv, s), (qt, kh, kt, vh, ot, lse)

    def bwd(res, dy):
        qt, kh, kt, vh, ot, lse = res
        dot_ = dy.reshape(h, dv, s)
        ts = _tile(s, 2048)
        col = pl.BlockSpec((None, dv, ts), lambda hh, i: (hh, 0, i))
        delta = _pallas(delta_body, name=name + "_delta", grid=(h, s // ts), in_specs=[col, col],
                        out_specs=pl.BlockSpec((None, 1, ts), lambda hh, i: (hh, 0, i)),
                        out_shape=jax.ShapeDtypeStruct((h, 1, s), F32), compiler_params=_cparams("parallel", "parallel"))(ot, dot_)
        any_spec = pl.BlockSpec(memory_space=pl.ANY)
        dqt, dkt, dvt = _pallas(
            bwd_body, name=name + "_bwd", grid=(h, nq, nk),
            in_specs=[qt_spec(d), k_spec(d), kt_spec(d), k_spec(dv), qt_spec(dv), qt_spec(1), qt_spec(1)],
            out_specs=[qt_spec(d), any_spec, any_spec],
            out_shape=[jax.ShapeDtypeStruct((h, d, s), F32), jax.ShapeDtypeStruct((h, nk, d, tk), F32),
                       jax.ShapeDtypeStruct((h, nk, dv, tk), F32)],
            scratch_shapes=[pltpu.VMEM((d, tq), F32), pltpu.VMEM((nk, d, tk), F32), pltpu.VMEM((nk, dv, tk), F32)],
            compiler_params=_cparams("parallel", "arbitrary", "arbitrary"))(qt, kh, kt, vh, dot_, lse, delta)
        to_tokens = lambda t: jnp.transpose(t, (1, 3, 0, 2)).reshape(sk, h, t.shape[2])
        return jnp.transpose(dqt, (2, 0, 1)), to_tokens(dkt), to_tokens(dvt)

    f.defvjp(fwd, bwd)
    return f(q, k, v)


WATTN_TQ = 2 * C_BLOCK
WATTN_KW = WATTN_TQ + 2 * C_BLOCK


def window_attention(q, k, v, bias, sink, name="wattn"):
    hq, s, dh = q.shape
    g = hq // C_KV_HEADS
    tq, kw, half = WATTN_TQ, WATTN_KW, WATTN_KW // 2
    nt = s // tq
    scale = dh ** -0.5
    sink_b = jnp.broadcast_to(sink.astype(F32).reshape(hq, 1, 1), (hq, 1, LANES))
    neg = jnp.full((hq, C_BLOCK, C_BLOCK), NEG, F32)
    tile = jnp.concatenate(
        [jnp.concatenate([bias[:, cb - rb] if 0 <= cb - rb <= 2 else neg for cb in range(kw // C_BLOCK)], axis=2)
         for rb in range(tq // C_BLOCK)], axis=1)

    def key_bias(i):
        pos = lax.broadcasted_iota(jnp.int32, (1, kw), 1) + i * tq - C_BLOCK
        return jnp.where(jnp.logical_and(pos >= 0, pos < s), 0.0, NEG)

    def both(a_ref, b_ref):
        return jnp.concatenate([a_ref[...], b_ref[...]], axis=0)

    def fwd_body(q_ref, ka_ref, kb_ref, va_ref, vb_ref, b_ref, sk_ref, o_ref, lse_ref):
        kb_ = key_bias(pl.program_id(1))
        k_all, v_all = both(ka_ref, kb_ref), both(va_ref, vb_ref)
        for hh in range(g):
            sc = _dot(q_ref[hh], k_all, NT) * scale + b_ref[hh] + kb_
            snk = sk_ref[hh][:, :1]
            m = jnp.maximum(jnp.max(sc, axis=-1, keepdims=True), snk)
            p = jnp.exp(sc - m)
            l = jnp.sum(p, axis=-1, keepdims=True) + jnp.exp(snk - m)
            o_ref[hh] = (_dot(p, v_all, NN) / l).astype(o_ref.dtype)
            lse_ref[hh] = m + jnp.log(l)

    def bwd_body(q_ref, ka_ref, kb_ref, va_ref, vb_ref, b_ref, sk_ref, do_ref, lse_ref, dl_ref,
                 dq_ref, db_ref, dsink_ref, dk_hbm, dv_hbm, dk_acc, dv_acc):
        kv, i = pl.program_id(0), pl.program_id(1)

        @pl.when(i == 0)
        def _():
            dk_acc[...] = jnp.zeros_like(dk_acc)
            dv_acc[...] = jnp.zeros_like(dv_acc)
            db_ref[...] = jnp.zeros_like(db_ref)
            dsink_ref[...] = jnp.zeros_like(dsink_ref)

        kb_ = key_bias(i)
        k_all, v_all = both(ka_ref, kb_ref), both(va_ref, vb_ref)
        dk_t = jnp.zeros((kw, dh), F32)
        dv_t = jnp.zeros((kw, dh), F32)
        for hh in range(g):
            lse, dl, do = lse_ref[hh], dl_ref[hh], do_ref[hh]
            p = jnp.exp(_dot(q_ref[hh], k_all, NT) * scale + b_ref[hh] + kb_ - lse)
            ds = p * (_dot(do, v_all, NT) - dl)
            db_ref[hh] += ds
            total = jnp.broadcast_to(-jnp.sum(jnp.exp(sk_ref[hh][:, :1] - lse) * dl, axis=0, keepdims=True), (1, LANES))
            dsink_ref[hh] += jnp.where(lax.broadcasted_iota(jnp.int32, (1, LANES), 1) == 0, total, 0.0)
            dsb = (ds * scale).astype(MXU_DTYPE)
            dq_ref[hh] = _dot(dsb, k_all, NN).astype(dq_ref.dtype)
            dk_t += _dot(dsb, q_ref[hh], TN)
            dv_t += _dot(p, do, TN)
        rows = pl.ds(pl.multiple_of(i * tq, tq), kw)
        dk_acc[rows, :] += dk_t
        dv_acc[rows, :] += dv_t

        @pl.when(i == nt - 1)
        def _():
            pltpu.sync_copy(dk_acc, dk_hbm.at[kv])
            pltpu.sync_copy(dv_acc, dv_hbm.at[kv])

    def q_spec(width):
        return pl.BlockSpec((g, tq, width), lambda kv, i: (kv, i, 0))

    ka_spec = pl.BlockSpec((None, half, dh), lambda kv, i: (kv, i, 0))
    kb_spec = pl.BlockSpec((None, half, dh), lambda kv, i: (kv, i + 1, 0))
    b_spec = pl.BlockSpec((g, tq, kw), lambda kv, i: (kv, 0, 0))
    sk_spec = pl.BlockSpec((g, 1, LANES), lambda kv, i: (kv, 0, 0))

    def padded(t):
        return jnp.pad(t, ((0, 0), (C_BLOCK, C_BLOCK), (0, 0)))

    def run_fwd(q, kp, vp, tile, sink_b):
        return _pallas(
            fwd_body, name=name + "_fwd", grid=(C_KV_HEADS, nt),
            in_specs=[q_spec(dh), ka_spec, kb_spec, ka_spec, kb_spec, b_spec, sk_spec], out_specs=[q_spec(dh), q_spec(1)],
            out_shape=[jax.ShapeDtypeStruct((hq, s, dh), ACT_DTYPE), jax.ShapeDtypeStruct((hq, s, 1), F32)],
            compiler_params=_cparams("parallel", "parallel"))(q, kp, kp, vp, vp, tile, sink_b)

    @jax.custom_vjp
    def f(q, k, v, tile, sink_b):
        return run_fwd(q, padded(k), padded(v), tile, sink_b)[0]

    def fwd(q, k, v, tile, sink_b):
        kp, vp = padded(k), padded(v)
        o, lse = run_fwd(q, kp, vp, tile, sink_b)
        return o, (q, kp, vp, tile, sink_b, o, lse)

    def bwd(res, do):
        q, kp, vp, tile, sink_b, o, lse = res
        delta = _rowdot(o, do, name + "_delta")
        any_spec = pl.BlockSpec(memory_space=pl.ANY)
        acc = jax.ShapeDtypeStruct((C_KV_HEADS, s + 2 * C_BLOCK, dh), F32)
        dq, dtile, dsink, dkp, dvp = _pallas(
            bwd_body, name=name + "_bwd", grid=(C_KV_HEADS, nt),
            in_specs=[q_spec(dh), ka_spec, kb_spec, ka_spec, kb_spec, b_spec, sk_spec, q_spec(dh), q_spec(1), q_spec(1)],
            out_specs=[q_spec(dh), b_spec, sk_spec, any_spec, any_spec],
            out_shape=[jax.ShapeDtypeStruct((hq, s, dh), q.dtype), jax.ShapeDtypeStruct((hq, tq, kw), F32),
                       jax.ShapeDtypeStruct((hq, 1, LANES), F32), acc, acc],
            scratch_shapes=[pltpu.VMEM((s + 2 * C_BLOCK, dh), F32), pltpu.VMEM((s + 2 * C_BLOCK, dh), F32)],
            compiler_params=_cparams("parallel", "arbitrary"))(q, kp, kp, vp, vp, tile, sink_b, do, lse, delta)
        unpad = lambda t: t[:, C_BLOCK:-C_BLOCK].astype(kp.dtype)
        return dq, unpad(dkp), unpad(dvp), dtile, dsink

    f.defvjp(fwd, bwd)
    return f(q, k, v, tile, sink_b)


HG_PREP_ROWS = 256
HG_GROUP = 8
HG_INTRA_BLOCK = 256
HG_INTER_CHUNKS = 16
MLA_FWD_TQ, MLA_FWD_TK, MLA_BWD_TQ, MLA_TK = 2048, 2048, 2048, 1024


def _hdot(a, b, dims):
    b16 = jnp.bfloat16
    hi = b.astype(b16)
    rest = b - hi.astype(F32)
    mid = rest.astype(b16)
    lo = (rest - mid.astype(F32)).astype(b16)
    a16 = a.astype(b16)
    dn = (dims, ((), ()))
    return (lax.dot_general(a16, hi, dn, preferred_element_type=F32) + lax.dot_general(a16, mid, dn, preferred_element_type=F32)
            + lax.dot_general(a16, lo, dn, preferred_element_type=F32))


def hgrn_prep(q, z, lb, reverse, name):
    n_hp, s, tc = q.shape
    tb = _tile(s, HG_PREP_ROWS)
    ncb = tb // B_CHUNK

    def chunk_matrices():
        r = lax.broadcasted_iota(jnp.int32, (tb, tb), 0)
        cc = lax.broadcasted_iota(jnp.int32, (tb, tb), 1)
        same = r // B_CHUNK == cc // B_CHUNK
        tri = (cc >= r) if reverse else (cc <= r)
        cum = jnp.where(jnp.logical_and(same, tri), 1.0, 0.0).astype(F32)
        every = jnp.where(same, 1.0, 0.0).astype(F32)
        pr = lax.broadcasted_iota(jnp.int32, (ncb, tb), 0)
        pc = lax.broadcasted_iota(jnp.int32, (ncb, tb), 1)
        per_chunk = jnp.where(pc // B_CHUNK == pr, 1.0, 0.0).astype(F32)
        return cum, every, per_chunk

    def gates(zv, lbv):
        e = jnp.exp(-jnp.abs(zv))
        big, small = 1.0 / (1.0 + e), e / (1.0 + e)
        sig = jnp.where(zv >= 0, big, small)
        nsig = jnp.where(zv >= 0, small, big)
        f = lbv + (1.0 - lbv) * sig
        return sig, nsig, f, jnp.log(jnp.maximum(f, TINY)), (1.0 - lbv) * nsig

    def fwd_body(q_ref, z_ref, lb_ref, qd_ref, ki_ref, ke_ref, dec_ref):
        cum, every, per_chunk = chunk_matrices()
        _, _, _, lf, key = gates(z_ref[...], lb_ref[...])
        b = _hdot(cum, lf, NN)
        tot = _hdot(every, lf, NN)
        qd_ref[...] = q_ref[...] * jnp.exp(b)
        ki_ref[...] = key * jnp.exp(-b)
        ke_ref[...] = key * jnp.exp(tot - b)
        dec_ref[...] = jnp.exp(_hdot(per_chunk, lf, NN))

    def bwd_body(q_ref, z_ref, lb_ref, dqd_ref, dki_ref, dke_ref, ddec_ref, dq_ref, dz_ref, dlb_ref):
        cum, every, per_chunk = chunk_matrices()
        lbv = lb_ref[...]
        sig, nsig, f, lf, key = gates(z_ref[...], lbv)
        b = _hdot(cum, lf, NN)
        tot = _hdot(every, lf, NN)
        e_b, e_nb, e_tb = jnp.exp(b), jnp.exp(-b), jnp.exp(tot - b)
        dqd, dki, dke = dqd_ref[...], dki_ref[...], dke_ref[...]
        dq_ref[...] = dqd * e_b
        dkey = dki * e_nb + dke * e_tb
        t_end = dke * key * e_tb
        db = dqd * q_ref[...] * e_b - dki * key * e_nb - t_end
        dtot = ddec_ref[...] * jnp.exp(_hdot(per_chunk, lf, NN)) + _hdot(per_chunk, t_end, NN)
        dlf = _hdot(cum, db, TN) + _hdot(per_chunk, dtot, TN)
        df = jnp.where(f > TINY, dlf / f, 0.0)
        one_m_lb = 1.0 - lbv
        dz_ref[...] = (df - dkey) * one_m_lb * sig * nsig
        dlb_part = jnp.sum(df * nsig - dkey * nsig, axis=0, keepdims=True)

        @pl.when(pl.program_id(1) == 0)
        def _():
            dlb_ref[...] = jnp.zeros_like(dlb_ref)

        dlb_ref[...] += dlb_part

    tok = pl.BlockSpec((None, tb, tc), lambda j, i: (j, i, 0))
    vec = pl.BlockSpec((None, 1, tc), lambda j, i: (j, 0, 0))
    chk = pl.BlockSpec((None, ncb, tc), lambda j, i: (j, i, 0))
    grid = (n_hp, s // tb)
    tok_shape = jax.ShapeDtypeStruct((n_hp, s, tc), F32)
    chk_shape = jax.ShapeDtypeStruct((n_hp, s // B_CHUNK, tc), F32)

    def run_fwd(q, z, lb):
        return _pallas(fwd_body, name=name + "_fwd", grid=grid, in_specs=[tok, tok, vec], out_specs=[tok, tok, tok, chk],
                       out_shape=[tok_shape, tok_shape, tok_shape, chk_shape],
                       compiler_params=_cparams("parallel", "parallel"))(q, z, lb)

    @jax.custom_vjp
    def f(q, z, lb):
        return tuple(run_fwd(q, z, lb))

    def fwd(q, z, lb):
        return tuple(run_fwd(q, z, lb)), (q, z, lb)

    def bwd(res, cts):
        q, z, lb = res
        dq, dz, dlb = _pallas(
            bwd_body, name=name + "_bwd", grid=grid, in_specs=[tok, tok, vec, tok, tok, tok, chk], out_specs=[tok, tok, vec],
            out_shape=[tok_shape, tok_shape, jax.ShapeDtypeStruct((n_hp, 1, tc), F32)],
            compiler_params=_cparams("parallel", "arbitrary"))(q, z, lb, *cts)
        return dq, dz, dlb

    f.defvjp(fwd, bwd)
    return f(q, z, lb)


def _pair_cols(ref, hh, width):
    return ref[:, hh * width:(hh + 1) * width]


def hgrn_intra(qd, ki, v, reverse, name):
    s = qd.shape[1]
    tb = _tile(s, HG_INTRA_BLOCK)
    wk, wv = HG_GROUP * B_DK, HG_GROUP * B_DV

    def mask():
        r = lax.broadcasted_iota(jnp.int32, (tb, tb), 0)
        c = lax.broadcasted_iota(jnp.int32, (tb, tb), 1)
        return jnp.logical_and(r // B_CHUNK == c // B_CHUNK, (c >= r) if reverse else (c <= r))

    def fwd_body(q_ref, k_ref, v_ref, o_ref):
        msk = mask()
        for hh in range(HG_GROUP):
            sc = jnp.where(msk, _dot(_pair_cols(q_ref, hh, B_DK), _pair_cols(k_ref, hh, B_DK), NT), 0.0)
            o_ref[:, hh * B_DV:(hh + 1) * B_DV] = _dot(sc, _pair_cols(v_ref, hh, B_DV), NN)

    def bwd_body(q_ref, k_ref, v_ref, do_ref, dq_ref, dk_ref, dv_ref):
        msk = mask()
        for hh in range(HG_GROUP):
            q, k = _pair_cols(q_ref, hh, B_DK), _pair_cols(k_ref, hh, B_DK)
            vv, do = _pair_cols(v_ref, hh, B_DV), _pair_cols(do_ref, hh, B_DV)
            sc = jnp.where(msk, _dot(q, k, NT), 0.0)
            ds = jnp.where(msk, _dot(do, vv, NT), 0.0)
            dq_ref[:, hh * B_DK:(hh + 1) * B_DK] = _dot(ds, k, NN)
            dk_ref[:, hh * B_DK:(hh + 1) * B_DK] = _dot(ds, q, TN)
            dv_ref[:, hh * B_DV:(hh + 1) * B_DV] = _dot(sc, do, TN)

    ks = pl.BlockSpec((None, tb, wk), lambda hp, i: (hp, i, 0))
    vs = pl.BlockSpec((None, tb, wv), lambda hp, i: (hp, i, 0))
    grid = (B_HEADS // HG_GROUP, s // tb)

    def run_fwd(qd, ki, v):
        return _pallas(fwd_body, name=name + "_fwd", grid=grid, in_specs=[ks, ks, vs], out_specs=vs,
                       out_shape=jax.ShapeDtypeStruct(v.shape, F32), compiler_params=_cparams("parallel", "parallel"))(qd, ki, v)

    @jax.custom_vjp
    def f(qd, ki, v):
        return run_fwd(qd, ki, v)

    def fwd(qd, ki, v):
        return run_fwd(qd, ki, v), (qd, ki, v)

    def bwd(res, do):
        qd, ki, v = res
        return tuple(_pallas(
            bwd_body, name=name + "_bwd", grid=grid, in_specs=[ks, ks, vs, vs], out_specs=[ks, ks, vs],
            out_shape=[jax.ShapeDtypeStruct(qd.shape, F32), jax.ShapeDtypeStruct(ki.shape, F32),
                       jax.ShapeDtypeStruct(v.shape, F32)],
            compiler_params=_cparams("parallel", "parallel"))(qd, ki, v, do))

    f.defvjp(fwd, bwd)
    return f(qd, ki, v)


def hgrn_inter(qd, ke, v, dec, reverse, name):
    s = qd.shape[1]
    nc = s // B_CHUNK
    cpb = HG_INTER_CHUNKS if nc % HG_INTER_CHUNKS == 0 else nc
    tb = cpb * B_CHUNK
    nblk = nc // cpb
    wk, wv = HG_GROUP * B_DK, HG_GROUP * B_DV
    n_hp = B_HEADS // HG_GROUP

    def rows(c):
        return pl.ds(c * B_CHUNK, B_CHUNK)

    def kcols(hh):
        return slice(hh * B_DK, (hh + 1) * B_DK)

    def vcols(hh):
        return slice(hh * B_DV, (hh + 1) * B_DV)

    def order(flip):
        return reversed(range(cpb)) if flip else range(cpb)

    def fwd_body(q_ref, k_ref, v_ref, dec_ref, o_ref, st_ref, state):
        @pl.when(pl.program_id(1) == 0)
        def _():
            state[...] = jnp.zeros_like(state)

        for c in order(reverse):
            for hh in range(HG_GROUP):
                st = state[hh]
                st_ref[c, hh] = st
                o_ref[rows(c), vcols(hh)] = _dot(q_ref[rows(c), kcols(hh)], st, NT)
                state[hh] = st * dec_ref[pl.ds(c, 1), kcols(hh)] + _dot(v_ref[rows(c), vcols(hh)], k_ref[rows(c), kcols(hh)], TN)

    def bwd_body(q_ref, k_ref, v_ref, dec_ref, st_ref, do_ref, dq_ref, dk_ref, dv_ref, ddec_ref, dstate):
        @pl.when(pl.program_id(1) == 0)
        def _():
            dstate[...] = jnp.zeros_like(dstate)

        for c in order(not reverse):
            for hh in range(HG_GROUP):
                dst = dstate[hh]
                st = st_ref[c, hh]
                do_c = do_ref[rows(c), vcols(hh)]
                dk_ref[rows(c), kcols(hh)] = _dot(v_ref[rows(c), vcols(hh)], dst, NN)
                dv_ref[rows(c), vcols(hh)] = _dot(k_ref[rows(c), kcols(hh)], dst, NT)
                ddec_ref[pl.ds(c, 1), kcols(hh)] = jnp.sum(dst * st, axis=0, keepdims=True)
                dq_ref[rows(c), kcols(hh)] = _dot(do_c, st, NN)
                dstate[hh] = dst * dec_ref[pl.ds(c, 1), kcols(hh)] + _dot(do_c, q_ref[rows(c), kcols(hh)], TN)

    def specs(flip):
        blk = (lambda i: nblk - 1 - i) if flip else (lambda i: i)
        tok_k = pl.BlockSpec((None, tb, wk), lambda hp, i: (hp, blk(i), 0))
        tok_v = pl.BlockSpec((None, tb, wv), lambda hp, i: (hp, blk(i), 0))
        chk = pl.BlockSpec((None, cpb, wk), lambda hp, i: (hp, blk(i), 0))
        sts = pl.BlockSpec((None, cpb, HG_GROUP, B_DV, B_DK), lambda hp, i: (hp, blk(i), 0, 0, 0))
        return tok_k, tok_v, chk, sts

    scratch = [pltpu.VMEM((HG_GROUP, B_DV, B_DK), F32)]

    def run_fwd(qd, ke, v, dec):
        tok_k, tok_v, chk, sts = specs(reverse)
        return _pallas(
            fwd_body, name=name + "_fwd", grid=(n_hp, nblk), in_specs=[tok_k, tok_k, tok_v, chk], out_specs=[tok_v, sts],
            out_shape=[jax.ShapeDtypeStruct(v.shape, F32), jax.ShapeDtypeStruct((n_hp, nc, HG_GROUP, B_DV, B_DK), F32)],
            scratch_shapes=scratch, compiler_params=_cparams("parallel", "arbitrary"))(qd, ke, v, dec)

    @jax.custom_vjp
    def f(qd, ke, v, dec):
        return run_fwd(qd, ke, v, dec)[0]

    def fwd(qd, ke, v, dec):
        o, st = run_fwd(qd, ke, v, dec)
        return o, (qd, ke, v, dec, st)

    def bwd(res, do):
        qd, ke, v, dec, st = res
        tok_k, tok_v, chk, sts = specs(not reverse)
        return tuple(_pallas(
            bwd_body, name=name + "_bwd", grid=(n_hp, nblk), in_specs=[tok_k, tok_k, tok_v, chk, sts, tok_v],
            out_specs=[tok_k, tok_k, tok_v, chk],
            out_shape=[jax.ShapeDtypeStruct(qd.shape, F32), jax.ShapeDtypeStruct(ke.shape, F32),
                       jax.ShapeDtypeStruct(v.shape, F32), jax.ShapeDtypeStruct(dec.shape, F32)],
            scratch_shapes=scratch, compiler_params=_cparams("parallel", "arbitrary"))(qd, ke, v, dec, st, do))

    f.defvjp(fwd, bwd)
    return f(qd, ke, v, dec)


def loss_head(y, target, name="loss"):
    s, d = y.shape
    tr = _row_tile(s, d)

    def body(y_ref, t_ref, o_ref):
        @pl.when(pl.program_id(0) == 0)
        def _():
            o_ref[...] = jnp.zeros_like(o_ref)

        e = y_ref[...] - t_ref[...]
        part = jnp.sum(jnp.sum(e * e, axis=-1, keepdims=True), axis=0, keepdims=True) * (0.5 / d)
        o_ref[...] += jnp.broadcast_to(part, o_ref.shape)

    spec = pl.BlockSpec((tr, d), lambda i: (i, 0))

    def run(y, t):
        out = _pallas(body, name=name, grid=(s // tr,), in_specs=[spec, spec],
                      out_specs=pl.BlockSpec((SUBLANES, LANES), lambda i: (0, 0)),
                      out_shape=jax.ShapeDtypeStruct((SUBLANES, LANES), F32), compiler_params=_cparams("arbitrary"))(y, t)
        return out[0, 0]

    @jax.custom_vjp
    def f(y, t):
        return run(y, t)

    def fwd(y, t):
        return run(y, t), (y, t)

    def bwd(res, g):
        y, t = res
        dy = g * (y - t) * (1.0 / d)
        return dy, -dy

    f.defvjp(fwd, bwd)
    return f(y, target)


def _mesh_pos():
    return lax.axis_index("x"), lax.axis_index("y"), lax.axis_index("c")


def all_gather_shards(shards):
    n = len(shards)

    def body(*refs):
        ins, outs = refs[:n], refs[n:2 * n]
        send_sems, recv_sems, local_sems = refs[2 * n:]
        x, y, c = _mesh_pos()
        me, sibling = (x, y, c), (x, y, 1 - c)
        chips = [(1 - x, y), (x, 1 - y), (1 - x, 1 - y)]

        def slot(t, px, py, pc):
            return outs[t].at[4 * px + 2 * py + pc]

        def copy(t, k, block, to, src=None):
            return pltpu.make_async_remote_copy(
                src_ref=slot(t, *block) if src is None else src, dst_ref=slot(t, *block), send_sem=send_sems.at[t, k],
                recv_sem=recv_sems.at[t, k], device_id=to, device_id_type=pl.DeviceIdType.MESH)

        mine = [pltpu.make_async_copy(ins[t], slot(t, *me), local_sems.at[t]) for t in range(n)]
        for cp in mine:
            cp.start()
        first = []
        for t in range(n):
            first.append(copy(t, 0, me, sibling, src=ins[t]))
            first += [copy(t, 1 + j, me, (*chip, c), src=ins[t]) for j, chip in enumerate(chips)]
        for cp in first:
            cp.start()
        passed = []
        for j, chip in enumerate(chips):
            for t in range(n):
                copy(t, 1 + j, (*chip, c), me).wait_recv()
                cp = copy(t, 4 + j, (*chip, c), sibling)
                cp.start()
                passed.append(cp)
        for t in range(n):
            copy(t, 0, sibling, me).wait_recv()
            for j, chip in enumerate(chips):
                copy(t, 4 + j, (*chip, 1 - c), me).wait_recv()
        for cp in first + passed:
            cp.wait_send()
        for cp in mine:
            cp.wait()

    any_spec = pl.BlockSpec(memory_space=pl.ANY)
    return _pallas(
        body, name="all_gather_weights", out_shape=[jax.ShapeDtypeStruct((N_DEV, *s.shape), s.dtype) for s in shards],
        in_specs=[any_spec] * n, out_specs=[any_spec] * n,
        scratch_shapes=[pltpu.SemaphoreType.DMA((n, 7)), pltpu.SemaphoreType.DMA((n, 7)), pltpu.SemaphoreType.DMA((n,))],
    )(*shards)


def all_to_all_blocks(stacks):
    n = len(stacks)

    def body(*refs):
        ins, outs = refs[:n], refs[n:2 * n]
        send_sems, recv_sems, local_sems = refs[2 * n:]
        x, y, c = _mesh_pos()
        me = 4 * x + 2 * y + c
        mine = [pltpu.make_async_copy(ins[t].at[me], outs[t].at[me], local_sems.at[t]) for t in range(n)]
        for cp in mine:
            cp.start()
        copies = []
        for k in range(1, N_DEV):
            px = 1 - x if k & 4 else x
            py = 1 - y if k & 2 else y
            pc = 1 - c if k & 1 else c
            for t in range(n):
                cp = pltpu.make_async_remote_copy(
                    src_ref=ins[t].at[4 * px + 2 * py + pc], dst_ref=outs[t].at[me], send_sem=send_sems.at[t, k - 1],
                    recv_sem=recv_sems.at[t, k - 1], device_id=(px, py, pc), device_id_type=pl.DeviceIdType.MESH)
                cp.start()
                copies.append(cp)
        for cp in copies:
            cp.wait_recv()
        for cp in copies:
            cp.wait_send()
        for cp in mine:
            cp.wait()

    any_spec = pl.BlockSpec(memory_space=pl.ANY)
    return _pallas(
        body, name="all_to_all_grads", out_shape=[jax.ShapeDtypeStruct(s.shape, s.dtype) for s in stacks],
        in_specs=[any_spec] * n, out_specs=[any_spec] * n,
        scratch_shapes=[pltpu.SemaphoreType.DMA((n, 7)), pltpu.SemaphoreType.DMA((n, 7)), pltpu.SemaphoreType.DMA((n,))],
    )(*stacks)


def all_gather_small(v):
    r, w = v.shape

    def body(x_ref, out_ref, send_sems, recv_sems):
        x, y, c = _mesh_pos()
        me = 4 * x + 2 * y + c
        copies = []
        for k in range(1, N_DEV):
            px = 1 - x if k & 4 else x
            py = 1 - y if k & 2 else y
            pc = 1 - c if k & 1 else c
            cp = pltpu.make_async_remote_copy(
                src_ref=x_ref, dst_ref=out_ref.at[me], send_sem=send_sems.at[k - 1], recv_sem=recv_sems.at[k - 1],
                device_id=(px, py, pc), device_id_type=pl.DeviceIdType.MESH)
            cp.start()
            copies.append(cp)
        out_ref[me] = x_ref[...]
        for cp in copies:
            cp.wait_recv()
        for cp in copies:
            cp.wait_send()

    vmem = pl.BlockSpec(memory_space=pltpu.VMEM)
    return _pallas(
        body, name="all_gather_small", out_shape=jax.ShapeDtypeStruct((N_DEV, r, w), v.dtype), in_specs=[vmem],
        out_specs=vmem, scratch_shapes=[pltpu.SemaphoreType.DMA((7,)), pltpu.SemaphoreType.DMA((7,))],
    )(v)


def adamw_rows(parts, w, m, v, name):
    n, r, lanes = parts.shape
    tr = _tile(r, max(SUBLANES, (256 * 1024) // lanes), SUBLANES)
    c1 = 1.0 / (1.0 - ADAM_B1 ** ADAM_STEP)
    c2 = 1.0 / (1.0 - ADAM_B2 ** ADAM_STEP)

    def body(p_ref, w_ref, m_ref, v_ref, g_ref, d_ref, nm_ref, nv_ref):
        g = p_ref[0].astype(F32)
        for j in range(1, n):
            g = g + p_ref[j].astype(F32)
        nm = ADAM_B1 * m_ref[...] + (1.0 - ADAM_B1) * g
        nv = ADAM_B2 * v_ref[...] + (1.0 - ADAM_B2) * (g * g)
        g_ref[...] = g
        nm_ref[...] = nm
        nv_ref[...] = nv
        d_ref[...] = -ADAM_LR * ((nm * c1) / (jnp.sqrt(nv * c2) + ADAM_EPS) + ADAM_WD * w_ref[...])

    row = pl.BlockSpec((tr, lanes), lambda i: (i, 0))
    out = jax.ShapeDtypeStruct((r, lanes), F32)
    return _pallas(body, name=name, grid=(r // tr,), in_specs=[pl.BlockSpec((n, tr, lanes), lambda i: (0, i, 0)), row, row, row],
                   out_specs=[row, row, row, row], out_shape=[out, out, out, out], compiler_params=_cparams("parallel"))(
        parts, w, m, v)


def _padded(n):
    return -(-n // PACK_QUANTUM) * PACK_QUANTUM


def _pack(pieces, total_rows=None):
    flat = []
    for p in pieces:
        p = p.reshape(-1).astype(F32)
        flat.append(jnp.pad(p, (0, _padded(p.size) - p.size)))
    out = jnp.concatenate(flat).reshape(-1, LANES)
    if total_rows is not None and out.shape[0] != total_rows:
        out = jnp.pad(out, ((0, total_rows - out.shape[0]), (0, 0)))
    return out


def _pack_rows(sizes):
    rows = sum(_padded(n) for n in sizes) // LANES
    return -(-rows // PACK_ROW_TILE) * PACK_ROW_TILE


def _unpack(rows, shapes):
    lead = rows.shape[:-2]
    flat = rows.reshape(*lead, -1)
    out, off = [], 0
    for shp in shapes:
        n = int(np.prod(shp))
        out.append(flat[..., off:off + n].reshape(*lead, *shp))
        off += _padded(n)
    return out


def _shards_to_full(stacked, axis):
    moved = jnp.moveaxis(stacked, 0, axis)
    shp = list(stacked.shape[1:])
    shp[axis] *= N_DEV
    return moved.reshape(shp)


def _full_to_shards(full, axis):
    shp = list(full.shape)
    shp[axis:axis + 1] = [N_DEV, shp[axis] // N_DEV]
    return jnp.moveaxis(full.reshape(shp), axis, 0)


def _heads(t, n, d, dtype=ACT_DTYPE):
    return jnp.transpose(t.reshape(t.shape[0], n, d), (1, 0, 2)).astype(dtype)


def _unheads(t):
    return jnp.transpose(t, (1, 0, 2)).reshape(t.shape[1], -1)


def _rope_tables(s):
    half = A_ROPE // 2
    inv = ROPE_THETA ** (-jnp.arange(half, dtype=F32) / half)
    ang = jnp.arange(s, dtype=jnp.int32).astype(F32)[:, None] * inv[None, :]
    return jnp.cos(ang), jnp.sin(ang)


def _rope(t, cos, sin):
    half = A_ROPE // 2
    t1, t2 = t[..., :half], t[..., half:]
    c, sn = cos[:, None, :], sin[:, None, :]
    return jnp.concatenate([t1 * c - t2 * sn, t1 * sn + t2 * c], axis=-1)


def _t5_bucket(rel):
    nb = REL_BUCKETS // 2
    max_exact = nb // 2
    ret = (rel > 0).astype(jnp.int32) * nb
    n = jnp.abs(rel)
    large = max_exact + (jnp.log(jnp.maximum(n, 1).astype(F32) / max_exact)
                         / math.log(REL_MAX_DIST / max_exact) * (nb - max_exact)).astype(jnp.int32)
    large = jnp.minimum(large, nb - 1)
    return ret + jnp.where(n < max_exact, n, large)


def _window_bias(rel_bias):
    span = 3 * C_BLOCK
    rel = jnp.arange(span)[None, :] - C_BLOCK - jnp.arange(C_BLOCK)[:, None]
    onehot = (_t5_bucket(rel)[..., None] == jnp.arange(REL_BUCKETS)).astype(F32)
    bias = jnp.einsum("qkb,bh->hqk", onehot, rel_bias.astype(F32), precision=lax.Precision.HIGHEST)
    bias = jnp.where((jnp.abs(rel) <= C_WINDOW)[None], bias, NEG)
    return jnp.transpose(bias.reshape(C_HEADS, C_BLOCK, 3, C_BLOCK), (0, 2, 1, 3))


def _mla(cq, ckv, kr, gq, gkv, wuq, wukv, cos, sin):
    s = cq.shape[0]
    q = linear(rmsnorm(cq, gq, ACT_DTYPE, "rms_cq"), wuq, name="a_wuq").reshape(s, A_HEADS, A_NOPE + A_ROPE)
    q = jnp.concatenate([q[..., :A_NOPE], _rope(q[..., A_NOPE:], cos, sin)], axis=-1)
    kv = linear(rmsnorm(ckv, gkv, ACT_DTYPE, "rms_ckv"), wukv, name="a_wukv").reshape(s, A_HEADS, A_NOPE + A_V)
    k_rope = jnp.broadcast_to(_rope(kr[:, None, :], cos, sin), (s, A_HEADS, A_ROPE))
    k = jnp.concatenate([kv[..., :A_NOPE], k_rope], axis=-1)
    v = kv[..., A_NOPE:]
    return mla_attention(q, k, v, (A_NOPE + A_ROPE) ** -0.5)


def _hgrn2(q, f_fwd, f_bwd, i, g, lb_fwd, lb_bwd, g_out):
    s = q.shape[0]
    n_hp = B_HEADS // HG_GROUP
    pairs = lambda t: jnp.transpose(t.reshape(s, n_hp, -1), (1, 0, 2))
    qp, vp = pairs(q), pairs(i)
    o = None
    for z, lb, rev, tag in ((f_fwd, lb_fwd, False, "hgf"), (f_bwd, lb_bwd, True, "hgb")):
        qd, ki, ke, dec = hgrn_prep(qp, pairs(z), lb.astype(F32).reshape(n_hp, 1, -1), rev, tag + "_prep")
        part = hgrn_intra(qd, ki, vp, rev, tag + "_intra") + hgrn_inter(qd, ke, vp, dec, rev, tag + "_inter")
        o = part if o is None else o + part
    o = jnp.transpose(o, (1, 0, 2)).reshape(s * B_HEADS, B_DV)
    o = rmsnorm(o, g_out, F32, "rms_hg").reshape(s, B_HEADS * B_DV)
    return o * jax.nn.silu(g)


def _cross(x, h, mem_n, wq, wkv, wo):
    q = linear(h, wq, name="x_wq").reshape(h.shape[0], X_HEADS, X_DH)
    kv = linear(mem_n, wkv, name="x_wkv").reshape(mem_n.shape[0], 2, X_HEADS, X_DH)
    o_t = mla_attention(q, kv[:, 0], kv[:, 1], X_DH ** -0.5, name="cross")
    return linear_res(x, o_t, wo, name="x_wo", a_transposed=True)


def _pad_w_in(w):
    cut = A_Q_RANK + A_KV_RANK + A_ROPE
    return jnp.concatenate([w[:, :cut], jnp.zeros((w.shape[0], KR_PAD), w.dtype), w[:, cut:]], axis=1)


def _model_loss(p, x, mem, target):
    s = x.shape[0]
    cos, sin = _rope_tables(s)
    sm = jax.nn.softmax(p["b_lb"].astype(F32), axis=1)
    lower_bounds = jnp.cumsum(sm, axis=1) - sm[:, :1]
    bias = _window_bias(p["rel_bias"])
    for l in range(DEPTH):
        h = rmsnorm(x, p["g_mix"][l], ACT_DTYPE, "rms_mix")
        z = linear(h, _pad_w_in(p["w_in"][l]), name="w_in")
        parts, start = [], 0
        for width in IN_SPLITS_PADDED:
            parts.append(z[:, start:start + width])
            start += width
        a_cq, a_ckv, a_kr, b_q, b_ff, b_fb, b_i, b_g, c_q, c_k, c_v, gate_a, gate_b, gate_c = parts
        y_a = _mla(a_cq, a_ckv, a_kr[:, :A_ROPE], p["a_gq"][l], p["a_gkv"][l], p["a_wuq"][l], p["a_wukv"][l], cos, sin)
        y_b = _hgrn2(b_q, b_ff, b_fb, b_i, b_g, lower_bounds[0, l], lower_bounds[1, l], p["b_gout"][l])
        y_c = _unheads(window_attention(_heads(c_q, C_HEADS, C_DH, F32), _heads(c_k, C_KV_HEADS, C_DH, F32),
                                        _heads(c_v, C_KV_HEADS, C_DH, F32), bias, p["c_sink"][l]))
        x = gated_merge_out(x, y_a, y_b, y_c, gate_a, gate_b, gate_c, p["w_br_a"][l], p["w_br_b"][l], p["w_br_c"][l],
                            p["w_out"][l])
        h = rmsnorm(x, p["g_x"][l], ACT_DTYPE, "rms_x")
        x = _cross(x, h, rmsnorm(mem, p["g_mem"][l], ACT_DTYPE, "rms_mem"), p["x_wq"][l], p["x_wkv"][l], p["x_wo"][l])
        h = rmsnorm(x, p["g_ffn"][l], ACT_DTYPE, "rms_ffn")
        x = swiglu_ffn(x, h, p["f_w1"][l], p["f_w3"][l], p["f_w2"][l])
    y = rmsnorm(x, p["g_final"], F32, "rms_final")
    return loss_head(y, target)


def kernel(x, mem, w_in, g_mix, a_gq, a_gkv, a_wuq, a_wukv, b_lb, b_gout, c_sink, rel_bias, w_br_a, w_br_b, w_br_c, w_out, g_x, g_mem, x_wq, x_wkv, x_wo, g_ffn, f_w1, f_w3, f_w2, g_final, loss_target, m_w_in, m_g_mix, m_a_gq, m_a_gkv, m_a_wuq, m_a_wukv, m_b_lb, m_b_gout, m_c_sink, m_rel_bias, m_w_br_a, m_w_br_b, m_w_br_c, m_w_out, m_g_x, m_g_mem, m_x_wq, m_x_wkv, m_x_wo, m_g_ffn, m_f_w1, m_f_w3, m_f_w2, m_g_final, v_w_in, v_g_mix, v_a_gq, v_a_gkv, v_a_wuq, v_a_wukv, v_b_lb, v_b_gout, v_c_sink, v_rel_bias, v_w_br_a, v_w_br_b, v_w_br_c, v_w_out, v_g_x, v_g_mem, v_x_wq, v_x_wkv, v_x_wo, v_g_ffn, v_f_w1, v_f_w3, v_f_w2, v_g_final):
    given = dict(locals())
    w = {n: given[n] for n in WEIGHT_ORDER}
    m = {n: given["m_" + n] for n in WEIGHT_ORDER}
    v = {n: given["v_" + n] for n in WEIGHT_ORDER}
    sh_names = [n for n, _ in SHARDED]
    rep_shapes = [w[n].shape for n in REPLICATED] + [(1,)]
    rep_rows = _pack_rows([int(np.prod(s)) for s in rep_shapes])

    wire = [w[n] if n in ELEMENTWISE_SHARDED else w[n].astype(MXU_DTYPE) for n in sh_names]
    gathered = all_gather_shards(wire)
    full = {n: _shards_to_full(t, ax).astype(F32) for (n, ax), t in zip(SHARDED, gathered)}
    full.update({n: w[n] for n in REPLICATED})

    loss, (grad_full, grad_x) = jax.value_and_grad(_model_loss, argnums=(0, 1))(full, x[0], mem[0], loss_target[0])

    received = all_to_all_blocks([_full_to_shards(grad_full[n], ax).astype(GRAD_WIRE_DTYPE) for n, ax in SHARDED])
    g_sh, d_sh, nm_sh, nv_sh = {}, {}, {}, {}
    for n, got in zip(sh_names, received):
        shp = w[n].shape
        rows = lambda t: t.reshape(-1, shp[-1])
        outs = adamw_rows(got.reshape(N_DEV, -1, shp[-1]), rows(w[n]), rows(m[n]), rows(v[n]), "adamw_" + n)
        g_sh[n], d_sh[n], nm_sh[n], nv_sh[n] = [o.reshape(shp) for o in outs]

    mine = _pack([grad_full[n] for n in REPLICATED] + [loss.reshape(1)], rep_rows)
    everyone = all_gather_small(mine)
    rep_w = [w[n] for n in REPLICATED] + [jnp.zeros((1,), F32)]
    outs = adamw_rows(everyone, _pack(rep_w, rep_rows), _pack([m[n] for n in REPLICATED] + [jnp.zeros((1,), F32)], rep_rows),
                      _pack([v[n] for n in REPLICATED] + [jnp.ones((1,), F32)], rep_rows), "adamw_replicated")
    rep_names = list(REPLICATED) + ["loss"]
    g_rp, d_rp, nm_rp, nv_rp = [dict(zip(rep_names, _unpack(o, rep_shapes))) for o in outs]

    def pick(sharded, replicated, n):
        return sharded[n] if n in sharded else replicated[n]

    return (g_rp["loss"].reshape(()), grad_x[None],
            *[pick(g_sh, g_rp, n) for n in WEIGHT_ORDER], *[pick(d_sh, d_rp, n) for n in WEIGHT_ORDER],
            *[pick(nm_sh, nm_rp, n) for n in WEIGHT_ORDER], *[pick(nv_sh, nv_rp, n) for n in WEIGHT_ORDER])
```

```python
import functools
import math

import jax
import jax.numpy as jnp
import numpy as np
from jax import lax
from jax.experimental import pallas as pl
from jax.experimental.pallas import tpu as pltpu

F32 = jnp.float32
MXU_DTYPE = jnp.bfloat16
ACT_DTYPE = jnp.bfloat16
GRAD_WIRE_DTYPE = jnp.bfloat16

V7X_VMEM_LIMIT_BYTES = 56 * 1024 * 1024
LANES = 128
SUBLANES = 8

N_DEV = 8
D_MODEL = 1024
DEPTH = 2
EPS = 1e-6
TINY = 1e-30
NEG = -1e30

A_HEADS, A_NOPE, A_ROPE, A_V, A_Q_RANK, A_KV_RANK = 8, 64, 32, 64, 384, 256
ROPE_THETA = 10000.0
B_HEADS, B_DK, B_DV, B_CHUNK = 8, 128, 64, 16
C_HEADS, C_KV_HEADS, C_DH, C_WINDOW, C_BLOCK = 8, 2, 64, 128, 128
REL_BUCKETS, REL_MAX_DIST = 32, 128
X_HEADS, X_DH = 4, 256
D_FF = 2816
IN_SPLITS = (A_Q_RANK, A_KV_RANK, A_ROPE, 1024, 1024, 1024, 512, 512, 512, 128, 128, 1024, 1024, 1024)
IN_WIDTH = sum(IN_SPLITS)
KR_PAD = LANES - A_ROPE
IN_SPLITS_PADDED = (A_Q_RANK, A_KV_RANK, LANES, 1024, 1024, 1024, 512, 512, 512, 128, 128, 1024, 1024, 1024)

ADAM_LR, ADAM_B1, ADAM_B2, ADAM_EPS, ADAM_WD, ADAM_STEP = 0.001, 0.9, 0.999, 1e-08, 0.01, 10

SHARDED = (("w_in", 2), ("a_wuq", 2), ("a_wukv", 2), ("b_lb", 2), ("w_br_a", 2), ("w_br_b", 2), ("w_br_c", 2),
           ("w_out", 1), ("x_wq", 1), ("x_wkv", 2), ("x_wo", 1), ("f_w1", 2), ("f_w3", 2), ("f_w2", 1))
ELEMENTWISE_SHARDED = ("b_lb",)
REPLICATED = ("g_mix", "a_gq", "a_gkv", "b_gout", "c_sink", "rel_bias", "g_x", "g_mem", "g_ffn", "g_final")
WEIGHT_ORDER = ("w_in", "g_mix", "a_gq", "a_gkv", "a_wuq", "a_wukv", "b_lb", "b_gout", "c_sink", "rel_bias", "w_br_a",
                "w_br_b", "w_br_c", "w_out", "g_x", "g_mem", "x_wq", "x_wkv", "x_wo", "g_ffn", "f_w1", "f_w3", "f_w2",
                "g_final")
PACK_QUANTUM = SUBLANES * LANES
PACK_ROW_TILE = 512


def _pallas(body, **kw):
    return pl.pallas_call(body, **kw)


def _cparams(*sem):
    return pltpu.CompilerParams(dimension_semantics=sem, vmem_limit_bytes=V7X_VMEM_LIMIT_BYTES)


def _tile(n, target, mult=LANES):
    t = (min(target, n) // mult) * mult
    while t >= mult:
        if n % t == 0:
            return t
        t -= mult
    return n


def _dot(a, b, dims):
    return lax.dot_general(a.astype(MXU_DTYPE), b.astype(MXU_DTYPE), (dims, ((), ())), preferred_element_type=F32)


NN = ((1,), (0,))
NT = ((1,), (1,))
TN = ((0,), (0,))


MM_VMEM_BUDGET_BYTES = 40 * 1024 * 1024
MM_MAX_TILE = 4352
MM_MAX_ROW_TILE = 2048
MM_HBM_BYTES_PER_S = 2.5e12
MM_STEP_S = 0.4e-6
MM_DMA_ROW_OVERHEAD_BYTES = 512.0


def _tile_options(n, cap):
    out = [t for t in range(LANES, min(n, cap) + 1, LANES) if n % t == 0]
    if n <= cap and n not in out:
        out.append(n)
    return out or [n]


@functools.lru_cache(maxsize=None)
def _mm_plan(m, n, k, ta, tb, a_bytes, b_bytes, o_bytes):
    best = None
    for tk in _tile_options(k, MM_MAX_TILE):
        nk = k // tk
        for tn in _tile_options(n, MM_MAX_TILE):
            for tm in _tile_options(m, MM_MAX_ROW_TILE):
                vmem = 2 * (tm * tk * a_bytes + tk * tn * b_bytes + tm * tn * o_bytes) + tm * tn * 4
                vmem += (tm * tk * 2 if a_bytes == 4 else 0) + (tk * tn * 2 if b_bytes == 4 else 0)
                if vmem > MM_VMEM_BUDGET_BYTES:
                    continue

                def eff(elems, nbytes):
                    return (elems * nbytes) / (elems * nbytes + MM_DMA_ROW_OVERHEAD_BYTES)

                ea, eb, eo = eff(tm if ta else tk, a_bytes), eff(tk if tb else tn, b_bytes), eff(tn, o_bytes)
                for order in ("mn", "nm"):
                    if nk == 1 and order == "nm":
                        a_tr, b_tr = m * k * a_bytes * (n // tn), k * n * b_bytes
                    elif nk == 1:
                        a_tr, b_tr = m * k * a_bytes, k * n * b_bytes * (m // tm)
                    else:
                        a_tr, b_tr = m * k * a_bytes * (n // tn), k * n * b_bytes * (m // tm)
                    steps = (m // tm) * (n // tn) * nk
                    cost = (a_tr / ea + b_tr / eb + m * n * o_bytes / eo) / MM_HBM_BYTES_PER_S + steps * MM_STEP_S
                    if best is None or cost < best[0]:
                        best = (cost, tm, tn, tk, order)
    assert best is not None, (m, n, k)
    return best[1:]


def _mm(a, b, ta=False, tb=False, out_dtype=F32, name="mm", res=None):
    m, k = (a.shape[1], a.shape[0]) if ta else a.shape
    kb, n = (b.shape[1], b.shape[0]) if tb else b.shape
    assert k == kb, (a.shape, b.shape, ta, tb)
    tm, tn, tk, order = _mm_plan(m, n, k, ta, tb, a.dtype.itemsize, b.dtype.itemsize, jnp.dtype(out_dtype).itemsize)
    nk = k // tk
    dims = ((0 if ta else 1,), (1 if tb else 0,))
    out_shape = jax.ShapeDtypeStruct((m, n), out_dtype)

    assert res is None or nk == 1, (name, k, tk)
    if nk == 1:
        def body(a_ref, b_ref, *rest):
            acc = _dot(a_ref[...], b_ref[...], dims)
            if res is not None:
                acc = rest[0][...] + acc
            rest[-1][...] = acc.astype(rest[-1].dtype)

        if order == "nm":
            mi, ni = (lambda j, i: i), (lambda j, i: j)
            grid = (n // tn, m // tm)
        else:
            mi, ni = (lambda i, j: i), (lambda i, j: j)
            grid = (m // tm, n // tn)
        a_spec = pl.BlockSpec((tk, tm), lambda p, q: (0, mi(p, q))) if ta else pl.BlockSpec((tm, tk), lambda p, q: (mi(p, q), 0))
        b_spec = pl.BlockSpec((tn, tk), lambda p, q: (ni(p, q), 0)) if tb else pl.BlockSpec((tk, tn), lambda p, q: (0, ni(p, q)))
        o_spec = pl.BlockSpec((tm, tn), lambda p, q: (mi(p, q), ni(p, q)))
        extra = [] if res is None else [res]
        return _pallas(body, name=name, grid=grid, in_specs=[a_spec, b_spec] + [o_spec] * len(extra), out_specs=o_spec,
                       out_shape=out_shape, compiler_params=_cparams("parallel", "parallel"))(a, b, *extra)

    direct = jnp.dtype(out_dtype) == jnp.dtype(F32)

    def body(a_ref, b_ref, o_ref, *scratch):
        acc_ref = o_ref if direct else scratch[0]
        kk = pl.program_id(2)

        @pl.when(kk == 0)
        def _():
            acc_ref[...] = jnp.zeros_like(acc_ref)

        acc_ref[...] += _dot(a_ref[...], b_ref[...], dims)

        if not direct:
            @pl.when(kk == nk - 1)
            def _():
                o_ref[...] = acc_ref[...].astype(o_ref.dtype)

    a_spec = pl.BlockSpec((tk, tm), lambda i, j, kk: (kk, i)) if ta else pl.BlockSpec((tm, tk), lambda i, j, kk: (i, kk))
    b_spec = pl.BlockSpec((tn, tk), lambda i, j, kk: (j, kk)) if tb else pl.BlockSpec((tk, tn), lambda i, j, kk: (kk, j))
    return _pallas(
        body, name=name, grid=(m // tm, n // tn, nk), in_specs=[a_spec, b_spec],
        out_specs=pl.BlockSpec((tm, tn), lambda i, j, kk: (i, j)), out_shape=out_shape,
        scratch_shapes=[] if direct else [pltpu.VMEM((tm, tn), F32)],
        compiler_params=_cparams("parallel", "parallel", "arbitrary"),
    )(a, b)


def linear(a, w, out_dtype=F32, name="lin"):
    @jax.custom_vjp
    def f(a, w):
        return _mm(a.astype(ACT_DTYPE), w.astype(MXU_DTYPE), out_dtype=out_dtype, name=name + "_fwd")

    def fwd(a, w):
        ab, wb = a.astype(ACT_DTYPE), w.astype(MXU_DTYPE)
        return _mm(ab, wb, out_dtype=out_dtype, name=name + "_fwd"), (ab, wb, jnp.zeros((0,), a.dtype))

    def bwd(res, g):
        ab, wb, like_a = res
        gb = g.astype(ACT_DTYPE)
        da = _mm(gb, wb, tb=True, out_dtype=like_a.dtype, name=name + "_dx")
        dw = _mm(ab, gb, ta=True, out_dtype=F32, name=name + "_dw")
        return da, dw

    f.defvjp(fwd, bwd)
    return f(a, w)


def linear_res(x, a, w, name="lin", a_transposed=False):
    @jax.custom_vjp
    def f(x, a, w):
        return _mm(a.astype(ACT_DTYPE), w.astype(MXU_DTYPE), ta=a_transposed, name=name + "_fwd", res=x)

    def fwd(x, a, w):
        ab, wb = a.astype(ACT_DTYPE), w.astype(MXU_DTYPE)
        return _mm(ab, wb, ta=a_transposed, name=name + "_fwd", res=x), (ab, wb, jnp.zeros((0,), a.dtype))

    def bwd(res, g):
        ab, wb, like_a = res
        gb = g.astype(ACT_DTYPE)
        if a_transposed:
            da = _mm(wb, gb, tb=True, out_dtype=like_a.dtype, name=name + "_dx")
        else:
            da = _mm(gb, wb, tb=True, out_dtype=like_a.dtype, name=name + "_dx")
        dw = _mm(ab, gb, ta=not a_transposed, out_dtype=F32, name=name + "_dw")
        return g, da, dw

    f.defvjp(fwd, bwd)
    return f(x, a, w)


FFN_ROW_TILE = 512
FFN_COL_TILE = 1408


def _sigmoid(a):
    return 1.0 / (1.0 + jnp.exp(-a))


def swiglu_ffn(x, h, w1, w3, w2, name="ffn"):
    m, d = h.shape
    f_dim = w1.shape[1]
    tm, tn = _tile(m, FFN_ROW_TILE), _tile(f_dim, FFN_COL_TILE)

    def up_body(h_ref, w1_ref, w3_ref, t_ref, a_ref, b_ref):
        hv = h_ref[...]
        a = _dot(hv, w1_ref[...], NN)
        b = _dot(hv, w3_ref[...], NN)
        a_ref[...] = a.astype(a_ref.dtype)
        b_ref[...] = b.astype(b_ref.dtype)
        t_ref[...] = (a * _sigmoid(a) * b).astype(t_ref.dtype)

    def dt_body(g_ref, w2_ref, a_ref, b_ref, da_ref, db_ref):
        dt = _dot(g_ref[...], w2_ref[...], NT)
        a, b = a_ref[...].astype(F32), b_ref[...].astype(F32)
        sg = _sigmoid(a)
        da_ref[...] = (dt * b * (sg * (1.0 + a * (1.0 - sg)))).astype(da_ref.dtype)
        db_ref[...] = (dt * (a * sg)).astype(db_ref.dtype)

    row = pl.BlockSpec((tm, d), lambda j, i: (i, 0))
    w_up = pl.BlockSpec((d, tn), lambda j, i: (0, j))
    w_dn = pl.BlockSpec((tn, d), lambda j, i: (j, 0))
    tile = pl.BlockSpec((tm, tn), lambda j, i: (i, j))
    grid = (f_dim // tn, m // tm)

    def run_up(hb, w1b, w3b):
        return _pallas(up_body, name=name + "_up", grid=grid, in_specs=[row, w_up, w_up], out_specs=[tile, tile, tile],
                       out_shape=[jax.ShapeDtypeStruct((m, f_dim), ACT_DTYPE)] * 3,
                       compiler_params=_cparams("parallel", "parallel"))(hb, w1b, w3b)

    def forward(x, h, w1, w3, w2):
        hb = h.astype(ACT_DTYPE)
        w1b, w3b, w2b = w1.astype(MXU_DTYPE), w3.astype(MXU_DTYPE), w2.astype(MXU_DTYPE)
        t, a, b = run_up(hb, w1b, w3b)
        return _mm(t, w2b, name=name + "_down", res=x), (hb, w1b, w3b, w2b, t, a, b, jnp.zeros((0,), h.dtype))

    @jax.custom_vjp
    def f(x, h, w1, w3, w2):
        return forward(x, h, w1, w3, w2)[0]

    def bwd(res, g):
        hb, w1b, w3b, w2b, t, a, b, like_h = res
        gb = g.astype(ACT_DTYPE)
        da, db = _pallas(dt_body, name=name + "_dt", grid=grid, in_specs=[row, w_dn, tile, tile], out_specs=[tile, tile],
                         out_shape=[jax.ShapeDtypeStruct((m, f_dim), ACT_DTYPE)] * 2,
                         compiler_params=_cparams("parallel", "parallel"))(gb, w2b, a, b)
        dw2 = _mm(t, gb, ta=True, name=name + "_dw2")
        def dh_body(da_ref, db_ref, w1_ref, w3_ref, o_ref):
            o_ref[...] = (_dot(da_ref[...], w1_ref[...], NT) + _dot(db_ref[...], w3_ref[...], NT)).astype(o_ref.dtype)

        wide = pl.BlockSpec((tm, f_dim), lambda i: (i, 0))
        w_all = pl.BlockSpec((d, f_dim), lambda i: (0, 0))
        dh = _pallas(dh_body, name=name + "_dx", grid=(m // tm,), in_specs=[wide, wide, w_all, w_all],
                     out_specs=pl.BlockSpec((tm, d), lambda i: (i, 0)), out_shape=jax.ShapeDtypeStruct((m, d), like_h.dtype),
                     compiler_params=_cparams("parallel"))(da, db, w1b, w3b)
        dw1 = _mm(hb, da, ta=True, name=name + "_dw1")
        dw3 = _mm(hb, db, ta=True, name=name + "_dw3")
        return g, dh, dw1, dw3, dw2

    f.defvjp(lambda *args: forward(*args), bwd)
    return f(x, h, w1, w3, w2)


MERGE_ROW_TILE = 256


def gated_merge_out(x, ya_t, yb, yc, ga, gb, gc, wa, wb, wc, wo, name="merge"):
    s, d = x.shape
    e = yb.shape[1]
    tm = _tile(s, MERGE_ROW_TILE)

    def branches(ya_ref, yb_ref, yc_ref, wa_ref, wb_ref, wc_ref):
        return (_dot(ya_ref[...], wa_ref[...], TN), _dot(yb_ref[...], wb_ref[...], NN), _dot(yc_ref[...], wc_ref[...], NN))

    def fwd_body(x_ref, ya_ref, yb_ref, yc_ref, ga_ref, gb_ref, gc_ref, wa_ref, wb_ref, wc_ref, wo_ref, o_ref, m_ref):
        pa, pb, pc = branches(ya_ref, yb_ref, yc_ref, wa_ref, wb_ref, wc_ref)
        merged = _sigmoid(ga_ref[...]) * pa + _sigmoid(gb_ref[...]) * pb + _sigmoid(gc_ref[...]) * pc
        mb = merged.astype(m_ref.dtype)
        m_ref[...] = mb
        o_ref[...] = x_ref[...] + _dot(mb, wo_ref[...], NN)

    def bwd_body(g_ref, ya_ref, yb_ref, yc_ref, ga_ref, gb_ref, gc_ref, wa_ref, wb_ref, wc_ref, wo_ref,
                 dga_ref, dgb_ref, dgc_ref, dpa_ref, dpb_ref, dpc_ref):
        dm = _dot(g_ref[...], wo_ref[...], NT)
        ps = branches(ya_ref, yb_ref, yc_ref, wa_ref, wb_ref, wc_ref)
        for p_i, gate_ref, dg_ref, dp_ref in zip(ps, (ga_ref, gb_ref, gc_ref), (dga_ref, dgb_ref, dgc_ref),
                                                 (dpa_ref, dpb_ref, dpc_ref)):
            sg = _sigmoid(gate_ref[...])
            dg_ref[...] = dm * p_i * (sg * (1.0 - sg))
            dp_ref[...] = (dm * sg).astype(dp_ref.dtype)

    rows = lambda width: pl.BlockSpec((tm, width), lambda i: (i, 0))
    cols_t = pl.BlockSpec((e, tm), lambda i: (0, i))
    whole = lambda r, c: pl.BlockSpec((r, c), lambda i: (0, 0))
    in_common = [cols_t, rows(e), rows(e), rows(d), rows(d), rows(d), whole(e, d), whole(e, d), whole(e, d), whole(d, d)]

    def forward(x, ya_t, yb, yc, ga, gb, gc, wa, wb, wc, wo):
        cast = lambda t: t.astype(ACT_DTYPE)
        ops = (cast(ya_t), cast(yb), cast(yc), ga, gb, gc, cast(wa), cast(wb), cast(wc), cast(wo))
        out, merged = _pallas(
            fwd_body, name=name + "_fwd", grid=(s // tm,), in_specs=[rows(d)] + in_common, out_specs=[rows(d), rows(d)],
            out_shape=[jax.ShapeDtypeStruct((s, d), F32), jax.ShapeDtypeStruct((s, d), ACT_DTYPE)],
            compiler_params=_cparams("parallel"))(x, *ops)
        like = tuple(jnp.zeros((0,), t.dtype) for t in (ya_t, yb, yc))
        return out, (ops, merged, like)

    @jax.custom_vjp
    def f(*args):
        return forward(*args)[0]

    def bwd(res, g):
        ops, merged, like = res
        ya_b, yb_b, yc_b, ga, gb, gc, wa_b, wb_b, wc_b, wo_b = ops
        gbf = g.astype(ACT_DTYPE)
        gate_ct = jax.ShapeDtypeStruct((s, d), F32)
        branch_ct = jax.ShapeDtypeStruct((s, d), ACT_DTYPE)
        dga, dgb, dgc, dpa, dpb, dpc = _pallas(
            bwd_body, name=name + "_bwd", grid=(s // tm,), in_specs=[rows(d)] + in_common, out_specs=[rows(d)] * 6,
            out_shape=[gate_ct] * 3 + [branch_ct] * 3, compiler_params=_cparams("parallel"))(gbf, *ops)
        dya_t = _mm(wa_b, dpa, tb=True, out_dtype=like[0].dtype, name=name + "_dya")
        dyb = _mm(dpb, wb_b, tb=True, out_dtype=like[1].dtype, name=name + "_dyb")
        dyc = _mm(dpc, wc_b, tb=True, out_dtype=like[2].dtype, name=name + "_dyc")
        dwa = _mm(ya_b, dpa, name=name + "_dwa")
        dwb = _mm(yb_b, dpb, ta=True, name=name + "_dwb")
        dwc = _mm(yc_b, dpc, ta=True, name=name + "_dwc")
        dwo = _mm(merged, gbf, ta=True, name=name + "_dwo")
        return g, dya_t, dyb, dyc, dga, dgb, dgc, dwa, dwb, dwc, dwo

    f.defvjp(lambda *args: forward(*args), bwd)
    return f(x, ya_t, yb, yc, ga, gb, gc, wa, wb, wc, wo)


def _row_tile(rows, width):
    return _tile(rows, max(SUBLANES, (512 * 1024) // width), 16)


def rmsnorm(x, g, out_dtype=F32, name="rms"):
    rows, d = x.shape
    tr = _row_tile(rows, d)
    n_steps = rows // tr

    def fwd_body(x_ref, g_ref, o_ref):
        xv = x_ref[...].astype(F32)
        r = lax.rsqrt(jnp.mean(xv * xv, axis=-1, keepdims=True) + EPS)
        o_ref[...] = (xv * r * g_ref[...]).astype(o_ref.dtype)

    def bwd_body(x_ref, g_ref, dy_ref, dx_ref, dg_ref):
        xv = x_ref[...].astype(F32)
        dy = dy_ref[...].astype(F32)
        r = lax.rsqrt(jnp.mean(xv * xv, axis=-1, keepdims=True) + EPS)
        xh = xv * r
        dxh = dy * g_ref[...]
        dx_ref[...] = (r * (dxh - xh * jnp.mean(dxh * xh, axis=-1, keepdims=True))).astype(dx_ref.dtype)

        @pl.when(pl.program_id(0) == 0)
        def _():
            dg_ref[...] = jnp.zeros_like(dg_ref)

        dg_ref[...] += jnp.sum(dy * xh, axis=0, keepdims=True)

    row_spec = pl.BlockSpec((tr, d), lambda i: (i, 0))
    vec_spec = pl.BlockSpec((1, d), lambda i: (0, 0))

    def run_fwd(x, g):
        return _pallas(fwd_body, name=name + "_fwd", grid=(n_steps,), in_specs=[row_spec, vec_spec], out_specs=row_spec,
                       out_shape=jax.ShapeDtypeStruct((rows, d), out_dtype), compiler_params=_cparams("parallel"))(
            x, g.reshape(1, d).astype(F32))

    @jax.custom_vjp
    def f(x, g):
        return run_fwd(x, g)

    def fwd(x, g):
        return run_fwd(x, g), (x, g)

    def bwd(res, dy):
        x, g = res
        dx, dg = _pallas(
            bwd_body, name=name + "_bwd", grid=(n_steps,), in_specs=[row_spec, vec_spec, row_spec],
            out_specs=[row_spec, vec_spec],
            out_shape=[jax.ShapeDtypeStruct((rows, d), x.dtype), jax.ShapeDtypeStruct((1, d), F32)],
            compiler_params=_cparams("arbitrary"))(x, g.reshape(1, d).astype(F32), dy)
        return dx, dg.reshape(g.shape).astype(g.dtype)

    f.defvjp(fwd, bwd)
    return f(x, g)


def _rowdot(a, b, name):
    h, s, d = a.shape
    ts = _tile(s, 2048)

    def body(a_ref, b_ref, o_ref):
        o_ref[...] = jnp.sum(a_ref[...].astype(F32) * b_ref[...].astype(F32), axis=-1, keepdims=True)

    spec = pl.BlockSpec((None, ts, d), lambda hh, i: (hh, i, 0))
    return _pallas(body, name=name, grid=(h, s // ts), in_specs=[spec, spec],
                   out_specs=pl.BlockSpec((None, ts, 1), lambda hh, i: (hh, i, 0)),
                   out_shape=jax.ShapeDtypeStruct((h, s, 1), F32), compiler_params=_cparams("parallel", "parallel"))(a, b)


LOG2E = 1.4426950408889634


def mla_attention(q, k, v, scale, name="mla"):
    s, h, d = q.shape
    sk, dv = k.shape[0], v.shape[2]
    tq, tk = _tile(s, MLA_BWD_TQ), _tile(sk, MLA_TK)
    nq, nk = s // tq, sk // tk
    tqf, tkf = _tile(s, MLA_FWD_TQ), _tile(sk, MLA_FWD_TK)
    nkf = sk // tkf
    ones_rows = 16
    c = scale * LOG2E

    def fwd_body(qt_ref, k_ref, vt_ref, ot_ref, lse_ref, m_ref, acc_ref):
        j = pl.program_id(2)

        @pl.when(j == 0)
        def _():
            m_ref[...] = jnp.full_like(m_ref, NEG)
            acc_ref[...] = jnp.zeros_like(acc_ref)

        st = _dot(k_ref[...], qt_ref[...], NN)
        m_prev = m_ref[...]
        m_new = jnp.maximum(m_prev, jnp.max(st, axis=0, keepdims=True) * c)
        pt = jnp.exp2(st * c - m_new)
        acc_ref[...] = jnp.exp2(m_prev - m_new) * acc_ref[...] + _dot(vt_ref[...], pt, NN)
        m_ref[...] = m_new

        @pl.when(j == nkf - 1)
        def _():
            l = acc_ref[dv:dv + 1, :]
            ot_ref[...] = (acc_ref[:dv, :] / l).astype(ot_ref.dtype)
            lse_ref[...] = m_ref[...] + jnp.log2(l)

    def delta_body(ot_ref, dot_ref, o_ref):
        o_ref[...] = jnp.sum(ot_ref[...].astype(F32) * dot_ref[...].astype(F32), axis=0, keepdims=True)

    def bwd_body(qt_ref, k_ref, kt_ref, v_ref, dot_ref, lse_ref, dl_ref, dqt_ref, dk_hbm, dv_hbm, dq_acc, dk_acc, dv_acc):
        hh, i, j = pl.program_id(0), pl.program_id(1), pl.program_id(2)

        @pl.when(j == 0)
        def _():
            dq_acc[...] = jnp.zeros_like(dq_acc)

        @pl.when(i == 0)
        def _():
            dk_acc[j] = jnp.zeros((d, tk), F32)
            dv_acc[j] = jnp.zeros((dv, tk), F32)

        qt, dot_ = qt_ref[...], dot_ref[...]
        pt = jnp.exp2(_dot(k_ref[...], qt, NN) * c - lse_ref[...])
        dst = (pt * (_dot(v_ref[...], dot_, NN) - dl_ref[...])).astype(MXU_DTYPE)
        dv_acc[j] += _dot(dot_, pt, NT)
        dk_acc[j] += _dot(qt, dst, NT)
        dq_acc[...] += _dot(kt_ref[...], dst, NN)

        @pl.when(j == nk - 1)
        def _():
            dqt_ref[...] = dq_acc[...] * scale

        @pl.when(i == nq - 1)
        def _():
            dk_acc[j] = dk_acc[j] * scale
            pltpu.sync_copy(dk_acc.at[j], dk_hbm.at[hh, j])
            pltpu.sync_copy(dv_acc.at[j], dv_hbm.at[hh, j])

    def qt_spec(width):
        return pl.BlockSpec((None, width, tq), lambda hh, i, j: (hh, 0, i))

    def kt_spec(width):
        return pl.BlockSpec((None, width, tk), lambda hh, i, j: (hh, 0, j))

    def k_spec(width):
        return pl.BlockSpec((None, tk, width), lambda hh, i, j: (hh, j, 0))

    def layouts(q, k, v):
        cast = lambda t: t.astype(ACT_DTYPE)
        return (cast(jnp.transpose(q, (1, 2, 0))), cast(jnp.transpose(k, (1, 0, 2))), cast(jnp.transpose(k, (1, 2, 0))),
                cast(jnp.transpose(v, (1, 0, 2))), cast(jnp.transpose(v, (1, 2, 0))))

    def run_fwd(qt, kh, vt):
        vt_ones = jnp.concatenate([vt, jnp.ones((h, ones_rows, sk), vt.dtype)], axis=1)

        def qf_spec(width):
            return pl.BlockSpec((None, width, tqf), lambda hh, i, j: (hh, 0, i))

        kf_spec = pl.BlockSpec((None, tkf, d), lambda hh, i, j: (hh, j, 0))
        vf_spec = pl.BlockSpec((None, dv + ones_rows, tkf), lambda hh, i, j: (hh, 0, j))
        return _pallas(
            fwd_body, name=name + "_fwd", grid=(h, s // tqf, nkf), in_specs=[qf_spec(d), kf_spec, vf_spec],
            out_specs=[qf_spec(dv), qf_spec(1)],
            out_shape=[jax.ShapeDtypeStruct((h, dv, s), ACT_DTYPE), jax.ShapeDtypeStruct((h, 1, s), F32)],
            scratch_shapes=[pltpu.VMEM((1, tqf), F32), pltpu.VMEM((dv + ones_rows, tqf), F32)],
            compiler_params=_cparams("parallel", "parallel", "arbitrary"))(qt, kh, vt_ones)

    @jax.custom_vjp
    def f(q, k, v):
        qt, kh, _, _, vt = layouts(q, k, v)
        return run_fwd(qt, kh, vt)[0].reshape(h * dv, s)

    def fwd(q, k, v):
        qt, kh, kt, vh, vt = layouts(q, k, v)
        ot, lse = run_fwd(qt, kh, vt)
        return ot.reshape(h * dv, s), (qt, kh, kt, vh, ot, lse)

    def bwd(res, dy):
        qt, kh, kt, vh, ot, lse = res
        dot_ = dy.reshape(h, dv, s)
        ts = _tile(s, 2048)
        col = pl.BlockSpec((None, dv, ts), lambda hh, i: (hh, 0, i))
        delta = _pallas(delta_body, name=name + "_delta", grid=(h, s // ts), in_specs=[col, col],
                        out_specs=pl.BlockSpec((None, 1, ts), lambda hh, i: (hh, 0, i)),
                        out_shape=jax.ShapeDtypeStruct((h, 1, s), F32), compiler_params=_cparams("parallel", "parallel"))(ot, dot_)
        any_spec = pl.BlockSpec(memory_space=pl.ANY)
        dqt, dkt, dvt = _pallas(
            bwd_body, name=name + "_bwd", grid=(h, nq, nk),
            in_specs=[qt_spec(d), k_spec(d), kt_spec(d), k_spec(dv), qt_spec(dv), qt_spec(1), qt_spec(1)],
            out_specs=[qt_spec(d), any_spec, any_spec],
            out_shape=[jax.ShapeDtypeStruct((h, d, s), F32), jax.ShapeDtypeStruct((h, nk, d, tk), F32),
                       jax.ShapeDtypeStruct((h, nk, dv, tk), F32)],
            scratch_shapes=[pltpu.VMEM((d, tq), F32), pltpu.VMEM((nk, d, tk), F32), pltpu.VMEM((nk, dv, tk), F32)],
            compiler_params=_cparams("parallel", "arbitrary", "arbitrary"))(qt, kh, kt, vh, dot_, lse, delta)
        to_tokens = lambda t: jnp.transpose(t, (1, 3, 0, 2)).reshape(sk, h, t.shape[2])
        return jnp.transpose(dqt, (2, 0, 1)), to_tokens(dkt), to_tokens(dvt)

    f.defvjp(fwd, bwd)
    return f(q, k, v)


WATTN_TQ = 2 * C_BLOCK
WATTN_KW = WATTN_TQ + 2 * C_BLOCK


def window_attention(q, k, v, bias, sink, name="wattn"):
    hq, s, dh = q.shape
    g = hq // C_KV_HEADS
    tq, kw, half = WATTN_TQ, WATTN_KW, WATTN_KW // 2
    nt = s // tq
    scale = dh ** -0.5
    sink_b = jnp.broadcast_to(sink.astype(F32).reshape(hq, 1, 1), (hq, 1, LANES))
    neg = jnp.full((hq, C_BLOCK, C_BLOCK), NEG, F32)
    tile = jnp.concatenate(
        [jnp.concatenate([bias[:, cb - rb] if 0 <= cb - rb <= 2 else neg for cb in range(kw // C_BLOCK)], axis=2)
         for rb in range(tq // C_BLOCK)], axis=1)

    def key_bias(i):
        pos = lax.broadcasted_iota(jnp.int32, (1, kw), 1) + i * tq - C_BLOCK
        return jnp.where(jnp.logical_and(pos >= 0, pos < s), 0.0, NEG)

    def both(a_ref, b_ref):
        return jnp.concatenate([a_ref[...], b_ref[...]], axis=0)

    def fwd_body(q_ref, ka_ref, kb_ref, va_ref, vb_ref, b_ref, sk_ref, o_ref, lse_ref):
        kb_ = key_bias(pl.program_id(1))
        k_all, v_all = both(ka_ref, kb_ref), both(va_ref, vb_ref)
        for hh in range(g):
            sc = _dot(q_ref[hh], k_all, NT) * scale + b_ref[hh] + kb_
            snk = sk_ref[hh][:, :1]
            m = jnp.maximum(jnp.max(sc, axis=-1, keepdims=True), snk)
            p = jnp.exp(sc - m)
            l = jnp.sum(p, axis=-1, keepdims=True) + jnp.exp(snk - m)
            o_ref[hh] = (_dot(p, v_all, NN) / l).astype(o_ref.dtype)
            lse_ref[hh] = m + jnp.log(l)

    def bwd_body(q_ref, ka_ref, kb_ref, va_ref, vb_ref, b_ref, sk_ref, do_ref, lse_ref, dl_ref,
                 dq_ref, db_ref, dsink_ref, dk_hbm, dv_hbm, dk_acc, dv_acc):
        kv, i = pl.program_id(0), pl.program_id(1)

        @pl.when(i == 0)
        def _():
            dk_acc[...] = jnp.zeros_like(dk_acc)
            dv_acc[...] = jnp.zeros_like(dv_acc)
            db_ref[...] = jnp.zeros_like(db_ref)
            dsink_ref[...] = jnp.zeros_like(dsink_ref)

        kb_ = key_bias(i)
        k_all, v_all = both(ka_ref, kb_ref), both(va_ref, vb_ref)
        dk_t = jnp.zeros((kw, dh), F32)
        dv_t = jnp.zeros((kw, dh), F32)
        for hh in range(g):
            lse, dl, do = lse_ref[hh], dl_ref[hh], do_ref[hh]
            p = jnp.exp(_dot(q_ref[hh], k_all, NT) * scale + b_ref[hh] + kb_ - lse)
            ds = p * (_dot(do, v_all, NT) - dl)
            db_ref[hh] += ds
            total = jnp.broadcast_to(-jnp.sum(jnp.exp(sk_ref[hh][:, :1] - lse) * dl, axis=0, keepdims=True), (1, LANES))
            dsink_ref[hh] += jnp.where(lax.broadcasted_iota(jnp.int32, (1, LANES), 1) == 0, total, 0.0)
            dsb = (ds * scale).astype(MXU_DTYPE)
            dq_ref[hh] = _dot(dsb, k_all, NN).astype(dq_ref.dtype)
            dk_t += _dot(dsb, q_ref[hh], TN)
            dv_t += _dot(p, do, TN)
        rows = pl.ds(pl.multiple_of(i * tq, tq), kw)
        dk_acc[rows, :] += dk_t
        dv_acc[rows, :] += dv_t

        @pl.when(i == nt - 1)
        def _():
            pltpu.sync_copy(dk_acc, dk_hbm.at[kv])
            pltpu.sync_copy(dv_acc, dv_hbm.at[kv])

    def q_spec(width):
        return pl.BlockSpec((g, tq, width), lambda kv, i: (kv, i, 0))

    ka_spec = pl.BlockSpec((None, half, dh), lambda kv, i: (kv, i, 0))
    kb_spec = pl.BlockSpec((None, half, dh), lambda kv, i: (kv, i + 1, 0))
    b_spec = pl.BlockSpec((g, tq, kw), lambda kv, i: (kv, 0, 0))
    sk_spec = pl.BlockSpec((g, 1, LANES), lambda kv, i: (kv, 0, 0))

    def padded(t):
        return jnp.pad(t, ((0, 0), (C_BLOCK, C_BLOCK), (0, 0)))

    def run_fwd(q, kp, vp, tile, sink_b):
        return _pallas(
            fwd_body, name=name + "_fwd", grid=(C_KV_HEADS, nt),
            in_specs=[q_spec(dh), ka_spec, kb_spec, ka_spec, kb_spec, b_spec, sk_spec], out_specs=[q_spec(dh), q_spec(1)],
            out_shape=[jax.ShapeDtypeStruct((hq, s, dh), ACT_DTYPE), jax.ShapeDtypeStruct((hq, s, 1), F32)],
            compiler_params=_cparams("parallel", "parallel"))(q, kp, kp, vp, vp, tile, sink_b)

    @jax.custom_vjp
    def f(q, k, v, tile, sink_b):
        return run_fwd(q, padded(k), padded(v), tile, sink_b)[0]

    def fwd(q, k, v, tile, sink_b):
        kp, vp = padded(k), padded(v)
        o, lse = run_fwd(q, kp, vp, tile, sink_b)
        return o, (q, kp, vp, tile, sink_b, o, lse)

    def bwd(res, do):
        q, kp, vp, tile, sink_b, o, lse = res
        delta = _rowdot(o, do, name + "_delta")
        any_spec = pl.BlockSpec(memory_space=pl.ANY)
        acc = jax.ShapeDtypeStruct((C_KV_HEADS, s + 2 * C_BLOCK, dh), F32)
        dq, dtile, dsink, dkp, dvp = _pallas(
            bwd_body, name=name + "_bwd", grid=(C_KV_HEADS, nt),
            in_specs=[q_spec(dh), ka_spec, kb_spec, ka_spec, kb_spec, b_spec, sk_spec, q_spec(dh), q_spec(1), q_spec(1)],
            out_specs=[q_spec(dh), b_spec, sk_spec, any_spec, any_spec],
            out_shape=[jax.ShapeDtypeStruct((hq, s, dh), q.dtype), jax.ShapeDtypeStruct((hq, tq, kw), F32),
                       jax.ShapeDtypeStruct((hq, 1, LANES), F32), acc, acc],
            scratch_shapes=[pltpu.VMEM((s + 2 * C_BLOCK, dh), F32), pltpu.VMEM((s + 2 * C_BLOCK, dh), F32)],
            compiler_params=_cparams("parallel", "arbitrary"))(q, kp, kp, vp, vp, tile, sink_b, do, lse, delta)
        unpad = lambda t: t[:, C_BLOCK:-C_BLOCK].astype(kp.dtype)
        return dq, unpad(dkp), unpad(dvp), dtile, dsink

    f.defvjp(fwd, bwd)
    return f(q, k, v, tile, sink_b)


HG_PREP_ROWS = 256
HG_GROUP = 8
HG_INTRA_BLOCK = 256
HG_INTER_CHUNKS = 16
MLA_FWD_TQ, MLA_FWD_TK, MLA_BWD_TQ, MLA_TK = 2048, 2048, 2048, 1024


def _hdot(a, b, dims):
    b16 = jnp.bfloat16
    hi = b.astype(b16)
    rest = b - hi.astype(F32)
    mid = rest.astype(b16)
    lo = (rest - mid.astype(F32)).astype(b16)
    a16 = a.astype(b16)
    dn = (dims, ((), ()))
    return (lax.dot_general(a16, hi, dn, preferred_element_type=F32) + lax.dot_general(a16, mid, dn, preferred_element_type=F32)
            + lax.dot_general(a16, lo, dn, preferred_element_type=F32))


def hgrn_prep(q, z, lb, reverse, name):
    n_hp, s, tc = q.shape
    tb = _tile(s, HG_PREP_ROWS)
    ncb = tb // B_CHUNK

    def chunk_matrices():
        r = lax.broadcasted_iota(jnp.int32, (tb, tb), 0)
        cc = lax.broadcasted_iota(jnp.int32, (tb, tb), 1)
        same = r // B_CHUNK == cc // B_CHUNK
        tri = (cc >= r) if reverse else (cc <= r)
        cum = jnp.where(jnp.logical_and(same, tri), 1.0, 0.0).astype(F32)
        every = jnp.where(same, 1.0, 0.0).astype(F32)
        pr = lax.broadcasted_iota(jnp.int32, (ncb, tb), 0)
        pc = lax.broadcasted_iota(jnp.int32, (ncb, tb), 1)
        per_chunk = jnp.where(pc // B_CHUNK == pr, 1.0, 0.0).astype(F32)
        return cum, every, per_chunk

    def gates(zv, lbv):
        e = jnp.exp(-jnp.abs(zv))
        big, small = 1.0 / (1.0 + e), e / (1.0 + e)
        sig = jnp.where(zv >= 0, big, small)
        nsig = jnp.where(zv >= 0, small, big)
        f = lbv + (1.0 - lbv) * sig
        return sig, nsig, f, jnp.log(jnp.maximum(f, TINY)), (1.0 - lbv) * nsig

    def fwd_body(q_ref, z_ref, lb_ref, qd_ref, ki_ref, ke_ref, dec_ref):
        cum, every, per_chunk = chunk_matrices()
        _, _, _, lf, key = gates(z_ref[...], lb_ref[...])
        b = _hdot(cum, lf, NN)
        tot = _hdot(every, lf, NN)
        qd_ref[...] = q_ref[...] * jnp.exp(b)
        ki_ref[...] = key * jnp.exp(-b)
        ke_ref[...] = key * jnp.exp(tot - b)
        dec_ref[...] = jnp.exp(_hdot(per_chunk, lf, NN))

    def bwd_body(q_ref, z_ref, lb_ref, dqd_ref, dki_ref, dke_ref, ddec_ref, dq_ref, dz_ref, dlb_ref):
        cum, every, per_chunk = chunk_matrices()
        lbv = lb_ref[...]
        sig, nsig, f, lf, key = gates(z_ref[...], lbv)
        b = _hdot(cum, lf, NN)
        tot = _hdot(every, lf, NN)
        e_b, e_nb, e_tb = jnp.exp(b), jnp.exp(-b), jnp.exp(tot - b)
        dqd, dki, dke = dqd_ref[...], dki_ref[...], dke_ref[...]
        dq_ref[...] = dqd * e_b
        dkey = dki * e_nb + dke * e_tb
        t_end = dke * key * e_tb
        db = dqd * q_ref[...] * e_b - dki * key * e_nb - t_end
        dtot = ddec_ref[...] * jnp.exp(_hdot(per_chunk, lf, NN)) + _hdot(per_chunk, t_end, NN)
        dlf = _hdot(cum, db, TN) + _hdot(per_chunk, dtot, TN)
        df = jnp.where(f > TINY, dlf / f, 0.0)
        one_m_lb = 1.0 - lbv
        dz_ref[...] = (df - dkey) * one_m_lb * sig * nsig
        dlb_part = jnp.sum(df * nsig - dkey * nsig, axis=0, keepdims=True)

        @pl.when(pl.program_id(1) == 0)
        def _():
            dlb_ref[...] = jnp.zeros_like(dlb_ref)

        dlb_ref[...] += dlb_part

    tok = pl.BlockSpec((None, tb, tc), lambda j, i: (j, i, 0))
    vec = pl.BlockSpec((None, 1, tc), lambda j, i: (j, 0, 0))
    chk = pl.BlockSpec((None, ncb, tc), lambda j, i: (j, i, 0))
    grid = (n_hp, s // tb)
    tok_shape = jax.ShapeDtypeStruct((n_hp, s, tc), F32)
    chk_shape = jax.ShapeDtypeStruct((n_hp, s // B_CHUNK, tc), F32)

    def run_fwd(q, z, lb):
        return _pallas(fwd_body, name=name + "_fwd", grid=grid, in_specs=[tok, tok, vec], out_specs=[tok, tok, tok, chk],
                       out_shape=[tok_shape, tok_shape, tok_shape, chk_shape],
                       compiler_params=_cparams("parallel", "parallel"))(q, z, lb)

    @jax.custom_vjp
    def f(q, z, lb):
        return tuple(run_fwd(q, z, lb))

    def fwd(q, z, lb):
        return tuple(run_fwd(q, z, lb)), (q, z, lb)

    def bwd(res, cts):
        q, z, lb = res
        dq, dz, dlb = _pallas(
            bwd_body, name=name + "_bwd", grid=grid, in_specs=[tok, tok, vec, tok, tok, tok, chk], out_specs=[tok, tok, vec],
            out_shape=[tok_shape, tok_shape, jax.ShapeDtypeStruct((n_hp, 1, tc), F32)],
            compiler_params=_cparams("parallel", "arbitrary"))(q, z, lb, *cts)
        return dq, dz, dlb

    f.defvjp(fwd, bwd)
    return f(q, z, lb)


def _pair_cols(ref, hh, width):
    return ref[:, hh * width:(hh + 1) * width]


def hgrn_intra(qd, ki, v, reverse, name):
    s = qd.shape[1]
    tb = _tile(s, HG_INTRA_BLOCK)
    wk, wv = HG_GROUP * B_DK, HG_GROUP * B_DV

    def mask():
        r = lax.broadcasted_iota(jnp.int32, (tb, tb), 0)
        c = lax.broadcasted_iota(jnp.int32, (tb, tb), 1)
        return jnp.logical_and(r // B_CHUNK == c // B_CHUNK, (c >= r) if reverse else (c <= r))

    def fwd_body(q_ref, k_ref, v_ref, o_ref):
        msk = mask()
        for hh in range(HG_GROUP):
            sc = jnp.where(msk, _dot(_pair_cols(q_ref, hh, B_DK), _pair_cols(k_ref, hh, B_DK), NT), 0.0)
            o_ref[:, hh * B_DV:(hh + 1) * B_DV] = _dot(sc, _pair_cols(v_ref, hh, B_DV), NN)

    def bwd_body(q_ref, k_ref, v_ref, do_ref, dq_ref, dk_ref, dv_ref):
        msk = mask()
        for hh in range(HG_GROUP):
            q, k = _pair_cols(q_ref, hh, B_DK), _pair_cols(k_ref, hh, B_DK)
            vv, do = _pair_cols(v_ref, hh, B_DV), _pair_cols(do_ref, hh, B_DV)
            sc = jnp.where(msk, _dot(q, k, NT), 0.0)
            ds = jnp.where(msk, _dot(do, vv, NT), 0.0)
            dq_ref[:, hh * B_DK:(hh + 1) * B_DK] = _dot(ds, k, NN)
            dk_ref[:, hh * B_DK:(hh + 1) * B_DK] = _dot(ds, q, TN)
            dv_ref[:, hh * B_DV:(hh + 1) * B_DV] = _dot(sc, do, TN)

    ks = pl.BlockSpec((None, tb, wk), lambda hp, i: (hp, i, 0))
    vs = pl.BlockSpec((None, tb, wv), lambda hp, i: (hp, i, 0))
    grid = (B_HEADS // HG_GROUP, s // tb)

    def run_fwd(qd, ki, v):
        return _pallas(fwd_body, name=name + "_fwd", grid=grid, in_specs=[ks, ks, vs], out_specs=vs,
                       out_shape=jax.ShapeDtypeStruct(v.shape, F32), compiler_params=_cparams("parallel", "parallel"))(qd, ki, v)

    @jax.custom_vjp
    def f(qd, ki, v):
        return run_fwd(qd, ki, v)

    def fwd(qd, ki, v):
        return run_fwd(qd, ki, v), (qd, ki, v)

    def bwd(res, do):
        qd, ki, v = res
        return tuple(_pallas(
            bwd_body, name=name + "_bwd", grid=grid, in_specs=[ks, ks, vs, vs], out_specs=[ks, ks, vs],
            out_shape=[jax.ShapeDtypeStruct(qd.shape, F32), jax.ShapeDtypeStruct(ki.shape, F32),
                       jax.ShapeDtypeStruct(v.shape, F32)],
            compiler_params=_cparams("parallel", "parallel"))(qd, ki, v, do))

    f.defvjp(fwd, bwd)
    return f(qd, ki, v)


def hgrn_inter(qd, ke, v, dec, reverse, name):
    s = qd.shape[1]
    nc = s // B_CHUNK
    cpb = HG_INTER_CHUNKS if nc % HG_INTER_CHUNKS == 0 else nc
    tb = cpb * B_CHUNK
    nblk = nc // cpb
    wk, wv = HG_GROUP * B_DK, HG_GROUP * B_DV
    n_hp = B_HEADS // HG_GROUP

    def rows(c):
        return pl.ds(c * B_CHUNK, B_CHUNK)

    def kcols(hh):
        return slice(hh * B_DK, (hh + 1) * B_DK)

    def vcols(hh):
        return slice(hh * B_DV, (hh + 1) * B_DV)

    def order(flip):
        return reversed(range(cpb)) if flip else range(cpb)

    def fwd_body(q_ref, k_ref, v_ref, dec_ref, o_ref, st_ref, state):
        @pl.when(pl.program_id(1) == 0)
        def _():
            state[...] = jnp.zeros_like(state)

        for c in order(reverse):
            for hh in range(HG_GROUP):
                st = state[hh]
                st_ref[c, hh] = st
                o_ref[rows(c), vcols(hh)] = _dot(q_ref[rows(c), kcols(hh)], st, NT)
                state[hh] = st * dec_ref[pl.ds(c, 1), kcols(hh)] + _dot(v_ref[rows(c), vcols(hh)], k_ref[rows(c), kcols(hh)], TN)

    def bwd_body(q_ref, k_ref, v_ref, dec_ref, st_ref, do_ref, dq_ref, dk_ref, dv_ref, ddec_ref, dstate):
        @pl.when(pl.program_id(1) == 0)
        def _():
            dstate[...] = jnp.zeros_like(dstate)

        for c in order(not reverse):
            for hh in range(HG_GROUP):
                dst = dstate[hh]
                st = st_ref[c, hh]
                do_c = do_ref[rows(c), vcols(hh)]
                dk_ref[rows(c), kcols(hh)] = _dot(v_ref[rows(c), vcols(hh)], dst, NN)
                dv_ref[rows(c), vcols(hh)] = _dot(k_ref[rows(c), kcols(hh)], dst, NT)
                ddec_ref[pl.ds(c, 1), kcols(hh)] = jnp.sum(dst * st, axis=0, keepdims=True)
                dq_ref[rows(c), kcols(hh)] = _dot(do_c, st, NN)
                dstate[hh] = dst * dec_ref[pl.ds(c, 1), kcols(hh)] + _dot(do_c, q_ref[rows(c), kcols(hh)], TN)

    def specs(flip):
        blk = (lambda i: nblk - 1 - i) if flip else (lambda i: i)
        tok_k = pl.BlockSpec((None, tb, wk), lambda hp, i: (hp, blk(i), 0))
        tok_v = pl.BlockSpec((None, tb, wv), lambda hp, i: (hp, blk(i), 0))
        chk = pl.BlockSpec((None, cpb, wk), lambda hp, i: (hp, blk(i), 0))
        sts = pl.BlockSpec((None, cpb, HG_GROUP, B_DV, B_DK), lambda hp, i: (hp, blk(i), 0, 0, 0))
        return tok_k, tok_v, chk, sts

    scratch = [pltpu.VMEM((HG_GROUP, B_DV, B_DK), F32)]

    def run_fwd(qd, ke, v, dec):
        tok_k, tok_v, chk, sts = specs(reverse)
        return _pallas(
            fwd_body, name=name + "_fwd", grid=(n_hp, nblk), in_specs=[tok_k, tok_k, tok_v, chk], out_specs=[tok_v, sts],
            out_shape=[jax.ShapeDtypeStruct(v.shape, F32), jax.ShapeDtypeStruct((n_hp, nc, HG_GROUP, B_DV, B_DK), F32)],
            scratch_shapes=scratch, compiler_params=_cparams("parallel", "arbitrary"))(qd, ke, v, dec)

    @jax.custom_vjp
    def f(qd, ke, v, dec):
        return run_fwd(qd, ke, v, dec)[0]

    def fwd(qd, ke, v, dec):
        o, st = run_fwd(qd, ke, v, dec)
        return o, (qd, ke, v, dec, st)

    def bwd(res, do):
        qd, ke, v, dec, st = res
        tok_k, tok_v, chk, sts = specs(not reverse)
        return tuple(_pallas(
            bwd_body, name=name + "_bwd", grid=(n_hp, nblk), in_specs=[tok_k, tok_k, tok_v, chk, sts, tok_v],
            out_specs=[tok_k, tok_k, tok_v, chk],
            out_shape=[jax.ShapeDtypeStruct(qd.shape, F32), jax.ShapeDtypeStruct(ke.shape, F32),
                       jax.ShapeDtypeStruct(v.shape, F32), jax.ShapeDtypeStruct(dec.shape, F32)],
            scratch_shapes=scratch, compiler_params=_cparams("parallel", "arbitrary"))(qd, ke, v, dec, st, do))

    f.defvjp(fwd, bwd)
    return f(qd, ke, v, dec)


def loss_head(y, target, name="loss"):
    s, d = y.shape
    tr = _row_tile(s, d)

    def body(y_ref, t_ref, o_ref):
        @pl.when(pl.program_id(0) == 0)
        def _():
            o_ref[...] = jnp.zeros_like(o_ref)

        e = y_ref[...] - t_ref[...]
        part = jnp.sum(jnp.sum(e * e, axis=-1, keepdims=True), axis=0, keepdims=True) * (0.5 / d)
        o_ref[...] += jnp.broadcast_to(part, o_ref.shape)

    spec = pl.BlockSpec((tr, d), lambda i: (i, 0))

    def run(y, t):
        out = _pallas(body, name=name, grid=(s // tr,), in_specs=[spec, spec],
                      out_specs=pl.BlockSpec((SUBLANES, LANES), lambda i: (0, 0)),
                      out_shape=jax.ShapeDtypeStruct((SUBLANES, LANES), F32), compiler_params=_cparams("arbitrary"))(y, t)
        return out[0, 0]

    @jax.custom_vjp
    def f(y, t):
        return run(y, t)

    def fwd(y, t):
        return run(y, t), (y, t)

    def bwd(res, g):
        y, t = res
        dy = g * (y - t) * (1.0 / d)
        return dy, -dy

    f.defvjp(fwd, bwd)
    return f(y, target)


def _mesh_pos():
    return lax.axis_index("x"), lax.axis_index("y"), lax.axis_index("c")


def all_gather_shards(shards):
    n = len(shards)

    def body(*refs):
        ins, outs = refs[:n], refs[n:2 * n]
        send_sems, recv_sems, local_sems = refs[2 * n:]
        x, y, c = _mesh_pos()
        me, sibling = (x, y, c), (x, y, 1 - c)
        chips = [(1 - x, y), (x, 1 - y), (1 - x, 1 - y)]

        def slot(t, px, py, pc):
            return outs[t].at[4 * px + 2 * py + pc]

        def copy(t, k, block, to, src=None):
            return pltpu.make_async_remote_copy(
                src_ref=slot(t, *block) if src is None else src, dst_ref=slot(t, *block), send_sem=send_sems.at[t, k],
                recv_sem=recv_sems.at[t, k], device_id=to, device_id_type=pl.DeviceIdType.MESH)

        mine = [pltpu.make_async_copy(ins[t], slot(t, *me), local_sems.at[t]) for t in range(n)]
        for cp in mine:
            cp.start()
        first = []
        for t in range(n):
            first.append(copy(t, 0, me, sibling, src=ins[t]))
            first += [copy(t, 1 + j, me, (*chip, c), src=ins[t]) for j, chip in enumerate(chips)]
        for cp in first:
            cp.start()
        passed = []
        for j, chip in enumerate(chips):
            for t in range(n):
                copy(t, 1 + j, (*chip, c), me).wait_recv()
                cp = copy(t, 4 + j, (*chip, c), sibling)
                cp.start()
                passed.append(cp)
        for t in range(n):
            copy(t, 0, sibling, me).wait_recv()
            for j, chip in enumerate(chips):
                copy(t, 4 + j, (*chip, 1 - c), me).wait_recv()
        for cp in first + passed:
            cp.wait_send()
        for cp in mine:
            cp.wait()

    any_spec = pl.BlockSpec(memory_space=pl.ANY)
    return _pallas(
        body, name="all_gather_weights", out_shape=[jax.ShapeDtypeStruct((N_DEV, *s.shape), s.dtype) for s in shards],
        in_specs=[any_spec] * n, out_specs=[any_spec] * n,
        scratch_shapes=[pltpu.SemaphoreType.DMA((n, 7)), pltpu.SemaphoreType.DMA((n, 7)), pltpu.SemaphoreType.DMA((n,))],
    )(*shards)


def all_to_all_blocks(stacks):
    n = len(stacks)

    def body(*refs):
        ins, outs = refs[:n], refs[n:2 * n]
        send_sems, recv_sems, local_sems = refs[2 * n:]
        x, y, c = _mesh_pos()
        me = 4 * x + 2 * y + c
        mine = [pltpu.make_async_copy(ins[t].at[me], outs[t].at[me], local_sems.at[t]) for t in range(n)]
        for cp in mine:
            cp.start()
        copies = []
        for k in range(1, N_DEV):
            px = 1 - x if k & 4 else x
            py = 1 - y if k & 2 else y
            pc = 1 - c if k & 1 else c
            for t in range(n):
                cp = pltpu.make_async_remote_copy(
                    src_ref=ins[t].at[4 * px + 2 * py + pc], dst_ref=outs[t].at[me], send_sem=send_sems.at[t, k - 1],
                    recv_sem=recv_sems.at[t, k - 1], device_id=(px, py, pc), device_id_type=pl.DeviceIdType.MESH)
                cp.start()
                copies.append(cp)
        for cp in copies:
            cp.wait_recv()
        for cp in copies:
            cp.wait_send()
        for cp in mine:
            cp.wait()

    any_spec = pl.BlockSpec(memory_space=pl.ANY)
    return _pallas(
        body, name="all_to_all_grads", out_shape=[jax.ShapeDtypeStruct(s.shape, s.dtype) for s in stacks],
        in_specs=[any_spec] * n, out_specs=[any_spec] * n,
        scratch_shapes=[pltpu.SemaphoreType.DMA((n, 7)), pltpu.SemaphoreType.DMA((n, 7)), pltpu.SemaphoreType.DMA((n,))],
    )(*stacks)


def all_gather_small(v):
    r, w = v.shape

    def body(x_ref, out_ref, send_sems, recv_sems):
        x, y, c = _mesh_pos()
        me = 4 * x + 2 * y + c
        copies = []
        for k in range(1, N_DEV):
            px = 1 - x if k & 4 else x
            py = 1 - y if k & 2 else y
            pc = 1 - c if k & 1 else c
            cp = pltpu.make_async_remote_copy(
                src_ref=x_ref, dst_ref=out_ref.at[me], send_sem=send_sems.at[k - 1], recv_sem=recv_sems.at[k - 1],
                device_id=(px, py, pc), device_id_type=pl.DeviceIdType.MESH)
            cp.start()
            copies.append(cp)
        out_ref[me] = x_ref[...]
        for cp in copies:
            cp.wait_recv()
        for cp in copies:
            cp.wait_send()

    vmem = pl.BlockSpec(memory_space=pltpu.VMEM)
    return _pallas(
        body, name="all_gather_small", out_shape=jax.ShapeDtypeStruct((N_DEV, r, w), v.dtype), in_specs=[vmem],
        out_specs=vmem, scratch_shapes=[pltpu.SemaphoreType.DMA((7,)), pltpu.SemaphoreType.DMA((7,))],
    )(v)


def adamw_rows(parts, w, m, v, name):
    n, r, lanes = parts.shape
    tr = _tile(r, max(SUBLANES, (256 * 1024) // lanes), SUBLANES)
    c1 = 1.0 / (1.0 - ADAM_B1 ** ADAM_STEP)
    c2 = 1.0 / (1.0 - ADAM_B2 ** ADAM_STEP)

    def body(p_ref, w_ref, m_ref, v_ref, g_ref, d_ref, nm_ref, nv_ref):
        g = p_ref[0].astype(F32)
        for j in range(1, n):
            g = g + p_ref[j].astype(F32)
        nm = ADAM_B1 * m_ref[...] + (1.0 - ADAM_B1) * g
        nv = ADAM_B2 * v_ref[...] + (1.0 - ADAM_B2) * (g * g)
        g_ref[...] = g
        nm_ref[...] = nm
        nv_ref[...] = nv
        d_ref[...] = -ADAM_LR * ((nm * c1) / (jnp.sqrt(nv * c2) + ADAM_EPS) + ADAM_WD * w_ref[...])

    row = pl.BlockSpec((tr, lanes), lambda i: (i, 0))
    out = jax.ShapeDtypeStruct((r, lanes), F32)
    return _pallas(body, name=name, grid=(r // tr,), in_specs=[pl.BlockSpec((n, tr, lanes), lambda i: (0, i, 0)), row, row, row],
                   out_specs=[row, row, row, row], out_shape=[out, out, out, out], compiler_params=_cparams("parallel"))(
        parts, w, m, v)


def _padded(n):
    return -(-n // PACK_QUANTUM) * PACK_QUANTUM


def _pack(pieces, total_rows=None):
    flat = []
    for p in pieces:
        p = p.reshape(-1).astype(F32)
        flat.append(jnp.pad(p, (0, _padded(p.size) - p.size)))
    out = jnp.concatenate(flat).reshape(-1, LANES)
    if total_rows is not None and out.shape[0] != total_rows:
        out = jnp.pad(out, ((0, total_rows - out.shape[0]), (0, 0)))
    return out


def _pack_rows(sizes):
    rows = sum(_padded(n) for n in sizes) // LANES
    return -(-rows // PACK_ROW_TILE) * PACK_ROW_TILE


def _unpack(rows, shapes):
    lead = rows.shape[:-2]
    flat = rows.reshape(*lead, -1)
    out, off = [], 0
    for shp in shapes:
        n = int(np.prod(shp))
        out.append(flat[..., off:off + n].reshape(*lead, *shp))
        off += _padded(n)
    return out


def _shards_to_full(stacked, axis):
    moved = jnp.moveaxis(stacked, 0, axis)
    shp = list(stacked.shape[1:])
    shp[axis] *= N_DEV
    return moved.reshape(shp)


def _full_to_shards(full, axis):
    shp = list(full.shape)
    shp[axis:axis + 1] = [N_DEV, shp[axis] // N_DEV]
    return jnp.moveaxis(full.reshape(shp), axis, 0)


def _heads(t, n, d, dtype=ACT_DTYPE):
    return jnp.transpose(t.reshape(t.shape[0], n, d), (1, 0, 2)).astype(dtype)


def _unheads(t):
    return jnp.transpose(t, (1, 0, 2)).reshape(t.shape[1], -1)


def _rope_tables(s):
    half = A_ROPE // 2
    inv = ROPE_THETA ** (-jnp.arange(half, dtype=F32) / half)
    ang = jnp.arange(s, dtype=jnp.int32).astype(F32)[:, None] * inv[None, :]
    return jnp.cos(ang), jnp.sin(ang)


def _rope(t, cos, sin):
    half = A_ROPE // 2
    t1, t2 = t[..., :half], t[..., half:]
    c, sn = cos[:, None, :], sin[:, None, :]
    return jnp.concatenate([t1 * c - t2 * sn, t1 * sn + t2 * c], axis=-1)


def _t5_bucket(rel):
    nb = REL_BUCKETS // 2
    max_exact = nb // 2
    ret = (rel > 0).astype(jnp.int32) * nb
    n = jnp.abs(rel)
    large = max_exact + (jnp.log(jnp.maximum(n, 1).astype(F32) / max_exact)
                         / math.log(REL_MAX_DIST / max_exact) * (nb - max_exact)).astype(jnp.int32)
    large = jnp.minimum(large, nb - 1)
    return ret + jnp.where(n < max_exact, n, large)


def _window_bias(rel_bias):
    span = 3 * C_BLOCK
    rel = jnp.arange(span)[None, :] - C_BLOCK - jnp.arange(C_BLOCK)[:, None]
    onehot = (_t5_bucket(rel)[..., None] == jnp.arange(REL_BUCKETS)).astype(F32)
    bias = jnp.einsum("qkb,bh->hqk", onehot, rel_bias.astype(F32), precision=lax.Precision.HIGHEST)
    bias = jnp.where((jnp.abs(rel) <= C_WINDOW)[None], bias, NEG)
    return jnp.transpose(bias.reshape(C_HEADS, C_BLOCK, 3, C_BLOCK), (0, 2, 1, 3))


def _mla(cq, ckv, kr, gq, gkv, wuq, wukv, cos, sin):
    s = cq.shape[0]
    q = linear(rmsnorm(cq, gq, ACT_DTYPE, "rms_cq"), wuq, name="a_wuq").reshape(s, A_HEADS, A_NOPE + A_ROPE)
    q = jnp.concatenate([q[..., :A_NOPE], _rope(q[..., A_NOPE:], cos, sin)], axis=-1)
    kv = linear(rmsnorm(ckv, gkv, ACT_DTYPE, "rms_ckv"), wukv, name="a_wukv").reshape(s, A_HEADS, A_NOPE + A_V)
    k_rope = jnp.broadcast_to(_rope(kr[:, None, :], cos, sin), (s, A_HEADS, A_ROPE))
    k = jnp.concatenate([kv[..., :A_NOPE], k_rope], axis=-1)
    v = kv[..., A_NOPE:]
    return mla_attention(q, k, v, (A_NOPE + A_ROPE) ** -0.5)


def _hgrn2(q, f_fwd, f_bwd, i, g, lb_fwd, lb_bwd, g_out):
    s = q.shape[0]
    n_hp = B_HEADS // HG_GROUP
    pairs = lambda t: jnp.transpose(t.reshape(s, n_hp, -1), (1, 0, 2))
    qp, vp = pairs(q), pairs(i)
    o = None
    for z, lb, rev, tag in ((f_fwd, lb_fwd, False, "hgf"), (f_bwd, lb_bwd, True, "hgb")):
        qd, ki, ke, dec = hgrn_prep(qp, pairs(z), lb.astype(F32).reshape(n_hp, 1, -1), rev, tag + "_prep")
        part = hgrn_intra(qd, ki, vp, rev, tag + "_intra") + hgrn_inter(qd, ke, vp, dec, rev, tag + "_inter")
        o = part if o is None else o + part
    o = jnp.transpose(o, (1, 0, 2)).reshape(s * B_HEADS, B_DV)
    o = rmsnorm(o, g_out, F32, "rms_hg").reshape(s, B_HEADS * B_DV)
    return o * jax.nn.silu(g)


def _cross(x, h, mem_n, wq, wkv, wo):
    q = linear(h, wq, name="x_wq").reshape(h.shape[0], X_HEADS, X_DH)
    kv = linear(mem_n, wkv, name="x_wkv").reshape(mem_n.shape[0], 2, X_HEADS, X_DH)
    o_t = mla_attention(q, kv[:, 0], kv[:, 1], X_DH ** -0.5, name="cross")
    return linear_res(x, o_t, wo, name="x_wo", a_transposed=True)


def _pad_w_in(w):
    cut = A_Q_RANK + A_KV_RANK + A_ROPE
    return jnp.concatenate([w[:, :cut], jnp.zeros((w.shape[0], KR_PAD), w.dtype), w[:, cut:]], axis=1)


def _model_loss(p, x, mem, target):
    s = x.shape[0]
    cos, sin = _rope_tables(s)
    sm = jax.nn.softmax(p["b_lb"].astype(F32), axis=1)
    lower_bounds = jnp.cumsum(sm, axis=1) - sm[:, :1]
    bias = _window_bias(p["rel_bias"])
    for l in range(DEPTH):
        h = rmsnorm(x, p["g_mix"][l], ACT_DTYPE, "rms_mix")
        z = linear(h, _pad_w_in(p["w_in"][l]), name="w_in")
        parts, start = [], 0
        for width in IN_SPLITS_PADDED:
            parts.append(z[:, start:start + width])
            start += width
        a_cq, a_ckv, a_kr, b_q, b_ff, b_fb, b_i, b_g, c_q, c_k, c_v, gate_a, gate_b, gate_c = parts
        y_a = _mla(a_cq, a_ckv, a_kr[:, :A_ROPE], p["a_gq"][l], p["a_gkv"][l], p["a_wuq"][l], p["a_wukv"][l], cos, sin)
        y_b = _hgrn2(b_q, b_ff, b_fb, b_i, b_g, lower_bounds[0, l], lower_bounds[1, l], p["b_gout"][l])
        y_c = _unheads(window_attention(_heads(c_q, C_HEADS, C_DH, F32), _heads(c_k, C_KV_HEADS, C_DH, F32),
                                        _heads(c_v, C_KV_HEADS, C_DH, F32), bias, p["c_sink"][l]))
        x = gated_merge_out(x, y_a, y_b, y_c, gate_a, gate_b, gate_c, p["w_br_a"][l], p["w_br_b"][l], p["w_br_c"][l],
                            p["w_out"][l])
        h = rmsnorm(x, p["g_x"][l], ACT_DTYPE, "rms_x")
        x = _cross(x, h, rmsnorm(mem, p["g_mem"][l], ACT_DTYPE, "rms_mem"), p["x_wq"][l], p["x_wkv"][l], p["x_wo"][l])
        h = rmsnorm(x, p["g_ffn"][l], ACT_DTYPE, "rms_ffn")
        x = swiglu_ffn(x, h, p["f_w1"][l], p["f_w3"][l], p["f_w2"][l])
    y = rmsnorm(x, p["g_final"], F32, "rms_final")
    return loss_head(y, target)


def kernel(x, mem, w_in, g_mix, a_gq, a_gkv, a_wuq, a_wukv, b_lb, b_gout, c_sink, rel_bias, w_br_a, w_br_b, w_br_c, w_out, g_x, g_mem, x_wq, x_wkv, x_wo, g_ffn, f_w1, f_w3, f_w2, g_final, loss_target, m_w_in, m_g_mix, m_a_gq, m_a_gkv, m_a_wuq, m_a_wukv, m_b_lb, m_b_gout, m_c_sink, m_rel_bias, m_w_br_a, m_w_br_b, m_w_br_c, m_w_out, m_g_x, m_g_mem, m_x_wq, m_x_wkv, m_x_wo, m_g_ffn, m_f_w1, m_f_w3, m_f_w2, m_g_final, v_w_in, v_g_mix, v_a_gq, v_a_gkv, v_a_wuq, v_a_wukv, v_b_lb, v_b_gout, v_c_sink, v_rel_bias, v_w_br_a, v_w_br_b, v_w_br_c, v_w_out, v_g_x, v_g_mem, v_x_wq, v_x_wkv, v_x_wo, v_g_ffn, v_f_w1, v_f_w3, v_f_w2, v_g_final):
    given = dict(locals())
    w = {n: given[n] for n in WEIGHT_ORDER}
    m = {n: given["m_" + n] for n in WEIGHT_ORDER}
    v = {n: given["v_" + n] for n in WEIGHT_ORDER}
    sh_names = [n for n, _ in SHARDED]
    rep_shapes = [w[n].shape for n in REPLICATED] + [(1,)]
    rep_rows = _pack_rows([int(np.prod(s)) for s in rep_shapes])

    wire = [w[n] if n in ELEMENTWISE_SHARDED else w[n].astype(MXU_DTYPE) for n in sh_names]
    gathered = all_gather_shards(wire)
    full = {n: _shards_to_full(t, ax).astype(F32) for (n, ax), t in zip(SHARDED, gathered)}
    full.update({n: w[n] for n in REPLICATED})

    loss, (grad_full, grad_x) = jax.value_and_grad(_model_loss, argnums=(0, 1))(full, x[0], mem[0], loss_target[0])

    received = all_to_all_blocks([_full_to_shards(grad_full[n], ax).astype(GRAD_WIRE_DTYPE) for n, ax in SHARDED])
    g_sh, d_sh, nm_sh, nv_sh = {}, {}, {}, {}
    for n, got in zip(sh_names, received):
        shp = w[n].shape
        rows = lambda t: t.reshape(-1, shp[-1])
        outs = adamw_rows(got.reshape(N_DEV, -1, shp[-1]), rows(w[n]), rows(m[n]), rows(v[n]), "adamw_" + n)
        g_sh[n], d_sh[n], nm_sh[n], nv_sh[n] = [o.reshape(shp) for o in outs]

    mine = _pack([grad_full[n] for n in REPLICATED] + [loss.reshape(1)], rep_rows)
    everyone = all_gather_small(mine)
    rep_w = [w[n] for n in REPLICATED] + [jnp.zeros((1,), F32)]
    outs = adamw_rows(everyone, _pack(rep_w, rep_rows), _pack([m[n] for n in REPLICATED] + [jnp.zeros((1,), F32)], rep_rows),
                      _pack([v[n] for n in REPLICATED] + [jnp.ones((1,), F32)], rep_rows), "adamw_replicated")
    rep_names = list(REPLICATED) + ["loss"]
    g_rp, d_rp, nm_rp, nv_rp = [dict(zip(rep_names, _unpack(o, rep_shapes))) for o in outs]

    def pick(sharded, replicated, n):
        return sharded[n] if n in sharded else replicated[n]

    return (g_rp["loss"].reshape(()), grad_x[None],
            *[pick(g_sh, g_rp, n) for n in WEIGHT_ORDER], *[pick(d_sh, d_rp, n) for n in WEIGHT_ORDER],
            *[pick(nm_sh, nm_rp, n) for n in WEIGHT_ORDER], *[pick(nv_sh, nv_rp, n) for n in WEIGHT_ORDER])
```

```python
import functools
import math

import jax
import jax.numpy as jnp
import numpy as np
from jax import lax
from jax.experimental import pallas as pl
from jax.experimental.pallas import tpu as pltpu

F32 = jnp.float32
MXU_DTYPE = jnp.bfloat16
ACT_DTYPE = jnp.bfloat16
GRAD_WIRE_DTYPE = jnp.bfloat16

V7X_VMEM_LIMIT_BYTES = 56 * 1024 * 1024
LANES = 128
SUBLANES = 8

N_DEV = 8
D_MODEL = 1024
DEPTH = 2
EPS = 1e-6
TINY = 1e-30
NEG = -1e30

A_HEADS, A_NOPE, A_ROPE, A_V, A_Q_RANK, A_KV_RANK = 8, 64, 32, 64, 384, 256
ROPE_THETA = 10000.0
B_HEADS, B_DK, B_DV, B_CHUNK = 8, 128, 64, 16
C_HEADS, C_KV_HEADS, C_DH, C_WINDOW, C_BLOCK = 8, 2, 64, 128, 128
REL_BUCKETS, REL_MAX_DIST = 32, 128
X_HEADS, X_DH = 4, 256
D_FF = 2816
IN_SPLITS = (A_Q_RANK, A_KV_RANK, A_ROPE, 1024, 1024, 1024, 512, 512, 512, 128, 128, 1024, 1024, 1024)
IN_WIDTH = sum(IN_SPLITS)
KR_PAD = LANES - A_ROPE
IN_SPLITS_PADDED = (A_Q_RANK, A_KV_RANK, LANES, 1024, 1024, 1024, 512, 512, 512, 128, 128, 1024, 1024, 1024)

ADAM_LR, ADAM_B1, ADAM_B2, ADAM_EPS, ADAM_WD, ADAM_STEP = 0.001, 0.9, 0.999, 1e-08, 0.01, 10

SHARDED = (("w_in", 2), ("a_wuq", 2), ("a_wukv", 2), ("b_lb", 2), ("w_br_a", 2), ("w_br_b", 2), ("w_br_c", 2),
           ("w_out", 1), ("x_wq", 1), ("x_wkv", 2), ("x_wo", 1), ("f_w1", 2), ("f_w3", 2), ("f_w2", 1))
ELEMENTWISE_SHARDED = ("b_lb",)
REPLICATED = ("g_mix", "a_gq", "a_gkv", "b_gout", "c_sink", "rel_bias", "g_x", "g_mem", "g_ffn", "g_final")
WEIGHT_ORDER = ("w_in", "g_mix", "a_gq", "a_gkv", "a_wuq", "a_wukv", "b_lb", "b_gout", "c_sink", "rel_bias", "w_br_a",
                "w_br_b", "w_br_c", "w_out", "g_x", "g_mem", "x_wq", "x_wkv", "x_wo", "g_ffn", "f_w1", "f_w3", "f_w2",
                "g_final")
PACK_QUANTUM = SUBLANES * LANES
PACK_ROW_TILE = 512


def _pallas(body, **kw):
    return pl.pallas_call(body, **kw)


def _cparams(*sem):
    return pltpu.CompilerParams(dimension_semantics=sem, vmem_limit_bytes=V7X_VMEM_LIMIT_BYTES)


def _tile(n, target, mult=LANES):
    t = (min(target, n) // mult) * mult
    while t >= mult:
        if n % t == 0:
            return t
        t -= mult
    return n


def _dot(a, b, dims):
    return lax.dot_general(a.astype(MXU_DTYPE), b.astype(MXU_DTYPE), (dims, ((), ())), preferred_element_type=F32)


NN = ((1,), (0,))
NT = ((1,), (1,))
TN = ((0,), (0,))


MM_VMEM_BUDGET_BYTES = 40 * 1024 * 1024
MM_MAX_TILE = 4352
MM_MAX_ROW_TILE = 2048
MM_HBM_BYTES_PER_S = 2.5e12
MM_STEP_S = 0.4e-6
MM_DMA_ROW_OVERHEAD_BYTES = 512.0


def _tile_options(n, cap):
    out = [t for t in range(LANES, min(n, cap) + 1, LANES) if n % t == 0]
    if n <= cap and n not in out:
        out.append(n)
    return out or [n]


@functools.lru_cache(maxsize=None)
def _mm_plan(m, n, k, ta, tb, a_bytes, b_bytes, o_bytes):
    best = None
    for tk in _tile_options(k, MM_MAX_TILE):
        nk = k // tk
        for tn in _tile_options(n, MM_MAX_TILE):
            for tm in _tile_options(m, MM_MAX_ROW_TILE):
                vmem = 2 * (tm * tk * a_bytes + tk * tn * b_bytes + tm * tn * o_bytes) + tm * tn * 4
                vmem += (tm * tk * 2 if a_bytes == 4 else 0) + (tk * tn * 2 if b_bytes == 4 else 0)
                if vmem > MM_VMEM_BUDGET_BYTES:
                    continue

                def eff(elems, nbytes):
                    return (elems * nbytes) / (elems * nbytes + MM_DMA_ROW_OVERHEAD_BYTES)

                ea, eb, eo = eff(tm if ta else tk, a_bytes), eff(tk if tb else tn, b_bytes), eff(tn, o_bytes)
                for order in ("mn", "nm"):
                    if nk == 1 and order == "nm":
                        a_tr, b_tr = m * k * a_bytes * (n // tn), k * n * b_bytes
                    elif nk == 1:
                        a_tr, b_tr = m * k * a_bytes, k * n * b_bytes * (m // tm)
                    else:
                        a_tr, b_tr = m * k * a_bytes * (n // tn), k * n * b_bytes * (m // tm)
                    steps = (m // tm) * (n // tn) * nk
                    cost = (a_tr / ea + b_tr / eb + m * n * o_bytes / eo) / MM_HBM_BYTES_PER_S + steps * MM_STEP_S
                    if best is None or cost < best[0]:
                        best = (cost, tm, tn, tk, order)
    assert best is not None, (m, n, k)
    return best[1:]


def _mm(a, b, ta=False, tb=False, out_dtype=F32, name="mm", res=None):
    m, k = (a.shape[1], a.shape[0]) if ta else a.shape
    kb, n = (b.shape[1], b.shape[0]) if tb else b.shape
    assert k == kb, (a.shape, b.shape, ta, tb)
    tm, tn, tk, order = _mm_plan(m, n, k, ta, tb, a.dtype.itemsize, b.dtype.itemsize, jnp.dtype(out_dtype).itemsize)
    nk = k // tk
    dims = ((0 if ta else 1,), (1 if tb else 0,))
    out_shape = jax.ShapeDtypeStruct((m, n), out_dtype)

    assert res is None or nk == 1, (name, k, tk)
    if nk == 1:
        def body(a_ref, b_ref, *rest):
            acc = _dot(a_ref[...], b_ref[...], dims)
            if res is not None:
                acc = rest[0][...] + acc
            rest[-1][...] = acc.astype(rest[-1].dtype)

        if order == "nm":
            mi, ni = (lambda j, i: i), (lambda j, i: j)
            grid = (n // tn, m // tm)
        else:
            mi, ni = (lambda i, j: i), (lambda i, j: j)
            grid = (m // tm, n // tn)
        a_spec = pl.BlockSpec((tk, tm), lambda p, q: (0, mi(p, q))) if ta else pl.BlockSpec((tm, tk), lambda p, q: (mi(p, q), 0))
        b_spec = pl.BlockSpec((tn, tk), lambda p, q: (ni(p, q), 0)) if tb else pl.BlockSpec((tk, tn), lambda p, q: (0, ni(p, q)))
        o_spec = pl.BlockSpec((tm, tn), lambda p, q: (mi(p, q), ni(p, q)))
        extra = [] if res is None else [res]
        return _pallas(body, name=name, grid=grid, in_specs=[a_spec, b_spec] + [o_spec] * len(extra), out_specs=o_spec,
                       out_shape=out_shape, compiler_params=_cparams("parallel", "parallel"))(a, b, *extra)

    direct = jnp.dtype(out_dtype) == jnp.dtype(F32)

    def body(a_ref, b_ref, o_ref, *scratch):
        acc_ref = o_ref if direct else scratch[0]
        kk = pl.program_id(2)

        @pl.when(kk == 0)
        def _():
            acc_ref[...] = jnp.zeros_like(acc_ref)

        acc_ref[...] += _dot(a_ref[...], b_ref[...], dims)

        if not direct:
            @pl.when(kk == nk - 1)
            def _():
                o_ref[...] = acc_ref[...].astype(o_ref.dtype)

    a_spec = pl.BlockSpec((tk, tm), lambda i, j, kk: (kk, i)) if ta else pl.BlockSpec((tm, tk), lambda i, j, kk: (i, kk))
    b_spec = pl.BlockSpec((tn, tk), lambda i, j, kk: (j, kk)) if tb else pl.BlockSpec((tk, tn), lambda i, j, kk: (kk, j))
    return _pallas(
        body, name=name, grid=(m // tm, n // tn, nk), in_specs=[a_spec, b_spec],
        out_specs=pl.BlockSpec((tm, tn), lambda i, j, kk: (i, j)), out_shape=out_shape,
        scratch_shapes=[] if direct else [pltpu.VMEM((tm, tn), F32)],
        compiler_params=_cparams("parallel", "parallel", "arbitrary"),
    )(a, b)


def linear(a, w, out_dtype=F32, name="lin"):
    @jax.custom_vjp
    def f(a, w):
        return _mm(a.astype(ACT_DTYPE), w.astype(MXU_DTYPE), out_dtype=out_dtype, name=name + "_fwd")

    def fwd(a, w):
        ab, wb = a.astype(ACT_DTYPE), w.astype(MXU_DTYPE)
        return _mm(ab, wb, out_dtype=out_dtype, name=name + "_fwd"), (ab, wb, jnp.zeros((0,), a.dtype))

    def bwd(res, g):
        ab, wb, like_a = res
        gb = g.astype(ACT_DTYPE)
        da = _mm(gb, wb, tb=True, out_dtype=like_a.dtype, name=name + "_dx")
        dw = _mm(ab, gb, ta=True, out_dtype=F32, name=name + "_dw")
        return da, dw

    f.defvjp(fwd, bwd)
    return f(a, w)


def linear_res(x, a, w, name="lin", a_transposed=False):
    @jax.custom_vjp
    def f(x, a, w):
        return _mm(a.astype(ACT_DTYPE), w.astype(MXU_DTYPE), ta=a_transposed, name=name + "_fwd", res=x)

    def fwd(x, a, w):
        ab, wb = a.astype(ACT_DTYPE), w.astype(MXU_DTYPE)
        return _mm(ab, wb, ta=a_transposed, name=name + "_fwd", res=x), (ab, wb, jnp.zeros((0,), a.dtype))

    def bwd(res, g):
        ab, wb, like_a = res
        gb = g.astype(ACT_DTYPE)
        if a_transposed:
            da = _mm(wb, gb, tb=True, out_dtype=like_a.dtype, name=name + "_dx")
        else:
            da = _mm(gb, wb, tb=True, out_dtype=like_a.dtype, name=name + "_dx")
        dw = _mm(ab, gb, ta=not a_transposed, out_dtype=F32, name=name + "_dw")
        return g, da, dw

    f.defvjp(fwd, bwd)
    return f(x, a, w)


FFN_ROW_TILE = 512
FFN_COL_TILE = 1408


def _sigmoid(a):
    return 1.0 / (1.0 + jnp.exp(-a))


def swiglu_ffn(x, h, w1, w3, w2, name="ffn"):
    m, d = h.shape
    f_dim = w1.shape[1]
    tm, tn = _tile(m, FFN_ROW_TILE), _tile(f_dim, FFN_COL_TILE)

    def up_body(h_ref, w1_ref, w3_ref, t_ref, a_ref, b_ref):
        hv = h_ref[...]
        a = _dot(hv, w1_ref[...], NN)
        b = _dot(hv, w3_ref[...], NN)
        a_ref[...] = a.astype(a_ref.dtype)
        b_ref[...] = b.astype(b_ref.dtype)
        t_ref[...] = (a * _sigmoid(a) * b).astype(t_ref.dtype)

    def dt_body(g_ref, w2_ref, a_ref, b_ref, da_ref, db_ref):
        dt = _dot(g_ref[...], w2_ref[...], NT)
        a, b = a_ref[...].astype(F32), b_ref[...].astype(F32)
        sg = _sigmoid(a)
        da_ref[...] = (dt * b * (sg * (1.0 + a * (1.0 - sg)))).astype(da_ref.dtype)
        db_ref[...] = (dt * (a * sg)).astype(db_ref.dtype)

    row = pl.BlockSpec((tm, d), lambda j, i: (i, 0))
    w_up = pl.BlockSpec((d, tn), lambda j, i: (0, j))
    w_dn = pl.BlockSpec((tn, d), lambda j, i: (j, 0))
    tile = pl.BlockSpec((tm, tn), lambda j, i: (i, j))
    grid = (f_dim // tn, m // tm)

    def run_up(hb, w1b, w3b):
        return _pallas(up_body, name=name + "_up", grid=grid, in_specs=[row, w_up, w_up], out_specs=[tile, tile, tile],
                       out_shape=[jax.ShapeDtypeStruct((m, f_dim), ACT_DTYPE)] * 3,
                       compiler_params=_cparams("parallel", "parallel"))(hb, w1b, w3b)

    def forward(x, h, w1, w3, w2):
        hb = h.astype(ACT_DTYPE)
        w1b, w3b, w2b = w1.astype(MXU_DTYPE), w3.astype(MXU_DTYPE), w2.astype(MXU_DTYPE)
        t, a, b = run_up(hb, w1b, w3b)
        return _mm(t, w2b, name=name + "_down", res=x), (hb, w1b, w3b, w2b, t, a, b, jnp.zeros((0,), h.dtype))

    @jax.custom_vjp
    def f(x, h, w1, w3, w2):
        return forward(x, h, w1, w3, w2)[0]

    def bwd(res, g):
        hb, w1b, w3b, w2b, t, a, b, like_h = res
        gb = g.astype(ACT_DTYPE)
        da, db = _pallas(dt_body, name=name + "_dt", grid=grid, in_specs=[row, w_dn, tile, tile], out_specs=[tile, tile],
                         out_shape=[jax.ShapeDtypeStruct((m, f_dim), ACT_DTYPE)] * 2,
                         compiler_params=_cparams("parallel", "parallel"))(gb, w2b, a, b)
        dw2 = _mm(t, gb, ta=True, name=name + "_dw2")
        def dh_body(da_ref, db_ref, w1_ref, w3_ref, o_ref):
            o_ref[...] = (_dot(da_ref[...], w1_ref[...], NT) + _dot(db_ref[...], w3_ref[...], NT)).astype(o_ref.dtype)

        wide = pl.BlockSpec((tm, f_dim), lambda i: (i, 0))
        w_all = pl.BlockSpec((d, f_dim), lambda i: (0, 0))
        dh = _pallas(dh_body, name=name + "_dx", grid=(m // tm,), in_specs=[wide, wide, w_all, w_all],
                     out_specs=pl.BlockSpec((tm, d), lambda i: (i, 0)), out_shape=jax.ShapeDtypeStruct((m, d), like_h.dtype),
                     compiler_params=_cparams("parallel"))(da, db, w1b, w3b)
        dw1 = _mm(hb, da, ta=True, name=name + "_dw1")
        dw3 = _mm(hb, db, ta=True, name=name + "_dw3")
        return g, dh, dw1, dw3, dw2

    f.defvjp(lambda *args: forward(*args), bwd)
    return f(x, h, w1, w3, w2)


MERGE_ROW_TILE = 256


def gated_merge_out(x, ya_t, yb, yc, ga, gb, gc, wa, wb, wc, wo, name="merge"):
    s, d = x.shape
    e = yb.shape[1]
    tm = _tile(s, MERGE_ROW_TILE)

    def branches(ya_ref, yb_ref, yc_ref, wa_ref, wb_ref, wc_ref):
        return (_dot(ya_ref[...], wa_ref[...], TN), _dot(yb_ref[...], wb_ref[...], NN), _dot(yc_ref[...], wc_ref[...], NN))

    def fwd_body(x_ref, ya_ref, yb_ref, yc_ref, ga_ref, gb_ref, gc_ref, wa_ref, wb_ref, wc_ref, wo_ref, o_ref, m_ref):
        pa, pb, pc = branches(ya_ref, yb_ref, yc_ref, wa_ref, wb_ref, wc_ref)
        merged = _sigmoid(ga_ref[...]) * pa + _sigmoid(gb_ref[...]) * pb + _sigmoid(gc_ref[...]) * pc
        mb = merged.astype(m_ref.dtype)
        m_ref[...] = mb
        o_ref[...] = x_ref[...] + _dot(mb, wo_ref[...], NN)

    def bwd_body(g_ref, ya_ref, yb_ref, yc_ref, ga_ref, gb_ref, gc_ref, wa_ref, wb_ref, wc_ref, wo_ref,
                 dga_ref, dgb_ref, dgc_ref, dpa_ref, dpb_ref, dpc_ref):
        dm = _dot(g_ref[...], wo_ref[...], NT)
        ps = branches(ya_ref, yb_ref, yc_ref, wa_ref, wb_ref, wc_ref)
        for p_i, gate_ref, dg_ref, dp_ref in zip(ps, (ga_ref, gb_ref, gc_ref), (dga_ref, dgb_ref, dgc_ref),
                                                 (dpa_ref, dpb_ref, dpc_ref)):
            sg = _sigmoid(gate_ref[...])
            dg_ref[...] = dm * p_i * (sg * (1.0 - sg))
            dp_ref[...] = (dm * sg).astype(dp_ref.dtype)

    rows = lambda width: pl.BlockSpec((tm, width), lambda i: (i, 0))
    cols_t = pl.BlockSpec((e, tm), lambda i: (0, i))
    whole = lambda r, c: pl.BlockSpec((r, c), lambda i: (0, 0))
    in_common = [cols_t, rows(e), rows(e), rows(d), rows(d), rows(d), whole(e, d), whole(e, d), whole(e, d), whole(d, d)]

    def forward(x, ya_t, yb, yc, ga, gb, gc, wa, wb, wc, wo):
        cast = lambda t: t.astype(ACT_DTYPE)
        ops = (cast(ya_t), cast(yb), cast(yc), ga, gb, gc, cast(wa), cast(wb), cast(wc), cast(wo))
        out, merged = _pallas(
            fwd_body, name=name + "_fwd", grid=(s // tm,), in_specs=[rows(d)] + in_common, out_specs=[rows(d), rows(d)],
            out_shape=[jax.ShapeDtypeStruct((s, d), F32), jax.ShapeDtypeStruct((s, d), ACT_DTYPE)],
            compiler_params=_cparams("parallel"))(x, *ops)
        like = tuple(jnp.zeros((0,), t.dtype) for t in (ya_t, yb, yc))
        return out, (ops, merged, like)

    @jax.custom_vjp
    def f(*args):
        return forward(*args)[0]

    def bwd(res, g):
        ops, merged, like = res
        ya_b, yb_b, yc_b, ga, gb, gc, wa_b, wb_b, wc_b, wo_b = ops
        gbf = g.astype(ACT_DTYPE)
        gate_ct = jax.ShapeDtypeStruct((s, d), F32)
        branch_ct = jax.ShapeDtypeStruct((s, d), ACT_DTYPE)
        dga, dgb, dgc, dpa, dpb, dpc = _pallas(
            bwd_body, name=name + "_bwd", grid=(s // tm,), in_specs=[rows(d)] + in_common, out_specs=[rows(d)] * 6,
            out_shape=[gate_ct] * 3 + [branch_ct] * 3, compiler_params=_cparams("parallel"))(gbf, *ops)
        dya_t = _mm(wa_b, dpa, tb=True, out_dtype=like[0].dtype, name=name + "_dya")
        dyb = _mm(dpb, wb_b, tb=True, out_dtype=like[1].dtype, name=name + "_dyb")
        dyc = _mm(dpc, wc_b, tb=True, out_dtype=like[2].dtype, name=name + "_dyc")
        dwa = _mm(ya_b, dpa, name=name + "_dwa")
        dwb = _mm(yb_b, dpb, ta=True, name=name + "_dwb")
        dwc = _mm(yc_b, dpc, ta=True, name=name + "_dwc")
        dwo = _mm(merged, gbf, ta=True, name=name + "_dwo")
        return g, dya_t, dyb, dyc, dga, dgb, dgc, dwa, dwb, dwc, dwo

    f.defvjp(lambda *args: forward(*args), bwd)
    return f(x, ya_t, yb, yc, ga, gb, gc, wa, wb, wc, wo)


def _row_tile(rows, width):
    return _tile(rows, max(SUBLANES, (512 * 1024) // width), 16)


def rmsnorm(x, g, out_dtype=F32, name="rms"):
    rows, d = x.shape
    tr = _row_tile(rows, d)
    n_steps = rows // tr

    def fwd_body(x_ref, g_ref, o_ref):
        xv = x_ref[...].astype(F32)
        r = lax.rsqrt(jnp.mean(xv * xv, axis=-1, keepdims=True) + EPS)
        o_ref[...] = (xv * r * g_ref[...]).astype(o_ref.dtype)

    def bwd_body(x_ref, g_ref, dy_ref, dx_ref, dg_ref):
        xv = x_ref[...].astype(F32)
        dy = dy_ref[...].astype(F32)
        r = lax.rsqrt(jnp.mean(xv * xv, axis=-1, keepdims=True) + EPS)
        xh = xv * r
        dxh = dy * g_ref[...]
        dx_ref[...] = (r * (dxh - xh * jnp.mean(dxh * xh, axis=-1, keepdims=True))).astype(dx_ref.dtype)

        @pl.when(pl.program_id(0) == 0)
        def _():
            dg_ref[...] = jnp.zeros_like(dg_ref)

        dg_ref[...] += jnp.sum(dy * xh, axis=0, keepdims=True)

    row_spec = pl.BlockSpec((tr, d), lambda i: (i, 0))
    vec_spec = pl.BlockSpec((1, d), lambda i: (0, 0))

    def run_fwd(x, g):
        return _pallas(fwd_body, name=name + "_fwd", grid=(n_steps,), in_specs=[row_spec, vec_spec], out_specs=row_spec,
                       out_shape=jax.ShapeDtypeStruct((rows, d), out_dtype), compiler_params=_cparams("parallel"))(
            x, g.reshape(1, d).astype(F32))

    @jax.custom_vjp
    def f(x, g):
        return run_fwd(x, g)

    def fwd(x, g):
        return run_fwd(x, g), (x, g)

    def bwd(res, dy):
        x, g = res
        dx, dg = _pallas(
            bwd_body, name=name + "_bwd", grid=(n_steps,), in_specs=[row_spec, vec_spec, row_spec],
            out_specs=[row_spec, vec_spec],
            out_shape=[jax.ShapeDtypeStruct((rows, d), x.dtype), jax.ShapeDtypeStruct((1, d), F32)],
            compiler_params=_cparams("arbitrary"))(x, g.reshape(1, d).astype(F32), dy)
        return dx, dg.reshape(g.shape).astype(g.dtype)

    f.defvjp(fwd, bwd)
    return f(x, g)


def _rowdot(a, b, name):
    h, s, d = a.shape
    ts = _tile(s, 2048)

    def body(a_ref, b_ref, o_ref):
        o_ref[...] = jnp.sum(a_ref[...].astype(F32) * b_ref[...].astype(F32), axis=-1, keepdims=True)

    spec = pl.BlockSpec((None, ts, d), lambda hh, i: (hh, i, 0))
    return _pallas(body, name=name, grid=(h, s // ts), in_specs=[spec, spec],
                   out_specs=pl.BlockSpec((None, ts, 1), lambda hh, i: (hh, i, 0)),
                   out_shape=jax.ShapeDtypeStruct((h, s, 1), F32), compiler_params=_cparams("parallel", "parallel"))(a, b)


LOG2E = 1.4426950408889634


def mla_attention(q, k, v, scale, name="mla"):
    s, h, d = q.shape
    sk, dv = k.shape[0], v.shape[2]
    tq, tk = _tile(s, MLA_BWD_TQ), _tile(sk, MLA_TK)
    nq, nk = s // tq, sk // tk
    tqf, tkf = _tile(s, MLA_FWD_TQ), _tile(sk, MLA_FWD_TK)
    nkf = sk // tkf
    ones_rows = 16
    c = scale * LOG2E

    def fwd_body(qt_ref, k_ref, vt_ref, ot_ref, lse_ref, m_ref, acc_ref):
        j = pl.program_id(2)

        @pl.when(j == 0)
        def _():
            m_ref[...] = jnp.full_like(m_ref, NEG)
            acc_ref[...] = jnp.zeros_like(acc_ref)

        st = _dot(k_ref[...], qt_ref[...], NN)
        m_prev = m_ref[...]
        m_new = jnp.maximum(m_prev, jnp.max(st, axis=0, keepdims=True) * c)
        pt = jnp.exp2(st * c - m_new)
        acc_ref[...] = jnp.exp2(m_prev - m_new) * acc_ref[...] + _dot(vt_ref[...], pt, NN)
        m_ref[...] = m_new

        @pl.when(j == nkf - 1)
        def _():
            l = acc_ref[dv:dv + 1, :]
            ot_ref[...] = (acc_ref[:dv, :] / l).astype(ot_ref.dtype)
            lse_ref[...] = m_ref[...] + jnp.log2(l)

    def delta_body(ot_ref, dot_ref, o_ref):
        o_ref[...] = jnp.sum(ot_ref[...].astype(F32) * dot_ref[...].astype(F32), axis=0, keepdims=True)

    def bwd_body(qt_ref, k_ref, kt_ref, v_ref, dot_ref, lse_ref, dl_ref, dqt_ref, dk_hbm, dv_hbm, dq_acc, dk_acc, dv_acc):
        hh, i, j = pl.program_id(0), pl.program_id(1), pl.program_id(2)

        @pl.when(j == 0)
        def _():
            dq_acc[...] = jnp.zeros_like(dq_acc)

        @pl.when(i == 0)
        def _():
            dk_acc[j] = jnp.zeros((d, tk), F32)
            dv_acc[j] = jnp.zeros((dv, tk), F32)

        qt, dot_ = qt_ref[...], dot_ref[...]
        pt = jnp.exp2(_dot(k_ref[...], qt, NN) * c - lse_ref[...])
        dst = (pt * (_dot(v_ref[...], dot_, NN) - dl_ref[...])).astype(MXU_DTYPE)
        dv_acc[j] += _dot(dot_, pt, NT)
        dk_acc[j] += _dot(qt, dst, NT)
        dq_acc[...] += _dot(kt_ref[...], dst, NN)

        @pl.when(j == nk - 1)
        def _():
            dqt_ref[...] = dq_acc[...] * scale

        @pl.when(i == nq - 1)
        def _():
            dk_acc[j] = dk_acc[j] * scale
            pltpu.sync_copy(dk_acc.at[j], dk_hbm.at[hh, j])
            pltpu.sync_copy(dv_acc.at[j], dv_hbm.at[hh, j])

    def qt_spec(width):
        return pl.BlockSpec((None, width, tq), lambda hh, i, j: (hh, 0, i))

    def kt_spec(width):
        return pl.BlockSpec((None, width, tk), lambda hh, i, j: (hh, 0, j))

    def k_spec(width):
        return pl.BlockSpec((None, tk, width), lambda hh, i, j: (hh, j, 0))

    def layouts(q, k, v):
        cast = lambda t: t.astype(ACT_DTYPE)
        return (cast(jnp.transpose(q, (1, 2, 0))), cast(jnp.transpose(k, (1, 0, 2))), cast(jnp.transpose(k, (1, 2, 0))),
                cast(jnp.transpose(v, (1, 0, 2))), cast(jnp.transpose(v, (1, 2, 0))))

    def run_fwd(qt, kh, vt):
        vt_ones = jnp.concatenate([vt, jnp.ones((h, ones_rows, sk), vt.dtype)], axis=1)

        def qf_spec(width):
            return pl.BlockSpec((None, width, tqf), lambda hh, i, j: (hh, 0, i))

        kf_spec = pl.BlockSpec((None, tkf, d), lambda hh, i, j: (hh, j, 0))
        vf_spec = pl.BlockSpec((None, dv + ones_rows, tkf), lambda hh, i, j: (hh, 0, j))
        return _pallas(
            fwd_body, name=name + "_fwd", grid=(h, s // tqf, nkf), in_specs=[qf_spec(d), kf_spec, vf_spec],
            out_specs=[qf_spec(dv), qf_spec(1)],
            out_shape=[jax.ShapeDtypeStruct((h, dv, s), ACT_DTYPE), jax.ShapeDtypeStruct((h, 1, s), F32)],
            scratch_shapes=[pltpu.VMEM((1, tqf), F32), pltpu.VMEM((dv + ones_rows, tqf), F32)],
            compiler_params=_cparams("parallel", "parallel", "arbitrary"))(qt, kh, vt_ones)

    @jax.custom_vjp
    def f(q, k, v):
        qt, kh, _, _, vt = layouts(q, k, v)
        return run_fwd(qt, kh, vt)[0].reshape(h * dv, s)

    def fwd(q, k, v):
        qt, kh, kt, vh, vt = layouts(q, k, v)
        ot, lse = run_fwd(qt, kh, vt)
        return ot.reshape(h * dv, s), (qt, kh, kt, vh, ot, lse)

    def bwd(res, dy):
        qt, kh, kt, vh, ot, lse = res
        dot_ = dy.reshape(h, dv, s)
        ts = _tile(s, 2048)
        col = pl.BlockSpec((None, dv, ts), lambda hh, i: (hh, 0, i))
        delta = _pallas(delta_body, name=name + "_delta", grid=(h, s // ts), in_specs=[col, col],
                        out_specs=pl.BlockSpec((None, 1, ts), lambda hh, i: (hh, 0, i)),
                        out_shape=jax.ShapeDtypeStruct((h, 1, s), F32), compiler_params=_cparams("parallel", "parallel"))(ot, dot_)
        any_spec = pl.BlockSpec(memory_space=pl.ANY)
        dqt, dkt, dvt = _pallas(
            bwd_body, name=name + "_bwd", grid=(h, nq, nk),
            in_specs=[qt_spec(d), k_spec(d), kt_spec(d), k_spec(dv), qt_spec(dv), qt_spec(1), qt_spec(1)],
            out_specs=[qt_spec(d), any_spec, any_spec],
            out_shape=[jax.ShapeDtypeStruct((h, d, s), F32), jax.ShapeDtypeStruct((h, nk, d, tk), F32),
                       jax.ShapeDtypeStruct((h, nk, dv, tk), F32)],
            scratch_shapes=[pltpu.VMEM((d, tq), F32), pltpu.VMEM((nk, d, tk), F32), pltpu.VMEM((nk, dv, tk), F32)],
            compiler_params=_cparams("parallel", "arbitrary", "arbitrary"))(qt, kh, kt, vh, dot_, lse, delta)
        to_tokens = lambda t: jnp.transpose(t, (1, 3, 0, 2)).reshape(sk, h, t.shape[2])
        return jnp.transpose(dqt, (2, 0, 1)), to_tokens(dkt), to_tokens(dvt)

    f.defvjp(fwd, bwd)
    return f(q, k, v)


WATTN_TQ = 2 * C_BLOCK
WATTN_KW = WATTN_TQ + 2 * C_BLOCK


def window_attention(q, k, v, bias, sink, name="wattn"):
    hq, s, dh = q.shape
    g = hq // C_KV_HEADS
    tq, kw, half = WATTN_TQ, WATTN_KW, WATTN_KW // 2
    nt = s // tq
    scale = dh ** -0.5
    sink_b = jnp.broadcast_to(sink.astype(F32).reshape(hq, 1, 1), (hq, 1, LANES))
    neg = jnp.full((hq, C_BLOCK, C_BLOCK), NEG, F32)
    tile = jnp.concatenate(
        [jnp.concatenate([bias[:, cb - rb] if 0 <= cb - rb <= 2 else neg for cb in range(kw // C_BLOCK)], axis=2)
         for rb in range(tq // C_BLOCK)], axis=1)

    def key_bias(i):
        pos = lax.broadcasted_iota(jnp.int32, (1, kw), 1) + i * tq - C_BLOCK
        return jnp.where(jnp.logical_and(pos >= 0, pos < s), 0.0, NEG)

    def both(a_ref, b_ref):
        return jnp.concatenate([a_ref[...], b_ref[...]], axis=0)

    def fwd_body(q_ref, ka_ref, kb_ref, va_ref, vb_ref, b_ref, sk_ref, o_ref, lse_ref):
        kb_ = key_bias(pl.program_id(1))
        k_all, v_all = both(ka_ref, kb_ref), both(va_ref, vb_ref)
        for hh in range(g):
            sc = _dot(q_ref[hh], k_all, NT) * scale + b_ref[hh] + kb_
            snk = sk_ref[hh][:, :1]
            m = jnp.maximum(jnp.max(sc, axis=-1, keepdims=True), snk)
            p = jnp.exp(sc - m)
            l = jnp.sum(p, axis=-1, keepdims=True) + jnp.exp(snk - m)
            o_ref[hh] = (_dot(p, v_all, NN) / l).astype(o_ref.dtype)
            lse_ref[hh] = m + jnp.log(l)

    def bwd_body(q_ref, ka_ref, kb_ref, va_ref, vb_ref, b_ref, sk_ref, do_ref, lse_ref, dl_ref,
                 dq_ref, db_ref, dsink_ref, dk_hbm, dv_hbm, dk_acc, dv_acc):
        kv, i = pl.program_id(0), pl.program_id(1)

        @pl.when(i == 0)
        def _():
            dk_acc[...] = jnp.zeros_like(dk_acc)
            dv_acc[...] = jnp.zeros_like(dv_acc)
            db_ref[...] = jnp.zeros_like(db_ref)
            dsink_ref[...] = jnp.zeros_like(dsink_ref)

        kb_ = key_bias(i)
        k_all, v_all = both(ka_ref, kb_ref), both(va_ref, vb_ref)
        dk_t = jnp.zeros((kw, dh), F32)
        dv_t = jnp.zeros((kw, dh), F32)
        for hh in range(g):
            lse, dl, do = lse_ref[hh], dl_ref[hh], do_ref[hh]
            p = jnp.exp(_dot(q_ref[hh], k_all, NT) * scale + b_ref[hh] + kb_ - lse)
            ds = p * (_dot(do, v_all, NT) - dl)
            db_ref[hh] += ds
            total = jnp.broadcast_to(-jnp.sum(jnp.exp(sk_ref[hh][:, :1] - lse) * dl, axis=0, keepdims=True), (1, LANES))
            dsink_ref[hh] += jnp.where(lax.broadcasted_iota(jnp.int32, (1, LANES), 1) == 0, total, 0.0)
            dsb = (ds * scale).astype(MXU_DTYPE)
            dq_ref[hh] = _dot(dsb, k_all, NN).astype(dq_ref.dtype)
            dk_t += _dot(dsb, q_ref[hh], TN)
            dv_t += _dot(p, do, TN)
        rows = pl.ds(pl.multiple_of(i * tq, tq), kw)
        dk_acc[rows, :] += dk_t
        dv_acc[rows, :] += dv_t

        @pl.when(i == nt - 1)
        def _():
            pltpu.sync_copy(dk_acc, dk_hbm.at[kv])
            pltpu.sync_copy(dv_acc, dv_hbm.at[kv])

    def q_spec(width):
        return pl.BlockSpec((g, tq, width), lambda kv, i: (kv, i, 0))

    ka_spec = pl.BlockSpec((None, half, dh), lambda kv, i: (kv, i, 0))
    kb_spec = pl.BlockSpec((None, half, dh), lambda kv, i: (kv, i + 1, 0))
    b_spec = pl.BlockSpec((g, tq, kw), lambda kv, i: (kv, 0, 0))
    sk_spec = pl.BlockSpec((g, 1, LANES), lambda kv, i: (kv, 0, 0))

    def padded(t):
        return jnp.pad(t, ((0, 0), (C_BLOCK, C_BLOCK), (0, 0)))

    def run_fwd(q, kp, vp, tile, sink_b):
        return _pallas(
            fwd_body, name=name + "_fwd", grid=(C_KV_HEADS, nt),
            in_specs=[q_spec(dh), ka_spec, kb_spec, ka_spec, kb_spec, b_spec, sk_spec], out_specs=[q_spec(dh), q_spec(1)],
            out_shape=[jax.ShapeDtypeStruct((hq, s, dh), ACT_DTYPE), jax.ShapeDtypeStruct((hq, s, 1), F32)],
            compiler_params=_cparams("parallel", "parallel"))(q, kp, kp, vp, vp, tile, sink_b)

    @jax.custom_vjp
    def f(q, k, v, tile, sink_b):
        return run_fwd(q, padded(k), padded(v), tile, sink_b)[0]

    def fwd(q, k, v, tile, sink_b):
        kp, vp = padded(k), padded(v)
        o, lse = run_fwd(q, kp, vp, tile, sink_b)
        return o, (q, kp, vp, tile, sink_b, o, lse)

    def bwd(res, do):
        q, kp, vp, tile, sink_b, o, lse = res
        delta = _rowdot(o, do, name + "_delta")
        any_spec = pl.BlockSpec(memory_space=pl.ANY)
        acc = jax.ShapeDtypeStruct((C_KV_HEADS, s + 2 * C_BLOCK, dh), F32)
        dq, dtile, dsink, dkp, dvp = _pallas(
            bwd_body, name=name + "_bwd", grid=(C_KV_HEADS, nt),
            in_specs=[q_spec(dh), ka_spec, kb_spec, ka_spec, kb_spec, b_spec, sk_spec, q_spec(dh), q_spec(1), q_spec(1)],
            out_specs=[q_spec(dh), b_spec, sk_spec, any_spec, any_spec],
            out_shape=[jax.ShapeDtypeStruct((hq, s, dh), q.dtype), jax.ShapeDtypeStruct((hq, tq, kw), F32),
                       jax.ShapeDtypeStruct((hq, 1, LANES), F32), acc, acc],
            scratch_shapes=[pltpu.VMEM((s + 2 * C_BLOCK, dh), F32), pltpu.VMEM((s + 2 * C_BLOCK, dh), F32)],
            compiler_params=_cparams("parallel", "arbitrary"))(q, kp, kp, vp, vp, tile, sink_b, do, lse, delta)
        unpad = lambda t: t[:, C_BLOCK:-C_BLOCK].astype(kp.dtype)
        return dq, unpad(dkp), unpad(dvp), dtile, dsink

    f.defvjp(fwd, bwd)
    return f(q, k, v, tile, sink_b)


HG_PREP_ROWS = 256
HG_GROUP = 8
HG_INTRA_BLOCK = 256
HG_INTER_CHUNKS = 16
MLA_FWD_TQ, MLA_FWD_TK, MLA_BWD_TQ, MLA_TK = 2048, 2048, 2048, 1024


def _hdot(a, b, dims, ones_first=True):
    b16 = jnp.bfloat16
    ones, full = (a, b) if ones_first else (b, a)
    hi = full.astype(b16)
    rest = full - hi.astype(F32)
    mid = rest.astype(b16)
    lo = (rest - mid.astype(F32)).astype(b16)
    ones16 = ones.astype(b16)
    dn = (dims, ((), ()))

    def one(piece):
        lhs, rhs = (ones16, piece) if ones_first else (piece, ones16)
        return lax.dot_general(lhs, rhs, dn, preferred_element_type=F32)

    return one(hi) + one(mid) + one(lo)


HG_OUT_ROWS = 256


def hgrn_out(o, gate, g_out, name="hg_out"):
    s, c = o.shape
    tb = _tile(s, HG_OUT_ROWS)
    g_row = jnp.tile(g_out.astype(F32).reshape(1, B_DV), (1, c // B_DV))

    def head_mean(v):
        r = lax.broadcasted_iota(jnp.int32, (c, c), 0)
        cc = lax.broadcasted_iota(jnp.int32, (c, c), 1)
        same = jnp.where(r // B_DV == cc // B_DV, 1.0, 0.0).astype(F32)
        return _hdot(v, same, NN, ones_first=False) * (1.0 / B_DV)

    def fwd_body(o_ref, gate_ref, g_ref, y_ref):
        ov, gt = o_ref[...], gate_ref[...]
        xh = ov * lax.rsqrt(head_mean(ov * ov) + EPS)
        y_ref[...] = xh * g_ref[...] * (gt * _sigmoid(gt))

    def bwd_body(o_ref, gate_ref, g_ref, dy_ref, do_ref, dgate_ref, dg_ref):
        ov, gt, gv, dy = o_ref[...], gate_ref[...], g_ref[...], dy_ref[...]
        r = lax.rsqrt(head_mean(ov * ov) + EPS)
        xh = ov * r
        sg = _sigmoid(gt)
        dgate_ref[...] = dy * xh * gv * (sg * (1.0 + gt * (1.0 - sg)))
        dn = dy * (gt * sg)
        dxh = dn * gv
        do_ref[...] = r * (dxh - xh * head_mean(dxh * xh))

        @pl.when(pl.program_id(0) == 0)
        def _():
            dg_ref[...] = jnp.zeros_like(dg_ref)

        dg_ref[...] += jnp.sum(dn * xh, axis=0, keepdims=True)

    row = pl.BlockSpec((tb, c), lambda i: (i, 0))
    vec = pl.BlockSpec((1, c), lambda i: (0, 0))
    shape = jax.ShapeDtypeStruct((s, c), F32)

    def run_fwd(o, gate, g_row):
        return _pallas(fwd_body, name=name + "_fwd", grid=(s // tb,), in_specs=[row, row, vec], out_specs=row, out_shape=shape,
                       compiler_params=_cparams("parallel"))(o, gate, g_row)

    @jax.custom_vjp
    def f(o, gate, g_row):
        return run_fwd(o, gate, g_row)

    def fwd(o, gate, g_row):
        return run_fwd(o, gate, g_row), (o, gate, g_row)

    def bwd(res, dy):
        o, gate, g_row = res
        return tuple(_pallas(
            bwd_body, name=name + "_bwd", grid=(s // tb,), in_specs=[row, row, vec, row], out_specs=[row, row, vec],
            out_shape=[shape, shape, jax.ShapeDtypeStruct((1, c), F32)], compiler_params=_cparams("arbitrary"))(o, gate, g_row, dy))

    f.defvjp(fwd, bwd)
    return f(o, gate, g_row)


def hgrn_prep(q, z, lb, reverse, name):
    n_hp, s, tc = q.shape
    tb = _tile(s, HG_PREP_ROWS)
    ncb = tb // B_CHUNK

    def chunk_matrices():
        r = lax.broadcasted_iota(jnp.int32, (tb, tb), 0)
        cc = lax.broadcasted_iota(jnp.int32, (tb, tb), 1)
        same = r // B_CHUNK == cc // B_CHUNK
        tri = (cc >= r) if reverse else (cc <= r)
        cum = jnp.where(jnp.logical_and(same, tri), 1.0, 0.0).astype(F32)
        every = jnp.where(same, 1.0, 0.0).astype(F32)
        pr = lax.broadcasted_iota(jnp.int32, (ncb, tb), 0)
        pc = lax.broadcasted_iota(jnp.int32, (ncb, tb), 1)
        per_chunk = jnp.where(pc // B_CHUNK == pr, 1.0, 0.0).astype(F32)
        return cum, every, per_chunk

    def gates(zv, lbv):
        e = jnp.exp(-jnp.abs(zv))
        big, small = 1.0 / (1.0 + e), e / (1.0 + e)
        sig = jnp.where(zv >= 0, big, small)
        nsig = jnp.where(zv >= 0, small, big)
        f = lbv + (1.0 - lbv) * sig
        return sig, nsig, f, jnp.log(jnp.maximum(f, TINY)), (1.0 - lbv) * nsig

    def fwd_body(q_ref, z_ref, lb_ref, qd_ref, ki_ref, ke_ref, dec_ref):
        cum, every, per_chunk = chunk_matrices()
        _, _, _, lf, key = gates(z_ref[...], lb_ref[...])
        b = _hdot(cum, lf, NN)
        tot = _hdot(every, lf, NN)
        qd_ref[...] = q_ref[...] * jnp.exp(b)
        ki_ref[...] = key * jnp.exp(-b)
        ke_ref[...] = key * jnp.exp(tot - b)
        dec_ref[...] = jnp.exp(_hdot(per_chunk, lf, NN))

    def bwd_body(q_ref, z_ref, lb_ref, dqd_ref, dki_ref, dke_ref, ddec_ref, dq_ref, dz_ref, dlb_ref):
        cum, every, per_chunk = chunk_matrices()
        lbv = lb_ref[...]
        sig, nsig, f, lf, key = gates(z_ref[...], lbv)
        b = _hdot(cum, lf, NN)
        tot = _hdot(every, lf, NN)
        e_b, e_nb, e_tb = jnp.exp(b), jnp.exp(-b), jnp.exp(tot - b)
        dqd, dki, dke = dqd_ref[...], dki_ref[...], dke_ref[...]
        dq_ref[...] = dqd * e_b
        dkey = dki * e_nb + dke * e_tb
        t_end = dke * key * e_tb
        db = dqd * q_ref[...] * e_b - dki * key * e_nb - t_end
        dtot = ddec_ref[...] * jnp.exp(_hdot(per_chunk, lf, NN)) + _hdot(per_chunk, t_end, NN)
        dlf = _hdot(cum, db, TN) + _hdot(per_chunk, dtot, TN)
        df = jnp.where(f > TINY, dlf / f, 0.0)
        one_m_lb = 1.0 - lbv
        dz_ref[...] = (df - dkey) * one_m_lb * sig * nsig
        dlb_part = jnp.sum(df * nsig - dkey * nsig, axis=0, keepdims=True)

        @pl.when(pl.program_id(1) == 0)
        def _():
            dlb_ref[...] = jnp.zeros_like(dlb_ref)

        dlb_ref[...] += dlb_part

    tok = pl.BlockSpec((None, tb, tc), lambda j, i: (j, i, 0))
    vec = pl.BlockSpec((None, 1, tc), lambda j, i: (j, 0, 0))
    chk = pl.BlockSpec((None, ncb, tc), lambda j, i: (j, i, 0))
    grid = (n_hp, s // tb)
    tok_shape = jax.ShapeDtypeStruct((n_hp, s, tc), F32)
    chk_shape = jax.ShapeDtypeStruct((n_hp, s // B_CHUNK, tc), F32)

    def run_fwd(q, z, lb):
        return _pallas(fwd_body, name=name + "_fwd", grid=grid, in_specs=[tok, tok, vec], out_specs=[tok, tok, tok, chk],
                       out_shape=[tok_shape, tok_shape, tok_shape, chk_shape],
                       compiler_params=_cparams("parallel", "parallel"))(q, z, lb)

    @jax.custom_vjp
    def f(q, z, lb):
        return tuple(run_fwd(q, z, lb))

    def fwd(q, z, lb):
        return tuple(run_fwd(q, z, lb)), (q, z, lb)

    def bwd(res, cts):
        q, z, lb = res
        dq, dz, dlb = _pallas(
            bwd_body, name=name + "_bwd", grid=grid, in_specs=[tok, tok, vec, tok, tok, tok, chk], out_specs=[tok, tok, vec],
            out_shape=[tok_shape, tok_shape, jax.ShapeDtypeStruct((n_hp, 1, tc), F32)],
            compiler_params=_cparams("parallel", "arbitrary"))(q, z, lb, *cts)
        return dq, dz, dlb

    f.defvjp(fwd, bwd)
    return f(q, z, lb)


def _pair_cols(ref, hh, width):
    return ref[:, hh * width:(hh + 1) * width]


def hgrn_intra(qd, ki, v, reverse, name):
    s = qd.shape[1]
    tb = _tile(s, HG_INTRA_BLOCK)
    wk, wv = HG_GROUP * B_DK, HG_GROUP * B_DV

    def mask():
        r = lax.broadcasted_iota(jnp.int32, (tb, tb), 0)
        c = lax.broadcasted_iota(jnp.int32, (tb, tb), 1)
        return jnp.logical_and(r // B_CHUNK == c // B_CHUNK, (c >= r) if reverse else (c <= r))

    def fwd_body(q_ref, k_ref, v_ref, o_ref):
        msk = mask()
        for hh in range(HG_GROUP):
            sc = jnp.where(msk, _dot(_pair_cols(q_ref, hh, B_DK), _pair_cols(k_ref, hh, B_DK), NT), 0.0)
            o_ref[:, hh * B_DV:(hh + 1) * B_DV] = _dot(sc, _pair_cols(v_ref, hh, B_DV), NN)

    def bwd_body(q_ref, k_ref, v_ref, do_ref, dq_ref, dk_ref, dv_ref):
        msk = mask()
        for hh in range(HG_GROUP):
            q, k = _pair_cols(q_ref, hh, B_DK), _pair_cols(k_ref, hh, B_DK)
            vv, do = _pair_cols(v_ref, hh, B_DV), _pair_cols(do_ref, hh, B_DV)
            sc = jnp.where(msk, _dot(q, k, NT), 0.0)
            ds = jnp.where(msk, _dot(do, vv, NT), 0.0)
            dq_ref[:, hh * B_DK:(hh + 1) * B_DK] = _dot(ds, k, NN)
            dk_ref[:, hh * B_DK:(hh + 1) * B_DK] = _dot(ds, q, TN)
            dv_ref[:, hh * B_DV:(hh + 1) * B_DV] = _dot(sc, do, TN)

    ks = pl.BlockSpec((None, tb, wk), lambda hp, i: (hp, i, 0))
    vs = pl.BlockSpec((None, tb, wv), lambda hp, i: (hp, i, 0))
    grid = (B_HEADS // HG_GROUP, s // tb)

    def run_fwd(qd, ki, v):
        return _pallas(fwd_body, name=name + "_fwd", grid=grid, in_specs=[ks, ks, vs], out_specs=vs,
                       out_shape=jax.ShapeDtypeStruct(v.shape, F32), compiler_params=_cparams("parallel", "parallel"))(qd, ki, v)

    @jax.custom_vjp
    def f(qd, ki, v):
        return run_fwd(qd, ki, v)

    def fwd(qd, ki, v):
        return run_fwd(qd, ki, v), (qd, ki, v)

    def bwd(res, do):
        qd, ki, v = res
        return tuple(_pallas(
            bwd_body, name=name + "_bwd", grid=grid, in_specs=[ks, ks, vs, vs], out_specs=[ks, ks, vs],
            out_shape=[jax.ShapeDtypeStruct(qd.shape, F32), jax.ShapeDtypeStruct(ki.shape, F32),
                       jax.ShapeDtypeStruct(v.shape, F32)],
            compiler_params=_cparams("parallel", "parallel"))(qd, ki, v, do))

    f.defvjp(fwd, bwd)
    return f(qd, ki, v)


def hgrn_inter(qd, ke, v, dec, reverse, name):
    s = qd.shape[1]
    nc = s // B_CHUNK
    cpb = HG_INTER_CHUNKS if nc % HG_INTER_CHUNKS == 0 else nc
    tb = cpb * B_CHUNK
    nblk = nc // cpb
    wk, wv = HG_GROUP * B_DK, HG_GROUP * B_DV
    n_hp = B_HEADS // HG_GROUP

    def rows(c):
        return pl.ds(c * B_CHUNK, B_CHUNK)

    def kcols(hh):
        return slice(hh * B_DK, (hh + 1) * B_DK)

    def vcols(hh):
        return slice(hh * B_DV, (hh + 1) * B_DV)

    def order(flip):
        return reversed(range(cpb)) if flip else range(cpb)

    def fwd_body(q_ref, k_ref, v_ref, dec_ref, o_ref, st_ref, state):
        @pl.when(pl.program_id(1) == 0)
        def _():
            state[...] = jnp.zeros_like(state)

        for c in order(reverse):
            for hh in range(HG_GROUP):
                st = state[hh]
                st_ref[c, hh] = st
                o_ref[rows(c), vcols(hh)] = _dot(q_ref[rows(c), kcols(hh)], st, NT)
                state[hh] = st * dec_ref[pl.ds(c, 1), kcols(hh)] + _dot(v_ref[rows(c), vcols(hh)], k_ref[rows(c), kcols(hh)], TN)

    def bwd_body(q_ref, k_ref, v_ref, dec_ref, st_ref, do_ref, dq_ref, dk_ref, dv_ref, ddec_ref, dstate):
        @pl.when(pl.program_id(1) == 0)
        def _():
            dstate[...] = jnp.zeros_like(dstate)

        for c in order(not reverse):
            for hh in range(HG_GROUP):
                dst = dstate[hh]
                st = st_ref[c, hh]
                do_c = do_ref[rows(c), vcols(hh)]
                dk_ref[rows(c), kcols(hh)] = _dot(v_ref[rows(c), vcols(hh)], dst, NN)
                dv_ref[rows(c), vcols(hh)] = _dot(k_ref[rows(c), kcols(hh)], dst, NT)
                ddec_ref[pl.ds(c, 1), kcols(hh)] = jnp.sum(dst * st, axis=0, keepdims=True)
                dq_ref[rows(c), kcols(hh)] = _dot(do_c, st, NN)
                dstate[hh] = dst * dec_ref[pl.ds(c, 1), kcols(hh)] + _dot(do_c, q_ref[rows(c), kcols(hh)], TN)

    def specs(flip):
        blk = (lambda i: nblk - 1 - i) if flip else (lambda i: i)
        tok_k = pl.BlockSpec((None, tb, wk), lambda hp, i: (hp, blk(i), 0))
        tok_v = pl.BlockSpec((None, tb, wv), lambda hp, i: (hp, blk(i), 0))
        chk = pl.BlockSpec((None, cpb, wk), lambda hp, i: (hp, blk(i), 0))
        sts = pl.BlockSpec((None, cpb, HG_GROUP, B_DV, B_DK), lambda hp, i: (hp, blk(i), 0, 0, 0))
        return tok_k, tok_v, chk, sts

    scratch = [pltpu.VMEM((HG_GROUP, B_DV, B_DK), F32)]

    def run_fwd(qd, ke, v, dec):
        tok_k, tok_v, chk, sts = specs(reverse)
        return _pallas(
            fwd_body, name=name + "_fwd", grid=(n_hp, nblk), in_specs=[tok_k, tok_k, tok_v, chk], out_specs=[tok_v, sts],
            out_shape=[jax.ShapeDtypeStruct(v.shape, F32), jax.ShapeDtypeStruct((n_hp, nc, HG_GROUP, B_DV, B_DK), F32)],
            scratch_shapes=scratch, compiler_params=_cparams("parallel", "arbitrary"))(qd, ke, v, dec)

    @jax.custom_vjp
    def f(qd, ke, v, dec):
        return run_fwd(qd, ke, v, dec)[0]

    def fwd(qd, ke, v, dec):
        o, st = run_fwd(qd, ke, v, dec)
        return o, (qd, ke, v, dec, st)

    def bwd(res, do):
        qd, ke, v, dec, st = res
        tok_k, tok_v, chk, sts = specs(not reverse)
        return tuple(_pallas(
            bwd_body, name=name + "_bwd", grid=(n_hp, nblk), in_specs=[tok_k, tok_k, tok_v, chk, sts, tok_v],
            out_specs=[tok_k, tok_k, tok_v, chk],
            out_shape=[jax.ShapeDtypeStruct(qd.shape, F32), jax.ShapeDtypeStruct(ke.shape, F32),
                       jax.ShapeDtypeStruct(v.shape, F32), jax.ShapeDtypeStruct(dec.shape, F32)],
            scratch_shapes=scratch, compiler_params=_cparams("parallel", "arbitrary"))(qd, ke, v, dec, st, do))

    f.defvjp(fwd, bwd)
    return f(qd, ke, v, dec)


def loss_head(y, target, name="loss"):
    s, d = y.shape
    tr = _row_tile(s, d)

    def body(y_ref, t_ref, o_ref):
        @pl.when(pl.program_id(0) == 0)
        def _():
            o_ref[...] = jnp.zeros_like(o_ref)

        e = y_ref[...] - t_ref[...]
        part = jnp.sum(jnp.sum(e * e, axis=-1, keepdims=True), axis=0, keepdims=True) * (0.5 / d)
        o_ref[...] += jnp.broadcast_to(part, o_ref.shape)

    spec = pl.BlockSpec((tr, d), lambda i: (i, 0))

    def run(y, t):
        out = _pallas(body, name=name, grid=(s // tr,), in_specs=[spec, spec],
                      out_specs=pl.BlockSpec((SUBLANES, LANES), lambda i: (0, 0)),
                      out_shape=jax.ShapeDtypeStruct((SUBLANES, LANES), F32), compiler_params=_cparams("arbitrary"))(y, t)
        return out[0, 0]

    @jax.custom_vjp
    def f(y, t):
        return run(y, t)

    def fwd(y, t):
        return run(y, t), (y, t)

    def bwd(res, g):
        y, t = res
        dy = g * (y - t) * (1.0 / d)
        return dy, -dy

    f.defvjp(fwd, bwd)
    return f(y, target)


def _mesh_pos():
    return lax.axis_index("x"), lax.axis_index("y"), lax.axis_index("c")


def all_gather_shards(shards):
    n = len(shards)

    def body(*refs):
        ins, outs = refs[:n], refs[n:2 * n]
        send_sems, recv_sems, local_sems = refs[2 * n:]
        x, y, c = _mesh_pos()
        me, sibling = (x, y, c), (x, y, 1 - c)
        chips = [(1 - x, y), (x, 1 - y), (1 - x, 1 - y)]

        def slot(t, px, py, pc):
            return outs[t].at[4 * px + 2 * py + pc]

        def copy(t, k, block, to, src=None):
            return pltpu.make_async_remote_copy(
                src_ref=slot(t, *block) if src is None else src, dst_ref=slot(t, *block), send_sem=send_sems.at[t, k],
                recv_sem=recv_sems.at[t, k], device_id=to, device_id_type=pl.DeviceIdType.MESH)

        mine = [pltpu.make_async_copy(ins[t], slot(t, *me), local_sems.at[t]) for t in range(n)]
        for cp in mine:
            cp.start()
        first = []
        for t in range(n):
            first.append(copy(t, 0, me, sibling, src=ins[t]))
            first += [copy(t, 1 + j, me, (*chip, c), src=ins[t]) for j, chip in enumerate(chips)]
        for cp in first:
            cp.start()
        passed = []
        for j, chip in enumerate(chips):
            for t in range(n):
                copy(t, 1 + j, (*chip, c), me).wait_recv()
                cp = copy(t, 4 + j, (*chip, c), sibling)
                cp.start()
                passed.append(cp)
        for t in range(n):
            copy(t, 0, sibling, me).wait_recv()
            for j, chip in enumerate(chips):
                copy(t, 4 + j, (*chip, 1 - c), me).wait_recv()
        for cp in first + passed:
            cp.wait_send()
        for cp in mine:
            cp.wait()

    any_spec = pl.BlockSpec(memory_space=pl.ANY)
    return _pallas(
        body, name="all_gather_weights", out_shape=[jax.ShapeDtypeStruct((N_DEV, *s.shape), s.dtype) for s in shards],
        in_specs=[any_spec] * n, out_specs=[any_spec] * n,
        scratch_shapes=[pltpu.SemaphoreType.DMA((n, 7)), pltpu.SemaphoreType.DMA((n, 7)), pltpu.SemaphoreType.DMA((n,))],
    )(*shards)


def all_to_all_blocks(stacks):
    n = len(stacks)

    def body(*refs):
        ins, outs = refs[:n], refs[n:2 * n]
        send_sems, recv_sems, local_sems = refs[2 * n:]
        x, y, c = _mesh_pos()
        me = 4 * x + 2 * y + c
        mine = [pltpu.make_async_copy(ins[t].at[me], outs[t].at[me], local_sems.at[t]) for t in range(n)]
        for cp in mine:
            cp.start()
        copies = []
        for k in range(1, N_DEV):
            px = 1 - x if k & 4 else x
            py = 1 - y if k & 2 else y
            pc = 1 - c if k & 1 else c
            for t in range(n):
                cp = pltpu.make_async_remote_copy(
                    src_ref=ins[t].at[4 * px + 2 * py + pc], dst_ref=outs[t].at[me], send_sem=send_sems.at[t, k - 1],
                    recv_sem=recv_sems.at[t, k - 1], device_id=(px, py, pc), device_id_type=pl.DeviceIdType.MESH)
                cp.start()
                copies.append(cp)
        for cp in copies:
            cp.wait_recv()
        for cp in copies:
            cp.wait_send()
        for cp in mine:
            cp.wait()

    any_spec = pl.BlockSpec(memory_space=pl.ANY)
    return _pallas(
        body, name="all_to_all_grads", out_shape=[jax.ShapeDtypeStruct(s.shape, s.dtype) for s in stacks],
        in_specs=[any_spec] * n, out_specs=[any_spec] * n,
        scratch_shapes=[pltpu.SemaphoreType.DMA((n, 7)), pltpu.SemaphoreType.DMA((n, 7)), pltpu.SemaphoreType.DMA((n,))],
    )(*stacks)


def all_gather_small(v):
    r, w = v.shape

    def body(x_ref, out_ref, send_sems, recv_sems):
        x, y, c = _mesh_pos()
        me = 4 * x + 2 * y + c
        copies = []
        for k in range(1, N_DEV):
            px = 1 - x if k & 4 else x
            py = 1 - y if k & 2 else y
            pc = 1 - c if k & 1 else c
            cp = pltpu.make_async_remote_copy(
                src_ref=x_ref, dst_ref=out_ref.at[me], send_sem=send_sems.at[k - 1], recv_sem=recv_sems.at[k - 1],
                device_id=(px, py, pc), device_id_type=pl.DeviceIdType.MESH)
            cp.start()
            copies.append(cp)
        out_ref[me] = x_ref[...]
        for cp in copies:
            cp.wait_recv()
        for cp in copies:
            cp.wait_send()

    vmem = pl.BlockSpec(memory_space=pltpu.VMEM)
    return _pallas(
        body, name="all_gather_small", out_shape=jax.ShapeDtypeStruct((N_DEV, r, w), v.dtype), in_specs=[vmem],
        out_specs=vmem, scratch_shapes=[pltpu.SemaphoreType.DMA((7,)), pltpu.SemaphoreType.DMA((7,))],
    )(v)


def adamw_rows(parts, w, m, v, name):
    n, r, lanes = parts.shape
    tr = _tile(r, max(SUBLANES, (256 * 1024) // lanes), SUBLANES)
    c1 = 1.0 / (1.0 - ADAM_B1 ** ADAM_STEP)
    c2 = 1.0 / (1.0 - ADAM_B2 ** ADAM_STEP)

    def body(p_ref, w_ref, m_ref, v_ref, g_ref, d_ref, nm_ref, nv_ref):
        g = p_ref[0].astype(F32)
        for j in range(1, n):
            g = g + p_ref[j].astype(F32)
        nm = ADAM_B1 * m_ref[...] + (1.0 - ADAM_B1) * g
        nv = ADAM_B2 * v_ref[...] + (1.0 - ADAM_B2) * (g * g)
        g_ref[...] = g
        nm_ref[...] = nm
        nv_ref[...] = nv
        d_ref[...] = -ADAM_LR * ((nm * c1) / (jnp.sqrt(nv * c2) + ADAM_EPS) + ADAM_WD * w_ref[...])

    row = pl.BlockSpec((tr, lanes), lambda i: (i, 0))
    out = jax.ShapeDtypeStruct((r, lanes), F32)
    return _pallas(body, name=name, grid=(r // tr,), in_specs=[pl.BlockSpec((n, tr, lanes), lambda i: (0, i, 0)), row, row, row],
                   out_specs=[row, row, row, row], out_shape=[out, out, out, out], compiler_params=_cparams("parallel"))(
        parts, w, m, v)


def _padded(n):
    return -(-n // PACK_QUANTUM) * PACK_QUANTUM


def _pack(pieces, total_rows=None):
    flat = []
    for p in pieces:
        p = p.reshape(-1).astype(F32)
        flat.append(jnp.pad(p, (0, _padded(p.size) - p.size)))
    out = jnp.concatenate(flat).reshape(-1, LANES)
    if total_rows is not None and out.shape[0] != total_rows:
        out = jnp.pad(out, ((0, total_rows - out.shape[0]), (0, 0)))
    return out


def _pack_rows(sizes):
    rows = sum(_padded(n) for n in sizes) // LANES
    return -(-rows // PACK_ROW_TILE) * PACK_ROW_TILE


def _unpack(rows, shapes):
    lead = rows.shape[:-2]
    flat = rows.reshape(*lead, -1)
    out, off = [], 0
    for shp in shapes:
        n = int(np.prod(shp))
        out.append(flat[..., off:off + n].reshape(*lead, *shp))
        off += _padded(n)
    return out


def _shards_to_full(stacked, axis):
    moved = jnp.moveaxis(stacked, 0, axis)
    shp = list(stacked.shape[1:])
    shp[axis] *= N_DEV
    return moved.reshape(shp)


def _full_to_shards(full, axis):
    shp = list(full.shape)
    shp[axis:axis + 1] = [N_DEV, shp[axis] // N_DEV]
    return jnp.moveaxis(full.reshape(shp), axis, 0)


def _heads(t, n, d, dtype=ACT_DTYPE):
    return jnp.transpose(t.reshape(t.shape[0], n, d), (1, 0, 2)).astype(dtype)


def _unheads(t):
    return jnp.transpose(t, (1, 0, 2)).reshape(t.shape[1], -1)


def _rope_tables(s):
    half = A_ROPE // 2
    inv = ROPE_THETA ** (-jnp.arange(half, dtype=F32) / half)
    ang = jnp.arange(s, dtype=jnp.int32).astype(F32)[:, None] * inv[None, :]
    return jnp.cos(ang), jnp.sin(ang)


def _rope(t, cos, sin):
    half = A_ROPE // 2
    t1, t2 = t[..., :half], t[..., half:]
    c, sn = cos[:, None, :], sin[:, None, :]
    return jnp.concatenate([t1 * c - t2 * sn, t1 * sn + t2 * c], axis=-1)


def _t5_bucket(rel):
    nb = REL_BUCKETS // 2
    max_exact = nb // 2
    ret = (rel > 0).astype(jnp.int32) * nb
    n = jnp.abs(rel)
    large = max_exact + (jnp.log(jnp.maximum(n, 1).astype(F32) / max_exact)
                         / math.log(REL_MAX_DIST / max_exact) * (nb - max_exact)).astype(jnp.int32)
    large = jnp.minimum(large, nb - 1)
    return ret + jnp.where(n < max_exact, n, large)


def _window_bias(rel_bias):
    span = 3 * C_BLOCK
    rel = jnp.arange(span)[None, :] - C_BLOCK - jnp.arange(C_BLOCK)[:, None]
    onehot = (_t5_bucket(rel)[..., None] == jnp.arange(REL_BUCKETS)).astype(F32)
    bias = jnp.einsum("qkb,bh->hqk", onehot, rel_bias.astype(F32), precision=lax.Precision.HIGHEST)
    bias = jnp.where((jnp.abs(rel) <= C_WINDOW)[None], bias, NEG)
    return jnp.transpose(bias.reshape(C_HEADS, C_BLOCK, 3, C_BLOCK), (0, 2, 1, 3))


def _mla(cq, ckv, kr, gq, gkv, wuq, wukv, cos, sin):
    s = cq.shape[0]
    q = linear(rmsnorm(cq, gq, ACT_DTYPE, "rms_cq"), wuq, name="a_wuq").reshape(s, A_HEADS, A_NOPE + A_ROPE)
    q = jnp.concatenate([q[..., :A_NOPE], _rope(q[..., A_NOPE:], cos, sin)], axis=-1)
    kv = linear(rmsnorm(ckv, gkv, ACT_DTYPE, "rms_ckv"), wukv, name="a_wukv").reshape(s, A_HEADS, A_NOPE + A_V)
    k_rope = jnp.broadcast_to(_rope(kr[:, None, :], cos, sin), (s, A_HEADS, A_ROPE))
    k = jnp.concatenate([kv[..., :A_NOPE], k_rope], axis=-1)
    v = kv[..., A_NOPE:]
    return mla_attention(q, k, v, (A_NOPE + A_ROPE) ** -0.5)


def _hgrn2(q, f_fwd, f_bwd, i, g, lb_fwd, lb_bwd, g_out):
    s = q.shape[0]
    n_hp = B_HEADS // HG_GROUP
    pairs = lambda t: jnp.transpose(t.reshape(s, n_hp, -1), (1, 0, 2))
    qp, vp = pairs(q), pairs(i)
    o = None
    for z, lb, rev, tag in ((f_fwd, lb_fwd, False, "hgf"), (f_bwd, lb_bwd, True, "hgb")):
        qd, ki, ke, dec = hgrn_prep(qp, pairs(z), lb.astype(F32).reshape(n_hp, 1, -1), rev, tag + "_prep")
        part = hgrn_intra(qd, ki, vp, rev, tag + "_intra") + hgrn_inter(qd, ke, vp, dec, rev, tag + "_inter")
        o = part if o is None else o + part
    return hgrn_out(jnp.transpose(o, (1, 0, 2)).reshape(s, B_HEADS * B_DV), g, g_out)


def _cross(x, h, mem_n, wq, wkv, wo):
    q = linear(h, wq, name="x_wq").reshape(h.shape[0], X_HEADS, X_DH)
    kv = linear(mem_n, wkv, name="x_wkv").reshape(mem_n.shape[0], 2, X_HEADS, X_DH)
    o_t = mla_attention(q, kv[:, 0], kv[:, 1], X_DH ** -0.5, name="cross")
    return linear_res(x, o_t, wo, name="x_wo", a_transposed=True)


def _pad_w_in(w):
    cut = A_Q_RANK + A_KV_RANK + A_ROPE
    return jnp.concatenate([w[:, :cut], jnp.zeros((w.shape[0], KR_PAD), w.dtype), w[:, cut:]], axis=1)


def _model_loss(p, x, mem, target):
    s = x.shape[0]
    cos, sin = _rope_tables(s)
    sm = jax.nn.softmax(p["b_lb"].astype(F32), axis=1)
    lower_bounds = jnp.cumsum(sm, axis=1) - sm[:, :1]
    bias = _window_bias(p["rel_bias"])
    for l in range(DEPTH):
        h = rmsnorm(x, p["g_mix"][l], ACT_DTYPE, "rms_mix")
        z = linear(h, _pad_w_in(p["w_in"][l]), name="w_in")
        parts, start = [], 0
        for width in IN_SPLITS_PADDED:
            parts.append(z[:, start:start + width])
            start += width
        a_cq, a_ckv, a_kr, b_q, b_ff, b_fb, b_i, b_g, c_q, c_k, c_v, gate_a, gate_b, gate_c = parts
        y_a = _mla(a_cq, a_ckv, a_kr[:, :A_ROPE], p["a_gq"][l], p["a_gkv"][l], p["a_wuq"][l], p["a_wukv"][l], cos, sin)
        y_b = _hgrn2(b_q, b_ff, b_fb, b_i, b_g, lower_bounds[0, l], lower_bounds[1, l], p["b_gout"][l])
        y_c = _unheads(window_attention(_heads(c_q, C_HEADS, C_DH, F32), _heads(c_k, C_KV_HEADS, C_DH, F32),
                                        _heads(c_v, C_KV_HEADS, C_DH, F32), bias, p["c_sink"][l]))
        x = gated_merge_out(x, y_a, y_b, y_c, gate_a, gate_b, gate_c, p["w_br_a"][l], p["w_br_b"][l], p["w_br_c"][l],
                            p["w_out"][l])
        h = rmsnorm(x, p["g_x"][l], ACT_DTYPE, "rms_x")
        x = _cross(x, h, rmsnorm(mem, p["g_mem"][l], ACT_DTYPE, "rms_mem"), p["x_wq"][l], p["x_wkv"][l], p["x_wo"][l])
        h = rmsnorm(x, p["g_ffn"][l], ACT_DTYPE, "rms_ffn")
        x = swiglu_ffn(x, h, p["f_w1"][l], p["f_w3"][l], p["f_w2"][l])
    y = rmsnorm(x, p["g_final"], F32, "rms_final")
    return loss_head(y, target)


def kernel(x, mem, w_in, g_mix, a_gq, a_gkv, a_wuq, a_wukv, b_lb, b_gout, c_sink, rel_bias, w_br_a, w_br_b, w_br_c, w_out, g_x, g_mem, x_wq, x_wkv, x_wo, g_ffn, f_w1, f_w3, f_w2, g_final, loss_target, m_w_in, m_g_mix, m_a_gq, m_a_gkv, m_a_wuq, m_a_wukv, m_b_lb, m_b_gout, m_c_sink, m_rel_bias, m_w_br_a, m_w_br_b, m_w_br_c, m_w_out, m_g_x, m_g_mem, m_x_wq, m_x_wkv, m_x_wo, m_g_ffn, m_f_w1, m_f_w3, m_f_w2, m_g_final, v_w_in, v_g_mix, v_a_gq, v_a_gkv, v_a_wuq, v_a_wukv, v_b_lb, v_b_gout, v_c_sink, v_rel_bias, v_w_br_a, v_w_br_b, v_w_br_c, v_w_out, v_g_x, v_g_mem, v_x_wq, v_x_wkv, v_x_wo, v_g_ffn, v_f_w1, v_f_w3, v_f_w2, v_g_final):
    given = dict(locals())
    w = {n: given[n] for n in WEIGHT_ORDER}
    m = {n: given["m_" + n] for n in WEIGHT_ORDER}
    v = {n: given["v_" + n] for n in WEIGHT_ORDER}
    sh_names = [n for n, _ in SHARDED]
    rep_shapes = [w[n].shape for n in REPLICATED] + [(1,)]
    rep_rows = _pack_rows([int(np.prod(s)) for s in rep_shapes])

    wire = [w[n] if n in ELEMENTWISE_SHARDED else w[n].astype(MXU_DTYPE) for n in sh_names]
    gathered = all_gather_shards(wire)
    full = {n: _shards_to_full(t, ax).astype(F32) for (n, ax), t in zip(SHARDED, gathered)}
    full.update({n: w[n] for n in REPLICATED})

    loss, (grad_full, grad_x) = jax.value_and_grad(_model_loss, argnums=(0, 1))(full, x[0], mem[0], loss_target[0])

    received = all_to_all_blocks([_full_to_shards(grad_full[n], ax).astype(GRAD_WIRE_DTYPE) for n, ax in SHARDED])
    g_sh, d_sh, nm_sh, nv_sh = {}, {}, {}, {}
    for n, got in zip(sh_names, received):
        shp = w[n].shape
        rows = lambda t: t.reshape(-1, shp[-1])
        outs = adamw_rows(got.reshape(N_DEV, -1, shp[-1]), rows(w[n]), rows(m[n]), rows(v[n]), "adamw_" + n)
        g_sh[n], d_sh[n], nm_sh[n], nv_sh[n] = [o.reshape(shp) for o in outs]

    mine = _pack([grad_full[n] for n in REPLICATED] + [loss.reshape(1)], rep_rows)
    everyone = all_gather_small(mine)
    rep_w = [w[n] for n in REPLICATED] + [jnp.zeros((1,), F32)]
    outs = adamw_rows(everyone, _pack(rep_w, rep_rows), _pack([m[n] for n in REPLICATED] + [jnp.zeros((1,), F32)], rep_rows),
                      _pack([v[n] for n in REPLICATED] + [jnp.ones((1,), F32)], rep_rows), "adamw_replicated")
    rep_names = list(REPLICATED) + ["loss"]
    g_rp, d_rp, nm_rp, nv_rp = [dict(zip(rep_names, _unpack(o, rep_shapes))) for o in outs]

    def pick(sharded, replicated, n):
        return sharded[n] if n in sharded else replicated[n]

    return (g_rp["loss"].reshape(()), grad_x[None],
            *[pick(g_sh, g_rp, n) for n in WEIGHT_ORDER], *[pick(d_sh, d_rp, n) for n in WEIGHT_ORDER],
            *[pick(nm_sh, nm_rp, n) for n in WEIGHT_ORDER], *[pick(nv_sh, nv_rp, n) for n in WEIGHT_ORDER])
```

```python
import functools
import math

import jax
import jax.numpy as jnp
import numpy as np
from jax import lax
from jax.experimental import pallas as pl
from jax.experimental.pallas import tpu as pltpu

F32 = jnp.float32
MXU_DTYPE = jnp.bfloat16
ACT_DTYPE = jnp.bfloat16
GRAD_WIRE_DTYPE = jnp.bfloat16

V7X_VMEM_LIMIT_BYTES = 56 * 1024 * 1024
LANES = 128
SUBLANES = 8

N_DEV = 8
D_MODEL = 1024
DEPTH = 2
EPS = 1e-6
TINY = 1e-30
NEG = -1e30

A_HEADS, A_NOPE, A_ROPE, A_V, A_Q_RANK, A_KV_RANK = 8, 64, 32, 64, 384, 256
ROPE_THETA = 10000.0
B_HEADS, B_DK, B_DV, B_CHUNK = 8, 128, 64, 16
C_HEADS, C_KV_HEADS, C_DH, C_WINDOW, C_BLOCK = 8, 2, 64, 128, 128
REL_BUCKETS, REL_MAX_DIST = 32, 128
X_HEADS, X_DH = 4, 256
D_FF = 2816
IN_SPLITS = (A_Q_RANK, A_KV_RANK, A_ROPE, 1024, 1024, 1024, 512, 512, 512, 128, 128, 1024, 1024, 1024)
IN_WIDTH = sum(IN_SPLITS)
KR_PAD = LANES - A_ROPE
IN_SPLITS_PADDED = (A_Q_RANK, A_KV_RANK, LANES, 1024, 1024, 1024, 512, 512, 512, 128, 128, 1024, 1024, 1024)

ADAM_LR, ADAM_B1, ADAM_B2, ADAM_EPS, ADAM_WD, ADAM_STEP = 0.001, 0.9, 0.999, 1e-08, 0.01, 10

SHARDED = (("w_in", 2), ("a_wuq", 2), ("a_wukv", 2), ("b_lb", 2), ("w_br_a", 2), ("w_br_b", 2), ("w_br_c", 2),
           ("w_out", 1), ("x_wq", 1), ("x_wkv", 2), ("x_wo", 1), ("f_w1", 2), ("f_w3", 2), ("f_w2", 1))
ELEMENTWISE_SHARDED = ("b_lb",)
REPLICATED = ("g_mix", "a_gq", "a_gkv", "b_gout", "c_sink", "rel_bias", "g_x", "g_mem", "g_ffn", "g_final")
WEIGHT_ORDER = ("w_in", "g_mix", "a_gq", "a_gkv", "a_wuq", "a_wukv", "b_lb", "b_gout", "c_sink", "rel_bias", "w_br_a",
                "w_br_b", "w_br_c", "w_out", "g_x", "g_mem", "x_wq", "x_wkv", "x_wo", "g_ffn", "f_w1", "f_w3", "f_w2",
                "g_final")
PACK_QUANTUM = SUBLANES * LANES
PACK_ROW_TILE = 512


def _pallas(body, **kw):
    return pl.pallas_call(body, **kw)


def _cparams(*sem):
    return pltpu.CompilerParams(dimension_semantics=sem, vmem_limit_bytes=V7X_VMEM_LIMIT_BYTES)


def _tile(n, target, mult=LANES):
    t = (min(target, n) // mult) * mult
    while t >= mult:
        if n % t == 0:
            return t
        t -= mult
    return n


def _dot(a, b, dims):
    return lax.dot_general(a.astype(MXU_DTYPE), b.astype(MXU_DTYPE), (dims, ((), ())), preferred_element_type=F32)


NN = ((1,), (0,))
NT = ((1,), (1,))
TN = ((0,), (0,))


MM_VMEM_BUDGET_BYTES = 40 * 1024 * 1024
MM_MAX_TILE = 4352
MM_MAX_ROW_TILE = 2048
MM_HBM_BYTES_PER_S = 2.5e12
MM_STEP_S = 0.4e-6
MM_DMA_ROW_OVERHEAD_BYTES = 512.0


def _tile_options(n, cap):
    out = [t for t in range(LANES, min(n, cap) + 1, LANES) if n % t == 0]
    if n <= cap and n not in out:
        out.append(n)
    return out or [n]


@functools.lru_cache(maxsize=None)
def _mm_plan(m, n, k, ta, tb, a_bytes, b_bytes, o_bytes):
    best = None
    for tk in _tile_options(k, MM_MAX_TILE):
        nk = k // tk
        for tn in _tile_options(n, MM_MAX_TILE):
            for tm in _tile_options(m, MM_MAX_ROW_TILE):
                vmem = 2 * (tm * tk * a_bytes + tk * tn * b_bytes + tm * tn * o_bytes) + tm * tn * 4
                vmem += (tm * tk * 2 if a_bytes == 4 else 0) + (tk * tn * 2 if b_bytes == 4 else 0)
                if vmem > MM_VMEM_BUDGET_BYTES:
                    continue

                def eff(elems, nbytes):
                    return (elems * nbytes) / (elems * nbytes + MM_DMA_ROW_OVERHEAD_BYTES)

                ea, eb, eo = eff(tm if ta else tk, a_bytes), eff(tk if tb else tn, b_bytes), eff(tn, o_bytes)
                for order in ("mn", "nm"):
                    if nk == 1 and order == "nm":
                        a_tr, b_tr = m * k * a_bytes * (n // tn), k * n * b_bytes
                    elif nk == 1:
                        a_tr, b_tr = m * k * a_bytes, k * n * b_bytes * (m // tm)
                    else:
                        a_tr, b_tr = m * k * a_bytes * (n // tn), k * n * b_bytes * (m // tm)
                    steps = (m // tm) * (n // tn) * nk
                    cost = (a_tr / ea + b_tr / eb + m * n * o_bytes / eo) / MM_HBM_BYTES_PER_S + steps * MM_STEP_S
                    if best is None or cost < best[0]:
                        best = (cost, tm, tn, tk, order)
    assert best is not None, (m, n, k)
    return best[1:]


def _mm(a, b, ta=False, tb=False, out_dtype=F32, name="mm", res=None):
    m, k = (a.shape[1], a.shape[0]) if ta else a.shape
    kb, n = (b.shape[1], b.shape[0]) if tb else b.shape
    assert k == kb, (a.shape, b.shape, ta, tb)
    tm, tn, tk, order = _mm_plan(m, n, k, ta, tb, a.dtype.itemsize, b.dtype.itemsize, jnp.dtype(out_dtype).itemsize)
    nk = k // tk
    dims = ((0 if ta else 1,), (1 if tb else 0,))
    out_shape = jax.ShapeDtypeStruct((m, n), out_dtype)

    assert res is None or nk == 1, (name, k, tk)
    if nk == 1:
        def body(a_ref, b_ref, *rest):
            acc = _dot(a_ref[...], b_ref[...], dims)
            if res is not None:
                acc = rest[0][...] + acc
            rest[-1][...] = acc.astype(rest[-1].dtype)

        if order == "nm":
            mi, ni = (lambda j, i: i), (lambda j, i: j)
            grid = (n // tn, m // tm)
        else:
            mi, ni = (lambda i, j: i), (lambda i, j: j)
            grid = (m // tm, n // tn)
        a_spec = pl.BlockSpec((tk, tm), lambda p, q: (0, mi(p, q))) if ta else pl.BlockSpec((tm, tk), lambda p, q: (mi(p, q), 0))
        b_spec = pl.BlockSpec((tn, tk), lambda p, q: (ni(p, q), 0)) if tb else pl.BlockSpec((tk, tn), lambda p, q: (0, ni(p, q)))
        o_spec = pl.BlockSpec((tm, tn), lambda p, q: (mi(p, q), ni(p, q)))
        extra = [] if res is None else [res]
        return _pallas(body, name=name, grid=grid, in_specs=[a_spec, b_spec] + [o_spec] * len(extra), out_specs=o_spec,
                       out_shape=out_shape, compiler_params=_cparams("parallel", "parallel"))(a, b, *extra)

    direct = jnp.dtype(out_dtype) == jnp.dtype(F32)

    def body(a_ref, b_ref, o_ref, *scratch):
        acc_ref = o_ref if direct else scratch[0]
        kk = pl.program_id(2)

        @pl.when(kk == 0)
        def _():
            acc_ref[...] = jnp.zeros_like(acc_ref)

        acc_ref[...] += _dot(a_ref[...], b_ref[...], dims)

        if not direct:
            @pl.when(kk == nk - 1)
            def _():
                o_ref[...] = acc_ref[...].astype(o_ref.dtype)

    a_spec = pl.BlockSpec((tk, tm), lambda i, j, kk: (kk, i)) if ta else pl.BlockSpec((tm, tk), lambda i, j, kk: (i, kk))
    b_spec = pl.BlockSpec((tn, tk), lambda i, j, kk: (j, kk)) if tb else pl.BlockSpec((tk, tn), lambda i, j, kk: (kk, j))
    return _pallas(
        body, name=name, grid=(m // tm, n // tn, nk), in_specs=[a_spec, b_spec],
        out_specs=pl.BlockSpec((tm, tn), lambda i, j, kk: (i, j)), out_shape=out_shape,
        scratch_shapes=[] if direct else [pltpu.VMEM((tm, tn), F32)],
        compiler_params=_cparams("parallel", "parallel", "arbitrary"),
    )(a, b)


def linear(a, w, out_dtype=F32, name="lin"):
    @jax.custom_vjp
    def f(a, w):
        return _mm(a.astype(ACT_DTYPE), w.astype(MXU_DTYPE), out_dtype=out_dtype, name=name + "_fwd")

    def fwd(a, w):
        ab, wb = a.astype(ACT_DTYPE), w.astype(MXU_DTYPE)
        return _mm(ab, wb, out_dtype=out_dtype, name=name + "_fwd"), (ab, wb, jnp.zeros((0,), a.dtype))

    def bwd(res, g):
        ab, wb, like_a = res
        gb = g.astype(ACT_DTYPE)
        da = _mm(gb, wb, tb=True, out_dtype=like_a.dtype, name=name + "_dx")
        dw = _mm(ab, gb, ta=True, out_dtype=F32, name=name + "_dw")
        return da, dw

    f.defvjp(fwd, bwd)
    return f(a, w)


def linear_res(x, a, w, name="lin", a_transposed=False):
    @jax.custom_vjp
    def f(x, a, w):
        return _mm(a.astype(ACT_DTYPE), w.astype(MXU_DTYPE), ta=a_transposed, name=name + "_fwd", res=x)

    def fwd(x, a, w):
        ab, wb = a.astype(ACT_DTYPE), w.astype(MXU_DTYPE)
        return _mm(ab, wb, ta=a_transposed, name=name + "_fwd", res=x), (ab, wb, jnp.zeros((0,), a.dtype))

    def bwd(res, g):
        ab, wb, like_a = res
        gb = g.astype(ACT_DTYPE)
        if a_transposed:
            da = _mm(wb, gb, tb=True, out_dtype=like_a.dtype, name=name + "_dx")
        else:
            da = _mm(gb, wb, tb=True, out_dtype=like_a.dtype, name=name + "_dx")
        dw = _mm(ab, gb, ta=not a_transposed, out_dtype=F32, name=name + "_dw")
        return g, da, dw

    f.defvjp(fwd, bwd)
    return f(x, a, w)


FFN_ROW_TILE = 512
FFN_COL_TILE = 1408


def _sigmoid(a):
    return 1.0 / (1.0 + jnp.exp(-a))


def swiglu_ffn(x, h, w1, w3, w2, name="ffn"):
    m, d = h.shape
    f_dim = w1.shape[1]
    tm, tn = _tile(m, FFN_ROW_TILE), _tile(f_dim, FFN_COL_TILE)

    def up_body(h_ref, w1_ref, w3_ref, t_ref, a_ref, b_ref):
        hv = h_ref[...]
        a = _dot(hv, w1_ref[...], NN)
        b = _dot(hv, w3_ref[...], NN)
        a_ref[...] = a.astype(a_ref.dtype)
        b_ref[...] = b.astype(b_ref.dtype)
        t_ref[...] = (a * _sigmoid(a) * b).astype(t_ref.dtype)

    def dt_body(g_ref, w2_ref, a_ref, b_ref, da_ref, db_ref):
        dt = _dot(g_ref[...], w2_ref[...], NT)
        a, b = a_ref[...].astype(F32), b_ref[...].astype(F32)
        sg = _sigmoid(a)
        da_ref[...] = (dt * b * (sg * (1.0 + a * (1.0 - sg)))).astype(da_ref.dtype)
        db_ref[...] = (dt * (a * sg)).astype(db_ref.dtype)

    row = pl.BlockSpec((tm, d), lambda j, i: (i, 0))
    w_up = pl.BlockSpec((d, tn), lambda j, i: (0, j))
    w_dn = pl.BlockSpec((tn, d), lambda j, i: (j, 0))
    tile = pl.BlockSpec((tm, tn), lambda j, i: (i, j))
    grid = (f_dim // tn, m // tm)

    def run_up(hb, w1b, w3b):
        return _pallas(up_body, name=name + "_up", grid=grid, in_specs=[row, w_up, w_up], out_specs=[tile, tile, tile],
                       out_shape=[jax.ShapeDtypeStruct((m, f_dim), ACT_DTYPE)] * 3,
                       compiler_params=_cparams("parallel", "parallel"))(hb, w1b, w3b)

    def forward(x, h, w1, w3, w2):
        hb = h.astype(ACT_DTYPE)
        w1b, w3b, w2b = w1.astype(MXU_DTYPE), w3.astype(MXU_DTYPE), w2.astype(MXU_DTYPE)
        t, a, b = run_up(hb, w1b, w3b)
        return _mm(t, w2b, name=name + "_down", res=x), (hb, w1b, w3b, w2b, t, a, b, jnp.zeros((0,), h.dtype))

    @jax.custom_vjp
    def f(x, h, w1, w3, w2):
        return forward(x, h, w1, w3, w2)[0]

    def bwd(res, g):
        hb, w1b, w3b, w2b, t, a, b, like_h = res
        gb = g.astype(ACT_DTYPE)
        da, db = _pallas(dt_body, name=name + "_dt", grid=grid, in_specs=[row, w_dn, tile, tile], out_specs=[tile, tile],
                         out_shape=[jax.ShapeDtypeStruct((m, f_dim), ACT_DTYPE)] * 2,
                         compiler_params=_cparams("parallel", "parallel"))(gb, w2b, a, b)
        dw2 = _mm(t, gb, ta=True, name=name + "_dw2")
        def dh_body(da_ref, db_ref, w1_ref, w3_ref, o_ref):
            o_ref[...] = (_dot(da_ref[...], w1_ref[...], NT) + _dot(db_ref[...], w3_ref[...], NT)).astype(o_ref.dtype)

        wide = pl.BlockSpec((tm, f_dim), lambda i: (i, 0))
        w_all = pl.BlockSpec((d, f_dim), lambda i: (0, 0))
        dh = _pallas(dh_body, name=name + "_dx", grid=(m // tm,), in_specs=[wide, wide, w_all, w_all],
                     out_specs=pl.BlockSpec((tm, d), lambda i: (i, 0)), out_shape=jax.ShapeDtypeStruct((m, d), like_h.dtype),
                     compiler_params=_cparams("parallel"))(da, db, w1b, w3b)
        dw1 = _mm(hb, da, ta=True, name=name + "_dw1")
        dw3 = _mm(hb, db, ta=True, name=name + "_dw3")
        return g, dh, dw1, dw3, dw2

    f.defvjp(lambda *args: forward(*args), bwd)
    return f(x, h, w1, w3, w2)


MERGE_ROW_TILE = 256


def gated_merge_out(x, ya_t, yb, yc, ga, gb, gc, wa, wb, wc, wo, name="merge"):
    s, d = x.shape
    e = yb.shape[1]
    tm = _tile(s, MERGE_ROW_TILE)

    def branches(ya_ref, yb_ref, yc_ref, wa_ref, wb_ref, wc_ref):
        return (_dot(ya_ref[...], wa_ref[...], TN), _dot(yb_ref[...], wb_ref[...], NN), _dot(yc_ref[...], wc_ref[...], NN))

    def fwd_body(x_ref, ya_ref, yb_ref, yc_ref, ga_ref, gb_ref, gc_ref, wa_ref, wb_ref, wc_ref, wo_ref, o_ref, m_ref):
        pa, pb, pc = branches(ya_ref, yb_ref, yc_ref, wa_ref, wb_ref, wc_ref)
        merged = _sigmoid(ga_ref[...]) * pa + _sigmoid(gb_ref[...]) * pb + _sigmoid(gc_ref[...]) * pc
        mb = merged.astype(m_ref.dtype)
        m_ref[...] = mb
        o_ref[...] = x_ref[...] + _dot(mb, wo_ref[...], NN)

    def bwd_body(g_ref, ya_ref, yb_ref, yc_ref, ga_ref, gb_ref, gc_ref, wa_ref, wb_ref, wc_ref, wo_ref,
                 dga_ref, dgb_ref, dgc_ref, dpa_ref, dpb_ref, dpc_ref):
        dm = _dot(g_ref[...], wo_ref[...], NT)
        ps = branches(ya_ref, yb_ref, yc_ref, wa_ref, wb_ref, wc_ref)
        for p_i, gate_ref, dg_ref, dp_ref in zip(ps, (ga_ref, gb_ref, gc_ref), (dga_ref, dgb_ref, dgc_ref),
                                                 (dpa_ref, dpb_ref, dpc_ref)):
            sg = _sigmoid(gate_ref[...])
            dg_ref[...] = dm * p_i * (sg * (1.0 - sg))
            dp_ref[...] = (dm * sg).astype(dp_ref.dtype)

    rows = lambda width: pl.BlockSpec((tm, width), lambda i: (i, 0))
    cols_t = pl.BlockSpec((e, tm), lambda i: (0, i))
    whole = lambda r, c: pl.BlockSpec((r, c), lambda i: (0, 0))
    in_common = [cols_t, rows(e), rows(e), rows(d), rows(d), rows(d), whole(e, d), whole(e, d), whole(e, d), whole(d, d)]

    def forward(x, ya_t, yb, yc, ga, gb, gc, wa, wb, wc, wo):
        cast = lambda t: t.astype(ACT_DTYPE)
        ops = (cast(ya_t), cast(yb), cast(yc), ga, gb, gc, cast(wa), cast(wb), cast(wc), cast(wo))
        out, merged = _pallas(
            fwd_body, name=name + "_fwd", grid=(s // tm,), in_specs=[rows(d)] + in_common, out_specs=[rows(d), rows(d)],
            out_shape=[jax.ShapeDtypeStruct((s, d), F32), jax.ShapeDtypeStruct((s, d), ACT_DTYPE)],
            compiler_params=_cparams("parallel"))(x, *ops)
        like = tuple(jnp.zeros((0,), t.dtype) for t in (ya_t, yb, yc))
        return out, (ops, merged, like)

    @jax.custom_vjp
    def f(*args):
        return forward(*args)[0]

    def bwd(res, g):
        ops, merged, like = res
        ya_b, yb_b, yc_b, ga, gb, gc, wa_b, wb_b, wc_b, wo_b = ops
        gbf = g.astype(ACT_DTYPE)
        gate_ct = jax.ShapeDtypeStruct((s, d), F32)
        branch_ct = jax.ShapeDtypeStruct((s, d), ACT_DTYPE)
        dga, dgb, dgc, dpa, dpb, dpc = _pallas(
            bwd_body, name=name + "_bwd", grid=(s // tm,), in_specs=[rows(d)] + in_common, out_specs=[rows(d)] * 6,
            out_shape=[gate_ct] * 3 + [branch_ct] * 3, compiler_params=_cparams("parallel"))(gbf, *ops)
        dya_t = _mm(wa_b, dpa, tb=True, out_dtype=like[0].dtype, name=name + "_dya")
        dyb = _mm(dpb, wb_b, tb=True, out_dtype=like[1].dtype, name=name + "_dyb")
        dyc = _mm(dpc, wc_b, tb=True, out_dtype=like[2].dtype, name=name + "_dyc")
        dwa = _mm(ya_b, dpa, name=name + "_dwa")
        dwb = _mm(yb_b, dpb, ta=True, name=name + "_dwb")
        dwc = _mm(yc_b, dpc, ta=True, name=name + "_dwc")
        dwo = _mm(merged, gbf, ta=True, name=name + "_dwo")
        return g, dya_t, dyb, dyc, dga, dgb, dgc, dwa, dwb, dwc, dwo

    f.defvjp(lambda *args: forward(*args), bwd)
    return f(x, ya_t, yb, yc, ga, gb, gc, wa, wb, wc, wo)


def _row_tile(rows, width):
    return _tile(rows, max(SUBLANES, (512 * 1024) // width), 16)


def rmsnorm(x, g, out_dtype=F32, name="rms"):
    rows, d = x.shape
    tr = _row_tile(rows, d)
    n_steps = rows // tr

    def fwd_body(x_ref, g_ref, o_ref):
        xv = x_ref[...].astype(F32)
        r = lax.rsqrt(jnp.mean(xv * xv, axis=-1, keepdims=True) + EPS)
        o_ref[...] = (xv * r * g_ref[...]).astype(o_ref.dtype)

    def bwd_body(x_ref, g_ref, dy_ref, dx_ref, dg_ref):
        xv = x_ref[...].astype(F32)
        dy = dy_ref[...].astype(F32)
        r = lax.rsqrt(jnp.mean(xv * xv, axis=-1, keepdims=True) + EPS)
        xh = xv * r
        dxh = dy * g_ref[...]
        dx_ref[...] = (r * (dxh - xh * jnp.mean(dxh * xh, axis=-1, keepdims=True))).astype(dx_ref.dtype)

        @pl.when(pl.program_id(0) == 0)
        def _():
            dg_ref[...] = jnp.zeros_like(dg_ref)

        dg_ref[...] += jnp.sum(dy * xh, axis=0, keepdims=True)

    row_spec = pl.BlockSpec((tr, d), lambda i: (i, 0))
    vec_spec = pl.BlockSpec((1, d), lambda i: (0, 0))

    def run_fwd(x, g):
        return _pallas(fwd_body, name=name + "_fwd", grid=(n_steps,), in_specs=[row_spec, vec_spec], out_specs=row_spec,
                       out_shape=jax.ShapeDtypeStruct((rows, d), out_dtype), compiler_params=_cparams("parallel"))(
            x, g.reshape(1, d).astype(F32))

    @jax.custom_vjp
    def f(x, g):
        return run_fwd(x, g)

    def fwd(x, g):
        return run_fwd(x, g), (x, g)

    def bwd(res, dy):
        x, g = res
        dx, dg = _pallas(
            bwd_body, name=name + "_bwd", grid=(n_steps,), in_specs=[row_spec, vec_spec, row_spec],
            out_specs=[row_spec, vec_spec],
            out_shape=[jax.ShapeDtypeStruct((rows, d), x.dtype), jax.ShapeDtypeStruct((1, d), F32)],
            compiler_params=_cparams("arbitrary"))(x, g.reshape(1, d).astype(F32), dy)
        return dx, dg.reshape(g.shape).astype(g.dtype)

    f.defvjp(fwd, bwd)
    return f(x, g)


def rmsnorm_keep(x, g, out_dtype=F32, name="rms"):
    rows, d = x.shape
    tr = _row_tile(rows, d)
    n_steps = rows // tr

    def bwd_body(x_ref, g_ref, dy_ref, dkeep_ref, dx_ref, dg_ref):
        xv = x_ref[...]
        dy = dy_ref[...].astype(F32)
        r = lax.rsqrt(jnp.mean(xv * xv, axis=-1, keepdims=True) + EPS)
        xh = xv * r
        dxh = dy * g_ref[...]
        dx_ref[...] = dkeep_ref[...] + r * (dxh - xh * jnp.mean(dxh * xh, axis=-1, keepdims=True))

        @pl.when(pl.program_id(0) == 0)
        def _():
            dg_ref[...] = jnp.zeros_like(dg_ref)

        dg_ref[...] += jnp.sum(dy * xh, axis=0, keepdims=True)

    row_spec = pl.BlockSpec((tr, d), lambda i: (i, 0))
    vec_spec = pl.BlockSpec((1, d), lambda i: (0, 0))

    @jax.custom_vjp
    def f(x, g):
        return rmsnorm(x, g, out_dtype, name), x

    def fwd(x, g):
        return (rmsnorm(x, g, out_dtype, name), x), (x, g)

    def bwd(res, cts):
        x, g = res
        dy, dkeep = cts
        dx, dg = _pallas(
            bwd_body, name=name + "_bwd", grid=(n_steps,), in_specs=[row_spec, vec_spec, row_spec, row_spec],
            out_specs=[row_spec, vec_spec],
            out_shape=[jax.ShapeDtypeStruct((rows, d), F32), jax.ShapeDtypeStruct((1, d), F32)],
            compiler_params=_cparams("arbitrary"))(x, g.reshape(1, d).astype(F32), dy, dkeep)
        return dx, dg.reshape(g.shape).astype(g.dtype)

    f.defvjp(fwd, bwd)
    return f(x, g)


def _rowdot(a, b, name):
    h, s, d = a.shape
    ts = _tile(s, 2048)

    def body(a_ref, b_ref, o_ref):
        o_ref[...] = jnp.sum(a_ref[...].astype(F32) * b_ref[...].astype(F32), axis=-1, keepdims=True)

    spec = pl.BlockSpec((None, ts, d), lambda hh, i: (hh, i, 0))
    return _pallas(body, name=name, grid=(h, s // ts), in_specs=[spec, spec],
                   out_specs=pl.BlockSpec((None, ts, 1), lambda hh, i: (hh, i, 0)),
                   out_shape=jax.ShapeDtypeStruct((h, s, 1), F32), compiler_params=_cparams("parallel", "parallel"))(a, b)


LOG2E = 1.4426950408889634


def mla_attention(q, k, v, scale, name="mla"):
    s, h, d = q.shape
    sk, dv = k.shape[0], v.shape[2]
    tq, tk = _tile(s, MLA_BWD_TQ), _tile(sk, MLA_TK)
    nq, nk = s // tq, sk // tk
    tqf, tkf = _tile(s, MLA_FWD_TQ), _tile(sk, MLA_FWD_TK)
    nkf = sk // tkf
    ones_rows = 16
    c = scale * LOG2E

    def fwd_body(qt_ref, k_ref, vt_ref, ot_ref, lse_ref, m_ref, acc_ref):
        j = pl.program_id(2)

        @pl.when(j == 0)
        def _():
            m_ref[...] = jnp.full_like(m_ref, NEG)
            acc_ref[...] = jnp.zeros_like(acc_ref)

        st = _dot(k_ref[...], qt_ref[...], NN)
        m_prev = m_ref[...]
        m_new = jnp.maximum(m_prev, jnp.max(st, axis=0, keepdims=True) * c)
        pt = jnp.exp2(st * c - m_new)
        acc_ref[...] = jnp.exp2(m_prev - m_new) * acc_ref[...] + _dot(vt_ref[...], pt, NN)
        m_ref[...] = m_new

        @pl.when(j == nkf - 1)
        def _():
            l = acc_ref[dv:dv + 1, :]
            ot_ref[...] = (acc_ref[:dv, :] / l).astype(ot_ref.dtype)
            lse_ref[...] = m_ref[...] + jnp.log2(l)

    def delta_body(ot_ref, dot_ref, o_ref):
        o_ref[...] = jnp.sum(ot_ref[...].astype(F32) * dot_ref[...].astype(F32), axis=0, keepdims=True)

    def bwd_body(qt_ref, k_ref, kt_ref, v_ref, dot_ref, lse_ref, dl_ref, dqt_ref, dk_hbm, dv_hbm, dq_acc, dk_acc, dv_acc):
        hh, i, j = pl.program_id(0), pl.program_id(1), pl.program_id(2)

        @pl.when(j == 0)
        def _():
            dq_acc[...] = jnp.zeros_like(dq_acc)

        @pl.when(i == 0)
        def _():
            dk_acc[j] = jnp.zeros((d, tk), F32)
            dv_acc[j] = jnp.zeros((dv, tk), F32)

        qt, dot_ = qt_ref[...], dot_ref[...]
        pt = jnp.exp2(_dot(k_ref[...], qt, NN) * c - lse_ref[...])
        dst = (pt * (_dot(v_ref[...], dot_, NN) - dl_ref[...])).astype(MXU_DTYPE)
        dv_acc[j] += _dot(dot_, pt, NT)
        dk_acc[j] += _dot(qt, dst, NT)
        dq_acc[...] += _dot(kt_ref[...], dst, NN)

        @pl.when(j == nk - 1)
        def _():
            dqt_ref[...] = dq_acc[...] * scale

        @pl.when(i == nq - 1)
        def _():
            dk_acc[j] = dk_acc[j] * scale
            pltpu.sync_copy(dk_acc.at[j], dk_hbm.at[hh, j])
            pltpu.sync_copy(dv_acc.at[j], dv_hbm.at[hh, j])

    def qt_spec(width):
        return pl.BlockSpec((None, width, tq), lambda hh, i, j: (hh, 0, i))

    def kt_spec(width):
        return pl.BlockSpec((None, width, tk), lambda hh, i, j: (hh, 0, j))

    def k_spec(width):
        return pl.BlockSpec((None, tk, width), lambda hh, i, j: (hh, j, 0))

    def layouts(q, k, v):
        cast = lambda t: t.astype(ACT_DTYPE)
        return (cast(jnp.transpose(q, (1, 2, 0))), cast(jnp.transpose(k, (1, 0, 2))), cast(jnp.transpose(k, (1, 2, 0))),
                cast(jnp.transpose(v, (1, 0, 2))), cast(jnp.transpose(v, (1, 2, 0))))

    def run_fwd(qt, kh, vt):
        vt_ones = jnp.concatenate([vt, jnp.ones((h, ones_rows, sk), vt.dtype)], axis=1)

        def qf_spec(width):
            return pl.BlockSpec((None, width, tqf), lambda hh, i, j: (hh, 0, i))

        kf_spec = pl.BlockSpec((None, tkf, d), lambda hh, i, j: (hh, j, 0))
        vf_spec = pl.BlockSpec((None, dv + ones_rows, tkf), lambda hh, i, j: (hh, 0, j))
        return _pallas(
            fwd_body, name=name + "_fwd", grid=(h, s // tqf, nkf), in_specs=[qf_spec(d), kf_spec, vf_spec],
            out_specs=[qf_spec(dv), qf_spec(1)],
            out_shape=[jax.ShapeDtypeStruct((h, dv, s), ACT_DTYPE), jax.ShapeDtypeStruct((h, 1, s), F32)],
            scratch_shapes=[pltpu.VMEM((1, tqf), F32), pltpu.VMEM((dv + ones_rows, tqf), F32)],
            compiler_params=_cparams("parallel", "parallel", "arbitrary"))(qt, kh, vt_ones)

    @jax.custom_vjp
    def f(q, k, v):
        qt, kh, _, _, vt = layouts(q, k, v)
        return run_fwd(qt, kh, vt)[0].reshape(h * dv, s)

    def fwd(q, k, v):
        qt, kh, kt, vh, vt = layouts(q, k, v)
        ot, lse = run_fwd(qt, kh, vt)
        return ot.reshape(h * dv, s), (qt, kh, kt, vh, ot, lse)

    def bwd(res, dy):
        qt, kh, kt, vh, ot, lse = res
        dot_ = dy.reshape(h, dv, s)
        ts = _tile(s, 2048)
        col = pl.BlockSpec((None, dv, ts), lambda hh, i: (hh, 0, i))
        delta = _pallas(delta_body, name=name + "_delta", grid=(h, s // ts), in_specs=[col, col],
                        out_specs=pl.BlockSpec((None, 1, ts), lambda hh, i: (hh, 0, i)),
                        out_shape=jax.ShapeDtypeStruct((h, 1, s), F32), compiler_params=_cparams("parallel", "parallel"))(ot, dot_)
        any_spec = pl.BlockSpec(memory_space=pl.ANY)
        dqt, dkt, dvt = _pallas(
            bwd_body, name=name + "_bwd", grid=(h, nq, nk),
            in_specs=[qt_spec(d), k_spec(d), kt_spec(d), k_spec(dv), qt_spec(dv), qt_spec(1), qt_spec(1)],
            out_specs=[qt_spec(d), any_spec, any_spec],
            out_shape=[jax.ShapeDtypeStruct((h, d, s), F32), jax.ShapeDtypeStruct((h, nk, d, tk), F32),
                       jax.ShapeDtypeStruct((h, nk, dv, tk), F32)],
            scratch_shapes=[pltpu.VMEM((d, tq), F32), pltpu.VMEM((nk, d, tk), F32), pltpu.VMEM((nk, dv, tk), F32)],
            compiler_params=_cparams("parallel", "arbitrary", "arbitrary"))(qt, kh, kt, vh, dot_, lse, delta)
        to_tokens = lambda t: jnp.transpose(t, (1, 3, 0, 2)).reshape(sk, h, t.shape[2])
        return jnp.transpose(dqt, (2, 0, 1)), to_tokens(dkt), to_tokens(dvt)

    f.defvjp(fwd, bwd)
    return f(q, k, v)


WATTN_TQ = 2 * C_BLOCK
WATTN_KW = WATTN_TQ + 2 * C_BLOCK


def window_attention(q, k, v, bias, sink, name="wattn"):
    hq, s, dh = q.shape
    g = hq // C_KV_HEADS
    tq, kw, half = WATTN_TQ, WATTN_KW, WATTN_KW // 2
    nt = s // tq
    scale = dh ** -0.5
    sink_b = jnp.broadcast_to(sink.astype(F32).reshape(hq, 1, 1), (hq, 1, LANES))
    neg = jnp.full((hq, C_BLOCK, C_BLOCK), NEG, F32)
    tile = jnp.concatenate(
        [jnp.concatenate([bias[:, cb - rb] if 0 <= cb - rb <= 2 else neg for cb in range(kw // C_BLOCK)], axis=2)
         for rb in range(tq // C_BLOCK)], axis=1)

    def key_bias(i):
        pos = lax.broadcasted_iota(jnp.int32, (1, kw), 1) + i * tq - C_BLOCK
        return jnp.where(jnp.logical_and(pos >= 0, pos < s), 0.0, NEG)

    def both(a_ref, b_ref):
        return jnp.concatenate([a_ref[...], b_ref[...]], axis=0)

    def fwd_body(q_ref, ka_ref, kb_ref, va_ref, vb_ref, b_ref, sk_ref, o_ref, lse_ref):
        kb_ = key_bias(pl.program_id(1))
        k_all, v_all = both(ka_ref, kb_ref), both(va_ref, vb_ref)
        for hh in range(g):
            sc = _dot(q_ref[hh], k_all, NT) * scale + b_ref[hh] + kb_
            snk = sk_ref[hh][:, :1]
            m = jnp.maximum(jnp.max(sc, axis=-1, keepdims=True), snk)
            p = jnp.exp(sc - m)
            l = jnp.sum(p, axis=-1, keepdims=True) + jnp.exp(snk - m)
            o_ref[hh] = (_dot(p, v_all, NN) / l).astype(o_ref.dtype)
            lse_ref[hh] = m + jnp.log(l)

    def bwd_body(q_ref, ka_ref, kb_ref, va_ref, vb_ref, b_ref, sk_ref, do_ref, lse_ref, dl_ref,
                 dq_ref, db_ref, dsink_ref, dk_hbm, dv_hbm, dk_acc, dv_acc):
        kv, i = pl.program_id(0), pl.program_id(1)

        @pl.when(i == 0)
        def _():
            dk_acc[...] = jnp.zeros_like(dk_acc)
            dv_acc[...] = jnp.zeros_like(dv_acc)
            db_ref[...] = jnp.zeros_like(db_ref)
            dsink_ref[...] = jnp.zeros_like(dsink_ref)

        kb_ = key_bias(i)
        k_all, v_all = both(ka_ref, kb_ref), both(va_ref, vb_ref)
        dk_t = jnp.zeros((kw, dh), F32)
        dv_t = jnp.zeros((kw, dh), F32)
        for hh in range(g):
            lse, dl, do = lse_ref[hh], dl_ref[hh], do_ref[hh]
            p = jnp.exp(_dot(q_ref[hh], k_all, NT) * scale + b_ref[hh] + kb_ - lse)
            ds = p * (_dot(do, v_all, NT) - dl)
            db_ref[hh] += ds
            total = jnp.broadcast_to(-jnp.sum(jnp.exp(sk_ref[hh][:, :1] - lse) * dl, axis=0, keepdims=True), (1, LANES))
            dsink_ref[hh] += jnp.where(lax.broadcasted_iota(jnp.int32, (1, LANES), 1) == 0, total, 0.0)
            dsb = (ds * scale).astype(MXU_DTYPE)
            dq_ref[hh] = _dot(dsb, k_all, NN).astype(dq_ref.dtype)
            dk_t += _dot(dsb, q_ref[hh], TN)
            dv_t += _dot(p, do, TN)
        rows = pl.ds(pl.multiple_of(i * tq, tq), kw)
        dk_acc[rows, :] += dk_t
        dv_acc[rows, :] += dv_t

        @pl.when(i == nt - 1)
        def _():
            pltpu.sync_copy(dk_acc, dk_hbm.at[kv])
            pltpu.sync_copy(dv_acc, dv_hbm.at[kv])

    def q_spec(width):
        return pl.BlockSpec((g, tq, width), lambda kv, i: (kv, i, 0))

    ka_spec = pl.BlockSpec((None, half, dh), lambda kv, i: (kv, i, 0))
    kb_spec = pl.BlockSpec((None, half, dh), lambda kv, i: (kv, i + 1, 0))
    b_spec = pl.BlockSpec((g, tq, kw), lambda kv, i: (kv, 0, 0))
    sk_spec = pl.BlockSpec((g, 1, LANES), lambda kv, i: (kv, 0, 0))

    def padded(t):
        return jnp.pad(t, ((0, 0), (C_BLOCK, C_BLOCK), (0, 0)))

    def run_fwd(q, kp, vp, tile, sink_b):
        return _pallas(
            fwd_body, name=name + "_fwd", grid=(C_KV_HEADS, nt),
            in_specs=[q_spec(dh), ka_spec, kb_spec, ka_spec, kb_spec, b_spec, sk_spec], out_specs=[q_spec(dh), q_spec(1)],
            out_shape=[jax.ShapeDtypeStruct((hq, s, dh), ACT_DTYPE), jax.ShapeDtypeStruct((hq, s, 1), F32)],
            compiler_params=_cparams("parallel", "parallel"))(q, kp, kp, vp, vp, tile, sink_b)

    @jax.custom_vjp
    def f(q, k, v, tile, sink_b):
        return run_fwd(q, padded(k), padded(v), tile, sink_b)[0]

    def fwd(q, k, v, tile, sink_b):
        kp, vp = padded(k), padded(v)
        o, lse = run_fwd(q, kp, vp, tile, sink_b)
        return o, (q, kp, vp, tile, sink_b, o, lse)

    def bwd(res, do):
        q, kp, vp, tile, sink_b, o, lse = res
        delta = _rowdot(o, do, name + "_delta")
        any_spec = pl.BlockSpec(memory_space=pl.ANY)
        acc = jax.ShapeDtypeStruct((C_KV_HEADS, s + 2 * C_BLOCK, dh), F32)
        dq, dtile, dsink, dkp, dvp = _pallas(
            bwd_body, name=name + "_bwd", grid=(C_KV_HEADS, nt),
            in_specs=[q_spec(dh), ka_spec, kb_spec, ka_spec, kb_spec, b_spec, sk_spec, q_spec(dh), q_spec(1), q_spec(1)],
            out_specs=[q_spec(dh), b_spec, sk_spec, any_spec, any_spec],
            out_shape=[jax.ShapeDtypeStruct((hq, s, dh), q.dtype), jax.ShapeDtypeStruct((hq, tq, kw), F32),
                       jax.ShapeDtypeStruct((hq, 1, LANES), F32), acc, acc],
            scratch_shapes=[pltpu.VMEM((s + 2 * C_BLOCK, dh), F32), pltpu.VMEM((s + 2 * C_BLOCK, dh), F32)],
            compiler_params=_cparams("parallel", "arbitrary"))(q, kp, kp, vp, vp, tile, sink_b, do, lse, delta)
        unpad = lambda t: t[:, C_BLOCK:-C_BLOCK].astype(kp.dtype)
        return dq, unpad(dkp), unpad(dvp), dtile, dsink

    f.defvjp(fwd, bwd)
    return f(q, k, v, tile, sink_b)


HG_PREP_ROWS = 256
HG_GROUP = 8
HG_INTRA_BLOCK = 256
HG_INTER_CHUNKS = 16
MLA_FWD_TQ, MLA_FWD_TK, MLA_BWD_TQ, MLA_TK = 2048, 2048, 2048, 1024


def _hdot(a, b, dims, ones_first=True):
    b16 = jnp.bfloat16
    ones, full = (a, b) if ones_first else (b, a)
    hi = full.astype(b16)
    rest = full - hi.astype(F32)
    mid = rest.astype(b16)
    lo = (rest - mid.astype(F32)).astype(b16)
    ones16 = ones.astype(b16)
    dn = (dims, ((), ()))

    def one(piece):
        lhs, rhs = (ones16, piece) if ones_first else (piece, ones16)
        return lax.dot_general(lhs, rhs, dn, preferred_element_type=F32)

    return one(hi) + one(mid) + one(lo)


HG_OUT_ROWS = 256


def hgrn_out(o, gate, g_out, name="hg_out"):
    s, c = o.shape
    tb = _tile(s, HG_OUT_ROWS)
    g_row = jnp.tile(g_out.astype(F32).reshape(1, B_DV), (1, c // B_DV))

    def head_mean(v):
        r = lax.broadcasted_iota(jnp.int32, (c, c), 0)
        cc = lax.broadcasted_iota(jnp.int32, (c, c), 1)
        same = jnp.where(r // B_DV == cc // B_DV, 1.0, 0.0).astype(F32)
        return _hdot(v, same, NN, ones_first=False) * (1.0 / B_DV)

    def fwd_body(o_ref, gate_ref, g_ref, y_ref):
        ov, gt = o_ref[...], gate_ref[...]
        xh = ov * lax.rsqrt(head_mean(ov * ov) + EPS)
        y_ref[...] = xh * g_ref[...] * (gt * _sigmoid(gt))

    def bwd_body(o_ref, gate_ref, g_ref, dy_ref, do_ref, dgate_ref, dg_ref):
        ov, gt, gv, dy = o_ref[...], gate_ref[...], g_ref[...], dy_ref[...]
        r = lax.rsqrt(head_mean(ov * ov) + EPS)
        xh = ov * r
        sg = _sigmoid(gt)
        dgate_ref[...] = dy * xh * gv * (sg * (1.0 + gt * (1.0 - sg)))
        dn = dy * (gt * sg)
        dxh = dn * gv
        do_ref[...] = r * (dxh - xh * head_mean(dxh * xh))

        @pl.when(pl.program_id(0) == 0)
        def _():
            dg_ref[...] = jnp.zeros_like(dg_ref)

        dg_ref[...] += jnp.sum(dn * xh, axis=0, keepdims=True)

    row = pl.BlockSpec((tb, c), lambda i: (i, 0))
    vec = pl.BlockSpec((1, c), lambda i: (0, 0))
    shape = jax.ShapeDtypeStruct((s, c), F32)

    def run_fwd(o, gate, g_row):
        return _pallas(fwd_body, name=name + "_fwd", grid=(s // tb,), in_specs=[row, row, vec], out_specs=row, out_shape=shape,
                       compiler_params=_cparams("parallel"))(o, gate, g_row)

    @jax.custom_vjp
    def f(o, gate, g_row):
        return run_fwd(o, gate, g_row)

    def fwd(o, gate, g_row):
        return run_fwd(o, gate, g_row), (o, gate, g_row)

    def bwd(res, dy):
        o, gate, g_row = res
        return tuple(_pallas(
            bwd_body, name=name + "_bwd", grid=(s // tb,), in_specs=[row, row, vec, row], out_specs=[row, row, vec],
            out_shape=[shape, shape, jax.ShapeDtypeStruct((1, c), F32)], compiler_params=_cparams("arbitrary"))(o, gate, g_row, dy))

    f.defvjp(fwd, bwd)
    return f(o, gate, g_row)


def hgrn_prep(q, z, lb, reverse, name):
    n_hp, s, tc = q.shape
    tb = _tile(s, HG_PREP_ROWS)
    ncb = tb // B_CHUNK

    def chunk_matrices():
        r = lax.broadcasted_iota(jnp.int32, (tb, tb), 0)
        cc = lax.broadcasted_iota(jnp.int32, (tb, tb), 1)
        same = r // B_CHUNK == cc // B_CHUNK
        tri = (cc >= r) if reverse else (cc <= r)
        cum = jnp.where(jnp.logical_and(same, tri), 1.0, 0.0).astype(F32)
        every = jnp.where(same, 1.0, 0.0).astype(F32)
        pr = lax.broadcasted_iota(jnp.int32, (ncb, tb), 0)
        pc = lax.broadcasted_iota(jnp.int32, (ncb, tb), 1)
        per_chunk = jnp.where(pc // B_CHUNK == pr, 1.0, 0.0).astype(F32)
        return cum, every, per_chunk

    def gates(zv, lbv):
        e = jnp.exp(-jnp.abs(zv))
        big, small = 1.0 / (1.0 + e), e / (1.0 + e)
        sig = jnp.where(zv >= 0, big, small)
        nsig = jnp.where(zv >= 0, small, big)
        f = lbv + (1.0 - lbv) * sig
        return sig, nsig, f, jnp.log(jnp.maximum(f, TINY)), (1.0 - lbv) * nsig

    def fwd_body(q_ref, z_ref, lb_ref, qd_ref, ki_ref, ke_ref, dec_ref):
        cum, every, per_chunk = chunk_matrices()
        _, _, _, lf, key = gates(z_ref[...], lb_ref[...])
        b = _hdot(cum, lf, NN)
        tot = _hdot(every, lf, NN)
        qd_ref[...] = q_ref[...] * jnp.exp(b)
        ki_ref[...] = key * jnp.exp(-b)
        ke_ref[...] = key * jnp.exp(tot - b)
        dec_ref[...] = jnp.exp(_hdot(per_chunk, lf, NN))

    def bwd_body(q_ref, z_ref, lb_ref, dqd_ref, dki_ref, dke_ref, ddec_ref, dq_ref, dz_ref, dlb_ref):
        cum, every, per_chunk = chunk_matrices()
        lbv = lb_ref[...]
        sig, nsig, f, lf, key = gates(z_ref[...], lbv)
        b = _hdot(cum, lf, NN)
        tot = _hdot(every, lf, NN)
        e_b, e_nb, e_tb = jnp.exp(b), jnp.exp(-b), jnp.exp(tot - b)
        dqd, dki, dke = dqd_ref[...], dki_ref[...], dke_ref[...]
        dq_ref[...] = dqd * e_b
        dkey = dki * e_nb + dke * e_tb
        t_end = dke * key * e_tb
        db = dqd * q_ref[...] * e_b - dki * key * e_nb - t_end
        dtot = ddec_ref[...] * jnp.exp(_hdot(per_chunk, lf, NN)) + _hdot(per_chunk, t_end, NN)
        dlf = _hdot(cum, db, TN) + _hdot(per_chunk, dtot, TN)
        df = jnp.where(f > TINY, dlf / f, 0.0)
        one_m_lb = 1.0 - lbv
        dz_ref[...] = (df - dkey) * one_m_lb * sig * nsig
        dlb_part = jnp.sum(df * nsig - dkey * nsig, axis=0, keepdims=True)

        @pl.when(pl.program_id(1) == 0)
        def _():
            dlb_ref[...] = jnp.zeros_like(dlb_ref)

        dlb_ref[...] += dlb_part

    tok = pl.BlockSpec((None, tb, tc), lambda j, i: (j, i, 0))
    vec = pl.BlockSpec((None, 1, tc), lambda j, i: (j, 0, 0))
    chk = pl.BlockSpec((None, ncb, tc), lambda j, i: (j, i, 0))
    grid = (n_hp, s // tb)
    tok_shape = jax.ShapeDtypeStruct((n_hp, s, tc), F32)
    chk_shape = jax.ShapeDtypeStruct((n_hp, s // B_CHUNK, tc), F32)

    def run_fwd(q, z, lb):
        return _pallas(fwd_body, name=name + "_fwd", grid=grid, in_specs=[tok, tok, vec], out_specs=[tok, tok, tok, chk],
                       out_shape=[tok_shape, tok_shape, tok_shape, chk_shape],
                       compiler_params=_cparams("parallel", "parallel"))(q, z, lb)

    @jax.custom_vjp
    def f(q, z, lb):
        return tuple(run_fwd(q, z, lb))

    def fwd(q, z, lb):
        return tuple(run_fwd(q, z, lb)), (q, z, lb)

    def bwd(res, cts):
        q, z, lb = res
        dq, dz, dlb = _pallas(
            bwd_body, name=name + "_bwd", grid=grid, in_specs=[tok, tok, vec, tok, tok, tok, chk], out_specs=[tok, tok, vec],
            out_shape=[tok_shape, tok_shape, jax.ShapeDtypeStruct((n_hp, 1, tc), F32)],
            compiler_params=_cparams("parallel", "arbitrary"))(q, z, lb, *cts)
        return dq, dz, dlb

    f.defvjp(fwd, bwd)
    return f(q, z, lb)


def _pair_cols(ref, hh, width):
    return ref[:, hh * width:(hh + 1) * width]


def hgrn_intra(qd, ki, v, reverse, name):
    s = qd.shape[1]
    tb = _tile(s, HG_INTRA_BLOCK)
    wk, wv = HG_GROUP * B_DK, HG_GROUP * B_DV

    def mask():
        r = lax.broadcasted_iota(jnp.int32, (tb, tb), 0)
        c = lax.broadcasted_iota(jnp.int32, (tb, tb), 1)
        return jnp.logical_and(r // B_CHUNK == c // B_CHUNK, (c >= r) if reverse else (c <= r))

    def fwd_body(q_ref, k_ref, v_ref, o_ref):
        msk = mask()
        for hh in range(HG_GROUP):
            sc = jnp.where(msk, _dot(_pair_cols(q_ref, hh, B_DK), _pair_cols(k_ref, hh, B_DK), NT), 0.0)
            o_ref[:, hh * B_DV:(hh + 1) * B_DV] = _dot(sc, _pair_cols(v_ref, hh, B_DV), NN)

    def bwd_body(q_ref, k_ref, v_ref, do_ref, dq_ref, dk_ref, dv_ref):
        msk = mask()
        for hh in range(HG_GROUP):
            q, k = _pair_cols(q_ref, hh, B_DK), _pair_cols(k_ref, hh, B_DK)
            vv, do = _pair_cols(v_ref, hh, B_DV), _pair_cols(do_ref, hh, B_DV)
            sc = jnp.where(msk, _dot(q, k, NT), 0.0)
            ds = jnp.where(msk, _dot(do, vv, NT), 0.0)
            dq_ref[:, hh * B_DK:(hh + 1) * B_DK] = _dot(ds, k, NN)
            dk_ref[:, hh * B_DK:(hh + 1) * B_DK] = _dot(ds, q, TN)
            dv_ref[:, hh * B_DV:(hh + 1) * B_DV] = _dot(sc, do, TN)

    ks = pl.BlockSpec((None, tb, wk), lambda hp, i: (hp, i, 0))
    vs = pl.BlockSpec((None, tb, wv), lambda hp, i: (hp, i, 0))
    grid = (B_HEADS // HG_GROUP, s // tb)

    def run_fwd(qd, ki, v):
        return _pallas(fwd_body, name=name + "_fwd", grid=grid, in_specs=[ks, ks, vs], out_specs=vs,
                       out_shape=jax.ShapeDtypeStruct(v.shape, F32), compiler_params=_cparams("parallel", "parallel"))(qd, ki, v)

    @jax.custom_vjp
    def f(qd, ki, v):
        return run_fwd(qd, ki, v)

    def fwd(qd, ki, v):
        return run_fwd(qd, ki, v), (qd, ki, v)

    def bwd(res, do):
        qd, ki, v = res
        return tuple(_pallas(
            bwd_body, name=name + "_bwd", grid=grid, in_specs=[ks, ks, vs, vs], out_specs=[ks, ks, vs],
            out_shape=[jax.ShapeDtypeStruct(qd.shape, F32), jax.ShapeDtypeStruct(ki.shape, F32),
                       jax.ShapeDtypeStruct(v.shape, F32)],
            compiler_params=_cparams("parallel", "parallel"))(qd, ki, v, do))

    f.defvjp(fwd, bwd)
    return f(qd, ki, v)


def hgrn_inter(qd, ke, v, dec, reverse, name):
    s = qd.shape[1]
    nc = s // B_CHUNK
    cpb = HG_INTER_CHUNKS if nc % HG_INTER_CHUNKS == 0 else nc
    tb = cpb * B_CHUNK
    nblk = nc // cpb
    wk, wv = HG_GROUP * B_DK, HG_GROUP * B_DV
    n_hp = B_HEADS // HG_GROUP

    def rows(c):
        return pl.ds(c * B_CHUNK, B_CHUNK)

    def kcols(hh):
        return slice(hh * B_DK, (hh + 1) * B_DK)

    def vcols(hh):
        return slice(hh * B_DV, (hh + 1) * B_DV)

    def order(flip):
        return reversed(range(cpb)) if flip else range(cpb)

    def fwd_body(q_ref, k_ref, v_ref, dec_ref, o_ref, st_ref, state):
        @pl.when(pl.program_id(1) == 0)
        def _():
            state[...] = jnp.zeros_like(state)

        for c in order(reverse):
            for hh in range(HG_GROUP):
                st = state[hh]
                st_ref[c, hh] = st
                o_ref[rows(c), vcols(hh)] = _dot(q_ref[rows(c), kcols(hh)], st, NT)
                state[hh] = st * dec_ref[pl.ds(c, 1), kcols(hh)] + _dot(v_ref[rows(c), vcols(hh)], k_ref[rows(c), kcols(hh)], TN)

    def bwd_body(q_ref, k_ref, v_ref, dec_ref, st_ref, do_ref, dq_ref, dk_ref, dv_ref, ddec_ref, dstate):
        @pl.when(pl.program_id(1) == 0)
        def _():
            dstate[...] = jnp.zeros_like(dstate)

        for c in order(not reverse):
            for hh in range(HG_GROUP):
                dst = dstate[hh]
                st = st_ref[c, hh]
                do_c = do_ref[rows(c), vcols(hh)]
                dk_ref[rows(c), kcols(hh)] = _dot(v_ref[rows(c), vcols(hh)], dst, NN)
                dv_ref[rows(c), vcols(hh)] = _dot(k_ref[rows(c), kcols(hh)], dst, NT)
                ddec_ref[pl.ds(c, 1), kcols(hh)] = jnp.sum(dst * st, axis=0, keepdims=True)
                dq_ref[rows(c), kcols(hh)] = _dot(do_c, st, NN)
                dstate[hh] = dst * dec_ref[pl.ds(c, 1), kcols(hh)] + _dot(do_c, q_ref[rows(c), kcols(hh)], TN)

    def specs(flip):
        blk = (lambda i: nblk - 1 - i) if flip else (lambda i: i)
        tok_k = pl.BlockSpec((None, tb, wk), lambda hp, i: (hp, blk(i), 0))
        tok_v = pl.BlockSpec((None, tb, wv), lambda hp, i: (hp, blk(i), 0))
        chk = pl.BlockSpec((None, cpb, wk), lambda hp, i: (hp, blk(i), 0))
        sts = pl.BlockSpec((None, cpb, HG_GROUP, B_DV, B_DK), lambda hp, i: (hp, blk(i), 0, 0, 0))
        return tok_k, tok_v, chk, sts

    scratch = [pltpu.VMEM((HG_GROUP, B_DV, B_DK), F32)]

    def run_fwd(qd, ke, v, dec):
        tok_k, tok_v, chk, sts = specs(reverse)
        return _pallas(
            fwd_body, name=name + "_fwd", grid=(n_hp, nblk), in_specs=[tok_k, tok_k, tok_v, chk], out_specs=[tok_v, sts],
            out_shape=[jax.ShapeDtypeStruct(v.shape, F32), jax.ShapeDtypeStruct((n_hp, nc, HG_GROUP, B_DV, B_DK), F32)],
            scratch_shapes=scratch, compiler_params=_cparams("parallel", "arbitrary"))(qd, ke, v, dec)

    @jax.custom_vjp
    def f(qd, ke, v, dec):
        return run_fwd(qd, ke, v, dec)[0]

    def fwd(qd, ke, v, dec):
        o, st = run_fwd(qd, ke, v, dec)
        return o, (qd, ke, v, dec, st)

    def bwd(res, do):
        qd, ke, v, dec, st = res
        tok_k, tok_v, chk, sts = specs(not reverse)
        return tuple(_pallas(
            bwd_body, name=name + "_bwd", grid=(n_hp, nblk), in_specs=[tok_k, tok_k, tok_v, chk, sts, tok_v],
            out_specs=[tok_k, tok_k, tok_v, chk],
            out_shape=[jax.ShapeDtypeStruct(qd.shape, F32), jax.ShapeDtypeStruct(ke.shape, F32),
                       jax.ShapeDtypeStruct(v.shape, F32), jax.ShapeDtypeStruct(dec.shape, F32)],
            scratch_shapes=scratch, compiler_params=_cparams("parallel", "arbitrary"))(qd, ke, v, dec, st, do))

    f.defvjp(fwd, bwd)
    return f(qd, ke, v, dec)


def loss_head(y, target, name="loss"):
    s, d = y.shape
    tr = _row_tile(s, d)

    def body(y_ref, t_ref, o_ref):
        @pl.when(pl.program_id(0) == 0)
        def _():
            o_ref[...] = jnp.zeros_like(o_ref)

        e = y_ref[...] - t_ref[...]
        part = jnp.sum(jnp.sum(e * e, axis=-1, keepdims=True), axis=0, keepdims=True) * (0.5 / d)
        o_ref[...] += jnp.broadcast_to(part, o_ref.shape)

    spec = pl.BlockSpec((tr, d), lambda i: (i, 0))

    def run(y, t):
        out = _pallas(body, name=name, grid=(s // tr,), in_specs=[spec, spec],
                      out_specs=pl.BlockSpec((SUBLANES, LANES), lambda i: (0, 0)),
                      out_shape=jax.ShapeDtypeStruct((SUBLANES, LANES), F32), compiler_params=_cparams("arbitrary"))(y, t)
        return out[0, 0]

    @jax.custom_vjp
    def f(y, t):
        return run(y, t)

    def fwd(y, t):
        return run(y, t), (y, t)

    def bwd(res, g):
        y, t = res
        dy = g * (y - t) * (1.0 / d)
        return dy, -dy

    f.defvjp(fwd, bwd)
    return f(y, target)


def _mesh_pos():
    return lax.axis_index("x"), lax.axis_index("y"), lax.axis_index("c")


def all_gather_shards(shards):
    n = len(shards)

    def body(*refs):
        ins, outs = refs[:n], refs[n:2 * n]
        send_sems, recv_sems, local_sems = refs[2 * n:]
        x, y, c = _mesh_pos()
        me, sibling = (x, y, c), (x, y, 1 - c)
        chips = [(1 - x, y), (x, 1 - y), (1 - x, 1 - y)]

        def slot(t, px, py, pc):
            return outs[t].at[4 * px + 2 * py + pc]

        def copy(t, k, block, to, src=None):
            return pltpu.make_async_remote_copy(
                src_ref=slot(t, *block) if src is None else src, dst_ref=slot(t, *block), send_sem=send_sems.at[t, k],
                recv_sem=recv_sems.at[t, k], device_id=to, device_id_type=pl.DeviceIdType.MESH)

        mine = [pltpu.make_async_copy(ins[t], slot(t, *me), local_sems.at[t]) for t in range(n)]
        for cp in mine:
            cp.start()
        first = []
        for t in range(n):
            first.append(copy(t, 0, me, sibling, src=ins[t]))
            first += [copy(t, 1 + j, me, (*chip, c), src=ins[t]) for j, chip in enumerate(chips)]
        for cp in first:
            cp.start()
        passed = []
        for j, chip in enumerate(chips):
            for t in range(n):
                copy(t, 1 + j, (*chip, c), me).wait_recv()
                cp = copy(t, 4 + j, (*chip, c), sibling)
                cp.start()
                passed.append(cp)
        for t in range(n):
            copy(t, 0, sibling, me).wait_recv()
            for j, chip in enumerate(chips):
                copy(t, 4 + j, (*chip, 1 - c), me).wait_recv()
        for cp in first + passed:
            cp.wait_send()
        for cp in mine:
            cp.wait()

    any_spec = pl.BlockSpec(memory_space=pl.ANY)
    return _pallas(
        body, name="all_gather_weights", out_shape=[jax.ShapeDtypeStruct((N_DEV, *s.shape), s.dtype) for s in shards],
        in_specs=[any_spec] * n, out_specs=[any_spec] * n,
        scratch_shapes=[pltpu.SemaphoreType.DMA((n, 7)), pltpu.SemaphoreType.DMA((n, 7)), pltpu.SemaphoreType.DMA((n,))],
    )(*shards)


def all_to_all_blocks(stacks):
    n = len(stacks)

    def body(*refs):
        ins, outs = refs[:n], refs[n:2 * n]
        send_sems, recv_sems, local_sems = refs[2 * n:]
        x, y, c = _mesh_pos()
        me = 4 * x + 2 * y + c
        mine = [pltpu.make_async_copy(ins[t].at[me], outs[t].at[me], local_sems.at[t]) for t in range(n)]
        for cp in mine:
            cp.start()
        copies = []
        for k in range(1, N_DEV):
            px = 1 - x if k & 4 else x
            py = 1 - y if k & 2 else y
            pc = 1 - c if k & 1 else c
            for t in range(n):
                cp = pltpu.make_async_remote_copy(
                    src_ref=ins[t].at[4 * px + 2 * py + pc], dst_ref=outs[t].at[me], send_sem=send_sems.at[t, k - 1],
                    recv_sem=recv_sems.at[t, k - 1], device_id=(px, py, pc), device_id_type=pl.DeviceIdType.MESH)
                cp.start()
                copies.append(cp)
        for cp in copies:
            cp.wait_recv()
        for cp in copies:
            cp.wait_send()
        for cp in mine:
            cp.wait()

    any_spec = pl.BlockSpec(memory_space=pl.ANY)
    return _pallas(
        body, name="all_to_all_grads", out_shape=[jax.ShapeDtypeStruct(s.shape, s.dtype) for s in stacks],
        in_specs=[any_spec] * n, out_specs=[any_spec] * n,
        scratch_shapes=[pltpu.SemaphoreType.DMA((n, 7)), pltpu.SemaphoreType.DMA((n, 7)), pltpu.SemaphoreType.DMA((n,))],
    )(*stacks)


def all_gather_small(v):
    r, w = v.shape

    def body(x_ref, out_ref, send_sems, recv_sems):
        x, y, c = _mesh_pos()
        me = 4 * x + 2 * y + c
        copies = []
        for k in range(1, N_DEV):
            px = 1 - x if k & 4 else x
            py = 1 - y if k & 2 else y
            pc = 1 - c if k & 1 else c
            cp = pltpu.make_async_remote_copy(
                src_ref=x_ref, dst_ref=out_ref.at[me], send_sem=send_sems.at[k - 1], recv_sem=recv_sems.at[k - 1],
                device_id=(px, py, pc), device_id_type=pl.DeviceIdType.MESH)
            cp.start()
            copies.append(cp)
        out_ref[me] = x_ref[...]
        for cp in copies:
            cp.wait_recv()
        for cp in copies:
            cp.wait_send()

    vmem = pl.BlockSpec(memory_space=pltpu.VMEM)
    return _pallas(
        body, name="all_gather_small", out_shape=jax.ShapeDtypeStruct((N_DEV, r, w), v.dtype), in_specs=[vmem],
        out_specs=vmem, scratch_shapes=[pltpu.SemaphoreType.DMA((7,)), pltpu.SemaphoreType.DMA((7,))],
    )(v)


def adamw_rows(parts, w, m, v, name):
    n, r, lanes = parts.shape
    tr = _tile(r, max(SUBLANES, (256 * 1024) // lanes), SUBLANES)
    c1 = 1.0 / (1.0 - ADAM_B1 ** ADAM_STEP)
    c2 = 1.0 / (1.0 - ADAM_B2 ** ADAM_STEP)

    def body(p_ref, w_ref, m_ref, v_ref, g_ref, d_ref, nm_ref, nv_ref):
        g = p_ref[0].astype(F32)
        for j in range(1, n):
            g = g + p_ref[j].astype(F32)
        nm = ADAM_B1 * m_ref[...] + (1.0 - ADAM_B1) * g
        nv = ADAM_B2 * v_ref[...] + (1.0 - ADAM_B2) * (g * g)
        g_ref[...] = g
        nm_ref[...] = nm
        nv_ref[...] = nv
        d_ref[...] = -ADAM_LR * ((nm * c1) / (jnp.sqrt(nv * c2) + ADAM_EPS) + ADAM_WD * w_ref[...])

    row = pl.BlockSpec((tr, lanes), lambda i: (i, 0))
    out = jax.ShapeDtypeStruct((r, lanes), F32)
    return _pallas(body, name=name, grid=(r // tr,), in_specs=[pl.BlockSpec((n, tr, lanes), lambda i: (0, i, 0)), row, row, row],
                   out_specs=[row, row, row, row], out_shape=[out, out, out, out], compiler_params=_cparams("parallel"))(
        parts, w, m, v)


def _padded(n):
    return -(-n // PACK_QUANTUM) * PACK_QUANTUM


def _pack(pieces, total_rows=None):
    flat = []
    for p in pieces:
        p = p.reshape(-1).astype(F32)
        flat.append(jnp.pad(p, (0, _padded(p.size) - p.size)))
    out = jnp.concatenate(flat).reshape(-1, LANES)
    if total_rows is not None and out.shape[0] != total_rows:
        out = jnp.pad(out, ((0, total_rows - out.shape[0]), (0, 0)))
    return out


def _pack_rows(sizes):
    rows = sum(_padded(n) for n in sizes) // LANES
    return -(-rows // PACK_ROW_TILE) * PACK_ROW_TILE


def _unpack(rows, shapes):
    lead = rows.shape[:-2]
    flat = rows.reshape(*lead, -1)
    out, off = [], 0
    for shp in shapes:
        n = int(np.prod(shp))
        out.append(flat[..., off:off + n].reshape(*lead, *shp))
        off += _padded(n)
    return out


def _shards_to_full(stacked, axis):
    moved = jnp.moveaxis(stacked, 0, axis)
    shp = list(stacked.shape[1:])
    shp[axis] *= N_DEV
    return moved.reshape(shp)


def _full_to_shards(full, axis):
    shp = list(full.shape)
    shp[axis:axis + 1] = [N_DEV, shp[axis] // N_DEV]
    return jnp.moveaxis(full.reshape(shp), axis, 0)


def _heads(t, n, d, dtype=ACT_DTYPE):
    return jnp.transpose(t.reshape(t.shape[0], n, d), (1, 0, 2)).astype(dtype)


def _unheads(t):
    return jnp.transpose(t, (1, 0, 2)).reshape(t.shape[1], -1)


def _rope_tables(s):
    half = A_ROPE // 2
    inv = ROPE_THETA ** (-jnp.arange(half, dtype=F32) / half)
    ang = jnp.arange(s, dtype=jnp.int32).astype(F32)[:, None] * inv[None, :]
    return jnp.cos(ang), jnp.sin(ang)


def _rope(t, cos, sin):
    half = A_ROPE // 2
    t1, t2 = t[..., :half], t[..., half:]
    c, sn = cos[:, None, :], sin[:, None, :]
    return jnp.concatenate([t1 * c - t2 * sn, t1 * sn + t2 * c], axis=-1)


def _t5_bucket(rel):
    nb = REL_BUCKETS // 2
    max_exact = nb // 2
    ret = (rel > 0).astype(jnp.int32) * nb
    n = jnp.abs(rel)
    large = max_exact + (jnp.log(jnp.maximum(n, 1).astype(F32) / max_exact)
                         / math.log(REL_MAX_DIST / max_exact) * (nb - max_exact)).astype(jnp.int32)
    large = jnp.minimum(large, nb - 1)
    return ret + jnp.where(n < max_exact, n, large)


def _window_bias(rel_bias):
    span = 3 * C_BLOCK
    rel = jnp.arange(span)[None, :] - C_BLOCK - jnp.arange(C_BLOCK)[:, None]
    onehot = (_t5_bucket(rel)[..., None] == jnp.arange(REL_BUCKETS)).astype(F32)
    bias = jnp.einsum("qkb,bh->hqk", onehot, rel_bias.astype(F32), precision=lax.Precision.HIGHEST)
    bias = jnp.where((jnp.abs(rel) <= C_WINDOW)[None], bias, NEG)
    return jnp.transpose(bias.reshape(C_HEADS, C_BLOCK, 3, C_BLOCK), (0, 2, 1, 3))


def _mla(cq, ckv, kr, gq, gkv, wuq, wukv, cos, sin):
    s = cq.shape[0]
    q = linear(rmsnorm(cq, gq, ACT_DTYPE, "rms_cq"), wuq, name="a_wuq").reshape(s, A_HEADS, A_NOPE + A_ROPE)
    q = jnp.concatenate([q[..., :A_NOPE], _rope(q[..., A_NOPE:], cos, sin)], axis=-1)
    kv = linear(rmsnorm(ckv, gkv, ACT_DTYPE, "rms_ckv"), wukv, name="a_wukv").reshape(s, A_HEADS, A_NOPE + A_V)
    k_rope = jnp.broadcast_to(_rope(kr[:, None, :], cos, sin), (s, A_HEADS, A_ROPE))
    k = jnp.concatenate([kv[..., :A_NOPE], k_rope], axis=-1)
    v = kv[..., A_NOPE:]
    return mla_attention(q, k, v, (A_NOPE + A_ROPE) ** -0.5)


def _hgrn2(q, f_fwd, f_bwd, i, g, lb_fwd, lb_bwd, g_out):
    s = q.shape[0]
    n_hp = B_HEADS // HG_GROUP
    pairs = lambda t: jnp.transpose(t.reshape(s, n_hp, -1), (1, 0, 2))
    qp, vp = pairs(q), pairs(i)
    o = None
    for z, lb, rev, tag in ((f_fwd, lb_fwd, False, "hgf"), (f_bwd, lb_bwd, True, "hgb")):
        qd, ki, ke, dec = hgrn_prep(qp, pairs(z), lb.astype(F32).reshape(n_hp, 1, -1), rev, tag + "_prep")
        part = hgrn_intra(qd, ki, vp, rev, tag + "_intra") + hgrn_inter(qd, ke, vp, dec, rev, tag + "_inter")
        o = part if o is None else o + part
    return hgrn_out(jnp.transpose(o, (1, 0, 2)).reshape(s, B_HEADS * B_DV), g, g_out)


def _cross(x, h, mem_n, wq, wkv, wo):
    q = linear(h, wq, name="x_wq").reshape(h.shape[0], X_HEADS, X_DH)
    kv = linear(mem_n, wkv, name="x_wkv").reshape(mem_n.shape[0], 2, X_HEADS, X_DH)
    o_t = mla_attention(q, kv[:, 0], kv[:, 1], X_DH ** -0.5, name="cross")
    return linear_res(x, o_t, wo, name="x_wo", a_transposed=True)


def _pad_w_in(w):
    cut = A_Q_RANK + A_KV_RANK + A_ROPE
    return jnp.concatenate([w[:, :cut], jnp.zeros((w.shape[0], KR_PAD), w.dtype), w[:, cut:]], axis=1)


def _model_loss(p, x, mem, target):
    s = x.shape[0]
    cos, sin = _rope_tables(s)
    sm = jax.nn.softmax(p["b_lb"].astype(F32), axis=1)
    lower_bounds = jnp.cumsum(sm, axis=1) - sm[:, :1]
    bias = _window_bias(p["rel_bias"])
    for l in range(DEPTH):
        h, x = rmsnorm_keep(x, p["g_mix"][l], ACT_DTYPE, "rms_mix")
        z = linear(h, _pad_w_in(p["w_in"][l]), name="w_in")
        parts, start = [], 0
        for width in IN_SPLITS_PADDED:
            parts.append(z[:, start:start + width])
            start += width
        a_cq, a_ckv, a_kr, b_q, b_ff, b_fb, b_i, b_g, c_q, c_k, c_v, gate_a, gate_b, gate_c = parts
        y_a = _mla(a_cq, a_ckv, a_kr[:, :A_ROPE], p["a_gq"][l], p["a_gkv"][l], p["a_wuq"][l], p["a_wukv"][l], cos, sin)
        y_b = _hgrn2(b_q, b_ff, b_fb, b_i, b_g, lower_bounds[0, l], lower_bounds[1, l], p["b_gout"][l])
        y_c = _unheads(window_attention(_heads(c_q, C_HEADS, C_DH, F32), _heads(c_k, C_KV_HEADS, C_DH, F32),
                                        _heads(c_v, C_KV_HEADS, C_DH, F32), bias, p["c_sink"][l]))
        x = gated_merge_out(x, y_a, y_b, y_c, gate_a, gate_b, gate_c, p["w_br_a"][l], p["w_br_b"][l], p["w_br_c"][l],
                            p["w_out"][l])
        h, x = rmsnorm_keep(x, p["g_x"][l], ACT_DTYPE, "rms_x")
        x = _cross(x, h, rmsnorm(mem, p["g_mem"][l], ACT_DTYPE, "rms_mem"), p["x_wq"][l], p["x_wkv"][l], p["x_wo"][l])
        h, x = rmsnorm_keep(x, p["g_ffn"][l], ACT_DTYPE, "rms_ffn")
        x = swiglu_ffn(x, h, p["f_w1"][l], p["f_w3"][l], p["f_w2"][l])
    y = rmsnorm(x, p["g_final"], F32, "rms_final")
    return loss_head(y, target)


def kernel(x, mem, w_in, g_mix, a_gq, a_gkv, a_wuq, a_wukv, b_lb, b_gout, c_sink, rel_bias, w_br_a, w_br_b, w_br_c, w_out, g_x, g_mem, x_wq, x_wkv, x_wo, g_ffn, f_w1, f_w3, f_w2, g_final, loss_target, m_w_in, m_g_mix, m_a_gq, m_a_gkv, m_a_wuq, m_a_wukv, m_b_lb, m_b_gout, m_c_sink, m_rel_bias, m_w_br_a, m_w_br_b, m_w_br_c, m_w_out, m_g_x, m_g_mem, m_x_wq, m_x_wkv, m_x_wo, m_g_ffn, m_f_w1, m_f_w3, m_f_w2, m_g_final, v_w_in, v_g_mix, v_a_gq, v_a_gkv, v_a_wuq, v_a_wukv, v_b_lb, v_b_gout, v_c_sink, v_rel_bias, v_w_br_a, v_w_br_b, v_w_br_c, v_w_out, v_g_x, v_g_mem, v_x_wq, v_x_wkv, v_x_wo, v_g_ffn, v_f_w1, v_f_w3, v_f_w2, v_g_final):
    given = dict(locals())
    w = {n: given[n] for n in WEIGHT_ORDER}
    m = {n: given["m_" + n] for n in WEIGHT_ORDER}
    v = {n: given["v_" + n] for n in WEIGHT_ORDER}
    sh_names = [n for n, _ in SHARDED]
    rep_shapes = [w[n].shape for n in REPLICATED] + [(1,)]
    rep_rows = _pack_rows([int(np.prod(s)) for s in rep_shapes])

    wire = [w[n] if n in ELEMENTWISE_SHARDED else w[n].astype(MXU_DTYPE) for n in sh_names]
    gathered = all_gather_shards(wire)
    full = {n: _shards_to_full(t, ax).astype(F32) for (n, ax), t in zip(SHARDED, gathered)}
    full.update({n: w[n] for n in REPLICATED})

    loss, (grad_full, grad_x) = jax.value_and_grad(_model_loss, argnums=(0, 1))(full, x[0], mem[0], loss_target[0])

    received = all_to_all_blocks([_full_to_shards(grad_full[n], ax).astype(GRAD_WIRE_DTYPE) for n, ax in SHARDED])
    g_sh, d_sh, nm_sh, nv_sh = {}, {}, {}, {}
    for n, got in zip(sh_names, received):
        shp = w[n].shape
        rows = lambda t: t.reshape(-1, shp[-1])
        outs = adamw_rows(got.reshape(N_DEV, -1, shp[-1]), rows(w[n]), rows(m[n]), rows(v[n]), "adamw_" + n)
        g_sh[n], d_sh[n], nm_sh[n], nv_sh[n] = [o.reshape(shp) for o in outs]

    mine = _pack([grad_full[n] for n in REPLICATED] + [loss.reshape(1)], rep_rows)
    everyone = all_gather_small(mine)
    rep_w = [w[n] for n in REPLICATED] + [jnp.zeros((1,), F32)]
    outs = adamw_rows(everyone, _pack(rep_w, rep_rows), _pack([m[n] for n in REPLICATED] + [jnp.zeros((1,), F32)], rep_rows),
                      _pack([v[n] for n in REPLICATED] + [jnp.ones((1,), F32)], rep_rows), "adamw_replicated")
    rep_names = list(REPLICATED) + ["loss"]
    g_rp, d_rp, nm_rp, nv_rp = [dict(zip(rep_names, _unpack(o, rep_shapes))) for o in outs]

    def pick(sharded, replicated, n):
        return sharded[n] if n in sharded else replicated[n]

    return (g_rp["loss"].reshape(()), grad_x[None],
            *[pick(g_sh, g_rp, n) for n in WEIGHT_ORDER], *[pick(d_sh, d_rp, n) for n in WEIGHT_ORDER],
            *[pick(nm_sh, nm_rp, n) for n in WEIGHT_ORDER], *[pick(nv_sh, nv_rp, n) for n in WEIGHT_ORDER])
```

```python
import functools
import math

import jax
import jax.numpy as jnp
import numpy as np
from jax import lax
from jax.experimental import pallas as pl
from jax.experimental.pallas import tpu as pltpu

F32 = jnp.float32
MXU_DTYPE = jnp.bfloat16
ACT_DTYPE = jnp.bfloat16
GRAD_WIRE_DTYPE = jnp.bfloat16

V7X_VMEM_LIMIT_BYTES = 56 * 1024 * 1024
LANES = 128
SUBLANES = 8

N_DEV = 8
D_MODEL = 1024
DEPTH = 2
EPS = 1e-6
TINY = 1e-30
NEG = -1e30

A_HEADS, A_NOPE, A_ROPE, A_V, A_Q_RANK, A_KV_RANK = 8, 64, 32, 64, 384, 256
ROPE_THETA = 10000.0
B_HEADS, B_DK, B_DV, B_CHUNK = 8, 128, 64, 16
C_HEADS, C_KV_HEADS, C_DH, C_WINDOW, C_BLOCK = 8, 2, 64, 128, 128
REL_BUCKETS, REL_MAX_DIST = 32, 128
X_HEADS, X_DH = 4, 256
D_FF = 2816
IN_SPLITS = (A_Q_RANK, A_KV_RANK, A_ROPE, 1024, 1024, 1024, 512, 512, 512, 128, 128, 1024, 1024, 1024)
IN_WIDTH = sum(IN_SPLITS)
KR_PAD = LANES - A_ROPE
IN_SPLITS_PADDED = (A_Q_RANK, A_KV_RANK, LANES, 1024, 1024, 1024, 512, 512, 512, 128, 128, 1024, 1024, 1024)

ADAM_LR, ADAM_B1, ADAM_B2, ADAM_EPS, ADAM_WD, ADAM_STEP = 0.001, 0.9, 0.999, 1e-08, 0.01, 10

SHARDED = (("w_in", 2), ("a_wuq", 2), ("a_wukv", 2), ("b_lb", 2), ("w_br_a", 2), ("w_br_b", 2), ("w_br_c", 2),
           ("w_out", 1), ("x_wq", 1), ("x_wkv", 2), ("x_wo", 1), ("f_w1", 2), ("f_w3", 2), ("f_w2", 1))
ELEMENTWISE_SHARDED = ("b_lb",)
REPLICATED = ("g_mix", "a_gq", "a_gkv", "b_gout", "c_sink", "rel_bias", "g_x", "g_mem", "g_ffn", "g_final")
WEIGHT_ORDER = ("w_in", "g_mix", "a_gq", "a_gkv", "a_wuq", "a_wukv", "b_lb", "b_gout", "c_sink", "rel_bias", "w_br_a",
                "w_br_b", "w_br_c", "w_out", "g_x", "g_mem", "x_wq", "x_wkv", "x_wo", "g_ffn", "f_w1", "f_w3", "f_w2",
                "g_final")
PACK_QUANTUM = SUBLANES * LANES
PACK_ROW_TILE = 512


def _pallas(body, **kw):
    return pl.pallas_call(body, **kw)


def _cparams(*sem):
    return pltpu.CompilerParams(dimension_semantics=sem, vmem_limit_bytes=V7X_VMEM_LIMIT_BYTES)


def _tile(n, target, mult=LANES):
    t = (min(target, n) // mult) * mult
    while t >= mult:
        if n % t == 0:
            return t
        t -= mult
    return n


def _dot(a, b, dims):
    return lax.dot_general(a.astype(MXU_DTYPE), b.astype(MXU_DTYPE), (dims, ((), ())), preferred_element_type=F32)


NN = ((1,), (0,))
NT = ((1,), (1,))
TN = ((0,), (0,))


MM_VMEM_BUDGET_BYTES = 40 * 1024 * 1024
MM_MAX_TILE = 4352
MM_MAX_ROW_TILE = 2048
MM_HBM_BYTES_PER_S = 2.5e12
MM_STEP_S = 0.4e-6
MM_DMA_ROW_OVERHEAD_BYTES = 512.0


def _tile_options(n, cap):
    out = [t for t in range(LANES, min(n, cap) + 1, LANES) if n % t == 0]
    if n <= cap and n not in out:
        out.append(n)
    return out or [n]


@functools.lru_cache(maxsize=None)
def _mm_plan(m, n, k, ta, tb, a_bytes, b_bytes, o_bytes):
    best = None
    for tk in _tile_options(k, MM_MAX_TILE):
        nk = k // tk
        for tn in _tile_options(n, MM_MAX_TILE):
            for tm in _tile_options(m, MM_MAX_ROW_TILE):
                vmem = 2 * (tm * tk * a_bytes + tk * tn * b_bytes + tm * tn * o_bytes) + tm * tn * 4
                vmem += (tm * tk * 2 if a_bytes == 4 else 0) + (tk * tn * 2 if b_bytes == 4 else 0)
                if vmem > MM_VMEM_BUDGET_BYTES:
                    continue

                def eff(elems, nbytes):
                    return (elems * nbytes) / (elems * nbytes + MM_DMA_ROW_OVERHEAD_BYTES)

                ea, eb, eo = eff(tm if ta else tk, a_bytes), eff(tk if tb else tn, b_bytes), eff(tn, o_bytes)
                for order in ("mn", "nm"):
                    if nk == 1 and order == "nm":
                        a_tr, b_tr = m * k * a_bytes * (n // tn), k * n * b_bytes
                    elif nk == 1:
                        a_tr, b_tr = m * k * a_bytes, k * n * b_bytes * (m // tm)
                    else:
                        a_tr, b_tr = m * k * a_bytes * (n // tn), k * n * b_bytes * (m // tm)
                    steps = (m // tm) * (n // tn) * nk
                    cost = (a_tr / ea + b_tr / eb + m * n * o_bytes / eo) / MM_HBM_BYTES_PER_S + steps * MM_STEP_S
                    if best is None or cost < best[0]:
                        best = (cost, tm, tn, tk, order)
    assert best is not None, (m, n, k)
    return best[1:]


def _mm(a, b, ta=False, tb=False, out_dtype=F32, name="mm", res=None):
    m, k = (a.shape[1], a.shape[0]) if ta else a.shape
    kb, n = (b.shape[1], b.shape[0]) if tb else b.shape
    assert k == kb, (a.shape, b.shape, ta, tb)
    tm, tn, tk, order = _mm_plan(m, n, k, ta, tb, a.dtype.itemsize, b.dtype.itemsize, jnp.dtype(out_dtype).itemsize)
    nk = k // tk
    dims = ((0 if ta else 1,), (1 if tb else 0,))
    out_shape = jax.ShapeDtypeStruct((m, n), out_dtype)

    assert res is None or nk == 1, (name, k, tk)
    if nk == 1:
        def body(a_ref, b_ref, *rest):
            acc = _dot(a_ref[...], b_ref[...], dims)
            if res is not None:
                acc = rest[0][...] + acc
            rest[-1][...] = acc.astype(rest[-1].dtype)

        if order == "nm":
            mi, ni = (lambda j, i: i), (lambda j, i: j)
            grid = (n // tn, m // tm)
        else:
            mi, ni = (lambda i, j: i), (lambda i, j: j)
            grid = (m // tm, n // tn)
        a_spec = pl.BlockSpec((tk, tm), lambda p, q: (0, mi(p, q))) if ta else pl.BlockSpec((tm, tk), lambda p, q: (mi(p, q), 0))
        b_spec = pl.BlockSpec((tn, tk), lambda p, q: (ni(p, q), 0)) if tb else pl.BlockSpec((tk, tn), lambda p, q: (0, ni(p, q)))
        o_spec = pl.BlockSpec((tm, tn), lambda p, q: (mi(p, q), ni(p, q)))
        extra = [] if res is None else [res]
        return _pallas(body, name=name, grid=grid, in_specs=[a_spec, b_spec] + [o_spec] * len(extra), out_specs=o_spec,
                       out_shape=out_shape, compiler_params=_cparams("parallel", "parallel"))(a, b, *extra)

    direct = jnp.dtype(out_dtype) == jnp.dtype(F32)

    def body(a_ref, b_ref, o_ref, *scratch):
        acc_ref = o_ref if direct else scratch[0]
        kk = pl.program_id(2)

        @pl.when(kk == 0)
        def _():
            acc_ref[...] = jnp.zeros_like(acc_ref)

        acc_ref[...] += _dot(a_ref[...], b_ref[...], dims)

        if not direct:
            @pl.when(kk == nk - 1)
            def _():
                o_ref[...] = acc_ref[...].astype(o_ref.dtype)

    a_spec = pl.BlockSpec((tk, tm), lambda i, j, kk: (kk, i)) if ta else pl.BlockSpec((tm, tk), lambda i, j, kk: (i, kk))
    b_spec = pl.BlockSpec((tn, tk), lambda i, j, kk: (j, kk)) if tb else pl.BlockSpec((tk, tn), lambda i, j, kk: (kk, j))
    return _pallas(
        body, name=name, grid=(m // tm, n // tn, nk), in_specs=[a_spec, b_spec],
        out_specs=pl.BlockSpec((tm, tn), lambda i, j, kk: (i, j)), out_shape=out_shape,
        scratch_shapes=[] if direct else [pltpu.VMEM((tm, tn), F32)],
        compiler_params=_cparams("parallel", "parallel", "arbitrary"),
    )(a, b)


def linear(a, w, out_dtype=F32, name="lin"):
    @jax.custom_vjp
    def f(a, w):
        return _mm(a.astype(ACT_DTYPE), w.astype(MXU_DTYPE), out_dtype=out_dtype, name=name + "_fwd")

    def fwd(a, w):
        ab, wb = a.astype(ACT_DTYPE), w.astype(MXU_DTYPE)
        return _mm(ab, wb, out_dtype=out_dtype, name=name + "_fwd"), (ab, wb, jnp.zeros((0,), a.dtype))

    def bwd(res, g):
        ab, wb, like_a = res
        gb = g.astype(ACT_DTYPE)
        da = _mm(gb, wb, tb=True, out_dtype=like_a.dtype, name=name + "_dx")
        dw = _mm(ab, gb, ta=True, out_dtype=F32, name=name + "_dw")
        return da, dw

    f.defvjp(fwd, bwd)
    return f(a, w)


def linear_res(x, a, w, name="lin", a_transposed=False):
    @jax.custom_vjp
    def f(x, a, w):
        return _mm(a.astype(ACT_DTYPE), w.astype(MXU_DTYPE), ta=a_transposed, name=name + "_fwd", res=x)

    def fwd(x, a, w):
        ab, wb = a.astype(ACT_DTYPE), w.astype(MXU_DTYPE)
        return _mm(ab, wb, ta=a_transposed, name=name + "_fwd", res=x), (ab, wb, jnp.zeros((0,), a.dtype))

    def bwd(res, g):
        ab, wb, like_a = res
        gb = g.astype(ACT_DTYPE)
        if a_transposed:
            da = _mm(wb, gb, tb=True, out_dtype=like_a.dtype, name=name + "_dx")
        else:
            da = _mm(gb, wb, tb=True, out_dtype=like_a.dtype, name=name + "_dx")
        dw = _mm(ab, gb, ta=not a_transposed, out_dtype=F32, name=name + "_dw")
        return g, da, dw

    f.defvjp(fwd, bwd)
    return f(x, a, w)


FFN_ROW_TILE = 512
FFN_COL_TILE = 1408


def _sigmoid(a):
    return 1.0 / (1.0 + jnp.exp(-a))


def swiglu_ffn(x, h, w1, w3, w2, name="ffn"):
    m, d = h.shape
    f_dim = w1.shape[1]
    tm, tn = _tile(m, FFN_ROW_TILE), _tile(f_dim, FFN_COL_TILE)

    def up_body(h_ref, w1_ref, w3_ref, t_ref, a_ref, b_ref):
        hv = h_ref[...]
        a = _dot(hv, w1_ref[...], NN)
        b = _dot(hv, w3_ref[...], NN)
        a_ref[...] = a.astype(a_ref.dtype)
        b_ref[...] = b.astype(b_ref.dtype)
        t_ref[...] = (a * _sigmoid(a) * b).astype(t_ref.dtype)

    def dt_body(g_ref, w2_ref, a_ref, b_ref, da_ref, db_ref):
        dt = _dot(g_ref[...], w2_ref[...], NT)
        a, b = a_ref[...].astype(F32), b_ref[...].astype(F32)
        sg = _sigmoid(a)
        da_ref[...] = (dt * b * (sg * (1.0 + a * (1.0 - sg)))).astype(da_ref.dtype)
        db_ref[...] = (dt * (a * sg)).astype(db_ref.dtype)

    row = pl.BlockSpec((tm, d), lambda j, i: (i, 0))
    w_up = pl.BlockSpec((d, tn), lambda j, i: (0, j))
    w_dn = pl.BlockSpec((tn, d), lambda j, i: (j, 0))
    tile = pl.BlockSpec((tm, tn), lambda j, i: (i, j))
    grid = (f_dim // tn, m // tm)

    def run_up(hb, w1b, w3b):
        return _pallas(up_body, name=name + "_up", grid=grid, in_specs=[row, w_up, w_up], out_specs=[tile, tile, tile],
                       out_shape=[jax.ShapeDtypeStruct((m, f_dim), ACT_DTYPE)] * 3,
                       compiler_params=_cparams("parallel", "parallel"))(hb, w1b, w3b)

    def forward(x, h, w1, w3, w2):
        hb = h.astype(ACT_DTYPE)
        w1b, w3b, w2b = w1.astype(MXU_DTYPE), w3.astype(MXU_DTYPE), w2.astype(MXU_DTYPE)
        t, a, b = run_up(hb, w1b, w3b)
        return _mm(t, w2b, name=name + "_down", res=x), (hb, w1b, w3b, w2b, t, a, b, jnp.zeros((0,), h.dtype))

    @jax.custom_vjp
    def f(x, h, w1, w3, w2):
        return forward(x, h, w1, w3, w2)[0]

    def bwd(res, g):
        hb, w1b, w3b, w2b, t, a, b, like_h = res
        gb = g.astype(ACT_DTYPE)
        da, db = _pallas(dt_body, name=name + "_dt", grid=grid, in_specs=[row, w_dn, tile, tile], out_specs=[tile, tile],
                         out_shape=[jax.ShapeDtypeStruct((m, f_dim), ACT_DTYPE)] * 2,
                         compiler_params=_cparams("parallel", "parallel"))(gb, w2b, a, b)
        dw2 = _mm(t, gb, ta=True, name=name + "_dw2")
        def dh_body(da_ref, db_ref, w1_ref, w3_ref, o_ref):
            o_ref[...] = (_dot(da_ref[...], w1_ref[...], NT) + _dot(db_ref[...], w3_ref[...], NT)).astype(o_ref.dtype)

        wide = pl.BlockSpec((tm, f_dim), lambda i: (i, 0))
        w_all = pl.BlockSpec((d, f_dim), lambda i: (0, 0))
        dh = _pallas(dh_body, name=name + "_dx", grid=(m // tm,), in_specs=[wide, wide, w_all, w_all],
                     out_specs=pl.BlockSpec((tm, d), lambda i: (i, 0)), out_shape=jax.ShapeDtypeStruct((m, d), like_h.dtype),
                     compiler_params=_cparams("parallel"))(da, db, w1b, w3b)
        dw1 = _mm(hb, da, ta=True, name=name + "_dw1")
        dw3 = _mm(hb, db, ta=True, name=name + "_dw3")
        return g, dh, dw1, dw3, dw2

    f.defvjp(lambda *args: forward(*args), bwd)
    return f(x, h, w1, w3, w2)


MERGE_ROW_TILE = 256


def gated_merge_out(x, ya_t, yb, yc, ga, gb, gc, wa, wb, wc, wo, name="merge"):
    s, d = x.shape
    e = yb.shape[1]
    tm = _tile(s, MERGE_ROW_TILE)

    def branches(ya_ref, yb_ref, yc_ref, wa_ref, wb_ref, wc_ref):
        return (_dot(ya_ref[...], wa_ref[...], TN), _dot(yb_ref[...], wb_ref[...], NN), _dot(yc_ref[...], wc_ref[...], NN))

    def fwd_body(x_ref, ya_ref, yb_ref, yc_ref, ga_ref, gb_ref, gc_ref, wa_ref, wb_ref, wc_ref, wo_ref, o_ref, m_ref):
        pa, pb, pc = branches(ya_ref, yb_ref, yc_ref, wa_ref, wb_ref, wc_ref)
        merged = _sigmoid(ga_ref[...]) * pa + _sigmoid(gb_ref[...]) * pb + _sigmoid(gc_ref[...]) * pc
        mb = merged.astype(m_ref.dtype)
        m_ref[...] = mb
        o_ref[...] = x_ref[...] + _dot(mb, wo_ref[...], NN)

    def bwd_body(g_ref, ya_ref, yb_ref, yc_ref, ga_ref, gb_ref, gc_ref, wa_ref, wb_ref, wc_ref, wo_ref,
                 dga_ref, dgb_ref, dgc_ref, dpa_ref, dpb_ref, dpc_ref):
        dm = _dot(g_ref[...], wo_ref[...], NT)
        ps = branches(ya_ref, yb_ref, yc_ref, wa_ref, wb_ref, wc_ref)
        for p_i, gate_ref, dg_ref, dp_ref in zip(ps, (ga_ref, gb_ref, gc_ref), (dga_ref, dgb_ref, dgc_ref),
                                                 (dpa_ref, dpb_ref, dpc_ref)):
            sg = _sigmoid(gate_ref[...])
            dg_ref[...] = dm * p_i * (sg * (1.0 - sg))
            dp_ref[...] = (dm * sg).astype(dp_ref.dtype)

    rows = lambda width: pl.BlockSpec((tm, width), lambda i: (i, 0))
    cols_t = pl.BlockSpec((e, tm), lambda i: (0, i))
    whole = lambda r, c: pl.BlockSpec((r, c), lambda i: (0, 0))
    in_common = [cols_t, rows(e), rows(e), rows(d), rows(d), rows(d), whole(e, d), whole(e, d), whole(e, d), whole(d, d)]

    def forward(x, ya_t, yb, yc, ga, gb, gc, wa, wb, wc, wo):
        cast = lambda t: t.astype(ACT_DTYPE)
        ops = (cast(ya_t), cast(yb), cast(yc), ga, gb, gc, cast(wa), cast(wb), cast(wc), cast(wo))
        out, merged = _pallas(
            fwd_body, name=name + "_fwd", grid=(s // tm,), in_specs=[rows(d)] + in_common, out_specs=[rows(d), rows(d)],
            out_shape=[jax.ShapeDtypeStruct((s, d), F32), jax.ShapeDtypeStruct((s, d), ACT_DTYPE)],
            compiler_params=_cparams("parallel"))(x, *ops)
        like = tuple(jnp.zeros((0,), t.dtype) for t in (ya_t, yb, yc))
        return out, (ops, merged, like)

    @jax.custom_vjp
    def f(*args):
        return forward(*args)[0]

    def bwd(res, g):
        ops, merged, like = res
        ya_b, yb_b, yc_b, ga, gb, gc, wa_b, wb_b, wc_b, wo_b = ops
        gbf = g.astype(ACT_DTYPE)
        gate_ct = jax.ShapeDtypeStruct((s, d), F32)
        branch_ct = jax.ShapeDtypeStruct((s, d), ACT_DTYPE)
        dga, dgb, dgc, dpa, dpb, dpc = _pallas(
            bwd_body, name=name + "_bwd", grid=(s // tm,), in_specs=[rows(d)] + in_common, out_specs=[rows(d)] * 6,
            out_shape=[gate_ct] * 3 + [branch_ct] * 3, compiler_params=_cparams("parallel"))(gbf, *ops)
        dya_t = _mm(wa_b, dpa, tb=True, out_dtype=like[0].dtype, name=name + "_dya")
        dyb = _mm(dpb, wb_b, tb=True, out_dtype=like[1].dtype, name=name + "_dyb")
        dyc = _mm(dpc, wc_b, tb=True, out_dtype=like[2].dtype, name=name + "_dyc")
        dwa = _mm(ya_b, dpa, name=name + "_dwa")
        dwb = _mm(yb_b, dpb, ta=True, name=name + "_dwb")
        dwc = _mm(yc_b, dpc, ta=True, name=name + "_dwc")
        dwo = _mm(merged, gbf, ta=True, name=name + "_dwo")
        return g, dya_t, dyb, dyc, dga, dgb, dgc, dwa, dwb, dwc, dwo

    f.defvjp(lambda *args: forward(*args), bwd)
    return f(x, ya_t, yb, yc, ga, gb, gc, wa, wb, wc, wo)


def _row_tile(rows, width):
    return _tile(rows, max(SUBLANES, (512 * 1024) // width), 16)


def rmsnorm(x, g, out_dtype=F32, name="rms"):
    rows, d = x.shape
    tr = _row_tile(rows, d)
    n_steps = rows // tr

    def fwd_body(x_ref, g_ref, o_ref):
        xv = x_ref[...].astype(F32)
        r = lax.rsqrt(jnp.mean(xv * xv, axis=-1, keepdims=True) + EPS)
        o_ref[...] = (xv * r * g_ref[...]).astype(o_ref.dtype)

    def bwd_body(x_ref, g_ref, dy_ref, dx_ref, dg_ref):
        xv = x_ref[...].astype(F32)
        dy = dy_ref[...].astype(F32)
        r = lax.rsqrt(jnp.mean(xv * xv, axis=-1, keepdims=True) + EPS)
        xh = xv * r
        dxh = dy * g_ref[...]
        dx_ref[...] = (r * (dxh - xh * jnp.mean(dxh * xh, axis=-1, keepdims=True))).astype(dx_ref.dtype)

        @pl.when(pl.program_id(0) == 0)
        def _():
            dg_ref[...] = jnp.zeros_like(dg_ref)

        dg_ref[...] += jnp.sum(dy * xh, axis=0, keepdims=True)

    row_spec = pl.BlockSpec((tr, d), lambda i: (i, 0))
    vec_spec = pl.BlockSpec((1, d), lambda i: (0, 0))

    def run_fwd(x, g):
        return _pallas(fwd_body, name=name + "_fwd", grid=(n_steps,), in_specs=[row_spec, vec_spec], out_specs=row_spec,
                       out_shape=jax.ShapeDtypeStruct((rows, d), out_dtype), compiler_params=_cparams("parallel"))(
            x, g.reshape(1, d).astype(F32))

    @jax.custom_vjp
    def f(x, g):
        return run_fwd(x, g)

    def fwd(x, g):
        return run_fwd(x, g), (x, g)

    def bwd(res, dy):
        x, g = res
        dx, dg = _pallas(
            bwd_body, name=name + "_bwd", grid=(n_steps,), in_specs=[row_spec, vec_spec, row_spec],
            out_specs=[row_spec, vec_spec],
            out_shape=[jax.ShapeDtypeStruct((rows, d), x.dtype), jax.ShapeDtypeStruct((1, d), F32)],
            compiler_params=_cparams("arbitrary"))(x, g.reshape(1, d).astype(F32), dy)
        return dx, dg.reshape(g.shape).astype(g.dtype)

    f.defvjp(fwd, bwd)
    return f(x, g)


def rmsnorm_keep(x, g, out_dtype=F32, name="rms"):
    rows, d = x.shape
    tr = _row_tile(rows, d)
    n_steps = rows // tr

    def bwd_body(x_ref, g_ref, dy_ref, dkeep_ref, dx_ref, dg_ref):
        xv = x_ref[...]
        dy = dy_ref[...].astype(F32)
        r = lax.rsqrt(jnp.mean(xv * xv, axis=-1, keepdims=True) + EPS)
        xh = xv * r
        dxh = dy * g_ref[...]
        dx_ref[...] = dkeep_ref[...] + r * (dxh - xh * jnp.mean(dxh * xh, axis=-1, keepdims=True))

        @pl.when(pl.program_id(0) == 0)
        def _():
            dg_ref[...] = jnp.zeros_like(dg_ref)

        dg_ref[...] += jnp.sum(dy * xh, axis=0, keepdims=True)

    row_spec = pl.BlockSpec((tr, d), lambda i: (i, 0))
    vec_spec = pl.BlockSpec((1, d), lambda i: (0, 0))

    @jax.custom_vjp
    def f(x, g):
        return rmsnorm(x, g, out_dtype, name), x

    def fwd(x, g):
        return (rmsnorm(x, g, out_dtype, name), x), (x, g)

    def bwd(res, cts):
        x, g = res
        dy, dkeep = cts
        dx, dg = _pallas(
            bwd_body, name=name + "_bwd", grid=(n_steps,), in_specs=[row_spec, vec_spec, row_spec, row_spec],
            out_specs=[row_spec, vec_spec],
            out_shape=[jax.ShapeDtypeStruct((rows, d), F32), jax.ShapeDtypeStruct((1, d), F32)],
            compiler_params=_cparams("arbitrary"))(x, g.reshape(1, d).astype(F32), dy, dkeep)
        return dx, dg.reshape(g.shape).astype(g.dtype)

    f.defvjp(fwd, bwd)
    return f(x, g)


def _rowdot(a, b, name):
    h, s, d = a.shape
    ts = _tile(s, 2048)

    def body(a_ref, b_ref, o_ref):
        o_ref[...] = jnp.sum(a_ref[...].astype(F32) * b_ref[...].astype(F32), axis=-1, keepdims=True)

    spec = pl.BlockSpec((None, ts, d), lambda hh, i: (hh, i, 0))
    return _pallas(body, name=name, grid=(h, s // ts), in_specs=[spec, spec],
                   out_specs=pl.BlockSpec((None, ts, 1), lambda hh, i: (hh, i, 0)),
                   out_shape=jax.ShapeDtypeStruct((h, s, 1), F32), compiler_params=_cparams("parallel", "parallel"))(a, b)


LOG2E = 1.4426950408889634


def mla_attention(q, k, v, scale, name="mla"):
    s, h, d = q.shape
    sk, dv = k.shape[0], v.shape[2]
    tq, tk = _tile(s, MLA_BWD_TQ), _tile(sk, MLA_TK)
    nq, nk = s // tq, sk // tk
    tqf, tkf = _tile(s, MLA_FWD_TQ), _tile(sk, MLA_FWD_TK)
    nkf = sk // tkf
    ones_rows = 16
    c = scale * LOG2E

    def fwd_body(qt_ref, k_ref, vt_ref, ot_ref, lse_ref, m_ref, acc_ref):
        j = pl.program_id(2)

        @pl.when(j == 0)
        def _():
            m_ref[...] = jnp.full_like(m_ref, NEG)
            acc_ref[...] = jnp.zeros_like(acc_ref)

        st = _dot(k_ref[...], qt_ref[...], NN)
        m_prev = m_ref[...]
        m_new = jnp.maximum(m_prev, jnp.max(st, axis=0, keepdims=True) * c)
        pt = jnp.exp2(st * c - m_new)
        acc_ref[...] = jnp.exp2(m_prev - m_new) * acc_ref[...] + _dot(vt_ref[...], pt, NN)
        m_ref[...] = m_new

        @pl.when(j == nkf - 1)
        def _():
            l = acc_ref[dv:dv + 1, :]
            ot_ref[...] = (acc_ref[:dv, :] / l).astype(ot_ref.dtype)
            lse_ref[...] = m_ref[...] + jnp.log2(l)

    def delta_body(ot_ref, dot_ref, o_ref):
        o_ref[...] = jnp.sum(ot_ref[...].astype(F32) * dot_ref[...].astype(F32), axis=0, keepdims=True)

    def bwd_body(qt_ref, k_ref, kt_ref, v_ref, dot_ref, lse_ref, dl_ref, dqt_ref, dk_hbm, dv_hbm, dq_acc, dk_acc, dv_acc):
        hh, i, j = pl.program_id(0), pl.program_id(1), pl.program_id(2)

        @pl.when(j == 0)
        def _():
            dq_acc[...] = jnp.zeros_like(dq_acc)

        @pl.when(i == 0)
        def _():
            dk_acc[j] = jnp.zeros((d, tk), F32)
            dv_acc[j] = jnp.zeros((dv, tk), F32)

        qt, dot_ = qt_ref[...], dot_ref[...]
        pt = jnp.exp2(_dot(k_ref[...], qt, NN) * c - lse_ref[...])
        dst = (pt * (_dot(v_ref[...], dot_, NN) - dl_ref[...])).astype(MXU_DTYPE)
        dv_acc[j] += _dot(dot_, pt, NT)
        dk_acc[j] += _dot(qt, dst, NT)
        dq_acc[...] += _dot(kt_ref[...], dst, NN)

        @pl.when(j == nk - 1)
        def _():
            dqt_ref[...] = dq_acc[...] * scale

        @pl.when(i == nq - 1)
        def _():
            dk_acc[j] = dk_acc[j] * scale
            pltpu.sync_copy(dk_acc.at[j], dk_hbm.at[hh, j])
            pltpu.sync_copy(dv_acc.at[j], dv_hbm.at[hh, j])

    def qt_spec(width):
        return pl.BlockSpec((None, width, tq), lambda hh, i, j: (hh, 0, i))

    def kt_spec(width):
        return pl.BlockSpec((None, width, tk), lambda hh, i, j: (hh, 0, j))

    def k_spec(width):
        return pl.BlockSpec((None, tk, width), lambda hh, i, j: (hh, j, 0))

    def layouts(q, k, v):
        cast = lambda t: t.astype(ACT_DTYPE)
        return (cast(jnp.transpose(q, (1, 2, 0))), cast(jnp.transpose(k, (1, 0, 2))), cast(jnp.transpose(k, (1, 2, 0))),
                cast(jnp.transpose(v, (1, 0, 2))), cast(jnp.transpose(v, (1, 2, 0))))

    def run_fwd(qt, kh, vt):
        vt_ones = jnp.concatenate([vt, jnp.ones((h, ones_rows, sk), vt.dtype)], axis=1)

        def qf_spec(width):
            return pl.BlockSpec((None, width, tqf), lambda hh, i, j: (hh, 0, i))

        kf_spec = pl.BlockSpec((None, tkf, d), lambda hh, i, j: (hh, j, 0))
        vf_spec = pl.BlockSpec((None, dv + ones_rows, tkf), lambda hh, i, j: (hh, 0, j))
        return _pallas(
            fwd_body, name=name + "_fwd", grid=(h, s // tqf, nkf), in_specs=[qf_spec(d), kf_spec, vf_spec],
            out_specs=[qf_spec(dv), qf_spec(1)],
            out_shape=[jax.ShapeDtypeStruct((h, dv, s), ACT_DTYPE), jax.ShapeDtypeStruct((h, 1, s), F32)],
            scratch_shapes=[pltpu.VMEM((1, tqf), F32), pltpu.VMEM((dv + ones_rows, tqf), F32)],
            compiler_params=_cparams("parallel", "parallel", "arbitrary"))(qt, kh, vt_ones)

    @jax.custom_vjp
    def f(q, k, v):
        qt, kh, _, _, vt = layouts(q, k, v)
        return run_fwd(qt, kh, vt)[0].reshape(h * dv, s)

    def fwd(q, k, v):
        qt, kh, kt, vh, vt = layouts(q, k, v)
        ot, lse = run_fwd(qt, kh, vt)
        return ot.reshape(h * dv, s), (qt, kh, kt, vh, ot, lse)

    def bwd(res, dy):
        qt, kh, kt, vh, ot, lse = res
        dot_ = dy.reshape(h, dv, s)
        ts = _tile(s, 2048)
        col = pl.BlockSpec((None, dv, ts), lambda hh, i: (hh, 0, i))
        delta = _pallas(delta_body, name=name + "_delta", grid=(h, s // ts), in_specs=[col, col],
                        out_specs=pl.BlockSpec((None, 1, ts), lambda hh, i: (hh, 0, i)),
                        out_shape=jax.ShapeDtypeStruct((h, 1, s), F32), compiler_params=_cparams("parallel", "parallel"))(ot, dot_)
        any_spec = pl.BlockSpec(memory_space=pl.ANY)
        dqt, dkt, dvt = _pallas(
            bwd_body, name=name + "_bwd", grid=(h, nq, nk),
            in_specs=[qt_spec(d), k_spec(d), kt_spec(d), k_spec(dv), qt_spec(dv), qt_spec(1), qt_spec(1)],
            out_specs=[qt_spec(d), any_spec, any_spec],
            out_shape=[jax.ShapeDtypeStruct((h, d, s), F32), jax.ShapeDtypeStruct((h, nk, d, tk), F32),
                       jax.ShapeDtypeStruct((h, nk, dv, tk), F32)],
            scratch_shapes=[pltpu.VMEM((d, tq), F32), pltpu.VMEM((nk, d, tk), F32), pltpu.VMEM((nk, dv, tk), F32)],
            compiler_params=_cparams("parallel", "arbitrary", "arbitrary"))(qt, kh, kt, vh, dot_, lse, delta)
        to_tokens = lambda t: jnp.transpose(t, (1, 3, 0, 2)).reshape(sk, h, t.shape[2])
        return jnp.transpose(dqt, (2, 0, 1)), to_tokens(dkt), to_tokens(dvt)

    f.defvjp(fwd, bwd)
    return f(q, k, v)


WATTN_TQ = 2 * C_BLOCK
WATTN_KW = WATTN_TQ + 2 * C_BLOCK


def window_attention(q, k, v, bias, sink, name="wattn"):
    hq, s, dh = q.shape
    g = hq // C_KV_HEADS
    tq, kw, half = WATTN_TQ, WATTN_KW, WATTN_KW // 2
    nt = s // tq
    scale = dh ** -0.5
    sink_b = jnp.broadcast_to(sink.astype(F32).reshape(hq, 1, 1), (hq, 1, LANES))
    neg = jnp.full((hq, C_BLOCK, C_BLOCK), NEG, F32)
    tile = jnp.concatenate(
        [jnp.concatenate([bias[:, cb - rb] if 0 <= cb - rb <= 2 else neg for cb in range(kw // C_BLOCK)], axis=2)
         for rb in range(tq // C_BLOCK)], axis=1)

    def key_bias(i):
        pos = lax.broadcasted_iota(jnp.int32, (1, kw), 1) + i * tq - C_BLOCK
        return jnp.where(jnp.logical_and(pos >= 0, pos < s), 0.0, NEG)

    def both(a_ref, b_ref):
        return jnp.concatenate([a_ref[...], b_ref[...]], axis=0)

    def fwd_body(q_ref, ka_ref, kb_ref, va_ref, vb_ref, b_ref, sk_ref, o_ref, lse_ref):
        kb_ = key_bias(pl.program_id(1))
        k_all, v_all = both(ka_ref, kb_ref), both(va_ref, vb_ref)
        for hh in range(g):
            sc = _dot(q_ref[hh], k_all, NT) * scale + b_ref[hh] + kb_
            snk = sk_ref[hh][:, :1]
            m = jnp.maximum(jnp.max(sc, axis=-1, keepdims=True), snk)
            p = jnp.exp(sc - m)
            l = jnp.sum(p, axis=-1, keepdims=True) + jnp.exp(snk - m)
            o_ref[hh] = (_dot(p, v_all, NN) / l).astype(o_ref.dtype)
            lse_ref[hh] = m + jnp.log(l)

    def bwd_body(q_ref, ka_ref, kb_ref, va_ref, vb_ref, b_ref, sk_ref, do_ref, lse_ref, dl_ref,
                 dq_ref, db_ref, dsink_ref, dk_hbm, dv_hbm, dk_acc, dv_acc):
        kv, i = pl.program_id(0), pl.program_id(1)

        @pl.when(i == 0)
        def _():
            dk_acc[...] = jnp.zeros_like(dk_acc)
            dv_acc[...] = jnp.zeros_like(dv_acc)
            db_ref[...] = jnp.zeros_like(db_ref)
            dsink_ref[...] = jnp.zeros_like(dsink_ref)

        kb_ = key_bias(i)
        k_all, v_all = both(ka_ref, kb_ref), both(va_ref, vb_ref)
        dk_t = jnp.zeros((kw, dh), F32)
        dv_t = jnp.zeros((kw, dh), F32)
        for hh in range(g):
            lse, dl, do = lse_ref[hh], dl_ref[hh], do_ref[hh]
            p = jnp.exp(_dot(q_ref[hh], k_all, NT) * scale + b_ref[hh] + kb_ - lse)
            ds = p * (_dot(do, v_all, NT) - dl)
            db_ref[hh] += ds
            total = jnp.broadcast_to(-jnp.sum(jnp.exp(sk_ref[hh][:, :1] - lse) * dl, axis=0, keepdims=True), (1, LANES))
            dsink_ref[hh] += jnp.where(lax.broadcasted_iota(jnp.int32, (1, LANES), 1) == 0, total, 0.0)
            dsb = (ds * scale).astype(MXU_DTYPE)
            dq_ref[hh] = _dot(dsb, k_all, NN).astype(dq_ref.dtype)
            dk_t += _dot(dsb, q_ref[hh], TN)
            dv_t += _dot(p, do, TN)
        rows = pl.ds(pl.multiple_of(i * tq, tq), kw)
        dk_acc[rows, :] += dk_t
        dv_acc[rows, :] += dv_t

        @pl.when(i == nt - 1)
        def _():
            pltpu.sync_copy(dk_acc, dk_hbm.at[kv])
            pltpu.sync_copy(dv_acc, dv_hbm.at[kv])

    def q_spec(width):
        return pl.BlockSpec((g, tq, width), lambda kv, i: (kv, i, 0))

    ka_spec = pl.BlockSpec((None, half, dh), lambda kv, i: (kv, i, 0))
    kb_spec = pl.BlockSpec((None, half, dh), lambda kv, i: (kv, i + 1, 0))
    b_spec = pl.BlockSpec((g, tq, kw), lambda kv, i: (kv, 0, 0))
    sk_spec = pl.BlockSpec((g, 1, LANES), lambda kv, i: (kv, 0, 0))

    def padded(t):
        return jnp.pad(t, ((0, 0), (C_BLOCK, C_BLOCK), (0, 0)))

    def run_fwd(q, kp, vp, tile, sink_b):
        return _pallas(
            fwd_body, name=name + "_fwd", grid=(C_KV_HEADS, nt),
            in_specs=[q_spec(dh), ka_spec, kb_spec, ka_spec, kb_spec, b_spec, sk_spec], out_specs=[q_spec(dh), q_spec(1)],
            out_shape=[jax.ShapeDtypeStruct((hq, s, dh), ACT_DTYPE), jax.ShapeDtypeStruct((hq, s, 1), F32)],
            compiler_params=_cparams("parallel", "parallel"))(q, kp, kp, vp, vp, tile, sink_b)

    @jax.custom_vjp
    def f(q, k, v, tile, sink_b):
        return run_fwd(q, padded(k), padded(v), tile, sink_b)[0]

    def fwd(q, k, v, tile, sink_b):
        kp, vp = padded(k), padded(v)
        o, lse = run_fwd(q, kp, vp, tile, sink_b)
        return o, (q, kp, vp, tile, sink_b, o, lse)

    def bwd(res, do):
        q, kp, vp, tile, sink_b, o, lse = res
        delta = _rowdot(o, do, name + "_delta")
        any_spec = pl.BlockSpec(memory_space=pl.ANY)
        acc = jax.ShapeDtypeStruct((C_KV_HEADS, s + 2 * C_BLOCK, dh), F32)
        dq, dtile, dsink, dkp, dvp = _pallas(
            bwd_body, name=name + "_bwd", grid=(C_KV_HEADS, nt),
            in_specs=[q_spec(dh), ka_spec, kb_spec, ka_spec, kb_spec, b_spec, sk_spec, q_spec(dh), q_spec(1), q_spec(1)],
            out_specs=[q_spec(dh), b_spec, sk_spec, any_spec, any_spec],
            out_shape=[jax.ShapeDtypeStruct((hq, s, dh), q.dtype), jax.ShapeDtypeStruct((hq, tq, kw), F32),
                       jax.ShapeDtypeStruct((hq, 1, LANES), F32), acc, acc],
            scratch_shapes=[pltpu.VMEM((s + 2 * C_BLOCK, dh), F32), pltpu.VMEM((s + 2 * C_BLOCK, dh), F32)],
            compiler_params=_cparams("parallel", "arbitrary"))(q, kp, kp, vp, vp, tile, sink_b, do, lse, delta)
        unpad = lambda t: t[:, C_BLOCK:-C_BLOCK].astype(kp.dtype)
        return dq, unpad(dkp), unpad(dvp), dtile, dsink

    f.defvjp(fwd, bwd)
    return f(q, k, v, tile, sink_b)


HG_PREP_ROWS = 256
HG_GROUP = 8
HG_INTRA_BLOCK = 256
HG_INTER_CHUNKS = 32
MLA_FWD_TQ, MLA_FWD_TK, MLA_BWD_TQ, MLA_TK = 2048, 2048, 2048, 1024


def _hdot(a, b, dims, ones_first=True):
    b16 = jnp.bfloat16
    ones, full = (a, b) if ones_first else (b, a)
    hi = full.astype(b16)
    rest = full - hi.astype(F32)
    mid = rest.astype(b16)
    lo = (rest - mid.astype(F32)).astype(b16)
    ones16 = ones.astype(b16)
    dn = (dims, ((), ()))

    def one(piece):
        lhs, rhs = (ones16, piece) if ones_first else (piece, ones16)
        return lax.dot_general(lhs, rhs, dn, preferred_element_type=F32)

    return one(hi) + one(mid) + one(lo)


HG_OUT_ROWS = 256


def hgrn_out(o, gate, g_out, name="hg_out"):
    s, c = o.shape
    tb = _tile(s, HG_OUT_ROWS)
    g_row = jnp.tile(g_out.astype(F32).reshape(1, B_DV), (1, c // B_DV))

    def head_mean(v):
        r = lax.broadcasted_iota(jnp.int32, (c, c), 0)
        cc = lax.broadcasted_iota(jnp.int32, (c, c), 1)
        same = jnp.where(r // B_DV == cc // B_DV, 1.0, 0.0).astype(F32)
        return _hdot(v, same, NN, ones_first=False) * (1.0 / B_DV)

    def fwd_body(o_ref, gate_ref, g_ref, y_ref):
        ov, gt = o_ref[...], gate_ref[...]
        xh = ov * lax.rsqrt(head_mean(ov * ov) + EPS)
        y_ref[...] = xh * g_ref[...] * (gt * _sigmoid(gt))

    def bwd_body(o_ref, gate_ref, g_ref, dy_ref, do_ref, dgate_ref, dg_ref):
        ov, gt, gv, dy = o_ref[...], gate_ref[...], g_ref[...], dy_ref[...]
        r = lax.rsqrt(head_mean(ov * ov) + EPS)
        xh = ov * r
        sg = _sigmoid(gt)
        dgate_ref[...] = dy * xh * gv * (sg * (1.0 + gt * (1.0 - sg)))
        dn = dy * (gt * sg)
        dxh = dn * gv
        do_ref[...] = r * (dxh - xh * head_mean(dxh * xh))

        @pl.when(pl.program_id(0) == 0)
        def _():
            dg_ref[...] = jnp.zeros_like(dg_ref)

        dg_ref[...] += jnp.sum(dn * xh, axis=0, keepdims=True)

    row = pl.BlockSpec((tb, c), lambda i: (i, 0))
    vec = pl.BlockSpec((1, c), lambda i: (0, 0))
    shape = jax.ShapeDtypeStruct((s, c), F32)

    def run_fwd(o, gate, g_row):
        return _pallas(fwd_body, name=name + "_fwd", grid=(s // tb,), in_specs=[row, row, vec], out_specs=row, out_shape=shape,
                       compiler_params=_cparams("parallel"))(o, gate, g_row)

    @jax.custom_vjp
    def f(o, gate, g_row):
        return run_fwd(o, gate, g_row)

    def fwd(o, gate, g_row):
        return run_fwd(o, gate, g_row), (o, gate, g_row)

    def bwd(res, dy):
        o, gate, g_row = res
        return tuple(_pallas(
            bwd_body, name=name + "_bwd", grid=(s // tb,), in_specs=[row, row, vec, row], out_specs=[row, row, vec],
            out_shape=[shape, shape, jax.ShapeDtypeStruct((1, c), F32)], compiler_params=_cparams("arbitrary"))(o, gate, g_row, dy))

    f.defvjp(fwd, bwd)
    return f(o, gate, g_row)


def hgrn_prep(q, z, lb, reverse, name):
    n_hp, s, tc = q.shape
    tb = _tile(s, HG_PREP_ROWS)
    ncb = tb // B_CHUNK

    def chunk_matrices():
        r = lax.broadcasted_iota(jnp.int32, (tb, tb), 0)
        cc = lax.broadcasted_iota(jnp.int32, (tb, tb), 1)
        same = r // B_CHUNK == cc // B_CHUNK
        tri = (cc >= r) if reverse else (cc <= r)
        cum = jnp.where(jnp.logical_and(same, tri), 1.0, 0.0).astype(F32)
        every = jnp.where(same, 1.0, 0.0).astype(F32)
        pr = lax.broadcasted_iota(jnp.int32, (ncb, tb), 0)
        pc = lax.broadcasted_iota(jnp.int32, (ncb, tb), 1)
        per_chunk = jnp.where(pc // B_CHUNK == pr, 1.0, 0.0).astype(F32)
        return cum, every, per_chunk

    def gates(zv, lbv):
        e = jnp.exp(-jnp.abs(zv))
        big, small = 1.0 / (1.0 + e), e / (1.0 + e)
        sig = jnp.where(zv >= 0, big, small)
        nsig = jnp.where(zv >= 0, small, big)
        f = lbv + (1.0 - lbv) * sig
        return sig, nsig, f, jnp.log(jnp.maximum(f, TINY)), (1.0 - lbv) * nsig

    def fwd_body(q_ref, z_ref, lb_ref, qd_ref, ki_ref, ke_ref, dec_ref):
        cum, every, per_chunk = chunk_matrices()
        _, _, _, lf, key = gates(z_ref[...], lb_ref[...])
        b = _hdot(cum, lf, NN)
        tot = _hdot(every, lf, NN)
        qd_ref[...] = q_ref[...] * jnp.exp(b)
        ki_ref[...] = key * jnp.exp(-b)
        ke_ref[...] = key * jnp.exp(tot - b)
        dec_ref[...] = jnp.exp(_hdot(per_chunk, lf, NN))

    def bwd_body(q_ref, z_ref, lb_ref, dqd_ref, dki_ref, dke_ref, ddec_ref, dq_ref, dz_ref, dlb_ref):
        cum, every, per_chunk = chunk_matrices()
        lbv = lb_ref[...]
        sig, nsig, f, lf, key = gates(z_ref[...], lbv)
        b = _hdot(cum, lf, NN)
        tot = _hdot(every, lf, NN)
        e_b, e_nb, e_tb = jnp.exp(b), jnp.exp(-b), jnp.exp(tot - b)
        dqd, dki, dke = dqd_ref[...], dki_ref[...], dke_ref[...]
        dq_ref[...] = dqd * e_b
        dkey = dki * e_nb + dke * e_tb
        t_end = dke * key * e_tb
        db = dqd * q_ref[...] * e_b - dki * key * e_nb - t_end
        dtot = ddec_ref[...] * jnp.exp(_hdot(per_chunk, lf, NN)) + _hdot(per_chunk, t_end, NN)
        dlf = _hdot(cum, db, TN) + _hdot(per_chunk, dtot, TN)
        df = jnp.where(f > TINY, dlf / f, 0.0)
        one_m_lb = 1.0 - lbv
        dz_ref[...] = (df - dkey) * one_m_lb * sig * nsig
        dlb_part = jnp.sum(df * nsig - dkey * nsig, axis=0, keepdims=True)

        @pl.when(pl.program_id(1) == 0)
        def _():
            dlb_ref[...] = jnp.zeros_like(dlb_ref)

        dlb_ref[...] += dlb_part

    tok = pl.BlockSpec((None, tb, tc), lambda j, i: (j, i, 0))
    vec = pl.BlockSpec((None, 1, tc), lambda j, i: (j, 0, 0))
    chk = pl.BlockSpec((None, ncb, tc), lambda j, i: (j, i, 0))
    grid = (n_hp, s // tb)
    tok_shape = jax.ShapeDtypeStruct((n_hp, s, tc), F32)
    chk_shape = jax.ShapeDtypeStruct((n_hp, s // B_CHUNK, tc), F32)

    def run_fwd(q, z, lb):
        return _pallas(fwd_body, name=name + "_fwd", grid=grid, in_specs=[tok, tok, vec], out_specs=[tok, tok, tok, chk],
                       out_shape=[tok_shape, tok_shape, tok_shape, chk_shape],
                       compiler_params=_cparams("parallel", "parallel"))(q, z, lb)

    @jax.custom_vjp
    def f(q, z, lb):
        return tuple(run_fwd(q, z, lb))

    def fwd(q, z, lb):
        return tuple(run_fwd(q, z, lb)), (q, z, lb)

    def bwd(res, cts):
        q, z, lb = res
        dq, dz, dlb = _pallas(
            bwd_body, name=name + "_bwd", grid=grid, in_specs=[tok, tok, vec, tok, tok, tok, chk], out_specs=[tok, tok, vec],
            out_shape=[tok_shape, tok_shape, jax.ShapeDtypeStruct((n_hp, 1, tc), F32)],
            compiler_params=_cparams("parallel", "arbitrary"))(q, z, lb, *cts)
        return dq, dz, dlb

    f.defvjp(fwd, bwd)
    return f(q, z, lb)


def _pair_cols(ref, hh, width):
    return ref[:, hh * width:(hh + 1) * width]


def hgrn_intra(qd, ki, v, reverse, name):
    s = qd.shape[1]
    tb = _tile(s, HG_INTRA_BLOCK)
    wk, wv = HG_GROUP * B_DK, HG_GROUP * B_DV

    def mask():
        r = lax.broadcasted_iota(jnp.int32, (tb, tb), 0)
        c = lax.broadcasted_iota(jnp.int32, (tb, tb), 1)
        return jnp.logical_and(r // B_CHUNK == c // B_CHUNK, (c >= r) if reverse else (c <= r))

    def fwd_body(q_ref, k_ref, v_ref, o_ref):
        msk = mask()
        for hh in range(HG_GROUP):
            sc = jnp.where(msk, _dot(_pair_cols(q_ref, hh, B_DK), _pair_cols(k_ref, hh, B_DK), NT), 0.0)
            o_ref[:, hh * B_DV:(hh + 1) * B_DV] = _dot(sc, _pair_cols(v_ref, hh, B_DV), NN)

    def bwd_body(q_ref, k_ref, v_ref, do_ref, dq_ref, dk_ref, dv_ref):
        msk = mask()
        for hh in range(HG_GROUP):
            q, k = _pair_cols(q_ref, hh, B_DK), _pair_cols(k_ref, hh, B_DK)
            vv, do = _pair_cols(v_ref, hh, B_DV), _pair_cols(do_ref, hh, B_DV)
            sc = jnp.where(msk, _dot(q, k, NT), 0.0)
            ds = jnp.where(msk, _dot(do, vv, NT), 0.0)
            dq_ref[:, hh * B_DK:(hh + 1) * B_DK] = _dot(ds, k, NN)
            dk_ref[:, hh * B_DK:(hh + 1) * B_DK] = _dot(ds, q, TN)
            dv_ref[:, hh * B_DV:(hh + 1) * B_DV] = _dot(sc, do, TN)

    ks = pl.BlockSpec((None, tb, wk), lambda hp, i: (hp, i, 0))
    vs = pl.BlockSpec((None, tb, wv), lambda hp, i: (hp, i, 0))
    grid = (B_HEADS // HG_GROUP, s // tb)

    def run_fwd(qd, ki, v):
        return _pallas(fwd_body, name=name + "_fwd", grid=grid, in_specs=[ks, ks, vs], out_specs=vs,
                       out_shape=jax.ShapeDtypeStruct(v.shape, F32), compiler_params=_cparams("parallel", "parallel"))(qd, ki, v)

    @jax.custom_vjp
    def f(qd, ki, v):
        return run_fwd(qd, ki, v)

    def fwd(qd, ki, v):
        return run_fwd(qd, ki, v), (qd, ki, v)

    def bwd(res, do):
        qd, ki, v = res
        return tuple(_pallas(
            bwd_body, name=name + "_bwd", grid=grid, in_specs=[ks, ks, vs, vs], out_specs=[ks, ks, vs],
            out_shape=[jax.ShapeDtypeStruct(qd.shape, F32), jax.ShapeDtypeStruct(ki.shape, F32),
                       jax.ShapeDtypeStruct(v.shape, F32)],
            compiler_params=_cparams("parallel", "parallel"))(qd, ki, v, do))

    f.defvjp(fwd, bwd)
    return f(qd, ki, v)


def hgrn_inter(qd, ke, v, dec, reverse, name):
    s = qd.shape[1]
    nc = s // B_CHUNK
    cpb = HG_INTER_CHUNKS if nc % HG_INTER_CHUNKS == 0 else nc
    tb = cpb * B_CHUNK
    nblk = nc // cpb
    wk, wv = HG_GROUP * B_DK, HG_GROUP * B_DV
    n_hp = B_HEADS // HG_GROUP

    def rows(c):
        return pl.ds(c * B_CHUNK, B_CHUNK)

    def kcols(hh):
        return slice(hh * B_DK, (hh + 1) * B_DK)

    def vcols(hh):
        return slice(hh * B_DV, (hh + 1) * B_DV)

    def order(flip):
        return reversed(range(cpb)) if flip else range(cpb)

    def fwd_body(q_ref, k_ref, v_ref, dec_ref, o_ref, st_ref, state):
        @pl.when(pl.program_id(1) == 0)
        def _():
            state[...] = jnp.zeros_like(state)

        for c in order(reverse):
            for hh in range(HG_GROUP):
                st = state[hh]
                st_ref[c, hh] = st
                o_ref[rows(c), vcols(hh)] = _dot(q_ref[rows(c), kcols(hh)], st, NT)
                state[hh] = st * dec_ref[pl.ds(c, 1), kcols(hh)] + _dot(v_ref[rows(c), vcols(hh)], k_ref[rows(c), kcols(hh)], TN)

    def bwd_body(q_ref, k_ref, v_ref, dec_ref, st_ref, do_ref, dq_ref, dk_ref, dv_ref, ddec_ref, dstate):
        @pl.when(pl.program_id(1) == 0)
        def _():
            dstate[...] = jnp.zeros_like(dstate)

        for c in order(not reverse):
            for hh in range(HG_GROUP):
                dst = dstate[hh]
                st = st_ref[c, hh]
                do_c = do_ref[rows(c), vcols(hh)]
                dk_ref[rows(c), kcols(hh)] = _dot(v_ref[rows(c), vcols(hh)], dst, NN)
                dv_ref[rows(c), vcols(hh)] = _dot(k_ref[rows(c), kcols(hh)], dst, NT)
                ddec_ref[pl.ds(c, 1), kcols(hh)] = jnp.sum(dst * st, axis=0, keepdims=True)
                dq_ref[rows(c), kcols(hh)] = _dot(do_c, st, NN)
                dstate[hh] = dst * dec_ref[pl.ds(c, 1), kcols(hh)] + _dot(do_c, q_ref[rows(c), kcols(hh)], TN)

    def specs(flip):
        blk = (lambda i: nblk - 1 - i) if flip else (lambda i: i)
        tok_k = pl.BlockSpec((None, tb, wk), lambda hp, i: (hp, blk(i), 0))
        tok_v = pl.BlockSpec((None, tb, wv), lambda hp, i: (hp, blk(i), 0))
        chk = pl.BlockSpec((None, cpb, wk), lambda hp, i: (hp, blk(i), 0))
        sts = pl.BlockSpec((None, cpb, HG_GROUP, B_DV, B_DK), lambda hp, i: (hp, blk(i), 0, 0, 0))
        return tok_k, tok_v, chk, sts

    scratch = [pltpu.VMEM((HG_GROUP, B_DV, B_DK), F32)]

    def run_fwd(qd, ke, v, dec):
        tok_k, tok_v, chk, sts = specs(reverse)
        return _pallas(
            fwd_body, name=name + "_fwd", grid=(n_hp, nblk), in_specs=[tok_k, tok_k, tok_v, chk], out_specs=[tok_v, sts],
            out_shape=[jax.ShapeDtypeStruct(v.shape, F32), jax.ShapeDtypeStruct((n_hp, nc, HG_GROUP, B_DV, B_DK), F32)],
            scratch_shapes=scratch, compiler_params=_cparams("parallel", "arbitrary"))(qd, ke, v, dec)

    @jax.custom_vjp
    def f(qd, ke, v, dec):
        return run_fwd(qd, ke, v, dec)[0]

    def fwd(qd, ke, v, dec):
        o, st = run_fwd(qd, ke, v, dec)
        return o, (qd, ke, v, dec, st)

    def bwd(res, do):
        qd, ke, v, dec, st = res
        tok_k, tok_v, chk, sts = specs(not reverse)
        return tuple(_pallas(
            bwd_body, name=name + "_bwd", grid=(n_hp, nblk), in_specs=[tok_k, tok_k, tok_v, chk, sts, tok_v],
            out_specs=[tok_k, tok_k, tok_v, chk],
            out_shape=[jax.ShapeDtypeStruct(qd.shape, F32), jax.ShapeDtypeStruct(ke.shape, F32),
                       jax.ShapeDtypeStruct(v.shape, F32), jax.ShapeDtypeStruct(dec.shape, F32)],
            scratch_shapes=scratch, compiler_params=_cparams("parallel", "arbitrary"))(qd, ke, v, dec, st, do))

    f.defvjp(fwd, bwd)
    return f(qd, ke, v, dec)


def loss_head(y, target, name="loss"):
    s, d = y.shape
    tr = _row_tile(s, d)

    def body(y_ref, t_ref, o_ref):
        @pl.when(pl.program_id(0) == 0)
        def _():
            o_ref[...] = jnp.zeros_like(o_ref)

        e = y_ref[...] - t_ref[...]
        part = jnp.sum(jnp.sum(e * e, axis=-1, keepdims=True), axis=0, keepdims=True) * (0.5 / d)
        o_ref[...] += jnp.broadcast_to(part, o_ref.shape)

    spec = pl.BlockSpec((tr, d), lambda i: (i, 0))

    def run(y, t):
        out = _pallas(body, name=name, grid=(s // tr,), in_specs=[spec, spec],
                      out_specs=pl.BlockSpec((SUBLANES, LANES), lambda i: (0, 0)),
                      out_shape=jax.ShapeDtypeStruct((SUBLANES, LANES), F32), compiler_params=_cparams("arbitrary"))(y, t)
        return out[0, 0]

    @jax.custom_vjp
    def f(y, t):
        return run(y, t)

    def fwd(y, t):
        return run(y, t), (y, t)

    def bwd(res, g):
        y, t = res
        dy = g * (y - t) * (1.0 / d)
        return dy, -dy

    f.defvjp(fwd, bwd)
    return f(y, target)


def _mesh_pos():
    return lax.axis_index("x"), lax.axis_index("y"), lax.axis_index("c")


def all_gather_shards(shards):
    n = len(shards)

    def body(*refs):
        ins, outs = refs[:n], refs[n:2 * n]
        send_sems, recv_sems, local_sems = refs[2 * n:]
        x, y, c = _mesh_pos()
        me, sibling = (x, y, c), (x, y, 1 - c)
        chips = [(1 - x, y), (x, 1 - y), (1 - x, 1 - y)]

        def slot(t, px, py, pc):
            return outs[t].at[4 * px + 2 * py + pc]

        def copy(t, k, block, to, src=None):
            return pltpu.make_async_remote_copy(
                src_ref=slot(t, *block) if src is None else src, dst_ref=slot(t, *block), send_sem=send_sems.at[t, k],
                recv_sem=recv_sems.at[t, k], device_id=to, device_id_type=pl.DeviceIdType.MESH)

        mine = [pltpu.make_async_copy(ins[t], slot(t, *me), local_sems.at[t]) for t in range(n)]
        for cp in mine:
            cp.start()
        first = []
        for t in range(n):
            first.append(copy(t, 0, me, sibling, src=ins[t]))
            first += [copy(t, 1 + j, me, (*chip, c), src=ins[t]) for j, chip in enumerate(chips)]
        for cp in first:
            cp.start()
        passed = []
        for j, chip in enumerate(chips):
            for t in range(n):
                copy(t, 1 + j, (*chip, c), me).wait_recv()
                cp = copy(t, 4 + j, (*chip, c), sibling)
                cp.start()
                passed.append(cp)
        for t in range(n):
            copy(t, 0, sibling, me).wait_recv()
            for j, chip in enumerate(chips):
                copy(t, 4 + j, (*chip, 1 - c), me).wait_recv()
        for cp in first + passed:
            cp.wait_send()
        for cp in mine:
            cp.wait()

    any_spec = pl.BlockSpec(memory_space=pl.ANY)
    return _pallas(
        body, name="all_gather_weights", out_shape=[jax.ShapeDtypeStruct((N_DEV, *s.shape), s.dtype) for s in shards],
        in_specs=[any_spec] * n, out_specs=[any_spec] * n,
        scratch_shapes=[pltpu.SemaphoreType.DMA((n, 7)), pltpu.SemaphoreType.DMA((n, 7)), pltpu.SemaphoreType.DMA((n,))],
    )(*shards)


def all_to_all_blocks(stacks):
    n = len(stacks)

    def body(*refs):
        ins, outs = refs[:n], refs[n:2 * n]
        send_sems, recv_sems, local_sems = refs[2 * n:]
        x, y, c = _mesh_pos()
        me = 4 * x + 2 * y + c
        mine = [pltpu.make_async_copy(ins[t].at[me], outs[t].at[me], local_sems.at[t]) for t in range(n)]
        for cp in mine:
            cp.start()
        copies = []
        for k in range(1, N_DEV):
            px = 1 - x if k & 4 else x
            py = 1 - y if k & 2 else y
            pc = 1 - c if k & 1 else c
            for t in range(n):
                cp = pltpu.make_async_remote_copy(
                    src_ref=ins[t].at[4 * px + 2 * py + pc], dst_ref=outs[t].at[me], send_sem=send_sems.at[t, k - 1],
                    recv_sem=recv_sems.at[t, k - 1], device_id=(px, py, pc), device_id_type=pl.DeviceIdType.MESH)
                cp.start()
                copies.append(cp)
        for cp in copies:
            cp.wait_recv()
        for cp in copies:
            cp.wait_send()
        for cp in mine:
            cp.wait()

    any_spec = pl.BlockSpec(memory_space=pl.ANY)
    return _pallas(
        body, name="all_to_all_grads", out_shape=[jax.ShapeDtypeStruct(s.shape, s.dtype) for s in stacks],
        in_specs=[any_spec] * n, out_specs=[any_spec] * n,
        scratch_shapes=[pltpu.SemaphoreType.DMA((n, 7)), pltpu.SemaphoreType.DMA((n, 7)), pltpu.SemaphoreType.DMA((n,))],
    )(*stacks)


def all_gather_small(v):
    r, w = v.shape

    def body(x_ref, out_ref, send_sems, recv_sems):
        x, y, c = _mesh_pos()
        me = 4 * x + 2 * y + c
        copies = []
        for k in range(1, N_DEV):
            px = 1 - x if k & 4 else x
            py = 1 - y if k & 2 else y
            pc = 1 - c if k & 1 else c
            cp = pltpu.make_async_remote_copy(
                src_ref=x_ref, dst_ref=out_ref.at[me], send_sem=send_sems.at[k - 1], recv_sem=recv_sems.at[k - 1],
                device_id=(px, py, pc), device_id_type=pl.DeviceIdType.MESH)
            cp.start()
            copies.append(cp)
        out_ref[me] = x_ref[...]
        for cp in copies:
            cp.wait_recv()
        for cp in copies:
            cp.wait_send()

    vmem = pl.BlockSpec(memory_space=pltpu.VMEM)
    return _pallas(
        body, name="all_gather_small", out_shape=jax.ShapeDtypeStruct((N_DEV, r, w), v.dtype), in_specs=[vmem],
        out_specs=vmem, scratch_shapes=[pltpu.SemaphoreType.DMA((7,)), pltpu.SemaphoreType.DMA((7,))],
    )(v)


def adamw_rows(parts, w, m, v, name):
    n, r, lanes = parts.shape
    tr = _tile(r, max(SUBLANES, (256 * 1024) // lanes), SUBLANES)
    c1 = 1.0 / (1.0 - ADAM_B1 ** ADAM_STEP)
    c2 = 1.0 / (1.0 - ADAM_B2 ** ADAM_STEP)

    def body(p_ref, w_ref, m_ref, v_ref, g_ref, d_ref, nm_ref, nv_ref):
        g = p_ref[0].astype(F32)
        for j in range(1, n):
            g = g + p_ref[j].astype(F32)
        nm = ADAM_B1 * m_ref[...] + (1.0 - ADAM_B1) * g
        nv = ADAM_B2 * v_ref[...] + (1.0 - ADAM_B2) * (g * g)
        g_ref[...] = g
        nm_ref[...] = nm
        nv_ref[...] = nv
        d_ref[...] = -ADAM_LR * ((nm * c1) / (jnp.sqrt(nv * c2) + ADAM_EPS) + ADAM_WD * w_ref[...])

    row = pl.BlockSpec((tr, lanes), lambda i: (i, 0))
    out = jax.ShapeDtypeStruct((r, lanes), F32)
    return _pallas(body, name=name, grid=(r // tr,), in_specs=[pl.BlockSpec((n, tr, lanes), lambda i: (0, i, 0)), row, row, row],
                   out_specs=[row, row, row, row], out_shape=[out, out, out, out], compiler_params=_cparams("parallel"))(
        parts, w, m, v)


def _padded(n):
    return -(-n // PACK_QUANTUM) * PACK_QUANTUM


def _pack(pieces, total_rows=None):
    flat = []
    for p in pieces:
        p = p.reshape(-1).astype(F32)
        flat.append(jnp.pad(p, (0, _padded(p.size) - p.size)))
    out = jnp.concatenate(flat).reshape(-1, LANES)
    if total_rows is not None and out.shape[0] != total_rows:
        out = jnp.pad(out, ((0, total_rows - out.shape[0]), (0, 0)))
    return out


def _pack_rows(sizes):
    rows = sum(_padded(n) for n in sizes) // LANES
    return -(-rows // PACK_ROW_TILE) * PACK_ROW_TILE


def _unpack(rows, shapes):
    lead = rows.shape[:-2]
    flat = rows.reshape(*lead, -1)
    out, off = [], 0
    for shp in shapes:
        n = int(np.prod(shp))
        out.append(flat[..., off:off + n].reshape(*lead, *shp))
        off += _padded(n)
    return out


def _shards_to_full(stacked, axis):
    moved = jnp.moveaxis(stacked, 0, axis)
    shp = list(stacked.shape[1:])
    shp[axis] *= N_DEV
    return moved.reshape(shp)


def _full_to_shards(full, axis):
    shp = list(full.shape)
    shp[axis:axis + 1] = [N_DEV, shp[axis] // N_DEV]
    return jnp.moveaxis(full.reshape(shp), axis, 0)


def _heads(t, n, d, dtype=ACT_DTYPE):
    return jnp.transpose(t.reshape(t.shape[0], n, d), (1, 0, 2)).astype(dtype)


def _unheads(t):
    return jnp.transpose(t, (1, 0, 2)).reshape(t.shape[1], -1)


def _rope_tables(s):
    half = A_ROPE // 2
    inv = ROPE_THETA ** (-jnp.arange(half, dtype=F32) / half)
    ang = jnp.arange(s, dtype=jnp.int32).astype(F32)[:, None] * inv[None, :]
    return jnp.cos(ang), jnp.sin(ang)


def _rope(t, cos, sin):
    half = A_ROPE // 2
    t1, t2 = t[..., :half], t[..., half:]
    c, sn = cos[:, None, :], sin[:, None, :]
    return jnp.concatenate([t1 * c - t2 * sn, t1 * sn + t2 * c], axis=-1)


def _t5_bucket(rel):
    nb = REL_BUCKETS // 2
    max_exact = nb // 2
    ret = (rel > 0).astype(jnp.int32) * nb
    n = jnp.abs(rel)
    large = max_exact + (jnp.log(jnp.maximum(n, 1).astype(F32) / max_exact)
                         / math.log(REL_MAX_DIST / max_exact) * (nb - max_exact)).astype(jnp.int32)
    large = jnp.minimum(large, nb - 1)
    return ret + jnp.where(n < max_exact, n, large)


def _window_bias(rel_bias):
    span = 3 * C_BLOCK
    rel = jnp.arange(span)[None, :] - C_BLOCK - jnp.arange(C_BLOCK)[:, None]
    onehot = (_t5_bucket(rel)[..., None] == jnp.arange(REL_BUCKETS)).astype(F32)
    bias = jnp.einsum("qkb,bh->hqk", onehot, rel_bias.astype(F32), precision=lax.Precision.HIGHEST)
    bias = jnp.where((jnp.abs(rel) <= C_WINDOW)[None], bias, NEG)
    return jnp.transpose(bias.reshape(C_HEADS, C_BLOCK, 3, C_BLOCK), (0, 2, 1, 3))


def _mla(cq, ckv, kr, gq, gkv, wuq, wukv, cos, sin):
    s = cq.shape[0]
    q = linear(rmsnorm(cq, gq, ACT_DTYPE, "rms_cq"), wuq, name="a_wuq").reshape(s, A_HEADS, A_NOPE + A_ROPE)
    q = jnp.concatenate([q[..., :A_NOPE], _rope(q[..., A_NOPE:], cos, sin)], axis=-1)
    kv = linear(rmsnorm(ckv, gkv, ACT_DTYPE, "rms_ckv"), wukv, name="a_wukv").reshape(s, A_HEADS, A_NOPE + A_V)
    k_rope = jnp.broadcast_to(_rope(kr[:, None, :], cos, sin), (s, A_HEADS, A_ROPE))
    k = jnp.concatenate([kv[..., :A_NOPE], k_rope], axis=-1)
    v = kv[..., A_NOPE:]
    return mla_attention(q, k, v, (A_NOPE + A_ROPE) ** -0.5)


def _hgrn2(q, f_fwd, f_bwd, i, g, lb_fwd, lb_bwd, g_out):
    s = q.shape[0]
    n_hp = B_HEADS // HG_GROUP
    pairs = lambda t: jnp.transpose(t.reshape(s, n_hp, -1), (1, 0, 2))
    qp, vp = pairs(q), pairs(i)
    o = None
    for z, lb, rev, tag in ((f_fwd, lb_fwd, False, "hgf"), (f_bwd, lb_bwd, True, "hgb")):
        qd, ki, ke, dec = hgrn_prep(qp, pairs(z), lb.astype(F32).reshape(n_hp, 1, -1), rev, tag + "_prep")
        part = hgrn_intra(qd, ki, vp, rev, tag + "_intra") + hgrn_inter(qd, ke, vp, dec, rev, tag + "_inter")
        o = part if o is None else o + part
    return hgrn_out(jnp.transpose(o, (1, 0, 2)).reshape(s, B_HEADS * B_DV), g, g_out)


def _cross(x, h, mem_n, wq, wkv, wo):
    q = linear(h, wq, name="x_wq").reshape(h.shape[0], X_HEADS, X_DH)
    kv = linear(mem_n, wkv, name="x_wkv").reshape(mem_n.shape[0], 2, X_HEADS, X_DH)
    o_t = mla_attention(q, kv[:, 0], kv[:, 1], X_DH ** -0.5, name="cross")
    return linear_res(x, o_t, wo, name="x_wo", a_transposed=True)


def _pad_w_in(w):
    cut = A_Q_RANK + A_KV_RANK + A_ROPE
    return jnp.concatenate([w[:, :cut], jnp.zeros((w.shape[0], KR_PAD), w.dtype), w[:, cut:]], axis=1)


def _model_loss(p, x, mem, target):
    s = x.shape[0]
    cos, sin = _rope_tables(s)
    sm = jax.nn.softmax(p["b_lb"].astype(F32), axis=1)
    lower_bounds = jnp.cumsum(sm, axis=1) - sm[:, :1]
    bias = _window_bias(p["rel_bias"])
    for l in range(DEPTH):
        h, x = rmsnorm_keep(x, p["g_mix"][l], ACT_DTYPE, "rms_mix")
        z = linear(h, _pad_w_in(p["w_in"][l]), name="w_in")
        parts, start = [], 0
        for width in IN_SPLITS_PADDED:
            parts.append(z[:, start:start + width])
            start += width
        a_cq, a_ckv, a_kr, b_q, b_ff, b_fb, b_i, b_g, c_q, c_k, c_v, gate_a, gate_b, gate_c = parts
        y_a = _mla(a_cq, a_ckv, a_kr[:, :A_ROPE], p["a_gq"][l], p["a_gkv"][l], p["a_wuq"][l], p["a_wukv"][l], cos, sin)
        y_b = _hgrn2(b_q, b_ff, b_fb, b_i, b_g, lower_bounds[0, l], lower_bounds[1, l], p["b_gout"][l])
        y_c = _unheads(window_attention(_heads(c_q, C_HEADS, C_DH, F32), _heads(c_k, C_KV_HEADS, C_DH, F32),
                                        _heads(c_v, C_KV_HEADS, C_DH, F32), bias, p["c_sink"][l]))
        x = gated_merge_out(x, y_a, y_b, y_c, gate_a, gate_b, gate_c, p["w_br_a"][l], p["w_br_b"][l], p["w_br_c"][l],
                            p["w_out"][l])
        h, x = rmsnorm_keep(x, p["g_x"][l], ACT_DTYPE, "rms_x")
        x = _cross(x, h, rmsnorm(mem, p["g_mem"][l], ACT_DTYPE, "rms_mem"), p["x_wq"][l], p["x_wkv"][l], p["x_wo"][l])
        h, x = rmsnorm_keep(x, p["g_ffn"][l], ACT_DTYPE, "rms_ffn")
        x = swiglu_ffn(x, h, p["f_w1"][l], p["f_w3"][l], p["f_w2"][l])
    y = rmsnorm(x, p["g_final"], F32, "rms_final")
    return loss_head(y, target)


def kernel(x, mem, w_in, g_mix, a_gq, a_gkv, a_wuq, a_wukv, b_lb, b_gout, c_sink, rel_bias, w_br_a, w_br_b, w_br_c, w_out, g_x, g_mem, x_wq, x_wkv, x_wo, g_ffn, f_w1, f_w3, f_w2, g_final, loss_target, m_w_in, m_g_mix, m_a_gq, m_a_gkv, m_a_wuq, m_a_wukv, m_b_lb, m_b_gout, m_c_sink, m_rel_bias, m_w_br_a, m_w_br_b, m_w_br_c, m_w_out, m_g_x, m_g_mem, m_x_wq, m_x_wkv, m_x_wo, m_g_ffn, m_f_w1, m_f_w3, m_f_w2, m_g_final, v_w_in, v_g_mix, v_a_gq, v_a_gkv, v_a_wuq, v_a_wukv, v_b_lb, v_b_gout, v_c_sink, v_rel_bias, v_w_br_a, v_w_br_b, v_w_br_c, v_w_out, v_g_x, v_g_mem, v_x_wq, v_x_wkv, v_x_wo, v_g_ffn, v_f_w1, v_f_w3, v_f_w2, v_g_final):
    given = dict(locals())
    w = {n: given[n] for n in WEIGHT_ORDER}
    m = {n: given["m_" + n] for n in WEIGHT_ORDER}
    v = {n: given["v_" + n] for n in WEIGHT_ORDER}
    sh_names = [n for n, _ in SHARDED]
    rep_shapes = [w[n].shape for n in REPLICATED] + [(1,)]
    rep_rows = _pack_rows([int(np.prod(s)) for s in rep_shapes])

    wire = [w[n] if n in ELEMENTWISE_SHARDED else w[n].astype(MXU_DTYPE) for n in sh_names]
    gathered = all_gather_shards(wire)
    full = {n: _shards_to_full(t, ax).astype(F32) for (n, ax), t in zip(SHARDED, gathered)}
    full.update({n: w[n] for n in REPLICATED})

    loss, (grad_full, grad_x) = jax.value_and_grad(_model_loss, argnums=(0, 1))(full, x[0], mem[0], loss_target[0])

    received = all_to_all_blocks([_full_to_shards(grad_full[n], ax).astype(GRAD_WIRE_DTYPE) for n, ax in SHARDED])
    g_sh, d_sh, nm_sh, nv_sh = {}, {}, {}, {}
    for n, got in zip(sh_names, received):
        shp = w[n].shape
        rows = lambda t: t.reshape(-1, shp[-1])
        outs = adamw_rows(got.reshape(N_DEV, -1, shp[-1]), rows(w[n]), rows(m[n]), rows(v[n]), "adamw_" + n)
        g_sh[n], d_sh[n], nm_sh[n], nv_sh[n] = [o.reshape(shp) for o in outs]

    mine = _pack([grad_full[n] for n in REPLICATED] + [loss.reshape(1)], rep_rows)
    everyone = all_gather_small(mine)
    rep_w = [w[n] for n in REPLICATED] + [jnp.zeros((1,), F32)]
    outs = adamw_rows(everyone, _pack(rep_w, rep_rows), _pack([m[n] for n in REPLICATED] + [jnp.zeros((1,), F32)], rep_rows),
                      _pack([v[n] for n in REPLICATED] + [jnp.ones((1,), F32)], rep_rows), "adamw_replicated")
    rep_names = list(REPLICATED) + ["loss"]
    g_rp, d_rp, nm_rp, nv_rp = [dict(zip(rep_names, _unpack(o, rep_shapes))) for o in outs]

    def pick(sharded, replicated, n):
        return sharded[n] if n in sharded else replicated[n]

    return (g_rp["loss"].reshape(()), grad_x[None],
            *[pick(g_sh, g_rp, n) for n in WEIGHT_ORDER], *[pick(d_sh, d_rp, n) for n in WEIGHT_ORDER],
            *[pick(nm_sh, nm_rp, n) for n in WEIGHT_ORDER], *[pick(nv_sh, nv_rp, n) for n in WEIGHT_ORDER])
```

```python
import functools
import math

import jax
import jax.numpy as jnp
import numpy as np
from jax import lax
from jax.experimental import pallas as pl
from jax.experimental.pallas import tpu as pltpu

F32 = jnp.float32
MXU_DTYPE = jnp.bfloat16
ACT_DTYPE = jnp.bfloat16
GRAD_WIRE_DTYPE = jnp.bfloat16

V7X_VMEM_LIMIT_BYTES = 56 * 1024 * 1024
LANES = 128
SUBLANES = 8

N_DEV = 8
D_MODEL = 1024
DEPTH = 2
EPS = 1e-6
TINY = 1e-30
NEG = -1e30

A_HEADS, A_NOPE, A_ROPE, A_V, A_Q_RANK, A_KV_RANK = 8, 64, 32, 64, 384, 256
ROPE_THETA = 10000.0
B_HEADS, B_DK, B_DV, B_CHUNK = 8, 128, 64, 16
C_HEADS, C_KV_HEADS, C_DH, C_WINDOW, C_BLOCK = 8, 2, 64, 128, 128
REL_BUCKETS, REL_MAX_DIST = 32, 128
X_HEADS, X_DH = 4, 256
D_FF = 2816
IN_SPLITS = (A_Q_RANK, A_KV_RANK, A_ROPE, 1024, 1024, 1024, 512, 512, 512, 128, 128, 1024, 1024, 1024)
IN_WIDTH = sum(IN_SPLITS)
KR_PAD = LANES - A_ROPE
IN_SPLITS_PADDED = (A_Q_RANK, A_KV_RANK, LANES, 1024, 1024, 1024, 512, 512, 512, 128, 128, 1024, 1024, 1024)

ADAM_LR, ADAM_B1, ADAM_B2, ADAM_EPS, ADAM_WD, ADAM_STEP = 0.001, 0.9, 0.999, 1e-08, 0.01, 10

SHARDED = (("w_in", 2), ("a_wuq", 2), ("a_wukv", 2), ("b_lb", 2), ("w_br_a", 2), ("w_br_b", 2), ("w_br_c", 2),
           ("w_out", 1), ("x_wq", 1), ("x_wkv", 2), ("x_wo", 1), ("f_w1", 2), ("f_w3", 2), ("f_w2", 1))
ELEMENTWISE_SHARDED = ("b_lb",)
REPLICATED = ("g_mix", "a_gq", "a_gkv", "b_gout", "c_sink", "rel_bias", "g_x", "g_mem", "g_ffn", "g_final")
WEIGHT_ORDER = ("w_in", "g_mix", "a_gq", "a_gkv", "a_wuq", "a_wukv", "b_lb", "b_gout", "c_sink", "rel_bias", "w_br_a",
                "w_br_b", "w_br_c", "w_out", "g_x", "g_mem", "x_wq", "x_wkv", "x_wo", "g_ffn", "f_w1", "f_w3", "f_w2",
                "g_final")
PACK_QUANTUM = SUBLANES * LANES
PACK_ROW_TILE = 512


def _pallas(body, **kw):
    return pl.pallas_call(body, **kw)


def _cparams(*sem):
    return pltpu.CompilerParams(dimension_semantics=sem, vmem_limit_bytes=V7X_VMEM_LIMIT_BYTES)


def _tile(n, target, mult=LANES):
    t = (min(target, n) // mult) * mult
    while t >= mult:
        if n % t == 0:
            return t
        t -= mult
    return n


def _dot(a, b, dims):
    return lax.dot_general(a.astype(MXU_DTYPE), b.astype(MXU_DTYPE), (dims, ((), ())), preferred_element_type=F32)


NN = ((1,), (0,))
NT = ((1,), (1,))
TN = ((0,), (0,))


MM_VMEM_BUDGET_BYTES = 40 * 1024 * 1024
MM_MAX_TILE = 4352
MM_MAX_ROW_TILE = 2048
MM_HBM_BYTES_PER_S = 2.5e12
MM_STEP_S = 0.4e-6
MM_DMA_ROW_OVERHEAD_BYTES = 512.0


def _tile_options(n, cap):
    out = [t for t in range(LANES, min(n, cap) + 1, LANES) if n % t == 0]
    if n <= cap and n not in out:
        out.append(n)
    return out or [n]


@functools.lru_cache(maxsize=None)
def _mm_plan(m, n, k, ta, tb, a_bytes, b_bytes, o_bytes):
    best = None
    for tk in _tile_options(k, MM_MAX_TILE):
        nk = k // tk
        for tn in _tile_options(n, MM_MAX_TILE):
            for tm in _tile_options(m, MM_MAX_ROW_TILE):
                vmem = 2 * (tm * tk * a_bytes + tk * tn * b_bytes + tm * tn * o_bytes) + tm * tn * 4
                vmem += (tm * tk * 2 if a_bytes == 4 else 0) + (tk * tn * 2 if b_bytes == 4 else 0)
                if vmem > MM_VMEM_BUDGET_BYTES:
                    continue

                def eff(elems, nbytes):
                    return (elems * nbytes) / (elems * nbytes + MM_DMA_ROW_OVERHEAD_BYTES)

                ea, eb, eo = eff(tm if ta else tk, a_bytes), eff(tk if tb else tn, b_bytes), eff(tn, o_bytes)
                for order in ("mn", "nm"):
                    if nk == 1 and order == "nm":
                        a_tr, b_tr = m * k * a_bytes * (n // tn), k * n * b_bytes
                    elif nk == 1:
                        a_tr, b_tr = m * k * a_bytes, k * n * b_bytes * (m // tm)
                    else:
                        a_tr, b_tr = m * k * a_bytes * (n // tn), k * n * b_bytes * (m // tm)
                    steps = (m // tm) * (n // tn) * nk
                    cost = (a_tr / ea + b_tr / eb + m * n * o_bytes / eo) / MM_HBM_BYTES_PER_S + steps * MM_STEP_S
                    if best is None or cost < best[0]:
                        best = (cost, tm, tn, tk, order)
    assert best is not None, (m, n, k)
    return best[1:]


def _mm(a, b, ta=False, tb=False, out_dtype=F32, name="mm", res=None):
    m, k = (a.shape[1], a.shape[0]) if ta else a.shape
    kb, n = (b.shape[1], b.shape[0]) if tb else b.shape
    assert k == kb, (a.shape, b.shape, ta, tb)
    tm, tn, tk, order = _mm_plan(m, n, k, ta, tb, a.dtype.itemsize, b.dtype.itemsize, jnp.dtype(out_dtype).itemsize)
    nk = k // tk
    dims = ((0 if ta else 1,), (1 if tb else 0,))
    out_shape = jax.ShapeDtypeStruct((m, n), out_dtype)

    assert res is None or nk == 1, (name, k, tk)
    if nk == 1:
        def body(a_ref, b_ref, *rest):
            acc = _dot(a_ref[...], b_ref[...], dims)
            if res is not None:
                acc = rest[0][...] + acc
            rest[-1][...] = acc.astype(rest[-1].dtype)

        if order == "nm":
            mi, ni = (lambda j, i: i), (lambda j, i: j)
            grid = (n // tn, m // tm)
        else:
            mi, ni = (lambda i, j: i), (lambda i, j: j)
            grid = (m // tm, n // tn)
        a_spec = pl.BlockSpec((tk, tm), lambda p, q: (0, mi(p, q))) if ta else pl.BlockSpec((tm, tk), lambda p, q: (mi(p, q), 0))
        b_spec = pl.BlockSpec((tn, tk), lambda p, q: (ni(p, q), 0)) if tb else pl.BlockSpec((tk, tn), lambda p, q: (0, ni(p, q)))
        o_spec = pl.BlockSpec((tm, tn), lambda p, q: (mi(p, q), ni(p, q)))
        extra = [] if res is None else [res]
        return _pallas(body, name=name, grid=grid, in_specs=[a_spec, b_spec] + [o_spec] * len(extra), out_specs=o_spec,
                       out_shape=out_shape, compiler_params=_cparams("parallel", "parallel"))(a, b, *extra)

    direct = jnp.dtype(out_dtype) == jnp.dtype(F32)

    def body(a_ref, b_ref, o_ref, *scratch):
        acc_ref = o_ref if direct else scratch[0]
        kk = pl.program_id(2)

        @pl.when(kk == 0)
        def _():
            acc_ref[...] = jnp.zeros_like(acc_ref)

        acc_ref[...] += _dot(a_ref[...], b_ref[...], dims)

        if not direct:
            @pl.when(kk == nk - 1)
            def _():
                o_ref[...] = acc_ref[...].astype(o_ref.dtype)

    a_spec = pl.BlockSpec((tk, tm), lambda i, j, kk: (kk, i)) if ta else pl.BlockSpec((tm, tk), lambda i, j, kk: (i, kk))
    b_spec = pl.BlockSpec((tn, tk), lambda i, j, kk: (j, kk)) if tb else pl.BlockSpec((tk, tn), lambda i, j, kk: (kk, j))
    return _pallas(
        body, name=name, grid=(m // tm, n // tn, nk), in_specs=[a_spec, b_spec],
        out_specs=pl.BlockSpec((tm, tn), lambda i, j, kk: (i, j)), out_shape=out_shape,
        scratch_shapes=[] if direct else [pltpu.VMEM((tm, tn), F32)],
        compiler_params=_cparams("parallel", "parallel", "arbitrary"),
    )(a, b)


def linear(a, w, out_dtype=F32, name="lin"):
    @jax.custom_vjp
    def f(a, w):
        return _mm(a.astype(ACT_DTYPE), w.astype(MXU_DTYPE), out_dtype=out_dtype, name=name + "_fwd")

    def fwd(a, w):
        ab, wb = a.astype(ACT_DTYPE), w.astype(MXU_DTYPE)
        return _mm(ab, wb, out_dtype=out_dtype, name=name + "_fwd"), (ab, wb, jnp.zeros((0,), a.dtype))

    def bwd(res, g):
        ab, wb, like_a = res
        gb = g.astype(ACT_DTYPE)
        da = _mm(gb, wb, tb=True, out_dtype=like_a.dtype, name=name + "_dx")
        dw = _mm(ab, gb, ta=True, out_dtype=F32, name=name + "_dw")
        return da, dw

    f.defvjp(fwd, bwd)
    return f(a, w)


def linear_res(x, a, w, name="lin", a_transposed=False):
    @jax.custom_vjp
    def f(x, a, w):
        return _mm(a.astype(ACT_DTYPE), w.astype(MXU_DTYPE), ta=a_transposed, name=name + "_fwd", res=x)

    def fwd(x, a, w):
        ab, wb = a.astype(ACT_DTYPE), w.astype(MXU_DTYPE)
        return _mm(ab, wb, ta=a_transposed, name=name + "_fwd", res=x), (ab, wb, jnp.zeros((0,), a.dtype))

    def bwd(res, g):
        ab, wb, like_a = res
        gb = g.astype(ACT_DTYPE)
        if a_transposed:
            da = _mm(wb, gb, tb=True, out_dtype=like_a.dtype, name=name + "_dx")
        else:
            da = _mm(gb, wb, tb=True, out_dtype=like_a.dtype, name=name + "_dx")
        dw = _mm(ab, gb, ta=not a_transposed, out_dtype=F32, name=name + "_dw")
        return g, da, dw

    f.defvjp(fwd, bwd)
    return f(x, a, w)


FFN_ROW_TILE = 512
FFN_COL_TILE = 1408


def _sigmoid(a):
    return 1.0 / (1.0 + jnp.exp(-a))


def swiglu_ffn(x, h, w1, w3, w2, name="ffn"):
    m, d = h.shape
    f_dim = w1.shape[1]
    tm, tn = _tile(m, FFN_ROW_TILE), _tile(f_dim, FFN_COL_TILE)

    def up_body(h_ref, w1_ref, w3_ref, t_ref, a_ref, b_ref):
        hv = h_ref[...]
        a = _dot(hv, w1_ref[...], NN)
        b = _dot(hv, w3_ref[...], NN)
        a_ref[...] = a.astype(a_ref.dtype)
        b_ref[...] = b.astype(b_ref.dtype)
        t_ref[...] = (a * _sigmoid(a) * b).astype(t_ref.dtype)

    def dt_body(g_ref, w2_ref, a_ref, b_ref, da_ref, db_ref):
        dt = _dot(g_ref[...], w2_ref[...], NT)
        a, b = a_ref[...].astype(F32), b_ref[...].astype(F32)
        sg = _sigmoid(a)
        da_ref[...] = (dt * b * (sg * (1.0 + a * (1.0 - sg)))).astype(da_ref.dtype)
        db_ref[...] = (dt * (a * sg)).astype(db_ref.dtype)

    row = pl.BlockSpec((tm, d), lambda j, i: (i, 0))
    w_up = pl.BlockSpec((d, tn), lambda j, i: (0, j))
    w_dn = pl.BlockSpec((tn, d), lambda j, i: (j, 0))
    tile = pl.BlockSpec((tm, tn), lambda j, i: (i, j))
    grid = (f_dim // tn, m // tm)

    def run_up(hb, w1b, w3b):
        return _pallas(up_body, name=name + "_up", grid=grid, in_specs=[row, w_up, w_up], out_specs=[tile, tile, tile],
                       out_shape=[jax.ShapeDtypeStruct((m, f_dim), ACT_DTYPE)] * 3,
                       compiler_params=_cparams("parallel", "parallel"))(hb, w1b, w3b)

    def forward(x, h, w1, w3, w2):
        hb = h.astype(ACT_DTYPE)
        w1b, w3b, w2b = w1.astype(MXU_DTYPE), w3.astype(MXU_DTYPE), w2.astype(MXU_DTYPE)
        t, a, b = run_up(hb, w1b, w3b)
        return _mm(t, w2b, name=name + "_down", res=x), (hb, w1b, w3b, w2b, t, a, b, jnp.zeros((0,), h.dtype))

    @jax.custom_vjp
    def f(x, h, w1, w3, w2):
        return forward(x, h, w1, w3, w2)[0]

    def bwd(res, g):
        hb, w1b, w3b, w2b, t, a, b, like_h = res
        gb = g.astype(ACT_DTYPE)
        da, db = _pallas(dt_body, name=name + "_dt", grid=grid, in_specs=[row, w_dn, tile, tile], out_specs=[tile, tile],
                         out_shape=[jax.ShapeDtypeStruct((m, f_dim), ACT_DTYPE)] * 2,
                         compiler_params=_cparams("parallel", "parallel"))(gb, w2b, a, b)
        dw2 = _mm(t, gb, ta=True, name=name + "_dw2")
        def dh_body(da_ref, db_ref, w1_ref, w3_ref, o_ref):
            o_ref[...] = (_dot(da_ref[...], w1_ref[...], NT) + _dot(db_ref[...], w3_ref[...], NT)).astype(o_ref.dtype)

        wide = pl.BlockSpec((tm, f_dim), lambda i: (i, 0))
        w_all = pl.BlockSpec((d, f_dim), lambda i: (0, 0))
        dh = _pallas(dh_body, name=name + "_dx", grid=(m // tm,), in_specs=[wide, wide, w_all, w_all],
                     out_specs=pl.BlockSpec((tm, d), lambda i: (i, 0)), out_shape=jax.ShapeDtypeStruct((m, d), like_h.dtype),
                     compiler_params=_cparams("parallel"))(da, db, w1b, w3b)
        dw1 = _mm(hb, da, ta=True, name=name + "_dw1")
        dw3 = _mm(hb, db, ta=True, name=name + "_dw3")
        return g, dh, dw1, dw3, dw2

    f.defvjp(lambda *args: forward(*args), bwd)
    return f(x, h, w1, w3, w2)


MERGE_ROW_TILE = 512


def gated_merge_out(x, ya_t, yb, yc, ga, gb, gc, wa, wb, wc, wo, name="merge"):
    s, d = x.shape
    e = yb.shape[1]
    tm = _tile(s, MERGE_ROW_TILE)

    def branches(ya_ref, yb_ref, yc_ref, wa_ref, wb_ref, wc_ref):
        return (_dot(ya_ref[...], wa_ref[...], TN), _dot(yb_ref[...], wb_ref[...], NN), _dot(yc_ref[...], wc_ref[...], NN))

    def fwd_body(x_ref, ya_ref, yb_ref, yc_ref, ga_ref, gb_ref, gc_ref, wa_ref, wb_ref, wc_ref, wo_ref, o_ref, m_ref):
        pa, pb, pc = branches(ya_ref, yb_ref, yc_ref, wa_ref, wb_ref, wc_ref)
        merged = _sigmoid(ga_ref[...]) * pa + _sigmoid(gb_ref[...]) * pb + _sigmoid(gc_ref[...]) * pc
        mb = merged.astype(m_ref.dtype)
        m_ref[...] = mb
        o_ref[...] = x_ref[...] + _dot(mb, wo_ref[...], NN)

    def bwd_body(g_ref, ya_ref, yb_ref, yc_ref, ga_ref, gb_ref, gc_ref, wa_ref, wb_ref, wc_ref, wo_ref,
                 dga_ref, dgb_ref, dgc_ref, dpa_ref, dpb_ref, dpc_ref):
        dm = _dot(g_ref[...], wo_ref[...], NT)
        ps = branches(ya_ref, yb_ref, yc_ref, wa_ref, wb_ref, wc_ref)
        for p_i, gate_ref, dg_ref, dp_ref in zip(ps, (ga_ref, gb_ref, gc_ref), (dga_ref, dgb_ref, dgc_ref),
                                                 (dpa_ref, dpb_ref, dpc_ref)):
            sg = _sigmoid(gate_ref[...])
            dg_ref[...] = dm * p_i * (sg * (1.0 - sg))
            dp_ref[...] = (dm * sg).astype(dp_ref.dtype)

    rows = lambda width: pl.BlockSpec((tm, width), lambda i: (i, 0))
    cols_t = pl.BlockSpec((e, tm), lambda i: (0, i))
    whole = lambda r, c: pl.BlockSpec((r, c), lambda i: (0, 0))
    in_common = [cols_t, rows(e), rows(e), rows(d), rows(d), rows(d), whole(e, d), whole(e, d), whole(e, d), whole(d, d)]

    def forward(x, ya_t, yb, yc, ga, gb, gc, wa, wb, wc, wo):
        cast = lambda t: t.astype(ACT_DTYPE)
        ops = (cast(ya_t), cast(yb), cast(yc), ga, gb, gc, cast(wa), cast(wb), cast(wc), cast(wo))
        out, merged = _pallas(
            fwd_body, name=name + "_fwd", grid=(s // tm,), in_specs=[rows(d)] + in_common, out_specs=[rows(d), rows(d)],
            out_shape=[jax.ShapeDtypeStruct((s, d), F32), jax.ShapeDtypeStruct((s, d), ACT_DTYPE)],
            compiler_params=_cparams("parallel"))(x, *ops)
        like = tuple(jnp.zeros((0,), t.dtype) for t in (ya_t, yb, yc))
        return out, (ops, merged, like)

    @jax.custom_vjp
    def f(*args):
        return forward(*args)[0]

    def bwd(res, g):
        ops, merged, like = res
        ya_b, yb_b, yc_b, ga, gb, gc, wa_b, wb_b, wc_b, wo_b = ops
        gbf = g.astype(ACT_DTYPE)
        gate_ct = jax.ShapeDtypeStruct((s, d), F32)
        branch_ct = jax.ShapeDtypeStruct((s, d), ACT_DTYPE)
        dga, dgb, dgc, dpa, dpb, dpc = _pallas(
            bwd_body, name=name + "_bwd", grid=(s // tm,), in_specs=[rows(d)] + in_common, out_specs=[rows(d)] * 6,
            out_shape=[gate_ct] * 3 + [branch_ct] * 3, compiler_params=_cparams("parallel"))(gbf, *ops)
        dya_t = _mm(wa_b, dpa, tb=True, out_dtype=like[0].dtype, name=name + "_dya")
        dyb = _mm(dpb, wb_b, tb=True, out_dtype=like[1].dtype, name=name + "_dyb")
        dyc = _mm(dpc, wc_b, tb=True, out_dtype=like[2].dtype, name=name + "_dyc")
        dwa = _mm(ya_b, dpa, name=name + "_dwa")
        dwb = _mm(yb_b, dpb, ta=True, name=name + "_dwb")
        dwc = _mm(yc_b, dpc, ta=True, name=name + "_dwc")
        dwo = _mm(merged, gbf, ta=True, name=name + "_dwo")
        return g, dya_t, dyb, dyc, dga, dgb, dgc, dwa, dwb, dwc, dwo

    f.defvjp(lambda *args: forward(*args), bwd)
    return f(x, ya_t, yb, yc, ga, gb, gc, wa, wb, wc, wo)


def _row_tile(rows, width):
    return _tile(rows, max(SUBLANES, (512 * 1024) // width), 16)


def rmsnorm(x, g, out_dtype=F32, name="rms"):
    rows, d = x.shape
    tr = _row_tile(rows, d)
    n_steps = rows // tr

    def fwd_body(x_ref, g_ref, o_ref):
        xv = x_ref[...].astype(F32)
        r = lax.rsqrt(jnp.mean(xv * xv, axis=-1, keepdims=True) + EPS)
        o_ref[...] = (xv * r * g_ref[...]).astype(o_ref.dtype)

    def bwd_body(x_ref, g_ref, dy_ref, dx_ref, dg_ref):
        xv = x_ref[...].astype(F32)
        dy = dy_ref[...].astype(F32)
        r = lax.rsqrt(jnp.mean(xv * xv, axis=-1, keepdims=True) + EPS)
        xh = xv * r
        dxh = dy * g_ref[...]
        dx_ref[...] = (r * (dxh - xh * jnp.mean(dxh * xh, axis=-1, keepdims=True))).astype(dx_ref.dtype)

        @pl.when(pl.program_id(0) == 0)
        def _():
            dg_ref[...] = jnp.zeros_like(dg_ref)

        dg_ref[...] += jnp.sum(dy * xh, axis=0, keepdims=True)

    row_spec = pl.BlockSpec((tr, d), lambda i: (i, 0))
    vec_spec = pl.BlockSpec((1, d), lambda i: (0, 0))

    def run_fwd(x, g):
        return _pallas(fwd_body, name=name + "_fwd", grid=(n_steps,), in_specs=[row_spec, vec_spec], out_specs=row_spec,
                       out_shape=jax.ShapeDtypeStruct((rows, d), out_dtype), compiler_params=_cparams("parallel"))(
            x, g.reshape(1, d).astype(F32))

    @jax.custom_vjp
    def f(x, g):
        return run_fwd(x, g)

    def fwd(x, g):
        return run_fwd(x, g), (x, g)

    def bwd(res, dy):
        x, g = res
        dx, dg = _pallas(
            bwd_body, name=name + "_bwd", grid=(n_steps,), in_specs=[row_spec, vec_spec, row_spec],
            out_specs=[row_spec, vec_spec],
            out_shape=[jax.ShapeDtypeStruct((rows, d), x.dtype), jax.ShapeDtypeStruct((1, d), F32)],
            compiler_params=_cparams("arbitrary"))(x, g.reshape(1, d).astype(F32), dy)
        return dx, dg.reshape(g.shape).astype(g.dtype)

    f.defvjp(fwd, bwd)
    return f(x, g)


def rmsnorm_keep(x, g, out_dtype=F32, name="rms"):
    rows, d = x.shape
    tr = _row_tile(rows, d)
    n_steps = rows // tr

    def bwd_body(x_ref, g_ref, dy_ref, dkeep_ref, dx_ref, dg_ref):
        xv = x_ref[...]
        dy = dy_ref[...].astype(F32)
        r = lax.rsqrt(jnp.mean(xv * xv, axis=-1, keepdims=True) + EPS)
        xh = xv * r
        dxh = dy * g_ref[...]
        dx_ref[...] = dkeep_ref[...] + r * (dxh - xh * jnp.mean(dxh * xh, axis=-1, keepdims=True))

        @pl.when(pl.program_id(0) == 0)
        def _():
            dg_ref[...] = jnp.zeros_like(dg_ref)

        dg_ref[...] += jnp.sum(dy * xh, axis=0, keepdims=True)

    row_spec = pl.BlockSpec((tr, d), lambda i: (i, 0))
    vec_spec = pl.BlockSpec((1, d), lambda i: (0, 0))

    @jax.custom_vjp
    def f(x, g):
        return rmsnorm(x, g, out_dtype, name), x

    def fwd(x, g):
        return (rmsnorm(x, g, out_dtype, name), x), (x, g)

    def bwd(res, cts):
        x, g = res
        dy, dkeep = cts
        dx, dg = _pallas(
            bwd_body, name=name + "_bwd", grid=(n_steps,), in_specs=[row_spec, vec_spec, row_spec, row_spec],
            out_specs=[row_spec, vec_spec],
            out_shape=[jax.ShapeDtypeStruct((rows, d), F32), jax.ShapeDtypeStruct((1, d), F32)],
            compiler_params=_cparams("arbitrary"))(x, g.reshape(1, d).astype(F32), dy, dkeep)
        return dx, dg.reshape(g.shape).astype(g.dtype)

    f.defvjp(fwd, bwd)
    return f(x, g)


def _rowdot(a, b, name):
    h, s, d = a.shape
    ts = _tile(s, 2048)

    def body(a_ref, b_ref, o_ref):
        o_ref[...] = jnp.sum(a_ref[...].astype(F32) * b_ref[...].astype(F32), axis=-1, keepdims=True)

    spec = pl.BlockSpec((None, ts, d), lambda hh, i: (hh, i, 0))
    return _pallas(body, name=name, grid=(h, s // ts), in_specs=[spec, spec],
                   out_specs=pl.BlockSpec((None, ts, 1), lambda hh, i: (hh, i, 0)),
                   out_shape=jax.ShapeDtypeStruct((h, s, 1), F32), compiler_params=_cparams("parallel", "parallel"))(a, b)


LOG2E = 1.4426950408889634


def mla_attention(q, k, v, scale, name="mla"):
    s, h, d = q.shape
    sk, dv = k.shape[0], v.shape[2]
    tq, tk = _tile(s, MLA_BWD_TQ), _tile(sk, MLA_TK)
    nq, nk = s // tq, sk // tk
    tqf, tkf = _tile(s, MLA_FWD_TQ), _tile(sk, MLA_FWD_TK)
    nkf = sk // tkf
    ones_rows = 16
    c = scale * LOG2E

    def fwd_body(qt_ref, k_ref, vt_ref, ot_ref, lse_ref, m_ref, acc_ref):
        j = pl.program_id(2)

        @pl.when(j == 0)
        def _():
            m_ref[...] = jnp.full_like(m_ref, NEG)
            acc_ref[...] = jnp.zeros_like(acc_ref)

        st = _dot(k_ref[...], qt_ref[...], NN)
        m_prev = m_ref[...]
        m_new = jnp.maximum(m_prev, jnp.max(st, axis=0, keepdims=True) * c)
        pt = jnp.exp2(st * c - m_new)
        acc_ref[...] = jnp.exp2(m_prev - m_new) * acc_ref[...] + _dot(vt_ref[...], pt, NN)
        m_ref[...] = m_new

        @pl.when(j == nkf - 1)
        def _():
            l = acc_ref[dv:dv + 1, :]
            ot_ref[...] = (acc_ref[:dv, :] / l).astype(ot_ref.dtype)
            lse_ref[...] = m_ref[...] + jnp.log2(l)

    def delta_body(ot_ref, dot_ref, o_ref):
        o_ref[...] = jnp.sum(ot_ref[...].astype(F32) * dot_ref[...].astype(F32), axis=0, keepdims=True)

    def bwd_body(qt_ref, k_ref, kt_ref, v_ref, dot_ref, lse_ref, dl_ref, dqt_ref, dk_hbm, dv_hbm, dq_acc, dk_acc, dv_acc):
        hh, i, j = pl.program_id(0), pl.program_id(1), pl.program_id(2)

        @pl.when(j == 0)
        def _():
            dq_acc[...] = jnp.zeros_like(dq_acc)

        @pl.when(i == 0)
        def _():
            dk_acc[j] = jnp.zeros((d, tk), F32)
            dv_acc[j] = jnp.zeros((dv, tk), F32)

        qt, dot_ = qt_ref[...], dot_ref[...]
        pt = jnp.exp2(_dot(k_ref[...], qt, NN) * c - lse_ref[...])
        dst = (pt * (_dot(v_ref[...], dot_, NN) - dl_ref[...])).astype(MXU_DTYPE)
        dv_acc[j] += _dot(dot_, pt, NT)
        dk_acc[j] += _dot(qt, dst, NT)
        dq_acc[...] += _dot(kt_ref[...], dst, NN)

        @pl.when(j == nk - 1)
        def _():
            dqt_ref[...] = dq_acc[...] * scale

        @pl.when(i == nq - 1)
        def _():
            dk_acc[j] = dk_acc[j] * scale
            pltpu.sync_copy(dk_acc.at[j], dk_hbm.at[hh, j])
            pltpu.sync_copy(dv_acc.at[j], dv_hbm.at[hh, j])

    def qt_spec(width):
        return pl.BlockSpec((None, width, tq), lambda hh, i, j: (hh, 0, i))

    def kt_spec(width):
        return pl.BlockSpec((None, width, tk), lambda hh, i, j: (hh, 0, j))

    def k_spec(width):
        return pl.BlockSpec((None, tk, width), lambda hh, i, j: (hh, j, 0))

    def layouts(q, k, v):
        cast = lambda t: t.astype(ACT_DTYPE)
        return (cast(jnp.transpose(q, (1, 2, 0))), cast(jnp.transpose(k, (1, 0, 2))), cast(jnp.transpose(k, (1, 2, 0))),
                cast(jnp.transpose(v, (1, 0, 2))), cast(jnp.transpose(v, (1, 2, 0))))

    def run_fwd(qt, kh, vt):
        vt_ones = jnp.concatenate([vt, jnp.ones((h, ones_rows, sk), vt.dtype)], axis=1)

        def qf_spec(width):
            return pl.BlockSpec((None, width, tqf), lambda hh, i, j: (hh, 0, i))

        kf_spec = pl.BlockSpec((None, tkf, d), lambda hh, i, j: (hh, j, 0))
        vf_spec = pl.BlockSpec((None, dv + ones_rows, tkf), lambda hh, i, j: (hh, 0, j))
        return _pallas(
            fwd_body, name=name + "_fwd", grid=(h, s // tqf, nkf), in_specs=[qf_spec(d), kf_spec, vf_spec],
            out_specs=[qf_spec(dv), qf_spec(1)],
            out_shape=[jax.ShapeDtypeStruct((h, dv, s), ACT_DTYPE), jax.ShapeDtypeStruct((h, 1, s), F32)],
            scratch_shapes=[pltpu.VMEM((1, tqf), F32), pltpu.VMEM((dv + ones_rows, tqf), F32)],
            compiler_params=_cparams("parallel", "parallel", "arbitrary"))(qt, kh, vt_ones)

    @jax.custom_vjp
    def f(q, k, v):
        qt, kh, _, _, vt = layouts(q, k, v)
        return run_fwd(qt, kh, vt)[0].reshape(h * dv, s)

    def fwd(q, k, v):
        qt, kh, kt, vh, vt = layouts(q, k, v)
        ot, lse = run_fwd(qt, kh, vt)
        return ot.reshape(h * dv, s), (qt, kh, kt, vh, ot, lse)

    def bwd(res, dy):
        qt, kh, kt, vh, ot, lse = res
        dot_ = dy.reshape(h, dv, s)
        ts = _tile(s, 2048)
        col = pl.BlockSpec((None, dv, ts), lambda hh, i: (hh, 0, i))
        delta = _pallas(delta_body, name=name + "_delta", grid=(h, s // ts), in_specs=[col, col],
                        out_specs=pl.BlockSpec((None, 1, ts), lambda hh, i: (hh, 0, i)),
                        out_shape=jax.ShapeDtypeStruct((h, 1, s), F32), compiler_params=_cparams("parallel", "parallel"))(ot, dot_)
        any_spec = pl.BlockSpec(memory_space=pl.ANY)
        dqt, dkt, dvt = _pallas(
            bwd_body, name=name + "_bwd", grid=(h, nq, nk),
            in_specs=[qt_spec(d), k_spec(d), kt_spec(d), k_spec(dv), qt_spec(dv), qt_spec(1), qt_spec(1)],
            out_specs=[qt_spec(d), any_spec, any_spec],
            out_shape=[jax.ShapeDtypeStruct((h, d, s), F32), jax.ShapeDtypeStruct((h, nk, d, tk), F32),
                       jax.ShapeDtypeStruct((h, nk, dv, tk), F32)],
            scratch_shapes=[pltpu.VMEM((d, tq), F32), pltpu.VMEM((nk, d, tk), F32), pltpu.VMEM((nk, dv, tk), F32)],
            compiler_params=_cparams("parallel", "arbitrary", "arbitrary"))(qt, kh, kt, vh, dot_, lse, delta)
        to_tokens = lambda t: jnp.transpose(t, (1, 3, 0, 2)).reshape(sk, h, t.shape[2])
        return jnp.transpose(dqt, (2, 0, 1)), to_tokens(dkt), to_tokens(dvt)

    f.defvjp(fwd, bwd)
    return f(q, k, v)


WATTN_TQ = 2 * C_BLOCK
WATTN_KW = WATTN_TQ + 2 * C_BLOCK


def window_attention(q, k, v, bias, sink, name="wattn"):
    hq, s, dh = q.shape
    g = hq // C_KV_HEADS
    tq, kw, half = WATTN_TQ, WATTN_KW, WATTN_KW // 2
    nt = s // tq
    scale = dh ** -0.5
    sink_b = jnp.broadcast_to(sink.astype(F32).reshape(hq, 1, 1), (hq, 1, LANES))
    neg = jnp.full((hq, C_BLOCK, C_BLOCK), NEG, F32)
    tile = jnp.concatenate(
        [jnp.concatenate([bias[:, cb - rb] if 0 <= cb - rb <= 2 else neg for cb in range(kw // C_BLOCK)], axis=2)
         for rb in range(tq // C_BLOCK)], axis=1)

    def key_bias(i):
        pos = lax.broadcasted_iota(jnp.int32, (1, kw), 1) + i * tq - C_BLOCK
        return jnp.where(jnp.logical_and(pos >= 0, pos < s), 0.0, NEG)

    def both(a_ref, b_ref):
        return jnp.concatenate([a_ref[...], b_ref[...]], axis=0)

    def fwd_body(q_ref, ka_ref, kb_ref, va_ref, vb_ref, b_ref, sk_ref, o_ref, lse_ref):
        kb_ = key_bias(pl.program_id(1))
        k_all, v_all = both(ka_ref, kb_ref), both(va_ref, vb_ref)
        for hh in range(g):
            sc = _dot(q_ref[hh], k_all, NT) * scale + b_ref[hh] + kb_
            snk = sk_ref[hh][:, :1]
            m = jnp.maximum(jnp.max(sc, axis=-1, keepdims=True), snk)
            p = jnp.exp(sc - m)
            l = jnp.sum(p, axis=-1, keepdims=True) + jnp.exp(snk - m)
            o_ref[hh] = (_dot(p, v_all, NN) / l).astype(o_ref.dtype)
            lse_ref[hh] = m + jnp.log(l)

    def bwd_body(q_ref, ka_ref, kb_ref, va_ref, vb_ref, b_ref, sk_ref, do_ref, lse_ref, dl_ref,
                 dq_ref, db_ref, dsink_ref, dk_hbm, dv_hbm, dk_acc, dv_acc):
        kv, i = pl.program_id(0), pl.program_id(1)

        @pl.when(i == 0)
        def _():
            dk_acc[...] = jnp.zeros_like(dk_acc)
            dv_acc[...] = jnp.zeros_like(dv_acc)
            db_ref[...] = jnp.zeros_like(db_ref)
            dsink_ref[...] = jnp.zeros_like(dsink_ref)

        kb_ = key_bias(i)
        k_all, v_all = both(ka_ref, kb_ref), both(va_ref, vb_ref)
        dk_t = jnp.zeros((kw, dh), F32)
        dv_t = jnp.zeros((kw, dh), F32)
        for hh in range(g):
            lse, dl, do = lse_ref[hh], dl_ref[hh], do_ref[hh]
            p = jnp.exp(_dot(q_ref[hh], k_all, NT) * scale + b_ref[hh] + kb_ - lse)
            ds = p * (_dot(do, v_all, NT) - dl)
            db_ref[hh] += ds
            total = jnp.broadcast_to(-jnp.sum(jnp.exp(sk_ref[hh][:, :1] - lse) * dl, axis=0, keepdims=True), (1, LANES))
            dsink_ref[hh] += jnp.where(lax.broadcasted_iota(jnp.int32, (1, LANES), 1) == 0, total, 0.0)
            dsb = (ds * scale).astype(MXU_DTYPE)
            dq_ref[hh] = _dot(dsb, k_all, NN).astype(dq_ref.dtype)
            dk_t += _dot(dsb, q_ref[hh], TN)
            dv_t += _dot(p, do, TN)
        rows = pl.ds(pl.multiple_of(i * tq, tq), kw)
        dk_acc[rows, :] += dk_t
        dv_acc[rows, :] += dv_t

        @pl.when(i == nt - 1)
        def _():
            pltpu.sync_copy(dk_acc, dk_hbm.at[kv])
            pltpu.sync_copy(dv_acc, dv_hbm.at[kv])

    def q_spec(width):
        return pl.BlockSpec((g, tq, width), lambda kv, i: (kv, i, 0))

    ka_spec = pl.BlockSpec((None, half, dh), lambda kv, i: (kv, i, 0))
    kb_spec = pl.BlockSpec((None, half, dh), lambda kv, i: (kv, i + 1, 0))
    b_spec = pl.BlockSpec((g, tq, kw), lambda kv, i: (kv, 0, 0))
    sk_spec = pl.BlockSpec((g, 1, LANES), lambda kv, i: (kv, 0, 0))

    def padded(t):
        return jnp.pad(t, ((0, 0), (C_BLOCK, C_BLOCK), (0, 0)))

    def run_fwd(q, kp, vp, tile, sink_b):
        return _pallas(
            fwd_body, name=name + "_fwd", grid=(C_KV_HEADS, nt),
            in_specs=[q_spec(dh), ka_spec, kb_spec, ka_spec, kb_spec, b_spec, sk_spec], out_specs=[q_spec(dh), q_spec(1)],
            out_shape=[jax.ShapeDtypeStruct((hq, s, dh), ACT_DTYPE), jax.ShapeDtypeStruct((hq, s, 1), F32)],
            compiler_params=_cparams("parallel", "parallel"))(q, kp, kp, vp, vp, tile, sink_b)

    @jax.custom_vjp
    def f(q, k, v, tile, sink_b):
        return run_fwd(q, padded(k), padded(v), tile, sink_b)[0]

    def fwd(q, k, v, tile, sink_b):
        kp, vp = padded(k), padded(v)
        o, lse = run_fwd(q, kp, vp, tile, sink_b)
        return o, (q, kp, vp, tile, sink_b, o, lse)

    def bwd(res, do):
        q, kp, vp, tile, sink_b, o, lse = res
        delta = _rowdot(o, do, name + "_delta")
        any_spec = pl.BlockSpec(memory_space=pl.ANY)
        acc = jax.ShapeDtypeStruct((C_KV_HEADS, s + 2 * C_BLOCK, dh), F32)
        dq, dtile, dsink, dkp, dvp = _pallas(
            bwd_body, name=name + "_bwd", grid=(C_KV_HEADS, nt),
            in_specs=[q_spec(dh), ka_spec, kb_spec, ka_spec, kb_spec, b_spec, sk_spec, q_spec(dh), q_spec(1), q_spec(1)],
            out_specs=[q_spec(dh), b_spec, sk_spec, any_spec, any_spec],
            out_shape=[jax.ShapeDtypeStruct((hq, s, dh), q.dtype), jax.ShapeDtypeStruct((hq, tq, kw), F32),
                       jax.ShapeDtypeStruct((hq, 1, LANES), F32), acc, acc],
            scratch_shapes=[pltpu.VMEM((s + 2 * C_BLOCK, dh), F32), pltpu.VMEM((s + 2 * C_BLOCK, dh), F32)],
            compiler_params=_cparams("parallel", "arbitrary"))(q, kp, kp, vp, vp, tile, sink_b, do, lse, delta)
        unpad = lambda t: t[:, C_BLOCK:-C_BLOCK].astype(kp.dtype)
        return dq, unpad(dkp), unpad(dvp), dtile, dsink

    f.defvjp(fwd, bwd)
    return f(q, k, v, tile, sink_b)


HG_PREP_ROWS = 256
HG_GROUP = 8
HG_INTRA_BLOCK = 256
HG_INTER_CHUNKS = 32
MLA_FWD_TQ, MLA_FWD_TK, MLA_BWD_TQ, MLA_TK = 2048, 2048, 2048, 1024


def _hdot(a, b, dims, ones_first=True):
    b16 = jnp.bfloat16
    ones, full = (a, b) if ones_first else (b, a)
    hi = full.astype(b16)
    rest = full - hi.astype(F32)
    mid = rest.astype(b16)
    lo = (rest - mid.astype(F32)).astype(b16)
    ones16 = ones.astype(b16)
    dn = (dims, ((), ()))

    def one(piece):
        lhs, rhs = (ones16, piece) if ones_first else (piece, ones16)
        return lax.dot_general(lhs, rhs, dn, preferred_element_type=F32)

    return one(hi) + one(mid) + one(lo)


HG_OUT_ROWS = 256


def hgrn_out(o, gate, g_out, name="hg_out"):
    s, c = o.shape
    tb = _tile(s, HG_OUT_ROWS)
    g_row = jnp.tile(g_out.astype(F32).reshape(1, B_DV), (1, c // B_DV))

    def head_mean(v):
        r = lax.broadcasted_iota(jnp.int32, (c, c), 0)
        cc = lax.broadcasted_iota(jnp.int32, (c, c), 1)
        same = jnp.where(r // B_DV == cc // B_DV, 1.0, 0.0).astype(F32)
        return _hdot(v, same, NN, ones_first=False) * (1.0 / B_DV)

    def fwd_body(o_ref, gate_ref, g_ref, y_ref):
        ov, gt = o_ref[...], gate_ref[...]
        xh = ov * lax.rsqrt(head_mean(ov * ov) + EPS)
        y_ref[...] = xh * g_ref[...] * (gt * _sigmoid(gt))

    def bwd_body(o_ref, gate_ref, g_ref, dy_ref, do_ref, dgate_ref, dg_ref):
        ov, gt, gv, dy = o_ref[...], gate_ref[...], g_ref[...], dy_ref[...]
        r = lax.rsqrt(head_mean(ov * ov) + EPS)
        xh = ov * r
        sg = _sigmoid(gt)
        dgate_ref[...] = dy * xh * gv * (sg * (1.0 + gt * (1.0 - sg)))
        dn = dy * (gt * sg)
        dxh = dn * gv
        do_ref[...] = r * (dxh - xh * head_mean(dxh * xh))

        @pl.when(pl.program_id(0) == 0)
        def _():
            dg_ref[...] = jnp.zeros_like(dg_ref)

        dg_ref[...] += jnp.sum(dn * xh, axis=0, keepdims=True)

    row = pl.BlockSpec((tb, c), lambda i: (i, 0))
    vec = pl.BlockSpec((1, c), lambda i: (0, 0))
    shape = jax.ShapeDtypeStruct((s, c), F32)

    def run_fwd(o, gate, g_row):
        return _pallas(fwd_body, name=name + "_fwd", grid=(s // tb,), in_specs=[row, row, vec], out_specs=row, out_shape=shape,
                       compiler_params=_cparams("parallel"))(o, gate, g_row)

    @jax.custom_vjp
    def f(o, gate, g_row):
        return run_fwd(o, gate, g_row)

    def fwd(o, gate, g_row):
        return run_fwd(o, gate, g_row), (o, gate, g_row)

    def bwd(res, dy):
        o, gate, g_row = res
        return tuple(_pallas(
            bwd_body, name=name + "_bwd", grid=(s // tb,), in_specs=[row, row, vec, row], out_specs=[row, row, vec],
            out_shape=[shape, shape, jax.ShapeDtypeStruct((1, c), F32)], compiler_params=_cparams("arbitrary"))(o, gate, g_row, dy))

    f.defvjp(fwd, bwd)
    return f(o, gate, g_row)


def hgrn_prep(q, z, lb, reverse, name):
    n_hp, s, tc = q.shape
    tb = _tile(s, HG_PREP_ROWS)
    ncb = tb // B_CHUNK

    def chunk_matrices():
        r = lax.broadcasted_iota(jnp.int32, (tb, tb), 0)
        cc = lax.broadcasted_iota(jnp.int32, (tb, tb), 1)
        same = r // B_CHUNK == cc // B_CHUNK
        tri = (cc >= r) if reverse else (cc <= r)
        cum = jnp.where(jnp.logical_and(same, tri), 1.0, 0.0).astype(F32)
        every = jnp.where(same, 1.0, 0.0).astype(F32)
        pr = lax.broadcasted_iota(jnp.int32, (ncb, tb), 0)
        pc = lax.broadcasted_iota(jnp.int32, (ncb, tb), 1)
        per_chunk = jnp.where(pc // B_CHUNK == pr, 1.0, 0.0).astype(F32)
        return cum, every, per_chunk

    def gates(zv, lbv):
        e = jnp.exp(-jnp.abs(zv))
        big, small = 1.0 / (1.0 + e), e / (1.0 + e)
        sig = jnp.where(zv >= 0, big, small)
        nsig = jnp.where(zv >= 0, small, big)
        f = lbv + (1.0 - lbv) * sig
        return sig, nsig, f, jnp.log(jnp.maximum(f, TINY)), (1.0 - lbv) * nsig

    def fwd_body(q_ref, z_ref, lb_ref, qd_ref, ki_ref, ke_ref, dec_ref):
        cum, every, per_chunk = chunk_matrices()
        _, _, _, lf, key = gates(z_ref[...], lb_ref[...])
        b = _hdot(cum, lf, NN)
        tot = _hdot(every, lf, NN)
        qd_ref[...] = q_ref[...] * jnp.exp(b)
        ki_ref[...] = key * jnp.exp(-b)
        ke_ref[...] = key * jnp.exp(tot - b)
        dec_ref[...] = jnp.exp(_hdot(per_chunk, lf, NN))

    def bwd_body(q_ref, z_ref, lb_ref, dqd_ref, dki_ref, dke_ref, ddec_ref, dq_ref, dz_ref, dlb_ref):
        cum, every, per_chunk = chunk_matrices()
        lbv = lb_ref[...]
        sig, nsig, f, lf, key = gates(z_ref[...], lbv)
        b = _hdot(cum, lf, NN)
        tot = _hdot(every, lf, NN)
        e_b, e_nb, e_tb = jnp.exp(b), jnp.exp(-b), jnp.exp(tot - b)
        dqd, dki, dke = dqd_ref[...], dki_ref[...], dke_ref[...]
        dq_ref[...] = dqd * e_b
        dkey = dki * e_nb + dke * e_tb
        t_end = dke * key * e_tb
        db = dqd * q_ref[...] * e_b - dki * key * e_nb - t_end
        dtot = ddec_ref[...] * jnp.exp(_hdot(per_chunk, lf, NN)) + _hdot(per_chunk, t_end, NN)
        dlf = _hdot(cum, db, TN) + _hdot(per_chunk, dtot, TN)
        df = jnp.where(f > TINY, dlf / f, 0.0)
        one_m_lb = 1.0 - lbv
        dz_ref[...] = (df - dkey) * one_m_lb * sig * nsig
        dlb_part = jnp.sum(df * nsig - dkey * nsig, axis=0, keepdims=True)

        @pl.when(pl.program_id(1) == 0)
        def _():
            dlb_ref[...] = jnp.zeros_like(dlb_ref)

        dlb_ref[...] += dlb_part

    tok = pl.BlockSpec((None, tb, tc), lambda j, i: (j, i, 0))
    vec = pl.BlockSpec((None, 1, tc), lambda j, i: (j, 0, 0))
    chk = pl.BlockSpec((None, ncb, tc), lambda j, i: (j, i, 0))
    grid = (n_hp, s // tb)
    tok_shape = jax.ShapeDtypeStruct((n_hp, s, tc), F32)
    chk_shape = jax.ShapeDtypeStruct((n_hp, s // B_CHUNK, tc), F32)

    def run_fwd(q, z, lb):
        return _pallas(fwd_body, name=name + "_fwd", grid=grid, in_specs=[tok, tok, vec], out_specs=[tok, tok, tok, chk],
                       out_shape=[tok_shape, tok_shape, tok_shape, chk_shape],
                       compiler_params=_cparams("parallel", "parallel"))(q, z, lb)

    @jax.custom_vjp
    def f(q, z, lb):
        return tuple(run_fwd(q, z, lb))

    def fwd(q, z, lb):
        return tuple(run_fwd(q, z, lb)), (q, z, lb)

    def bwd(res, cts):
        q, z, lb = res
        dq, dz, dlb = _pallas(
            bwd_body, name=name + "_bwd", grid=grid, in_specs=[tok, tok, vec, tok, tok, tok, chk], out_specs=[tok, tok, vec],
            out_shape=[tok_shape, tok_shape, jax.ShapeDtypeStruct((n_hp, 1, tc), F32)],
            compiler_params=_cparams("parallel", "arbitrary"))(q, z, lb, *cts)
        return dq, dz, dlb

    f.defvjp(fwd, bwd)
    return f(q, z, lb)


def _pair_cols(ref, hh, width):
    return ref[:, hh * width:(hh + 1) * width]


def hgrn_intra(qd, ki, v, reverse, name):
    s = qd.shape[1]
    tb = _tile(s, HG_INTRA_BLOCK)
    wk, wv = HG_GROUP * B_DK, HG_GROUP * B_DV

    def mask():
        r = lax.broadcasted_iota(jnp.int32, (tb, tb), 0)
        c = lax.broadcasted_iota(jnp.int32, (tb, tb), 1)
        return jnp.logical_and(r // B_CHUNK == c // B_CHUNK, (c >= r) if reverse else (c <= r))

    def fwd_body(q_ref, k_ref, v_ref, o_ref):
        msk = mask()
        for hh in range(HG_GROUP):
            sc = jnp.where(msk, _dot(_pair_cols(q_ref, hh, B_DK), _pair_cols(k_ref, hh, B_DK), NT), 0.0)
            o_ref[:, hh * B_DV:(hh + 1) * B_DV] = _dot(sc, _pair_cols(v_ref, hh, B_DV), NN)

    def bwd_body(q_ref, k_ref, v_ref, do_ref, dq_ref, dk_ref, dv_ref):
        msk = mask()
        for hh in range(HG_GROUP):
            q, k = _pair_cols(q_ref, hh, B_DK), _pair_cols(k_ref, hh, B_DK)
            vv, do = _pair_cols(v_ref, hh, B_DV), _pair_cols(do_ref, hh, B_DV)
            sc = jnp.where(msk, _dot(q, k, NT), 0.0)
            ds = jnp.where(msk, _dot(do, vv, NT), 0.0)
            dq_ref[:, hh * B_DK:(hh + 1) * B_DK] = _dot(ds, k, NN)
            dk_ref[:, hh * B_DK:(hh + 1) * B_DK] = _dot(ds, q, TN)
            dv_ref[:, hh * B_DV:(hh + 1) * B_DV] = _dot(sc, do, TN)

    ks = pl.BlockSpec((None, tb, wk), lambda hp, i: (hp, i, 0))
    vs = pl.BlockSpec((None, tb, wv), lambda hp, i: (hp, i, 0))
    grid = (B_HEADS // HG_GROUP, s // tb)

    def run_fwd(qd, ki, v):
        return _pallas(fwd_body, name=name + "_fwd", grid=grid, in_specs=[ks, ks, vs], out_specs=vs,
                       out_shape=jax.ShapeDtypeStruct(v.shape, F32), compiler_params=_cparams("parallel", "parallel"))(qd, ki, v)

    @jax.custom_vjp
    def f(qd, ki, v):
        return run_fwd(qd, ki, v)

    def fwd(qd, ki, v):
        return run_fwd(qd, ki, v), (qd, ki, v)

    def bwd(res, do):
        qd, ki, v = res
        return tuple(_pallas(
            bwd_body, name=name + "_bwd", grid=grid, in_specs=[ks, ks, vs, vs], out_specs=[ks, ks, vs],
            out_shape=[jax.ShapeDtypeStruct(qd.shape, F32), jax.ShapeDtypeStruct(ki.shape, F32),
                       jax.ShapeDtypeStruct(v.shape, F32)],
            compiler_params=_cparams("parallel", "parallel"))(qd, ki, v, do))

    f.defvjp(fwd, bwd)
    return f(qd, ki, v)


def hgrn_inter(qd, ke, v, dec, reverse, name):
    s = qd.shape[1]
    nc = s // B_CHUNK
    cpb = HG_INTER_CHUNKS if nc % HG_INTER_CHUNKS == 0 else nc
    tb = cpb * B_CHUNK
    nblk = nc // cpb
    wk, wv = HG_GROUP * B_DK, HG_GROUP * B_DV
    n_hp = B_HEADS // HG_GROUP

    def rows(c):
        return pl.ds(c * B_CHUNK, B_CHUNK)

    def kcols(hh):
        return slice(hh * B_DK, (hh + 1) * B_DK)

    def vcols(hh):
        return slice(hh * B_DV, (hh + 1) * B_DV)

    def order(flip):
        return reversed(range(cpb)) if flip else range(cpb)

    def fwd_body(q_ref, k_ref, v_ref, dec_ref, o_ref, st_ref, state):
        @pl.when(pl.program_id(1) == 0)
        def _():
            state[...] = jnp.zeros_like(state)

        for c in order(reverse):
            for hh in range(HG_GROUP):
                st = state[hh]
                st_ref[c, hh] = st
                o_ref[rows(c), vcols(hh)] = _dot(q_ref[rows(c), kcols(hh)], st, NT)
                state[hh] = st * dec_ref[pl.ds(c, 1), kcols(hh)] + _dot(v_ref[rows(c), vcols(hh)], k_ref[rows(c), kcols(hh)], TN)

    def bwd_body(q_ref, k_ref, v_ref, dec_ref, st_ref, do_ref, dq_ref, dk_ref, dv_ref, ddec_ref, dstate):
        @pl.when(pl.program_id(1) == 0)
        def _():
            dstate[...] = jnp.zeros_like(dstate)

        for c in order(not reverse):
            for hh in range(HG_GROUP):
                dst = dstate[hh]
                st = st_ref[c, hh]
                do_c = do_ref[rows(c), vcols(hh)]
                dk_ref[rows(c), kcols(hh)] = _dot(v_ref[rows(c), vcols(hh)], dst, NN)
                dv_ref[rows(c), vcols(hh)] = _dot(k_ref[rows(c), kcols(hh)], dst, NT)
                ddec_ref[pl.ds(c, 1), kcols(hh)] = jnp.sum(dst * st, axis=0, keepdims=True)
                dq_ref[rows(c), kcols(hh)] = _dot(do_c, st, NN)
                dstate[hh] = dst * dec_ref[pl.ds(c, 1), kcols(hh)] + _dot(do_c, q_ref[rows(c), kcols(hh)], TN)

    def specs(flip):
        blk = (lambda i: nblk - 1 - i) if flip else (lambda i: i)
        tok_k = pl.BlockSpec((None, tb, wk), lambda hp, i: (hp, blk(i), 0))
        tok_v = pl.BlockSpec((None, tb, wv), lambda hp, i: (hp, blk(i), 0))
        chk = pl.BlockSpec((None, cpb, wk), lambda hp, i: (hp, blk(i), 0))
        sts = pl.BlockSpec((None, cpb, HG_GROUP, B_DV, B_DK), lambda hp, i: (hp, blk(i), 0, 0, 0))
        return tok_k, tok_v, chk, sts

    scratch = [pltpu.VMEM((HG_GROUP, B_DV, B_DK), F32)]

    def run_fwd(qd, ke, v, dec):
        tok_k, tok_v, chk, sts = specs(reverse)
        return _pallas(
            fwd_body, name=name + "_fwd", grid=(n_hp, nblk), in_specs=[tok_k, tok_k, tok_v, chk], out_specs=[tok_v, sts],
            out_shape=[jax.ShapeDtypeStruct(v.shape, F32), jax.ShapeDtypeStruct((n_hp, nc, HG_GROUP, B_DV, B_DK), F32)],
            scratch_shapes=scratch, compiler_params=_cparams("parallel", "arbitrary"))(qd, ke, v, dec)

    @jax.custom_vjp
    def f(qd, ke, v, dec):
        return run_fwd(qd, ke, v, dec)[0]

    def fwd(qd, ke, v, dec):
        o, st = run_fwd(qd, ke, v, dec)
        return o, (qd, ke, v, dec, st)

    def bwd(res, do):
        qd, ke, v, dec, st = res
        tok_k, tok_v, chk, sts = specs(not reverse)
        return tuple(_pallas(
            bwd_body, name=name + "_bwd", grid=(n_hp, nblk), in_specs=[tok_k, tok_k, tok_v, chk, sts, tok_v],
            out_specs=[tok_k, tok_k, tok_v, chk],
            out_shape=[jax.ShapeDtypeStruct(qd.shape, F32), jax.ShapeDtypeStruct(ke.shape, F32),
                       jax.ShapeDtypeStruct(v.shape, F32), jax.ShapeDtypeStruct(dec.shape, F32)],
            scratch_shapes=scratch, compiler_params=_cparams("parallel", "arbitrary"))(qd, ke, v, dec, st, do))

    f.defvjp(fwd, bwd)
    return f(qd, ke, v, dec)


def loss_head(y, target, name="loss"):
    s, d = y.shape
    tr = _row_tile(s, d)

    def body(y_ref, t_ref, o_ref):
        @pl.when(pl.program_id(0) == 0)
        def _():
            o_ref[...] = jnp.zeros_like(o_ref)

        e = y_ref[...] - t_ref[...]
        part = jnp.sum(jnp.sum(e * e, axis=-1, keepdims=True), axis=0, keepdims=True) * (0.5 / d)
        o_ref[...] += jnp.broadcast_to(part, o_ref.shape)

    spec = pl.BlockSpec((tr, d), lambda i: (i, 0))

    def run(y, t):
        out = _pallas(body, name=name, grid=(s // tr,), in_specs=[spec, spec],
                      out_specs=pl.BlockSpec((SUBLANES, LANES), lambda i: (0, 0)),
                      out_shape=jax.ShapeDtypeStruct((SUBLANES, LANES), F32), compiler_params=_cparams("arbitrary"))(y, t)
        return out[0, 0]

    @jax.custom_vjp
    def f(y, t):
        return run(y, t)

    def fwd(y, t):
        return run(y, t), (y, t)

    def bwd(res, g):
        y, t = res
        dy = g * (y - t) * (1.0 / d)
        return dy, -dy

    f.defvjp(fwd, bwd)
    return f(y, target)


def _mesh_pos():
    return lax.axis_index("x"), lax.axis_index("y"), lax.axis_index("c")


def all_gather_shards(shards):
    n = len(shards)

    def body(*refs):
        ins, outs = refs[:n], refs[n:2 * n]
        send_sems, recv_sems, local_sems = refs[2 * n:]
        x, y, c = _mesh_pos()
        me, sibling = (x, y, c), (x, y, 1 - c)
        chips = [(1 - x, y), (x, 1 - y), (1 - x, 1 - y)]

        def slot(t, px, py, pc):
            return outs[t].at[4 * px + 2 * py + pc]

        def copy(t, k, block, to, src=None):
            return pltpu.make_async_remote_copy(
                src_ref=slot(t, *block) if src is None else src, dst_ref=slot(t, *block), send_sem=send_sems.at[t, k],
                recv_sem=recv_sems.at[t, k], device_id=to, device_id_type=pl.DeviceIdType.MESH)

        mine = [pltpu.make_async_copy(ins[t], slot(t, *me), local_sems.at[t]) for t in range(n)]
        for cp in mine:
            cp.start()
        first = []
        for t in range(n):
            first.append(copy(t, 0, me, sibling, src=ins[t]))
            first += [copy(t, 1 + j, me, (*chip, c), src=ins[t]) for j, chip in enumerate(chips)]
        for cp in first:
            cp.start()
        passed = []
        for j, chip in enumerate(chips):
            for t in range(n):
                copy(t, 1 + j, (*chip, c), me).wait_recv()
                cp = copy(t, 4 + j, (*chip, c), sibling)
                cp.start()
                passed.append(cp)
        for t in range(n):
            copy(t, 0, sibling, me).wait_recv()
            for j, chip in enumerate(chips):
                copy(t, 4 + j, (*chip, 1 - c), me).wait_recv()
        for cp in first + passed:
            cp.wait_send()
        for cp in mine:
            cp.wait()

    any_spec = pl.BlockSpec(memory_space=pl.ANY)
    return _pallas(
        body, name="all_gather_weights", out_shape=[jax.ShapeDtypeStruct((N_DEV, *s.shape), s.dtype) for s in shards],
        in_specs=[any_spec] * n, out_specs=[any_spec] * n,
        scratch_shapes=[pltpu.SemaphoreType.DMA((n, 7)), pltpu.SemaphoreType.DMA((n, 7)), pltpu.SemaphoreType.DMA((n,))],
    )(*shards)


def all_to_all_blocks(stacks):
    n = len(stacks)

    def body(*refs):
        ins, outs = refs[:n], refs[n:2 * n]
        send_sems, recv_sems, local_sems = refs[2 * n:]
        x, y, c = _mesh_pos()
        me = 4 * x + 2 * y + c
        mine = [pltpu.make_async_copy(ins[t].at[me], outs[t].at[me], local_sems.at[t]) for t in range(n)]
        for cp in mine:
            cp.start()
        copies = []
        for k in range(1, N_DEV):
            px = 1 - x if k & 4 else x
            py = 1 - y if k & 2 else y
            pc = 1 - c if k & 1 else c
            for t in range(n):
                cp = pltpu.make_async_remote_copy(
                    src_ref=ins[t].at[4 * px + 2 * py + pc], dst_ref=outs[t].at[me], send_sem=send_sems.at[t, k - 1],
                    recv_sem=recv_sems.at[t, k - 1], device_id=(px, py, pc), device_id_type=pl.DeviceIdType.MESH)
                cp.start()
                copies.append(cp)
        for cp in copies:
            cp.wait_recv()
        for cp in copies:
            cp.wait_send()
        for cp in mine:
            cp.wait()

    any_spec = pl.BlockSpec(memory_space=pl.ANY)
    return _pallas(
        body, name="all_to_all_grads", out_shape=[jax.ShapeDtypeStruct(s.shape, s.dtype) for s in stacks],
        in_specs=[any_spec] * n, out_specs=[any_spec] * n,
        scratch_shapes=[pltpu.SemaphoreType.DMA((n, 7)), pltpu.SemaphoreType.DMA((n, 7)), pltpu.SemaphoreType.DMA((n,))],
    )(*stacks)


def all_gather_small(v):
    r, w = v.shape

    def body(x_ref, out_ref, send_sems, recv_sems):
        x, y, c = _mesh_pos()
        me = 4 * x + 2 * y + c
        copies = []
        for k in range(1, N_DEV):
            px = 1 - x if k & 4 else x
            py = 1 - y if k & 2 else y
            pc = 1 - c if k & 1 else c
            cp = pltpu.make_async_remote_copy(
                src_ref=x_ref, dst_ref=out_ref.at[me], send_sem=send_sems.at[k - 1], recv_sem=recv_sems.at[k - 1],
                device_id=(px, py, pc), device_id_type=pl.DeviceIdType.MESH)
            cp.start()
            copies.append(cp)
        out_ref[me] = x_ref[...]
        for cp in copies:
            cp.wait_recv()
        for cp in copies:
            cp.wait_send()

    vmem = pl.BlockSpec(memory_space=pltpu.VMEM)
    return _pallas(
        body, name="all_gather_small", out_shape=jax.ShapeDtypeStruct((N_DEV, r, w), v.dtype), in_specs=[vmem],
        out_specs=vmem, scratch_shapes=[pltpu.SemaphoreType.DMA((7,)), pltpu.SemaphoreType.DMA((7,))],
    )(v)


def adamw_rows(parts, w, m, v, name):
    n, r, lanes = parts.shape
    tr = _tile(r, max(SUBLANES, (256 * 1024) // lanes), SUBLANES)
    c1 = 1.0 / (1.0 - ADAM_B1 ** ADAM_STEP)
    c2 = 1.0 / (1.0 - ADAM_B2 ** ADAM_STEP)

    def body(p_ref, w_ref, m_ref, v_ref, g_ref, d_ref, nm_ref, nv_ref):
        g = p_ref[0].astype(F32)
        for j in range(1, n):
            g = g + p_ref[j].astype(F32)
        nm = ADAM_B1 * m_ref[...] + (1.0 - ADAM_B1) * g
        nv = ADAM_B2 * v_ref[...] + (1.0 - ADAM_B2) * (g * g)
        g_ref[...] = g
        nm_ref[...] = nm
        nv_ref[...] = nv
        d_ref[...] = -ADAM_LR * ((nm * c1) / (jnp.sqrt(nv * c2) + ADAM_EPS) + ADAM_WD * w_ref[...])

    row = pl.BlockSpec((tr, lanes), lambda i: (i, 0))
    out = jax.ShapeDtypeStruct((r, lanes), F32)
    return _pallas(body, name=name, grid=(r // tr,), in_specs=[pl.BlockSpec((n, tr, lanes), lambda i: (0, i, 0)), row, row, row],
                   out_specs=[row, row, row, row], out_shape=[out, out, out, out], compiler_params=_cparams("parallel"))(
        parts, w, m, v)


def _padded(n):
    return -(-n // PACK_QUANTUM) * PACK_QUANTUM


def _pack(pieces, total_rows=None):
    flat = []
    for p in pieces:
        p = p.reshape(-1).astype(F32)
        flat.append(jnp.pad(p, (0, _padded(p.size) - p.size)))
    out = jnp.concatenate(flat).reshape(-1, LANES)
    if total_rows is not None and out.shape[0] != total_rows:
        out = jnp.pad(out, ((0, total_rows - out.shape[0]), (0, 0)))
    return out


def _pack_rows(sizes):
    rows = sum(_padded(n) for n in sizes) // LANES
    return -(-rows // PACK_ROW_TILE) * PACK_ROW_TILE


def _unpack(rows, shapes):
    lead = rows.shape[:-2]
    flat = rows.reshape(*lead, -1)
    out, off = [], 0
    for shp in shapes:
        n = int(np.prod(shp))
        out.append(flat[..., off:off + n].reshape(*lead, *shp))
        off += _padded(n)
    return out


def _shards_to_full(stacked, axis):
    moved = jnp.moveaxis(stacked, 0, axis)
    shp = list(stacked.shape[1:])
    shp[axis] *= N_DEV
    return moved.reshape(shp)


def _full_to_shards(full, axis):
    shp = list(full.shape)
    shp[axis:axis + 1] = [N_DEV, shp[axis] // N_DEV]
    return jnp.moveaxis(full.reshape(shp), axis, 0)


def _heads(t, n, d, dtype=ACT_DTYPE):
    return jnp.transpose(t.reshape(t.shape[0], n, d), (1, 0, 2)).astype(dtype)


def _unheads(t):
    return jnp.transpose(t, (1, 0, 2)).reshape(t.shape[1], -1)


def _rope_tables(s):
    half = A_ROPE // 2
    inv = ROPE_THETA ** (-jnp.arange(half, dtype=F32) / half)
    ang = jnp.arange(s, dtype=jnp.int32).astype(F32)[:, None] * inv[None, :]
    return jnp.cos(ang), jnp.sin(ang)


def _rope(t, cos, sin):
    half = A_ROPE // 2
    t1, t2 = t[..., :half], t[..., half:]
    c, sn = cos[:, None, :], sin[:, None, :]
    return jnp.concatenate([t1 * c - t2 * sn, t1 * sn + t2 * c], axis=-1)


def _t5_bucket(rel):
    nb = REL_BUCKETS // 2
    max_exact = nb // 2
    ret = (rel > 0).astype(jnp.int32) * nb
    n = jnp.abs(rel)
    large = max_exact + (jnp.log(jnp.maximum(n, 1).astype(F32) / max_exact)
                         / math.log(REL_MAX_DIST / max_exact) * (nb - max_exact)).astype(jnp.int32)
    large = jnp.minimum(large, nb - 1)
    return ret + jnp.where(n < max_exact, n, large)


def _window_bias(rel_bias):
    span = 3 * C_BLOCK
    rel = jnp.arange(span)[None, :] - C_BLOCK - jnp.arange(C_BLOCK)[:, None]
    onehot = (_t5_bucket(rel)[..., None] == jnp.arange(REL_BUCKETS)).astype(F32)
    bias = jnp.einsum("qkb,bh->hqk", onehot, rel_bias.astype(F32), precision=lax.Precision.HIGHEST)
    bias = jnp.where((jnp.abs(rel) <= C_WINDOW)[None], bias, NEG)
    return jnp.transpose(bias.reshape(C_HEADS, C_BLOCK, 3, C_BLOCK), (0, 2, 1, 3))


def _mla(cq, ckv, kr, gq, gkv, wuq, wukv, cos, sin):
    s = cq.shape[0]
    q = linear(rmsnorm(cq, gq, ACT_DTYPE, "rms_cq"), wuq, name="a_wuq").reshape(s, A_HEADS, A_NOPE + A_ROPE)
    q = jnp.concatenate([q[..., :A_NOPE], _rope(q[..., A_NOPE:], cos, sin)], axis=-1)
    kv = linear(rmsnorm(ckv, gkv, ACT_DTYPE, "rms_ckv"), wukv, name="a_wukv").reshape(s, A_HEADS, A_NOPE + A_V)
    k_rope = jnp.broadcast_to(_rope(kr[:, None, :], cos, sin), (s, A_HEADS, A_ROPE))
    k = jnp.concatenate([kv[..., :A_NOPE], k_rope], axis=-1)
    v = kv[..., A_NOPE:]
    return mla_attention(q, k, v, (A_NOPE + A_ROPE) ** -0.5)


def _hgrn2(q, f_fwd, f_bwd, i, g, lb_fwd, lb_bwd, g_out):
    s = q.shape[0]
    n_hp = B_HEADS // HG_GROUP
    pairs = lambda t: jnp.transpose(t.reshape(s, n_hp, -1), (1, 0, 2))
    qp, vp = pairs(q), pairs(i)
    o = None
    for z, lb, rev, tag in ((f_fwd, lb_fwd, False, "hgf"), (f_bwd, lb_bwd, True, "hgb")):
        qd, ki, ke, dec = hgrn_prep(qp, pairs(z), lb.astype(F32).reshape(n_hp, 1, -1), rev, tag + "_prep")
        part = hgrn_intra(qd, ki, vp, rev, tag + "_intra") + hgrn_inter(qd, ke, vp, dec, rev, tag + "_inter")
        o = part if o is None else o + part
    return hgrn_out(jnp.transpose(o, (1, 0, 2)).reshape(s, B_HEADS * B_DV), g, g_out)


def _cross(x, h, mem_n, wq, wkv, wo):
    q = linear(h, wq, name="x_wq").reshape(h.shape[0], X_HEADS, X_DH)
    kv = linear(mem_n, wkv, name="x_wkv").reshape(mem_n.shape[0], 2, X_HEADS, X_DH)
    o_t = mla_attention(q, kv[:, 0], kv[:, 1], X_DH ** -0.5, name="cross")
    return linear_res(x, o_t, wo, name="x_wo", a_transposed=True)


def _pad_w_in(w):
    cut = A_Q_RANK + A_KV_RANK + A_ROPE
    return jnp.concatenate([w[:, :cut], jnp.zeros((w.shape[0], KR_PAD), w.dtype), w[:, cut:]], axis=1)


def _model_loss(p, x, mem, target):
    s = x.shape[0]
    cos, sin = _rope_tables(s)
    sm = jax.nn.softmax(p["b_lb"].astype(F32), axis=1)
    lower_bounds = jnp.cumsum(sm, axis=1) - sm[:, :1]
    bias = _window_bias(p["rel_bias"])
    for l in range(DEPTH):
        h, x = rmsnorm_keep(x, p["g_mix"][l], ACT_DTYPE, "rms_mix")
        z = linear(h, _pad_w_in(p["w_in"][l]), name="w_in")
        parts, start = [], 0
        for width in IN_SPLITS_PADDED:
            parts.append(z[:, start:start + width])
            start += width
        a_cq, a_ckv, a_kr, b_q, b_ff, b_fb, b_i, b_g, c_q, c_k, c_v, gate_a, gate_b, gate_c = parts
        y_a = _mla(a_cq, a_ckv, a_kr[:, :A_ROPE], p["a_gq"][l], p["a_gkv"][l], p["a_wuq"][l], p["a_wukv"][l], cos, sin)
        y_b = _hgrn2(b_q, b_ff, b_fb, b_i, b_g, lower_bounds[0, l], lower_bounds[1, l], p["b_gout"][l])
        y_c = _unheads(window_attention(_heads(c_q, C_HEADS, C_DH, F32), _heads(c_k, C_KV_HEADS, C_DH, F32),
                                        _heads(c_v, C_KV_HEADS, C_DH, F32), bias, p["c_sink"][l]))
        x = gated_merge_out(x, y_a, y_b, y_c, gate_a, gate_b, gate_c, p["w_br_a"][l], p["w_br_b"][l], p["w_br_c"][l],
                            p["w_out"][l])
        h, x = rmsnorm_keep(x, p["g_x"][l], ACT_DTYPE, "rms_x")
        x = _cross(x, h, rmsnorm(mem, p["g_mem"][l], ACT_DTYPE, "rms_mem"), p["x_wq"][l], p["x_wkv"][l], p["x_wo"][l])
        h, x = rmsnorm_keep(x, p["g_ffn"][l], ACT_DTYPE, "rms_ffn")
        x = swiglu_ffn(x, h, p["f_w1"][l], p["f_w3"][l], p["f_w2"][l])
    y = rmsnorm(x, p["g_final"], F32, "rms_final")
    return loss_head(y, target)


def kernel(x, mem, w_in, g_mix, a_gq, a_gkv, a_wuq, a_wukv, b_lb, b_gout, c_sink, rel_bias, w_br_a, w_br_b, w_br_c, w_out, g_x, g_mem, x_wq, x_wkv, x_wo, g_ffn, f_w1, f_w3, f_w2, g_final, loss_target, m_w_in, m_g_mix, m_a_gq, m_a_gkv, m_a_wuq, m_a_wukv, m_b_lb, m_b_gout, m_c_sink, m_rel_bias, m_w_br_a, m_w_br_b, m_w_br_c, m_w_out, m_g_x, m_g_mem, m_x_wq, m_x_wkv, m_x_wo, m_g_ffn, m_f_w1, m_f_w3, m_f_w2, m_g_final, v_w_in, v_g_mix, v_a_gq, v_a_gkv, v_a_wuq, v_a_wukv, v_b_lb, v_b_gout, v_c_sink, v_rel_bias, v_w_br_a, v_w_br_b, v_w_br_c, v_w_out, v_g_x, v_g_mem, v_x_wq, v_x_wkv, v_x_wo, v_g_ffn, v_f_w1, v_f_w3, v_f_w2, v_g_final):
    given = dict(locals())
    w = {n: given[n] for n in WEIGHT_ORDER}
    m = {n: given["m_" + n] for n in WEIGHT_ORDER}
    v = {n: given["v_" + n] for n in WEIGHT_ORDER}
    sh_names = [n for n, _ in SHARDED]
    rep_shapes = [w[n].shape for n in REPLICATED] + [(1,)]
    rep_rows = _pack_rows([int(np.prod(s)) for s in rep_shapes])

    wire = [w[n] if n in ELEMENTWISE_SHARDED else w[n].astype(MXU_DTYPE) for n in sh_names]
    gathered = all_gather_shards(wire)
    full = {n: _shards_to_full(t, ax).astype(F32) for (n, ax), t in zip(SHARDED, gathered)}
    full.update({n: w[n] for n in REPLICATED})

    loss, (grad_full, grad_x) = jax.value_and_grad(_model_loss, argnums=(0, 1))(full, x[0], mem[0], loss_target[0])

    received = all_to_all_blocks([_full_to_shards(grad_full[n], ax).astype(GRAD_WIRE_DTYPE) for n, ax in SHARDED])
    g_sh, d_sh, nm_sh, nv_sh = {}, {}, {}, {}
    for n, got in zip(sh_names, received):
        shp = w[n].shape
        rows = lambda t: t.reshape(-1, shp[-1])
        outs = adamw_rows(got.reshape(N_DEV, -1, shp[-1]), rows(w[n]), rows(m[n]), rows(v[n]), "adamw_" + n)
        g_sh[n], d_sh[n], nm_sh[n], nv_sh[n] = [o.reshape(shp) for o in outs]

    mine = _pack([grad_full[n] for n in REPLICATED] + [loss.reshape(1)], rep_rows)
    everyone = all_gather_small(mine)
    rep_w = [w[n] for n in REPLICATED] + [jnp.zeros((1,), F32)]
    outs = adamw_rows(everyone, _pack(rep_w, rep_rows), _pack([m[n] for n in REPLICATED] + [jnp.zeros((1,), F32)], rep_rows),
                      _pack([v[n] for n in REPLICATED] + [jnp.ones((1,), F32)], rep_rows), "adamw_replicated")
    rep_names = list(REPLICATED) + ["loss"]
    g_rp, d_rp, nm_rp, nv_rp = [dict(zip(rep_names, _unpack(o, rep_shapes))) for o in outs]

    def pick(sharded, replicated, n):
        return sharded[n] if n in sharded else replicated[n]

    return (g_rp["loss"].reshape(()), grad_x[None],
            *[pick(g_sh, g_rp, n) for n in WEIGHT_ORDER], *[pick(d_sh, d_rp, n) for n in WEIGHT_ORDER],
            *[pick(nm_sh, nm_rp, n) for n in WEIGHT_ORDER], *[pick(nv_sh, nv_rp, n) for n in WEIGHT_ORDER])
```
